```python
import jax, jax.numpy as jnp
from jax import lax
import numpy as np

D_MODEL = 1024
BATCH = 1
SEQ = 16384
DEPTH = 2

HEAD_DIM = 64
NSA_HEADS = 8
NSA_GROUPS = 2
CMP_LEN = 32
CMP_STRIDE = 16
SLC_BLOCK = 64
SLC_TOPK = 16
WINDOW = 512
FOX_HEADS = 8
MLA_HEADS = 16
Q_LORA = 384
KV_LORA = 256
QK_NOPE = 64
QK_ROPE = 32
V_HEAD = 64
N_EXPERTS = 64
TOP_K = 8
N_GROUPS = 8
TOPK_GROUPS = 4
EXPERT_FF = 256
SHARED_FF = 256
ROUTED_SCALE = 2.5
ROW_BLOCK = 128
Q_BLOCK = 128
ROPE_THETA = 10000.0
EPS = 1e-6
NEG = -1e30
BIG = 1e6
N_EVEN = (DEPTH + 1) // 2
N_ODD = DEPTH // 2
NSA_Q_COLS = NSA_HEADS * HEAD_DIM
NSA_KV_COLS = 6 * NSA_GROUPS * HEAD_DIM
NSA_GATE_COLS = 3 * NSA_HEADS
FOX_QKV_COLS = 3 * FOX_HEADS * HEAD_DIM
FOX_F_COLS = FOX_HEADS
HY_IN = NSA_Q_COLS + NSA_KV_COLS + NSA_GATE_COLS + FOX_QKV_COLS + FOX_F_COLS
HY_OUT = (NSA_HEADS + FOX_HEADS) * HEAD_DIM
MLA_IN = Q_LORA + KV_LORA + QK_ROPE
MLA_OUT = MLA_HEADS * V_HEAD

kernel_name = "hybrid_nsa_fox_mla_moe_adaln"

F32 = jnp.float32


def rms_norm(x, g):
    x32 = x.astype(F32)
    y = x32 * lax.rsqrt(jnp.mean(x32 * x32, axis=-1, keepdims=True) + EPS)
    return (y * g.astype(F32)).astype(x.dtype)


def rope_cos_sin(positions, dim):
    inv = ROPE_THETA ** (-jnp.arange(0, dim, 2, dtype=F32) / dim)
    ang = positions.astype(F32)[..., None] * inv
    return jnp.cos(ang)[:, :, None, :], jnp.sin(ang)[:, :, None, :]


def apply_rope(x, cos, sin):
    x32 = x.astype(F32)
    x1, x2 = jnp.split(x32, 2, axis=-1)
    return jnp.concatenate([x1 * cos - x2 * sin, x2 * cos + x1 * sin], axis=-1).astype(x.dtype)


def masked_softmax(s, mask):
    s = jnp.where(mask, s, NEG)
    m = jnp.max(s, axis=-1, keepdims=True)
    e = jnp.where(mask, jnp.exp(s - m), 0.0)
    return e / jnp.maximum(jnp.sum(e, axis=-1, keepdims=True), 1e-30)


def causal_attention(q, k, v, scale, log_cum=None):
    B, S, H, _ = q.shape
    dv = v.shape[-1]
    nb = S // Q_BLOCK

    def q_block(i):
        qs = i * Q_BLOCK
        qb = lax.dynamic_slice_in_dim(q, qs, Q_BLOCK, 1).astype(F32)
        tq = qs + jnp.arange(Q_BLOCK)
        cq = None if log_cum is None else lax.dynamic_slice_in_dim(log_cum, qs, Q_BLOCK, 1).transpose(0, 2, 1)

        def kv_step(j, carry):
            m, l, acc = carry
            ks = j * Q_BLOCK
            kb = lax.dynamic_slice_in_dim(k, ks, Q_BLOCK, 1).astype(F32)
            vb = lax.dynamic_slice_in_dim(v, ks, Q_BLOCK, 1).astype(F32)
            s = jnp.einsum('bqhd,bkhd->bhqk', qb, kb) * scale
            if log_cum is not None:
                ck = lax.dynamic_slice_in_dim(log_cum, ks, Q_BLOCK, 1).transpose(0, 2, 1)
                s = s + cq[..., :, None] - ck[..., None, :]
            tk = ks + jnp.arange(Q_BLOCK)
            s = jnp.where((tk[None, :] <= tq[:, None])[None, None], s, NEG)
            m_new = jnp.maximum(m, jnp.max(s, axis=-1))
            a = jnp.exp(m - m_new)
            p = jnp.exp(s - m_new[..., None])
            l = l * a + jnp.sum(p, axis=-1)
            acc = acc * a[..., None] + jnp.einsum('bhqk,bkhd->bhqd', p, vb)
            return m_new, l, acc

        init = (jnp.full((B, H, Q_BLOCK), NEG, F32), jnp.zeros((B, H, Q_BLOCK), F32),
                jnp.zeros((B, H, Q_BLOCK, dv), F32))
        m, l, acc = lax.fori_loop(0, i + 1, kv_step, init)
        return (acc / l[..., None]).transpose(0, 2, 1, 3)

    out = lax.map(q_block, jnp.arange(nb))
    return out.transpose(1, 0, 2, 3, 4).reshape(B, S, H, dv).astype(v.dtype)


def nsa_attention(q, k_cmp_tok, v_cmp_tok, k_slc, v_slc, k_win, v_win, gates, cmp_pe, w_cmp, k_norm):
    B, S, H, dh = q.shape
    G = NSA_GROUPS
    hpg = H // G
    scale = dh ** -0.5
    r = CMP_LEN // CMP_STRIDE
    n_chunks = S // CMP_STRIDE
    n_cmp = n_chunks - r + 1

    def compress(t, pe, w):
        ch = t.reshape(B, n_chunks, CMP_STRIDE, G, dh)
        blocks = jnp.concatenate([ch[:, j:j + n_cmp] for j in range(r)], axis=2)
        blocks = blocks + pe[None, None, :, None, :].astype(t.dtype)
        return jnp.einsum('bnlgd,lde->bnge', blocks, w.reshape(CMP_LEN, dh, dh))

    k_cmp = rms_norm(compress(k_cmp_tok, cmp_pe[0], w_cmp[0]), k_norm).astype(F32)
    v_cmp = compress(v_cmp_tok, cmp_pe[1], w_cmp[1]).astype(F32)
    cmp_start = jnp.arange(n_cmp) * CMP_STRIDE
    cmp_end = cmp_start + CMP_LEN - 1
    n_slc = S // SLC_BLOCK
    n_sel = min(SLC_TOPK, n_slc)
    slc_start = jnp.arange(n_slc) * SLC_BLOCK
    overlap = jnp.clip(jnp.minimum(cmp_start[:, None] + CMP_LEN, slc_start[None, :] + SLC_BLOCK)
                       - jnp.maximum(cmp_start[:, None], slc_start[None, :]), 0, None).astype(F32) / CMP_STRIDE
    k_slc_b = k_slc.reshape(B, n_slc, SLC_BLOCK, G, dh).transpose(0, 3, 1, 2, 4)
    v_slc_b = v_slc.reshape(B, n_slc, SLC_BLOCK, G, dh).transpose(0, 3, 1, 2, 4)
    k_win_p = jnp.pad(k_win, ((0, 0), (WINDOW, 0), (0, 0), (0, 0)))
    v_win_p = jnp.pad(v_win, ((0, 0), (WINDOW, 0), (0, 0), (0, 0)))
    bi = jnp.arange(B)[:, None, None, None]
    gi = jnp.arange(G)[None, None, :, None]
    jj = jnp.arange(n_slc)

    def block(i):
        qs = i * Q_BLOCK
        tq = qs + jnp.arange(Q_BLOCK)
        qb = lax.dynamic_slice_in_dim(q, qs, Q_BLOCK, 1).reshape(B, Q_BLOCK, G, hpg, dh).astype(F32)
        s = jnp.einsum('bqghd,bngd->bqghn', qb, k_cmp) * scale
        p_cmp = masked_softmax(s, (cmp_end[None, :] <= tq[:, None])[None, :, None, None, :])
        o_cmp = jnp.einsum('bqghn,bngd->bqghd', p_cmp, v_cmp)
        imp = jnp.einsum('bqghn,nj->bqgj', p_cmp, overlap)
        cur = (tq // SLC_BLOCK)[:, None]
        forced = (jj[None, :] == 0) | (jj[None, :] == cur) | (jj[None, :] == cur - 1)
        imp = jnp.where(forced[None, :, None, :], imp + BIG, imp)
        imp = jnp.where((jj[None, :] * SLC_BLOCK <= tq[:, None])[None, :, None, :], imp, NEG)
        _, idx = lax.top_k(imp, n_sel)
        ksel = k_slc_b[bi, gi, idx].reshape(B, Q_BLOCK, G, n_sel * SLC_BLOCK, dh).astype(F32)
        vsel = v_slc_b[bi, gi, idx].reshape(B, Q_BLOCK, G, n_sel * SLC_BLOCK, dh).astype(F32)
        kpos = (idx[..., None] * SLC_BLOCK + jnp.arange(SLC_BLOCK)).reshape(B, Q_BLOCK, G, n_sel * SLC_BLOCK)
        s = jnp.einsum('bqghd,bqgkd->bqghk', qb, ksel) * scale
        p = masked_softmax(s, (kpos <= tq[None, :, None, None])[:, :, :, None, :])
        o_slc = jnp.einsum('bqghk,bqgkd->bqghd', p, vsel)
        kw = lax.dynamic_slice_in_dim(k_win_p, qs, WINDOW + Q_BLOCK, 1).astype(F32)
        vw = lax.dynamic_slice_in_dim(v_win_p, qs, WINDOW + Q_BLOCK, 1).astype(F32)
        wpos = qs - WINDOW + jnp.arange(WINDOW + Q_BLOCK)
        dist = tq[:, None] - wpos[None, :]
        wmask = (dist >= 0) & (dist < WINDOW) & (wpos[None, :] >= 0)
        s = jnp.einsum('bqghd,bkgd->bqghk', qb, kw) * scale
        p = masked_softmax(s, wmask[None, :, None, None, :])
        o_win = jnp.einsum('bqghk,bkgd->bqghd', p, vw)
        return jnp.stack([o_cmp, o_slc, o_win], axis=-2)

    out = lax.map(block, jnp.arange(S // Q_BLOCK))
    out = out.transpose(1, 0, 2, 3, 4, 5, 6).reshape(B, S, H, 3, dh)
    return jnp.einsum('bshr,bshrd->bshd', gates.astype(F32), out).astype(q.dtype)


def hybrid_mixer(h, cos, sin, w_in, fox_f_bias, nsa_q_norm, nsa_k_norm, nsa_cmp_pe, nsa_w_cmp,
                 fox_q_norm, fox_k_norm, w_out):
    B, S, _ = h.shape
    proj = h @ w_in
    cuts = list(np.cumsum([NSA_Q_COLS, NSA_KV_COLS, NSA_GATE_COLS, FOX_QKV_COLS]))
    nq, nkv, ng, fqkv, ff = jnp.split(proj, cuts, axis=-1)
    q = apply_rope(rms_norm(nq.reshape(B, S, NSA_HEADS, HEAD_DIM), nsa_q_norm), cos, sin)
    kv = nkv.reshape(B, S, 6, NSA_GROUPS, HEAD_DIM)
    k_cmp = apply_rope(kv[:, :, 0], cos, sin)
    k_slc = apply_rope(rms_norm(kv[:, :, 2], nsa_k_norm), cos, sin)
    k_win = apply_rope(rms_norm(kv[:, :, 4], nsa_k_norm), cos, sin)
    gates = jax.nn.sigmoid(ng.reshape(B, S, NSA_HEADS, 3))
    o_a = nsa_attention(q, k_cmp, kv[:, :, 1], k_slc, kv[:, :, 3], k_win, kv[:, :, 5], gates,
                        nsa_cmp_pe, nsa_w_cmp, nsa_k_norm)
    fqkv = fqkv.reshape(B, S, 3, FOX_HEADS, HEAD_DIM)
    fq = rms_norm(fqkv[:, :, 0], fox_q_norm)
    fk = rms_norm(fqkv[:, :, 1], fox_k_norm)
    log_f = jax.nn.log_sigmoid(ff.astype(F32) + fox_f_bias.astype(F32))
    o_b = causal_attention(fq, fk, fqkv[:, :, 2], HEAD_DIM ** -0.5, jnp.cumsum(log_f, axis=1))
    o = jnp.concatenate([o_a.reshape(B, S, -1), o_b.reshape(B, S, -1)], axis=-1)
    return o @ w_out


def mla_mixer(h, cos, sin, w_in, q_a_norm, kv_a_norm, w_uq, w_ukv, qn_norm, kn_norm, qr_norm, kr_norm, w_out):
    B, S, _ = h.shape
    proj = h @ w_in
    cq, ckv, kr = jnp.split(proj, [Q_LORA, Q_LORA + KV_LORA], axis=-1)
    q = (rms_norm(cq, q_a_norm) @ w_uq).reshape(B, S, MLA_HEADS, QK_NOPE + QK_ROPE)
    kv = (rms_norm(ckv, kv_a_norm) @ w_ukv).reshape(B, S, MLA_HEADS, QK_NOPE + V_HEAD)
    q_nope = rms_norm(q[..., :QK_NOPE], qn_norm)
    q_rope = apply_rope(rms_norm(q[..., QK_NOPE:], qr_norm), cos, sin)
    k_nope = rms_norm(kv[..., :QK_NOPE], kn_norm)
    v = kv[..., QK_NOPE:]
    k_rope = apply_rope(rms_norm(kr.reshape(B, S, 1, QK_ROPE), kr_norm), cos, sin)
    qf = jnp.concatenate([q_nope, q_rope], axis=-1)
    kf = jnp.concatenate([k_nope, jnp.broadcast_to(k_rope, (B, S, MLA_HEADS, QK_ROPE))], axis=-1)
    o = causal_attention(qf, kf, v, (QK_NOPE + QK_ROPE) ** -0.5)
    return o.reshape(B, S, -1) @ w_out


def moe_ffn(h, w_router, router_bias, w_gate, w_up, w_down, ws_gate, ws_up, ws_down):
    B, S, D = h.shape
    N = B * S
    NK = N * TOP_K
    xt = h.reshape(N, D)
    scores = jax.nn.sigmoid((xt @ w_router).astype(F32))
    sel = scores + router_bias.astype(F32)
    grp = sel.reshape(N, N_GROUPS, N_EXPERTS // N_GROUPS)
    grp_score = jnp.sum(lax.top_k(grp, 2)[0], axis=-1)
    _, gidx = lax.top_k(grp_score, TOPK_GROUPS)
    gmask = jnp.any(gidx[..., None] == jnp.arange(N_GROUPS), axis=-2)
    emask = jnp.repeat(gmask, N_EXPERTS // N_GROUPS, axis=-1)
    _, eidx = lax.top_k(jnp.where(emask, sel, NEG), TOP_K)
    w = jnp.take_along_axis(scores, eidx, axis=-1)
    w = w / jnp.sum(w, axis=-1, keepdims=True) * ROUTED_SCALE
    flat_e = eidx.reshape(-1)
    flat_t = jnp.repeat(jnp.arange(N, dtype=jnp.int32), TOP_K)
    flat_w = w.reshape(-1)
    order = jnp.argsort(flat_e)
    se, st, sw = flat_e[order], flat_t[order], flat_w[order]
    counts = jnp.bincount(flat_e, length=N_EXPERTS)
    start = jnp.cumsum(counts) - counts
    padded = (counts + ROW_BLOCK - 1) // ROW_BLOCK * ROW_BLOCK
    pend = jnp.cumsum(padded)
    pstart = pend - padded
    dest = pstart[se] + jnp.arange(NK) - start[se]
    n_blocks = (NK + N_EXPERTS * (ROW_BLOCK - 1) + ROW_BLOCK - 1) // ROW_BLOCK
    rows = n_blocks * ROW_BLOCK
    row_tok = jnp.full((rows,), N, jnp.int32).at[dest].set(st)
    row_w = jnp.zeros((rows,), F32).at[dest].set(sw)
    blk_e = jnp.minimum(jnp.searchsorted(pend, jnp.arange(n_blocks) * ROW_BLOCK, side='right'), N_EXPERTS - 1)
    x_pad = jnp.concatenate([xt, jnp.zeros((1, D), xt.dtype)], axis=0)

    def expert_block(args):
        tok, wr, e = args
        xb = x_pad[tok]
        a = jax.nn.silu(xb @ w_gate[e]) * (xb @ w_up[e])
        return (a @ w_down[e]) * wr[:, None].astype(xb.dtype)

    y_rows = lax.map(expert_block, (row_tok.reshape(n_blocks, ROW_BLOCK), row_w.reshape(n_blocks, ROW_BLOCK), blk_e))
    routed = jax.ops.segment_sum(y_rows.reshape(rows, D), row_tok, num_segments=N + 1)[:N]
    shared = (jax.nn.silu(xt @ ws_gate) * (xt @ ws_up)) @ ws_down
    return (routed + shared).reshape(B, S, D)


def setup_inputs(seed: int = 0) -> dict:
    key = jax.random.key(seed)
    ks = iter(jax.random.split(key, 64))

    def nrm(shape, scale):
        return scale * jax.random.normal(next(ks), shape, F32)

    def gain(shape):
        return 1.0 + nrm(shape, 0.1)

    D = D_MODEL
    x = nrm((BATCH, SEQ, D), 1.0)
    c = nrm((BATCH, D), 1.0)
    positions = (jax.random.randint(next(ks), (BATCH, 1), 0, 4096, dtype=jnp.int32)
                 + jnp.arange(SEQ, dtype=jnp.int32)[None, :])
    return {
        "x": x, "c": c, "positions": positions,
        "norm_attn": gain((DEPTH, D)), "norm_ffn": gain((DEPTH, D)),
        "w_ada": nrm((DEPTH, D, 6 * D), 0.25 * D ** -0.5), "b_ada": nrm((DEPTH, 6 * D), 0.01),
        "hy_w_in": nrm((N_EVEN, D, HY_IN), D ** -0.5),
        "fox_f_bias": 2.0 + nrm((N_EVEN, FOX_HEADS), 0.5),
        "nsa_q_norm": gain((N_EVEN, HEAD_DIM)), "nsa_k_norm": gain((N_EVEN, HEAD_DIM)),
        "nsa_cmp_pe": nrm((N_EVEN, 2, CMP_LEN, HEAD_DIM), 0.5),
        "nsa_w_cmp": nrm((N_EVEN, 2, CMP_LEN * HEAD_DIM, HEAD_DIM), (CMP_LEN * HEAD_DIM) ** -0.5),
        "fox_q_norm": gain((N_EVEN, HEAD_DIM)), "fox_k_norm": gain((N_EVEN, HEAD_DIM)),
        "hy_w_out": nrm((N_EVEN, HY_OUT, D), HY_OUT ** -0.5),
        "mla_w_in": nrm((N_ODD, D, MLA_IN), D ** -0.5),
        "mla_q_a_norm": gain((N_ODD, Q_LORA)), "mla_kv_a_norm": gain((N_ODD, KV_LORA)),
        "mla_w_uq": nrm((N_ODD, Q_LORA, MLA_HEADS * (QK_NOPE + QK_ROPE)), Q_LORA ** -0.5),
        "mla_w_ukv": nrm((N_ODD, KV_LORA, MLA_HEADS * (QK_NOPE + V_HEAD)), KV_LORA ** -0.5),
        "mla_qn_norm": gain((N_ODD, QK_NOPE)), "mla_kn_norm": gain((N_ODD, QK_NOPE)),
        "mla_qr_norm": gain((N_ODD, QK_ROPE)), "mla_kr_norm": gain((N_ODD, QK_ROPE)),
        "mla_w_out": nrm((N_ODD, MLA_OUT, D), MLA_OUT ** -0.5),
        "moe_w_router": nrm((DEPTH, D, N_EXPERTS), D ** -0.5),
        "moe_router_bias": nrm((DEPTH, N_EXPERTS), 0.01),
        "moe_w_gate": nrm((DEPTH, N_EXPERTS, D, EXPERT_FF), D ** -0.5),
        "moe_w_up": nrm((DEPTH, N_EXPERTS, D, EXPERT_FF), D ** -0.5),
        "moe_w_down": nrm((DEPTH, N_EXPERTS, EXPERT_FF, D), EXPERT_FF ** -0.5),
        "moe_ws_gate": nrm((DEPTH, D, SHARED_FF), D ** -0.5),
        "moe_ws_up": nrm((DEPTH, D, SHARED_FF), D ** -0.5),
        "moe_ws_down": nrm((DEPTH, SHARED_FF, D), SHARED_FF ** -0.5),
    }


def reference(x, c, positions, norm_attn, norm_ffn, w_ada, b_ada,
              hy_w_in, fox_f_bias, nsa_q_norm, nsa_k_norm, nsa_cmp_pe, nsa_w_cmp, fox_q_norm, fox_k_norm, hy_w_out,
              mla_w_in, mla_q_a_norm, mla_kv_a_norm, mla_w_uq, mla_w_ukv, mla_qn_norm, mla_kn_norm,
              mla_qr_norm, mla_kr_norm, mla_w_out,
              moe_w_router, moe_router_bias, moe_w_gate, moe_w_up, moe_w_down, moe_ws_gate, moe_ws_up, moe_ws_down):
    cond = jax.nn.silu(c)
    cos_a, sin_a = rope_cos_sin(positions, HEAD_DIM)
    cos_c, sin_c = rope_cos_sin(positions, QK_ROPE)
    for layer in range(DEPTH):
        mod = cond @ w_ada[layer] + b_ada[layer]
        sh1, sc1, g1, sh2, sc2, g2 = jnp.split(mod[:, None, :], 6, axis=-1)
        h = rms_norm(x, norm_attn[layer]) * (1.0 + sc1) + sh1
        i = layer // 2
        if layer % 2 == 0:
            y = hybrid_mixer(h, cos_a, sin_a, hy_w_in[i], fox_f_bias[i], nsa_q_norm[i], nsa_k_norm[i],
                             nsa_cmp_pe[i], nsa_w_cmp[i], fox_q_norm[i], fox_k_norm[i], hy_w_out[i])
        else:
            y = mla_mixer(h, cos_c, sin_c, mla_w_in[i], mla_q_a_norm[i], mla_kv_a_norm[i], mla_w_uq[i],
                          mla_w_ukv[i], mla_qn_norm[i], mla_kn_norm[i], mla_qr_norm[i], mla_kr_norm[i], mla_w_out[i])
        x = x + g1 * y
        h = rms_norm(x, norm_ffn[layer]) * (1.0 + sc2) + sh2
        x = x + g2 * moe_ffn(h, moe_w_router[layer], moe_router_bias[layer], moe_w_gate[layer], moe_w_up[layer],
                             moe_w_down[layer], moe_ws_gate[layer], moe_ws_up[layer], moe_ws_down[layer])
    return x
```

```python
import functools

import numpy as np
import jax
import jax.numpy as jnp
from jax import lax
from jax.experimental import pallas as pl
from jax.experimental.pallas import tpu as pltpu

F32 = jnp.float32
BF16 = jnp.bfloat16
HIGHEST = lax.Precision.HIGHEST

LANE = 128
VMEM_LIMIT_BYTES = 56 * 1024 * 1024

HEAD_DIM = 64
NSA_HEADS = 8
NSA_GROUPS = 2
NSA_HPG = NSA_HEADS // NSA_GROUPS
CMP_LEN = 32
CMP_STRIDE = 16
SLC_BLOCK = 64
SLC_TOPK = 16
WINDOW = 512
FOX_HEADS = 8
MLA_HEADS = 16
Q_LORA = 384
KV_LORA = 256
QK_NOPE = 64
QK_ROPE = 32
V_HEAD = 64
N_EXPERTS = 64
TOP_K = 8
N_GROUPS = 8
TOPK_GROUPS = 4
EXPERT_FF = 256
ROUTED_SCALE = 2.5
ROPE_THETA = 10000.0
EPS = 1e-6
NEG = -1e30
BIG = 1e6

ROW_TILE = 512
PREP_TILE = 256
Q_TILE_NSA = 128
KV_TILE = 512
Q_TILE_FLASH = 512
SUPER_BLOCKS = 64
MOE_CHUNK = 128

HY_Q0 = 0
HY_KV0 = HY_Q0 + NSA_HEADS
HY_F0 = HY_KV0 + 6 * NSA_GROUPS
HY_G0 = HY_F0 + 3 * FOX_HEADS
HY_FF = HY_G0 + NSA_GROUPS
HY_BLOCKS = HY_FF + 2
HY_COL_TILE = 12 * LANE
assert (HY_BLOCKS * LANE) % HY_COL_TILE == 0


def _cparams(*sem):
    return pltpu.CompilerParams(dimension_semantics=sem, vmem_limit_bytes=VMEM_LIMIT_BYTES)


def _lane_iota(shape):
    return lax.broadcasted_iota(jnp.int32, shape, len(shape) - 1)


def _row_iota(shape):
    return lax.broadcasted_iota(jnp.int32, shape, len(shape) - 2)


def _dot_nt(a, b):
    return lax.dot_general(a, b, (((1,), (1,)), ((), ())), preferred_element_type=F32)


def _dot_tn(a, b):
    return lax.dot_general(a, b, (((0,), (0,)), ((), ())), preferred_element_type=F32)


def _ada_kernel(c_ref, w_ref, b_ref, o_ref):
    c = c_ref[...]
    cond = c * jax.nn.sigmoid(c)
    o_ref[0] = jnp.dot(cond, w_ref[0], precision=HIGHEST, preferred_element_type=F32) + b_ref[0]


def _ada_mod(c, w_ada, b_ada):
    depth, d, n = w_ada.shape
    tn = 768
    c8 = jnp.broadcast_to(c.reshape(1, d), (8, d))
    out = pl.pallas_call(
        _ada_kernel,
        grid=(depth, n // tn),
        in_specs=[pl.BlockSpec((8, d), lambda l, j: (0, 0)),
                  pl.BlockSpec((1, d, tn), lambda l, j: (l, 0, j)),
                  pl.BlockSpec((1, 1, tn), lambda l, j: (l, 0, j))],
        out_specs=pl.BlockSpec((1, 8, tn), lambda l, j: (l, 0, j)),
        out_shape=jax.ShapeDtypeStruct((depth, 8, n), F32),
        compiler_params=_cparams("parallel", "parallel"),
        name="ada_mod",
    )(c8, w_ada, b_ada.reshape(depth, 1, n))
    return out[:, 0:1, :]


def _norm_mod(x, g, sc, sh):
    ms = jnp.mean(x * x, axis=-1, keepdims=True)
    return (x * lax.rsqrt(ms + EPS) * g) * (1.0 + sc) + sh


def _nmm_kernel(x_ref, g_ref, sc_ref, sh_ref, w_ref, o_ref, h_scr):
    @pl.when(pl.program_id(1) == 0)
    def _():
        h_scr[...] = _norm_mod(x_ref[...], g_ref[...], sc_ref[...], sh_ref[...]).astype(BF16)

    o_ref[...] = jnp.dot(h_scr[...], w_ref[...], preferred_element_type=F32)


def _norm_mod_matmul(x2, g, sc, sh, w, tn):
    s, d = x2.shape
    n = w.shape[1]
    vec = pl.BlockSpec((1, d), lambda i, j: (0, 0))
    return pl.pallas_call(
        _nmm_kernel,
        grid=(s // ROW_TILE, n // tn),
        in_specs=[pl.BlockSpec((ROW_TILE, d), lambda i, j: (i, 0)), vec, vec, vec,
                  pl.BlockSpec((d, tn), lambda i, j: (0, j))],
        out_specs=pl.BlockSpec((ROW_TILE, tn), lambda i, j: (i, j)),
        out_shape=jax.ShapeDtypeStruct((s, n), F32),
        scratch_shapes=[pltpu.VMEM((ROW_TILE, d), BF16)],
        compiler_params=_cparams("parallel", "arbitrary"),
        name="norm_mod_matmul",
    )(x2, g, sc, sh, w)


def _head_rms(x, gain, n_real):
    ss = jnp.sum(x * x, axis=-1, keepdims=True)
    return x * lax.rsqrt(ss * (1.0 / n_real) + EPS) * gain


def _rope64(x, cos, sin):
    lane = _lane_iota(x.shape)
    rot = jnp.where(lane < 32, -pltpu.roll(x, LANE - 32, 1), pltpu.roll(x, 32, 1))
    return x * cos + rot * sin


def _split3(c):
    hi = c.astype(BF16).astype(F32)
    r1 = c - hi
    mid = r1.astype(BF16).astype(F32)
    lo = (r1 - mid).astype(BF16).astype(F32)
    return hi, mid, lo


def _hy_prep_kernel(p_ref, pos_ref, inv_ref, gq_ref, gk_ref, gfq_ref, gfk_ref, fb_ref,
                    qn_ref, kct_ref, vct_ref, ks_ref, vs_ref, kw_ref, vw_ref, gate_ref,
                    fq_ref, fk_ref, fv_ref, carry_ref):
    i = pl.program_id(0)
    tm = PREP_TILE
    shp = (tm, LANE)
    lane = _lane_iota(shp)

    def blk(b):
        return p_ref[:, b * LANE:(b + 1) * LANE]

    ang = pos_ref[...] * inv_ref[...]
    real = lane < HEAD_DIM
    cos = jnp.where(real, jnp.cos(ang), 1.0)
    sin = jnp.where(real, jnp.sin(ang), 0.0)
    gq, gk, gfq, gfk = gq_ref[...], gk_ref[...], gfq_ref[...], gfk_ref[...]
    scale = HEAD_DIM ** -0.5

    for h in range(NSA_HEADS):
        q = _rope64(_head_rms(blk(HY_Q0 + h), gq, HEAD_DIM), cos, sin) * scale
        qn_ref[:, h * LANE:(h + 1) * LANE] = q.astype(BF16)

    row = _row_iota(shp) + i * tm
    onehot = jnp.where(lane - HEAD_DIM == ((row // SLC_BLOCK) % SUPER_BLOCKS), 1.0, 0.0)
    for g in range(NSA_GROUPS):
        def kv(r):
            return blk(HY_KV0 + r * NSA_GROUPS + g)
        sl = slice(g * LANE, (g + 1) * LANE)
        kct_ref[g] = _rope64(kv(0), cos, sin)[:, :HEAD_DIM].astype(BF16)
        vct_ref[g] = kv(1)[:, :HEAD_DIM].astype(BF16)
        ks = _rope64(_head_rms(kv(2), gk, HEAD_DIM), cos, sin)
        ks_ref[:, sl] = (ks + onehot).astype(BF16)
        vs_ref[:, sl] = kv(3).astype(BF16)
        kw_ref[:, sl] = _rope64(_head_rms(kv(4), gk, HEAD_DIM), cos, sin).astype(BF16)
        vw_ref[:, sl] = kv(5).astype(BF16)
        gate_ref[:, sl] = jax.nn.sigmoid(blk(HY_G0 + g))

    @pl.when(i == 0)
    def _():
        carry_ref[...] = jnp.zeros_like(carry_ref)

    z = blk(HY_FF) + fb_ref[...]
    logf = jnp.minimum(z, 0.0) - jnp.log1p(jnp.exp(-jnp.abs(z)))
    tri = jnp.where(_row_iota((tm, tm)) >= _lane_iota((tm, tm)), 1.0, 0.0).astype(F32)
    cum = jnp.dot(tri, logf, precision=HIGHEST, preferred_element_type=F32) + carry_ref[...]
    carry_ref[...] = cum[tm - 1:tm, :]

    for h in range(FOX_HEADS):
        c = jnp.broadcast_to(cum[:, h:h + 1], shp)
        hi, mid, lo = _split3(c)
        fq = _head_rms(blk(HY_F0 + h), gfq, HEAD_DIM) * scale
        fq = jnp.where(real, fq, jnp.where(lane == 64, hi, jnp.where(lane == 65, mid, jnp.where(
            lane == 66, lo, jnp.where(lane < 70, 1.0, 0.0)))))
        fk = _head_rms(blk(HY_F0 + FOX_HEADS + h), gfk, HEAD_DIM)
        fk = jnp.where(real, fk, jnp.where(lane < 67, 1.0, jnp.where(lane == 67, -hi, jnp.where(
            lane == 68, -mid, jnp.where(lane == 69, -lo, 0.0)))))
        sl = slice(h * LANE, (h + 1) * LANE)
        fq_ref[:, sl] = fq.astype(BF16)
        fk_ref[:, sl] = fk.astype(BF16)
        fv_ref[:, sl] = blk(HY_F0 + 2 * FOX_HEADS + h).astype(BF16)


def _hy_prep(proj, posf, inv128, gq, gk, gfq, gfk, fbias):
    s = proj.shape[0]
    tm = PREP_TILE
    vec = pl.BlockSpec((1, LANE), lambda i: (0, 0))

    def rows(nb):
        return pl.BlockSpec((tm, nb * LANE), lambda i: (i, 0))

    def shape(nb, dt=BF16):
        return jax.ShapeDtypeStruct((s, nb * LANE), dt)

    tok = pl.BlockSpec((NSA_GROUPS, tm, HEAD_DIM), lambda i: (0, i, 0))
    tok_shape = jax.ShapeDtypeStruct((NSA_GROUPS, s, HEAD_DIM), BF16)
    return pl.pallas_call(
        _hy_prep_kernel,
        grid=(s // tm,),
        in_specs=[rows(HY_BLOCKS), pl.BlockSpec((tm, 1), lambda i: (i, 0)), vec, vec, vec, vec, vec, vec],
        out_specs=[rows(NSA_HEADS), tok, tok, rows(NSA_GROUPS), rows(NSA_GROUPS), rows(NSA_GROUPS),
                   rows(NSA_GROUPS), rows(NSA_GROUPS), rows(FOX_HEADS), rows(FOX_HEADS), rows(FOX_HEADS)],
        out_shape=[shape(NSA_HEADS), tok_shape, tok_shape, shape(NSA_GROUPS), shape(NSA_GROUPS),
                   shape(NSA_GROUPS), shape(NSA_GROUPS), shape(NSA_GROUPS, F32), shape(FOX_HEADS),
                   shape(FOX_HEADS), shape(FOX_HEADS)],
        scratch_shapes=[pltpu.VMEM((1, LANE), F32)],
        compiler_params=_cparams("arbitrary"),
        name="hybrid_prep",
    )(proj, posf, inv128, gq, gk, gfq, gfk, fbias)


def _compress_kernel(kc_ref, vc_ref, wk_ref, wv_ref, pek_ref, pev_ref, gk_ref, ko_ref, vo_ref):
    half = CMP_STRIDE * HEAD_DIM

    def comp(ch_ref, w_ref, pe_ref):
        ch = ch_ref[0]
        nc = ch.shape[0]
        a = jnp.dot(ch, w_ref[:half], preferred_element_type=F32)
        b = jnp.dot(ch, w_ref[half:], preferred_element_type=F32)
        nxt = pltpu.roll(b, nc - 1, 0)
        pe = jnp.dot(jnp.broadcast_to(pe_ref[...], (8, 2 * half)).astype(BF16), w_ref[...],
                     preferred_element_type=F32)[0:1]
        return a + nxt + pe

    ko_ref[0] = _head_rms(comp(kc_ref, wk_ref, pek_ref), gk_ref[...], HEAD_DIM).astype(BF16)
    vo_ref[0] = comp(vc_ref, wv_ref, pev_ref).astype(BF16)


def _compress(kct, vct, w_cmp, cmp_pe, k_norm):
    g, s, _ = kct.shape
    nc = s // CMP_STRIDE
    wide = CMP_STRIDE * HEAD_DIM
    kch = kct.reshape(g, nc, wide)
    vch = vct.reshape(g, nc, wide)
    w_pad = jnp.pad(w_cmp, ((0, 0), (0, 0), (0, LANE - HEAD_DIM))).astype(BF16)
    ch = pl.BlockSpec((1, nc, wide), lambda i: (i, 0, 0))
    wspec = pl.BlockSpec((2 * wide, LANE), lambda i: (0, 0))
    pespec = pl.BlockSpec((1, 2 * wide), lambda i: (0, 0))
    out = pl.BlockSpec((1, nc, LANE), lambda i: (i, 0, 0))
    oshape = jax.ShapeDtypeStruct((g, nc, LANE), BF16)
    return pl.pallas_call(
        _compress_kernel,
        grid=(g,),
        in_specs=[ch, ch, wspec, wspec, pespec, pespec, pl.BlockSpec((1, LANE), lambda i: (0, 0))],
        out_specs=[out, out],
        out_shape=[oshape, oshape],
        compiler_params=_cparams("parallel"),
        name="nsa_compress",
    )(kch, vch, w_pad[0], w_pad[1], cmp_pe[0].reshape(1, 2 * wide).astype(F32),
      cmp_pe[1].reshape(1, 2 * wide).astype(F32), _pad_lanes(k_norm))


def _masked_softmax(s, mask):
    s = jnp.where(mask, s, NEG)
    m = jnp.max(s, axis=-1, keepdims=True)
    e = jnp.where(mask, jnp.exp(s - m), 0.0)
    return e / jnp.maximum(jnp.sum(e, axis=-1, keepdims=True), 1e-30)


def _online_step(s, v_tile, m_ref, l_ref, acc_ref):
    m_prev = m_ref[...]
    m_new = jnp.maximum(m_prev, jnp.max(s, axis=-1, keepdims=True))
    a = jnp.exp(m_prev - m_new)
    p = jnp.exp(s - m_new)
    l_ref[...] = a * l_ref[...] + jnp.sum(p, axis=-1, keepdims=True)
    acc_ref[...] = a * acc_ref[...] + jnp.dot(p.astype(BF16), v_tile, preferred_element_type=F32)
    m_ref[...] = m_new


def _nsa_kernel(q_ref, kc_ref, vc_ref, ks_ref, vs_ref, kw_ref, vw_ref, gate_ref, ovt_ref, o_ref,
                qaug_ref, m_ref, l_ref, acc_ref, *, n_sel):
    i = pl.program_id(1)
    tq = Q_TILE_NSA
    rows = NSA_HPG * tq
    qs = i * tq
    nc = kc_ref.shape[1]
    nslc = ovt_ref.shape[0]
    n_super = nslc // SUPER_BLOCKS

    qst = jnp.concatenate([q_ref[:, h * LANE:(h + 1) * LANE] for h in range(NSA_HPG)], axis=0)
    tq_col = qs + (_row_iota((rows, 1)) % tq)

    s = _dot_nt(qst, kc_ref[0])
    cmp_end = _lane_iota((1, nc)) * CMP_STRIDE + (CMP_LEN - 1)
    p = _masked_softmax(s, cmp_end <= tq_col)
    o_cmp = jnp.dot(p.astype(BF16), vc_ref[0], preferred_element_type=F32)

    psum = p[0:tq]
    for h in range(1, NSA_HPG):
        psum = psum + p[h * tq:(h + 1) * tq]
    p_hi = psum.astype(BF16)
    p_lo = (psum - p_hi.astype(F32)).astype(BF16)
    ovt = ovt_ref[...]
    imp = _dot_nt(ovt, p_hi) + _dot_nt(ovt, p_lo)

    jj = _row_iota((nslc, tq))
    tq_row = qs + _lane_iota((nslc, tq))
    cur = tq_row // SLC_BLOCK
    forced = (jj == 0) | (jj == cur) | (jj == cur - 1)
    causal_blk = jj * SLC_BLOCK <= tq_row
    val = jnp.where(forced, imp + BIG, imp)
    val = jnp.where(causal_blk, val, NEG)

    jjf = jj.astype(F32)

    def pick(_, carry):
        val, sel = carry
        mx = jnp.max(val, axis=0, keepdims=True)
        idx = jnp.min(jnp.where(val == mx, jjf, float(nslc)), axis=0, keepdims=True)
        hit = jjf == idx
        return jnp.where(hit, -jnp.inf, val), jnp.where(hit, 1.0, sel)

    _, sel = lax.fori_loop(0, n_sel, pick, (val, jnp.zeros((nslc, tq), F32)))
    bias_t = jnp.where((sel > 0.0) & causal_blk, 0.0, NEG)
    bias = bias_t.T

    lane = _lane_iota((tq, LANE))
    bias_sh = pltpu.roll(bias, HEAD_DIM, 1)
    for st in range(n_super):
        c = st // 2
        src = bias_sh if st % 2 == 0 else bias
        b128 = src[:, c * LANE:(c + 1) * LANE]
        for h in range(NSA_HPG):
            qh = q_ref[:, h * LANE:(h + 1) * LANE].astype(F32)
            qaug_ref[st, h * tq:(h + 1) * tq, :] = jnp.where(lane < HEAD_DIM, qh, b128).astype(BF16)

    m_ref[...] = jnp.full(m_ref.shape, NEG, F32)
    l_ref[...] = jnp.zeros(l_ref.shape, F32)
    acc_ref[...] = jnp.zeros(acc_ref.shape, F32)
    tk = KV_TILE
    per_super = SUPER_BLOCKS * SLC_BLOCK // tk
    j_last = (qs + tq - 1) // tk

    def slc_scores(j):
        k0 = pl.multiple_of(j * tk, tk)
        return _dot_nt(qaug_ref[j // per_super], ks_ref[pl.ds(k0, tk), :]), vs_ref[pl.ds(k0, tk), :], k0

    def full_tile(j, _):
        s, v, _k0 = slc_scores(j)
        _online_step(s, v, m_ref, l_ref, acc_ref)
        return 0

    lax.fori_loop(0, j_last, full_tile, 0)
    s, v, k0 = slc_scores(j_last)
    kpos = k0 + _lane_iota((1, tk))
    _online_step(jnp.where(kpos <= tq_col, s, NEG), v, m_ref, l_ref, acc_ref)
    o_slc = acc_ref[...] / jnp.maximum(l_ref[...], 1e-30)

    wlen = WINDOW + tq
    ws = pl.multiple_of(jnp.maximum(qs - WINDOW, 0), tq)
    s = _dot_nt(qst, kw_ref[pl.ds(ws, wlen), :])
    dist = tq_col - (ws + _lane_iota((1, wlen)))
    p = _masked_softmax(s, (dist >= 0) & (dist < WINDOW))
    o_win = jnp.dot(p.astype(BF16), vw_ref[pl.ds(ws, wlen), :], preferred_element_type=F32)

    gate = gate_ref[...]
    for h in range(NSA_HPG):
        r = slice(h * tq, (h + 1) * tq)
        o = (gate[:, 3 * h:3 * h + 1] * o_cmp[r] + gate[:, 3 * h + 1:3 * h + 2] * o_slc[r]
             + gate[:, 3 * h + 2:3 * h + 3] * o_win[r])
        o_ref[:, h * LANE:(h + 1) * LANE] = o.astype(BF16)


def _overlap_t(s, nslc_pad):
    nc = s // CMP_STRIDE
    cmp_start = np.arange(nc) * CMP_STRIDE
    slc_start = np.arange(nslc_pad) * SLC_BLOCK
    ov = np.clip(np.minimum(cmp_start[:, None] + CMP_LEN, slc_start[None, :] + SLC_BLOCK)
                 - np.maximum(cmp_start[:, None], slc_start[None, :]), 0, None) / CMP_STRIDE
    ov[nc - CMP_LEN // CMP_STRIDE + 1:, :] = 0.0
    ov[:, s // SLC_BLOCK:] = 0.0
    return jnp.asarray(ov.T, BF16)


def _nsa_attention(qn, kc, vc, ks, vs, kw, vw, gates):
    s = qn.shape[0]
    nc = s // CMP_STRIDE
    n_slc = s // SLC_BLOCK
    nslc_pad = -(-n_slc // LANE) * LANE
    tq = Q_TILE_NSA
    rows = NSA_HPG * tq
    cm = pl.BlockSpec((1, nc, LANE), lambda g, i: (g, 0, 0))
    res = pl.BlockSpec((s, LANE), lambda g, i: (0, g))
    return pl.pallas_call(
        functools.partial(_nsa_kernel, n_sel=min(SLC_TOPK, n_slc)),
        grid=(NSA_GROUPS, s // tq),
        in_specs=[pl.BlockSpec((tq, NSA_HPG * LANE), lambda g, i: (i, g)), cm, cm, res, res, res, res,
                  pl.BlockSpec((tq, LANE), lambda g, i: (i, g)),
                  pl.BlockSpec((nslc_pad, nc), lambda g, i: (0, 0))],
        out_specs=pl.BlockSpec((tq, NSA_HPG * LANE), lambda g, i: (i, g)),
        out_shape=jax.ShapeDtypeStruct((s, NSA_HEADS * LANE), BF16),
        scratch_shapes=[pltpu.VMEM((nslc_pad // SUPER_BLOCKS, rows, LANE), BF16),
                        pltpu.VMEM((rows, 1), F32), pltpu.VMEM((rows, 1), F32),
                        pltpu.VMEM((rows, LANE), F32)],
        compiler_params=_cparams("parallel", "arbitrary"),
        name="nsa_attention",
    )(qn, kc, vc, ks, vs, kw, vw, gates, _overlap_t(s, nslc_pad))


def _flash_kernel(q_ref, k_ref, v_ref, o_ref, m_ref, l_ref, acc_ref):
    i = pl.program_id(1)
    tq, tk = Q_TILE_FLASH, KV_TILE
    q = q_ref[...]
    m_ref[...] = jnp.full(m_ref.shape, NEG, F32)
    l_ref[...] = jnp.zeros(l_ref.shape, F32)
    acc_ref[...] = jnp.zeros(acc_ref.shape, F32)

    def tile(j):
        k0 = pl.multiple_of(j * tk, tk)
        return _dot_nt(q, k_ref[pl.ds(k0, tk), :]), v_ref[pl.ds(k0, tk), :]

    def full_tile(j, _):
        s, v = tile(j)
        _online_step(s, v, m_ref, l_ref, acc_ref)
        return 0

    lax.fori_loop(0, i, full_tile, 0)
    s, v = tile(i)
    causal = _lane_iota((tq, tk)) <= _row_iota((tq, tk))
    _online_step(jnp.where(causal, s, NEG), v, m_ref, l_ref, acc_ref)
    o_ref[...] = (acc_ref[...] / l_ref[...]).astype(BF16)


def _causal_attention(q, k, v):
    s, width = q.shape
    heads = width // LANE
    assert Q_TILE_FLASH == KV_TILE
    res = pl.BlockSpec((s, LANE), lambda h, i: (0, h))
    tile = pl.BlockSpec((Q_TILE_FLASH, LANE), lambda h, i: (i, h))
    return pl.pallas_call(
        _flash_kernel,
        grid=(heads, s // Q_TILE_FLASH),
        in_specs=[tile, res, res],
        out_specs=tile,
        out_shape=jax.ShapeDtypeStruct((s, width), BF16),
        scratch_shapes=[pltpu.VMEM((Q_TILE_FLASH, 1), F32), pltpu.VMEM((Q_TILE_FLASH, 1), F32),
                        pltpu.VMEM((Q_TILE_FLASH, LANE), F32)],
        compiler_params=_cparams("parallel", "arbitrary"),
        name="causal_attention",
    )(q, k, v)


def _out_proj_kernel(oa_ref, ob_ref, wa_ref, wb_ref, x_ref, g_ref, o_ref):
    y = jnp.dot(oa_ref[...], wa_ref[...], preferred_element_type=F32)
    y = y + jnp.dot(ob_ref[...], wb_ref[...], preferred_element_type=F32)
    o_ref[...] = x_ref[...] + g_ref[...] * y


def _out_proj(oa, ob, cola, colb, wa, wb, x2, gate):
    s, d = x2.shape
    ka = wa.shape[0]
    tm = ROW_TILE
    return pl.pallas_call(
        _out_proj_kernel,
        grid=(s // tm,),
        in_specs=[pl.BlockSpec((tm, ka), lambda i: (i, cola)), pl.BlockSpec((tm, ka), lambda i: (i, colb)),
                  pl.BlockSpec((ka, d), lambda i: (0, 0)), pl.BlockSpec((ka, d), lambda i: (0, 0)),
                  pl.BlockSpec((tm, d), lambda i: (i, 0)), pl.BlockSpec((1, d), lambda i: (0, 0))],
        out_specs=pl.BlockSpec((tm, d), lambda i: (i, 0)),
        out_shape=jax.ShapeDtypeStruct((s, d), F32),
        compiler_params=_cparams("parallel"),
        name="out_proj",
    )(oa, ob, wa, wb, x2, gate)


def _mla_prep_kernel(p_ref, pos_ref, inv_ref, gqa_ref, gkva_ref, wuq_ref, wuk_ref, wuv_ref,
                     gq_ref, gk_ref, gkr_ref, q_ref, k_ref, v_ref):
    shp = (ROW_TILE, LANE)
    lane = _lane_iota(shp)
    nope = lane < QK_NOPE
    rope = (lane >= QK_NOPE) & (lane < QK_NOPE + QK_ROPE)
    ang = pos_ref[...] * inv_ref[...]
    cos = jnp.where(rope, jnp.cos(ang), 1.0)
    sin = jnp.where(rope, jnp.sin(ang), 0.0)

    def rope32(x):
        half = QK_ROPE // 2
        rot = jnp.where(lane < QK_NOPE + half, -pltpu.roll(x, LANE - half, 1), pltpu.roll(x, half, 1))
        return x * cos + rot * sin

    def low_rank_norm(x, g):
        ms = jnp.mean(x * x, axis=-1, keepdims=True)
        return (x * lax.rsqrt(ms + EPS) * g).astype(BF16)

    nq = Q_LORA // LANE
    cq = low_rank_norm(p_ref[:, :Q_LORA], gqa_ref[...])
    ckv = low_rank_norm(p_ref[:, Q_LORA:Q_LORA + KV_LORA], gkva_ref[...])
    kr = p_ref[:, (nq + KV_LORA // LANE) * LANE:(nq + KV_LORA // LANE + 1) * LANE]
    k_rope = rope32(_head_rms(kr, gkr_ref[...], QK_ROPE))

    gq, gk = gq_ref[...], gk_ref[...]
    scale = (QK_NOPE + QK_ROPE) ** -0.5
    pair = 2 * LANE
    for hp in range(MLA_HEADS // 2):
        cols = slice(hp * pair, (hp + 1) * pair)
        q2 = jnp.dot(cq, wuq_ref[:, cols], preferred_element_type=F32)
        k2 = jnp.dot(ckv, wuk_ref[:, cols], preferred_element_type=F32)
        v2 = jnp.dot(ckv, wuv_ref[:, cols], preferred_element_type=F32)
        v_ref[:, cols] = v2.astype(BF16)
        for sub in range(2):
            sl = slice(hp * pair + sub * LANE, hp * pair + (sub + 1) * LANE)
            x = q2[:, sub * LANE:(sub + 1) * LANE]
            ss_n = jnp.sum(jnp.where(nope, x * x, 0.0), axis=-1, keepdims=True)
            ss_r = jnp.sum(jnp.where(rope, x * x, 0.0), axis=-1, keepdims=True)
            inv_rms = jnp.where(nope, lax.rsqrt(ss_n * (1.0 / QK_NOPE) + EPS),
                                lax.rsqrt(ss_r * (1.0 / QK_ROPE) + EPS))
            q_ref[:, sl] = (rope32(x * inv_rms * gq) * scale).astype(BF16)
            kn = _head_rms(k2[:, sub * LANE:(sub + 1) * LANE], gk, QK_NOPE)
            k_ref[:, sl] = (kn + k_rope).astype(BF16)


def _mla_prep(proj, posf, inv128, gqa, gkva, wuq, wuk, wuv, gq, gk, gkr):
    s, n = proj.shape
    tm = ROW_TILE
    width = MLA_HEADS * LANE

    def full(a):
        return pl.BlockSpec(a.shape, lambda i: (0, 0))

    out = pl.BlockSpec((tm, width), lambda i: (i, 0))
    oshape = jax.ShapeDtypeStruct((s, width), BF16)
    args = (inv128, gqa, gkva, wuq, wuk, wuv, gq, gk, gkr)
    return pl.pallas_call(
        _mla_prep_kernel,
        grid=(s // tm,),
        in_specs=[pl.BlockSpec((tm, n), lambda i: (i, 0)), pl.BlockSpec((tm, 1), lambda i: (i, 0))]
                 + [full(a) for a in args],
        out_specs=[out, out, out],
        out_shape=[oshape, oshape, oshape],
        compiler_params=_cparams("parallel"),
        name="mla_prep",
    )(proj, posf, *args)


def _rank_lt(v, k):
    n = v.shape[0]
    row = _row_iota(v.shape)
    rank = jnp.zeros(v.shape, F32)
    for b in range(n):
        vb = v[b:b + 1, :]
        rank = rank + jnp.where((vb > v) | ((vb == v) & (row > b)), 1.0, 0.0)
    return rank < k


def _moe_route_kernel(x_ref, g_ref, sc_ref, sh_ref, wr_ref, rb_ref, h_ref, pos_ref, wt_ref, cnt_ref):
    tm = ROW_TILE
    h = _norm_mod(x_ref[...], g_ref[...], sc_ref[...], sh_ref[...])
    h_ref[...] = h.astype(BF16)
    logits = jnp.dot(h, wr_ref[...], precision=HIGHEST, preferred_element_type=F32)
    lt = logits.T[:N_EXPERTS]
    scores = jax.nn.sigmoid(lt)
    sel = scores + rb_ref[...]

    per = N_EXPERTS // N_GROUPS
    grp = sel.reshape(N_GROUPS, per, tm)
    sub = lax.broadcasted_iota(jnp.int32, grp.shape, 1)
    m1 = jnp.max(grp, axis=1, keepdims=True)
    first = jnp.min(jnp.where(grp == m1, sub, per), axis=1, keepdims=True)
    m2 = jnp.max(jnp.where(sub == first, -jnp.inf, grp), axis=1, keepdims=True)
    gscore = (m1 + m2).reshape(N_GROUPS, tm)
    gmask = _rank_lt(gscore, TOPK_GROUPS)
    emask = jnp.broadcast_to(gmask.reshape(N_GROUPS, 1, tm), grp.shape).reshape(N_EXPERTS, tm)
    chosen = _rank_lt(jnp.where(emask, sel, NEG), TOP_K)

    w = jnp.where(chosen, scores, 0.0)
    wt_ref[...] = w / jnp.sum(w, axis=0, keepdims=True) * ROUTED_SCALE

    upper = jnp.where(_row_iota((tm, tm)) <= _lane_iota((tm, tm)), 1.0, 0.0).astype(BF16)
    incl = jnp.dot(jnp.where(chosen, 1.0, 0.0).astype(BF16), upper, preferred_element_type=F32)
    pos_ref[...] = jnp.where(chosen, incl - 1.0, -1.0)
    cnt_ref[0] = jnp.broadcast_to(incl[:, tm - 1:tm], (N_EXPERTS, LANE))


def _moe_route(x2, g, sc, sh, w_router_pad, router_bias_col):
    s, d = x2.shape
    tm = ROW_TILE
    vec = pl.BlockSpec((1, d), lambda i: (0, 0))
    et = pl.BlockSpec((N_EXPERTS, tm), lambda i: (0, i))
    return pl.pallas_call(
        _moe_route_kernel,
        grid=(s // tm,),
        in_specs=[pl.BlockSpec((tm, d), lambda i: (i, 0)), vec, vec, vec,
                  pl.BlockSpec((d, LANE), lambda i: (0, 0)),
                  pl.BlockSpec((N_EXPERTS, 1), lambda i: (0, 0))],
        out_specs=[pl.BlockSpec((tm, d), lambda i: (i, 0)), et, et,
                   pl.BlockSpec((1, N_EXPERTS, LANE), lambda i: (i, 0, 0))],
        out_shape=[jax.ShapeDtypeStruct((s, d), BF16), jax.ShapeDtypeStruct((N_EXPERTS, s), F32),
                   jax.ShapeDtypeStruct((N_EXPERTS, s), F32),
                   jax.ShapeDtypeStruct((s // tm, N_EXPERTS, LANE), F32)],
        compiler_params=_cparams("parallel"),
        name="moe_route",
    )(x2, g, sc, sh, w_router_pad, router_bias_col)


def _moe_kernel(cnt_ref, x_ref, h_ref, pos_ref, wt_ref, wg_ref, wu_ref, wd_ref, sg_ref, su_ref, sd_ref,
                g2_ref, o_ref, acc_ref):
    i = pl.program_id(0)
    e = pl.program_id(1)
    tm = ROW_TILE
    r = MOE_CHUNK

    @pl.when(e == 0)
    def _():
        h = h_ref[...]
        a = jnp.dot(h, sg_ref[...], preferred_element_type=F32)
        a = a * jax.nn.sigmoid(a) * jnp.dot(h, su_ref[...], preferred_element_type=F32)
        acc_ref[...] = jnp.dot(a.astype(BF16), sd_ref[...], preferred_element_type=F32)

    n = cnt_ref[i * N_EXPERTS + e]
    prow = pos_ref[pl.ds(e, 1), :]
    wrow = wt_ref[pl.ds(e, 1), :]

    def chunk(c, _):
        slot = (_row_iota((r, tm)) + c * r).astype(F32)
        hit = prow == slot
        onehot = jnp.where(hit, 1.0, 0.0).astype(BF16)
        xg = jnp.dot(onehot, h_ref[...], preferred_element_type=F32).astype(BF16)
        a = jnp.dot(xg, wg_ref[0], preferred_element_type=F32)
        a = a * jax.nn.sigmoid(a) * jnp.dot(xg, wu_ref[0], preferred_element_type=F32)
        y = jnp.dot(a.astype(BF16), wd_ref[0], preferred_element_type=F32)
        wr = jnp.sum(jnp.where(hit, wrow, 0.0), axis=-1, keepdims=True)
        acc_ref[...] += _dot_tn(onehot, (y * wr).astype(BF16))
        return 0

    lax.fori_loop(0, (n + r - 1) // r, chunk, 0)

    @pl.when(e == N_EXPERTS - 1)
    def _():
        o_ref[...] = x_ref[...] + g2_ref[...] * acc_ref[...]


def _moe_experts(counts, x2, h, pos_t, w_t, wg, wu, wd, sg, su, sd, g2):
    s, d = x2.shape
    tm = ROW_TILE
    ff = wg.shape[2]
    tile = pl.BlockSpec((tm, d), lambda i, e, c: (i, 0))
    et = pl.BlockSpec((N_EXPERTS, tm), lambda i, e, c: (0, i))

    def const(a):
        return pl.BlockSpec(a.shape, lambda i, e, c: (0,) * a.ndim)

    grid_spec = pltpu.PrefetchScalarGridSpec(
        num_scalar_prefetch=1,
        grid=(s // tm, N_EXPERTS),
        in_specs=[tile, tile, et, et,
                  pl.BlockSpec((1, d, ff), lambda i, e, c: (e, 0, 0)),
                  pl.BlockSpec((1, d, ff), lambda i, e, c: (e, 0, 0)),
                  pl.BlockSpec((1, ff, d), lambda i, e, c: (e, 0, 0)),
                  const(sg), const(su), const(sd), const(g2)],
        out_specs=tile,
        scratch_shapes=[pltpu.VMEM((tm, d), F32)],
    )
    return pl.pallas_call(
        _moe_kernel,
        grid_spec=grid_spec,
        out_shape=jax.ShapeDtypeStruct((s, d), F32),
        compiler_params=_cparams("parallel", "arbitrary"),
        name="moe_experts",
    )(counts, x2, h, pos_t, w_t, wg, wu, wd, sg, su, sd, g2)


def _pad_lanes(v, width=LANE, offset=0):
    out = jnp.zeros((1, width), F32)
    return out.at[0, offset:offset + v.shape[0]].set(v.astype(F32))


def _head_cols(w, n_heads, dim):
    d = w.shape[0]
    w3 = w.reshape(d, n_heads, dim)
    return jnp.pad(w3, ((0, 0), (0, 0), (0, LANE - dim))).reshape(d, n_heads * LANE)


def _hybrid_w_in(w_in):
    d = w_in.shape[0]
    nq = NSA_HEADS * HEAD_DIM
    nkv = 6 * NSA_GROUPS * HEAD_DIM
    ng = 3 * NSA_HEADS
    nf = 3 * FOX_HEADS * HEAD_DIM
    c0, c1, c2, c3 = nq, nq + nkv, nq + nkv + ng, nq + nkv + ng + nf
    gates = w_in[:, c1:c2].reshape(d, NSA_GROUPS, 3 * NSA_HPG)
    gates = jnp.pad(gates, ((0, 0), (0, 0), (0, LANE - 3 * NSA_HPG))).reshape(d, NSA_GROUPS * LANE)
    ff = jnp.pad(w_in[:, c3:], ((0, 0), (0, 2 * LANE - FOX_HEADS)))
    return jnp.concatenate([
        _head_cols(w_in[:, :c0], NSA_HEADS, HEAD_DIM),
        _head_cols(w_in[:, c0:c1], 6 * NSA_GROUPS, HEAD_DIM),
        _head_cols(w_in[:, c2:c3], 3 * FOX_HEADS, HEAD_DIM),
        gates, ff], axis=1).astype(BF16)


def _pad_head_rows(w, n_heads, dim):
    d = w.shape[1]
    w3 = w.reshape(n_heads, dim, d)
    return jnp.pad(w3, ((0, 0), (0, LANE - dim), (0, 0))).reshape(n_heads * LANE, d).astype(BF16)


def _rope_inv(dim, offset):
    inv = ROPE_THETA ** (-jnp.arange(0, dim, 2, dtype=F32) / dim)
    return _pad_lanes(jnp.concatenate([inv, inv]), offset=offset)


def _hybrid_mixer(x2, posf, mods, norm_g, w_in, fox_f_bias, nsa_q_norm, nsa_k_norm, nsa_cmp_pe, nsa_w_cmp,
                  fox_q_norm, fox_k_norm, w_out):
    sh1, sc1, g1 = mods
    proj = _norm_mod_matmul(x2, norm_g, sc1, sh1, _hybrid_w_in(w_in), tn=HY_COL_TILE)
    (qn, kct, vct, ks, vs, kw, vw, gates, fq, fk, fv) = _hy_prep(
        proj, posf, _rope_inv(HEAD_DIM, 0), _pad_lanes(nsa_q_norm), _pad_lanes(nsa_k_norm),
        _pad_lanes(fox_q_norm), _pad_lanes(fox_k_norm), _pad_lanes(fox_f_bias))
    kc, vc = _compress(kct, vct, nsa_w_cmp, nsa_cmp_pe, nsa_k_norm)
    o_a = _nsa_attention(qn, kc, vc, ks, vs, kw, vw, gates)
    o_b = _causal_attention(fq, fk, fv)
    half = NSA_HEADS * HEAD_DIM
    wa = _pad_head_rows(w_out[:half], NSA_HEADS, HEAD_DIM)
    wb = _pad_head_rows(w_out[half:], FOX_HEADS, HEAD_DIM)
    return _out_proj(o_a, o_b, 0, 0, wa, wb, x2, g1)


def _mla_mixer(x2, posf, mods, norm_g, w_in, q_a_norm, kv_a_norm, w_uq, w_ukv, qn_norm, kn_norm, qr_norm,
               kr_norm, w_out):
    sh1, sc1, g1 = mods
    d = x2.shape[1]
    w_kr = jnp.zeros((d, LANE), F32).at[:, QK_NOPE:QK_NOPE + QK_ROPE].set(w_in[:, Q_LORA + KV_LORA:])
    w_in_p = jnp.concatenate([w_in[:, :Q_LORA + KV_LORA], w_kr], axis=1).astype(BF16)
    proj = _norm_mod_matmul(x2, norm_g, sc1, sh1, w_in_p, tn=w_in_p.shape[1])
    hq = QK_NOPE + QK_ROPE
    wuq = _head_cols(w_uq, MLA_HEADS, hq).astype(BF16)
    wkv3 = w_ukv.reshape(KV_LORA, MLA_HEADS, QK_NOPE + V_HEAD)
    wuk = _head_cols(wkv3[:, :, :QK_NOPE].reshape(KV_LORA, -1), MLA_HEADS, QK_NOPE).astype(BF16)
    wuv = _head_cols(wkv3[:, :, QK_NOPE:].reshape(KV_LORA, -1), MLA_HEADS, V_HEAD).astype(BF16)
    gq = _pad_lanes(jnp.concatenate([qn_norm, qr_norm]))
    q, k, v = _mla_prep(proj, posf, _rope_inv(QK_ROPE, QK_NOPE), q_a_norm.reshape(1, -1).astype(F32),
                        kv_a_norm.reshape(1, -1).astype(F32), wuq, wuk, wuv, gq, _pad_lanes(kn_norm),
                        _pad_lanes(kr_norm, offset=QK_NOPE))
    o = _causal_attention(q, k, v)
    w_pad = _pad_head_rows(w_out, MLA_HEADS, V_HEAD)
    half = w_pad.shape[0] // 2
    return _out_proj(o, o, 0, 1, w_pad[:half], w_pad[half:], x2, g1)


def _moe_ffn(x2, mods, norm_g, w_router, router_bias, w_gate, w_up, w_down, ws_gate, ws_up, ws_down):
    sh2, sc2, g2 = mods
    w_r = jnp.pad(w_router.astype(F32), ((0, 0), (0, LANE - N_EXPERTS)))
    h, pos_t, w_t, cnt = _moe_route(x2, norm_g, sc2, sh2, w_r, router_bias.reshape(N_EXPERTS, 1).astype(F32))
    counts = cnt[:, :, 0].astype(jnp.int32).reshape(-1)
    return _moe_experts(counts, x2, h, pos_t, w_t, w_gate.astype(BF16), w_up.astype(BF16),
                        w_down.astype(BF16), ws_gate.astype(BF16), ws_up.astype(BF16), ws_down.astype(BF16), g2)


def kernel(x, c, positions, norm_attn, norm_ffn, w_ada, b_ada, hy_w_in, fox_f_bias, nsa_q_norm, nsa_k_norm, nsa_cmp_pe, nsa_w_cmp, fox_q_norm, fox_k_norm, hy_w_out, mla_w_in, mla_q_a_norm, mla_kv_a_norm, mla_w_uq, mla_w_ukv, mla_qn_norm, mla_kn_norm, mla_qr_norm, mla_kr_norm, mla_w_out, moe_w_router, moe_router_bias, moe_w_gate, moe_w_up, moe_w_down, moe_ws_gate, moe_ws_up, moe_ws_down):
    b, s, d = x.shape
    assert b == 1 and s % KV_TILE == 0 and s >= WINDOW + Q_TILE_NSA
    depth = w_ada.shape[0]
    x2 = x.reshape(s, d).astype(F32)
    posf = positions.reshape(s, 1).astype(F32)
    mod = _ada_mod(c.astype(F32), w_ada.astype(F32), b_ada.astype(F32))

    for layer in range(depth):
        m = [mod[layer, :, k * d:(k + 1) * d] for k in range(6)]
        i = layer // 2
        g_attn = norm_attn[layer].reshape(1, d).astype(F32)
        if layer % 2 == 0:
            x2 = _hybrid_mixer(x2, posf, m[0:3], g_attn, hy_w_in[i], fox_f_bias[i], nsa_q_norm[i],
                               nsa_k_norm[i], nsa_cmp_pe[i], nsa_w_cmp[i], fox_q_norm[i], fox_k_norm[i],
                               hy_w_out[i])
        else:
            x2 = _mla_mixer(x2, posf, m[0:3], g_attn, mla_w_in[i], mla_q_a_norm[i], mla_kv_a_norm[i],
                            mla_w_uq[i], mla_w_ukv[i], mla_qn_norm[i], mla_kn_norm[i], mla_qr_norm[i],
                            mla_kr_norm[i], mla_w_out[i])
        x2 = _moe_ffn(x2, m[3:6], norm_ffn[layer].reshape(1, d).astype(F32), moe_w_router[layer],
                      moe_router_bias[layer], moe_w_gate[layer], moe_w_up[layer], moe_w_down[layer],
                      moe_ws_gate[layer], moe_ws_up[layer], moe_ws_down[layer])
    return x2.reshape(b, s, d)
```

```python
import functools

import numpy as np
import jax
import jax.numpy as jnp
from jax import lax
from jax.experimental import pallas as pl
from jax.experimental.pallas import tpu as pltpu

F32 = jnp.float32
BF16 = jnp.bfloat16
HIGHEST = lax.Precision.HIGHEST

LANE = 128
VMEM_LIMIT_BYTES = 56 * 1024 * 1024

HEAD_DIM = 64
NSA_HEADS = 8
NSA_GROUPS = 2
NSA_HPG = NSA_HEADS // NSA_GROUPS
CMP_LEN = 32
CMP_STRIDE = 16
SLC_BLOCK = 64
SLC_TOPK = 16
WINDOW = 512
FOX_HEADS = 8
MLA_HEADS = 16
Q_LORA = 384
KV_LORA = 256
QK_NOPE = 64
QK_ROPE = 32
V_HEAD = 64
N_EXPERTS = 64
TOP_K = 8
N_GROUPS = 8
TOPK_GROUPS = 4
EXPERT_FF = 256
ROUTED_SCALE = 2.5
ROPE_THETA = 10000.0
EPS = 1e-6
NEG = -1e30
BIG = 1e6

ROW_TILE = 512
PREP_TILE = 512
GATE_ROWS = 16
Q_TILE_NSA = 256
KV_TILE = 512
Q_TILE_FLASH = 1024
Q_STRIP = 256
SUPER_BLOCKS = 64
MOE_CHUNK = 128
MOE_EXPERTS_PER_STEP = 4

HY_Q0 = 0
HY_KV0 = HY_Q0 + NSA_HEADS
HY_F0 = HY_KV0 + 6 * NSA_GROUPS
HY_G0 = HY_F0 + 3 * FOX_HEADS
HY_FF = HY_G0 + NSA_GROUPS
HY_BLOCKS = HY_FF + 2
HY_COL_TILE = 12 * LANE
assert (HY_BLOCKS * LANE) % HY_COL_TILE == 0


def _cparams(*sem):
    return pltpu.CompilerParams(dimension_semantics=sem, vmem_limit_bytes=VMEM_LIMIT_BYTES)


def _lane_iota(shape):
    return lax.broadcasted_iota(jnp.int32, shape, len(shape) - 1)


def _row_iota(shape):
    return lax.broadcasted_iota(jnp.int32, shape, len(shape) - 2)


def _dot_nt(a, b):
    return lax.dot_general(a, b, (((1,), (1,)), ((), ())), preferred_element_type=F32)


def _dot_tn(a, b):
    return lax.dot_general(a, b, (((0,), (0,)), ((), ())), preferred_element_type=F32)


def _ada_kernel(c_ref, w_ref, b_ref, o_ref):
    c = c_ref[...]
    cond = c * jax.nn.sigmoid(c)
    o_ref[0] = jnp.dot(cond, w_ref[0], precision=HIGHEST, preferred_element_type=F32) + b_ref[0]


def _ada_mod(c, w_ada, b_ada):
    depth, d, n = w_ada.shape
    tn = 768
    c8 = jnp.broadcast_to(c.reshape(1, d), (8, d))
    out = pl.pallas_call(
        _ada_kernel,
        grid=(depth, n // tn),
        in_specs=[pl.BlockSpec((8, d), lambda l, j: (0, 0)),
                  pl.BlockSpec((1, d, tn), lambda l, j: (l, 0, j)),
                  pl.BlockSpec((1, 1, tn), lambda l, j: (l, 0, j))],
        out_specs=pl.BlockSpec((1, 8, tn), lambda l, j: (l, 0, j)),
        out_shape=jax.ShapeDtypeStruct((depth, 8, n), F32),
        compiler_params=_cparams("parallel", "parallel"),
        name="ada_mod",
    )(c8, w_ada, b_ada.reshape(depth, 1, n))
    return out[:, 0:1, :]


def _norm_mod(x, g, sc, sh):
    ms = jnp.mean(x * x, axis=-1, keepdims=True)
    return (x * lax.rsqrt(ms + EPS) * g) * (1.0 + sc) + sh


def _nmm_kernel(x_ref, g_ref, sc_ref, sh_ref, w_ref, o_ref, h_scr):
    @pl.when(pl.program_id(1) == 0)
    def _():
        h_scr[...] = _norm_mod(x_ref[...], g_ref[...], sc_ref[...], sh_ref[...]).astype(BF16)

    o_ref[...] = jnp.dot(h_scr[...], w_ref[...], preferred_element_type=F32)


def _norm_mod_matmul(x2, g, sc, sh, w, tn):
    s, d = x2.shape
    n = w.shape[1]
    vec = pl.BlockSpec((1, d), lambda i, j: (0, 0))
    return pl.pallas_call(
        _nmm_kernel,
        grid=(s // ROW_TILE, n // tn),
        in_specs=[pl.BlockSpec((ROW_TILE, d), lambda i, j: (i, 0)), vec, vec, vec,
                  pl.BlockSpec((d, tn), lambda i, j: (0, j))],
        out_specs=pl.BlockSpec((ROW_TILE, tn), lambda i, j: (i, j)),
        out_shape=jax.ShapeDtypeStruct((s, n), F32),
        scratch_shapes=[pltpu.VMEM((ROW_TILE, d), BF16)],
        compiler_params=_cparams("parallel", "arbitrary"),
        name="norm_mod_matmul",
    )(x2, g, sc, sh, w)


def _head_rms(x, gain, n_real):
    ss = jnp.sum(x * x, axis=-1, keepdims=True)
    return x * lax.rsqrt(ss * (1.0 / n_real) + EPS) * gain


def _rope64(x, cos, sin):
    lane = _lane_iota(x.shape)
    rot = jnp.where(lane < 32, -pltpu.roll(x, LANE - 32, 1), pltpu.roll(x, 32, 1))
    return x * cos + rot * sin


def _split3(c):
    hi = c.astype(BF16).astype(F32)
    r1 = c - hi
    mid = r1.astype(BF16).astype(F32)
    lo = (r1 - mid).astype(BF16).astype(F32)
    return hi, mid, lo


def _hy_prep_kernel(p_ref, pos_ref, inv_ref, gq_ref, gk_ref, gfq_ref, gfk_ref, fb_ref,
                    qnt_ref, kct_ref, vct_ref, ks_ref, vst_ref, kw_ref, vwt_ref, gate_ref,
                    fqt_ref, fk_ref, fvt_ref, carry_ref):
    i = pl.program_id(0)
    tm = PREP_TILE
    shp = (tm, LANE)
    lane = _lane_iota(shp)

    def blk(b):
        return p_ref[:, b * LANE:(b + 1) * LANE]

    ang = pos_ref[...] * inv_ref[...]
    real = lane < HEAD_DIM
    cos = jnp.where(real, jnp.cos(ang), 1.0)
    sin = jnp.where(real, jnp.sin(ang), 0.0)
    gq, gk, gfq, gfk = gq_ref[...], gk_ref[...], gfq_ref[...], gfk_ref[...]
    scale = HEAD_DIM ** -0.5

    for h in range(NSA_HEADS):
        q = _rope64(_head_rms(blk(HY_Q0 + h), gq, HEAD_DIM), cos, sin) * scale
        qnt_ref[h * LANE:(h + 1) * LANE, :] = q.T.astype(BF16)

    row = _row_iota(shp) + i * tm
    onehot = jnp.where(lane - HEAD_DIM == ((row // SLC_BLOCK) % SUPER_BLOCKS), 1.0, 0.0)
    for g in range(NSA_GROUPS):
        def kv(r):
            return blk(HY_KV0 + r * NSA_GROUPS + g)
        sl = slice(g * LANE, (g + 1) * LANE)
        kct_ref[g] = _rope64(kv(0), cos, sin)[:, :HEAD_DIM].astype(BF16)
        vct_ref[g] = kv(1)[:, :HEAD_DIM].astype(BF16)
        ks = _rope64(_head_rms(kv(2), gk, HEAD_DIM), cos, sin)
        ks_ref[:, sl] = (ks + onehot).astype(BF16)
        vst_ref[g, 0] = kv(3).T.astype(BF16)
        kw_ref[:, sl] = _rope64(_head_rms(kv(4), gk, HEAD_DIM), cos, sin).astype(BF16)
        vwt = kv(5).T.astype(BF16)
        for cidx in range(tm // LANE):
            vwt_ref[g, cidx] = vwt[:, cidx * LANE:(cidx + 1) * LANE]
        gate_ref[g] = jax.nn.sigmoid(blk(HY_G0 + g)).T[:GATE_ROWS]

    @pl.when(i == 0)
    def _():
        carry_ref[...] = jnp.zeros_like(carry_ref)

    z = blk(HY_FF) + fb_ref[...]
    logf = jnp.minimum(z, 0.0) - jnp.log1p(jnp.exp(-jnp.abs(z)))
    tri = jnp.where(_row_iota((tm, tm)) >= _lane_iota((tm, tm)), 1.0, 0.0).astype(F32)
    cum = jnp.dot(tri, logf, precision=HIGHEST, preferred_element_type=F32) + carry_ref[...]
    carry_ref[...] = cum[tm - 1:tm, :]

    for h in range(FOX_HEADS):
        c = jnp.broadcast_to(cum[:, h:h + 1], shp)
        hi, mid, lo = _split3(c)
        fq = _head_rms(blk(HY_F0 + h), gfq, HEAD_DIM) * scale
        fq = jnp.where(real, fq, jnp.where(lane == 64, hi, jnp.where(lane == 65, mid, jnp.where(
            lane == 66, lo, jnp.where(lane < 70, 1.0, 0.0)))))
        fk = _head_rms(blk(HY_F0 + FOX_HEADS + h), gfk, HEAD_DIM)
        fk = jnp.where(real, fk, jnp.where(lane < 67, 1.0, jnp.where(lane == 67, -hi, jnp.where(
            lane == 68, -mid, jnp.where(lane == 69, -lo, 0.0)))))
        sl = slice(h * LANE, (h + 1) * LANE)
        fqt_ref[sl, :] = fq.T.astype(BF16)
        fk_ref[:, sl] = fk.astype(BF16)
        fvt_ref[h, 0] = blk(HY_F0 + 2 * FOX_HEADS + h).T.astype(BF16)


def _feat_major(heads, s, tm):
    return (pl.BlockSpec((heads * LANE, tm), lambda i: (0, i)),
            jax.ShapeDtypeStruct((heads * LANE, s), BF16))


def _value_tiles(heads, s, tm, tk):
    return (pl.BlockSpec((heads, tm // tk, LANE, tk), lambda i: (0, i, 0, 0)),
            jax.ShapeDtypeStruct((heads, s // tk, LANE, tk), BF16))


def _hy_prep(proj, posf, inv128, gq, gk, gfq, gfk, fbias):
    s = proj.shape[0]
    tm = PREP_TILE
    assert tm == KV_TILE
    vec = pl.BlockSpec((1, LANE), lambda i: (0, 0))

    def rows(nb):
        return (pl.BlockSpec((tm, nb * LANE), lambda i: (i, 0)), jax.ShapeDtypeStruct((s, nb * LANE), BF16))

    tok = (pl.BlockSpec((NSA_GROUPS, tm, HEAD_DIM), lambda i: (0, i, 0)),
           jax.ShapeDtypeStruct((NSA_GROUPS, s, HEAD_DIM), BF16))
    gate = (pl.BlockSpec((NSA_GROUPS, GATE_ROWS, tm), lambda i: (0, 0, i)),
            jax.ShapeDtypeStruct((NSA_GROUPS, GATE_ROWS, s), F32))
    outs = [_feat_major(NSA_HEADS, s, tm), tok, tok, rows(NSA_GROUPS), _value_tiles(NSA_GROUPS, s, tm, KV_TILE),
            rows(NSA_GROUPS), _value_tiles(NSA_GROUPS, s, tm, LANE), gate,
            _feat_major(FOX_HEADS, s, tm), rows(FOX_HEADS), _value_tiles(FOX_HEADS, s, tm, KV_TILE)]
    return pl.pallas_call(
        _hy_prep_kernel,
        grid=(s // tm,),
        in_specs=[pl.BlockSpec((tm, HY_BLOCKS * LANE), lambda i: (i, 0)), pl.BlockSpec((tm, 1), lambda i: (i, 0)),
                  vec, vec, vec, vec, vec, vec],
        out_specs=[o[0] for o in outs],
        out_shape=[o[1] for o in outs],
        scratch_shapes=[pltpu.VMEM((1, LANE), F32)],
        compiler_params=_cparams("arbitrary"),
        name="hybrid_prep",
    )(proj, posf, inv128, gq, gk, gfq, gfk, fbias)


def _compress_kernel(kc_ref, vc_ref, wk_ref, wv_ref, pek_ref, pev_ref, gk_ref, ko_ref, vo_ref):
    half = CMP_STRIDE * HEAD_DIM

    def comp(ch_ref, w_ref, pe_ref):
        ch = ch_ref[0]
        nc = ch.shape[0]
        a = jnp.dot(ch, w_ref[:half], preferred_element_type=F32)
        b = jnp.dot(ch, w_ref[half:], preferred_element_type=F32)
        nxt = pltpu.roll(b, nc - 1, 0)
        pe = jnp.dot(jnp.broadcast_to(pe_ref[...], (8, 2 * half)).astype(BF16), w_ref[...],
                     preferred_element_type=F32)[0:1]
        return a + nxt + pe

    ko_ref[0] = _head_rms(comp(kc_ref, wk_ref, pek_ref), gk_ref[...], HEAD_DIM).astype(BF16)
    vo_ref[0] = comp(vc_ref, wv_ref, pev_ref).T.astype(BF16)


def _compress(kct, vct, w_cmp, cmp_pe, k_norm):
    g, s, _ = kct.shape
    nc = s // CMP_STRIDE
    wide = CMP_STRIDE * HEAD_DIM
    kch = kct.reshape(g, nc, wide)
    vch = vct.reshape(g, nc, wide)
    w_pad = jnp.pad(w_cmp, ((0, 0), (0, 0), (0, LANE - HEAD_DIM))).astype(BF16)
    ch = pl.BlockSpec((1, nc, wide), lambda i: (i, 0, 0))
    wspec = pl.BlockSpec((2 * wide, LANE), lambda i: (0, 0))
    pespec = pl.BlockSpec((1, 2 * wide), lambda i: (0, 0))
    return pl.pallas_call(
        _compress_kernel,
        grid=(g,),
        in_specs=[ch, ch, wspec, wspec, pespec, pespec, pl.BlockSpec((1, LANE), lambda i: (0, 0))],
        out_specs=[pl.BlockSpec((1, nc, LANE), lambda i: (i, 0, 0)),
                   pl.BlockSpec((1, LANE, nc), lambda i: (i, 0, 0))],
        out_shape=[jax.ShapeDtypeStruct((g, nc, LANE), BF16), jax.ShapeDtypeStruct((g, LANE, nc), BF16)],
        compiler_params=_cparams("parallel"),
        name="nsa_compress",
    )(kch, vch, w_pad[0], w_pad[1], cmp_pe[0].reshape(1, 2 * wide).astype(F32),
      cmp_pe[1].reshape(1, 2 * wide).astype(F32), _pad_lanes(k_norm))


def _masked_softmax_t(s, mask):
    s = jnp.where(mask, s, NEG)
    m = jnp.max(s, axis=0, keepdims=True)
    e = jnp.where(mask, jnp.exp(s - m), 0.0)
    return e / jnp.maximum(jnp.sum(e, axis=0, keepdims=True), 1e-30)


def _online_steps(steps, ml, acc_ref):
    ml = list(ml)

    def scores(step):
        k_tile, qt, _, c, _ = step
        return jnp.dot(k_tile, qt[:, c * Q_STRIP:(c + 1) * Q_STRIP], preferred_element_type=F32)

    s_next = scores(steps[0])
    for idx, (_, _, vt, c, mask) in enumerate(steps):
        sl = slice(c * Q_STRIP, (c + 1) * Q_STRIP)
        s = s_next
        if idx + 1 < len(steps):
            s_next = scores(steps[idx + 1])
        if mask is not None:
            s = jnp.where(mask, s, NEG)
        m, l = ml[c]
        m_new = jnp.maximum(m, jnp.max(s, axis=0, keepdims=True))
        a = jnp.exp(m - m_new)
        p = jnp.exp(s - m_new)
        ml[c] = (m_new, a * l + jnp.sum(p, axis=0, keepdims=True))
        acc_ref[:, sl] = a * acc_ref[:, sl] + jnp.dot(vt, p.astype(BF16), preferred_element_type=F32)
    return ml


def _ml_init(n_strips):
    return [(jnp.full((1, Q_STRIP), NEG, F32), jnp.zeros((1, Q_STRIP), F32)) for _ in range(n_strips)]


def _ml_flat(ml):
    return tuple(x for pair in ml for x in pair)


def _ml_nest(flat):
    return [(flat[2 * c], flat[2 * c + 1]) for c in range(len(flat) // 2)]


def _nsa_kernel(qt_ref, kc_ref, vct_ref, ks_ref, vst_ref, kw_ref, vwt_ref, gate_ref, ovt_ref, o_ref,
                qaug_ref, acc_ref, *, n_sel):
    i = pl.program_id(1)
    tq = Q_TILE_NSA
    cols = NSA_HPG * tq
    qs = i * tq
    nc = kc_ref.shape[1]
    nslc = ovt_ref.shape[0]
    n_super = nslc // SUPER_BLOCKS

    qt = jnp.concatenate([qt_ref[h * LANE:(h + 1) * LANE, :] for h in range(NSA_HPG)], axis=1)
    tq_row = qs + (_lane_iota((1, cols)) % tq)

    s = jnp.dot(kc_ref[0], qt, preferred_element_type=F32)
    cmp_end = _row_iota((nc, 1)) * CMP_STRIDE + (CMP_LEN - 1)
    p = _masked_softmax_t(s, cmp_end <= tq_row)
    o_cmp = jnp.dot(vct_ref[0], p.astype(BF16), preferred_element_type=F32)

    psum = p[:, 0:tq]
    for h in range(1, NSA_HPG):
        psum = psum + p[:, h * tq:(h + 1) * tq]
    p_hi = psum.astype(BF16)
    p_lo = (psum - p_hi.astype(F32)).astype(BF16)
    ovt = ovt_ref[...]
    imp = (jnp.dot(ovt, p_hi, preferred_element_type=F32)
           + jnp.dot(ovt, p_lo, preferred_element_type=F32))

    jj = _row_iota((nslc, tq))
    tq_blk = qs + _lane_iota((nslc, tq))
    cur = tq_blk // SLC_BLOCK
    forced = (jj == 0) | (jj == cur) | (jj == cur - 1)
    causal_blk = jj * SLC_BLOCK <= tq_blk
    val = jnp.where(forced, imp + BIG, imp)
    val = jnp.where(causal_blk, val, NEG)

    jjf = jj.astype(F32)

    def pick(_, carry):
        val, sel = carry
        mx = jnp.max(val, axis=0, keepdims=True)
        idx = jnp.min(jnp.where(val == mx, jjf, float(nslc)), axis=0, keepdims=True)
        hit = jjf == idx
        return jnp.where(hit, -jnp.inf, val), jnp.where(hit, 1.0, sel)

    _, sel = lax.fori_loop(0, n_sel, pick, (val, jnp.zeros((nslc, tq), F32)))
    bias_t = jnp.where((sel > 0.0) & causal_blk, 0.0, NEG)

    q_rows = qt[:HEAD_DIM].astype(F32)
    for st in range(n_super):
        b = bias_t[st * SUPER_BLOCKS:(st + 1) * SUPER_BLOCKS]
        b = jnp.concatenate([b] * NSA_HPG, axis=1)
        qaug_ref[st] = jnp.concatenate([q_rows, b], axis=0).astype(BF16)

    acc_ref[...] = jnp.zeros(acc_ref.shape, F32)
    tk = KV_TILE
    per_super = SUPER_BLOCKS * SLC_BLOCK // tk
    j_last = (qs + tq - 1) // tk

    n_strips = cols // Q_STRIP

    def slc_steps(j, masks=None):
        k0 = pl.multiple_of(j * tk, tk)
        k_tile, qa, vt = ks_ref[pl.ds(k0, tk), :], qaug_ref[j // per_super], vst_ref[0, j]
        return [(k_tile, qa, vt, c, None if masks is None else masks[c]) for c in range(n_strips)]

    def pair(jj, flat):
        return _ml_flat(_online_steps(slc_steps(2 * jj) + slc_steps(2 * jj + 1), _ml_nest(flat), acc_ref))

    def single(j, flat):
        return _ml_flat(_online_steps(slc_steps(j), _ml_nest(flat), acc_ref))

    n_pairs = j_last // 2
    flat = lax.fori_loop(0, n_pairs, pair, _ml_flat(_ml_init(n_strips)))
    flat = lax.fori_loop(2 * n_pairs, j_last, single, flat)
    kpos = j_last * tk + _row_iota((tk, 1))
    masks = [kpos <= tq_row[:, c * Q_STRIP:(c + 1) * Q_STRIP] for c in range(n_strips)]
    ml = _online_steps(slc_steps(j_last, masks), _ml_nest(flat), acc_ref)
    l = jnp.concatenate([x[1] for x in ml], axis=1)
    o_slc = acc_ref[...] / jnp.maximum(l, 1e-30)

    wlen = WINDOW + tq
    ws = pl.multiple_of(jnp.maximum(qs - WINDOW, 0), tq)
    s = jnp.dot(kw_ref[pl.ds(ws, wlen), :], qt, preferred_element_type=F32)
    dist = tq_row - (ws + _row_iota((wlen, 1)))
    p = _masked_softmax_t(s, (dist >= 0) & (dist < WINDOW)).astype(BF16)
    wb = ws // LANE
    o_win = jnp.zeros((LANE, cols), F32)
    for c in range(wlen // LANE):
        o_win = o_win + jnp.dot(vwt_ref[0, wb + c], p[c * LANE:(c + 1) * LANE], preferred_element_type=F32)

    gate = gate_ref[0]
    for h in range(NSA_HPG):
        sl = slice(h * tq, (h + 1) * tq)
        o = (gate[3 * h:3 * h + 1] * o_cmp[:, sl] + gate[3 * h + 1:3 * h + 2] * o_slc[:, sl]
             + gate[3 * h + 2:3 * h + 3] * o_win[:, sl])
        o_ref[:, h * LANE:(h + 1) * LANE] = o.T.astype(BF16)


def _overlap_t(s, nslc_pad):
    nc = s // CMP_STRIDE
    cmp_start = np.arange(nc) * CMP_STRIDE
    slc_start = np.arange(nslc_pad) * SLC_BLOCK
    ov = np.clip(np.minimum(cmp_start[:, None] + CMP_LEN, slc_start[None, :] + SLC_BLOCK)
                 - np.maximum(cmp_start[:, None], slc_start[None, :]), 0, None) / CMP_STRIDE
    ov[nc - CMP_LEN // CMP_STRIDE + 1:, :] = 0.0
    ov[:, s // SLC_BLOCK:] = 0.0
    return jnp.asarray(ov.T, BF16)


def _nsa_attention(qnt, kc, vct, ks, vst, kw, vwt, gates):
    s = qnt.shape[1]
    nc = s // CMP_STRIDE
    n_slc = s // SLC_BLOCK
    nslc_pad = -(-n_slc // LANE) * LANE
    tq = Q_TILE_NSA
    cols = NSA_HPG * tq
    once = pl.Buffered(1)
    res = pl.BlockSpec((s, LANE), lambda g, i: (0, g), pipeline_mode=once)
    return pl.pallas_call(
        functools.partial(_nsa_kernel, n_sel=min(SLC_TOPK, n_slc)),
        grid=(NSA_GROUPS, s // tq),
        in_specs=[pl.BlockSpec((NSA_HPG * LANE, tq), lambda g, i: (g, i)),
                  pl.BlockSpec((1, nc, LANE), lambda g, i: (g, 0, 0), pipeline_mode=once),
                  pl.BlockSpec((1, LANE, nc), lambda g, i: (g, 0, 0), pipeline_mode=once),
                  res, pl.BlockSpec((1, s // KV_TILE, LANE, KV_TILE), lambda g, i: (g, 0, 0, 0),
                                    pipeline_mode=once),
                  res, pl.BlockSpec((1, s // LANE, LANE, LANE), lambda g, i: (g, 0, 0, 0), pipeline_mode=once),
                  pl.BlockSpec((1, GATE_ROWS, tq), lambda g, i: (g, 0, i)),
                  pl.BlockSpec((nslc_pad, nc), lambda g, i: (0, 0), pipeline_mode=once)],
        out_specs=pl.BlockSpec((tq, NSA_HPG * LANE), lambda g, i: (i, g)),
        out_shape=jax.ShapeDtypeStruct((s, NSA_HEADS * LANE), BF16),
        scratch_shapes=[pltpu.VMEM((nslc_pad // SUPER_BLOCKS, LANE, cols), BF16),
                        pltpu.VMEM((LANE, cols), F32)],
        compiler_params=_cparams("parallel", "arbitrary"),
        name="nsa_attention",
    )(qnt, kc, vct, ks, vst, kw, vwt, gates, _overlap_t(s, nslc_pad))


def _flash_kernel(qt_ref, k_ref, vt_ref, o_ref, acc_ref):
    i = pl.program_id(1)
    tq, tk = Q_TILE_FLASH, KV_TILE
    qt = qt_ref[...]
    acc_ref[...] = jnp.zeros(acc_ref.shape, F32)

    n_strips = tq // Q_STRIP
    per_q = tq // tk

    def tile_steps(j, d=None):
        k0 = pl.multiple_of(j * tk, tk)
        k_tile, vt = k_ref[pl.ds(k0, tk), :], vt_ref[0, j]
        steps = []
        for c in range(n_strips):
            mask = None
            if d is not None:
                if d * tk > (c + 1) * Q_STRIP - 1:
                    continue
                if (d + 1) * tk - 1 > c * Q_STRIP:
                    shp = (tk, Q_STRIP)
                    mask = _row_iota(shp) + d * tk <= _lane_iota(shp) + c * Q_STRIP
            steps.append((k_tile, qt, vt, c, mask))
        return steps

    def below(jj, flat):
        steps = [st for t in range(per_q) for st in tile_steps(jj * per_q + t)]
        return _ml_flat(_online_steps(steps, _ml_nest(flat), acc_ref))

    flat = lax.fori_loop(0, i, below, _ml_flat(_ml_init(n_strips)))
    steps = [st for d in range(per_q) for st in tile_steps(i * per_q + d, d)]
    ml = _online_steps(steps, _ml_nest(flat), acc_ref)
    l = jnp.concatenate([x[1] for x in ml], axis=1)
    o = acc_ref[...] / l
    for c0 in range(0, tq, LANE):
        o_ref[c0:c0 + LANE, :] = o[:, c0:c0 + LANE].T.astype(BF16)


def _causal_attention(qt, k, vt):
    s, width = k.shape
    heads = width // LANE
    assert Q_TILE_FLASH % KV_TILE == 0 and s % Q_TILE_FLASH == 0
    return pl.pallas_call(
        _flash_kernel,
        grid=(heads, s // Q_TILE_FLASH),
        in_specs=[pl.BlockSpec((LANE, Q_TILE_FLASH), lambda h, i: (h, i)),
                  pl.BlockSpec((s, LANE), lambda h, i: (0, h)),
                  pl.BlockSpec((1, s // KV_TILE, LANE, KV_TILE), lambda h, i: (h, 0, 0, 0))],
        out_specs=pl.BlockSpec((Q_TILE_FLASH, LANE), lambda h, i: (i, h)),
        out_shape=jax.ShapeDtypeStruct((s, width), BF16),
        scratch_shapes=[pltpu.VMEM((LANE, Q_TILE_FLASH), F32)],
        compiler_params=_cparams("parallel", "arbitrary"),
        name="causal_attention",
    )(qt, k, vt)


def _out_proj_kernel(oa_ref, ob_ref, wa_ref, wb_ref, x_ref, g_ref, o_ref):
    y = jnp.dot(oa_ref[...], wa_ref[...], preferred_element_type=F32)
    y = y + jnp.dot(ob_ref[...], wb_ref[...], preferred_element_type=F32)
    o_ref[...] = x_ref[...] + g_ref[...] * y


def _out_proj(oa, ob, cola, colb, wa, wb, x2, gate):
    s, d = x2.shape
    ka = wa.shape[0]
    tm = ROW_TILE
    return pl.pallas_call(
        _out_proj_kernel,
        grid=(s // tm,),
        in_specs=[pl.BlockSpec((tm, ka), lambda i: (i, cola)), pl.BlockSpec((tm, ka), lambda i: (i, colb)),
                  pl.BlockSpec((ka, d), lambda i: (0, 0)), pl.BlockSpec((ka, d), lambda i: (0, 0)),
                  pl.BlockSpec((tm, d), lambda i: (i, 0)), pl.BlockSpec((1, d), lambda i: (0, 0))],
        out_specs=pl.BlockSpec((tm, d), lambda i: (i, 0)),
        out_shape=jax.ShapeDtypeStruct((s, d), F32),
        compiler_params=_cparams("parallel"),
        name="out_proj",
    )(oa, ob, wa, wb, x2, gate)


def _mla_prep_kernel(p_ref, pos_ref, inv_ref, gqa_ref, gkva_ref, wuq_ref, wuk_ref, wuv_ref,
                     gq_ref, gk_ref, gkr_ref, qt_ref, k_ref, vt_ref):
    shp = (PREP_TILE, LANE)
    lane = _lane_iota(shp)
    nope = lane < QK_NOPE
    rope = (lane >= QK_NOPE) & (lane < QK_NOPE + QK_ROPE)
    ang = pos_ref[...] * inv_ref[...]
    cos = jnp.where(rope, jnp.cos(ang), 1.0)
    sin = jnp.where(rope, jnp.sin(ang), 0.0)

    def rope32(x):
        half = QK_ROPE // 2
        rot = jnp.where(lane < QK_NOPE + half, -pltpu.roll(x, LANE - half, 1), pltpu.roll(x, half, 1))
        return x * cos + rot * sin

    def low_rank_norm(x, g):
        ms = jnp.mean(x * x, axis=-1, keepdims=True)
        return (x * lax.rsqrt(ms + EPS) * g).astype(BF16)

    nq = Q_LORA // LANE
    cq = low_rank_norm(p_ref[:, :Q_LORA], gqa_ref[...])
    ckv = low_rank_norm(p_ref[:, Q_LORA:Q_LORA + KV_LORA], gkva_ref[...])
    kr = p_ref[:, (nq + KV_LORA // LANE) * LANE:(nq + KV_LORA // LANE + 1) * LANE]
    k_rope = rope32(_head_rms(kr, gkr_ref[...], QK_ROPE))

    gq, gk = gq_ref[...], gk_ref[...]
    scale = (QK_NOPE + QK_ROPE) ** -0.5
    pair = 2 * LANE
    for hp in range(MLA_HEADS // 2):
        cols = slice(hp * pair, (hp + 1) * pair)
        q2 = jnp.dot(cq, wuq_ref[:, cols], preferred_element_type=F32)
        k2 = jnp.dot(ckv, wuk_ref[:, cols], preferred_element_type=F32)
        v2 = jnp.dot(ckv, wuv_ref[:, cols], preferred_element_type=F32)
        for sub in range(2):
            head = 2 * hp + sub
            sl = slice(head * LANE, (head + 1) * LANE)
            half = slice(sub * LANE, (sub + 1) * LANE)
            x = q2[:, half]
            ss_n = jnp.sum(jnp.where(nope, x * x, 0.0), axis=-1, keepdims=True)
            ss_r = jnp.sum(jnp.where(rope, x * x, 0.0), axis=-1, keepdims=True)
            inv_rms = jnp.where(nope, lax.rsqrt(ss_n * (1.0 / QK_NOPE) + EPS),
                                lax.rsqrt(ss_r * (1.0 / QK_ROPE) + EPS))
            qt_ref[sl, :] = (rope32(x * inv_rms * gq) * scale).T.astype(BF16)
            kn = _head_rms(k2[:, half], gk, QK_NOPE)
            k_ref[:, sl] = (kn + k_rope).astype(BF16)
            vt_ref[head, 0] = v2[:, half].T.astype(BF16)


def _mla_prep(proj, posf, inv128, gqa, gkva, wuq, wuk, wuv, gq, gk, gkr):
    s, n = proj.shape
    tm = PREP_TILE
    assert tm == KV_TILE

    def full(a):
        return pl.BlockSpec(a.shape, lambda i: (0, 0))

    outs = [_feat_major(MLA_HEADS, s, tm),
            (pl.BlockSpec((tm, MLA_HEADS * LANE), lambda i: (i, 0)),
             jax.ShapeDtypeStruct((s, MLA_HEADS * LANE), BF16)),
            _value_tiles(MLA_HEADS, s, tm, KV_TILE)]
    args = (inv128, gqa, gkva, wuq, wuk, wuv, gq, gk, gkr)
    return pl.pallas_call(
        _mla_prep_kernel,
        grid=(s // tm,),
        in_specs=[pl.BlockSpec((tm, n), lambda i: (i, 0)), pl.BlockSpec((tm, 1), lambda i: (i, 0))]
                 + [full(a) for a in args],
        out_specs=[o[0] for o in outs],
        out_shape=[o[1] for o in outs],
        compiler_params=_cparams("parallel"),
        name="mla_prep",
    )(proj, posf, *args)


def _rank_lt(v, k):
    n = v.shape[0]
    row = _row_iota(v.shape)
    rank = jnp.zeros(v.shape, F32)
    for b in range(n):
        vb = v[b:b + 1, :]
        rank = rank + jnp.where((vb > v) | ((vb == v) & (row > b)), 1.0, 0.0)
    return rank < k


def _moe_route_kernel(x_ref, g_ref, sc_ref, sh_ref, wr_ref, rb_ref, h_ref, pos_ref, wt_ref, cnt_ref):
    tm = ROW_TILE
    h = _norm_mod(x_ref[...], g_ref[...], sc_ref[...], sh_ref[...])
    h_ref[...] = h.astype(BF16)
    logits = jnp.dot(h, wr_ref[...], precision=HIGHEST, preferred_element_type=F32)
    lt = logits.T[:N_EXPERTS]
    scores = jax.nn.sigmoid(lt)
    sel = scores + rb_ref[...]

    per = N_EXPERTS // N_GROUPS
    grp = sel.reshape(N_GROUPS, per, tm)
    sub = lax.broadcasted_iota(jnp.int32, grp.shape, 1)
    m1 = jnp.max(grp, axis=1, keepdims=True)
    first = jnp.min(jnp.where(grp == m1, sub, per), axis=1, keepdims=True)
    m2 = jnp.max(jnp.where(sub == first, -jnp.inf, grp), axis=1, keepdims=True)
    gscore = (m1 + m2).reshape(N_GROUPS, tm)
    gmask = _rank_lt(gscore, TOPK_GROUPS)
    emask = jnp.broadcast_to(gmask.reshape(N_GROUPS, 1, tm), grp.shape).reshape(N_EXPERTS, tm)
    chosen = _rank_lt(jnp.where(emask, sel, NEG), TOP_K)

    w = jnp.where(chosen, scores, 0.0)
    wt_ref[...] = w / jnp.sum(w, axis=0, keepdims=True) * ROUTED_SCALE

    upper = jnp.where(_row_iota((tm, tm)) <= _lane_iota((tm, tm)), 1.0, 0.0).astype(BF16)
    incl = jnp.dot(jnp.where(chosen, 1.0, 0.0).astype(BF16), upper, preferred_element_type=F32)
    pos_ref[...] = jnp.where(chosen, incl - 1.0, -1.0)
    cnt_ref[0] = jnp.broadcast_to(incl[:, tm - 1:tm], (N_EXPERTS, LANE))


def _moe_route(x2, g, sc, sh, w_router_pad, router_bias_col):
    s, d = x2.shape
    tm = ROW_TILE
    vec = pl.BlockSpec((1, d), lambda i: (0, 0))
    et = pl.BlockSpec((N_EXPERTS, tm), lambda i: (0, i))
    return pl.pallas_call(
        _moe_route_kernel,
        grid=(s // tm,),
        in_specs=[pl.BlockSpec((tm, d), lambda i: (i, 0)), vec, vec, vec,
                  pl.BlockSpec((d, LANE), lambda i: (0, 0)),
                  pl.BlockSpec((N_EXPERTS, 1), lambda i: (0, 0))],
        out_specs=[pl.BlockSpec((tm, d), lambda i: (i, 0)), et, et,
                   pl.BlockSpec((1, N_EXPERTS, LANE), lambda i: (i, 0, 0))],
        out_shape=[jax.ShapeDtypeStruct((s, d), BF16), jax.ShapeDtypeStruct((N_EXPERTS, s), F32),
                   jax.ShapeDtypeStruct((N_EXPERTS, s), F32),
                   jax.ShapeDtypeStruct((s // tm, N_EXPERTS, LANE), F32)],
        compiler_params=_cparams("parallel"),
        name="moe_route",
    )(x2, g, sc, sh, w_router_pad, router_bias_col)


def _moe_kernel(cnt_ref, x_ref, h_ref, pos_ref, wt_ref, wg_ref, wu_ref, wd_ref, sg_ref, su_ref, sd_ref,
                g2_ref, o_ref, acc_ref):
    i = pl.program_id(0)
    e = pl.program_id(1)
    tm = ROW_TILE
    r = MOE_CHUNK

    @pl.when(e == 0)
    def _():
        h = h_ref[...]
        a = jnp.dot(h, sg_ref[...], preferred_element_type=F32)
        a = a * jax.nn.sigmoid(a) * jnp.dot(h, su_ref[...], preferred_element_type=F32)
        acc_ref[...] = jnp.dot(a.astype(BF16), sd_ref[...], preferred_element_type=F32)

    first = e * MOE_EXPERTS_PER_STEP
    n = cnt_ref[i * N_EXPERTS + first]
    for k in range(1, MOE_EXPERTS_PER_STEP):
        n = jnp.maximum(n, cnt_ref[i * N_EXPERTS + first + k])
    prows = [pos_ref[pl.ds(first + k, 1), :] for k in range(MOE_EXPERTS_PER_STEP)]
    wrows = [wt_ref[pl.ds(first + k, 1), :] for k in range(MOE_EXPERTS_PER_STEP)]

    def chunk(c, _):
        slot = (_row_iota((r, tm)) + c * r).astype(F32)
        hits = [prow == slot for prow in prows]
        onehot = jnp.concatenate([jnp.where(hit, 1.0, 0.0).astype(BF16) for hit in hits], axis=0)
        xg = jnp.dot(onehot, h_ref[...], preferred_element_type=F32).astype(BF16)
        ys = []
        for k in range(MOE_EXPERTS_PER_STEP):
            xk = xg[k * r:(k + 1) * r]
            a = jnp.dot(xk, wg_ref[k], preferred_element_type=F32)
            a = a * jax.nn.sigmoid(a) * jnp.dot(xk, wu_ref[k], preferred_element_type=F32)
            y = jnp.dot(a.astype(BF16), wd_ref[k], preferred_element_type=F32)
            wr = jnp.sum(jnp.where(hits[k], wrows[k], 0.0), axis=-1, keepdims=True)
            ys.append((y * wr).astype(BF16))
        acc_ref[...] += _dot_tn(onehot, jnp.concatenate(ys, axis=0))
        return 0

    lax.fori_loop(0, (n + r - 1) // r, chunk, 0)

    @pl.when(e == N_EXPERTS // MOE_EXPERTS_PER_STEP - 1)
    def _():
        o_ref[...] = x_ref[...] + g2_ref[...] * acc_ref[...]


def _moe_experts(counts, x2, h, pos_t, w_t, wg, wu, wd, sg, su, sd, g2):
    s, d = x2.shape
    tm = ROW_TILE
    ff = wg.shape[2]
    tile = pl.BlockSpec((tm, d), lambda i, e, c: (i, 0))
    et = pl.BlockSpec((N_EXPERTS, tm), lambda i, e, c: (0, i))

    def const(a):
        return pl.BlockSpec(a.shape, lambda i, e, c: (0,) * a.ndim)

    per = MOE_EXPERTS_PER_STEP
    grid_spec = pltpu.PrefetchScalarGridSpec(
        num_scalar_prefetch=1,
        grid=(s // tm, N_EXPERTS // per),
        in_specs=[tile, tile, et, et,
                  pl.BlockSpec((per, d, ff), lambda i, e, c: (e, 0, 0)),
                  pl.BlockSpec((per, d, ff), lambda i, e, c: (e, 0, 0)),
                  pl.BlockSpec((per, ff, d), lambda i, e, c: (e, 0, 0)),
                  const(sg), const(su), const(sd), const(g2)],
        out_specs=tile,
        scratch_shapes=[pltpu.VMEM((tm, d), F32)],
    )
    return pl.pallas_call(
        _moe_kernel,
        grid_spec=grid_spec,
        out_shape=jax.ShapeDtypeStruct((s, d), F32),
        compiler_params=_cparams("parallel", "arbitrary"),
        name="moe_experts",
    )(counts, x2, h, pos_t, w_t, wg, wu, wd, sg, su, sd, g2)


def _pad_lanes(v, width=LANE, offset=0):
    out = jnp.zeros((1, width), F32)
    return out.at[0, offset:offset + v.shape[0]].set(v.astype(F32))


def _head_cols(w, n_heads, dim):
    d = w.shape[0]
    w3 = w.reshape(d, n_heads, dim)
    return jnp.pad(w3, ((0, 0), (0, 0), (0, LANE - dim))).reshape(d, n_heads * LANE)


def _hybrid_w_in(w_in):
    d = w_in.shape[0]
    nq = NSA_HEADS * HEAD_DIM
    nkv = 6 * NSA_GROUPS * HEAD_DIM
    ng = 3 * NSA_HEADS
    nf = 3 * FOX_HEADS * HEAD_DIM
    c0, c1, c2, c3 = nq, nq + nkv, nq + nkv + ng, nq + nkv + ng + nf
    gates = w_in[:, c1:c2].reshape(d, NSA_GROUPS, 3 * NSA_HPG)
    gates = jnp.pad(gates, ((0, 0), (0, 0), (0, LANE - 3 * NSA_HPG))).reshape(d, NSA_GROUPS * LANE)
    ff = jnp.pad(w_in[:, c3:], ((0, 0), (0, 2 * LANE - FOX_HEADS)))
    return jnp.concatenate([
        _head_cols(w_in[:, :c0], NSA_HEADS, HEAD_DIM),
        _head_cols(w_in[:, c0:c1], 6 * NSA_GROUPS, HEAD_DIM),
        _head_cols(w_in[:, c2:c3], 3 * FOX_HEADS, HEAD_DIM),
        gates, ff], axis=1).astype(BF16)


def _pad_head_rows(w, n_heads, dim):
    d = w.shape[1]
    w3 = w.reshape(n_heads, dim, d)
    return jnp.pad(w3, ((0, 0), (0, LANE - dim), (0, 0))).reshape(n_heads * LANE, d).astype(BF16)


def _rope_inv(dim, offset):
    inv = ROPE_THETA ** (-jnp.arange(0, dim, 2, dtype=F32) / dim)
    return _pad_lanes(jnp.concatenate([inv, inv]), offset=offset)


def _hybrid_mixer(x2, posf, mods, norm_g, w_in, fox_f_bias, nsa_q_norm, nsa_k_norm, nsa_cmp_pe, nsa_w_cmp,
                  fox_q_norm, fox_k_norm, w_out):
    sh1, sc1, g1 = mods
    proj = _norm_mod_matmul(x2, norm_g, sc1, sh1, _hybrid_w_in(w_in), tn=HY_COL_TILE)
    (qnt, kct, vct, ks, vst, kw, vwt, gates, fqt, fk, fvt) = _hy_prep(
        proj, posf, _rope_inv(HEAD_DIM, 0), _pad_lanes(nsa_q_norm), _pad_lanes(nsa_k_norm),
        _pad_lanes(fox_q_norm), _pad_lanes(fox_k_norm), _pad_lanes(fox_f_bias))
    kc, vc_t = _compress(kct, vct, nsa_w_cmp, nsa_cmp_pe, nsa_k_norm)
    o_a = _nsa_attention(qnt, kc, vc_t, ks, vst, kw, vwt, gates)
    o_b = _causal_attention(fqt, fk, fvt)
    half = NSA_HEADS * HEAD_DIM
    wa = _pad_head_rows(w_out[:half], NSA_HEADS, HEAD_DIM)
    wb = _pad_head_rows(w_out[half:], FOX_HEADS, HEAD_DIM)
    return _out_proj(o_a, o_b, 0, 0, wa, wb, x2, g1)


def _mla_mixer(x2, posf, mods, norm_g, w_in, q_a_norm, kv_a_norm, w_uq, w_ukv, qn_norm, kn_norm, qr_norm,
               kr_norm, w_out):
    sh1, sc1, g1 = mods
    d = x2.shape[1]
    w_kr = jnp.zeros((d, LANE), F32).at[:, QK_NOPE:QK_NOPE + QK_ROPE].set(w_in[:, Q_LORA + KV_LORA:])
    w_in_p = jnp.concatenate([w_in[:, :Q_LORA + KV_LORA], w_kr], axis=1).astype(BF16)
    proj = _norm_mod_matmul(x2, norm_g, sc1, sh1, w_in_p, tn=w_in_p.shape[1])
    hq = QK_NOPE + QK_ROPE
    wuq = _head_cols(w_uq, MLA_HEADS, hq).astype(BF16)
    wkv3 = w_ukv.reshape(KV_LORA, MLA_HEADS, QK_NOPE + V_HEAD)
    wuk = _head_cols(wkv3[:, :, :QK_NOPE].reshape(KV_LORA, -1), MLA_HEADS, QK_NOPE).astype(BF16)
    wuv = _head_cols(wkv3[:, :, QK_NOPE:].reshape(KV_LORA, -1), MLA_HEADS, V_HEAD).astype(BF16)
    gq = _pad_lanes(jnp.concatenate([qn_norm, qr_norm]))
    qt, k, vt = _mla_prep(proj, posf, _rope_inv(QK_ROPE, QK_NOPE), q_a_norm.reshape(1, -1).astype(F32),
                          kv_a_norm.reshape(1, -1).astype(F32), wuq, wuk, wuv, gq, _pad_lanes(kn_norm),
                          _pad_lanes(kr_norm, offset=QK_NOPE))
    o = _causal_attention(qt, k, vt)
    w_pad = _pad_head_rows(w_out, MLA_HEADS, V_HEAD)
    half = w_pad.shape[0] // 2
    return _out_proj(o, o, 0, 1, w_pad[:half], w_pad[half:], x2, g1)


def _moe_ffn(x2, mods, norm_g, w_router, router_bias, w_gate, w_up, w_down, ws_gate, ws_up, ws_down):
    sh2, sc2, g2 = mods
    w_r = jnp.pad(w_router.astype(F32), ((0, 0), (0, LANE - N_EXPERTS)))
    h, pos_t, w_t, cnt = _moe_route(x2, norm_g, sc2, sh2, w_r, router_bias.reshape(N_EXPERTS, 1).astype(F32))
    counts = cnt[:, :, 0].astype(jnp.int32).reshape(-1)
    return _moe_experts(counts, x2, h, pos_t, w_t, w_gate.astype(BF16), w_up.astype(BF16),
                        w_down.astype(BF16), ws_gate.astype(BF16), ws_up.astype(BF16), ws_down.astype(BF16), g2)


def kernel(x, c, positions, norm_attn, norm_ffn, w_ada, b_ada, hy_w_in, fox_f_bias, nsa_q_norm, nsa_k_norm, nsa_cmp_pe, nsa_w_cmp, fox_q_norm, fox_k_norm, hy_w_out, mla_w_in, mla_q_a_norm, mla_kv_a_norm, mla_w_uq, mla_w_ukv, mla_qn_norm, mla_kn_norm, mla_qr_norm, mla_kr_norm, mla_w_out, moe_w_router, moe_router_bias, moe_w_gate, moe_w_up, moe_w_down, moe_ws_gate, moe_ws_up, moe_ws_down):
    b, s, d = x.shape
    assert b == 1 and s % KV_TILE == 0 and s >= WINDOW + Q_TILE_NSA
    depth = w_ada.shape[0]
    x2 = x.reshape(s, d).astype(F32)
    posf = positions.reshape(s, 1).astype(F32)
    mod = _ada_mod(c.astype(F32), w_ada.astype(F32), b_ada.astype(F32))

    for layer in range(depth):
        m = [mod[layer, :, k * d:(k + 1) * d] for k in range(6)]
        i = layer // 2
        g_attn = norm_attn[layer].reshape(1, d).astype(F32)
        if layer % 2 == 0:
            x2 = _hybrid_mixer(x2, posf, m[0:3], g_attn, hy_w_in[i], fox_f_bias[i], nsa_q_norm[i],
                               nsa_k_norm[i], nsa_cmp_pe[i], nsa_w_cmp[i], fox_q_norm[i], fox_k_norm[i],
                               hy_w_out[i])
        else:
            x2 = _mla_mixer(x2, posf, m[0:3], g_attn, mla_w_in[i], mla_q_a_norm[i], mla_kv_a_norm[i],
                            mla_w_uq[i], mla_w_ukv[i], mla_qn_norm[i], mla_kn_norm[i], mla_qr_norm[i],
                            mla_kr_norm[i], mla_w_out[i])
        x2 = _moe_ffn(x2, m[3:6], norm_ffn[layer].reshape(1, d).astype(F32), moe_w_router[layer],
                      moe_router_bias[layer], moe_w_gate[layer], moe_w_up[layer], moe_w_down[layer],
                      moe_ws_gate[layer], moe_ws_up[layer], moe_ws_down[layer])
    return x2.reshape(b, s, d)
```

```python
import functools

import numpy as np
import jax
import jax.numpy as jnp
from jax import lax
from jax.experimental import pallas as pl
from jax.experimental.pallas import tpu as pltpu

F32 = jnp.float32
BF16 = jnp.bfloat16
HIGHEST = lax.Precision.HIGHEST

LANE = 128
VMEM_LIMIT_BYTES = 56 * 1024 * 1024

HEAD_DIM = 64
NSA_HEADS = 8
NSA_GROUPS = 2
NSA_HPG = NSA_HEADS // NSA_GROUPS
CMP_LEN = 32
CMP_STRIDE = 16
SLC_BLOCK = 64
SLC_TOPK = 16
WINDOW = 512
FOX_HEADS = 8
MLA_HEADS = 16
Q_LORA = 384
KV_LORA = 256
QK_NOPE = 64
QK_ROPE = 32
V_HEAD = 64
N_EXPERTS = 64
TOP_K = 8
N_GROUPS = 8
TOPK_GROUPS = 4
EXPERT_FF = 256
ROUTED_SCALE = 2.5
ROPE_THETA = 10000.0
EPS = 1e-6
NEG = -1e30
BIG = 1e6

ROW_TILE = 512
PREP_TILE = 512
GATE_ROWS = 16
Q_TILE_NSA = 256
KV_TILE = 512
Q_TILE_FLASH = 1024
Q_STRIP = 256
SUPER_BLOCKS = 32
SUM_ROW = 64
REF_ROW = 104
REF_SLAB = 96
EXP_GUARD = 100.0
LOG2E = 1.4426950408889634
MOE_CHUNK = 128
MOE_EXPERTS_PER_STEP = 4

HY_Q0 = 0
HY_KV0 = HY_Q0 + NSA_HEADS
HY_F0 = HY_KV0 + 6 * NSA_GROUPS
HY_G0 = HY_F0 + 3 * FOX_HEADS
HY_FF = HY_G0 + NSA_GROUPS
HY_BLOCKS = HY_FF + 2
HY_COL_TILE = 12 * LANE
assert (HY_BLOCKS * LANE) % HY_COL_TILE == 0


def _cparams(*sem):
    return pltpu.CompilerParams(dimension_semantics=sem, vmem_limit_bytes=VMEM_LIMIT_BYTES)


def _lane_iota(shape):
    return lax.broadcasted_iota(jnp.int32, shape, len(shape) - 1)


def _row_iota(shape):
    return lax.broadcasted_iota(jnp.int32, shape, len(shape) - 2)


def _dot_nt(a, b):
    return lax.dot_general(a, b, (((1,), (1,)), ((), ())), preferred_element_type=F32)


def _dot_tn(a, b):
    return lax.dot_general(a, b, (((0,), (0,)), ((), ())), preferred_element_type=F32)


def _ada_kernel(c_ref, w_ref, b_ref, o_ref):
    c = c_ref[...]
    cond = c * jax.nn.sigmoid(c)
    o_ref[0] = jnp.dot(cond, w_ref[0], precision=HIGHEST, preferred_element_type=F32) + b_ref[0]


def _ada_mod(c, w_ada, b_ada):
    depth, d, n = w_ada.shape
    tn = 768
    c8 = jnp.broadcast_to(c.reshape(1, d), (8, d))
    out = pl.pallas_call(
        _ada_kernel,
        grid=(depth, n // tn),
        in_specs=[pl.BlockSpec((8, d), lambda l, j: (0, 0)),
                  pl.BlockSpec((1, d, tn), lambda l, j: (l, 0, j)),
                  pl.BlockSpec((1, 1, tn), lambda l, j: (l, 0, j))],
        out_specs=pl.BlockSpec((1, 8, tn), lambda l, j: (l, 0, j)),
        out_shape=jax.ShapeDtypeStruct((depth, 8, n), F32),
        compiler_params=_cparams("parallel", "parallel"),
        name="ada_mod",
    )(c8, w_ada, b_ada.reshape(depth, 1, n))
    return out[:, 0:1, :]


def _norm_mod(x, g, sc, sh):
    ms = jnp.mean(x * x, axis=-1, keepdims=True)
    return (x * lax.rsqrt(ms + EPS) * g) * (1.0 + sc) + sh


def _nmm_kernel(x_ref, g_ref, sc_ref, sh_ref, w_ref, o_ref, h_scr):
    @pl.when(pl.program_id(1) == 0)
    def _():
        h_scr[...] = _norm_mod(x_ref[...], g_ref[...], sc_ref[...], sh_ref[...]).astype(BF16)

    o_ref[...] = jnp.dot(h_scr[...], w_ref[...], preferred_element_type=F32)


def _norm_mod_matmul(x2, g, sc, sh, w, tn):
    s, d = x2.shape
    n = w.shape[1]
    vec = pl.BlockSpec((1, d), lambda i, j: (0, 0))
    return pl.pallas_call(
        _nmm_kernel,
        grid=(s // ROW_TILE, n // tn),
        in_specs=[pl.BlockSpec((ROW_TILE, d), lambda i, j: (i, 0)), vec, vec, vec,
                  pl.BlockSpec((d, tn), lambda i, j: (0, j))],
        out_specs=pl.BlockSpec((ROW_TILE, tn), lambda i, j: (i, j)),
        out_shape=jax.ShapeDtypeStruct((s, n), F32),
        scratch_shapes=[pltpu.VMEM((ROW_TILE, d), BF16)],
        compiler_params=_cparams("parallel", "arbitrary"),
        name="norm_mod_matmul",
    )(x2, g, sc, sh, w)


def _head_rms(x, gain, n_real):
    ss = jnp.sum(x * x, axis=-1, keepdims=True)
    return x * lax.rsqrt(ss * (1.0 / n_real) + EPS) * gain


def _rope64(x, cos, sin):
    lane = _lane_iota(x.shape)
    rot = jnp.where(lane < 32, -pltpu.roll(x, LANE - 32, 1), pltpu.roll(x, 32, 1))
    return x * cos + rot * sin


def _split3(c):
    hi = c.astype(BF16).astype(F32)
    r1 = c - hi
    mid = r1.astype(BF16).astype(F32)
    lo = (r1 - mid).astype(BF16).astype(F32)
    return hi, mid, lo


def _hy_prep_kernel(p_ref, pos_ref, inv_ref, gq_ref, gk_ref, gfq_ref, gfk_ref, fb_ref,
                    qnt_ref, kct_ref, vct_ref, ks_ref, vst_ref, kw_ref, vwt_ref, gate_ref,
                    fqt_ref, fk_ref, fvt_ref, carry_ref):
    i = pl.program_id(0)
    tm = PREP_TILE
    shp = (tm, LANE)
    lane = _lane_iota(shp)

    def blk(b):
        return p_ref[:, b * LANE:(b + 1) * LANE]

    ang = pos_ref[...] * inv_ref[...]
    real = lane < HEAD_DIM
    cos = jnp.where(real, jnp.cos(ang), 1.0)
    sin = jnp.where(real, jnp.sin(ang), 0.0)
    gq, gk, gfq, gfk = gq_ref[...], gk_ref[...], gfq_ref[...], gfk_ref[...]
    scale = HEAD_DIM ** -0.5 * LOG2E
    ones_row = lane == SUM_ROW
    ref_ones = jnp.where((lane >= REF_ROW) & (lane < REF_ROW + 3), 1.0, 0.0)

    for h in range(NSA_HEADS):
        q = _rope64(_head_rms(blk(HY_Q0 + h), gq, HEAD_DIM), cos, sin) * scale
        qnt_ref[h * LANE:(h + 1) * LANE, :] = q.T.astype(BF16)

    row = _row_iota(shp) + i * tm
    onehot = jnp.where(lane - HEAD_DIM == ((row // SLC_BLOCK) % SUPER_BLOCKS), 1.0, 0.0)
    for g in range(NSA_GROUPS):
        def kv(r):
            return blk(HY_KV0 + r * NSA_GROUPS + g)
        sl = slice(g * LANE, (g + 1) * LANE)
        kct_ref[g] = _rope64(kv(0), cos, sin)[:, :HEAD_DIM].astype(BF16)
        vct_ref[g] = kv(1)[:, :HEAD_DIM].astype(BF16)
        ks = _rope64(_head_rms(kv(2), gk, HEAD_DIM), cos, sin)
        ks_ref[:, sl] = (ks + onehot + ref_ones).astype(BF16)
        vst_ref[g, 0] = jnp.where(ones_row, 1.0, kv(3)).T.astype(BF16)
        kw_ref[:, sl] = _rope64(_head_rms(kv(4), gk, HEAD_DIM), cos, sin).astype(BF16)
        vwt = kv(5).T.astype(BF16)
        for cidx in range(tm // LANE):
            vwt_ref[g, cidx] = vwt[:, cidx * LANE:(cidx + 1) * LANE]
        gate_ref[g] = jax.nn.sigmoid(blk(HY_G0 + g)).T[:GATE_ROWS]

    @pl.when(i == 0)
    def _():
        carry_ref[...] = jnp.zeros_like(carry_ref)

    z = blk(HY_FF) + fb_ref[...]
    logf = jnp.minimum(z, 0.0) - jnp.log1p(jnp.exp(-jnp.abs(z)))
    tri = jnp.where(_row_iota((tm, tm)) >= _lane_iota((tm, tm)), 1.0, 0.0).astype(F32)
    cum = jnp.dot(tri, logf, precision=HIGHEST, preferred_element_type=F32) + carry_ref[...]
    carry_ref[...] = cum[tm - 1:tm, :]

    for h in range(FOX_HEADS):
        c = jnp.broadcast_to(cum[:, h:h + 1], shp) * LOG2E
        hi, mid, lo = _split3(c)
        fq = _head_rms(blk(HY_F0 + h), gfq, HEAD_DIM) * scale
        fq = jnp.where(real, fq, jnp.where(lane == 64, hi, jnp.where(lane == 65, mid, jnp.where(
            lane == 66, lo, jnp.where(lane < 70, 1.0, 0.0)))))
        fk = _head_rms(blk(HY_F0 + FOX_HEADS + h), gfk, HEAD_DIM)
        fk = jnp.where(real, fk, jnp.where(lane < 67, 1.0, jnp.where(lane == 67, -hi, jnp.where(
            lane == 68, -mid, jnp.where(lane == 69, -lo, ref_ones)))))
        sl = slice(h * LANE, (h + 1) * LANE)
        fqt_ref[sl, :] = fq.T.astype(BF16)
        fk_ref[:, sl] = fk.astype(BF16)
        fvt_ref[h, 0] = jnp.where(ones_row, 1.0, blk(HY_F0 + 2 * FOX_HEADS + h)).T.astype(BF16)


def _feat_major(heads, s, tm):
    return (pl.BlockSpec((heads * LANE, tm), lambda i: (0, i)),
            jax.ShapeDtypeStruct((heads * LANE, s), BF16))


def _value_tiles(heads, s, tm, tk):
    return (pl.BlockSpec((heads, tm // tk, LANE, tk), lambda i: (0, i, 0, 0)),
            jax.ShapeDtypeStruct((heads, s // tk, LANE, tk), BF16))


def _hy_prep(proj, posf, inv128, gq, gk, gfq, gfk, fbias):
    s = proj.shape[0]
    tm = PREP_TILE
    assert tm == KV_TILE
    vec = pl.BlockSpec((1, LANE), lambda i: (0, 0))

    def rows(nb):
        return (pl.BlockSpec((tm, nb * LANE), lambda i: (i, 0)), jax.ShapeDtypeStruct((s, nb * LANE), BF16))

    tok = (pl.BlockSpec((NSA_GROUPS, tm, HEAD_DIM), lambda i: (0, i, 0)),
           jax.ShapeDtypeStruct((NSA_GROUPS, s, HEAD_DIM), BF16))
    gate = (pl.BlockSpec((NSA_GROUPS, GATE_ROWS, tm), lambda i: (0, 0, i)),
            jax.ShapeDtypeStruct((NSA_GROUPS, GATE_ROWS, s), F32))
    outs = [_feat_major(NSA_HEADS, s, tm), tok, tok, rows(NSA_GROUPS), _value_tiles(NSA_GROUPS, s, tm, KV_TILE),
            rows(NSA_GROUPS), _value_tiles(NSA_GROUPS, s, tm, LANE), gate,
            _feat_major(FOX_HEADS, s, tm), rows(FOX_HEADS), _value_tiles(FOX_HEADS, s, tm, KV_TILE)]
    return pl.pallas_call(
        _hy_prep_kernel,
        grid=(s // tm,),
        in_specs=[pl.BlockSpec((tm, HY_BLOCKS * LANE), lambda i: (i, 0)), pl.BlockSpec((tm, 1), lambda i: (i, 0)),
                  vec, vec, vec, vec, vec, vec],
        out_specs=[o[0] for o in outs],
        out_shape=[o[1] for o in outs],
        scratch_shapes=[pltpu.VMEM((1, LANE), F32)],
        compiler_params=_cparams("arbitrary"),
        name="hybrid_prep",
    )(proj, posf, inv128, gq, gk, gfq, gfk, fbias)


def _compress_kernel(kc_ref, vc_ref, wk_ref, wv_ref, pek_ref, pev_ref, gk_ref, ko_ref, vo_ref):
    half = CMP_STRIDE * HEAD_DIM

    def comp(ch_ref, w_ref, pe_ref):
        ch = ch_ref[0]
        nc = ch.shape[0]
        a = jnp.dot(ch, w_ref[:half], preferred_element_type=F32)
        b = jnp.dot(ch, w_ref[half:], preferred_element_type=F32)
        nxt = pltpu.roll(b, nc - 1, 0)
        pe = jnp.dot(jnp.broadcast_to(pe_ref[...], (8, 2 * half)).astype(BF16), w_ref[...],
                     preferred_element_type=F32)[0:1]
        return a + nxt + pe

    ko_ref[0] = _head_rms(comp(kc_ref, wk_ref, pek_ref), gk_ref[...], HEAD_DIM).astype(BF16)
    vo_ref[0] = comp(vc_ref, wv_ref, pev_ref).T.astype(BF16)


def _compress(kct, vct, w_cmp, cmp_pe, k_norm):
    g, s, _ = kct.shape
    nc = s // CMP_STRIDE
    wide = CMP_STRIDE * HEAD_DIM
    kch = kct.reshape(g, nc, wide)
    vch = vct.reshape(g, nc, wide)
    w_pad = jnp.pad(w_cmp, ((0, 0), (0, 0), (0, LANE - HEAD_DIM))).astype(BF16)
    ch = pl.BlockSpec((1, nc, wide), lambda i: (i, 0, 0))
    wspec = pl.BlockSpec((2 * wide, LANE), lambda i: (0, 0))
    pespec = pl.BlockSpec((1, 2 * wide), lambda i: (0, 0))
    return pl.pallas_call(
        _compress_kernel,
        grid=(g,),
        in_specs=[ch, ch, wspec, wspec, pespec, pespec, pl.BlockSpec((1, LANE), lambda i: (0, 0))],
        out_specs=[pl.BlockSpec((1, nc, LANE), lambda i: (i, 0, 0)),
                   pl.BlockSpec((1, LANE, nc), lambda i: (i, 0, 0))],
        out_shape=[jax.ShapeDtypeStruct((g, nc, LANE), BF16), jax.ShapeDtypeStruct((g, LANE, nc), BF16)],
        compiler_params=_cparams("parallel"),
        name="nsa_compress",
    )(kch, vch, w_pad[0], w_pad[1], cmp_pe[0].reshape(1, 2 * wide).astype(F32),
      cmp_pe[1].reshape(1, 2 * wide).astype(F32), _pad_lanes(k_norm))


def _masked_softmax_t(s, mask):
    s = jnp.where(mask, s, NEG)
    m = jnp.max(s, axis=0, keepdims=True)
    e = jnp.where(mask, jnp.exp2(s - m), 0.0)
    return e / jnp.maximum(jnp.sum(e, axis=0, keepdims=True), 1e-30)


def _online_steps(steps, ms, acc_ref):
    ms = list(ms)

    def scores(step):
        k_tile, qa, _, c, mask = step
        s = jnp.dot(k_tile, qa, preferred_element_type=F32)
        if mask is not None:
            s = jnp.where(mask, s, NEG)
        return s, jnp.max(s, axis=0, keepdims=True)

    nxt = scores(steps[0])
    for idx, (_, _, vt, c, _) in enumerate(steps):
        sl = slice(c * Q_STRIP, (c + 1) * Q_STRIP)
        s, s_max = nxt
        if idx + 1 < len(steps):
            nxt = scores(steps[idx + 1])
        m_new = jnp.maximum(ms[c], s_max)
        a = jnp.exp2(ms[c] - m_new)
        p = jnp.exp2((s - m_new).astype(BF16))
        ms[c] = m_new
        acc_ref[:, sl] = a * acc_ref[:, sl] + jnp.dot(vt, p, preferred_element_type=F32)
    return tuple(ms)


def _m_init(n_strips):
    return tuple(jnp.full((1, Q_STRIP), NEG, F32) for _ in range(n_strips))


def _with_ref_rows(qa, m):
    hi, mid, lo = _split3(-m)
    r = _row_iota((LANE - REF_SLAB, Q_STRIP)) + REF_SLAB
    slab = jnp.where(r == REF_ROW, hi, jnp.where(r == REF_ROW + 1, mid, jnp.where(r == REF_ROW + 2, lo, 0.0)))
    return jnp.concatenate([qa[:REF_SLAB], slab.astype(BF16)], axis=0)


def _first_tile_max(k_tile, qa_strips, masks):
    return tuple(jnp.max(jnp.where(mask, jnp.dot(k_tile, qa, preferred_element_type=F32), NEG), axis=0, keepdims=True)
                 for qa, mask in zip(qa_strips, masks))


def _fast_steps(steps, state, acc_ref):
    state = list(state)
    for a, b in zip(steps[:-1], steps[1:]):
        assert a[3] != b[3]

    def scores(step):
        k_tile, qa, _, c, mask = step
        s = jnp.dot(k_tile, _with_ref_rows(qa, state[c][0]), preferred_element_type=F32)
        if mask is not None:
            s = jnp.where(mask, s, NEG)
        return s

    s_next = scores(steps[0])
    for idx, (_, _, vt, c, _) in enumerate(steps):
        sl = slice(c * Q_STRIP, (c + 1) * Q_STRIP)
        s = s_next
        m, worst = state[c]
        cm = jnp.max(s, axis=0, keepdims=True)
        inc = jnp.maximum(cm, 0.0)
        state[c] = (m + inc, jnp.maximum(worst, cm))
        if idx + 1 < len(steps):
            s_next = scores(steps[idx + 1])
        p = jnp.exp2(s).astype(BF16)
        acc_ref[:, sl] = jnp.exp2(-inc) * (acc_ref[:, sl] + jnp.dot(vt, p, preferred_element_type=F32))
    return tuple(state)


def _flat(state):
    return tuple(x for pair in state for x in pair)


def _nest(flat):
    return tuple((flat[2 * c], flat[2 * c + 1]) for c in range(len(flat) // 2))


def _nsa_kernel(qt_ref, kc_ref, vct_ref, ks_ref, vst_ref, kw_ref, vwt_ref, gate_ref, ovt_ref, o_ref,
                qaug_ref, acc_ref, *, n_sel):
    i = pl.program_id(1)
    tq = Q_TILE_NSA
    cols = NSA_HPG * tq
    qs = i * tq
    nc = kc_ref.shape[1]
    nslc = ovt_ref.shape[0]
    n_super = nslc // SUPER_BLOCKS

    qt = jnp.concatenate([qt_ref[h * LANE:(h + 1) * LANE, :] for h in range(NSA_HPG)], axis=1)
    tq_row = qs + (_lane_iota((1, cols)) % tq)

    s = jnp.dot(kc_ref[0], qt, preferred_element_type=F32)
    cmp_end = _row_iota((nc, 1)) * CMP_STRIDE + (CMP_LEN - 1)
    p = _masked_softmax_t(s, cmp_end <= tq_row)
    o_cmp = jnp.dot(vct_ref[0], p.astype(BF16), preferred_element_type=F32)

    psum = p[:, 0:tq]
    for h in range(1, NSA_HPG):
        psum = psum + p[:, h * tq:(h + 1) * tq]
    p_hi = psum.astype(BF16)
    p_lo = (psum - p_hi.astype(F32)).astype(BF16)
    ovt = ovt_ref[...]
    imp = (jnp.dot(ovt, p_hi, preferred_element_type=F32)
           + jnp.dot(ovt, p_lo, preferred_element_type=F32))

    jj = _row_iota((nslc, tq))
    tq_blk = qs + _lane_iota((nslc, tq))
    cur = tq_blk // SLC_BLOCK
    forced = (jj == 0) | (jj == cur) | (jj == cur - 1)
    causal_blk = jj * SLC_BLOCK <= tq_blk
    val = jnp.where(forced, imp + BIG, imp)
    val = jnp.where(causal_blk, val, NEG)

    jjf = jj.astype(F32)

    def pick(_, carry):
        val, sel = carry
        mx = jnp.max(val, axis=0, keepdims=True)
        idx = jnp.min(jnp.where(val == mx, jjf, float(nslc)), axis=0, keepdims=True)
        hit = jjf == idx
        return jnp.where(hit, -jnp.inf, val), jnp.where(hit, 1.0, sel)

    _, sel = lax.fori_loop(0, n_sel, pick, (val, jnp.zeros((nslc, tq), F32)))
    bias_t = jnp.where((sel > 0.0) & causal_blk, 0.0, NEG)

    q_rows = qt[:HEAD_DIM].astype(F32)
    spare = jnp.zeros((LANE - HEAD_DIM - SUPER_BLOCKS, cols), F32)
    for st in range(n_super):
        b = bias_t[st * SUPER_BLOCKS:(st + 1) * SUPER_BLOCKS]
        b = jnp.concatenate([b] * NSA_HPG, axis=1)
        qaug_ref[st] = jnp.concatenate([q_rows, b, spare], axis=0).astype(BF16)

    tk = KV_TILE
    per_super = SUPER_BLOCKS * SLC_BLOCK // tk
    j_last = (qs + tq - 1) // tk
    n_strips = cols // Q_STRIP
    strips = [slice(c * Q_STRIP, (c + 1) * Q_STRIP) for c in range(n_strips)]

    def causal_masks(j):
        kpos = j * tk + _row_iota((tk, 1))
        return [kpos <= tq_row[:, sl] for sl in strips]

    def slc_steps(j, masks=None):
        k0 = pl.multiple_of(j * tk, tk)
        k_tile, vt, st = ks_ref[pl.ds(k0, tk), :], vst_ref[0, j], j // per_super
        return [(k_tile, qaug_ref[st, :, strips[c]], vt, c, None if masks is None else masks[c])
                for c in range(n_strips)]

    def pair(jj, flat):
        return _flat(_fast_steps(slc_steps(2 * jj) + slc_steps(2 * jj + 1), _nest(flat), acc_ref))

    def single(j, flat):
        return _flat(_fast_steps(slc_steps(j), _nest(flat), acc_ref))

    acc_ref[...] = jnp.zeros(acc_ref.shape, F32)
    m0 = _first_tile_max(ks_ref[0:tk, :], [qaug_ref[0, :, sl] for sl in strips], causal_masks(0))
    n_pairs = j_last // 2
    flat = lax.fori_loop(0, n_pairs, pair, _flat(tuple((m, jnp.zeros_like(m)) for m in m0)))
    flat = lax.fori_loop(2 * n_pairs, j_last, single, flat)
    state = _fast_steps(slc_steps(j_last, causal_masks(j_last)), _nest(flat), acc_ref)
    worst = jnp.max(jnp.concatenate([w for _, w in state], axis=1))

    @pl.when(worst > EXP_GUARD)
    def _():
        acc_ref[...] = jnp.zeros(acc_ref.shape, F32)
        lax.fori_loop(0, j_last + 1, lambda j, ms: _online_steps(slc_steps(j, causal_masks(j)), ms, acc_ref),
                      _m_init(n_strips))

    acc = acc_ref[...]
    o_slc = acc / jnp.maximum(acc[SUM_ROW:SUM_ROW + 1], 1e-30)

    wlen = WINDOW + tq
    ws = pl.multiple_of(jnp.maximum(qs - WINDOW, 0), tq)
    s = jnp.dot(kw_ref[pl.ds(ws, wlen), :], qt, preferred_element_type=F32)
    dist = tq_row - (ws + _row_iota((wlen, 1)))
    p = _masked_softmax_t(s, (dist >= 0) & (dist < WINDOW)).astype(BF16)
    wb = ws // LANE
    o_win = jnp.zeros((LANE, cols), F32)
    for c in range(wlen // LANE):
        o_win = o_win + jnp.dot(vwt_ref[0, wb + c], p[c * LANE:(c + 1) * LANE], preferred_element_type=F32)

    gate = gate_ref[0]
    for h in range(NSA_HPG):
        sl = slice(h * tq, (h + 1) * tq)
        o = (gate[3 * h:3 * h + 1] * o_cmp[:, sl] + gate[3 * h + 1:3 * h + 2] * o_slc[:, sl]
             + gate[3 * h + 2:3 * h + 3] * o_win[:, sl])
        o_ref[:, h * LANE:(h + 1) * LANE] = o.T.astype(BF16)


def _overlap_t(s, nslc_pad):
    nc = s // CMP_STRIDE
    cmp_start = np.arange(nc) * CMP_STRIDE
    slc_start = np.arange(nslc_pad) * SLC_BLOCK
    ov = np.clip(np.minimum(cmp_start[:, None] + CMP_LEN, slc_start[None, :] + SLC_BLOCK)
                 - np.maximum(cmp_start[:, None], slc_start[None, :]), 0, None) / CMP_STRIDE
    ov[nc - CMP_LEN // CMP_STRIDE + 1:, :] = 0.0
    ov[:, s // SLC_BLOCK:] = 0.0
    return jnp.asarray(ov.T, BF16)


def _nsa_attention(qnt, kc, vct, ks, vst, kw, vwt, gates):
    s = qnt.shape[1]
    nc = s // CMP_STRIDE
    n_slc = s // SLC_BLOCK
    nslc_pad = -(-n_slc // LANE) * LANE
    tq = Q_TILE_NSA
    cols = NSA_HPG * tq
    once = pl.Buffered(1)
    res = pl.BlockSpec((s, LANE), lambda g, i: (0, g), pipeline_mode=once)
    return pl.pallas_call(
        functools.partial(_nsa_kernel, n_sel=min(SLC_TOPK, n_slc)),
        grid=(NSA_GROUPS, s // tq),
        in_specs=[pl.BlockSpec((NSA_HPG * LANE, tq), lambda g, i: (g, i)),
                  pl.BlockSpec((1, nc, LANE), lambda g, i: (g, 0, 0), pipeline_mode=once),
                  pl.BlockSpec((1, LANE, nc), lambda g, i: (g, 0, 0), pipeline_mode=once),
                  res, pl.BlockSpec((1, s // KV_TILE, LANE, KV_TILE), lambda g, i: (g, 0, 0, 0),
                                    pipeline_mode=once),
                  res, pl.BlockSpec((1, s // LANE, LANE, LANE), lambda g, i: (g, 0, 0, 0), pipeline_mode=once),
                  pl.BlockSpec((1, GATE_ROWS, tq), lambda g, i: (g, 0, i)),
                  pl.BlockSpec((nslc_pad, nc), lambda g, i: (0, 0), pipeline_mode=once)],
        out_specs=pl.BlockSpec((tq, NSA_HPG * LANE), lambda g, i: (i, g)),
        out_shape=jax.ShapeDtypeStruct((s, NSA_HEADS * LANE), BF16),
        scratch_shapes=[pltpu.VMEM((nslc_pad // SUPER_BLOCKS, LANE, cols), BF16),
                        pltpu.VMEM((LANE, cols), F32)],
        compiler_params=_cparams("parallel", "arbitrary"),
        name="nsa_attention",
    )(qnt, kc, vct, ks, vst, kw, vwt, gates, _overlap_t(s, nslc_pad))


def _flash_kernel(qt_ref, k_ref, vt_ref, o_ref, acc_ref):
    i = pl.program_id(1)
    tq, tk = Q_TILE_FLASH, KV_TILE
    acc_ref[...] = jnp.zeros(acc_ref.shape, F32)

    n_strips = tq // Q_STRIP
    per_q = tq // tk
    qas = [qt_ref[:, c * Q_STRIP:(c + 1) * Q_STRIP] for c in range(n_strips)]

    def tile_steps(j, d=None):
        k0 = pl.multiple_of(j * tk, tk)
        k_tile, vt = k_ref[pl.ds(k0, tk), :], vt_ref[0, j]
        steps = []
        for c in range(n_strips):
            mask = None
            if d is not None:
                if d * tk > (c + 1) * Q_STRIP - 1:
                    continue
                if (d + 1) * tk - 1 > c * Q_STRIP:
                    shp = (tk, Q_STRIP)
                    mask = _row_iota(shp) + d * tk <= _lane_iota(shp) + c * Q_STRIP
            steps.append((k_tile, qas[c], vt, c, mask))
        return steps

    def any_tile_masks(j):
        shp = (tk, Q_STRIP)
        return [_row_iota(shp) + j * tk <= _lane_iota(shp) + (i * tq + c * Q_STRIP) for c in range(n_strips)]

    def below(t, flat):
        jj = i - 1 - t
        steps = [st for u in range(per_q) for st in tile_steps(jj * per_q + (per_q - 1 - u))]
        return _flat(_fast_steps(steps, _nest(flat), acc_ref))

    diag0 = i * per_q
    k_diag = k_ref[pl.ds(pl.multiple_of(diag0 * tk, tk), tk), :]
    m0 = _first_tile_max(k_diag, qas, any_tile_masks(diag0))
    steps = [st for d in range(per_q) for st in tile_steps(diag0 + d, d)]
    state = _fast_steps(steps, tuple((m, jnp.zeros_like(m)) for m in m0), acc_ref)
    state = _nest(lax.fori_loop(0, i, below, _flat(state)))
    worst = jnp.max(jnp.concatenate([w for _, w in state], axis=1))

    @pl.when(worst > EXP_GUARD)
    def _():
        acc_ref[...] = jnp.zeros(acc_ref.shape, F32)

        def exact(j, ms):
            k0 = pl.multiple_of(j * tk, tk)
            k_tile, vt, masks = k_ref[pl.ds(k0, tk), :], vt_ref[0, j], any_tile_masks(j)
            return _online_steps([(k_tile, qas[c], vt, c, masks[c]) for c in range(n_strips)], ms, acc_ref)

        lax.fori_loop(0, (i + 1) * per_q, exact, _m_init(n_strips))

    acc = acc_ref[...]
    o = acc / acc[SUM_ROW:SUM_ROW + 1]
    for c0 in range(0, tq, LANE):
        o_ref[c0:c0 + LANE, :] = o[:, c0:c0 + LANE].T.astype(BF16)


def _causal_attention(qt, k, vt):
    s, width = k.shape
    heads = width // LANE
    assert Q_TILE_FLASH % KV_TILE == 0 and s % Q_TILE_FLASH == 0
    return pl.pallas_call(
        _flash_kernel,
        grid=(heads, s // Q_TILE_FLASH),
        in_specs=[pl.BlockSpec((LANE, Q_TILE_FLASH), lambda h, i: (h, i)),
                  pl.BlockSpec((s, LANE), lambda h, i: (0, h)),
                  pl.BlockSpec((1, s // KV_TILE, LANE, KV_TILE), lambda h, i: (h, 0, 0, 0))],
        out_specs=pl.BlockSpec((Q_TILE_FLASH, LANE), lambda h, i: (i, h)),
        out_shape=jax.ShapeDtypeStruct((s, width), BF16),
        scratch_shapes=[pltpu.VMEM((LANE, Q_TILE_FLASH), F32)],
        compiler_params=_cparams("parallel", "arbitrary"),
        name="causal_attention",
    )(qt, k, vt)


def _out_proj_kernel(oa_ref, ob_ref, wa_ref, wb_ref, x_ref, g_ref, o_ref):
    y = jnp.dot(oa_ref[...], wa_ref[...], preferred_element_type=F32)
    y = y + jnp.dot(ob_ref[...], wb_ref[...], preferred_element_type=F32)
    o_ref[...] = x_ref[...] + g_ref[...] * y


def _out_proj(oa, ob, cola, colb, wa, wb, x2, gate):
    s, d = x2.shape
    ka = wa.shape[0]
    tm = ROW_TILE
    return pl.pallas_call(
        _out_proj_kernel,
        grid=(s // tm,),
        in_specs=[pl.BlockSpec((tm, ka), lambda i: (i, cola)), pl.BlockSpec((tm, ka), lambda i: (i, colb)),
                  pl.BlockSpec((ka, d), lambda i: (0, 0)), pl.BlockSpec((ka, d), lambda i: (0, 0)),
                  pl.BlockSpec((tm, d), lambda i: (i, 0)), pl.BlockSpec((1, d), lambda i: (0, 0))],
        out_specs=pl.BlockSpec((tm, d), lambda i: (i, 0)),
        out_shape=jax.ShapeDtypeStruct((s, d), F32),
        compiler_params=_cparams("parallel"),
        name="out_proj",
    )(oa, ob, wa, wb, x2, gate)


def _mla_prep_kernel(p_ref, pos_ref, inv_ref, gqa_ref, gkva_ref, wuq_ref, wuk_ref, wuv_ref,
                     gq_ref, gk_ref, gkr_ref, qt_ref, k_ref, vt_ref):
    shp = (PREP_TILE, LANE)
    lane = _lane_iota(shp)
    nope = lane < QK_NOPE
    rope = (lane >= QK_NOPE) & (lane < QK_NOPE + QK_ROPE)
    ref_ones = jnp.where((lane >= REF_ROW) & (lane < REF_ROW + 3), 1.0, 0.0)
    ang = pos_ref[...] * inv_ref[...]
    cos = jnp.where(rope, jnp.cos(ang), 1.0)
    sin = jnp.where(rope, jnp.sin(ang), 0.0)

    def rope32(x):
        half = QK_ROPE // 2
        rot = jnp.where(lane < QK_NOPE + half, -pltpu.roll(x, LANE - half, 1), pltpu.roll(x, half, 1))
        return x * cos + rot * sin

    def low_rank_norm(x, g):
        ms = jnp.mean(x * x, axis=-1, keepdims=True)
        return (x * lax.rsqrt(ms + EPS) * g).astype(BF16)

    nq = Q_LORA // LANE
    cq = low_rank_norm(p_ref[:, :Q_LORA], gqa_ref[...])
    ckv = low_rank_norm(p_ref[:, Q_LORA:Q_LORA + KV_LORA], gkva_ref[...])
    kr = p_ref[:, (nq + KV_LORA // LANE) * LANE:(nq + KV_LORA // LANE + 1) * LANE]
    k_rope = rope32(_head_rms(kr, gkr_ref[...], QK_ROPE))

    gq, gk = gq_ref[...], gk_ref[...]
    scale = (QK_NOPE + QK_ROPE) ** -0.5 * LOG2E
    pair = 2 * LANE
    for hp in range(MLA_HEADS // 2):
        cols = slice(hp * pair, (hp + 1) * pair)
        q2 = jnp.dot(cq, wuq_ref[:, cols], preferred_element_type=F32)
        k2 = jnp.dot(ckv, wuk_ref[:, cols], preferred_element_type=F32)
        v2 = jnp.dot(ckv, wuv_ref[:, cols], preferred_element_type=F32)
        for sub in range(2):
            head = 2 * hp + sub
            sl = slice(head * LANE, (head + 1) * LANE)
            half = slice(sub * LANE, (sub + 1) * LANE)
            x = q2[:, half]
            ss_n = jnp.sum(jnp.where(nope, x * x, 0.0), axis=-1, keepdims=True)
            ss_r = jnp.sum(jnp.where(rope, x * x, 0.0), axis=-1, keepdims=True)
            inv_rms = jnp.where(nope, lax.rsqrt(ss_n * (1.0 / QK_NOPE) + EPS),
                                lax.rsqrt(ss_r * (1.0 / QK_ROPE) + EPS))
            qt_ref[sl, :] = (rope32(x * inv_rms * gq) * scale).T.astype(BF16)
            kn = _head_rms(k2[:, half], gk, QK_NOPE)
            k_ref[:, sl] = (kn + k_rope + ref_ones).astype(BF16)
            vt_ref[head, 0] = jnp.where(lane == SUM_ROW, 1.0, v2[:, half]).T.astype(BF16)


def _mla_prep(proj, posf, inv128, gqa, gkva, wuq, wuk, wuv, gq, gk, gkr):
    s, n = proj.shape
    tm = PREP_TILE
    assert tm == KV_TILE

    def full(a):
        return pl.BlockSpec(a.shape, lambda i: (0, 0))

    outs = [_feat_major(MLA_HEADS, s, tm),
            (pl.BlockSpec((tm, MLA_HEADS * LANE), lambda i: (i, 0)),
             jax.ShapeDtypeStruct((s, MLA_HEADS * LANE), BF16)),
            _value_tiles(MLA_HEADS, s, tm, KV_TILE)]
    args = (inv128, gqa, gkva, wuq, wuk, wuv, gq, gk, gkr)
    return pl.pallas_call(
        _mla_prep_kernel,
        grid=(s // tm,),
        in_specs=[pl.BlockSpec((tm, n), lambda i: (i, 0)), pl.BlockSpec((tm, 1), lambda i: (i, 0))]
                 + [full(a) for a in args],
        out_specs=[o[0] for o in outs],
        out_shape=[o[1] for o in outs],
        compiler_params=_cparams("parallel"),
        name="mla_prep",
    )(proj, posf, *args)


def _rank_lt(v, k):
    n = v.shape[0]
    row = _row_iota(v.shape)
    rank = jnp.zeros(v.shape, F32)
    for b in range(n):
        vb = v[b:b + 1, :]
        rank = rank + jnp.where((vb > v) | ((vb == v) & (row > b)), 1.0, 0.0)
    return rank < k


def _moe_route_kernel(x_ref, g_ref, sc_ref, sh_ref, wr_ref, rb_ref, h_ref, pos_ref, wt_ref, cnt_ref):
    tm = ROW_TILE
    h = _norm_mod(x_ref[...], g_ref[...], sc_ref[...], sh_ref[...])
    h_ref[...] = h.astype(BF16)
    logits = jnp.dot(h, wr_ref[...], precision=HIGHEST, preferred_element_type=F32)
    lt = logits.T[:N_EXPERTS]
    scores = jax.nn.sigmoid(lt)
    sel = scores + rb_ref[...]

    per = N_EXPERTS // N_GROUPS
    grp = sel.reshape(N_GROUPS, per, tm)
    sub = lax.broadcasted_iota(jnp.int32, grp.shape, 1)
    m1 = jnp.max(grp, axis=1, keepdims=True)
    first = jnp.min(jnp.where(grp == m1, sub, per), axis=1, keepdims=True)
    m2 = jnp.max(jnp.where(sub == first, -jnp.inf, grp), axis=1, keepdims=True)
    gscore = (m1 + m2).reshape(N_GROUPS, tm)
    gmask = _rank_lt(gscore, TOPK_GROUPS)
    emask = jnp.broadcast_to(gmask.reshape(N_GROUPS, 1, tm), grp.shape).reshape(N_EXPERTS, tm)
    chosen = _rank_lt(jnp.where(emask, sel, NEG), TOP_K)

    w = jnp.where(chosen, scores, 0.0)
    wt_ref[...] = w / jnp.sum(w, axis=0, keepdims=True) * ROUTED_SCALE

    upper = jnp.where(_row_iota((tm, tm)) <= _lane_iota((tm, tm)), 1.0, 0.0).astype(BF16)
    incl = jnp.dot(jnp.where(chosen, 1.0, 0.0).astype(BF16), upper, preferred_element_type=F32)
    pos_ref[...] = jnp.where(chosen, incl - 1.0, -1.0)
    cnt_ref[0] = jnp.broadcast_to(incl[:, tm - 1:tm], (N_EXPERTS, LANE))


def _moe_route(x2, g, sc, sh, w_router_pad, router_bias_col):
    s, d = x2.shape
    tm = ROW_TILE
    vec = pl.BlockSpec((1, d), lambda i: (0, 0))
    et = pl.BlockSpec((N_EXPERTS, tm), lambda i: (0, i))
    return pl.pallas_call(
        _moe_route_kernel,
        grid=(s // tm,),
        in_specs=[pl.BlockSpec((tm, d), lambda i: (i, 0)), vec, vec, vec,
                  pl.BlockSpec((d, LANE), lambda i: (0, 0)),
                  pl.BlockSpec((N_EXPERTS, 1), lambda i: (0, 0))],
        out_specs=[pl.BlockSpec((tm, d), lambda i: (i, 0)), et, et,
                   pl.BlockSpec((1, N_EXPERTS, LANE), lambda i: (i, 0, 0))],
        out_shape=[jax.ShapeDtypeStruct((s, d), BF16), jax.ShapeDtypeStruct((N_EXPERTS, s), F32),
                   jax.ShapeDtypeStruct((N_EXPERTS, s), F32),
                   jax.ShapeDtypeStruct((s // tm, N_EXPERTS, LANE), F32)],
        compiler_params=_cparams("parallel"),
        name="moe_route",
    )(x2, g, sc, sh, w_router_pad, router_bias_col)


def _moe_kernel(cnt_ref, x_ref, h_ref, pos_ref, wt_ref, wg_ref, wu_ref, wd_ref, sg_ref, su_ref, sd_ref,
                g2_ref, o_ref, acc_ref):
    i = pl.program_id(0)
    e = pl.program_id(1)
    tm = ROW_TILE
    r = MOE_CHUNK

    @pl.when(e == 0)
    def _():
        h = h_ref[...]
        a = jnp.dot(h, sg_ref[...], preferred_element_type=F32)
        a = a * jax.nn.sigmoid(a) * jnp.dot(h, su_ref[...], preferred_element_type=F32)
        acc_ref[...] = jnp.dot(a.astype(BF16), sd_ref[...], preferred_element_type=F32)

    first = e * MOE_EXPERTS_PER_STEP
    n = cnt_ref[i * N_EXPERTS + first]
    for k in range(1, MOE_EXPERTS_PER_STEP):
        n = jnp.maximum(n, cnt_ref[i * N_EXPERTS + first + k])
    prows = [pos_ref[pl.ds(first + k, 1), :] for k in range(MOE_EXPERTS_PER_STEP)]
    wrows = [wt_ref[pl.ds(first + k, 1), :] for k in range(MOE_EXPERTS_PER_STEP)]

    def chunk(c, _):
        slot = (_row_iota((r, tm)) + c * r).astype(F32)
        hits = [prow == slot for prow in prows]
        onehot = jnp.concatenate([jnp.where(hit, 1.0, 0.0).astype(BF16) for hit in hits], axis=0)
        xg = jnp.dot(onehot, h_ref[...], preferred_element_type=F32).astype(BF16)
        ys = []
        for k in range(MOE_EXPERTS_PER_STEP):
            xk = xg[k * r:(k + 1) * r]
            a = jnp.dot(xk, wg_ref[k], preferred_element_type=F32)
            a = a * jax.nn.sigmoid(a) * jnp.dot(xk, wu_ref[k], preferred_element_type=F32)
            y = jnp.dot(a.astype(BF16), wd_ref[k], preferred_element_type=F32)
            wr = jnp.sum(jnp.where(hits[k], wrows[k], 0.0), axis=-1, keepdims=True)
            ys.append((y * wr).astype(BF16))
        acc_ref[...] += _dot_tn(onehot, jnp.concatenate(ys, axis=0))
        return 0

    lax.fori_loop(0, (n + r - 1) // r, chunk, 0)

    @pl.when(e == N_EXPERTS // MOE_EXPERTS_PER_STEP - 1)
    def _():
        o_ref[...] = x_ref[...] + g2_ref[...] * acc_ref[...]


def _moe_experts(counts, x2, h, pos_t, w_t, wg, wu, wd, sg, su, sd, g2):
    s, d = x2.shape
    tm = ROW_TILE
    ff = wg.shape[2]
    tile = pl.BlockSpec((tm, d), lambda i, e, c: (i, 0))
    et = pl.BlockSpec((N_EXPERTS, tm), lambda i, e, c: (0, i))

    def const(a):
        return pl.BlockSpec(a.shape, lambda i, e, c: (0,) * a.ndim)

    per = MOE_EXPERTS_PER_STEP
    grid_spec = pltpu.PrefetchScalarGridSpec(
        num_scalar_prefetch=1,
        grid=(s // tm, N_EXPERTS // per),
        in_specs=[tile, tile, et, et,
                  pl.BlockSpec((per, d, ff), lambda i, e, c: (e, 0, 0)),
                  pl.BlockSpec((per, d, ff), lambda i, e, c: (e, 0, 0)),
                  pl.BlockSpec((per, ff, d), lambda i, e, c: (e, 0, 0)),
                  const(sg), const(su), const(sd), const(g2)],
        out_specs=tile,
        scratch_shapes=[pltpu.VMEM((tm, d), F32)],
    )
    return pl.pallas_call(
        _moe_kernel,
        grid_spec=grid_spec,
        out_shape=jax.ShapeDtypeStruct((s, d), F32),
        compiler_params=_cparams("parallel", "arbitrary"),
        name="moe_experts",
    )(counts, x2, h, pos_t, w_t, wg, wu, wd, sg, su, sd, g2)


def _pad_lanes(v, width=LANE, offset=0):
    out = jnp.zeros((1, width), F32)
    return out.at[0, offset:offset + v.shape[0]].set(v.astype(F32))


def _head_cols(w, n_heads, dim):
    d = w.shape[0]
    w3 = w.reshape(d, n_heads, dim)
    return jnp.pad(w3, ((0, 0), (0, 0), (0, LANE - dim))).reshape(d, n_heads * LANE)


def _hybrid_w_in(w_in):
    d = w_in.shape[0]
    nq = NSA_HEADS * HEAD_DIM
    nkv = 6 * NSA_GROUPS * HEAD_DIM
    ng = 3 * NSA_HEADS
    nf = 3 * FOX_HEADS * HEAD_DIM
    c0, c1, c2, c3 = nq, nq + nkv, nq + nkv + ng, nq + nkv + ng + nf
    gates = w_in[:, c1:c2].reshape(d, NSA_GROUPS, 3 * NSA_HPG)
    gates = jnp.pad(gates, ((0, 0), (0, 0), (0, LANE - 3 * NSA_HPG))).reshape(d, NSA_GROUPS * LANE)
    ff = jnp.pad(w_in[:, c3:], ((0, 0), (0, 2 * LANE - FOX_HEADS)))
    return jnp.concatenate([
        _head_cols(w_in[:, :c0], NSA_HEADS, HEAD_DIM),
        _head_cols(w_in[:, c0:c1], 6 * NSA_GROUPS, HEAD_DIM),
        _head_cols(w_in[:, c2:c3], 3 * FOX_HEADS, HEAD_DIM),
        gates, ff], axis=1).astype(BF16)


def _pad_head_rows(w, n_heads, dim):
    d = w.shape[1]
    w3 = w.reshape(n_heads, dim, d)
    return jnp.pad(w3, ((0, 0), (0, LANE - dim), (0, 0))).reshape(n_heads * LANE, d).astype(BF16)


def _rope_inv(dim, offset):
    inv = ROPE_THETA ** (-jnp.arange(0, dim, 2, dtype=F32) / dim)
    return _pad_lanes(jnp.concatenate([inv, inv]), offset=offset)


def _hybrid_mixer(x2, posf, mods, norm_g, w_in, fox_f_bias, nsa_q_norm, nsa_k_norm, nsa_cmp_pe, nsa_w_cmp,
                  fox_q_norm, fox_k_norm, w_out):
    sh1, sc1, g1 = mods
    proj = _norm_mod_matmul(x2, norm_g, sc1, sh1, _hybrid_w_in(w_in), tn=HY_COL_TILE)
    (qnt, kct, vct, ks, vst, kw, vwt, gates, fqt, fk, fvt) = _hy_prep(
        proj, posf, _rope_inv(HEAD_DIM, 0), _pad_lanes(nsa_q_norm), _pad_lanes(nsa_k_norm),
        _pad_lanes(fox_q_norm), _pad_lanes(fox_k_norm), _pad_lanes(fox_f_bias))
    kc, vc_t = _compress(kct, vct, nsa_w_cmp, nsa_cmp_pe, nsa_k_norm)
    o_a = _nsa_attention(qnt, kc, vc_t, ks, vst, kw, vwt, gates)
    o_b = _causal_attention(fqt, fk, fvt)
    half = NSA_HEADS * HEAD_DIM
    wa = _pad_head_rows(w_out[:half], NSA_HEADS, HEAD_DIM)
    wb = _pad_head_rows(w_out[half:], FOX_HEADS, HEAD_DIM)
    return _out_proj(o_a, o_b, 0, 0, wa, wb, x2, g1)


def _mla_mixer(x2, posf, mods, norm_g, w_in, q_a_norm, kv_a_norm, w_uq, w_ukv, qn_norm, kn_norm, qr_norm,
               kr_norm, w_out):
    sh1, sc1, g1 = mods
    d = x2.shape[1]
    w_kr = jnp.zeros((d, LANE), F32).at[:, QK_NOPE:QK_NOPE + QK_ROPE].set(w_in[:, Q_LORA + KV_LORA:])
    w_in_p = jnp.concatenate([w_in[:, :Q_LORA + KV_LORA], w_kr], axis=1).astype(BF16)
    proj = _norm_mod_matmul(x2, norm_g, sc1, sh1, w_in_p, tn=w_in_p.shape[1])
    hq = QK_NOPE + QK_ROPE
    wuq = _head_cols(w_uq, MLA_HEADS, hq).astype(BF16)
    wkv3 = w_ukv.reshape(KV_LORA, MLA_HEADS, QK_NOPE + V_HEAD)
    wuk = _head_cols(wkv3[:, :, :QK_NOPE].reshape(KV_LORA, -1), MLA_HEADS, QK_NOPE).astype(BF16)
    wuv = _head_cols(wkv3[:, :, QK_NOPE:].reshape(KV_LORA, -1), MLA_HEADS, V_HEAD).astype(BF16)
    gq = _pad_lanes(jnp.concatenate([qn_norm, qr_norm]))
    qt, k, vt = _mla_prep(proj, posf, _rope_inv(QK_ROPE, QK_NOPE), q_a_norm.reshape(1, -1).astype(F32),
                          kv_a_norm.reshape(1, -1).astype(F32), wuq, wuk, wuv, gq, _pad_lanes(kn_norm),
                          _pad_lanes(kr_norm, offset=QK_NOPE))
    o = _causal_attention(qt, k, vt)
    w_pad = _pad_head_rows(w_out, MLA_HEADS, V_HEAD)
    half = w_pad.shape[0] // 2
    return _out_proj(o, o, 0, 1, w_pad[:half], w_pad[half:], x2, g1)


def _moe_ffn(x2, mods, norm_g, w_router, router_bias, w_gate, w_up, w_down, ws_gate, ws_up, ws_down):
    sh2, sc2, g2 = mods
    w_r = jnp.pad(w_router.astype(F32), ((0, 0), (0, LANE - N_EXPERTS)))
    h, pos_t, w_t, cnt = _moe_route(x2, norm_g, sc2, sh2, w_r, router_bias.reshape(N_EXPERTS, 1).astype(F32))
    counts = cnt[:, :, 0].astype(jnp.int32).reshape(-1)
    return _moe_experts(counts, x2, h, pos_t, w_t, w_gate.astype(BF16), w_up.astype(BF16),
                        w_down.astype(BF16), ws_gate.astype(BF16), ws_up.astype(BF16), ws_down.astype(BF16), g2)


def kernel(x, c, positions, norm_attn, norm_ffn, w_ada, b_ada, hy_w_in, fox_f_bias, nsa_q_norm, nsa_k_norm, nsa_cmp_pe, nsa_w_cmp, fox_q_norm, fox_k_norm, hy_w_out, mla_w_in, mla_q_a_norm, mla_kv_a_norm, mla_w_uq, mla_w_ukv, mla_qn_norm, mla_kn_norm, mla_qr_norm, mla_kr_norm, mla_w_out, moe_w_router, moe_router_bias, moe_w_gate, moe_w_up, moe_w_down, moe_ws_gate, moe_ws_up, moe_ws_down):
    b, s, d = x.shape
    assert b == 1 and s % KV_TILE == 0 and s >= WINDOW + Q_TILE_NSA
    depth = w_ada.shape[0]
    x2 = x.reshape(s, d).astype(F32)
    posf = positions.reshape(s, 1).astype(F32)
    mod = _ada_mod(c.astype(F32), w_ada.astype(F32), b_ada.astype(F32))

    for layer in range(depth):
        m = [mod[layer, :, k * d:(k + 1) * d] for k in range(6)]
        i = layer // 2
        g_attn = norm_attn[layer].reshape(1, d).astype(F32)
        if layer % 2 == 0:
            x2 = _hybrid_mixer(x2, posf, m[0:3], g_attn, hy_w_in[i], fox_f_bias[i], nsa_q_norm[i],
                               nsa_k_norm[i], nsa_cmp_pe[i], nsa_w_cmp[i], fox_q_norm[i], fox_k_norm[i],
                               hy_w_out[i])
        else:
            x2 = _mla_mixer(x2, posf, m[0:3], g_attn, mla_w_in[i], mla_q_a_norm[i], mla_kv_a_norm[i],
                            mla_w_uq[i], mla_w_ukv[i], mla_qn_norm[i], mla_kn_norm[i], mla_qr_norm[i],
                            mla_kr_norm[i], mla_w_out[i])
        x2 = _moe_ffn(x2, m[3:6], norm_ffn[layer].reshape(1, d).astype(F32), moe_w_router[layer],
                      moe_router_bias[layer], moe_w_gate[layer], moe_w_up[layer], moe_w_down[layer],
                      moe_ws_gate[layer], moe_ws_up[layer], moe_ws_down[layer])
    return x2.reshape(b, s, d)
```

```python
import functools

import numpy as np
import jax
import jax.numpy as jnp
from jax import lax
from jax.experimental import pallas as pl
from jax.experimental.pallas import tpu as pltpu

F32 = jnp.float32
BF16 = jnp.bfloat16
HIGHEST = lax.Precision.HIGHEST

LANE = 128
VMEM_LIMIT_BYTES = 56 * 1024 * 1024

HEAD_DIM = 64
NSA_HEADS = 8
NSA_GROUPS = 2
NSA_HPG = NSA_HEADS // NSA_GROUPS
CMP_LEN = 32
CMP_STRIDE = 16
SLC_BLOCK = 64
SLC_TOPK = 16
WINDOW = 512
FOX_HEADS = 8
MLA_HEADS = 16
Q_LORA = 384
KV_LORA = 256
QK_NOPE = 64
QK_ROPE = 32
V_HEAD = 64
N_EXPERTS = 64
TOP_K = 8
N_GROUPS = 8
TOPK_GROUPS = 4
EXPERT_FF = 256
ROUTED_SCALE = 2.5
ROPE_THETA = 10000.0
EPS = 1e-6
NEG = -1e30
BIG = 1e6

ROW_TILE = 512
PREP_TILE = 512
GATE_ROWS = 16
Q_TILE_NSA = 256
KV_TILE = 512
Q_TILE_FLASH = 1024
Q_STRIP = 256
SUPER_BLOCKS = 32
SUM_ROW = 64
REF_ROW = 104
REF_SLAB = 96
EXP_GUARD = 100.0
LOG2E = 1.4426950408889634
MOE_CHUNK = 128
MOE_EXPERTS_PER_STEP = 4

HY_Q0 = 0
HY_KV0 = HY_Q0 + NSA_HEADS
HY_F0 = HY_KV0 + 6 * NSA_GROUPS
HY_G0 = HY_F0 + 3 * FOX_HEADS
HY_FF = HY_G0 + NSA_GROUPS
HY_BLOCKS = HY_FF + 2
HY_COL_TILE = 12 * LANE
assert (HY_BLOCKS * LANE) % HY_COL_TILE == 0


def _cparams(*sem):
    return pltpu.CompilerParams(dimension_semantics=sem, vmem_limit_bytes=VMEM_LIMIT_BYTES)


def _lane_iota(shape):
    return lax.broadcasted_iota(jnp.int32, shape, len(shape) - 1)


def _row_iota(shape):
    return lax.broadcasted_iota(jnp.int32, shape, len(shape) - 2)


def _dot_nt(a, b):
    return lax.dot_general(a, b, (((1,), (1,)), ((), ())), preferred_element_type=F32)


def _dot_tn(a, b):
    return lax.dot_general(a, b, (((0,), (0,)), ((), ())), preferred_element_type=F32)


def _ada_kernel(c_ref, w_ref, b_ref, o_ref):
    c = c_ref[...]
    cond = c * jax.nn.sigmoid(c)
    o_ref[0] = jnp.dot(cond, w_ref[0], precision=HIGHEST, preferred_element_type=F32) + b_ref[0]


def _ada_mod(c, w_ada, b_ada):
    depth, d, n = w_ada.shape
    tn = 768
    c8 = jnp.broadcast_to(c.reshape(1, d), (8, d))
    out = pl.pallas_call(
        _ada_kernel,
        grid=(depth, n // tn),
        in_specs=[pl.BlockSpec((8, d), lambda l, j: (0, 0)),
                  pl.BlockSpec((1, d, tn), lambda l, j: (l, 0, j)),
                  pl.BlockSpec((1, 1, tn), lambda l, j: (l, 0, j))],
        out_specs=pl.BlockSpec((1, 8, tn), lambda l, j: (l, 0, j)),
        out_shape=jax.ShapeDtypeStruct((depth, 8, n), F32),
        compiler_params=_cparams("parallel", "parallel"),
        name="ada_mod",
    )(c8, w_ada, b_ada.reshape(depth, 1, n))
    return out[:, 0:1, :]


def _norm_mod(x, g, sc, sh):
    ms = jnp.mean(x * x, axis=-1, keepdims=True)
    return (x * lax.rsqrt(ms + EPS) * g) * (1.0 + sc) + sh


def _nmm_kernel(x_ref, g_ref, sc_ref, sh_ref, w_ref, o_ref, h_scr):
    @pl.when(pl.program_id(1) == 0)
    def _():
        h_scr[...] = _norm_mod(x_ref[...], g_ref[...], sc_ref[...], sh_ref[...]).astype(BF16)

    o_ref[...] = jnp.dot(h_scr[...], w_ref[...], preferred_element_type=F32)


def _norm_mod_matmul(x2, g, sc, sh, w, tn):
    s, d = x2.shape
    n = w.shape[1]
    vec = pl.BlockSpec((1, d), lambda i, j: (0, 0))
    return pl.pallas_call(
        _nmm_kernel,
        grid=(s // ROW_TILE, n // tn),
        in_specs=[pl.BlockSpec((ROW_TILE, d), lambda i, j: (i, 0)), vec, vec, vec,
                  pl.BlockSpec((d, tn), lambda i, j: (0, j))],
        out_specs=pl.BlockSpec((ROW_TILE, tn), lambda i, j: (i, j)),
        out_shape=jax.ShapeDtypeStruct((s, n), F32),
        scratch_shapes=[pltpu.VMEM((ROW_TILE, d), BF16)],
        compiler_params=_cparams("parallel", "arbitrary"),
        name="norm_mod_matmul",
    )(x2, g, sc, sh, w)


def _head_rms(x, gain, n_real):
    ss = jnp.sum(x * x, axis=-1, keepdims=True)
    return x * lax.rsqrt(ss * (1.0 / n_real) + EPS) * gain


def _rope64(x, cos, sin):
    lane = _lane_iota(x.shape)
    rot = jnp.where(lane < 32, -pltpu.roll(x, LANE - 32, 1), pltpu.roll(x, 32, 1))
    return x * cos + rot * sin


def _split3(c):
    hi = c.astype(BF16).astype(F32)
    r1 = c - hi
    mid = r1.astype(BF16).astype(F32)
    lo = (r1 - mid).astype(BF16).astype(F32)
    return hi, mid, lo


def _hy_prep_kernel(p_ref, pos_ref, inv_ref, gq_ref, gk_ref, gfq_ref, gfk_ref, fb_ref,
                    qnt_ref, kct_ref, vct_ref, ks_ref, vst_ref, kw_ref, vwt_ref, gate_ref,
                    fqt_ref, fk_ref, fvt_ref, carry_ref):
    i = pl.program_id(0)
    tm = PREP_TILE
    shp = (tm, LANE)
    lane = _lane_iota(shp)

    def blk(b):
        return p_ref[:, b * LANE:(b + 1) * LANE]

    ang = pos_ref[...] * inv_ref[...]
    real = lane < HEAD_DIM
    cos = jnp.where(real, jnp.cos(ang), 1.0)
    sin = jnp.where(real, jnp.sin(ang), 0.0)
    gq, gk, gfq, gfk = gq_ref[...], gk_ref[...], gfq_ref[...], gfk_ref[...]
    scale = HEAD_DIM ** -0.5 * LOG2E
    ones_row = lane == SUM_ROW
    ref_ones = jnp.where((lane >= REF_ROW) & (lane < REF_ROW + 3), 1.0, 0.0)

    for h in range(NSA_HEADS):
        q = _rope64(_head_rms(blk(HY_Q0 + h), gq, HEAD_DIM), cos, sin) * scale
        qnt_ref[h * LANE:(h + 1) * LANE, :] = q.T.astype(BF16)

    row = _row_iota(shp) + i * tm
    onehot = jnp.where(lane - HEAD_DIM == ((row // SLC_BLOCK) % SUPER_BLOCKS), 1.0, 0.0)
    for g in range(NSA_GROUPS):
        def kv(r):
            return blk(HY_KV0 + r * NSA_GROUPS + g)
        sl = slice(g * LANE, (g + 1) * LANE)
        kct_ref[g] = _rope64(kv(0), cos, sin)[:, :HEAD_DIM].astype(BF16)
        vct_ref[g] = kv(1)[:, :HEAD_DIM].astype(BF16)
        ks = _rope64(_head_rms(kv(2), gk, HEAD_DIM), cos, sin)
        ks_ref[:, sl] = (ks + onehot + ref_ones).astype(BF16)
        vst_ref[g, 0] = jnp.where(ones_row, 1.0, kv(3)).T.astype(BF16)
        kw_ref[:, sl] = _rope64(_head_rms(kv(4), gk, HEAD_DIM), cos, sin).astype(BF16)
        vwt = kv(5).T.astype(BF16)
        for cidx in range(tm // LANE):
            vwt_ref[g, cidx] = vwt[:, cidx * LANE:(cidx + 1) * LANE]
        gate_ref[g] = jax.nn.sigmoid(blk(HY_G0 + g)).T[:GATE_ROWS]

    @pl.when(i == 0)
    def _():
        carry_ref[...] = jnp.zeros_like(carry_ref)

    z = blk(HY_FF) + fb_ref[...]
    logf = jnp.minimum(z, 0.0) - jnp.log1p(jnp.exp(-jnp.abs(z)))
    tri = jnp.where(_row_iota((tm, tm)) >= _lane_iota((tm, tm)), 1.0, 0.0).astype(F32)
    cum = jnp.dot(tri, logf, precision=HIGHEST, preferred_element_type=F32) + carry_ref[...]
    carry_ref[...] = cum[tm - 1:tm, :]

    for h in range(FOX_HEADS):
        c = jnp.broadcast_to(cum[:, h:h + 1], shp) * LOG2E
        hi, mid, lo = _split3(c)
        fq = _head_rms(blk(HY_F0 + h), gfq, HEAD_DIM) * scale
        fq = jnp.where(real, fq, jnp.where(lane == 64, hi, jnp.where(lane == 65, mid, jnp.where(
            lane == 66, lo, jnp.where(lane < 70, 1.0, 0.0)))))
        fk = _head_rms(blk(HY_F0 + FOX_HEADS + h), gfk, HEAD_DIM)
        fk = jnp.where(real, fk, jnp.where(lane < 67, 1.0, jnp.where(lane == 67, -hi, jnp.where(
            lane == 68, -mid, jnp.where(lane == 69, -lo, ref_ones)))))
        sl = slice(h * LANE, (h + 1) * LANE)
        fqt_ref[sl, :] = fq.T.astype(BF16)
        fk_ref[:, sl] = fk.astype(BF16)
        fvt_ref[h, 0] = jnp.where(ones_row, 1.0, blk(HY_F0 + 2 * FOX_HEADS + h)).T.astype(BF16)


def _feat_major(heads, s, tm):
    return (pl.BlockSpec((heads * LANE, tm), lambda i: (0, i)),
            jax.ShapeDtypeStruct((heads * LANE, s), BF16))


def _value_tiles(heads, s, tm, tk):
    return (pl.BlockSpec((heads, tm // tk, LANE, tk), lambda i: (0, i, 0, 0)),
            jax.ShapeDtypeStruct((heads, s // tk, LANE, tk), BF16))


def _hy_prep(proj, posf, inv128, gq, gk, gfq, gfk, fbias):
    s = proj.shape[0]
    tm = PREP_TILE
    assert tm == KV_TILE
    vec = pl.BlockSpec((1, LANE), lambda i: (0, 0))

    def rows(nb):
        return (pl.BlockSpec((tm, nb * LANE), lambda i: (i, 0)), jax.ShapeDtypeStruct((s, nb * LANE), BF16))

    tok = (pl.BlockSpec((NSA_GROUPS, tm, HEAD_DIM), lambda i: (0, i, 0)),
           jax.ShapeDtypeStruct((NSA_GROUPS, s, HEAD_DIM), BF16))
    gate = (pl.BlockSpec((NSA_GROUPS, GATE_ROWS, tm), lambda i: (0, 0, i)),
            jax.ShapeDtypeStruct((NSA_GROUPS, GATE_ROWS, s), F32))
    outs = [_feat_major(NSA_HEADS, s, tm), tok, tok, rows(NSA_GROUPS), _value_tiles(NSA_GROUPS, s, tm, KV_TILE),
            rows(NSA_GROUPS), _value_tiles(NSA_GROUPS, s, tm, LANE), gate,
            _feat_major(FOX_HEADS, s, tm), rows(FOX_HEADS), _value_tiles(FOX_HEADS, s, tm, KV_TILE)]
    return pl.pallas_call(
        _hy_prep_kernel,
        grid=(s // tm,),
        in_specs=[pl.BlockSpec((tm, HY_BLOCKS * LANE), lambda i: (i, 0)), pl.BlockSpec((tm, 1), lambda i: (i, 0)),
                  vec, vec, vec, vec, vec, vec],
        out_specs=[o[0] for o in outs],
        out_shape=[o[1] for o in outs],
        scratch_shapes=[pltpu.VMEM((1, LANE), F32)],
        compiler_params=_cparams("arbitrary"),
        name="hybrid_prep",
    )(proj, posf, inv128, gq, gk, gfq, gfk, fbias)


def _compress_kernel(kc_ref, vc_ref, wk_ref, wv_ref, pek_ref, pev_ref, gk_ref, ko_ref, vo_ref):
    half = CMP_STRIDE * HEAD_DIM

    def comp(ch_ref, w_ref, pe_ref):
        ch = ch_ref[0]
        nc = ch.shape[0]
        a = jnp.dot(ch, w_ref[:half], preferred_element_type=F32)
        b = jnp.dot(ch, w_ref[half:], preferred_element_type=F32)
        nxt = pltpu.roll(b, nc - 1, 0)
        pe = jnp.dot(jnp.broadcast_to(pe_ref[...], (8, 2 * half)).astype(BF16), w_ref[...],
                     preferred_element_type=F32)[0:1]
        return a + nxt + pe

    ko_ref[0] = _head_rms(comp(kc_ref, wk_ref, pek_ref), gk_ref[...], HEAD_DIM).astype(BF16)
    vo_ref[0] = comp(vc_ref, wv_ref, pev_ref).T.astype(BF16)


def _compress(kct, vct, w_cmp, cmp_pe, k_norm):
    g, s, _ = kct.shape
    nc = s // CMP_STRIDE
    wide = CMP_STRIDE * HEAD_DIM
    kch = kct.reshape(g, nc, wide)
    vch = vct.reshape(g, nc, wide)
    w_pad = jnp.pad(w_cmp, ((0, 0), (0, 0), (0, LANE - HEAD_DIM))).astype(BF16)
    ch = pl.BlockSpec((1, nc, wide), lambda i: (i, 0, 0))
    wspec = pl.BlockSpec((2 * wide, LANE), lambda i: (0, 0))
    pespec = pl.BlockSpec((1, 2 * wide), lambda i: (0, 0))
    return pl.pallas_call(
        _compress_kernel,
        grid=(g,),
        in_specs=[ch, ch, wspec, wspec, pespec, pespec, pl.BlockSpec((1, LANE), lambda i: (0, 0))],
        out_specs=[pl.BlockSpec((1, nc, LANE), lambda i: (i, 0, 0)),
                   pl.BlockSpec((1, LANE, nc), lambda i: (i, 0, 0))],
        out_shape=[jax.ShapeDtypeStruct((g, nc, LANE), BF16), jax.ShapeDtypeStruct((g, LANE, nc), BF16)],
        compiler_params=_cparams("parallel"),
        name="nsa_compress",
    )(kch, vch, w_pad[0], w_pad[1], cmp_pe[0].reshape(1, 2 * wide).astype(F32),
      cmp_pe[1].reshape(1, 2 * wide).astype(F32), _pad_lanes(k_norm))


def _masked_softmax_t(s, mask):
    s = jnp.where(mask, s, NEG)
    m = jnp.max(s, axis=0, keepdims=True)
    e = jnp.where(mask, jnp.exp2(s - m), 0.0)
    return e / jnp.maximum(jnp.sum(e, axis=0, keepdims=True), 1e-30)


def _online_steps(steps, ms, acc_ref):
    ms = list(ms)

    def scores(step):
        k_tile, qa, _, c, mask = step
        s = jnp.dot(k_tile, qa, preferred_element_type=F32)
        if mask is not None:
            s = jnp.where(mask, s, NEG)
        return s, jnp.max(s, axis=0, keepdims=True)

    nxt = scores(steps[0])
    for idx, (_, _, vt, c, _) in enumerate(steps):
        sl = slice(c * Q_STRIP, (c + 1) * Q_STRIP)
        s, s_max = nxt
        if idx + 1 < len(steps):
            nxt = scores(steps[idx + 1])
        m_new = jnp.maximum(ms[c], s_max)
        a = jnp.exp2(ms[c] - m_new)
        p = jnp.exp2((s - m_new).astype(BF16))
        ms[c] = m_new
        acc_ref[:, sl] = a * acc_ref[:, sl] + jnp.dot(vt, p, preferred_element_type=F32)
    return tuple(ms)


def _m_init(n_strips):
    return tuple(jnp.full((1, Q_STRIP), NEG, F32) for _ in range(n_strips))


def _with_ref_rows(qa, m):
    hi, mid, lo = _split3(-m)
    r = _row_iota((LANE - REF_SLAB, Q_STRIP)) + REF_SLAB
    slab = jnp.where(r == REF_ROW, hi, jnp.where(r == REF_ROW + 1, mid, jnp.where(r == REF_ROW + 2, lo, 0.0)))
    return jnp.concatenate([qa[:REF_SLAB], slab.astype(BF16)], axis=0)


def _first_tile_max(k_tile, qa_strips, masks):
    return tuple(jnp.max(jnp.where(mask, jnp.dot(k_tile, qa, preferred_element_type=F32), NEG), axis=0, keepdims=True)
                 for qa, mask in zip(qa_strips, masks))


def _fast_steps(steps, state, acc_ref):
    state = list(state)
    for a, b in zip(steps[:-1], steps[1:]):
        assert a[3] != b[3]

    def scores(step):
        k_tile, qa, _, c, mask = step
        s = jnp.dot(k_tile, _with_ref_rows(qa, state[c][0]), preferred_element_type=F32)
        if mask is not None:
            s = jnp.where(mask, s, NEG)
        return s

    s_next = scores(steps[0])
    for idx, (_, _, vt, c, _) in enumerate(steps):
        sl = slice(c * Q_STRIP, (c + 1) * Q_STRIP)
        s = s_next
        m, worst = state[c]
        cm = jnp.max(s, axis=0, keepdims=True)
        inc = jnp.maximum(cm, 0.0)
        state[c] = (m + inc, jnp.maximum(worst, cm))
        if idx + 1 < len(steps):
            s_next = scores(steps[idx + 1])
        p = jnp.exp2(s).astype(BF16)
        acc_ref[:, sl] = jnp.exp2(-inc) * (acc_ref[:, sl] + jnp.dot(vt, p, preferred_element_type=F32))
    return tuple(state)


def _flat(state):
    return tuple(x for pair in state for x in pair)


def _nest(flat):
    return tuple((flat[2 * c], flat[2 * c + 1]) for c in range(len(flat) // 2))


def _nsa_kernel(qt_ref, kc_ref, vct_ref, ks_ref, vst_ref, kw_ref, vwt_ref, gate_ref, ovt_ref, o_ref,
                qaug_ref, acc_ref, *, n_sel):
    i = pl.program_id(1)
    tq = Q_TILE_NSA
    cols = NSA_HPG * tq
    qs = i * tq
    nc = kc_ref.shape[1]
    nslc = ovt_ref.shape[0]
    n_super = nslc // SUPER_BLOCKS

    qt = jnp.concatenate([qt_ref[h * LANE:(h + 1) * LANE, :] for h in range(NSA_HPG)], axis=1)
    tq_row = qs + (_lane_iota((1, cols)) % tq)

    s = jnp.dot(kc_ref[0], qt, preferred_element_type=F32)
    cmp_end = _row_iota((nc, 1)) * CMP_STRIDE + (CMP_LEN - 1)
    p = _masked_softmax_t(s, cmp_end <= tq_row)
    o_cmp = jnp.dot(vct_ref[0], p.astype(BF16), preferred_element_type=F32)

    psum = p[:, 0:tq]
    for h in range(1, NSA_HPG):
        psum = psum + p[:, h * tq:(h + 1) * tq]
    p_hi = psum.astype(BF16)
    p_lo = (psum - p_hi.astype(F32)).astype(BF16)
    ovt = ovt_ref[...]
    imp = (jnp.dot(ovt, p_hi, preferred_element_type=F32)
           + jnp.dot(ovt, p_lo, preferred_element_type=F32))

    jj = _row_iota((nslc, tq))
    tq_blk = qs + _lane_iota((nslc, tq))
    cur = tq_blk // SLC_BLOCK
    forced = (jj == 0) | (jj == cur) | (jj == cur - 1)
    causal_blk = jj * SLC_BLOCK <= tq_blk
    val = jnp.where(forced, imp + BIG, imp)
    val = jnp.where(causal_blk, val, NEG)

    jjf = jj.astype(F32)

    def pick(_, carry):
        val, sel = carry
        mx = jnp.max(val, axis=0, keepdims=True)
        idx = jnp.min(jnp.where(val == mx, jjf, float(nslc)), axis=0, keepdims=True)
        hit = jjf == idx
        return jnp.where(hit, -jnp.inf, val), jnp.where(hit, 1.0, sel)

    _, sel = lax.fori_loop(0, n_sel, pick, (val, jnp.zeros((nslc, tq), F32)))
    bias_t = jnp.where((sel > 0.0) & causal_blk, 0.0, NEG)

    q_rows = qt[:HEAD_DIM].astype(F32)
    spare = jnp.zeros((LANE - HEAD_DIM - SUPER_BLOCKS, cols), F32)
    for st in range(n_super):
        b = bias_t[st * SUPER_BLOCKS:(st + 1) * SUPER_BLOCKS]
        b = jnp.concatenate([b] * NSA_HPG, axis=1)
        qaug_ref[st] = jnp.concatenate([q_rows, b, spare], axis=0).astype(BF16)

    tk = KV_TILE
    per_super = SUPER_BLOCKS * SLC_BLOCK // tk
    j_last = (qs + tq - 1) // tk
    n_strips = cols // Q_STRIP
    strips = [slice(c * Q_STRIP, (c + 1) * Q_STRIP) for c in range(n_strips)]

    def causal_masks(j):
        kpos = j * tk + _row_iota((tk, 1))
        return [kpos <= tq_row[:, sl] for sl in strips]

    def slc_steps(j, masks=None):
        k0 = pl.multiple_of(j * tk, tk)
        k_tile, vt, st = ks_ref[pl.ds(k0, tk), :], vst_ref[0, j], j // per_super
        return [(k_tile, qaug_ref[st, :, strips[c]], vt, c, None if masks is None else masks[c])
                for c in range(n_strips)]

    def pair(jj, flat):
        return _flat(_fast_steps(slc_steps(2 * jj) + slc_steps(2 * jj + 1), _nest(flat), acc_ref))

    def single(j, flat):
        return _flat(_fast_steps(slc_steps(j), _nest(flat), acc_ref))

    acc_ref[...] = jnp.zeros(acc_ref.shape, F32)
    m0 = _first_tile_max(ks_ref[0:tk, :], [qaug_ref[0, :, sl] for sl in strips], causal_masks(0))
    n_pairs = j_last // 2
    flat = lax.fori_loop(0, n_pairs, pair, _flat(tuple((m, jnp.zeros_like(m)) for m in m0)))
    flat = lax.fori_loop(2 * n_pairs, j_last, single, flat)
    state = _fast_steps(slc_steps(j_last, causal_masks(j_last)), _nest(flat), acc_ref)
    worst = jnp.max(jnp.concatenate([w for _, w in state], axis=1))

    @pl.when(worst > EXP_GUARD)
    def _():
        acc_ref[...] = jnp.zeros(acc_ref.shape, F32)
        lax.fori_loop(0, j_last + 1, lambda j, ms: _online_steps(slc_steps(j, causal_masks(j)), ms, acc_ref),
                      _m_init(n_strips))

    acc = acc_ref[...]
    o_slc = acc / jnp.maximum(acc[SUM_ROW:SUM_ROW + 1], 1e-30)

    wlen = WINDOW + tq
    ws = pl.multiple_of(jnp.maximum(qs - WINDOW, 0), tq)
    s = jnp.dot(kw_ref[pl.ds(ws, wlen), :], qt, preferred_element_type=F32)
    dist = tq_row - (ws + _row_iota((wlen, 1)))
    p = _masked_softmax_t(s, (dist >= 0) & (dist < WINDOW)).astype(BF16)
    wb = ws // LANE
    o_win = jnp.zeros((LANE, cols), F32)
    for c in range(wlen // LANE):
        o_win = o_win + jnp.dot(vwt_ref[0, wb + c], p[c * LANE:(c + 1) * LANE], preferred_element_type=F32)

    gate = gate_ref[0]
    for h in range(NSA_HPG):
        sl = slice(h * tq, (h + 1) * tq)
        o = (gate[3 * h:3 * h + 1] * o_cmp[:, sl] + gate[3 * h + 1:3 * h + 2] * o_slc[:, sl]
             + gate[3 * h + 2:3 * h + 3] * o_win[:, sl])
        o_ref[:, h * LANE:(h + 1) * LANE] = o.T.astype(BF16)


def _overlap_t(s, nslc_pad):
    nc = s // CMP_STRIDE
    cmp_start = np.arange(nc) * CMP_STRIDE
    slc_start = np.arange(nslc_pad) * SLC_BLOCK
    ov = np.clip(np.minimum(cmp_start[:, None] + CMP_LEN, slc_start[None, :] + SLC_BLOCK)
                 - np.maximum(cmp_start[:, None], slc_start[None, :]), 0, None) / CMP_STRIDE
    ov[nc - CMP_LEN // CMP_STRIDE + 1:, :] = 0.0
    ov[:, s // SLC_BLOCK:] = 0.0
    return jnp.asarray(ov.T, BF16)


def _nsa_attention(qnt, kc, vct, ks, vst, kw, vwt, gates):
    s = qnt.shape[1]
    nc = s // CMP_STRIDE
    n_slc = s // SLC_BLOCK
    nslc_pad = -(-n_slc // LANE) * LANE
    tq = Q_TILE_NSA
    cols = NSA_HPG * tq
    once = pl.Buffered(1)
    res = pl.BlockSpec((s, LANE), lambda g, i: (0, g), pipeline_mode=once)
    return pl.pallas_call(
        functools.partial(_nsa_kernel, n_sel=min(SLC_TOPK, n_slc)),
        grid=(NSA_GROUPS, s // tq),
        in_specs=[pl.BlockSpec((NSA_HPG * LANE, tq), lambda g, i: (g, i)),
                  pl.BlockSpec((1, nc, LANE), lambda g, i: (g, 0, 0), pipeline_mode=once),
                  pl.BlockSpec((1, LANE, nc), lambda g, i: (g, 0, 0), pipeline_mode=once),
                  res, pl.BlockSpec((1, s // KV_TILE, LANE, KV_TILE), lambda g, i: (g, 0, 0, 0),
                                    pipeline_mode=once),
                  res, pl.BlockSpec((1, s // LANE, LANE, LANE), lambda g, i: (g, 0, 0, 0), pipeline_mode=once),
                  pl.BlockSpec((1, GATE_ROWS, tq), lambda g, i: (g, 0, i)),
                  pl.BlockSpec((nslc_pad, nc), lambda g, i: (0, 0), pipeline_mode=once)],
        out_specs=pl.BlockSpec((tq, NSA_HPG * LANE), lambda g, i: (i, g)),
        out_shape=jax.ShapeDtypeStruct((s, NSA_HEADS * LANE), BF16),
        scratch_shapes=[pltpu.VMEM((nslc_pad // SUPER_BLOCKS, LANE, cols), BF16),
                        pltpu.VMEM((LANE, cols), F32)],
        compiler_params=_cparams("parallel", "arbitrary"),
        name="nsa_attention",
    )(qnt, kc, vct, ks, vst, kw, vwt, gates, _overlap_t(s, nslc_pad))


def _flash_kernel(qt_ref, k_ref, vt_ref, o_ref, acc_ref):
    i = pl.program_id(1)
    tq, tk = Q_TILE_FLASH, KV_TILE
    acc_ref[...] = jnp.zeros(acc_ref.shape, F32)

    n_strips = tq // Q_STRIP
    per_q = tq // tk
    qas = [qt_ref[:, c * Q_STRIP:(c + 1) * Q_STRIP] for c in range(n_strips)]

    def tile_steps(j, d=None):
        k0 = pl.multiple_of(j * tk, tk)
        k_tile, vt = k_ref[pl.ds(k0, tk), :], vt_ref[0, j]
        steps = []
        for c in range(n_strips):
            mask = None
            if d is not None:
                if d * tk > (c + 1) * Q_STRIP - 1:
                    continue
                if (d + 1) * tk - 1 > c * Q_STRIP:
                    shp = (tk, Q_STRIP)
                    mask = _row_iota(shp) + d * tk <= _lane_iota(shp) + c * Q_STRIP
            steps.append((k_tile, qas[c], vt, c, mask))
        return steps

    def any_tile_masks(j):
        shp = (tk, Q_STRIP)
        return [_row_iota(shp) + j * tk <= _lane_iota(shp) + (i * tq + c * Q_STRIP) for c in range(n_strips)]

    def below(t, flat):
        jj = i - 1 - t
        steps = [st for u in range(per_q) for st in tile_steps(jj * per_q + (per_q - 1 - u))]
        return _flat(_fast_steps(steps, _nest(flat), acc_ref))

    diag0 = i * per_q
    own = [(c * Q_STRIP) // tk for c in range(n_strips)]
    m0 = tuple(_first_tile_max(k_ref[pl.ds(pl.multiple_of((diag0 + own[c]) * tk, tk), tk), :], [qas[c]],
                               [any_tile_masks(diag0 + own[c])[c]])[0] for c in range(n_strips))
    steps = [st for d in reversed(range(per_q)) for st in tile_steps(diag0 + d, d)]
    state = _fast_steps(steps, tuple((m, jnp.zeros_like(m)) for m in m0), acc_ref)
    state = _nest(lax.fori_loop(0, i, below, _flat(state)))
    worst = jnp.max(jnp.concatenate([w for _, w in state], axis=1))

    @pl.when(worst > EXP_GUARD)
    def _():
        acc_ref[...] = jnp.zeros(acc_ref.shape, F32)

        def exact(j, ms):
            k0 = pl.multiple_of(j * tk, tk)
            k_tile, vt, masks = k_ref[pl.ds(k0, tk), :], vt_ref[0, j], any_tile_masks(j)
            return _online_steps([(k_tile, qas[c], vt, c, masks[c]) for c in range(n_strips)], ms, acc_ref)

        lax.fori_loop(0, (i + 1) * per_q, exact, _m_init(n_strips))

    acc = acc_ref[...]
    o = acc / acc[SUM_ROW:SUM_ROW + 1]
    for c0 in range(0, tq, LANE):
        o_ref[c0:c0 + LANE, :] = o[:, c0:c0 + LANE].T.astype(BF16)


def _causal_attention(qt, k, vt):
    s, width = k.shape
    heads = width // LANE
    assert Q_TILE_FLASH % KV_TILE == 0 and s % Q_TILE_FLASH == 0
    return pl.pallas_call(
        _flash_kernel,
        grid=(heads, s // Q_TILE_FLASH),
        in_specs=[pl.BlockSpec((LANE, Q_TILE_FLASH), lambda h, i: (h, i)),
                  pl.BlockSpec((s, LANE), lambda h, i: (0, h)),
                  pl.BlockSpec((1, s // KV_TILE, LANE, KV_TILE), lambda h, i: (h, 0, 0, 0))],
        out_specs=pl.BlockSpec((Q_TILE_FLASH, LANE), lambda h, i: (i, h)),
        out_shape=jax.ShapeDtypeStruct((s, width), BF16),
        scratch_shapes=[pltpu.VMEM((LANE, Q_TILE_FLASH), F32)],
        compiler_params=_cparams("parallel", "arbitrary"),
        name="causal_attention",
    )(qt, k, vt)


def _out_proj_kernel(oa_ref, ob_ref, wa_ref, wb_ref, x_ref, g_ref, o_ref):
    y = jnp.dot(oa_ref[...], wa_ref[...], preferred_element_type=F32)
    y = y + jnp.dot(ob_ref[...], wb_ref[...], preferred_element_type=F32)
    o_ref[...] = x_ref[...] + g_ref[...] * y


def _out_proj(oa, ob, cola, colb, wa, wb, x2, gate):
    s, d = x2.shape
    ka = wa.shape[0]
    tm = ROW_TILE
    return pl.pallas_call(
        _out_proj_kernel,
        grid=(s // tm,),
        in_specs=[pl.BlockSpec((tm, ka), lambda i: (i, cola)), pl.BlockSpec((tm, ka), lambda i: (i, colb)),
                  pl.BlockSpec((ka, d), lambda i: (0, 0)), pl.BlockSpec((ka, d), lambda i: (0, 0)),
                  pl.BlockSpec((tm, d), lambda i: (i, 0)), pl.BlockSpec((1, d), lambda i: (0, 0))],
        out_specs=pl.BlockSpec((tm, d), lambda i: (i, 0)),
        out_shape=jax.ShapeDtypeStruct((s, d), F32),
        compiler_params=_cparams("parallel"),
        name="out_proj",
    )(oa, ob, wa, wb, x2, gate)


def _mla_prep_kernel(p_ref, pos_ref, inv_ref, gqa_ref, gkva_ref, wuq_ref, wuk_ref, wuv_ref,
                     gq_ref, gk_ref, gkr_ref, qt_ref, k_ref, vt_ref):
    shp = (PREP_TILE, LANE)
    lane = _lane_iota(shp)
    nope = lane < QK_NOPE
    rope = (lane >= QK_NOPE) & (lane < QK_NOPE + QK_ROPE)
    ref_ones = jnp.where((lane >= REF_ROW) & (lane < REF_ROW + 3), 1.0, 0.0)
    ang = pos_ref[...] * inv_ref[...]
    cos = jnp.where(rope, jnp.cos(ang), 1.0)
    sin = jnp.where(rope, jnp.sin(ang), 0.0)

    def rope32(x):
        half = QK_ROPE // 2
        rot = jnp.where(lane < QK_NOPE + half, -pltpu.roll(x, LANE - half, 1), pltpu.roll(x, half, 1))
        return x * cos + rot * sin

    def low_rank_norm(x, g):
        ms = jnp.mean(x * x, axis=-1, keepdims=True)
        return (x * lax.rsqrt(ms + EPS) * g).astype(BF16)

    nq = Q_LORA // LANE
    cq = low_rank_norm(p_ref[:, :Q_LORA], gqa_ref[...])
    ckv = low_rank_norm(p_ref[:, Q_LORA:Q_LORA + KV_LORA], gkva_ref[...])
    kr = p_ref[:, (nq + KV_LORA // LANE) * LANE:(nq + KV_LORA // LANE + 1) * LANE]
    k_rope = rope32(_head_rms(kr, gkr_ref[...], QK_ROPE))

    gq, gk = gq_ref[...], gk_ref[...]
    scale = (QK_NOPE + QK_ROPE) ** -0.5 * LOG2E
    pair = 2 * LANE
    for hp in range(MLA_HEADS // 2):
        cols = slice(hp * pair, (hp + 1) * pair)
        q2 = jnp.dot(cq, wuq_ref[:, cols], preferred_element_type=F32)
        k2 = jnp.dot(ckv, wuk_ref[:, cols], preferred_element_type=F32)
        v2 = jnp.dot(ckv, wuv_ref[:, cols], preferred_element_type=F32)
        for sub in range(2):
            head = 2 * hp + sub
            sl = slice(head * LANE, (head + 1) * LANE)
            half = slice(sub * LANE, (sub + 1) * LANE)
            x = q2[:, half]
            ss_n = jnp.sum(jnp.where(nope, x * x, 0.0), axis=-1, keepdims=True)
            ss_r = jnp.sum(jnp.where(rope, x * x, 0.0), axis=-1, keepdims=True)
            inv_rms = jnp.where(nope, lax.rsqrt(ss_n * (1.0 / QK_NOPE) + EPS),
                                lax.rsqrt(ss_r * (1.0 / QK_ROPE) + EPS))
            qt_ref[sl, :] = (rope32(x * inv_rms * gq) * scale).T.astype(BF16)
            kn = _head_rms(k2[:, half], gk, QK_NOPE)
            k_ref[:, sl] = (kn + k_rope + ref_ones).astype(BF16)
            vt_ref[head, 0] = jnp.where(lane == SUM_ROW, 1.0, v2[:, half]).T.astype(BF16)


def _mla_prep(proj, posf, inv128, gqa, gkva, wuq, wuk, wuv, gq, gk, gkr):
    s, n = proj.shape
    tm = PREP_TILE
    assert tm == KV_TILE

    def full(a):
        return pl.BlockSpec(a.shape, lambda i: (0, 0))

    outs = [_feat_major(MLA_HEADS, s, tm),
            (pl.BlockSpec((tm, MLA_HEADS * LANE), lambda i: (i, 0)),
             jax.ShapeDtypeStruct((s, MLA_HEADS * LANE), BF16)),
            _value_tiles(MLA_HEADS, s, tm, KV_TILE)]
    args = (inv128, gqa, gkva, wuq, wuk, wuv, gq, gk, gkr)
    return pl.pallas_call(
        _mla_prep_kernel,
        grid=(s // tm,),
        in_specs=[pl.BlockSpec((tm, n), lambda i: (i, 0)), pl.BlockSpec((tm, 1), lambda i: (i, 0))]
                 + [full(a) for a in args],
        out_specs=[o[0] for o in outs],
        out_shape=[o[1] for o in outs],
        compiler_params=_cparams("parallel"),
        name="mla_prep",
    )(proj, posf, *args)


def _rank_lt(v, k):
    n = v.shape[0]
    row = _row_iota(v.shape)
    rank = jnp.zeros(v.shape, F32)
    for b in range(n):
        vb = v[b:b + 1, :]
        rank = rank + jnp.where((vb > v) | ((vb == v) & (row > b)), 1.0, 0.0)
    return rank < k


def _moe_route_kernel(x_ref, g_ref, sc_ref, sh_ref, wr_ref, rb_ref, h_ref, pos_ref, wt_ref, cnt_ref):
    tm = ROW_TILE
    h = _norm_mod(x_ref[...], g_ref[...], sc_ref[...], sh_ref[...])
    h_ref[...] = h.astype(BF16)
    logits = jnp.dot(h, wr_ref[...], precision=HIGHEST, preferred_element_type=F32)
    lt = logits.T[:N_EXPERTS]
    scores = jax.nn.sigmoid(lt)
    sel = scores + rb_ref[...]

    per = N_EXPERTS // N_GROUPS
    grp = sel.reshape(N_GROUPS, per, tm)
    sub = lax.broadcasted_iota(jnp.int32, grp.shape, 1)
    m1 = jnp.max(grp, axis=1, keepdims=True)
    first = jnp.min(jnp.where(grp == m1, sub, per), axis=1, keepdims=True)
    m2 = jnp.max(jnp.where(sub == first, -jnp.inf, grp), axis=1, keepdims=True)
    gscore = (m1 + m2).reshape(N_GROUPS, tm)
    gmask = _rank_lt(gscore, TOPK_GROUPS)
    emask = jnp.broadcast_to(gmask.reshape(N_GROUPS, 1, tm), grp.shape).reshape(N_EXPERTS, tm)
    chosen = _rank_lt(jnp.where(emask, sel, NEG), TOP_K)

    w = jnp.where(chosen, scores, 0.0)
    wt_ref[...] = w / jnp.sum(w, axis=0, keepdims=True) * ROUTED_SCALE

    upper = jnp.where(_row_iota((tm, tm)) <= _lane_iota((tm, tm)), 1.0, 0.0).astype(BF16)
    incl = jnp.dot(jnp.where(chosen, 1.0, 0.0).astype(BF16), upper, preferred_element_type=F32)
    pos_ref[...] = jnp.where(chosen, incl - 1.0, -1.0)
    cnt_ref[0] = jnp.broadcast_to(incl[:, tm - 1:tm], (N_EXPERTS, LANE))


def _moe_route(x2, g, sc, sh, w_router_pad, router_bias_col):
    s, d = x2.shape
    tm = ROW_TILE
    vec = pl.BlockSpec((1, d), lambda i: (0, 0))
    et = pl.BlockSpec((N_EXPERTS, tm), lambda i: (0, i))
    return pl.pallas_call(
        _moe_route_kernel,
        grid=(s // tm,),
        in_specs=[pl.BlockSpec((tm, d), lambda i: (i, 0)), vec, vec, vec,
                  pl.BlockSpec((d, LANE), lambda i: (0, 0)),
                  pl.BlockSpec((N_EXPERTS, 1), lambda i: (0, 0))],
        out_specs=[pl.BlockSpec((tm, d), lambda i: (i, 0)), et, et,
                   pl.BlockSpec((1, N_EXPERTS, LANE), lambda i: (i, 0, 0))],
        out_shape=[jax.ShapeDtypeStruct((s, d), BF16), jax.ShapeDtypeStruct((N_EXPERTS, s), F32),
                   jax.ShapeDtypeStruct((N_EXPERTS, s), F32),
                   jax.ShapeDtypeStruct((s // tm, N_EXPERTS, LANE), F32)],
        compiler_params=_cparams("parallel"),
        name="moe_route",
    )(x2, g, sc, sh, w_router_pad, router_bias_col)


def _moe_kernel(cnt_ref, x_ref, h_ref, pos_ref, wt_ref, wg_ref, wu_ref, wd_ref, sg_ref, su_ref, sd_ref,
                g2_ref, o_ref, acc_ref):
    i = pl.program_id(0)
    e = pl.program_id(1)
    tm = ROW_TILE
    r = MOE_CHUNK

    @pl.when(e == 0)
    def _():
        h = h_ref[...]
        a = jnp.dot(h, sg_ref[...], preferred_element_type=F32)
        a = a * jax.nn.sigmoid(a) * jnp.dot(h, su_ref[...], preferred_element_type=F32)
        acc_ref[...] = jnp.dot(a.astype(BF16), sd_ref[...], preferred_element_type=F32)

    first = e * MOE_EXPERTS_PER_STEP
    n = cnt_ref[i * N_EXPERTS + first]
    for k in range(1, MOE_EXPERTS_PER_STEP):
        n = jnp.maximum(n, cnt_ref[i * N_EXPERTS + first + k])
    prows = [pos_ref[pl.ds(first + k, 1), :] for k in range(MOE_EXPERTS_PER_STEP)]
    wrows = [wt_ref[pl.ds(first + k, 1), :] for k in range(MOE_EXPERTS_PER_STEP)]

    def chunk(c, _):
        slot = (_row_iota((r, tm)) + c * r).astype(F32)
        hits = [prow == slot for prow in prows]
        onehot = jnp.concatenate([jnp.where(hit, 1.0, 0.0).astype(BF16) for hit in hits], axis=0)
        xg = jnp.dot(onehot, h_ref[...], preferred_element_type=F32).astype(BF16)
        ys = []
        for k in range(MOE_EXPERTS_PER_STEP):
            xk = xg[k * r:(k + 1) * r]
            a = jnp.dot(xk, wg_ref[k], preferred_element_type=F32)
            a = a * jax.nn.sigmoid(a) * jnp.dot(xk, wu_ref[k], preferred_element_type=F32)
            y = jnp.dot(a.astype(BF16), wd_ref[k], preferred_element_type=F32)
            wr = jnp.sum(jnp.where(hits[k], wrows[k], 0.0), axis=-1, keepdims=True)
            ys.append((y * wr).astype(BF16))
        acc_ref[...] += _dot_tn(onehot, jnp.concatenate(ys, axis=0))
        return 0

    lax.fori_loop(0, (n + r - 1) // r, chunk, 0)

    @pl.when(e == N_EXPERTS // MOE_EXPERTS_PER_STEP - 1)
    def _():
        o_ref[...] = x_ref[...] + g2_ref[...] * acc_ref[...]


def _moe_experts(counts, x2, h, pos_t, w_t, wg, wu, wd, sg, su, sd, g2):
    s, d = x2.shape
    tm = ROW_TILE
    ff = wg.shape[2]
    tile = pl.BlockSpec((tm, d), lambda i, e, c: (i, 0))
    et = pl.BlockSpec((N_EXPERTS, tm), lambda i, e, c: (0, i))

    def const(a):
        return pl.BlockSpec(a.shape, lambda i, e, c: (0,) * a.ndim)

    per = MOE_EXPERTS_PER_STEP
    grid_spec = pltpu.PrefetchScalarGridSpec(
        num_scalar_prefetch=1,
        grid=(s // tm, N_EXPERTS // per),
        in_specs=[tile, tile, et, et,
                  pl.BlockSpec((per, d, ff), lambda i, e, c: (e, 0, 0)),
                  pl.BlockSpec((per, d, ff), lambda i, e, c: (e, 0, 0)),
                  pl.BlockSpec((per, ff, d), lambda i, e, c: (e, 0, 0)),
                  const(sg), const(su), const(sd), const(g2)],
        out_specs=tile,
        scratch_shapes=[pltpu.VMEM((tm, d), F32)],
    )
    return pl.pallas_call(
        _moe_kernel,
        grid_spec=grid_spec,
        out_shape=jax.ShapeDtypeStruct((s, d), F32),
        compiler_params=_cparams("parallel", "arbitrary"),
        name="moe_experts",
    )(counts, x2, h, pos_t, w_t, wg, wu, wd, sg, su, sd, g2)


def _pad_lanes(v, width=LANE, offset=0):
    out = jnp.zeros((1, width), F32)
    return out.at[0, offset:offset + v.shape[0]].set(v.astype(F32))


def _head_cols(w, n_heads, dim):
    d = w.shape[0]
    w3 = w.reshape(d, n_heads, dim)
    return jnp.pad(w3, ((0, 0), (0, 0), (0, LANE - dim))).reshape(d, n_heads * LANE)


def _hybrid_w_in(w_in):
    d = w_in.shape[0]
    nq = NSA_HEADS * HEAD_DIM
    nkv = 6 * NSA_GROUPS * HEAD_DIM
    ng = 3 * NSA_HEADS
    nf = 3 * FOX_HEADS * HEAD_DIM
    c0, c1, c2, c3 = nq, nq + nkv, nq + nkv + ng, nq + nkv + ng + nf
    gates = w_in[:, c1:c2].reshape(d, NSA_GROUPS, 3 * NSA_HPG)
    gates = jnp.pad(gates, ((0, 0), (0, 0), (0, LANE - 3 * NSA_HPG))).reshape(d, NSA_GROUPS * LANE)
    ff = jnp.pad(w_in[:, c3:], ((0, 0), (0, 2 * LANE - FOX_HEADS)))
    return jnp.concatenate([
        _head_cols(w_in[:, :c0], NSA_HEADS, HEAD_DIM),
        _head_cols(w_in[:, c0:c1], 6 * NSA_GROUPS, HEAD_DIM),
        _head_cols(w_in[:, c2:c3], 3 * FOX_HEADS, HEAD_DIM),
        gates, ff], axis=1).astype(BF16)


def _pad_head_rows(w, n_heads, dim):
    d = w.shape[1]
    w3 = w.reshape(n_heads, dim, d)
    return jnp.pad(w3, ((0, 0), (0, LANE - dim), (0, 0))).reshape(n_heads * LANE, d).astype(BF16)


def _rope_inv(dim, offset):
    inv = ROPE_THETA ** (-jnp.arange(0, dim, 2, dtype=F32) / dim)
    return _pad_lanes(jnp.concatenate([inv, inv]), offset=offset)


def _hybrid_mixer(x2, posf, mods, norm_g, w_in, fox_f_bias, nsa_q_norm, nsa_k_norm, nsa_cmp_pe, nsa_w_cmp,
                  fox_q_norm, fox_k_norm, w_out):
    sh1, sc1, g1 = mods
    proj = _norm_mod_matmul(x2, norm_g, sc1, sh1, _hybrid_w_in(w_in), tn=HY_COL_TILE)
    (qnt, kct, vct, ks, vst, kw, vwt, gates, fqt, fk, fvt) = _hy_prep(
        proj, posf, _rope_inv(HEAD_DIM, 0), _pad_lanes(nsa_q_norm), _pad_lanes(nsa_k_norm),
        _pad_lanes(fox_q_norm), _pad_lanes(fox_k_norm), _pad_lanes(fox_f_bias))
    kc, vc_t = _compress(kct, vct, nsa_w_cmp, nsa_cmp_pe, nsa_k_norm)
    o_a = _nsa_attention(qnt, kc, vc_t, ks, vst, kw, vwt, gates)
    o_b = _causal_attention(fqt, fk, fvt)
    half = NSA_HEADS * HEAD_DIM
    wa = _pad_head_rows(w_out[:half], NSA_HEADS, HEAD_DIM)
    wb = _pad_head_rows(w_out[half:], FOX_HEADS, HEAD_DIM)
    return _out_proj(o_a, o_b, 0, 0, wa, wb, x2, g1)


def _mla_mixer(x2, posf, mods, norm_g, w_in, q_a_norm, kv_a_norm, w_uq, w_ukv, qn_norm, kn_norm, qr_norm,
               kr_norm, w_out):
    sh1, sc1, g1 = mods
    d = x2.shape[1]
    w_kr = jnp.zeros((d, LANE), F32).at[:, QK_NOPE:QK_NOPE + QK_ROPE].set(w_in[:, Q_LORA + KV_LORA:])
    w_in_p = jnp.concatenate([w_in[:, :Q_LORA + KV_LORA], w_kr], axis=1).astype(BF16)
    proj = _norm_mod_matmul(x2, norm_g, sc1, sh1, w_in_p, tn=w_in_p.shape[1])
    hq = QK_NOPE + QK_ROPE
    wuq = _head_cols(w_uq, MLA_HEADS, hq).astype(BF16)
    wkv3 = w_ukv.reshape(KV_LORA, MLA_HEADS, QK_NOPE + V_HEAD)
    wuk = _head_cols(wkv3[:, :, :QK_NOPE].reshape(KV_LORA, -1), MLA_HEADS, QK_NOPE).astype(BF16)
    wuv = _head_cols(wkv3[:, :, QK_NOPE:].reshape(KV_LORA, -1), MLA_HEADS, V_HEAD).astype(BF16)
    gq = _pad_lanes(jnp.concatenate([qn_norm, qr_norm]))
    qt, k, vt = _mla_prep(proj, posf, _rope_inv(QK_ROPE, QK_NOPE), q_a_norm.reshape(1, -1).astype(F32),
                          kv_a_norm.reshape(1, -1).astype(F32), wuq, wuk, wuv, gq, _pad_lanes(kn_norm),
                          _pad_lanes(kr_norm, offset=QK_NOPE))
    o = _causal_attention(qt, k, vt)
    w_pad = _pad_head_rows(w_out, MLA_HEADS, V_HEAD)
    half = w_pad.shape[0] // 2
    return _out_proj(o, o, 0, 1, w_pad[:half], w_pad[half:], x2, g1)


def _moe_ffn(x2, mods, norm_g, w_router, router_bias, w_gate, w_up, w_down, ws_gate, ws_up, ws_down):
    sh2, sc2, g2 = mods
    w_r = jnp.pad(w_router.astype(F32), ((0, 0), (0, LANE - N_EXPERTS)))
    h, pos_t, w_t, cnt = _moe_route(x2, norm_g, sc2, sh2, w_r, router_bias.reshape(N_EXPERTS, 1).astype(F32))
    counts = cnt[:, :, 0].astype(jnp.int32).reshape(-1)
    return _moe_experts(counts, x2, h, pos_t, w_t, w_gate.astype(BF16), w_up.astype(BF16),
                        w_down.astype(BF16), ws_gate.astype(BF16), ws_up.astype(BF16), ws_down.astype(BF16), g2)


def kernel(x, c, positions, norm_attn, norm_ffn, w_ada, b_ada, hy_w_in, fox_f_bias, nsa_q_norm, nsa_k_norm, nsa_cmp_pe, nsa_w_cmp, fox_q_norm, fox_k_norm, hy_w_out, mla_w_in, mla_q_a_norm, mla_kv_a_norm, mla_w_uq, mla_w_ukv, mla_qn_norm, mla_kn_norm, mla_qr_norm, mla_kr_norm, mla_w_out, moe_w_router, moe_router_bias, moe_w_gate, moe_w_up, moe_w_down, moe_ws_gate, moe_ws_up, moe_ws_down):
    b, s, d = x.shape
    assert b == 1 and s % KV_TILE == 0 and s >= WINDOW + Q_TILE_NSA
    depth = w_ada.shape[0]
    x2 = x.reshape(s, d).astype(F32)
    posf = positions.reshape(s, 1).astype(F32)
    mod = _ada_mod(c.astype(F32), w_ada.astype(F32), b_ada.astype(F32))

    for layer in range(depth):
        m = [mod[layer, :, k * d:(k + 1) * d] for k in range(6)]
        i = layer // 2
        g_attn = norm_attn[layer].reshape(1, d).astype(F32)
        if layer % 2 == 0:
            x2 = _hybrid_mixer(x2, posf, m[0:3], g_attn, hy_w_in[i], fox_f_bias[i], nsa_q_norm[i],
                               nsa_k_norm[i], nsa_cmp_pe[i], nsa_w_cmp[i], fox_q_norm[i], fox_k_norm[i],
                               hy_w_out[i])
        else:
            x2 = _mla_mixer(x2, posf, m[0:3], g_attn, mla_w_in[i], mla_q_a_norm[i], mla_kv_a_norm[i],
                            mla_w_uq[i], mla_w_ukv[i], mla_qn_norm[i], mla_kn_norm[i], mla_qr_norm[i],
                            mla_kr_norm[i], mla_w_out[i])
        x2 = _moe_ffn(x2, m[3:6], norm_ffn[layer].reshape(1, d).astype(F32), moe_w_router[layer],
                      moe_router_bias[layer], moe_w_gate[layer], moe_w_up[layer], moe_w_down[layer],
                      moe_ws_gate[layer], moe_ws_up[layer], moe_ws_down[layer])
    return x2.reshape(b, s, d)
```

```python
import functools

import numpy as np
import jax
import jax.numpy as jnp
from jax import lax
from jax.experimental import pallas as pl
from jax.experimental.pallas import tpu as pltpu

F32 = jnp.float32
BF16 = jnp.bfloat16
HIGHEST = lax.Precision.HIGHEST

LANE = 128
VMEM_LIMIT_BYTES = 56 * 1024 * 1024

HEAD_DIM = 64
NSA_HEADS = 8
NSA_GROUPS = 2
NSA_HPG = NSA_HEADS // NSA_GROUPS
CMP_LEN = 32
CMP_STRIDE = 16
SLC_BLOCK = 64
SLC_TOPK = 16
WINDOW = 512
FOX_HEADS = 8
MLA_HEADS = 16
Q_LORA = 384
KV_LORA = 256
QK_NOPE = 64
QK_ROPE = 32
V_HEAD = 64
N_EXPERTS = 64
TOP_K = 8
N_GROUPS = 8
TOPK_GROUPS = 4
EXPERT_FF = 256
ROUTED_SCALE = 2.5
ROPE_THETA = 10000.0
EPS = 1e-6
NEG = -1e30
BIG = 1e6

ROW_TILE = 512
PREP_TILE = 512
GATE_ROWS = 16
Q_TILE_NSA = 256
KV_TILE = 512
Q_TILE_FLASH = 1024
Q_STRIP = 256
SUPER_BLOCKS = 32
SUM_ROW = 64
REF_ROW = 104
REF_SLAB = 96
EXP_GUARD = 100.0
SKIP_MARGIN = 160.0
LOG2E = 1.4426950408889634
MOE_CHUNK = 128
MOE_EXPERTS_PER_STEP = 4

HY_Q0 = 0
HY_KV0 = HY_Q0 + NSA_HEADS
HY_F0 = HY_KV0 + 6 * NSA_GROUPS
HY_G0 = HY_F0 + 3 * FOX_HEADS
HY_FF = HY_G0 + NSA_GROUPS
HY_BLOCKS = HY_FF + 2
HY_COL_TILE = 12 * LANE
assert (HY_BLOCKS * LANE) % HY_COL_TILE == 0


def _cparams(*sem):
    return pltpu.CompilerParams(dimension_semantics=sem, vmem_limit_bytes=VMEM_LIMIT_BYTES)


def _lane_iota(shape):
    return lax.broadcasted_iota(jnp.int32, shape, len(shape) - 1)


def _row_iota(shape):
    return lax.broadcasted_iota(jnp.int32, shape, len(shape) - 2)


def _dot_nt(a, b):
    return lax.dot_general(a, b, (((1,), (1,)), ((), ())), preferred_element_type=F32)


def _dot_tn(a, b):
    return lax.dot_general(a, b, (((0,), (0,)), ((), ())), preferred_element_type=F32)


def _ada_kernel(c_ref, w_ref, b_ref, o_ref):
    c = c_ref[...]
    cond = c * jax.nn.sigmoid(c)
    o_ref[0] = jnp.dot(cond, w_ref[0], precision=HIGHEST, preferred_element_type=F32) + b_ref[0]


def _ada_mod(c, w_ada, b_ada):
    depth, d, n = w_ada.shape
    tn = 768
    c8 = jnp.broadcast_to(c.reshape(1, d), (8, d))
    out = pl.pallas_call(
        _ada_kernel,
        grid=(depth, n // tn),
        in_specs=[pl.BlockSpec((8, d), lambda l, j: (0, 0)),
                  pl.BlockSpec((1, d, tn), lambda l, j: (l, 0, j)),
                  pl.BlockSpec((1, 1, tn), lambda l, j: (l, 0, j))],
        out_specs=pl.BlockSpec((1, 8, tn), lambda l, j: (l, 0, j)),
        out_shape=jax.ShapeDtypeStruct((depth, 8, n), F32),
        compiler_params=_cparams("parallel", "parallel"),
        name="ada_mod",
    )(c8, w_ada, b_ada.reshape(depth, 1, n))
    return out[:, 0:1, :]


def _norm_mod(x, g, sc, sh):
    ms = jnp.mean(x * x, axis=-1, keepdims=True)
    return (x * lax.rsqrt(ms + EPS) * g) * (1.0 + sc) + sh


def _nmm_kernel(x_ref, g_ref, sc_ref, sh_ref, w_ref, o_ref, h_scr):
    @pl.when(pl.program_id(1) == 0)
    def _():
        h_scr[...] = _norm_mod(x_ref[...], g_ref[...], sc_ref[...], sh_ref[...]).astype(BF16)

    o_ref[...] = jnp.dot(h_scr[...], w_ref[...], preferred_element_type=F32)


def _norm_mod_matmul(x2, g, sc, sh, w, tn):
    s, d = x2.shape
    n = w.shape[1]
    vec = pl.BlockSpec((1, d), lambda i, j: (0, 0))
    return pl.pallas_call(
        _nmm_kernel,
        grid=(s // ROW_TILE, n // tn),
        in_specs=[pl.BlockSpec((ROW_TILE, d), lambda i, j: (i, 0)), vec, vec, vec,
                  pl.BlockSpec((d, tn), lambda i, j: (0, j))],
        out_specs=pl.BlockSpec((ROW_TILE, tn), lambda i, j: (i, j)),
        out_shape=jax.ShapeDtypeStruct((s, n), F32),
        scratch_shapes=[pltpu.VMEM((ROW_TILE, d), BF16)],
        compiler_params=_cparams("parallel", "arbitrary"),
        name="norm_mod_matmul",
    )(x2, g, sc, sh, w)


def _head_rms(x, gain, n_real):
    ss = jnp.sum(x * x, axis=-1, keepdims=True)
    return x * lax.rsqrt(ss * (1.0 / n_real) + EPS) * gain


def _rope64(x, cos, sin):
    lane = _lane_iota(x.shape)
    rot = jnp.where(lane < 32, -pltpu.roll(x, LANE - 32, 1), pltpu.roll(x, 32, 1))
    return x * cos + rot * sin


def _split3(c):
    hi = c.astype(BF16).astype(F32)
    r1 = c - hi
    mid = r1.astype(BF16).astype(F32)
    lo = (r1 - mid).astype(BF16).astype(F32)
    return hi, mid, lo


def _hy_prep_kernel(p_ref, pos_ref, inv_ref, gq_ref, gk_ref, gfq_ref, gfk_ref, fb_ref,
                    qnt_ref, kct_ref, vct_ref, ks_ref, vst_ref, kw_ref, vwt_ref, gate_ref,
                    fqt_ref, fk_ref, fvt_ref, cedge_ref, carry_ref):
    i = pl.program_id(0)
    tm = PREP_TILE
    shp = (tm, LANE)
    lane = _lane_iota(shp)

    def blk(b):
        return p_ref[:, b * LANE:(b + 1) * LANE]

    ang = pos_ref[...] * inv_ref[...]
    real = lane < HEAD_DIM
    cos = jnp.where(real, jnp.cos(ang), 1.0)
    sin = jnp.where(real, jnp.sin(ang), 0.0)
    gq, gk, gfq, gfk = gq_ref[...], gk_ref[...], gfq_ref[...], gfk_ref[...]
    scale = HEAD_DIM ** -0.5 * LOG2E
    ones_row = lane == SUM_ROW
    ref_ones = jnp.where((lane >= REF_ROW) & (lane < REF_ROW + 3), 1.0, 0.0)

    for h in range(NSA_HEADS):
        q = _rope64(_head_rms(blk(HY_Q0 + h), gq, HEAD_DIM), cos, sin) * scale
        qnt_ref[h * LANE:(h + 1) * LANE, :] = q.T.astype(BF16)

    row = _row_iota(shp) + i * tm
    onehot = jnp.where(lane - HEAD_DIM == ((row // SLC_BLOCK) % SUPER_BLOCKS), 1.0, 0.0)
    for g in range(NSA_GROUPS):
        def kv(r):
            return blk(HY_KV0 + r * NSA_GROUPS + g)
        sl = slice(g * LANE, (g + 1) * LANE)
        kct_ref[g] = _rope64(kv(0), cos, sin)[:, :HEAD_DIM].astype(BF16)
        vct_ref[g] = kv(1)[:, :HEAD_DIM].astype(BF16)
        ks = _rope64(_head_rms(kv(2), gk, HEAD_DIM), cos, sin)
        ks_ref[:, sl] = (ks + onehot + ref_ones).astype(BF16)
        vst_ref[g, 0] = jnp.where(ones_row, 1.0, kv(3)).T.astype(BF16)
        kw_ref[:, sl] = _rope64(_head_rms(kv(4), gk, HEAD_DIM), cos, sin).astype(BF16)
        vwt = kv(5).T.astype(BF16)
        for cidx in range(tm // LANE):
            vwt_ref[g, cidx] = vwt[:, cidx * LANE:(cidx + 1) * LANE]
        gate_ref[g] = jax.nn.sigmoid(blk(HY_G0 + g)).T[:GATE_ROWS]

    @pl.when(i == 0)
    def _():
        carry_ref[...] = jnp.zeros_like(carry_ref)

    z = blk(HY_FF) + fb_ref[...]
    logf = jnp.minimum(z, 0.0) - jnp.log1p(jnp.exp(-jnp.abs(z)))
    tri = jnp.where(_row_iota((tm, tm)) >= _lane_iota((tm, tm)), 1.0, 0.0).astype(F32)
    cum = jnp.dot(tri, logf, precision=HIGHEST, preferred_element_type=F32) + carry_ref[...]
    carry_ref[...] = cum[tm - 1:tm, :]
    cedge_ref[0] = jnp.concatenate([cum[0:1] * LOG2E, cum[tm - 1:tm] * LOG2E, jnp.zeros((6, LANE), F32)], axis=0)

    for h in range(FOX_HEADS):
        c = jnp.broadcast_to(cum[:, h:h + 1], shp) * LOG2E
        hi, mid, lo = _split3(c)
        fq = _head_rms(blk(HY_F0 + h), gfq, HEAD_DIM) * scale
        fq = jnp.where(real, fq, jnp.where(lane == 64, hi, jnp.where(lane == 65, mid, jnp.where(
            lane == 66, lo, jnp.where(lane < 70, 1.0, 0.0)))))
        fk = _head_rms(blk(HY_F0 + FOX_HEADS + h), gfk, HEAD_DIM)
        fk = jnp.where(real, fk, jnp.where(lane < 67, 1.0, jnp.where(lane == 67, -hi, jnp.where(
            lane == 68, -mid, jnp.where(lane == 69, -lo, ref_ones)))))
        sl = slice(h * LANE, (h + 1) * LANE)
        fqt_ref[sl, :] = fq.T.astype(BF16)
        fk_ref[:, sl] = fk.astype(BF16)
        fvt_ref[h, 0] = jnp.where(ones_row, 1.0, blk(HY_F0 + 2 * FOX_HEADS + h)).T.astype(BF16)


def _feat_major(heads, s, tm):
    return (pl.BlockSpec((heads * LANE, tm), lambda i: (0, i)),
            jax.ShapeDtypeStruct((heads * LANE, s), BF16))


def _value_tiles(heads, s, tm, tk):
    return (pl.BlockSpec((heads, tm // tk, LANE, tk), lambda i: (0, i, 0, 0)),
            jax.ShapeDtypeStruct((heads, s // tk, LANE, tk), BF16))


def _hy_prep(proj, posf, inv128, gq, gk, gfq, gfk, fbias):
    s = proj.shape[0]
    tm = PREP_TILE
    assert tm == KV_TILE
    vec = pl.BlockSpec((1, LANE), lambda i: (0, 0))

    def rows(nb):
        return (pl.BlockSpec((tm, nb * LANE), lambda i: (i, 0)), jax.ShapeDtypeStruct((s, nb * LANE), BF16))

    tok = (pl.BlockSpec((NSA_GROUPS, tm, HEAD_DIM), lambda i: (0, i, 0)),
           jax.ShapeDtypeStruct((NSA_GROUPS, s, HEAD_DIM), BF16))
    gate = (pl.BlockSpec((NSA_GROUPS, GATE_ROWS, tm), lambda i: (0, 0, i)),
            jax.ShapeDtypeStruct((NSA_GROUPS, GATE_ROWS, s), F32))
    outs = [_feat_major(NSA_HEADS, s, tm), tok, tok, rows(NSA_GROUPS), _value_tiles(NSA_GROUPS, s, tm, KV_TILE),
            rows(NSA_GROUPS), _value_tiles(NSA_GROUPS, s, tm, LANE), gate,
            _feat_major(FOX_HEADS, s, tm), rows(FOX_HEADS), _value_tiles(FOX_HEADS, s, tm, KV_TILE),
            (pl.BlockSpec((1, 8, LANE), lambda i: (i, 0, 0)), jax.ShapeDtypeStruct((s // tm, 8, LANE), F32))]
    return pl.pallas_call(
        _hy_prep_kernel,
        grid=(s // tm,),
        in_specs=[pl.BlockSpec((tm, HY_BLOCKS * LANE), lambda i: (i, 0)), pl.BlockSpec((tm, 1), lambda i: (i, 0)),
                  vec, vec, vec, vec, vec, vec],
        out_specs=[o[0] for o in outs],
        out_shape=[o[1] for o in outs],
        scratch_shapes=[pltpu.VMEM((1, LANE), F32)],
        compiler_params=_cparams("arbitrary"),
        name="hybrid_prep",
    )(proj, posf, inv128, gq, gk, gfq, gfk, fbias)


def _compress_kernel(kc_ref, vc_ref, wk_ref, wv_ref, pek_ref, pev_ref, gk_ref, ko_ref, vo_ref):
    half = CMP_STRIDE * HEAD_DIM

    def comp(ch_ref, w_ref, pe_ref):
        ch = ch_ref[0]
        nc = ch.shape[0]
        a = jnp.dot(ch, w_ref[:half], preferred_element_type=F32)
        b = jnp.dot(ch, w_ref[half:], preferred_element_type=F32)
        nxt = pltpu.roll(b, nc - 1, 0)
        pe = jnp.dot(jnp.broadcast_to(pe_ref[...], (8, 2 * half)).astype(BF16), w_ref[...],
                     preferred_element_type=F32)[0:1]
        return a + nxt + pe

    ko_ref[0] = _head_rms(comp(kc_ref, wk_ref, pek_ref), gk_ref[...], HEAD_DIM).astype(BF16)
    vo_ref[0] = comp(vc_ref, wv_ref, pev_ref).T.astype(BF16)


def _compress(kct, vct, w_cmp, cmp_pe, k_norm):
    g, s, _ = kct.shape
    nc = s // CMP_STRIDE
    wide = CMP_STRIDE * HEAD_DIM
    kch = kct.reshape(g, nc, wide)
    vch = vct.reshape(g, nc, wide)
    w_pad = jnp.pad(w_cmp, ((0, 0), (0, 0), (0, LANE - HEAD_DIM))).astype(BF16)
    ch = pl.BlockSpec((1, nc, wide), lambda i: (i, 0, 0))
    wspec = pl.BlockSpec((2 * wide, LANE), lambda i: (0, 0))
    pespec = pl.BlockSpec((1, 2 * wide), lambda i: (0, 0))
    return pl.pallas_call(
        _compress_kernel,
        grid=(g,),
        in_specs=[ch, ch, wspec, wspec, pespec, pespec, pl.BlockSpec((1, LANE), lambda i: (0, 0))],
        out_specs=[pl.BlockSpec((1, nc, LANE), lambda i: (i, 0, 0)),
                   pl.BlockSpec((1, LANE, nc), lambda i: (i, 0, 0))],
        out_shape=[jax.ShapeDtypeStruct((g, nc, LANE), BF16), jax.ShapeDtypeStruct((g, LANE, nc), BF16)],
        compiler_params=_cparams("parallel"),
        name="nsa_compress",
    )(kch, vch, w_pad[0], w_pad[1], cmp_pe[0].reshape(1, 2 * wide).astype(F32),
      cmp_pe[1].reshape(1, 2 * wide).astype(F32), _pad_lanes(k_norm))


def _masked_softmax_t(s, mask):
    s = jnp.where(mask, s, NEG)
    m = jnp.max(s, axis=0, keepdims=True)
    e = jnp.where(mask, jnp.exp2(s - m), 0.0)
    return e / jnp.maximum(jnp.sum(e, axis=0, keepdims=True), 1e-30)


def _online_steps(steps, ms, acc_ref):
    ms = list(ms)

    def scores(step):
        k_tile, qa, _, c, mask = step
        s = jnp.dot(k_tile, qa, preferred_element_type=F32)
        if mask is not None:
            s = jnp.where(mask, s, NEG)
        return s, jnp.max(s, axis=0, keepdims=True)

    nxt = scores(steps[0])
    for idx, (_, _, vt, c, _) in enumerate(steps):
        sl = slice(c * Q_STRIP, (c + 1) * Q_STRIP)
        s, s_max = nxt
        if idx + 1 < len(steps):
            nxt = scores(steps[idx + 1])
        m_new = jnp.maximum(ms[c], s_max)
        a = jnp.exp2(ms[c] - m_new)
        p = jnp.exp2((s - m_new).astype(BF16))
        ms[c] = m_new
        acc_ref[:, sl] = a * acc_ref[:, sl] + jnp.dot(vt, p, preferred_element_type=F32)
    return tuple(ms)


def _m_init(n_strips):
    return tuple(jnp.full((1, Q_STRIP), NEG, F32) for _ in range(n_strips))


def _with_ref_rows(qa, m):
    hi, mid, lo = _split3(-m)
    r = _row_iota((LANE - REF_SLAB, Q_STRIP)) + REF_SLAB
    slab = jnp.where(r == REF_ROW, hi, jnp.where(r == REF_ROW + 1, mid, jnp.where(r == REF_ROW + 2, lo, 0.0)))
    return jnp.concatenate([qa[:REF_SLAB], slab.astype(BF16)], axis=0)


def _first_tile_max(k_tile, qa_strips, masks):
    return tuple(jnp.max(jnp.where(mask, jnp.dot(k_tile, qa, preferred_element_type=F32), NEG), axis=0, keepdims=True)
                 for qa, mask in zip(qa_strips, masks))


def _fast_steps(steps, state, acc_ref):
    state = list(state)
    for a, b in zip(steps[:-1], steps[1:]):
        assert a[3] != b[3]

    def scores(step):
        k_tile, qa, _, c, mask = step
        s = jnp.dot(k_tile, _with_ref_rows(qa, state[c][0]), preferred_element_type=F32)
        if mask is not None:
            s = jnp.where(mask, s, NEG)
        return s

    s_next = scores(steps[0])
    for idx, (_, _, vt, c, _) in enumerate(steps):
        sl = slice(c * Q_STRIP, (c + 1) * Q_STRIP)
        s = s_next
        m, worst = state[c]
        cm = jnp.max(s, axis=0, keepdims=True)
        inc = jnp.maximum(cm, 0.0)
        state[c] = (m + inc, jnp.maximum(worst, cm))
        if idx + 1 < len(steps):
            s_next = scores(steps[idx + 1])
        p = jnp.exp2(s).astype(BF16)
        acc_ref[:, sl] = jnp.exp2(-inc) * (acc_ref[:, sl] + jnp.dot(vt, p, preferred_element_type=F32))
    return tuple(state)


def _flat(state):
    return tuple(x for pair in state for x in pair)


def _nest(flat):
    return tuple((flat[2 * c], flat[2 * c + 1]) for c in range(len(flat) // 2))


def _nsa_kernel(qt_ref, kc_ref, vct_ref, ks_ref, vst_ref, kw_ref, vwt_ref, gate_ref, ovt_ref, o_ref,
                qaug_ref, acc_ref, *, n_sel):
    i = pl.program_id(1)
    tq = Q_TILE_NSA
    cols = NSA_HPG * tq
    qs = i * tq
    nc = kc_ref.shape[1]
    nslc = ovt_ref.shape[0]
    n_super = nslc // SUPER_BLOCKS

    qt = jnp.concatenate([qt_ref[h * LANE:(h + 1) * LANE, :] for h in range(NSA_HPG)], axis=1)
    tq_row = qs + (_lane_iota((1, cols)) % tq)

    s = jnp.dot(kc_ref[0], qt, preferred_element_type=F32)
    cmp_end = _row_iota((nc, 1)) * CMP_STRIDE + (CMP_LEN - 1)
    p = _masked_softmax_t(s, cmp_end <= tq_row)
    o_cmp = jnp.dot(vct_ref[0], p.astype(BF16), preferred_element_type=F32)

    psum = p[:, 0:tq]
    for h in range(1, NSA_HPG):
        psum = psum + p[:, h * tq:(h + 1) * tq]
    p_hi = psum.astype(BF16)
    p_lo = (psum - p_hi.astype(F32)).astype(BF16)
    ovt = ovt_ref[...]
    imp = (jnp.dot(ovt, p_hi, preferred_element_type=F32)
           + jnp.dot(ovt, p_lo, preferred_element_type=F32))

    jj = _row_iota((nslc, tq))
    tq_blk = qs + _lane_iota((nslc, tq))
    cur = tq_blk // SLC_BLOCK
    forced = (jj == 0) | (jj == cur) | (jj == cur - 1)
    causal_blk = jj * SLC_BLOCK <= tq_blk
    val = jnp.where(forced, imp + BIG, imp)
    val = jnp.where(causal_blk, val, NEG)

    jjf = jj.astype(F32)

    def pick(_, carry):
        val, sel = carry
        mx = jnp.max(val, axis=0, keepdims=True)
        idx = jnp.min(jnp.where(val == mx, jjf, float(nslc)), axis=0, keepdims=True)
        hit = jjf == idx
        return jnp.where(hit, -jnp.inf, val), jnp.where(hit, 1.0, sel)

    _, sel = lax.fori_loop(0, n_sel, pick, (val, jnp.zeros((nslc, tq), F32)))
    bias_t = jnp.where((sel > 0.0) & causal_blk, 0.0, NEG)

    q_rows = qt[:HEAD_DIM].astype(F32)
    spare = jnp.zeros((LANE - HEAD_DIM - SUPER_BLOCKS, cols), F32)
    for st in range(n_super):
        b = bias_t[st * SUPER_BLOCKS:(st + 1) * SUPER_BLOCKS]
        b = jnp.concatenate([b] * NSA_HPG, axis=1)
        qaug_ref[st] = jnp.concatenate([q_rows, b, spare], axis=0).astype(BF16)

    tk = KV_TILE
    per_super = SUPER_BLOCKS * SLC_BLOCK // tk
    j_last = (qs + tq - 1) // tk
    n_strips = cols // Q_STRIP
    strips = [slice(c * Q_STRIP, (c + 1) * Q_STRIP) for c in range(n_strips)]

    def causal_masks(j):
        kpos = j * tk + _row_iota((tk, 1))
        return [kpos <= tq_row[:, sl] for sl in strips]

    def slc_steps(j, masks=None):
        k0 = pl.multiple_of(j * tk, tk)
        k_tile, vt, st = ks_ref[pl.ds(k0, tk), :], vst_ref[0, j], j // per_super
        return [(k_tile, qaug_ref[st, :, strips[c]], vt, c, None if masks is None else masks[c])
                for c in range(n_strips)]

    def pair(jj, flat):
        return _flat(_fast_steps(slc_steps(2 * jj) + slc_steps(2 * jj + 1), _nest(flat), acc_ref))

    def single(j, flat):
        return _flat(_fast_steps(slc_steps(j), _nest(flat), acc_ref))

    acc_ref[...] = jnp.zeros(acc_ref.shape, F32)
    m0 = _first_tile_max(ks_ref[0:tk, :], [qaug_ref[0, :, sl] for sl in strips], causal_masks(0))
    n_pairs = j_last // 2
    flat = lax.fori_loop(0, n_pairs, pair, _flat(tuple((m, jnp.zeros_like(m)) for m in m0)))
    flat = lax.fori_loop(2 * n_pairs, j_last, single, flat)
    state = _fast_steps(slc_steps(j_last, causal_masks(j_last)), _nest(flat), acc_ref)
    worst = jnp.max(jnp.concatenate([w for _, w in state], axis=1))

    @pl.when(worst > EXP_GUARD)
    def _():
        acc_ref[...] = jnp.zeros(acc_ref.shape, F32)
        lax.fori_loop(0, j_last + 1, lambda j, ms: _online_steps(slc_steps(j, causal_masks(j)), ms, acc_ref),
                      _m_init(n_strips))

    acc = acc_ref[...]
    o_slc = acc / jnp.maximum(acc[SUM_ROW:SUM_ROW + 1], 1e-30)

    wlen = WINDOW + tq
    ws = pl.multiple_of(jnp.maximum(qs - WINDOW, 0), tq)
    s = jnp.dot(kw_ref[pl.ds(ws, wlen), :], qt, preferred_element_type=F32)
    dist = tq_row - (ws + _row_iota((wlen, 1)))
    p = _masked_softmax_t(s, (dist >= 0) & (dist < WINDOW)).astype(BF16)
    wb = ws // LANE
    o_win = jnp.zeros((LANE, cols), F32)
    for c in range(wlen // LANE):
        o_win = o_win + jnp.dot(vwt_ref[0, wb + c], p[c * LANE:(c + 1) * LANE], preferred_element_type=F32)

    gate = gate_ref[0]
    for h in range(NSA_HPG):
        sl = slice(h * tq, (h + 1) * tq)
        o = (gate[3 * h:3 * h + 1] * o_cmp[:, sl] + gate[3 * h + 1:3 * h + 2] * o_slc[:, sl]
             + gate[3 * h + 2:3 * h + 3] * o_win[:, sl])
        o_ref[:, h * LANE:(h + 1) * LANE] = o.T.astype(BF16)


def _overlap_t(s, nslc_pad):
    nc = s // CMP_STRIDE
    cmp_start = np.arange(nc) * CMP_STRIDE
    slc_start = np.arange(nslc_pad) * SLC_BLOCK
    ov = np.clip(np.minimum(cmp_start[:, None] + CMP_LEN, slc_start[None, :] + SLC_BLOCK)
                 - np.maximum(cmp_start[:, None], slc_start[None, :]), 0, None) / CMP_STRIDE
    ov[nc - CMP_LEN // CMP_STRIDE + 1:, :] = 0.0
    ov[:, s // SLC_BLOCK:] = 0.0
    return jnp.asarray(ov.T, BF16)


def _nsa_attention(qnt, kc, vct, ks, vst, kw, vwt, gates):
    s = qnt.shape[1]
    nc = s // CMP_STRIDE
    n_slc = s // SLC_BLOCK
    nslc_pad = -(-n_slc // LANE) * LANE
    tq = Q_TILE_NSA
    cols = NSA_HPG * tq
    once = pl.Buffered(1)
    res = pl.BlockSpec((s, LANE), lambda g, i: (0, g), pipeline_mode=once)
    return pl.pallas_call(
        functools.partial(_nsa_kernel, n_sel=min(SLC_TOPK, n_slc)),
        grid=(NSA_GROUPS, s // tq),
        in_specs=[pl.BlockSpec((NSA_HPG * LANE, tq), lambda g, i: (g, i)),
                  pl.BlockSpec((1, nc, LANE), lambda g, i: (g, 0, 0), pipeline_mode=once),
                  pl.BlockSpec((1, LANE, nc), lambda g, i: (g, 0, 0), pipeline_mode=once),
                  res, pl.BlockSpec((1, s // KV_TILE, LANE, KV_TILE), lambda g, i: (g, 0, 0, 0),
                                    pipeline_mode=once),
                  res, pl.BlockSpec((1, s // LANE, LANE, LANE), lambda g, i: (g, 0, 0, 0), pipeline_mode=once),
                  pl.BlockSpec((1, GATE_ROWS, tq), lambda g, i: (g, 0, i)),
                  pl.BlockSpec((nslc_pad, nc), lambda g, i: (0, 0), pipeline_mode=once)],
        out_specs=pl.BlockSpec((tq, NSA_HPG * LANE), lambda g, i: (i, g)),
        out_shape=jax.ShapeDtypeStruct((s, NSA_HEADS * LANE), BF16),
        scratch_shapes=[pltpu.VMEM((nslc_pad // SUPER_BLOCKS, LANE, cols), BF16),
                        pltpu.VMEM((LANE, cols), F32)],
        compiler_params=_cparams("parallel", "arbitrary"),
        name="nsa_attention",
    )(qnt, kc, vct, ks, vst, kw, vwt, gates, _overlap_t(s, nslc_pad))


def _flash_kernel(cfirst_ref, clast_ref, slack_ref, qt_ref, k_ref, vt_ref, o_ref, acc_ref):
    h = pl.program_id(0)
    i = pl.program_id(1)
    tq, tk = Q_TILE_FLASH, KV_TILE
    n_tiles = k_ref.shape[0] // tk
    acc_ref[...] = jnp.zeros(acc_ref.shape, F32)

    n_strips = tq // Q_STRIP
    per_q = tq // tk
    qas = [qt_ref[:, c * Q_STRIP:(c + 1) * Q_STRIP] for c in range(n_strips)]

    def tile_steps(j, d=None):
        k0 = pl.multiple_of(j * tk, tk)
        k_tile, vt = k_ref[pl.ds(k0, tk), :], vt_ref[0, j]
        steps = []
        for c in range(n_strips):
            mask = None
            if d is not None:
                if d * tk > (c + 1) * Q_STRIP - 1:
                    continue
                if (d + 1) * tk - 1 > c * Q_STRIP:
                    shp = (tk, Q_STRIP)
                    mask = _row_iota(shp) + d * tk <= _lane_iota(shp) + c * Q_STRIP
            steps.append((k_tile, qas[c], vt, c, mask))
        return steps

    def any_tile_masks(j):
        shp = (tk, Q_STRIP)
        return [_row_iota(shp) + j * tk <= _lane_iota(shp) + (i * tq + c * Q_STRIP) for c in range(n_strips)]

    def below(t, flat):
        jj = i - 1 - t
        bound = (slack_ref[0] + cfirst_ref[h * n_tiles + i * per_q]
                 - clast_ref[h * n_tiles + jj * per_q + per_q - 1])

        def run(flat):
            steps = [st for u in range(per_q) for st in tile_steps(jj * per_q + (per_q - 1 - u))]
            return _flat(_fast_steps(steps, _nest(flat), acc_ref))

        return lax.cond(bound >= -SKIP_MARGIN, run, lambda flat: flat, flat)

    diag0 = i * per_q
    own = [(c * Q_STRIP) // tk for c in range(n_strips)]
    m0 = tuple(_first_tile_max(k_ref[pl.ds(pl.multiple_of((diag0 + own[c]) * tk, tk), tk), :], [qas[c]],
                               [any_tile_masks(diag0 + own[c])[c]])[0] for c in range(n_strips))
    steps = [st for d in reversed(range(per_q)) for st in tile_steps(diag0 + d, d)]
    state = _fast_steps(steps, tuple((m, jnp.zeros_like(m)) for m in m0), acc_ref)
    state = _nest(lax.fori_loop(0, i, below, _flat(state)))
    worst = jnp.max(jnp.concatenate([w for _, w in state], axis=1))

    @pl.when(worst > EXP_GUARD)
    def _():
        acc_ref[...] = jnp.zeros(acc_ref.shape, F32)

        def exact(j, ms):
            k0 = pl.multiple_of(j * tk, tk)
            k_tile, vt, masks = k_ref[pl.ds(k0, tk), :], vt_ref[0, j], any_tile_masks(j)
            return _online_steps([(k_tile, qas[c], vt, c, masks[c]) for c in range(n_strips)], ms, acc_ref)

        lax.fori_loop(0, (i + 1) * per_q, exact, _m_init(n_strips))

    acc = acc_ref[...]
    o = acc / acc[SUM_ROW:SUM_ROW + 1]
    for c0 in range(0, tq, LANE):
        o_ref[c0:c0 + LANE, :] = o[:, c0:c0 + LANE].T.astype(BF16)


def _causal_attention(qt, k, vt, bias_edges=None, slack=None):
    s, width = k.shape
    heads = width // LANE
    assert Q_TILE_FLASH % KV_TILE == 0 and s % Q_TILE_FLASH == 0
    n_tiles = s // KV_TILE
    if bias_edges is None:
        first = last = jnp.zeros((heads * n_tiles,), F32)
        slack = jnp.full((1,), -NEG, F32)
    else:
        first, last = (e.reshape(heads * n_tiles).astype(F32) for e in bias_edges)
    grid_spec = pltpu.PrefetchScalarGridSpec(
        num_scalar_prefetch=3,
        grid=(heads, s // Q_TILE_FLASH),
        in_specs=[pl.BlockSpec((LANE, Q_TILE_FLASH), lambda h, i, *_: (h, i)),
                  pl.BlockSpec((s, LANE), lambda h, i, *_: (0, h)),
                  pl.BlockSpec((1, n_tiles, LANE, KV_TILE), lambda h, i, *_: (h, 0, 0, 0))],
        out_specs=pl.BlockSpec((Q_TILE_FLASH, LANE), lambda h, i, *_: (i, h)),
        scratch_shapes=[pltpu.VMEM((LANE, Q_TILE_FLASH), F32)],
    )
    return pl.pallas_call(
        _flash_kernel,
        grid_spec=grid_spec,
        out_shape=jax.ShapeDtypeStruct((s, width), BF16),
        compiler_params=_cparams("parallel", "arbitrary"),
        name="causal_attention",
    )(first, last, slack.astype(F32), qt, k, vt)


def _out_proj_kernel(oa_ref, ob_ref, wa_ref, wb_ref, x_ref, g_ref, o_ref):
    y = jnp.dot(oa_ref[...], wa_ref[...], preferred_element_type=F32)
    y = y + jnp.dot(ob_ref[...], wb_ref[...], preferred_element_type=F32)
    o_ref[...] = x_ref[...] + g_ref[...] * y


def _out_proj(oa, ob, cola, colb, wa, wb, x2, gate):
    s, d = x2.shape
    ka = wa.shape[0]
    tm = ROW_TILE
    return pl.pallas_call(
        _out_proj_kernel,
        grid=(s // tm,),
        in_specs=[pl.BlockSpec((tm, ka), lambda i: (i, cola)), pl.BlockSpec((tm, ka), lambda i: (i, colb)),
                  pl.BlockSpec((ka, d), lambda i: (0, 0)), pl.BlockSpec((ka, d), lambda i: (0, 0)),
                  pl.BlockSpec((tm, d), lambda i: (i, 0)), pl.BlockSpec((1, d), lambda i: (0, 0))],
        out_specs=pl.BlockSpec((tm, d), lambda i: (i, 0)),
        out_shape=jax.ShapeDtypeStruct((s, d), F32),
        compiler_params=_cparams("parallel"),
        name="out_proj",
    )(oa, ob, wa, wb, x2, gate)


def _mla_prep_kernel(p_ref, pos_ref, inv_ref, gqa_ref, gkva_ref, wuq_ref, wuk_ref, wuv_ref,
                     gq_ref, gk_ref, gkr_ref, qt_ref, k_ref, vt_ref):
    shp = (PREP_TILE, LANE)
    lane = _lane_iota(shp)
    nope = lane < QK_NOPE
    rope = (lane >= QK_NOPE) & (lane < QK_NOPE + QK_ROPE)
    ref_ones = jnp.where((lane >= REF_ROW) & (lane < REF_ROW + 3), 1.0, 0.0)
    ang = pos_ref[...] * inv_ref[...]
    cos = jnp.where(rope, jnp.cos(ang), 1.0)
    sin = jnp.where(rope, jnp.sin(ang), 0.0)

    def rope32(x):
        half = QK_ROPE // 2
        rot = jnp.where(lane < QK_NOPE + half, -pltpu.roll(x, LANE - half, 1), pltpu.roll(x, half, 1))
        return x * cos + rot * sin

    def low_rank_norm(x, g):
        ms = jnp.mean(x * x, axis=-1, keepdims=True)
        return (x * lax.rsqrt(ms + EPS) * g).astype(BF16)

    nq = Q_LORA // LANE
    cq = low_rank_norm(p_ref[:, :Q_LORA], gqa_ref[...])
    ckv = low_rank_norm(p_ref[:, Q_LORA:Q_LORA + KV_LORA], gkva_ref[...])
    kr = p_ref[:, (nq + KV_LORA // LANE) * LANE:(nq + KV_LORA // LANE + 1) * LANE]
    k_rope = rope32(_head_rms(kr, gkr_ref[...], QK_ROPE))

    gq, gk = gq_ref[...], gk_ref[...]
    scale = (QK_NOPE + QK_ROPE) ** -0.5 * LOG2E
    pair = 2 * LANE
    for hp in range(MLA_HEADS // 2):
        cols = slice(hp * pair, (hp + 1) * pair)
        q2 = jnp.dot(cq, wuq_ref[:, cols], preferred_element_type=F32)
        k2 = jnp.dot(ckv, wuk_ref[:, cols], preferred_element_type=F32)
        v2 = jnp.dot(ckv, wuv_ref[:, cols], preferred_element_type=F32)
        for sub in range(2):
            head = 2 * hp + sub
            sl = slice(head * LANE, (head + 1) * LANE)
            half = slice(sub * LANE, (sub + 1) * LANE)
            x = q2[:, half]
            ss_n = jnp.sum(jnp.where(nope, x * x, 0.0), axis=-1, keepdims=True)
            ss_r = jnp.sum(jnp.where(rope, x * x, 0.0), axis=-1, keepdims=True)
            inv_rms = jnp.where(nope, lax.rsqrt(ss_n * (1.0 / QK_NOPE) + EPS),
                                lax.rsqrt(ss_r * (1.0 / QK_ROPE) + EPS))
            qt_ref[sl, :] = (rope32(x * inv_rms * gq) * scale).T.astype(BF16)
            kn = _head_rms(k2[:, half], gk, QK_NOPE)
            k_ref[:, sl] = (kn + k_rope + ref_ones).astype(BF16)
            vt_ref[head, 0] = jnp.where(lane == SUM_ROW, 1.0, v2[:, half]).T.astype(BF16)


def _mla_prep(proj, posf, inv128, gqa, gkva, wuq, wuk, wuv, gq, gk, gkr):
    s, n = proj.shape
    tm = PREP_TILE
    assert tm == KV_TILE

    def full(a):
        return pl.BlockSpec(a.shape, lambda i: (0, 0))

    outs = [_feat_major(MLA_HEADS, s, tm),
            (pl.BlockSpec((tm, MLA_HEADS * LANE), lambda i: (i, 0)),
             jax.ShapeDtypeStruct((s, MLA_HEADS * LANE), BF16)),
            _value_tiles(MLA_HEADS, s, tm, KV_TILE)]
    args = (inv128, gqa, gkva, wuq, wuk, wuv, gq, gk, gkr)
    return pl.pallas_call(
        _mla_prep_kernel,
        grid=(s // tm,),
        in_specs=[pl.BlockSpec((tm, n), lambda i: (i, 0)), pl.BlockSpec((tm, 1), lambda i: (i, 0))]
                 + [full(a) for a in args],
        out_specs=[o[0] for o in outs],
        out_shape=[o[1] for o in outs],
        compiler_params=_cparams("parallel"),
        name="mla_prep",
    )(proj, posf, *args)


def _rank_lt(v, k):
    n = v.shape[0]
    row = _row_iota(v.shape)
    rank = jnp.zeros(v.shape, F32)
    for b in range(n):
        vb = v[b:b + 1, :]
        rank = rank + jnp.where((vb > v) | ((vb == v) & (row > b)), 1.0, 0.0)
    return rank < k


def _moe_route_kernel(x_ref, g_ref, sc_ref, sh_ref, wr_ref, rb_ref, h_ref, pos_ref, wt_ref, cnt_ref):
    tm = ROW_TILE
    h = _norm_mod(x_ref[...], g_ref[...], sc_ref[...], sh_ref[...])
    h_ref[...] = h.astype(BF16)
    logits = jnp.dot(h, wr_ref[...], precision=HIGHEST, preferred_element_type=F32)
    lt = logits.T[:N_EXPERTS]
    scores = jax.nn.sigmoid(lt)
    sel = scores + rb_ref[...]

    per = N_EXPERTS // N_GROUPS
    grp = sel.reshape(N_GROUPS, per, tm)
    sub = lax.broadcasted_iota(jnp.int32, grp.shape, 1)
    m1 = jnp.max(grp, axis=1, keepdims=True)
    first = jnp.min(jnp.where(grp == m1, sub, per), axis=1, keepdims=True)
    m2 = jnp.max(jnp.where(sub == first, -jnp.inf, grp), axis=1, keepdims=True)
    gscore = (m1 + m2).reshape(N_GROUPS, tm)
    gmask = _rank_lt(gscore, TOPK_GROUPS)
    emask = jnp.broadcast_to(gmask.reshape(N_GROUPS, 1, tm), grp.shape).reshape(N_EXPERTS, tm)
    chosen = _rank_lt(jnp.where(emask, sel, NEG), TOP_K)

    w = jnp.where(chosen, scores, 0.0)
    wt_ref[...] = w / jnp.sum(w, axis=0, keepdims=True) * ROUTED_SCALE

    upper = jnp.where(_row_iota((tm, tm)) <= _lane_iota((tm, tm)), 1.0, 0.0).astype(BF16)
    incl = jnp.dot(jnp.where(chosen, 1.0, 0.0).astype(BF16), upper, preferred_element_type=F32)
    pos_ref[...] = jnp.where(chosen, incl - 1.0, -1.0)
    cnt_ref[0] = jnp.broadcast_to(incl[:, tm - 1:tm], (N_EXPERTS, LANE))


def _moe_route(x2, g, sc, sh, w_router_pad, router_bias_col):
    s, d = x2.shape
    tm = ROW_TILE
    vec = pl.BlockSpec((1, d), lambda i: (0, 0))
    et = pl.BlockSpec((N_EXPERTS, tm), lambda i: (0, i))
    return pl.pallas_call(
        _moe_route_kernel,
        grid=(s // tm,),
        in_specs=[pl.BlockSpec((tm, d), lambda i: (i, 0)), vec, vec, vec,
                  pl.BlockSpec((d, LANE), lambda i: (0, 0)),
                  pl.BlockSpec((N_EXPERTS, 1), lambda i: (0, 0))],
        out_specs=[pl.BlockSpec((tm, d), lambda i: (i, 0)), et, et,
                   pl.BlockSpec((1, N_EXPERTS, LANE), lambda i: (i, 0, 0))],
        out_shape=[jax.ShapeDtypeStruct((s, d), BF16), jax.ShapeDtypeStruct((N_EXPERTS, s), F32),
                   jax.ShapeDtypeStruct((N_EXPERTS, s), F32),
                   jax.ShapeDtypeStruct((s // tm, N_EXPERTS, LANE), F32)],
        compiler_params=_cparams("parallel"),
        name="moe_route",
    )(x2, g, sc, sh, w_router_pad, router_bias_col)


def _moe_kernel(cnt_ref, x_ref, h_ref, pos_ref, wt_ref, wg_ref, wu_ref, wd_ref, sg_ref, su_ref, sd_ref,
                g2_ref, o_ref, acc_ref):
    i = pl.program_id(0)
    e = pl.program_id(1)
    tm = ROW_TILE
    r = MOE_CHUNK

    @pl.when(e == 0)
    def _():
        h = h_ref[...]
        a = jnp.dot(h, sg_ref[...], preferred_element_type=F32)
        a = a * jax.nn.sigmoid(a) * jnp.dot(h, su_ref[...], preferred_element_type=F32)
        acc_ref[...] = jnp.dot(a.astype(BF16), sd_ref[...], preferred_element_type=F32)

    first = e * MOE_EXPERTS_PER_STEP
    n = cnt_ref[i * N_EXPERTS + first]
    for k in range(1, MOE_EXPERTS_PER_STEP):
        n = jnp.maximum(n, cnt_ref[i * N_EXPERTS + first + k])
    prows = [pos_ref[pl.ds(first + k, 1), :] for k in range(MOE_EXPERTS_PER_STEP)]
    wrows = [wt_ref[pl.ds(first + k, 1), :] for k in range(MOE_EXPERTS_PER_STEP)]

    def chunk(c, _):
        slot = (_row_iota((r, tm)) + c * r).astype(F32)
        hits = [prow == slot for prow in prows]
        onehot = jnp.concatenate([jnp.where(hit, 1.0, 0.0).astype(BF16) for hit in hits], axis=0)
        xg = jnp.dot(onehot, h_ref[...], preferred_element_type=F32).astype(BF16)
        ys = []
        for k in range(MOE_EXPERTS_PER_STEP):
            xk = xg[k * r:(k + 1) * r]
            a = jnp.dot(xk, wg_ref[k], preferred_element_type=F32)
            a = a * jax.nn.sigmoid(a) * jnp.dot(xk, wu_ref[k], preferred_element_type=F32)
            y = jnp.dot(a.astype(BF16), wd_ref[k], preferred_element_type=F32)
            wr = jnp.sum(jnp.where(hits[k], wrows[k], 0.0), axis=-1, keepdims=True)
            ys.append((y * wr).astype(BF16))
        acc_ref[...] += _dot_tn(onehot, jnp.concatenate(ys, axis=0))
        return 0

    lax.fori_loop(0, (n + r - 1) // r, chunk, 0)

    @pl.when(e == N_EXPERTS // MOE_EXPERTS_PER_STEP - 1)
    def _():
        o_ref[...] = x_ref[...] + g2_ref[...] * acc_ref[...]


def _moe_experts(counts, x2, h, pos_t, w_t, wg, wu, wd, sg, su, sd, g2):
    s, d = x2.shape
    tm = ROW_TILE
    ff = wg.shape[2]
    tile = pl.BlockSpec((tm, d), lambda i, e, c: (i, 0))
    et = pl.BlockSpec((N_EXPERTS, tm), lambda i, e, c: (0, i))

    def const(a):
        return pl.BlockSpec(a.shape, lambda i, e, c: (0,) * a.ndim)

    per = MOE_EXPERTS_PER_STEP
    grid_spec = pltpu.PrefetchScalarGridSpec(
        num_scalar_prefetch=1,
        grid=(s // tm, N_EXPERTS // per),
        in_specs=[tile, tile, et, et,
                  pl.BlockSpec((per, d, ff), lambda i, e, c: (e, 0, 0)),
                  pl.BlockSpec((per, d, ff), lambda i, e, c: (e, 0, 0)),
                  pl.BlockSpec((per, ff, d), lambda i, e, c: (e, 0, 0)),
                  const(sg), const(su), const(sd), const(g2)],
        out_specs=tile,
        scratch_shapes=[pltpu.VMEM((tm, d), F32)],
    )
    return pl.pallas_call(
        _moe_kernel,
        grid_spec=grid_spec,
        out_shape=jax.ShapeDtypeStruct((s, d), F32),
        compiler_params=_cparams("parallel", "arbitrary"),
        name="moe_experts",
    )(counts, x2, h, pos_t, w_t, wg, wu, wd, sg, su, sd, g2)


def _pad_lanes(v, width=LANE, offset=0):
    out = jnp.zeros((1, width), F32)
    return out.at[0, offset:offset + v.shape[0]].set(v.astype(F32))


def _head_cols(w, n_heads, dim):
    d = w.shape[0]
    w3 = w.reshape(d, n_heads, dim)
    return jnp.pad(w3, ((0, 0), (0, 0), (0, LANE - dim))).reshape(d, n_heads * LANE)


def _hybrid_w_in(w_in):
    d = w_in.shape[0]
    nq = NSA_HEADS * HEAD_DIM
    nkv = 6 * NSA_GROUPS * HEAD_DIM
    ng = 3 * NSA_HEADS
    nf = 3 * FOX_HEADS * HEAD_DIM
    c0, c1, c2, c3 = nq, nq + nkv, nq + nkv + ng, nq + nkv + ng + nf
    gates = w_in[:, c1:c2].reshape(d, NSA_GROUPS, 3 * NSA_HPG)
    gates = jnp.pad(gates, ((0, 0), (0, 0), (0, LANE - 3 * NSA_HPG))).reshape(d, NSA_GROUPS * LANE)
    ff = jnp.pad(w_in[:, c3:], ((0, 0), (0, 2 * LANE - FOX_HEADS)))
    return jnp.concatenate([
        _head_cols(w_in[:, :c0], NSA_HEADS, HEAD_DIM),
        _head_cols(w_in[:, c0:c1], 6 * NSA_GROUPS, HEAD_DIM),
        _head_cols(w_in[:, c2:c3], 3 * FOX_HEADS, HEAD_DIM),
        gates, ff], axis=1).astype(BF16)


def _pad_head_rows(w, n_heads, dim):
    d = w.shape[1]
    w3 = w.reshape(n_heads, dim, d)
    return jnp.pad(w3, ((0, 0), (0, LANE - dim), (0, 0))).reshape(n_heads * LANE, d).astype(BF16)


def _rope_inv(dim, offset):
    inv = ROPE_THETA ** (-jnp.arange(0, dim, 2, dtype=F32) / dim)
    return _pad_lanes(jnp.concatenate([inv, inv]), offset=offset)


def _hybrid_mixer(x2, posf, mods, norm_g, w_in, fox_f_bias, nsa_q_norm, nsa_k_norm, nsa_cmp_pe, nsa_w_cmp,
                  fox_q_norm, fox_k_norm, w_out):
    sh1, sc1, g1 = mods
    proj = _norm_mod_matmul(x2, norm_g, sc1, sh1, _hybrid_w_in(w_in), tn=HY_COL_TILE)
    (qnt, kct, vct, ks, vst, kw, vwt, gates, fqt, fk, fvt, cedge) = _hy_prep(
        proj, posf, _rope_inv(HEAD_DIM, 0), _pad_lanes(nsa_q_norm), _pad_lanes(nsa_k_norm),
        _pad_lanes(fox_q_norm), _pad_lanes(fox_k_norm), _pad_lanes(fox_f_bias))
    kc, vc_t = _compress(kct, vct, nsa_w_cmp, nsa_cmp_pe, nsa_k_norm)
    o_a = _nsa_attention(qnt, kc, vc_t, ks, vst, kw, vwt, gates)
    slack = (2.0 * HEAD_DIM ** 0.5 * LOG2E) * jnp.max(jnp.abs(fox_q_norm)) * jnp.max(jnp.abs(fox_k_norm))
    edges = (cedge[:, 0, :FOX_HEADS].T, cedge[:, 1, :FOX_HEADS].T)
    o_b = _causal_attention(fqt, fk, fvt, edges, slack.reshape(1))
    half = NSA_HEADS * HEAD_DIM
    wa = _pad_head_rows(w_out[:half], NSA_HEADS, HEAD_DIM)
    wb = _pad_head_rows(w_out[half:], FOX_HEADS, HEAD_DIM)
    return _out_proj(o_a, o_b, 0, 0, wa, wb, x2, g1)


def _mla_mixer(x2, posf, mods, norm_g, w_in, q_a_norm, kv_a_norm, w_uq, w_ukv, qn_norm, kn_norm, qr_norm,
               kr_norm, w_out):
    sh1, sc1, g1 = mods
    d = x2.shape[1]
    w_kr = jnp.zeros((d, LANE), F32).at[:, QK_NOPE:QK_NOPE + QK_ROPE].set(w_in[:, Q_LORA + KV_LORA:])
    w_in_p = jnp.concatenate([w_in[:, :Q_LORA + KV_LORA], w_kr], axis=1).astype(BF16)
    proj = _norm_mod_matmul(x2, norm_g, sc1, sh1, w_in_p, tn=w_in_p.shape[1])
    hq = QK_NOPE + QK_ROPE
    wuq = _head_cols(w_uq, MLA_HEADS, hq).astype(BF16)
    wkv3 = w_ukv.reshape(KV_LORA, MLA_HEADS, QK_NOPE + V_HEAD)
    wuk = _head_cols(wkv3[:, :, :QK_NOPE].reshape(KV_LORA, -1), MLA_HEADS, QK_NOPE).astype(BF16)
    wuv = _head_cols(wkv3[:, :, QK_NOPE:].reshape(KV_LORA, -1), MLA_HEADS, V_HEAD).astype(BF16)
    gq = _pad_lanes(jnp.concatenate([qn_norm, qr_norm]))
    qt, k, vt = _mla_prep(proj, posf, _rope_inv(QK_ROPE, QK_NOPE), q_a_norm.reshape(1, -1).astype(F32),
                          kv_a_norm.reshape(1, -1).astype(F32), wuq, wuk, wuv, gq, _pad_lanes(kn_norm),
                          _pad_lanes(kr_norm, offset=QK_NOPE))
    o = _causal_attention(qt, k, vt)
    w_pad = _pad_head_rows(w_out, MLA_HEADS, V_HEAD)
    half = w_pad.shape[0] // 2
    return _out_proj(o, o, 0, 1, w_pad[:half], w_pad[half:], x2, g1)


def _moe_ffn(x2, mods, norm_g, w_router, router_bias, w_gate, w_up, w_down, ws_gate, ws_up, ws_down):
    sh2, sc2, g2 = mods
    w_r = jnp.pad(w_router.astype(F32), ((0, 0), (0, LANE - N_EXPERTS)))
    h, pos_t, w_t, cnt = _moe_route(x2, norm_g, sc2, sh2, w_r, router_bias.reshape(N_EXPERTS, 1).astype(F32))
    counts = cnt[:, :, 0].astype(jnp.int32).reshape(-1)
    return _moe_experts(counts, x2, h, pos_t, w_t, w_gate.astype(BF16), w_up.astype(BF16),
                        w_down.astype(BF16), ws_gate.astype(BF16), ws_up.astype(BF16), ws_down.astype(BF16), g2)


def kernel(x, c, positions, norm_attn, norm_ffn, w_ada, b_ada, hy_w_in, fox_f_bias, nsa_q_norm, nsa_k_norm, nsa_cmp_pe, nsa_w_cmp, fox_q_norm, fox_k_norm, hy_w_out, mla_w_in, mla_q_a_norm, mla_kv_a_norm, mla_w_uq, mla_w_ukv, mla_qn_norm, mla_kn_norm, mla_qr_norm, mla_kr_norm, mla_w_out, moe_w_router, moe_router_bias, moe_w_gate, moe_w_up, moe_w_down, moe_ws_gate, moe_ws_up, moe_ws_down):
    b, s, d = x.shape
    assert b == 1 and s % KV_TILE == 0 and s >= WINDOW + Q_TILE_NSA
    depth = w_ada.shape[0]
    x2 = x.reshape(s, d).astype(F32)
    posf = positions.reshape(s, 1).astype(F32)
    mod = _ada_mod(c.astype(F32), w_ada.astype(F32), b_ada.astype(F32))

    for layer in range(depth):
        m = [mod[layer, :, k * d:(k + 1) * d] for k in range(6)]
        i = layer // 2
        g_attn = norm_attn[layer].reshape(1, d).astype(F32)
        if layer % 2 == 0:
            x2 = _hybrid_mixer(x2, posf, m[0:3], g_attn, hy_w_in[i], fox_f_bias[i], nsa_q_norm[i],
                               nsa_k_norm[i], nsa_cmp_pe[i], nsa_w_cmp[i], fox_q_norm[i], fox_k_norm[i],
                               hy_w_out[i])
        else:
            x2 = _mla_mixer(x2, posf, m[0:3], g_attn, mla_w_in[i], mla_q_a_norm[i], mla_kv_a_norm[i],
                            mla_w_uq[i], mla_w_ukv[i], mla_qn_norm[i], mla_kn_norm[i], mla_qr_norm[i],
                            mla_kr_norm[i], mla_w_out[i])
        x2 = _moe_ffn(x2, m[3:6], norm_ffn[layer].reshape(1, d).astype(F32), moe_w_router[layer],
                      moe_router_bias[layer], moe_w_gate[layer], moe_w_up[layer], moe_w_down[layer],
                      moe_ws_gate[layer], moe_ws_up[layer], moe_ws_down[layer])
    return x2.reshape(b, s, d)
```

```python
import functools

import numpy as np
import jax
import jax.numpy as jnp
from jax import lax
from jax.experimental import pallas as pl
from jax.experimental.pallas import tpu as pltpu

F32 = jnp.float32
BF16 = jnp.bfloat16
HIGHEST = lax.Precision.HIGHEST

LANE = 128
VMEM_LIMIT_BYTES = 56 * 1024 * 1024

HEAD_DIM = 64
NSA_HEADS = 8
NSA_GROUPS = 2
NSA_HPG = NSA_HEADS // NSA_GROUPS
CMP_LEN = 32
CMP_STRIDE = 16
SLC_BLOCK = 64
SLC_TOPK = 16
WINDOW = 512
FOX_HEADS = 8
MLA_HEADS = 16
Q_LORA = 384
KV_LORA = 256
QK_NOPE = 64
QK_ROPE = 32
V_HEAD = 64
N_EXPERTS = 64
TOP_K = 8
N_GROUPS = 8
TOPK_GROUPS = 4
EXPERT_FF = 256
ROUTED_SCALE = 2.5
ROPE_THETA = 10000.0
EPS = 1e-6
NEG = -1e30
BIG = 1e6

ROW_TILE = 512
PREP_TILE = 512
GATE_ROWS = 16
Q_TILE_NSA = 256
KV_TILE = 512
Q_TILE_FLASH = 1024
Q_STRIP = 256
SUPER_BLOCKS = 32
SUM_ROW = 64
REF_ROW = 104
REF_SLAB = 96
EXP_GUARD = 100.0
SCORE_LOOKAHEAD = 3
SKIP_MARGIN = 160.0
LOG2E = 1.4426950408889634
MOE_CHUNK = 128
MOE_EXPERTS_PER_STEP = 4

HY_Q0 = 0
HY_KV0 = HY_Q0 + NSA_HEADS
HY_F0 = HY_KV0 + 6 * NSA_GROUPS
HY_G0 = HY_F0 + 3 * FOX_HEADS
HY_FF = HY_G0 + NSA_GROUPS
HY_BLOCKS = HY_FF + 2
HY_COL_TILE = 12 * LANE
assert (HY_BLOCKS * LANE) % HY_COL_TILE == 0


def _cparams(*sem):
    return pltpu.CompilerParams(dimension_semantics=sem, vmem_limit_bytes=VMEM_LIMIT_BYTES)


def _lane_iota(shape):
    return lax.broadcasted_iota(jnp.int32, shape, len(shape) - 1)


def _row_iota(shape):
    return lax.broadcasted_iota(jnp.int32, shape, len(shape) - 2)


def _dot_nt(a, b):
    return lax.dot_general(a, b, (((1,), (1,)), ((), ())), preferred_element_type=F32)


def _dot_tn(a, b):
    return lax.dot_general(a, b, (((0,), (0,)), ((), ())), preferred_element_type=F32)


def _ada_kernel(c_ref, w_ref, b_ref, o_ref):
    c = c_ref[...]
    cond = c * jax.nn.sigmoid(c)
    o_ref[0] = jnp.dot(cond, w_ref[0], precision=HIGHEST, preferred_element_type=F32) + b_ref[0]


def _ada_mod(c, w_ada, b_ada):
    depth, d, n = w_ada.shape
    tn = 768
    c8 = jnp.broadcast_to(c.reshape(1, d), (8, d))
    out = pl.pallas_call(
        _ada_kernel,
        grid=(depth, n // tn),
        in_specs=[pl.BlockSpec((8, d), lambda l, j: (0, 0)),
                  pl.BlockSpec((1, d, tn), lambda l, j: (l, 0, j)),
                  pl.BlockSpec((1, 1, tn), lambda l, j: (l, 0, j))],
        out_specs=pl.BlockSpec((1, 8, tn), lambda l, j: (l, 0, j)),
        out_shape=jax.ShapeDtypeStruct((depth, 8, n), F32),
        compiler_params=_cparams("parallel", "parallel"),
        name="ada_mod",
    )(c8, w_ada, b_ada.reshape(depth, 1, n))
    return out[:, 0:1, :]


def _norm_mod(x, g, sc, sh):
    ms = jnp.mean(x * x, axis=-1, keepdims=True)
    return (x * lax.rsqrt(ms + EPS) * g) * (1.0 + sc) + sh


def _nmm_kernel(x_ref, g_ref, sc_ref, sh_ref, w_ref, o_ref, h_scr):
    @pl.when(pl.program_id(1) == 0)
    def _():
        h_scr[...] = _norm_mod(x_ref[...], g_ref[...], sc_ref[...], sh_ref[...]).astype(BF16)

    o_ref[...] = jnp.dot(h_scr[...], w_ref[...], preferred_element_type=F32)


def _norm_mod_matmul(x2, g, sc, sh, w, tn):
    s, d = x2.shape
    n = w.shape[1]
    vec = pl.BlockSpec((1, d), lambda i, j: (0, 0))
    return pl.pallas_call(
        _nmm_kernel,
        grid=(s // ROW_TILE, n // tn),
        in_specs=[pl.BlockSpec((ROW_TILE, d), lambda i, j: (i, 0)), vec, vec, vec,
                  pl.BlockSpec((d, tn), lambda i, j: (0, j))],
        out_specs=pl.BlockSpec((ROW_TILE, tn), lambda i, j: (i, j)),
        out_shape=jax.ShapeDtypeStruct((s, n), F32),
        scratch_shapes=[pltpu.VMEM((ROW_TILE, d), BF16)],
        compiler_params=_cparams("parallel", "arbitrary"),
        name="norm_mod_matmul",
    )(x2, g, sc, sh, w)


def _head_rms(x, gain, n_real):
    ss = jnp.sum(x * x, axis=-1, keepdims=True)
    return x * lax.rsqrt(ss * (1.0 / n_real) + EPS) * gain


def _rope64(x, cos, sin):
    lane = _lane_iota(x.shape)
    rot = jnp.where(lane < 32, -pltpu.roll(x, LANE - 32, 1), pltpu.roll(x, 32, 1))
    return x * cos + rot * sin


def _split3(c):
    hi = c.astype(BF16).astype(F32)
    r1 = c - hi
    mid = r1.astype(BF16).astype(F32)
    lo = (r1 - mid).astype(BF16).astype(F32)
    return hi, mid, lo


def _hy_prep_kernel(p_ref, pos_ref, inv_ref, gq_ref, gk_ref, gfq_ref, gfk_ref, fb_ref,
                    qnt_ref, kct_ref, vct_ref, ks_ref, vst_ref, kw_ref, vwt_ref, gate_ref,
                    fqt_ref, fk_ref, fvt_ref, cedge_ref, carry_ref):
    i = pl.program_id(0)
    tm = PREP_TILE
    shp = (tm, LANE)
    lane = _lane_iota(shp)

    def blk(b):
        return p_ref[:, b * LANE:(b + 1) * LANE]

    ang = pos_ref[...] * inv_ref[...]
    real = lane < HEAD_DIM
    cos = jnp.where(real, jnp.cos(ang), 1.0)
    sin = jnp.where(real, jnp.sin(ang), 0.0)
    gq, gk, gfq, gfk = gq_ref[...], gk_ref[...], gfq_ref[...], gfk_ref[...]
    scale = HEAD_DIM ** -0.5 * LOG2E
    ones_row = lane == SUM_ROW
    ref_ones = jnp.where((lane >= REF_ROW) & (lane < REF_ROW + 3), 1.0, 0.0)

    for h in range(NSA_HEADS):
        q = _rope64(_head_rms(blk(HY_Q0 + h), gq, HEAD_DIM), cos, sin) * scale
        qnt_ref[h * LANE:(h + 1) * LANE, :] = q.T.astype(BF16)

    row = _row_iota(shp) + i * tm
    onehot = jnp.where(lane - HEAD_DIM == ((row // SLC_BLOCK) % SUPER_BLOCKS), 1.0, 0.0)
    for g in range(NSA_GROUPS):
        def kv(r):
            return blk(HY_KV0 + r * NSA_GROUPS + g)
        sl = slice(g * LANE, (g + 1) * LANE)
        kct_ref[g] = _rope64(kv(0), cos, sin)[:, :HEAD_DIM].astype(BF16)
        vct_ref[g] = kv(1)[:, :HEAD_DIM].astype(BF16)
        ks = _rope64(_head_rms(kv(2), gk, HEAD_DIM), cos, sin)
        ks_ref[:, sl] = (ks + onehot + ref_ones).astype(BF16)
        vst_ref[g, 0] = jnp.where(ones_row, 1.0, kv(3)).T.astype(BF16)
        kw_ref[:, sl] = _rope64(_head_rms(kv(4), gk, HEAD_DIM), cos, sin).astype(BF16)
        vwt = kv(5).T.astype(BF16)
        for cidx in range(tm // LANE):
            vwt_ref[g, cidx] = vwt[:, cidx * LANE:(cidx + 1) * LANE]
        gate_ref[g] = jax.nn.sigmoid(blk(HY_G0 + g)).T[:GATE_ROWS]

    @pl.when(i == 0)
    def _():
        carry_ref[...] = jnp.zeros_like(carry_ref)

    z = blk(HY_FF) + fb_ref[...]
    logf = jnp.minimum(z, 0.0) - jnp.log1p(jnp.exp(-jnp.abs(z)))
    tri = jnp.where(_row_iota((tm, tm)) >= _lane_iota((tm, tm)), 1.0, 0.0).astype(F32)
    cum = jnp.dot(tri, logf, precision=HIGHEST, preferred_element_type=F32) + carry_ref[...]
    carry_ref[...] = cum[tm - 1:tm, :]
    cedge_ref[0] = jnp.concatenate([cum[0:1] * LOG2E, cum[tm - 1:tm] * LOG2E, jnp.zeros((6, LANE), F32)], axis=0)

    for h in range(FOX_HEADS):
        c = jnp.broadcast_to(cum[:, h:h + 1], shp) * LOG2E
        hi, mid, lo = _split3(c)
        fq = _head_rms(blk(HY_F0 + h), gfq, HEAD_DIM) * scale
        fq = jnp.where(real, fq, jnp.where(lane == 64, hi, jnp.where(lane == 65, mid, jnp.where(
            lane == 66, lo, jnp.where(lane < 70, 1.0, 0.0)))))
        fk = _head_rms(blk(HY_F0 + FOX_HEADS + h), gfk, HEAD_DIM)
        fk = jnp.where(real, fk, jnp.where(lane < 67, 1.0, jnp.where(lane == 67, -hi, jnp.where(
            lane == 68, -mid, jnp.where(lane == 69, -lo, ref_ones)))))
        sl = slice(h * LANE, (h + 1) * LANE)
        fqt_ref[sl, :] = fq.T.astype(BF16)
        fk_ref[:, sl] = fk.astype(BF16)
        fvt_ref[h, 0] = jnp.where(ones_row, 1.0, blk(HY_F0 + 2 * FOX_HEADS + h)).T.astype(BF16)


def _feat_major(heads, s, tm):
    return (pl.BlockSpec((heads * LANE, tm), lambda i: (0, i)),
            jax.ShapeDtypeStruct((heads * LANE, s), BF16))


def _value_tiles(heads, s, tm, tk):
    return (pl.BlockSpec((heads, tm // tk, LANE, tk), lambda i: (0, i, 0, 0)),
            jax.ShapeDtypeStruct((heads, s // tk, LANE, tk), BF16))


def _hy_prep(proj, posf, inv128, gq, gk, gfq, gfk, fbias):
    s = proj.shape[0]
    tm = PREP_TILE
    assert tm == KV_TILE
    vec = pl.BlockSpec((1, LANE), lambda i: (0, 0))

    def rows(nb):
        return (pl.BlockSpec((tm, nb * LANE), lambda i: (i, 0)), jax.ShapeDtypeStruct((s, nb * LANE), BF16))

    tok = (pl.BlockSpec((NSA_GROUPS, tm, HEAD_DIM), lambda i: (0, i, 0)),
           jax.ShapeDtypeStruct((NSA_GROUPS, s, HEAD_DIM), BF16))
    gate = (pl.BlockSpec((NSA_GROUPS, GATE_ROWS, tm), lambda i: (0, 0, i)),
            jax.ShapeDtypeStruct((NSA_GROUPS, GATE_ROWS, s), F32))
    outs = [_feat_major(NSA_HEADS, s, tm), tok, tok, rows(NSA_GROUPS), _value_tiles(NSA_GROUPS, s, tm, KV_TILE),
            rows(NSA_GROUPS), _value_tiles(NSA_GROUPS, s, tm, LANE), gate,
            _feat_major(FOX_HEADS, s, tm), rows(FOX_HEADS), _value_tiles(FOX_HEADS, s, tm, KV_TILE),
            (pl.BlockSpec((1, 8, LANE), lambda i: (i, 0, 0)), jax.ShapeDtypeStruct((s // tm, 8, LANE), F32))]
    return pl.pallas_call(
        _hy_prep_kernel,
        grid=(s // tm,),
        in_specs=[pl.BlockSpec((tm, HY_BLOCKS * LANE), lambda i: (i, 0)), pl.BlockSpec((tm, 1), lambda i: (i, 0)),
                  vec, vec, vec, vec, vec, vec],
        out_specs=[o[0] for o in outs],
        out_shape=[o[1] for o in outs],
        scratch_shapes=[pltpu.VMEM((1, LANE), F32)],
        compiler_params=_cparams("arbitrary"),
        name="hybrid_prep",
    )(proj, posf, inv128, gq, gk, gfq, gfk, fbias)


def _compress_kernel(kc_ref, vc_ref, wk_ref, wv_ref, pek_ref, pev_ref, gk_ref, ko_ref, vo_ref):
    half = CMP_STRIDE * HEAD_DIM

    def comp(ch_ref, w_ref, pe_ref):
        ch = ch_ref[0]
        nc = ch.shape[0]
        a = jnp.dot(ch, w_ref[:half], preferred_element_type=F32)
        b = jnp.dot(ch, w_ref[half:], preferred_element_type=F32)
        nxt = pltpu.roll(b, nc - 1, 0)
        pe = jnp.dot(jnp.broadcast_to(pe_ref[...], (8, 2 * half)).astype(BF16), w_ref[...],
                     preferred_element_type=F32)[0:1]
        return a + nxt + pe

    ko_ref[0] = _head_rms(comp(kc_ref, wk_ref, pek_ref), gk_ref[...], HEAD_DIM).astype(BF16)
    vo_ref[0] = comp(vc_ref, wv_ref, pev_ref).T.astype(BF16)


def _compress(kct, vct, w_cmp, cmp_pe, k_norm):
    g, s, _ = kct.shape
    nc = s // CMP_STRIDE
    wide = CMP_STRIDE * HEAD_DIM
    kch = kct.reshape(g, nc, wide)
    vch = vct.reshape(g, nc, wide)
    w_pad = jnp.pad(w_cmp, ((0, 0), (0, 0), (0, LANE - HEAD_DIM))).astype(BF16)
    ch = pl.BlockSpec((1, nc, wide), lambda i: (i, 0, 0))
    wspec = pl.BlockSpec((2 * wide, LANE), lambda i: (0, 0))
    pespec = pl.BlockSpec((1, 2 * wide), lambda i: (0, 0))
    return pl.pallas_call(
        _compress_kernel,
        grid=(g,),
        in_specs=[ch, ch, wspec, wspec, pespec, pespec, pl.BlockSpec((1, LANE), lambda i: (0, 0))],
        out_specs=[pl.BlockSpec((1, nc, LANE), lambda i: (i, 0, 0)),
                   pl.BlockSpec((1, LANE, nc), lambda i: (i, 0, 0))],
        out_shape=[jax.ShapeDtypeStruct((g, nc, LANE), BF16), jax.ShapeDtypeStruct((g, LANE, nc), BF16)],
        compiler_params=_cparams("parallel"),
        name="nsa_compress",
    )(kch, vch, w_pad[0], w_pad[1], cmp_pe[0].reshape(1, 2 * wide).astype(F32),
      cmp_pe[1].reshape(1, 2 * wide).astype(F32), _pad_lanes(k_norm))


def _masked_softmax_t(s, mask):
    s = jnp.where(mask, s, NEG)
    m = jnp.max(s, axis=0, keepdims=True)
    e = jnp.where(mask, jnp.exp2(s - m), 0.0)
    return e / jnp.maximum(jnp.sum(e, axis=0, keepdims=True), 1e-30)


def _online_steps(steps, ms, acc_ref):
    ms = list(ms)

    def scores(step):
        k_tile, qa, _, c, mask = step
        s = jnp.dot(k_tile, qa, preferred_element_type=F32)
        if mask is not None:
            s = jnp.where(mask, s, NEG)
        return s, jnp.max(s, axis=0, keepdims=True)

    nxt = scores(steps[0])
    for idx, (_, _, vt, c, _) in enumerate(steps):
        sl = slice(c * Q_STRIP, (c + 1) * Q_STRIP)
        s, s_max = nxt
        if idx + 1 < len(steps):
            nxt = scores(steps[idx + 1])
        m_new = jnp.maximum(ms[c], s_max)
        a = jnp.exp2(ms[c] - m_new)
        p = jnp.exp2((s - m_new).astype(BF16))
        ms[c] = m_new
        acc_ref[:, sl] = a * acc_ref[:, sl] + jnp.dot(vt, p, preferred_element_type=F32)
    return tuple(ms)


def _m_init(n_strips):
    return tuple(jnp.full((1, Q_STRIP), NEG, F32) for _ in range(n_strips))


def _with_ref_rows(qa, m):
    hi, mid, lo = _split3(-m)
    r = _row_iota((LANE - REF_SLAB, Q_STRIP)) + REF_SLAB
    slab = jnp.where(r == REF_ROW, hi, jnp.where(r == REF_ROW + 1, mid, jnp.where(r == REF_ROW + 2, lo, 0.0)))
    return jnp.concatenate([qa[:REF_SLAB], slab.astype(BF16)], axis=0)


def _first_tile_max(k_tile, qa_strips, masks):
    return tuple(jnp.max(jnp.where(mask, jnp.dot(k_tile, qa, preferred_element_type=F32), NEG), axis=0, keepdims=True)
                 for qa, mask in zip(qa_strips, masks))


def _fast_steps(steps, state, acc_ref):
    state = list(state)
    for k, step in enumerate(steps):
        assert all(prev[3] != step[3] for prev in steps[max(k - SCORE_LOOKAHEAD + 1, 0):k])

    def scores(step):
        k_tile, qa, _, c, mask = step
        s = jnp.dot(k_tile, _with_ref_rows(qa, state[c][0]), preferred_element_type=F32)
        if mask is not None:
            s = jnp.where(mask, s, NEG)
        return s

    ahead = [scores(st) for st in steps[:SCORE_LOOKAHEAD]]
    for idx, (_, _, vt, c, _) in enumerate(steps):
        sl = slice(c * Q_STRIP, (c + 1) * Q_STRIP)
        s = ahead.pop(0)
        m, worst = state[c]
        cm = jnp.max(s, axis=0, keepdims=True)
        inc = jnp.maximum(cm, 0.0)
        state[c] = (m + inc, jnp.maximum(worst, cm))
        if idx + SCORE_LOOKAHEAD < len(steps):
            ahead.append(scores(steps[idx + SCORE_LOOKAHEAD]))
        p = jnp.exp2(s).astype(BF16)
        acc_ref[:, sl] = jnp.exp2(-inc) * (acc_ref[:, sl] + jnp.dot(vt, p, preferred_element_type=F32))
    return tuple(state)


def _flat(state):
    return tuple(x for pair in state for x in pair)


def _nest(flat):
    return tuple((flat[2 * c], flat[2 * c + 1]) for c in range(len(flat) // 2))


def _nsa_kernel(qt_ref, kc_ref, vct_ref, ks_ref, vst_ref, kw_ref, vwt_ref, gate_ref, ovt_ref, o_ref,
                qaug_ref, acc_ref, *, n_sel):
    i = pl.program_id(1)
    tq = Q_TILE_NSA
    cols = NSA_HPG * tq
    qs = i * tq
    nc = kc_ref.shape[1]
    nslc = ovt_ref.shape[0]
    n_super = nslc // SUPER_BLOCKS

    qt = jnp.concatenate([qt_ref[h * LANE:(h + 1) * LANE, :] for h in range(NSA_HPG)], axis=1)
    tq_row = qs + (_lane_iota((1, cols)) % tq)

    s = jnp.dot(kc_ref[0], qt, preferred_element_type=F32)
    cmp_end = _row_iota((nc, 1)) * CMP_STRIDE + (CMP_LEN - 1)
    p = _masked_softmax_t(s, cmp_end <= tq_row)
    o_cmp = jnp.dot(vct_ref[0], p.astype(BF16), preferred_element_type=F32)

    psum = p[:, 0:tq]
    for h in range(1, NSA_HPG):
        psum = psum + p[:, h * tq:(h + 1) * tq]
    p_hi = psum.astype(BF16)
    p_lo = (psum - p_hi.astype(F32)).astype(BF16)
    ovt = ovt_ref[...]
    imp = (jnp.dot(ovt, p_hi, preferred_element_type=F32)
           + jnp.dot(ovt, p_lo, preferred_element_type=F32))

    jj = _row_iota((nslc, tq))
    tq_blk = qs + _lane_iota((nslc, tq))
    cur = tq_blk // SLC_BLOCK
    forced = (jj == 0) | (jj == cur) | (jj == cur - 1)
    causal_blk = jj * SLC_BLOCK <= tq_blk
    val = jnp.where(forced, imp + BIG, imp)
    val = jnp.where(causal_blk, val, NEG)

    jjf = jj.astype(F32)

    def pick(_, carry):
        val, sel = carry
        mx = jnp.max(val, axis=0, keepdims=True)
        idx = jnp.min(jnp.where(val == mx, jjf, float(nslc)), axis=0, keepdims=True)
        hit = jjf == idx
        return jnp.where(hit, -jnp.inf, val), jnp.where(hit, 1.0, sel)

    _, sel = lax.fori_loop(0, n_sel, pick, (val, jnp.zeros((nslc, tq), F32)))
    bias_t = jnp.where((sel > 0.0) & causal_blk, 0.0, NEG)

    q_rows = qt[:HEAD_DIM].astype(F32)
    spare = jnp.zeros((LANE - HEAD_DIM - SUPER_BLOCKS, cols), F32)
    for st in range(n_super):
        b = bias_t[st * SUPER_BLOCKS:(st + 1) * SUPER_BLOCKS]
        b = jnp.concatenate([b] * NSA_HPG, axis=1)
        qaug_ref[st] = jnp.concatenate([q_rows, b, spare], axis=0).astype(BF16)

    tk = KV_TILE
    per_super = SUPER_BLOCKS * SLC_BLOCK // tk
    j_last = (qs + tq - 1) // tk
    n_strips = cols // Q_STRIP
    strips = [slice(c * Q_STRIP, (c + 1) * Q_STRIP) for c in range(n_strips)]

    def causal_masks(j):
        kpos = j * tk + _row_iota((tk, 1))
        return [kpos <= tq_row[:, sl] for sl in strips]

    def slc_steps(j, masks=None):
        k0 = pl.multiple_of(j * tk, tk)
        k_tile, vt, st = ks_ref[pl.ds(k0, tk), :], vst_ref[0, j], j // per_super
        return [(k_tile, qaug_ref[st, :, strips[c]], vt, c, None if masks is None else masks[c])
                for c in range(n_strips)]

    def pair(jj, flat):
        return _flat(_fast_steps(slc_steps(2 * jj) + slc_steps(2 * jj + 1), _nest(flat), acc_ref))

    def single(j, flat):
        return _flat(_fast_steps(slc_steps(j), _nest(flat), acc_ref))

    acc_ref[...] = jnp.zeros(acc_ref.shape, F32)
    m0 = _first_tile_max(ks_ref[0:tk, :], [qaug_ref[0, :, sl] for sl in strips], causal_masks(0))
    n_pairs = j_last // 2
    flat = lax.fori_loop(0, n_pairs, pair, _flat(tuple((m, jnp.zeros_like(m)) for m in m0)))
    flat = lax.fori_loop(2 * n_pairs, j_last, single, flat)
    state = _fast_steps(slc_steps(j_last, causal_masks(j_last)), _nest(flat), acc_ref)
    worst = jnp.max(jnp.concatenate([w for _, w in state], axis=1))

    @pl.when(worst > EXP_GUARD)
    def _():
        acc_ref[...] = jnp.zeros(acc_ref.shape, F32)
        lax.fori_loop(0, j_last + 1, lambda j, ms: _online_steps(slc_steps(j, causal_masks(j)), ms, acc_ref),
                      _m_init(n_strips))

    acc = acc_ref[...]
    o_slc = acc / jnp.maximum(acc[SUM_ROW:SUM_ROW + 1], 1e-30)

    wlen = WINDOW + tq
    ws = pl.multiple_of(jnp.maximum(qs - WINDOW, 0), tq)
    s = jnp.dot(kw_ref[pl.ds(ws, wlen), :], qt, preferred_element_type=F32)
    dist = tq_row - (ws + _row_iota((wlen, 1)))
    p = _masked_softmax_t(s, (dist >= 0) & (dist < WINDOW)).astype(BF16)
    wb = ws // LANE
    o_win = jnp.zeros((LANE, cols), F32)
    for c in range(wlen // LANE):
        o_win = o_win + jnp.dot(vwt_ref[0, wb + c], p[c * LANE:(c + 1) * LANE], preferred_element_type=F32)

    gate = gate_ref[0]
    for h in range(NSA_HPG):
        sl = slice(h * tq, (h + 1) * tq)
        o = (gate[3 * h:3 * h + 1] * o_cmp[:, sl] + gate[3 * h + 1:3 * h + 2] * o_slc[:, sl]
             + gate[3 * h + 2:3 * h + 3] * o_win[:, sl])
        o_ref[:, h * LANE:(h + 1) * LANE] = o.T.astype(BF16)


def _overlap_t(s, nslc_pad):
    nc = s // CMP_STRIDE
    cmp_start = np.arange(nc) * CMP_STRIDE
    slc_start = np.arange(nslc_pad) * SLC_BLOCK
    ov = np.clip(np.minimum(cmp_start[:, None] + CMP_LEN, slc_start[None, :] + SLC_BLOCK)
                 - np.maximum(cmp_start[:, None], slc_start[None, :]), 0, None) / CMP_STRIDE
    ov[nc - CMP_LEN // CMP_STRIDE + 1:, :] = 0.0
    ov[:, s // SLC_BLOCK:] = 0.0
    return jnp.asarray(ov.T, BF16)


def _nsa_attention(qnt, kc, vct, ks, vst, kw, vwt, gates):
    s = qnt.shape[1]
    nc = s // CMP_STRIDE
    n_slc = s // SLC_BLOCK
    nslc_pad = -(-n_slc // LANE) * LANE
    tq = Q_TILE_NSA
    cols = NSA_HPG * tq
    once = pl.Buffered(1)
    res = pl.BlockSpec((s, LANE), lambda g, i: (0, g), pipeline_mode=once)
    return pl.pallas_call(
        functools.partial(_nsa_kernel, n_sel=min(SLC_TOPK, n_slc)),
        grid=(NSA_GROUPS, s // tq),
        in_specs=[pl.BlockSpec((NSA_HPG * LANE, tq), lambda g, i: (g, i)),
                  pl.BlockSpec((1, nc, LANE), lambda g, i: (g, 0, 0), pipeline_mode=once),
                  pl.BlockSpec((1, LANE, nc), lambda g, i: (g, 0, 0), pipeline_mode=once),
                  res, pl.BlockSpec((1, s // KV_TILE, LANE, KV_TILE), lambda g, i: (g, 0, 0, 0),
                                    pipeline_mode=once),
                  res, pl.BlockSpec((1, s // LANE, LANE, LANE), lambda g, i: (g, 0, 0, 0), pipeline_mode=once),
                  pl.BlockSpec((1, GATE_ROWS, tq), lambda g, i: (g, 0, i)),
                  pl.BlockSpec((nslc_pad, nc), lambda g, i: (0, 0), pipeline_mode=once)],
        out_specs=pl.BlockSpec((tq, NSA_HPG * LANE), lambda g, i: (i, g)),
        out_shape=jax.ShapeDtypeStruct((s, NSA_HEADS * LANE), BF16),
        scratch_shapes=[pltpu.VMEM((nslc_pad // SUPER_BLOCKS, LANE, cols), BF16),
                        pltpu.VMEM((LANE, cols), F32)],
        compiler_params=_cparams("parallel", "arbitrary"),
        name="nsa_attention",
    )(qnt, kc, vct, ks, vst, kw, vwt, gates, _overlap_t(s, nslc_pad))


def _flash_kernel(cfirst_ref, clast_ref, slack_ref, qt_ref, k_ref, vt_ref, o_ref, acc_ref):
    h = pl.program_id(0)
    i = pl.program_id(1)
    tq, tk = Q_TILE_FLASH, KV_TILE
    n_tiles = k_ref.shape[0] // tk
    acc_ref[...] = jnp.zeros(acc_ref.shape, F32)

    n_strips = tq // Q_STRIP
    per_q = tq // tk
    qas = [qt_ref[:, c * Q_STRIP:(c + 1) * Q_STRIP] for c in range(n_strips)]

    def tile_steps(j, d=None):
        k0 = pl.multiple_of(j * tk, tk)
        k_tile, vt = k_ref[pl.ds(k0, tk), :], vt_ref[0, j]
        steps = []
        for c in range(n_strips):
            mask = None
            if d is not None:
                if d * tk > (c + 1) * Q_STRIP - 1:
                    continue
                if (d + 1) * tk - 1 > c * Q_STRIP:
                    shp = (tk, Q_STRIP)
                    mask = _row_iota(shp) + d * tk <= _lane_iota(shp) + c * Q_STRIP
            steps.append((k_tile, qas[c], vt, c, mask))
        return steps

    def any_tile_masks(j):
        shp = (tk, Q_STRIP)
        return [_row_iota(shp) + j * tk <= _lane_iota(shp) + (i * tq + c * Q_STRIP) for c in range(n_strips)]

    def below(t, flat):
        jj = i - 1 - t
        bound = (slack_ref[0] + cfirst_ref[h * n_tiles + i * per_q]
                 - clast_ref[h * n_tiles + jj * per_q + per_q - 1])

        def run(flat):
            steps = [st for u in range(per_q) for st in tile_steps(jj * per_q + (per_q - 1 - u))]
            return _flat(_fast_steps(steps, _nest(flat), acc_ref))

        return lax.cond(bound >= -SKIP_MARGIN, run, lambda flat: flat, flat)

    diag0 = i * per_q
    own = [(c * Q_STRIP) // tk for c in range(n_strips)]
    m0 = tuple(_first_tile_max(k_ref[pl.ds(pl.multiple_of((diag0 + own[c]) * tk, tk), tk), :], [qas[c]],
                               [any_tile_masks(diag0 + own[c])[c]])[0] for c in range(n_strips))
    steps = [st for d in reversed(range(per_q)) for st in tile_steps(diag0 + d, d)]
    state = _fast_steps(steps, tuple((m, jnp.zeros_like(m)) for m in m0), acc_ref)
    state = _nest(lax.fori_loop(0, i, below, _flat(state)))
    worst = jnp.max(jnp.concatenate([w for _, w in state], axis=1))

    @pl.when(worst > EXP_GUARD)
    def _():
        acc_ref[...] = jnp.zeros(acc_ref.shape, F32)

        def exact(j, ms):
            k0 = pl.multiple_of(j * tk, tk)
            k_tile, vt, masks = k_ref[pl.ds(k0, tk), :], vt_ref[0, j], any_tile_masks(j)
            return _online_steps([(k_tile, qas[c], vt, c, masks[c]) for c in range(n_strips)], ms, acc_ref)

        lax.fori_loop(0, (i + 1) * per_q, exact, _m_init(n_strips))

    acc = acc_ref[...]
    o = acc / acc[SUM_ROW:SUM_ROW + 1]
    for c0 in range(0, tq, LANE):
        o_ref[c0:c0 + LANE, :] = o[:, c0:c0 + LANE].T.astype(BF16)


def _causal_attention(qt, k, vt, bias_edges=None, slack=None):
    s, width = k.shape
    heads = width // LANE
    assert Q_TILE_FLASH % KV_TILE == 0 and s % Q_TILE_FLASH == 0
    n_tiles = s // KV_TILE
    if bias_edges is None:
        first = last = jnp.zeros((heads * n_tiles,), F32)
        slack = jnp.full((1,), -NEG, F32)
    else:
        first, last = (e.reshape(heads * n_tiles).astype(F32) for e in bias_edges)
    grid_spec = pltpu.PrefetchScalarGridSpec(
        num_scalar_prefetch=3,
        grid=(heads, s // Q_TILE_FLASH),
        in_specs=[pl.BlockSpec((LANE, Q_TILE_FLASH), lambda h, i, *_: (h, i)),
                  pl.BlockSpec((s, LANE), lambda h, i, *_: (0, h)),
                  pl.BlockSpec((1, n_tiles, LANE, KV_TILE), lambda h, i, *_: (h, 0, 0, 0))],
        out_specs=pl.BlockSpec((Q_TILE_FLASH, LANE), lambda h, i, *_: (i, h)),
        scratch_shapes=[pltpu.VMEM((LANE, Q_TILE_FLASH), F32)],
    )
    return pl.pallas_call(
        _flash_kernel,
        grid_spec=grid_spec,
        out_shape=jax.ShapeDtypeStruct((s, width), BF16),
        compiler_params=_cparams("parallel", "arbitrary"),
        name="causal_attention",
    )(first, last, slack.astype(F32), qt, k, vt)


def _out_proj_kernel(oa_ref, ob_ref, wa_ref, wb_ref, x_ref, g_ref, o_ref):
    y = jnp.dot(oa_ref[...], wa_ref[...], preferred_element_type=F32)
    y = y + jnp.dot(ob_ref[...], wb_ref[...], preferred_element_type=F32)
    o_ref[...] = x_ref[...] + g_ref[...] * y


def _out_proj(oa, ob, cola, colb, wa, wb, x2, gate):
    s, d = x2.shape
    ka = wa.shape[0]
    tm = ROW_TILE
    return pl.pallas_call(
        _out_proj_kernel,
        grid=(s // tm,),
        in_specs=[pl.BlockSpec((tm, ka), lambda i: (i, cola)), pl.BlockSpec((tm, ka), lambda i: (i, colb)),
                  pl.BlockSpec((ka, d), lambda i: (0, 0)), pl.BlockSpec((ka, d), lambda i: (0, 0)),
                  pl.BlockSpec((tm, d), lambda i: (i, 0)), pl.BlockSpec((1, d), lambda i: (0, 0))],
        out_specs=pl.BlockSpec((tm, d), lambda i: (i, 0)),
        out_shape=jax.ShapeDtypeStruct((s, d), F32),
        compiler_params=_cparams("parallel"),
        name="out_proj",
    )(oa, ob, wa, wb, x2, gate)


def _mla_prep_kernel(p_ref, pos_ref, inv_ref, gqa_ref, gkva_ref, wuq_ref, wuk_ref, wuv_ref,
                     gq_ref, gk_ref, gkr_ref, qt_ref, k_ref, vt_ref):
    shp = (PREP_TILE, LANE)
    lane = _lane_iota(shp)
    nope = lane < QK_NOPE
    rope = (lane >= QK_NOPE) & (lane < QK_NOPE + QK_ROPE)
    ref_ones = jnp.where((lane >= REF_ROW) & (lane < REF_ROW + 3), 1.0, 0.0)
    ang = pos_ref[...] * inv_ref[...]
    cos = jnp.where(rope, jnp.cos(ang), 1.0)
    sin = jnp.where(rope, jnp.sin(ang), 0.0)

    def rope32(x):
        half = QK_ROPE // 2
        rot = jnp.where(lane < QK_NOPE + half, -pltpu.roll(x, LANE - half, 1), pltpu.roll(x, half, 1))
        return x * cos + rot * sin

    def low_rank_norm(x, g):
        ms = jnp.mean(x * x, axis=-1, keepdims=True)
        return (x * lax.rsqrt(ms + EPS) * g).astype(BF16)

    nq = Q_LORA // LANE
    cq = low_rank_norm(p_ref[:, :Q_LORA], gqa_ref[...])
    ckv = low_rank_norm(p_ref[:, Q_LORA:Q_LORA + KV_LORA], gkva_ref[...])
    kr = p_ref[:, (nq + KV_LORA // LANE) * LANE:(nq + KV_LORA // LANE + 1) * LANE]
    k_rope = rope32(_head_rms(kr, gkr_ref[...], QK_ROPE))

    gq, gk = gq_ref[...], gk_ref[...]
    scale = (QK_NOPE + QK_ROPE) ** -0.5 * LOG2E
    pair = 2 * LANE
    for hp in range(MLA_HEADS // 2):
        cols = slice(hp * pair, (hp + 1) * pair)
        q2 = jnp.dot(cq, wuq_ref[:, cols], preferred_element_type=F32)
        k2 = jnp.dot(ckv, wuk_ref[:, cols], preferred_element_type=F32)
        v2 = jnp.dot(ckv, wuv_ref[:, cols], preferred_element_type=F32)
        for sub in range(2):
            head = 2 * hp + sub
            sl = slice(head * LANE, (head + 1) * LANE)
            half = slice(sub * LANE, (sub + 1) * LANE)
            x = q2[:, half]
            ss_n = jnp.sum(jnp.where(nope, x * x, 0.0), axis=-1, keepdims=True)
            ss_r = jnp.sum(jnp.where(rope, x * x, 0.0), axis=-1, keepdims=True)
            inv_rms = jnp.where(nope, lax.rsqrt(ss_n * (1.0 / QK_NOPE) + EPS),
                                lax.rsqrt(ss_r * (1.0 / QK_ROPE) + EPS))
            qt_ref[sl, :] = (rope32(x * inv_rms * gq) * scale).T.astype(BF16)
            kn = _head_rms(k2[:, half], gk, QK_NOPE)
            k_ref[:, sl] = (kn + k_rope + ref_ones).astype(BF16)
            vt_ref[head, 0] = jnp.where(lane == SUM_ROW, 1.0, v2[:, half]).T.astype(BF16)


def _mla_prep(proj, posf, inv128, gqa, gkva, wuq, wuk, wuv, gq, gk, gkr):
    s, n = proj.shape
    tm = PREP_TILE
    assert tm == KV_TILE

    def full(a):
        return pl.BlockSpec(a.shape, lambda i: (0, 0))

    outs = [_feat_major(MLA_HEADS, s, tm),
            (pl.BlockSpec((tm, MLA_HEADS * LANE), lambda i: (i, 0)),
             jax.ShapeDtypeStruct((s, MLA_HEADS * LANE), BF16)),
            _value_tiles(MLA_HEADS, s, tm, KV_TILE)]
    args = (inv128, gqa, gkva, wuq, wuk, wuv, gq, gk, gkr)
    return pl.pallas_call(
        _mla_prep_kernel,
        grid=(s // tm,),
        in_specs=[pl.BlockSpec((tm, n), lambda i: (i, 0)), pl.BlockSpec((tm, 1), lambda i: (i, 0))]
                 + [full(a) for a in args],
        out_specs=[o[0] for o in outs],
        out_shape=[o[1] for o in outs],
        compiler_params=_cparams("parallel"),
        name="mla_prep",
    )(proj, posf, *args)


def _rank_lt(v, k):
    n = v.shape[0]
    row = _row_iota(v.shape)
    rank = jnp.zeros(v.shape, F32)
    for b in range(n):
        vb = v[b:b + 1, :]
        rank = rank + jnp.where((vb > v) | ((vb == v) & (row > b)), 1.0, 0.0)
    return rank < k


def _moe_route_kernel(x_ref, g_ref, sc_ref, sh_ref, wr_ref, rb_ref, h_ref, pos_ref, wt_ref, cnt_ref):
    tm = ROW_TILE
    h = _norm_mod(x_ref[...], g_ref[...], sc_ref[...], sh_ref[...])
    h_ref[...] = h.astype(BF16)
    logits = jnp.dot(h, wr_ref[...], precision=HIGHEST, preferred_element_type=F32)
    lt = logits.T[:N_EXPERTS]
    scores = jax.nn.sigmoid(lt)
    sel = scores + rb_ref[...]

    per = N_EXPERTS // N_GROUPS
    grp = sel.reshape(N_GROUPS, per, tm)
    sub = lax.broadcasted_iota(jnp.int32, grp.shape, 1)
    m1 = jnp.max(grp, axis=1, keepdims=True)
    first = jnp.min(jnp.where(grp == m1, sub, per), axis=1, keepdims=True)
    m2 = jnp.max(jnp.where(sub == first, -jnp.inf, grp), axis=1, keepdims=True)
    gscore = (m1 + m2).reshape(N_GROUPS, tm)
    gmask = _rank_lt(gscore, TOPK_GROUPS)
    emask = jnp.broadcast_to(gmask.reshape(N_GROUPS, 1, tm), grp.shape).reshape(N_EXPERTS, tm)
    chosen = _rank_lt(jnp.where(emask, sel, NEG), TOP_K)

    w = jnp.where(chosen, scores, 0.0)
    wt_ref[...] = w / jnp.sum(w, axis=0, keepdims=True) * ROUTED_SCALE

    upper = jnp.where(_row_iota((tm, tm)) <= _lane_iota((tm, tm)), 1.0, 0.0).astype(BF16)
    incl = jnp.dot(jnp.where(chosen, 1.0, 0.0).astype(BF16), upper, preferred_element_type=F32)
    pos_ref[...] = jnp.where(chosen, incl - 1.0, -1.0)
    cnt_ref[0] = jnp.broadcast_to(incl[:, tm - 1:tm], (N_EXPERTS, LANE))


def _moe_route(x2, g, sc, sh, w_router_pad, router_bias_col):
    s, d = x2.shape
    tm = ROW_TILE
    vec = pl.BlockSpec((1, d), lambda i: (0, 0))
    et = pl.BlockSpec((N_EXPERTS, tm), lambda i: (0, i))
    return pl.pallas_call(
        _moe_route_kernel,
        grid=(s // tm,),
        in_specs=[pl.BlockSpec((tm, d), lambda i: (i, 0)), vec, vec, vec,
                  pl.BlockSpec((d, LANE), lambda i: (0, 0)),
                  pl.BlockSpec((N_EXPERTS, 1), lambda i: (0, 0))],
        out_specs=[pl.BlockSpec((tm, d), lambda i: (i, 0)), et, et,
                   pl.BlockSpec((1, N_EXPERTS, LANE), lambda i: (i, 0, 0))],
        out_shape=[jax.ShapeDtypeStruct((s, d), BF16), jax.ShapeDtypeStruct((N_EXPERTS, s), F32),
                   jax.ShapeDtypeStruct((N_EXPERTS, s), F32),
                   jax.ShapeDtypeStruct((s // tm, N_EXPERTS, LANE), F32)],
        compiler_params=_cparams("parallel"),
        name="moe_route",
    )(x2, g, sc, sh, w_router_pad, router_bias_col)


def _moe_kernel(cnt_ref, x_ref, h_ref, pos_ref, wt_ref, wg_ref, wu_ref, wd_ref, sg_ref, su_ref, sd_ref,
                g2_ref, o_ref, acc_ref):
    i = pl.program_id(0)
    e = pl.program_id(1)
    tm = ROW_TILE
    r = MOE_CHUNK

    @pl.when(e == 0)
    def _():
        h = h_ref[...]
        a = jnp.dot(h, sg_ref[...], preferred_element_type=F32)
        a = a * jax.nn.sigmoid(a) * jnp.dot(h, su_ref[...], preferred_element_type=F32)
        acc_ref[...] = jnp.dot(a.astype(BF16), sd_ref[...], preferred_element_type=F32)

    first = e * MOE_EXPERTS_PER_STEP
    n = cnt_ref[i * N_EXPERTS + first]
    for k in range(1, MOE_EXPERTS_PER_STEP):
        n = jnp.maximum(n, cnt_ref[i * N_EXPERTS + first + k])
    prows = [pos_ref[pl.ds(first + k, 1), :] for k in range(MOE_EXPERTS_PER_STEP)]
    wrows = [wt_ref[pl.ds(first + k, 1), :] for k in range(MOE_EXPERTS_PER_STEP)]

    def chunk(c, _):
        slot = (_row_iota((r, tm)) + c * r).astype(F32)
        hits = [prow == slot for prow in prows]
        onehot = jnp.concatenate([jnp.where(hit, 1.0, 0.0).astype(BF16) for hit in hits], axis=0)
        xg = jnp.dot(onehot, h_ref[...], preferred_element_type=F32).astype(BF16)
        ys = []
        for k in range(MOE_EXPERTS_PER_STEP):
            xk = xg[k * r:(k + 1) * r]
            a = jnp.dot(xk, wg_ref[k], preferred_element_type=F32)
            a = a * jax.nn.sigmoid(a) * jnp.dot(xk, wu_ref[k], preferred_element_type=F32)
            y = jnp.dot(a.astype(BF16), wd_ref[k], preferred_element_type=F32)
            wr = jnp.sum(jnp.where(hits[k], wrows[k], 0.0), axis=-1, keepdims=True)
            ys.append((y * wr).astype(BF16))
        acc_ref[...] += _dot_tn(onehot, jnp.concatenate(ys, axis=0))
        return 0

    lax.fori_loop(0, (n + r - 1) // r, chunk, 0)

    @pl.when(e == N_EXPERTS // MOE_EXPERTS_PER_STEP - 1)
    def _():
        o_ref[...] = x_ref[...] + g2_ref[...] * acc_ref[...]


def _moe_experts(counts, x2, h, pos_t, w_t, wg, wu, wd, sg, su, sd, g2):
    s, d = x2.shape
    tm = ROW_TILE
    ff = wg.shape[2]
    tile = pl.BlockSpec((tm, d), lambda i, e, c: (i, 0))
    et = pl.BlockSpec((N_EXPERTS, tm), lambda i, e, c: (0, i))

    def const(a):
        return pl.BlockSpec(a.shape, lambda i, e, c: (0,) * a.ndim)

    per = MOE_EXPERTS_PER_STEP
    grid_spec = pltpu.PrefetchScalarGridSpec(
        num_scalar_prefetch=1,
        grid=(s // tm, N_EXPERTS // per),
        in_specs=[tile, tile, et, et,
                  pl.BlockSpec((per, d, ff), lambda i, e, c: (e, 0, 0)),
                  pl.BlockSpec((per, d, ff), lambda i, e, c: (e, 0, 0)),
                  pl.BlockSpec((per, ff, d), lambda i, e, c: (e, 0, 0)),
                  const(sg), const(su), const(sd), const(g2)],
        out_specs=tile,
        scratch_shapes=[pltpu.VMEM((tm, d), F32)],
    )
    return pl.pallas_call(
        _moe_kernel,
        grid_spec=grid_spec,
        out_shape=jax.ShapeDtypeStruct((s, d), F32),
        compiler_params=_cparams("parallel", "arbitrary"),
        name="moe_experts",
    )(counts, x2, h, pos_t, w_t, wg, wu, wd, sg, su, sd, g2)


def _pad_lanes(v, width=LANE, offset=0):
    out = jnp.zeros((1, width), F32)
    return out.at[0, offset:offset + v.shape[0]].set(v.astype(F32))


def _head_cols(w, n_heads, dim):
    d = w.shape[0]
    w3 = w.reshape(d, n_heads, dim)
    return jnp.pad(w3, ((0, 0), (0, 0), (0, LANE - dim))).reshape(d, n_heads * LANE)


def _hybrid_w_in(w_in):
    d = w_in.shape[0]
    nq = NSA_HEADS * HEAD_DIM
    nkv = 6 * NSA_GROUPS * HEAD_DIM
    ng = 3 * NSA_HEADS
    nf = 3 * FOX_HEADS * HEAD_DIM
    c0, c1, c2, c3 = nq, nq + nkv, nq + nkv + ng, nq + nkv + ng + nf
    gates = w_in[:, c1:c2].reshape(d, NSA_GROUPS, 3 * NSA_HPG)
    gates = jnp.pad(gates, ((0, 0), (0, 0), (0, LANE - 3 * NSA_HPG))).reshape(d, NSA_GROUPS * LANE)
    ff = jnp.pad(w_in[:, c3:], ((0, 0), (0, 2 * LANE - FOX_HEADS)))
    return jnp.concatenate([
        _head_cols(w_in[:, :c0], NSA_HEADS, HEAD_DIM),
        _head_cols(w_in[:, c0:c1], 6 * NSA_GROUPS, HEAD_DIM),
        _head_cols(w_in[:, c2:c3], 3 * FOX_HEADS, HEAD_DIM),
        gates, ff], axis=1).astype(BF16)


def _pad_head_rows(w, n_heads, dim):
    d = w.shape[1]
    w3 = w.reshape(n_heads, dim, d)
    return jnp.pad(w3, ((0, 0), (0, LANE - dim), (0, 0))).reshape(n_heads * LANE, d).astype(BF16)


def _rope_inv(dim, offset):
    inv = ROPE_THETA ** (-jnp.arange(0, dim, 2, dtype=F32) / dim)
    return _pad_lanes(jnp.concatenate([inv, inv]), offset=offset)


def _hybrid_mixer(x2, posf, mods, norm_g, w_in, fox_f_bias, nsa_q_norm, nsa_k_norm, nsa_cmp_pe, nsa_w_cmp,
                  fox_q_norm, fox_k_norm, w_out):
    sh1, sc1, g1 = mods
    proj = _norm_mod_matmul(x2, norm_g, sc1, sh1, _hybrid_w_in(w_in), tn=HY_COL_TILE)
    (qnt, kct, vct, ks, vst, kw, vwt, gates, fqt, fk, fvt, cedge) = _hy_prep(
        proj, posf, _rope_inv(HEAD_DIM, 0), _pad_lanes(nsa_q_norm), _pad_lanes(nsa_k_norm),
        _pad_lanes(fox_q_norm), _pad_lanes(fox_k_norm), _pad_lanes(fox_f_bias))
    kc, vc_t = _compress(kct, vct, nsa_w_cmp, nsa_cmp_pe, nsa_k_norm)
    o_a = _nsa_attention(qnt, kc, vc_t, ks, vst, kw, vwt, gates)
    slack = (2.0 * HEAD_DIM ** 0.5 * LOG2E) * jnp.max(jnp.abs(fox_q_norm)) * jnp.max(jnp.abs(fox_k_norm))
    edges = (cedge[:, 0, :FOX_HEADS].T, cedge[:, 1, :FOX_HEADS].T)
    o_b = _causal_attention(fqt, fk, fvt, edges, slack.reshape(1))
    half = NSA_HEADS * HEAD_DIM
    wa = _pad_head_rows(w_out[:half], NSA_HEADS, HEAD_DIM)
    wb = _pad_head_rows(w_out[half:], FOX_HEADS, HEAD_DIM)
    return _out_proj(o_a, o_b, 0, 0, wa, wb, x2, g1)


def _mla_mixer(x2, posf, mods, norm_g, w_in, q_a_norm, kv_a_norm, w_uq, w_ukv, qn_norm, kn_norm, qr_norm,
               kr_norm, w_out):
    sh1, sc1, g1 = mods
    d = x2.shape[1]
    w_kr = jnp.zeros((d, LANE), F32).at[:, QK_NOPE:QK_NOPE + QK_ROPE].set(w_in[:, Q_LORA + KV_LORA:])
    w_in_p = jnp.concatenate([w_in[:, :Q_LORA + KV_LORA], w_kr], axis=1).astype(BF16)
    proj = _norm_mod_matmul(x2, norm_g, sc1, sh1, w_in_p, tn=w_in_p.shape[1])
    hq = QK_NOPE + QK_ROPE
    wuq = _head_cols(w_uq, MLA_HEADS, hq).astype(BF16)
    wkv3 = w_ukv.reshape(KV_LORA, MLA_HEADS, QK_NOPE + V_HEAD)
    wuk = _head_cols(wkv3[:, :, :QK_NOPE].reshape(KV_LORA, -1), MLA_HEADS, QK_NOPE).astype(BF16)
    wuv = _head_cols(wkv3[:, :, QK_NOPE:].reshape(KV_LORA, -1), MLA_HEADS, V_HEAD).astype(BF16)
    gq = _pad_lanes(jnp.concatenate([qn_norm, qr_norm]))
    qt, k, vt = _mla_prep(proj, posf, _rope_inv(QK_ROPE, QK_NOPE), q_a_norm.reshape(1, -1).astype(F32),
                          kv_a_norm.reshape(1, -1).astype(F32), wuq, wuk, wuv, gq, _pad_lanes(kn_norm),
                          _pad_lanes(kr_norm, offset=QK_NOPE))
    o = _causal_attention(qt, k, vt)
    w_pad = _pad_head_rows(w_out, MLA_HEADS, V_HEAD)
    half = w_pad.shape[0] // 2
    return _out_proj(o, o, 0, 1, w_pad[:half], w_pad[half:], x2, g1)


def _moe_ffn(x2, mods, norm_g, w_router, router_bias, w_gate, w_up, w_down, ws_gate, ws_up, ws_down):
    sh2, sc2, g2 = mods
    w_r = jnp.pad(w_router.astype(F32), ((0, 0), (0, LANE - N_EXPERTS)))
    h, pos_t, w_t, cnt = _moe_route(x2, norm_g, sc2, sh2, w_r, router_bias.reshape(N_EXPERTS, 1).astype(F32))
    counts = cnt[:, :, 0].astype(jnp.int32).reshape(-1)
    return _moe_experts(counts, x2, h, pos_t, w_t, w_gate.astype(BF16), w_up.astype(BF16),
                        w_down.astype(BF16), ws_gate.astype(BF16), ws_up.astype(BF16), ws_down.astype(BF16), g2)


def kernel(x, c, positions, norm_attn, norm_ffn, w_ada, b_ada, hy_w_in, fox_f_bias, nsa_q_norm, nsa_k_norm, nsa_cmp_pe, nsa_w_cmp, fox_q_norm, fox_k_norm, hy_w_out, mla_w_in, mla_q_a_norm, mla_kv_a_norm, mla_w_uq, mla_w_ukv, mla_qn_norm, mla_kn_norm, mla_qr_norm, mla_kr_norm, mla_w_out, moe_w_router, moe_router_bias, moe_w_gate, moe_w_up, moe_w_down, moe_ws_gate, moe_ws_up, moe_ws_down):
    b, s, d = x.shape
    assert b == 1 and s % KV_TILE == 0 and s >= WINDOW + Q_TILE_NSA
    depth = w_ada.shape[0]
    x2 = x.reshape(s, d).astype(F32)
    posf = positions.reshape(s, 1).astype(F32)
    mod = _ada_mod(c.astype(F32), w_ada.astype(F32), b_ada.astype(F32))

    for layer in range(depth):
        m = [mod[layer, :, k * d:(k + 1) * d] for k in range(6)]
        i = layer // 2
        g_attn = norm_attn[layer].reshape(1, d).astype(F32)
        if layer % 2 == 0:
            x2 = _hybrid_mixer(x2, posf, m[0:3], g_attn, hy_w_in[i], fox_f_bias[i], nsa_q_norm[i],
                               nsa_k_norm[i], nsa_cmp_pe[i], nsa_w_cmp[i], fox_q_norm[i], fox_k_norm[i],
                               hy_w_out[i])
        else:
            x2 = _mla_mixer(x2, posf, m[0:3], g_attn, mla_w_in[i], mla_q_a_norm[i], mla_kv_a_norm[i],
                            mla_w_uq[i], mla_w_ukv[i], mla_qn_norm[i], mla_kn_norm[i], mla_qr_norm[i],
                            mla_kr_norm[i], mla_w_out[i])
        x2 = _moe_ffn(x2, m[3:6], norm_ffn[layer].reshape(1, d).astype(F32), moe_w_router[layer],
                      moe_router_bias[layer], moe_w_gate[layer], moe_w_up[layer], moe_w_down[layer],
                      moe_ws_gate[layer], moe_ws_up[layer], moe_ws_down[layer])
    return x2.reshape(b, s, d)
```

```python
import functools

import numpy as np
import jax
import jax.numpy as jnp
from jax import lax
from jax.experimental import pallas as pl
from jax.experimental.pallas import tpu as pltpu

F32 = jnp.float32
BF16 = jnp.bfloat16
HIGHEST = lax.Precision.HIGHEST

LANE = 128
VMEM_LIMIT_BYTES = 56 * 1024 * 1024

HEAD_DIM = 64
NSA_HEADS = 8
NSA_GROUPS = 2
NSA_HPG = NSA_HEADS // NSA_GROUPS
CMP_LEN = 32
CMP_STRIDE = 16
SLC_BLOCK = 64
SLC_TOPK = 16
WINDOW = 512
FOX_HEADS = 8
MLA_HEADS = 16
Q_LORA = 384
KV_LORA = 256
QK_NOPE = 64
QK_ROPE = 32
V_HEAD = 64
N_EXPERTS = 64
TOP_K = 8
N_GROUPS = 8
TOPK_GROUPS = 4
EXPERT_FF = 256
ROUTED_SCALE = 2.5
ROPE_THETA = 10000.0
EPS = 1e-6
NEG = -1e30
BIG = 1e6

ROW_TILE = 512
PREP_TILE = 512
GATE_ROWS = 16
Q_TILE_NSA = 256
KV_TILE = 512
Q_TILE_FLASH = 1024
Q_STRIP = 256
SUPER_BLOCKS = 32
SUM_ROW = 64
REF_ROW = 104
REF_SLAB = 96
EXP_GUARD = 100.0
SCORE_LOOKAHEAD = 3
SKIP_MARGIN = 160.0
LOG2E = 1.4426950408889634
MOE_CHUNK = 128
MOE_EXPERTS_PER_STEP = 8

HY_Q0 = 0
HY_KV0 = HY_Q0 + NSA_HEADS
HY_F0 = HY_KV0 + 6 * NSA_GROUPS
HY_G0 = HY_F0 + 3 * FOX_HEADS
HY_FF = HY_G0 + NSA_GROUPS
HY_BLOCKS = HY_FF + 2
HY_COL_TILE = 12 * LANE
assert (HY_BLOCKS * LANE) % HY_COL_TILE == 0


def _cparams(*sem):
    return pltpu.CompilerParams(dimension_semantics=sem, vmem_limit_bytes=VMEM_LIMIT_BYTES)


def _lane_iota(shape):
    return lax.broadcasted_iota(jnp.int32, shape, len(shape) - 1)


def _row_iota(shape):
    return lax.broadcasted_iota(jnp.int32, shape, len(shape) - 2)


def _dot_nt(a, b):
    return lax.dot_general(a, b, (((1,), (1,)), ((), ())), preferred_element_type=F32)


def _dot_tn(a, b):
    return lax.dot_general(a, b, (((0,), (0,)), ((), ())), preferred_element_type=F32)


def _ada_kernel(c_ref, w_ref, b_ref, o_ref):
    c = c_ref[...]
    cond = c * jax.nn.sigmoid(c)
    o_ref[0] = jnp.dot(cond, w_ref[0], precision=HIGHEST, preferred_element_type=F32) + b_ref[0]


def _ada_mod(c, w_ada, b_ada):
    depth, d, n = w_ada.shape
    tn = 768
    c8 = jnp.broadcast_to(c.reshape(1, d), (8, d))
    out = pl.pallas_call(
        _ada_kernel,
        grid=(depth, n // tn),
        in_specs=[pl.BlockSpec((8, d), lambda l, j: (0, 0)),
                  pl.BlockSpec((1, d, tn), lambda l, j: (l, 0, j)),
                  pl.BlockSpec((1, 1, tn), lambda l, j: (l, 0, j))],
        out_specs=pl.BlockSpec((1, 8, tn), lambda l, j: (l, 0, j)),
        out_shape=jax.ShapeDtypeStruct((depth, 8, n), F32),
        compiler_params=_cparams("parallel", "parallel"),
        name="ada_mod",
    )(c8, w_ada, b_ada.reshape(depth, 1, n))
    return out[:, 0:1, :]


def _norm_mod(x, g, sc, sh):
    ms = jnp.mean(x * x, axis=-1, keepdims=True)
    return (x * lax.rsqrt(ms + EPS) * g) * (1.0 + sc) + sh


def _nmm_kernel(x_ref, g_ref, sc_ref, sh_ref, w_ref, o_ref, h_scr):
    @pl.when(pl.program_id(1) == 0)
    def _():
        h_scr[...] = _norm_mod(x_ref[...], g_ref[...], sc_ref[...], sh_ref[...]).astype(BF16)

    o_ref[...] = jnp.dot(h_scr[...], w_ref[...], preferred_element_type=F32)


def _norm_mod_matmul(x2, g, sc, sh, w, tn):
    s, d = x2.shape
    n = w.shape[1]
    vec = pl.BlockSpec((1, d), lambda i, j: (0, 0))
    return pl.pallas_call(
        _nmm_kernel,
        grid=(s // ROW_TILE, n // tn),
        in_specs=[pl.BlockSpec((ROW_TILE, d), lambda i, j: (i, 0)), vec, vec, vec,
                  pl.BlockSpec((d, tn), lambda i, j: (0, j))],
        out_specs=pl.BlockSpec((ROW_TILE, tn), lambda i, j: (i, j)),
        out_shape=jax.ShapeDtypeStruct((s, n), F32),
        scratch_shapes=[pltpu.VMEM((ROW_TILE, d), BF16)],
        compiler_params=_cparams("parallel", "arbitrary"),
        name="norm_mod_matmul",
    )(x2, g, sc, sh, w)


def _head_rms(x, gain, n_real):
    ss = jnp.sum(x * x, axis=-1, keepdims=True)
    return x * lax.rsqrt(ss * (1.0 / n_real) + EPS) * gain


def _rope64(x, cos, sin):
    lane = _lane_iota(x.shape)
    rot = jnp.where(lane < 32, -pltpu.roll(x, LANE - 32, 1), pltpu.roll(x, 32, 1))
    return x * cos + rot * sin


def _split3(c):
    hi = c.astype(BF16).astype(F32)
    r1 = c - hi
    mid = r1.astype(BF16).astype(F32)
    lo = (r1 - mid).astype(BF16).astype(F32)
    return hi, mid, lo


def _hy_prep_kernel(p_ref, pos_ref, inv_ref, gq_ref, gk_ref, gfq_ref, gfk_ref, fb_ref,
                    qnt_ref, kct_ref, vct_ref, ks_ref, vst_ref, kw_ref, vwt_ref, gate_ref,
                    fqt_ref, fk_ref, fvt_ref, cedge_ref, carry_ref):
    i = pl.program_id(0)
    tm = PREP_TILE
    shp = (tm, LANE)
    lane = _lane_iota(shp)

    def blk(b):
        return p_ref[:, b * LANE:(b + 1) * LANE]

    ang = pos_ref[...] * inv_ref[...]
    real = lane < HEAD_DIM
    cos = jnp.where(real, jnp.cos(ang), 1.0)
    sin = jnp.where(real, jnp.sin(ang), 0.0)
    gq, gk, gfq, gfk = gq_ref[...], gk_ref[...], gfq_ref[...], gfk_ref[...]
    scale = HEAD_DIM ** -0.5 * LOG2E
    ones_row = lane == SUM_ROW
    ref_ones = jnp.where((lane >= REF_ROW) & (lane < REF_ROW + 3), 1.0, 0.0)

    for h in range(NSA_HEADS):
        q = _rope64(_head_rms(blk(HY_Q0 + h), gq, HEAD_DIM), cos, sin) * scale
        qnt_ref[h * LANE:(h + 1) * LANE, :] = q.T.astype(BF16)

    row = _row_iota(shp) + i * tm
    onehot = jnp.where(lane - HEAD_DIM == ((row // SLC_BLOCK) % SUPER_BLOCKS), 1.0, 0.0)
    for g in range(NSA_GROUPS):
        def kv(r):
            return blk(HY_KV0 + r * NSA_GROUPS + g)
        sl = slice(g * LANE, (g + 1) * LANE)
        kct_ref[g] = _rope64(kv(0), cos, sin)[:, :HEAD_DIM].astype(BF16)
        vct_ref[g] = kv(1)[:, :HEAD_DIM].astype(BF16)
        ks = _rope64(_head_rms(kv(2), gk, HEAD_DIM), cos, sin)
        ks_ref[:, sl] = (ks + onehot + ref_ones).astype(BF16)
        vst_ref[g, 0] = jnp.where(ones_row, 1.0, kv(3)).T.astype(BF16)
        kw_ref[:, sl] = _rope64(_head_rms(kv(4), gk, HEAD_DIM), cos, sin).astype(BF16)
        vwt = kv(5).T.astype(BF16)
        for cidx in range(tm // LANE):
            vwt_ref[g, cidx] = vwt[:, cidx * LANE:(cidx + 1) * LANE]
        gate_ref[g] = jax.nn.sigmoid(blk(HY_G0 + g)).T[:GATE_ROWS]

    @pl.when(i == 0)
    def _():
        carry_ref[...] = jnp.zeros_like(carry_ref)

    z = blk(HY_FF) + fb_ref[...]
    logf = jnp.minimum(z, 0.0) - jnp.log1p(jnp.exp(-jnp.abs(z)))
    tri = jnp.where(_row_iota((tm, tm)) >= _lane_iota((tm, tm)), 1.0, 0.0).astype(F32)
    cum = jnp.dot(tri, logf, precision=HIGHEST, preferred_element_type=F32) + carry_ref[...]
    carry_ref[...] = cum[tm - 1:tm, :]
    cedge_ref[0] = jnp.concatenate([cum[0:1] * LOG2E, cum[tm - 1:tm] * LOG2E, jnp.zeros((6, LANE), F32)], axis=0)

    for h in range(FOX_HEADS):
        c = jnp.broadcast_to(cum[:, h:h + 1], shp) * LOG2E
        hi, mid, lo = _split3(c)
        fq = _head_rms(blk(HY_F0 + h), gfq, HEAD_DIM) * scale
        fq = jnp.where(real, fq, jnp.where(lane == 64, hi, jnp.where(lane == 65, mid, jnp.where(
            lane == 66, lo, jnp.where(lane < 70, 1.0, 0.0)))))
        fk = _head_rms(blk(HY_F0 + FOX_HEADS + h), gfk, HEAD_DIM)
        fk = jnp.where(real, fk, jnp.where(lane < 67, 1.0, jnp.where(lane == 67, -hi, jnp.where(
            lane == 68, -mid, jnp.where(lane == 69, -lo, ref_ones)))))
        sl = slice(h * LANE, (h + 1) * LANE)
        fqt_ref[sl, :] = fq.T.astype(BF16)
        fk_ref[:, sl] = fk.astype(BF16)
        fvt_ref[h, 0] = jnp.where(ones_row, 1.0, blk(HY_F0 + 2 * FOX_HEADS + h)).T.astype(BF16)


def _feat_major(heads, s, tm):
    return (pl.BlockSpec((heads * LANE, tm), lambda i: (0, i)),
            jax.ShapeDtypeStruct((heads * LANE, s), BF16))


def _value_tiles(heads, s, tm, tk):
    return (pl.BlockSpec((heads, tm // tk, LANE, tk), lambda i: (0, i, 0, 0)),
            jax.ShapeDtypeStruct((heads, s // tk, LANE, tk), BF16))


def _hy_prep(proj, posf, inv128, gq, gk, gfq, gfk, fbias):
    s = proj.shape[0]
    tm = PREP_TILE
    assert tm == KV_TILE
    vec = pl.BlockSpec((1, LANE), lambda i: (0, 0))

    def rows(nb):
        return (pl.BlockSpec((tm, nb * LANE), lambda i: (i, 0)), jax.ShapeDtypeStruct((s, nb * LANE), BF16))

    tok = (pl.BlockSpec((NSA_GROUPS, tm, HEAD_DIM), lambda i: (0, i, 0)),
           jax.ShapeDtypeStruct((NSA_GROUPS, s, HEAD_DIM), BF16))
    gate = (pl.BlockSpec((NSA_GROUPS, GATE_ROWS, tm), lambda i: (0, 0, i)),
            jax.ShapeDtypeStruct((NSA_GROUPS, GATE_ROWS, s), F32))
    outs = [_feat_major(NSA_HEADS, s, tm), tok, tok, rows(NSA_GROUPS), _value_tiles(NSA_GROUPS, s, tm, KV_TILE),
            rows(NSA_GROUPS), _value_tiles(NSA_GROUPS, s, tm, LANE), gate,
            _feat_major(FOX_HEADS, s, tm), rows(FOX_HEADS), _value_tiles(FOX_HEADS, s, tm, KV_TILE),
            (pl.BlockSpec((1, 8, LANE), lambda i: (i, 0, 0)), jax.ShapeDtypeStruct((s // tm, 8, LANE), F32))]
    return pl.pallas_call(
        _hy_prep_kernel,
        grid=(s // tm,),
        in_specs=[pl.BlockSpec((tm, HY_BLOCKS * LANE), lambda i: (i, 0)), pl.BlockSpec((tm, 1), lambda i: (i, 0)),
                  vec, vec, vec, vec, vec, vec],
        out_specs=[o[0] for o in outs],
        out_shape=[o[1] for o in outs],
        scratch_shapes=[pltpu.VMEM((1, LANE), F32)],
        compiler_params=_cparams("arbitrary"),
        name="hybrid_prep",
    )(proj, posf, inv128, gq, gk, gfq, gfk, fbias)


def _compress_kernel(kc_ref, vc_ref, wk_ref, wv_ref, pek_ref, pev_ref, gk_ref, ko_ref, vo_ref):
    half = CMP_STRIDE * HEAD_DIM

    def comp(ch_ref, w_ref, pe_ref):
        ch = ch_ref[0]
        nc = ch.shape[0]
        a = jnp.dot(ch, w_ref[:half], preferred_element_type=F32)
        b = jnp.dot(ch, w_ref[half:], preferred_element_type=F32)
        nxt = pltpu.roll(b, nc - 1, 0)
        pe = jnp.dot(jnp.broadcast_to(pe_ref[...], (8, 2 * half)).astype(BF16), w_ref[...],
                     preferred_element_type=F32)[0:1]
        return a + nxt + pe

    ko_ref[0] = _head_rms(comp(kc_ref, wk_ref, pek_ref), gk_ref[...], HEAD_DIM).astype(BF16)
    vo_ref[0] = comp(vc_ref, wv_ref, pev_ref).T.astype(BF16)


def _compress(kct, vct, w_cmp, cmp_pe, k_norm):
    g, s, _ = kct.shape
    nc = s // CMP_STRIDE
    wide = CMP_STRIDE * HEAD_DIM
    kch = kct.reshape(g, nc, wide)
    vch = vct.reshape(g, nc, wide)
    w_pad = jnp.pad(w_cmp, ((0, 0), (0, 0), (0, LANE - HEAD_DIM))).astype(BF16)
    ch = pl.BlockSpec((1, nc, wide), lambda i: (i, 0, 0))
    wspec = pl.BlockSpec((2 * wide, LANE), lambda i: (0, 0))
    pespec = pl.BlockSpec((1, 2 * wide), lambda i: (0, 0))
    return pl.pallas_call(
        _compress_kernel,
        grid=(g,),
        in_specs=[ch, ch, wspec, wspec, pespec, pespec, pl.BlockSpec((1, LANE), lambda i: (0, 0))],
        out_specs=[pl.BlockSpec((1, nc, LANE), lambda i: (i, 0, 0)),
                   pl.BlockSpec((1, LANE, nc), lambda i: (i, 0, 0))],
        out_shape=[jax.ShapeDtypeStruct((g, nc, LANE), BF16), jax.ShapeDtypeStruct((g, LANE, nc), BF16)],
        compiler_params=_cparams("parallel"),
        name="nsa_compress",
    )(kch, vch, w_pad[0], w_pad[1], cmp_pe[0].reshape(1, 2 * wide).astype(F32),
      cmp_pe[1].reshape(1, 2 * wide).astype(F32), _pad_lanes(k_norm))


def _masked_softmax_t(s, mask):
    s = jnp.where(mask, s, NEG)
    m = jnp.max(s, axis=0, keepdims=True)
    e = jnp.where(mask, jnp.exp2(s - m), 0.0)
    return e, 1.0 / jnp.maximum(jnp.sum(e, axis=0, keepdims=True), 1e-30)


def _online_steps(steps, ms, acc_ref):
    ms = list(ms)

    def scores(step):
        k_tile, qa, _, c, mask = step
        s = jnp.dot(k_tile, qa, preferred_element_type=F32)
        if mask is not None:
            s = jnp.where(mask, s, NEG)
        return s, jnp.max(s, axis=0, keepdims=True)

    nxt = scores(steps[0])
    for idx, (_, _, vt, c, _) in enumerate(steps):
        sl = slice(c * Q_STRIP, (c + 1) * Q_STRIP)
        s, s_max = nxt
        if idx + 1 < len(steps):
            nxt = scores(steps[idx + 1])
        m_new = jnp.maximum(ms[c], s_max)
        a = jnp.exp2(ms[c] - m_new)
        p = jnp.exp2((s - m_new).astype(BF16))
        ms[c] = m_new
        acc_ref[:, sl] = a * acc_ref[:, sl] + jnp.dot(vt, p, preferred_element_type=F32)
    return tuple(ms)


def _m_init(n_strips):
    return tuple(jnp.full((1, Q_STRIP), NEG, F32) for _ in range(n_strips))


def _with_ref_rows(qa, m):
    hi, mid, lo = _split3(-m)
    r = _row_iota((LANE - REF_SLAB, Q_STRIP)) + REF_SLAB
    slab = jnp.where(r == REF_ROW, hi, jnp.where(r == REF_ROW + 1, mid, jnp.where(r == REF_ROW + 2, lo, 0.0)))
    return jnp.concatenate([qa[:REF_SLAB], slab.astype(BF16)], axis=0)


def _first_tile_max(k_tile, qa_strips, masks):
    return tuple(jnp.max(jnp.where(mask, jnp.dot(k_tile, qa, preferred_element_type=F32), NEG), axis=0, keepdims=True)
                 for qa, mask in zip(qa_strips, masks))


def _fast_steps(steps, state, acc_ref):
    state = list(state)
    for k, step in enumerate(steps):
        assert all(prev[3] != step[3] for prev in steps[max(k - SCORE_LOOKAHEAD + 1, 0):k])

    def scores(step):
        k_tile, qa, _, c, mask = step
        s = jnp.dot(k_tile, _with_ref_rows(qa, state[c][0]), preferred_element_type=F32)
        if mask is not None:
            s = jnp.where(mask, s, NEG)
        return s

    ahead = [scores(st) for st in steps[:SCORE_LOOKAHEAD]]
    for idx, (_, _, vt, c, _) in enumerate(steps):
        sl = slice(c * Q_STRIP, (c + 1) * Q_STRIP)
        s = ahead.pop(0)
        m, worst = state[c]
        cm = jnp.max(s, axis=0, keepdims=True)
        inc = jnp.maximum(cm, 0.0)
        state[c] = (m + inc, jnp.maximum(worst, cm))
        if idx + SCORE_LOOKAHEAD < len(steps):
            ahead.append(scores(steps[idx + SCORE_LOOKAHEAD]))
        p = jnp.exp2(s).astype(BF16)
        acc_ref[:, sl] = jnp.exp2(-inc) * (acc_ref[:, sl] + jnp.dot(vt, p, preferred_element_type=F32))
    return tuple(state)


def _flat(state):
    return tuple(x for pair in state for x in pair)


def _nest(flat):
    return tuple((flat[2 * c], flat[2 * c + 1]) for c in range(len(flat) // 2))


def _nsa_kernel(qt_ref, kc_ref, vct_ref, ks_ref, vst_ref, kw_ref, vwt_ref, gate_ref, ovt_ref, o_ref,
                qaug_ref, acc_ref, *, n_sel):
    i = pl.program_id(1)
    tq = Q_TILE_NSA
    cols = NSA_HPG * tq
    qs = i * tq
    nc = kc_ref.shape[1]
    nslc = ovt_ref.shape[0]
    n_super = nslc // SUPER_BLOCKS

    qt = jnp.concatenate([qt_ref[h * LANE:(h + 1) * LANE, :] for h in range(NSA_HPG)], axis=1)
    tq_row = qs + (_lane_iota((1, cols)) % tq)

    s = jnp.dot(kc_ref[0], qt, preferred_element_type=F32)
    cmp_end = _row_iota((nc, 1)) * CMP_STRIDE + (CMP_LEN - 1)
    e, inv_l = _masked_softmax_t(s, cmp_end <= tq_row)
    o_cmp = jnp.dot(vct_ref[0], e.astype(BF16), preferred_element_type=F32) * inv_l

    psum = e[:, 0:tq] * inv_l[:, 0:tq]
    for h in range(1, NSA_HPG):
        psum = psum + e[:, h * tq:(h + 1) * tq] * inv_l[:, h * tq:(h + 1) * tq]
    p_hi = psum.astype(BF16)
    p_lo = (psum - p_hi.astype(F32)).astype(BF16)
    ovt = ovt_ref[...]
    imp = (jnp.dot(ovt, p_hi, preferred_element_type=F32)
           + jnp.dot(ovt, p_lo, preferred_element_type=F32))

    jj = _row_iota((nslc, tq))
    tq_blk = qs + _lane_iota((nslc, tq))
    cur = tq_blk // SLC_BLOCK
    forced = (jj == 0) | (jj == cur) | (jj == cur - 1)
    causal_blk = jj * SLC_BLOCK <= tq_blk
    val = jnp.where(forced, imp + BIG, imp)
    val = jnp.where(causal_blk, val, NEG)

    jjf = jj.astype(F32)

    def pick(_, carry):
        val, sel = carry
        mx = jnp.max(val, axis=0, keepdims=True)
        idx = jnp.min(jnp.where(val == mx, jjf, float(nslc)), axis=0, keepdims=True)
        hit = jjf == idx
        return jnp.where(hit, -jnp.inf, val), jnp.where(hit, 1.0, sel)

    _, sel = lax.fori_loop(0, n_sel, pick, (val, jnp.zeros((nslc, tq), F32)))
    bias_t = jnp.where((sel > 0.0) & causal_blk, 0.0, NEG)

    q_rows = qt[:HEAD_DIM].astype(F32)
    spare = jnp.zeros((LANE - HEAD_DIM - SUPER_BLOCKS, cols), F32)
    for st in range(n_super):
        b = bias_t[st * SUPER_BLOCKS:(st + 1) * SUPER_BLOCKS]
        b = jnp.concatenate([b] * NSA_HPG, axis=1)
        qaug_ref[st] = jnp.concatenate([q_rows, b, spare], axis=0).astype(BF16)

    tk = KV_TILE
    per_super = SUPER_BLOCKS * SLC_BLOCK // tk
    j_last = (qs + tq - 1) // tk
    n_strips = cols // Q_STRIP
    strips = [slice(c * Q_STRIP, (c + 1) * Q_STRIP) for c in range(n_strips)]

    def causal_masks(j):
        kpos = j * tk + _row_iota((tk, 1))
        return [kpos <= tq_row[:, sl] for sl in strips]

    def slc_steps(j, masks=None):
        k0 = pl.multiple_of(j * tk, tk)
        k_tile, vt, st = ks_ref[pl.ds(k0, tk), :], vst_ref[0, j], j // per_super
        return [(k_tile, qaug_ref[st, :, strips[c]], vt, c, None if masks is None else masks[c])
                for c in range(n_strips)]

    def pair(jj, flat):
        return _flat(_fast_steps(slc_steps(2 * jj) + slc_steps(2 * jj + 1), _nest(flat), acc_ref))

    def single(j, flat):
        return _flat(_fast_steps(slc_steps(j), _nest(flat), acc_ref))

    acc_ref[...] = jnp.zeros(acc_ref.shape, F32)
    head_rows = 16
    head_masks = [_row_iota((head_rows, 1)) <= tq_row[:, sl] for sl in strips]
    m0 = _first_tile_max(ks_ref[0:head_rows, :], [qaug_ref[0, :, sl] for sl in strips], head_masks)
    n_pairs = j_last // 2
    flat = lax.fori_loop(0, n_pairs, pair, _flat(tuple((m, jnp.zeros_like(m)) for m in m0)))
    flat = lax.fori_loop(2 * n_pairs, j_last, single, flat)
    state = _fast_steps(slc_steps(j_last, causal_masks(j_last)), _nest(flat), acc_ref)
    worst = jnp.max(jnp.concatenate([w for _, w in state], axis=1))

    @pl.when(worst > EXP_GUARD)
    def _():
        acc_ref[...] = jnp.zeros(acc_ref.shape, F32)
        lax.fori_loop(0, j_last + 1, lambda j, ms: _online_steps(slc_steps(j, causal_masks(j)), ms, acc_ref),
                      _m_init(n_strips))

    acc = acc_ref[...]
    o_slc = acc / jnp.maximum(acc[SUM_ROW:SUM_ROW + 1], 1e-30)

    wlen = WINDOW + tq
    ws = pl.multiple_of(jnp.maximum(qs - WINDOW, 0), tq)
    s = jnp.dot(kw_ref[pl.ds(ws, wlen), :], qt, preferred_element_type=F32)
    dist = tq_row - (ws + _row_iota((wlen, 1)))
    e, inv_l = _masked_softmax_t(s, (dist >= 0) & (dist < WINDOW))
    e = e.astype(BF16)
    wb = ws // LANE
    o_win = jnp.zeros((LANE, cols), F32)
    for c in range(wlen // LANE):
        o_win = o_win + jnp.dot(vwt_ref[0, wb + c], e[c * LANE:(c + 1) * LANE], preferred_element_type=F32)
    o_win = o_win * inv_l

    gate = gate_ref[0]
    for h in range(NSA_HPG):
        sl = slice(h * tq, (h + 1) * tq)
        o = (gate[3 * h:3 * h + 1] * o_cmp[:, sl] + gate[3 * h + 1:3 * h + 2] * o_slc[:, sl]
             + gate[3 * h + 2:3 * h + 3] * o_win[:, sl])
        o_ref[:, h * LANE:(h + 1) * LANE] = o.T.astype(BF16)


def _overlap_t(s, nslc_pad):
    nc = s // CMP_STRIDE
    cmp_start = np.arange(nc) * CMP_STRIDE
    slc_start = np.arange(nslc_pad) * SLC_BLOCK
    ov = np.clip(np.minimum(cmp_start[:, None] + CMP_LEN, slc_start[None, :] + SLC_BLOCK)
                 - np.maximum(cmp_start[:, None], slc_start[None, :]), 0, None) / CMP_STRIDE
    ov[nc - CMP_LEN // CMP_STRIDE + 1:, :] = 0.0
    ov[:, s // SLC_BLOCK:] = 0.0
    return jnp.asarray(ov.T, BF16)


def _nsa_attention(qnt, kc, vct, ks, vst, kw, vwt, gates):
    s = qnt.shape[1]
    nc = s // CMP_STRIDE
    n_slc = s // SLC_BLOCK
    nslc_pad = -(-n_slc // LANE) * LANE
    tq = Q_TILE_NSA
    cols = NSA_HPG * tq
    once = pl.Buffered(1)
    res = pl.BlockSpec((s, LANE), lambda g, i: (0, g), pipeline_mode=once)
    return pl.pallas_call(
        functools.partial(_nsa_kernel, n_sel=min(SLC_TOPK, n_slc)),
        grid=(NSA_GROUPS, s // tq),
        in_specs=[pl.BlockSpec((NSA_HPG * LANE, tq), lambda g, i: (g, i)),
                  pl.BlockSpec((1, nc, LANE), lambda g, i: (g, 0, 0), pipeline_mode=once),
                  pl.BlockSpec((1, LANE, nc), lambda g, i: (g, 0, 0), pipeline_mode=once),
                  res, pl.BlockSpec((1, s // KV_TILE, LANE, KV_TILE), lambda g, i: (g, 0, 0, 0),
                                    pipeline_mode=once),
                  res, pl.BlockSpec((1, s // LANE, LANE, LANE), lambda g, i: (g, 0, 0, 0), pipeline_mode=once),
                  pl.BlockSpec((1, GATE_ROWS, tq), lambda g, i: (g, 0, i)),
                  pl.BlockSpec((nslc_pad, nc), lambda g, i: (0, 0), pipeline_mode=once)],
        out_specs=pl.BlockSpec((tq, NSA_HPG * LANE), lambda g, i: (i, g)),
        out_shape=jax.ShapeDtypeStruct((s, NSA_HEADS * LANE), BF16),
        scratch_shapes=[pltpu.VMEM((nslc_pad // SUPER_BLOCKS, LANE, cols), BF16),
                        pltpu.VMEM((LANE, cols), F32)],
        compiler_params=_cparams("parallel", "arbitrary"),
        name="nsa_attention",
    )(qnt, kc, vct, ks, vst, kw, vwt, gates, _overlap_t(s, nslc_pad))


def _flash_kernel(cfirst_ref, clast_ref, slack_ref, qt_ref, k_ref, vt_ref, o_ref, acc_ref):
    h = pl.program_id(0)
    i = pl.program_id(1)
    tq, tk = Q_TILE_FLASH, KV_TILE
    n_tiles = k_ref.shape[0] // tk
    acc_ref[...] = jnp.zeros(acc_ref.shape, F32)

    n_strips = tq // Q_STRIP
    per_q = tq // tk
    qas = [qt_ref[:, c * Q_STRIP:(c + 1) * Q_STRIP] for c in range(n_strips)]

    def tile_steps(j, d=None):
        k0 = pl.multiple_of(j * tk, tk)
        k_tile, vt = k_ref[pl.ds(k0, tk), :], vt_ref[0, j]
        steps = []
        for c in range(n_strips):
            mask = None
            if d is not None:
                if d * tk > (c + 1) * Q_STRIP - 1:
                    continue
                if (d + 1) * tk - 1 > c * Q_STRIP:
                    shp = (tk, Q_STRIP)
                    mask = _row_iota(shp) + d * tk <= _lane_iota(shp) + c * Q_STRIP
            steps.append((k_tile, qas[c], vt, c, mask))
        return steps

    def any_tile_masks(j):
        shp = (tk, Q_STRIP)
        return [_row_iota(shp) + j * tk <= _lane_iota(shp) + (i * tq + c * Q_STRIP) for c in range(n_strips)]

    def below(t, flat):
        jj = i - 1 - t
        bound = (slack_ref[0] + cfirst_ref[h * n_tiles + i * per_q]
                 - clast_ref[h * n_tiles + jj * per_q + per_q - 1])

        def run(flat):
            steps = [st for u in range(per_q) for st in tile_steps(jj * per_q + (per_q - 1 - u))]
            return _flat(_fast_steps(steps, _nest(flat), acc_ref))

        return lax.cond(bound >= -SKIP_MARGIN, run, lambda flat: flat, flat)

    diag0 = i * per_q
    own = [(c * Q_STRIP) // tk for c in range(n_strips)]
    m0 = tuple(_first_tile_max(k_ref[pl.ds(pl.multiple_of((diag0 + own[c]) * tk, tk), tk), :], [qas[c]],
                               [any_tile_masks(diag0 + own[c])[c]])[0] for c in range(n_strips))
    steps = [st for d in reversed(range(per_q)) for st in tile_steps(diag0 + d, d)]
    state = _fast_steps(steps, tuple((m, jnp.zeros_like(m)) for m in m0), acc_ref)
    state = _nest(lax.fori_loop(0, i, below, _flat(state)))
    worst = jnp.max(jnp.concatenate([w for _, w in state], axis=1))

    @pl.when(worst > EXP_GUARD)
    def _():
        acc_ref[...] = jnp.zeros(acc_ref.shape, F32)

        def exact(j, ms):
            k0 = pl.multiple_of(j * tk, tk)
            k_tile, vt, masks = k_ref[pl.ds(k0, tk), :], vt_ref[0, j], any_tile_masks(j)
            return _online_steps([(k_tile, qas[c], vt, c, masks[c]) for c in range(n_strips)], ms, acc_ref)

        lax.fori_loop(0, (i + 1) * per_q, exact, _m_init(n_strips))

    acc = acc_ref[...]
    o = acc / acc[SUM_ROW:SUM_ROW + 1]
    for c0 in range(0, tq, LANE):
        o_ref[c0:c0 + LANE, :] = o[:, c0:c0 + LANE].T.astype(BF16)


def _causal_attention(qt, k, vt, bias_edges=None, slack=None):
    s, width = k.shape
    heads = width // LANE
    assert Q_TILE_FLASH % KV_TILE == 0 and s % Q_TILE_FLASH == 0
    n_tiles = s // KV_TILE
    if bias_edges is None:
        first = last = jnp.zeros((heads * n_tiles,), F32)
        slack = jnp.full((1,), -NEG, F32)
    else:
        first, last = (e.reshape(heads * n_tiles).astype(F32) for e in bias_edges)
    grid_spec = pltpu.PrefetchScalarGridSpec(
        num_scalar_prefetch=3,
        grid=(heads, s // Q_TILE_FLASH),
        in_specs=[pl.BlockSpec((LANE, Q_TILE_FLASH), lambda h, i, *_: (h, i)),
                  pl.BlockSpec((s, LANE), lambda h, i, *_: (0, h)),
                  pl.BlockSpec((1, n_tiles, LANE, KV_TILE), lambda h, i, *_: (h, 0, 0, 0))],
        out_specs=pl.BlockSpec((Q_TILE_FLASH, LANE), lambda h, i, *_: (i, h)),
        scratch_shapes=[pltpu.VMEM((LANE, Q_TILE_FLASH), F32)],
    )
    return pl.pallas_call(
        _flash_kernel,
        grid_spec=grid_spec,
        out_shape=jax.ShapeDtypeStruct((s, width), BF16),
        compiler_params=_cparams("parallel", "arbitrary"),
        name="causal_attention",
    )(first, last, slack.astype(F32), qt, k, vt)


def _out_proj_kernel(oa_ref, ob_ref, wa_ref, wb_ref, x_ref, g_ref, o_ref):
    y = jnp.dot(oa_ref[...], wa_ref[...], preferred_element_type=F32)
    y = y + jnp.dot(ob_ref[...], wb_ref[...], preferred_element_type=F32)
    o_ref[...] = x_ref[...] + g_ref[...] * y


def _out_proj(oa, ob, cola, colb, wa, wb, x2, gate):
    s, d = x2.shape
    ka = wa.shape[0]
    tm = ROW_TILE
    return pl.pallas_call(
        _out_proj_kernel,
        grid=(s // tm,),
        in_specs=[pl.BlockSpec((tm, ka), lambda i: (i, cola)), pl.BlockSpec((tm, ka), lambda i: (i, colb)),
                  pl.BlockSpec((ka, d), lambda i: (0, 0)), pl.BlockSpec((ka, d), lambda i: (0, 0)),
                  pl.BlockSpec((tm, d), lambda i: (i, 0)), pl.BlockSpec((1, d), lambda i: (0, 0))],
        out_specs=pl.BlockSpec((tm, d), lambda i: (i, 0)),
        out_shape=jax.ShapeDtypeStruct((s, d), F32),
        compiler_params=_cparams("parallel"),
        name="out_proj",
    )(oa, ob, wa, wb, x2, gate)


def _mla_prep_kernel(p_ref, pos_ref, inv_ref, gqa_ref, gkva_ref, wuq_ref, wuk_ref, wuv_ref,
                     gq_ref, gk_ref, gkr_ref, qt_ref, k_ref, vt_ref):
    shp = (PREP_TILE, LANE)
    lane = _lane_iota(shp)
    nope = lane < QK_NOPE
    rope = (lane >= QK_NOPE) & (lane < QK_NOPE + QK_ROPE)
    ref_ones = jnp.where((lane >= REF_ROW) & (lane < REF_ROW + 3), 1.0, 0.0)
    ang = pos_ref[...] * inv_ref[...]
    cos = jnp.where(rope, jnp.cos(ang), 1.0)
    sin = jnp.where(rope, jnp.sin(ang), 0.0)

    def rope32(x):
        half = QK_ROPE // 2
        rot = jnp.where(lane < QK_NOPE + half, -pltpu.roll(x, LANE - half, 1), pltpu.roll(x, half, 1))
        return x * cos + rot * sin

    def low_rank_norm(x, g):
        ms = jnp.mean(x * x, axis=-1, keepdims=True)
        return (x * lax.rsqrt(ms + EPS) * g).astype(BF16)

    nq = Q_LORA // LANE
    cq = low_rank_norm(p_ref[:, :Q_LORA], gqa_ref[...])
    ckv = low_rank_norm(p_ref[:, Q_LORA:Q_LORA + KV_LORA], gkva_ref[...])
    kr = p_ref[:, (nq + KV_LORA // LANE) * LANE:(nq + KV_LORA // LANE + 1) * LANE]
    k_rope = rope32(_head_rms(kr, gkr_ref[...], QK_ROPE))

    gq, gk = gq_ref[...], gk_ref[...]
    scale = (QK_NOPE + QK_ROPE) ** -0.5 * LOG2E
    pair = 2 * LANE
    for hp in range(MLA_HEADS // 2):
        cols = slice(hp * pair, (hp + 1) * pair)
        q2 = jnp.dot(cq, wuq_ref[:, cols], preferred_element_type=F32)
        k2 = jnp.dot(ckv, wuk_ref[:, cols], preferred_element_type=F32)
        v2 = jnp.dot(ckv, wuv_ref[:, cols], preferred_element_type=F32)
        for sub in range(2):
            head = 2 * hp + sub
            sl = slice(head * LANE, (head + 1) * LANE)
            half = slice(sub * LANE, (sub + 1) * LANE)
            x = q2[:, half]
            ss_n = jnp.sum(jnp.where(nope, x * x, 0.0), axis=-1, keepdims=True)
            ss_r = jnp.sum(jnp.where(rope, x * x, 0.0), axis=-1, keepdims=True)
            inv_rms = jnp.where(nope, lax.rsqrt(ss_n * (1.0 / QK_NOPE) + EPS),
                                lax.rsqrt(ss_r * (1.0 / QK_ROPE) + EPS))
            qt_ref[sl, :] = (rope32(x * inv_rms * gq) * scale).T.astype(BF16)
            kn = _head_rms(k2[:, half], gk, QK_NOPE)
            k_ref[:, sl] = (kn + k_rope + ref_ones).astype(BF16)
            vt_ref[head, 0] = jnp.where(lane == SUM_ROW, 1.0, v2[:, half]).T.astype(BF16)


def _mla_prep(proj, posf, inv128, gqa, gkva, wuq, wuk, wuv, gq, gk, gkr):
    s, n = proj.shape
    tm = PREP_TILE
    assert tm == KV_TILE

    def full(a):
        return pl.BlockSpec(a.shape, lambda i: (0, 0))

    outs = [_feat_major(MLA_HEADS, s, tm),
            (pl.BlockSpec((tm, MLA_HEADS * LANE), lambda i: (i, 0)),
             jax.ShapeDtypeStruct((s, MLA_HEADS * LANE), BF16)),
            _value_tiles(MLA_HEADS, s, tm, KV_TILE)]
    args = (inv128, gqa, gkva, wuq, wuk, wuv, gq, gk, gkr)
    return pl.pallas_call(
        _mla_prep_kernel,
        grid=(s // tm,),
        in_specs=[pl.BlockSpec((tm, n), lambda i: (i, 0)), pl.BlockSpec((tm, 1), lambda i: (i, 0))]
                 + [full(a) for a in args],
        out_specs=[o[0] for o in outs],
        out_shape=[o[1] for o in outs],
        compiler_params=_cparams("parallel"),
        name="mla_prep",
    )(proj, posf, *args)


def _rank_lt(v, k):
    n = v.shape[0]
    row = _row_iota(v.shape)
    rank = jnp.zeros(v.shape, F32)
    for b in range(n):
        vb = v[b:b + 1, :]
        rank = rank + jnp.where((vb > v) | ((vb == v) & (row > b)), 1.0, 0.0)
    return rank < k


def _moe_route_kernel(x_ref, g_ref, sc_ref, sh_ref, wr_ref, rb_ref, h_ref, pos_ref, wt_ref, cnt_ref):
    tm = ROW_TILE
    h = _norm_mod(x_ref[...], g_ref[...], sc_ref[...], sh_ref[...])
    h_ref[...] = h.astype(BF16)
    logits = jnp.dot(h, wr_ref[...], precision=HIGHEST, preferred_element_type=F32)
    lt = logits.T[:N_EXPERTS]
    scores = jax.nn.sigmoid(lt)
    sel = scores + rb_ref[...]

    per = N_EXPERTS // N_GROUPS
    grp = sel.reshape(N_GROUPS, per, tm)
    sub = lax.broadcasted_iota(jnp.int32, grp.shape, 1)
    m1 = jnp.max(grp, axis=1, keepdims=True)
    first = jnp.min(jnp.where(grp == m1, sub, per), axis=1, keepdims=True)
    m2 = jnp.max(jnp.where(sub == first, -jnp.inf, grp), axis=1, keepdims=True)
    gscore = (m1 + m2).reshape(N_GROUPS, tm)
    gmask = _rank_lt(gscore, TOPK_GROUPS)
    emask = jnp.broadcast_to(gmask.reshape(N_GROUPS, 1, tm), grp.shape).reshape(N_EXPERTS, tm)
    chosen = _rank_lt(jnp.where(emask, sel, NEG), TOP_K)

    w = jnp.where(chosen, scores, 0.0)
    wt_ref[...] = w / jnp.sum(w, axis=0, keepdims=True) * ROUTED_SCALE

    upper = jnp.where(_row_iota((tm, tm)) <= _lane_iota((tm, tm)), 1.0, 0.0).astype(BF16)
    incl = jnp.dot(jnp.where(chosen, 1.0, 0.0).astype(BF16), upper, preferred_element_type=F32)
    pos_ref[...] = jnp.where(chosen, incl - 1.0, -1.0)
    cnt_ref[0] = jnp.broadcast_to(incl[:, tm - 1:tm], (N_EXPERTS, LANE))


def _moe_route(x2, g, sc, sh, w_router_pad, router_bias_col):
    s, d = x2.shape
    tm = ROW_TILE
    vec = pl.BlockSpec((1, d), lambda i: (0, 0))
    et = pl.BlockSpec((N_EXPERTS, tm), lambda i: (0, i))
    return pl.pallas_call(
        _moe_route_kernel,
        grid=(s // tm,),
        in_specs=[pl.BlockSpec((tm, d), lambda i: (i, 0)), vec, vec, vec,
                  pl.BlockSpec((d, LANE), lambda i: (0, 0)),
                  pl.BlockSpec((N_EXPERTS, 1), lambda i: (0, 0))],
        out_specs=[pl.BlockSpec((tm, d), lambda i: (i, 0)), et, et,
                   pl.BlockSpec((1, N_EXPERTS, LANE), lambda i: (i, 0, 0))],
        out_shape=[jax.ShapeDtypeStruct((s, d), BF16), jax.ShapeDtypeStruct((N_EXPERTS, s), F32),
                   jax.ShapeDtypeStruct((N_EXPERTS, s), F32),
                   jax.ShapeDtypeStruct((s // tm, N_EXPERTS, LANE), F32)],
        compiler_params=_cparams("parallel"),
        name="moe_route",
    )(x2, g, sc, sh, w_router_pad, router_bias_col)


def _moe_kernel(cnt_ref, x_ref, h_ref, pos_ref, wt_ref, wg_ref, wu_ref, wd_ref, sg_ref, su_ref, sd_ref,
                g2_ref, o_ref, acc_ref):
    i = pl.program_id(0)
    e = pl.program_id(1)
    tm = ROW_TILE
    r = MOE_CHUNK

    @pl.when(e == 0)
    def _():
        h = h_ref[...]
        a = jnp.dot(h, sg_ref[...], preferred_element_type=F32)
        a = a * jax.nn.sigmoid(a) * jnp.dot(h, su_ref[...], preferred_element_type=F32)
        acc_ref[...] = jnp.dot(a.astype(BF16), sd_ref[...], preferred_element_type=F32)

    first = e * MOE_EXPERTS_PER_STEP
    n = cnt_ref[i * N_EXPERTS + first]
    for k in range(1, MOE_EXPERTS_PER_STEP):
        n = jnp.maximum(n, cnt_ref[i * N_EXPERTS + first + k])
    prows = [pos_ref[pl.ds(first + k, 1), :] for k in range(MOE_EXPERTS_PER_STEP)]
    wrows = [wt_ref[pl.ds(first + k, 1), :] for k in range(MOE_EXPERTS_PER_STEP)]

    def chunk(c, _):
        slot = (_row_iota((r, tm)) + c * r).astype(F32)
        hits = [prow == slot for prow in prows]
        onehot = jnp.concatenate([jnp.where(hit, 1.0, 0.0).astype(BF16) for hit in hits], axis=0)
        xg = jnp.dot(onehot, h_ref[...], preferred_element_type=F32).astype(BF16)
        ys = []
        for k in range(MOE_EXPERTS_PER_STEP):
            xk = xg[k * r:(k + 1) * r]
            a = jnp.dot(xk, wg_ref[k], preferred_element_type=F32)
            a = a * jax.nn.sigmoid(a) * jnp.dot(xk, wu_ref[k], preferred_element_type=F32)
            y = jnp.dot(a.astype(BF16), wd_ref[k], preferred_element_type=F32)
            wr = jnp.sum(jnp.where(hits[k], wrows[k], 0.0), axis=-1, keepdims=True)
            ys.append((y * wr).astype(BF16))
        acc_ref[...] += _dot_tn(onehot, jnp.concatenate(ys, axis=0))
        return 0

    lax.fori_loop(0, (n + r - 1) // r, chunk, 0)

    @pl.when(e == N_EXPERTS // MOE_EXPERTS_PER_STEP - 1)
    def _():
        o_ref[...] = x_ref[...] + g2_ref[...] * acc_ref[...]


def _moe_experts(counts, x2, h, pos_t, w_t, wg, wu, wd, sg, su, sd, g2):
    s, d = x2.shape
    tm = ROW_TILE
    ff = wg.shape[2]
    tile = pl.BlockSpec((tm, d), lambda i, e, c: (i, 0))
    et = pl.BlockSpec((N_EXPERTS, tm), lambda i, e, c: (0, i))

    def const(a):
        return pl.BlockSpec(a.shape, lambda i, e, c: (0,) * a.ndim)

    per = MOE_EXPERTS_PER_STEP
    grid_spec = pltpu.PrefetchScalarGridSpec(
        num_scalar_prefetch=1,
        grid=(s // tm, N_EXPERTS // per),
        in_specs=[tile, tile, et, et,
                  pl.BlockSpec((per, d, ff), lambda i, e, c: (e, 0, 0)),
                  pl.BlockSpec((per, d, ff), lambda i, e, c: (e, 0, 0)),
                  pl.BlockSpec((per, ff, d), lambda i, e, c: (e, 0, 0)),
                  const(sg), const(su), const(sd), const(g2)],
        out_specs=tile,
        scratch_shapes=[pltpu.VMEM((tm, d), F32)],
    )
    return pl.pallas_call(
        _moe_kernel,
        grid_spec=grid_spec,
        out_shape=jax.ShapeDtypeStruct((s, d), F32),
        compiler_params=_cparams("parallel", "arbitrary"),
        name="moe_experts",
    )(counts, x2, h, pos_t, w_t, wg, wu, wd, sg, su, sd, g2)


def _pad_lanes(v, width=LANE, offset=0):
    out = jnp.zeros((1, width), F32)
    return out.at[0, offset:offset + v.shape[0]].set(v.astype(F32))


def _head_cols(w, n_heads, dim):
    d = w.shape[0]
    w3 = w.reshape(d, n_heads, dim)
    return jnp.pad(w3, ((0, 0), (0, 0), (0, LANE - dim))).reshape(d, n_heads * LANE)


def _hybrid_w_in(w_in):
    d = w_in.shape[0]
    nq = NSA_HEADS * HEAD_DIM
    nkv = 6 * NSA_GROUPS * HEAD_DIM
    ng = 3 * NSA_HEADS
    nf = 3 * FOX_HEADS * HEAD_DIM
    c0, c1, c2, c3 = nq, nq + nkv, nq + nkv + ng, nq + nkv + ng + nf
    gates = w_in[:, c1:c2].reshape(d, NSA_GROUPS, 3 * NSA_HPG)
    gates = jnp.pad(gates, ((0, 0), (0, 0), (0, LANE - 3 * NSA_HPG))).reshape(d, NSA_GROUPS * LANE)
    ff = jnp.pad(w_in[:, c3:], ((0, 0), (0, 2 * LANE - FOX_HEADS)))
    return jnp.concatenate([
        _head_cols(w_in[:, :c0], NSA_HEADS, HEAD_DIM),
        _head_cols(w_in[:, c0:c1], 6 * NSA_GROUPS, HEAD_DIM),
        _head_cols(w_in[:, c2:c3], 3 * FOX_HEADS, HEAD_DIM),
        gates, ff], axis=1).astype(BF16)


def _pad_head_rows(w, n_heads, dim):
    d = w.shape[1]
    w3 = w.reshape(n_heads, dim, d)
    return jnp.pad(w3, ((0, 0), (0, LANE - dim), (0, 0))).reshape(n_heads * LANE, d).astype(BF16)


def _rope_inv(dim, offset):
    inv = ROPE_THETA ** (-jnp.arange(0, dim, 2, dtype=F32) / dim)
    return _pad_lanes(jnp.concatenate([inv, inv]), offset=offset)


def _hybrid_mixer(x2, posf, mods, norm_g, w_in, fox_f_bias, nsa_q_norm, nsa_k_norm, nsa_cmp_pe, nsa_w_cmp,
                  fox_q_norm, fox_k_norm, w_out):
    sh1, sc1, g1 = mods
    proj = _norm_mod_matmul(x2, norm_g, sc1, sh1, _hybrid_w_in(w_in), tn=HY_COL_TILE)
    (qnt, kct, vct, ks, vst, kw, vwt, gates, fqt, fk, fvt, cedge) = _hy_prep(
        proj, posf, _rope_inv(HEAD_DIM, 0), _pad_lanes(nsa_q_norm), _pad_lanes(nsa_k_norm),
        _pad_lanes(fox_q_norm), _pad_lanes(fox_k_norm), _pad_lanes(fox_f_bias))
    kc, vc_t = _compress(kct, vct, nsa_w_cmp, nsa_cmp_pe, nsa_k_norm)
    o_a = _nsa_attention(qnt, kc, vc_t, ks, vst, kw, vwt, gates)
    slack = (2.0 * HEAD_DIM ** 0.5 * LOG2E) * jnp.max(jnp.abs(fox_q_norm)) * jnp.max(jnp.abs(fox_k_norm))
    edges = (cedge[:, 0, :FOX_HEADS].T, cedge[:, 1, :FOX_HEADS].T)
    o_b = _causal_attention(fqt, fk, fvt, edges, slack.reshape(1))
    half = NSA_HEADS * HEAD_DIM
    wa = _pad_head_rows(w_out[:half], NSA_HEADS, HEAD_DIM)
    wb = _pad_head_rows(w_out[half:], FOX_HEADS, HEAD_DIM)
    return _out_proj(o_a, o_b, 0, 0, wa, wb, x2, g1)


def _mla_mixer(x2, posf, mods, norm_g, w_in, q_a_norm, kv_a_norm, w_uq, w_ukv, qn_norm, kn_norm, qr_norm,
               kr_norm, w_out):
    sh1, sc1, g1 = mods
    d = x2.shape[1]
    w_kr = jnp.zeros((d, LANE), F32).at[:, QK_NOPE:QK_NOPE + QK_ROPE].set(w_in[:, Q_LORA + KV_LORA:])
    w_in_p = jnp.concatenate([w_in[:, :Q_LORA + KV_LORA], w_kr], axis=1).astype(BF16)
    proj = _norm_mod_matmul(x2, norm_g, sc1, sh1, w_in_p, tn=w_in_p.shape[1])
    hq = QK_NOPE + QK_ROPE
    wuq = _head_cols(w_uq, MLA_HEADS, hq).astype(BF16)
    wkv3 = w_ukv.reshape(KV_LORA, MLA_HEADS, QK_NOPE + V_HEAD)
    wuk = _head_cols(wkv3[:, :, :QK_NOPE].reshape(KV_LORA, -1), MLA_HEADS, QK_NOPE).astype(BF16)
    wuv = _head_cols(wkv3[:, :, QK_NOPE:].reshape(KV_LORA, -1), MLA_HEADS, V_HEAD).astype(BF16)
    gq = _pad_lanes(jnp.concatenate([qn_norm, qr_norm]))
    qt, k, vt = _mla_prep(proj, posf, _rope_inv(QK_ROPE, QK_NOPE), q_a_norm.reshape(1, -1).astype(F32),
                          kv_a_norm.reshape(1, -1).astype(F32), wuq, wuk, wuv, gq, _pad_lanes(kn_norm),
                          _pad_lanes(kr_norm, offset=QK_NOPE))
    o = _causal_attention(qt, k, vt)
    w_pad = _pad_head_rows(w_out, MLA_HEADS, V_HEAD)
    half = w_pad.shape[0] // 2
    return _out_proj(o, o, 0, 1, w_pad[:half], w_pad[half:], x2, g1)


def _moe_ffn(x2, mods, norm_g, w_router, router_bias, w_gate, w_up, w_down, ws_gate, ws_up, ws_down):
    sh2, sc2, g2 = mods
    w_r = jnp.pad(w_router.astype(F32), ((0, 0), (0, LANE - N_EXPERTS)))
    h, pos_t, w_t, cnt = _moe_route(x2, norm_g, sc2, sh2, w_r, router_bias.reshape(N_EXPERTS, 1).astype(F32))
    counts = cnt[:, :, 0].astype(jnp.int32).reshape(-1)
    return _moe_experts(counts, x2, h, pos_t, w_t, w_gate.astype(BF16), w_up.astype(BF16),
                        w_down.astype(BF16), ws_gate.astype(BF16), ws_up.astype(BF16), ws_down.astype(BF16), g2)


def kernel(x, c, positions, norm_attn, norm_ffn, w_ada, b_ada, hy_w_in, fox_f_bias, nsa_q_norm, nsa_k_norm, nsa_cmp_pe, nsa_w_cmp, fox_q_norm, fox_k_norm, hy_w_out, mla_w_in, mla_q_a_norm, mla_kv_a_norm, mla_w_uq, mla_w_ukv, mla_qn_norm, mla_kn_norm, mla_qr_norm, mla_kr_norm, mla_w_out, moe_w_router, moe_router_bias, moe_w_gate, moe_w_up, moe_w_down, moe_ws_gate, moe_ws_up, moe_ws_down):
    b, s, d = x.shape
    assert b == 1 and s % KV_TILE == 0 and s >= WINDOW + Q_TILE_NSA
    depth = w_ada.shape[0]
    x2 = x.reshape(s, d).astype(F32)
    posf = positions.reshape(s, 1).astype(F32)
    mod = _ada_mod(c.astype(F32), w_ada.astype(F32), b_ada.astype(F32))

    for layer in range(depth):
        m = [mod[layer, :, k * d:(k + 1) * d] for k in range(6)]
        i = layer // 2
        g_attn = norm_attn[layer].reshape(1, d).astype(F32)
        if layer % 2 == 0:
            x2 = _hybrid_mixer(x2, posf, m[0:3], g_attn, hy_w_in[i], fox_f_bias[i], nsa_q_norm[i],
                               nsa_k_norm[i], nsa_cmp_pe[i], nsa_w_cmp[i], fox_q_norm[i], fox_k_norm[i],
                               hy_w_out[i])
        else:
            x2 = _mla_mixer(x2, posf, m[0:3], g_attn, mla_w_in[i], mla_q_a_norm[i], mla_kv_a_norm[i],
                            mla_w_uq[i], mla_w_ukv[i], mla_qn_norm[i], mla_kn_norm[i], mla_qr_norm[i],
                            mla_kr_norm[i], mla_w_out[i])
        x2 = _moe_ffn(x2, m[3:6], norm_ffn[layer].reshape(1, d).astype(F32), moe_w_router[layer],
                      moe_router_bias[layer], moe_w_gate[layer], moe_w_up[layer], moe_w_down[layer],
                      moe_ws_gate[layer], moe_ws_up[layer], moe_ws_down[layer])
    return x2.reshape(b, s, d)
```

```python
import functools

import numpy as np
import jax
import jax.numpy as jnp
from jax import lax
from jax.experimental import pallas as pl
from jax.experimental.pallas import tpu as pltpu

F32 = jnp.float32
BF16 = jnp.bfloat16
HIGHEST = lax.Precision.HIGHEST

LANE = 128
VMEM_LIMIT_BYTES = 56 * 1024 * 1024

HEAD_DIM = 64
NSA_HEADS = 8
NSA_GROUPS = 2
NSA_HPG = NSA_HEADS // NSA_GROUPS
CMP_LEN = 32
CMP_STRIDE = 16
SLC_BLOCK = 64
SLC_TOPK = 16
WINDOW = 512
FOX_HEADS = 8
MLA_HEADS = 16
Q_LORA = 384
KV_LORA = 256
QK_NOPE = 64
QK_ROPE = 32
V_HEAD = 64
N_EXPERTS = 64
TOP_K = 8
N_GROUPS = 8
TOPK_GROUPS = 4
EXPERT_FF = 256
ROUTED_SCALE = 2.5
ROPE_THETA = 10000.0
EPS = 1e-6
NEG = -1e30
BIG = 1e6

ROW_TILE = 512
PREP_TILE = 512
GATE_ROWS = 16
Q_TILE_NSA = 256
KV_TILE = 512
Q_TILE_FOX = 1024
Q_TILE_MLA = 2048
Q_STRIP = 256
SUPER_BLOCKS = 32
SUM_ROW = 64
REF_ROW = 104
REF_SLAB = 96
EXP_GUARD = 100.0
SCORE_LOOKAHEAD = 3
SKIP_MARGIN = 160.0
LOG2E = 1.4426950408889634
MOE_CHUNK = 128
MOE_EXPERTS_PER_STEP = 8

HY_Q0 = 0
HY_KV0 = HY_Q0 + NSA_HEADS
HY_F0 = HY_KV0 + 6 * NSA_GROUPS
HY_G0 = HY_F0 + 3 * FOX_HEADS
HY_FF = HY_G0 + NSA_GROUPS
HY_BLOCKS = HY_FF + 2
HY_COL_TILE = 12 * LANE
assert (HY_BLOCKS * LANE) % HY_COL_TILE == 0


def _cparams(*sem):
    return pltpu.CompilerParams(dimension_semantics=sem, vmem_limit_bytes=VMEM_LIMIT_BYTES)


def _lane_iota(shape):
    return lax.broadcasted_iota(jnp.int32, shape, len(shape) - 1)


def _row_iota(shape):
    return lax.broadcasted_iota(jnp.int32, shape, len(shape) - 2)


def _dot_nt(a, b):
    return lax.dot_general(a, b, (((1,), (1,)), ((), ())), preferred_element_type=F32)


def _dot_tn(a, b):
    return lax.dot_general(a, b, (((0,), (0,)), ((), ())), preferred_element_type=F32)


def _ada_kernel(c_ref, w_ref, b_ref, o_ref):
    c = c_ref[...]
    cond = c * jax.nn.sigmoid(c)
    o_ref[0] = jnp.dot(cond, w_ref[0], precision=HIGHEST, preferred_element_type=F32) + b_ref[0]


def _ada_mod(c, w_ada, b_ada):
    depth, d, n = w_ada.shape
    tn = 768
    c8 = jnp.broadcast_to(c.reshape(1, d), (8, d))
    out = pl.pallas_call(
        _ada_kernel,
        grid=(depth, n // tn),
        in_specs=[pl.BlockSpec((8, d), lambda l, j: (0, 0)),
                  pl.BlockSpec((1, d, tn), lambda l, j: (l, 0, j)),
                  pl.BlockSpec((1, 1, tn), lambda l, j: (l, 0, j))],
        out_specs=pl.BlockSpec((1, 8, tn), lambda l, j: (l, 0, j)),
        out_shape=jax.ShapeDtypeStruct((depth, 8, n), F32),
        compiler_params=_cparams("parallel", "parallel"),
        name="ada_mod",
    )(c8, w_ada, b_ada.reshape(depth, 1, n))
    return out[:, 0:1, :]


def _norm_mod(x, g, sc, sh):
    ms = jnp.mean(x * x, axis=-1, keepdims=True)
    return (x * lax.rsqrt(ms + EPS) * g) * (1.0 + sc) + sh


def _nmm_kernel(x_ref, g_ref, sc_ref, sh_ref, w_ref, o_ref, h_scr):
    @pl.when(pl.program_id(1) == 0)
    def _():
        h_scr[...] = _norm_mod(x_ref[...], g_ref[...], sc_ref[...], sh_ref[...]).astype(BF16)

    o_ref[...] = jnp.dot(h_scr[...], w_ref[...], preferred_element_type=F32)


def _norm_mod_matmul(x2, g, sc, sh, w, tn):
    s, d = x2.shape
    n = w.shape[1]
    vec = pl.BlockSpec((1, d), lambda i, j: (0, 0))
    return pl.pallas_call(
        _nmm_kernel,
        grid=(s // ROW_TILE, n // tn),
        in_specs=[pl.BlockSpec((ROW_TILE, d), lambda i, j: (i, 0)), vec, vec, vec,
                  pl.BlockSpec((d, tn), lambda i, j: (0, j))],
        out_specs=pl.BlockSpec((ROW_TILE, tn), lambda i, j: (i, j)),
        out_shape=jax.ShapeDtypeStruct((s, n), F32),
        scratch_shapes=[pltpu.VMEM((ROW_TILE, d), BF16)],
        compiler_params=_cparams("parallel", "arbitrary"),
        name="norm_mod_matmul",
    )(x2, g, sc, sh, w)


def _head_rms(x, gain, n_real):
    ss = jnp.sum(x * x, axis=-1, keepdims=True)
    return x * lax.rsqrt(ss * (1.0 / n_real) + EPS) * gain


def _rope64(x, cos, sin):
    lane = _lane_iota(x.shape)
    rot = jnp.where(lane < 32, -pltpu.roll(x, LANE - 32, 1), pltpu.roll(x, 32, 1))
    return x * cos + rot * sin


def _split3(c):
    hi = c.astype(BF16).astype(F32)
    r1 = c - hi
    mid = r1.astype(BF16).astype(F32)
    lo = (r1 - mid).astype(BF16).astype(F32)
    return hi, mid, lo


def _hy_prep_kernel(p_ref, pos_ref, inv_ref, gq_ref, gk_ref, gfq_ref, gfk_ref, fb_ref,
                    qnt_ref, kct_ref, vct_ref, ks_ref, vst_ref, kw_ref, vwt_ref, gate_ref,
                    fqt_ref, fk_ref, fvt_ref, cedge_ref, carry_ref):
    i = pl.program_id(0)
    tm = PREP_TILE
    shp = (tm, LANE)
    lane = _lane_iota(shp)

    def blk(b):
        return p_ref[:, b * LANE:(b + 1) * LANE]

    ang = pos_ref[...] * inv_ref[...]
    real = lane < HEAD_DIM
    cos = jnp.where(real, jnp.cos(ang), 1.0)
    sin = jnp.where(real, jnp.sin(ang), 0.0)
    gq, gk, gfq, gfk = gq_ref[...], gk_ref[...], gfq_ref[...], gfk_ref[...]
    scale = HEAD_DIM ** -0.5 * LOG2E
    ones_row = lane == SUM_ROW
    ref_ones = jnp.where((lane >= REF_ROW) & (lane < REF_ROW + 3), 1.0, 0.0)

    for h in range(NSA_HEADS):
        q = _rope64(_head_rms(blk(HY_Q0 + h), gq, HEAD_DIM), cos, sin) * scale
        qnt_ref[h * LANE:(h + 1) * LANE, :] = q.T.astype(BF16)

    row = _row_iota(shp) + i * tm
    onehot = jnp.where(lane - HEAD_DIM == ((row // SLC_BLOCK) % SUPER_BLOCKS), 1.0, 0.0)
    for g in range(NSA_GROUPS):
        def kv(r):
            return blk(HY_KV0 + r * NSA_GROUPS + g)
        sl = slice(g * LANE, (g + 1) * LANE)
        kct_ref[g] = _rope64(kv(0), cos, sin)[:, :HEAD_DIM].astype(BF16)
        vct_ref[g] = kv(1)[:, :HEAD_DIM].astype(BF16)
        ks = _rope64(_head_rms(kv(2), gk, HEAD_DIM), cos, sin)
        ks_ref[:, sl] = (ks + onehot + ref_ones).astype(BF16)
        vst_ref[g, 0] = jnp.where(ones_row, 1.0, kv(3)).T.astype(BF16)
        kw_ref[:, sl] = _rope64(_head_rms(kv(4), gk, HEAD_DIM), cos, sin).astype(BF16)
        vwt = kv(5).T.astype(BF16)
        for cidx in range(tm // LANE):
            vwt_ref[g, cidx] = vwt[:, cidx * LANE:(cidx + 1) * LANE]
        gate_ref[g] = jax.nn.sigmoid(blk(HY_G0 + g)).T[:GATE_ROWS]

    @pl.when(i == 0)
    def _():
        carry_ref[...] = jnp.zeros_like(carry_ref)

    z = blk(HY_FF) + fb_ref[...]
    logf = jnp.minimum(z, 0.0) - jnp.log1p(jnp.exp(-jnp.abs(z)))
    tri = jnp.where(_row_iota((tm, tm)) >= _lane_iota((tm, tm)), 1.0, 0.0).astype(F32)
    cum = jnp.dot(tri, logf, precision=HIGHEST, preferred_element_type=F32) + carry_ref[...]
    carry_ref[...] = cum[tm - 1:tm, :]
    cedge_ref[0] = jnp.concatenate([cum[0:1] * LOG2E, cum[tm - 1:tm] * LOG2E, jnp.zeros((6, LANE), F32)], axis=0)

    for h in range(FOX_HEADS):
        c = jnp.broadcast_to(cum[:, h:h + 1], shp) * LOG2E
        hi, mid, lo = _split3(c)
        fq = _head_rms(blk(HY_F0 + h), gfq, HEAD_DIM) * scale
        fq = jnp.where(real, fq, jnp.where(lane == 64, hi, jnp.where(lane == 65, mid, jnp.where(
            lane == 66, lo, jnp.where(lane < 70, 1.0, 0.0)))))
        fk = _head_rms(blk(HY_F0 + FOX_HEADS + h), gfk, HEAD_DIM)
        fk = jnp.where(real, fk, jnp.where(lane < 67, 1.0, jnp.where(lane == 67, -hi, jnp.where(
            lane == 68, -mid, jnp.where(lane == 69, -lo, ref_ones)))))
        sl = slice(h * LANE, (h + 1) * LANE)
        fqt_ref[sl, :] = fq.T.astype(BF16)
        fk_ref[:, sl] = fk.astype(BF16)
        fvt_ref[h, 0] = jnp.where(ones_row, 1.0, blk(HY_F0 + 2 * FOX_HEADS + h)).T.astype(BF16)


def _feat_major(heads, s, tm):
    return (pl.BlockSpec((heads * LANE, tm), lambda i: (0, i)),
            jax.ShapeDtypeStruct((heads * LANE, s), BF16))


def _value_tiles(heads, s, tm, tk):
    return (pl.BlockSpec((heads, tm // tk, LANE, tk), lambda i: (0, i, 0, 0)),
            jax.ShapeDtypeStruct((heads, s // tk, LANE, tk), BF16))


def _hy_prep(proj, posf, inv128, gq, gk, gfq, gfk, fbias):
    s = proj.shape[0]
    tm = PREP_TILE
    assert tm == KV_TILE
    vec = pl.BlockSpec((1, LANE), lambda i: (0, 0))

    def rows(nb):
        return (pl.BlockSpec((tm, nb * LANE), lambda i: (i, 0)), jax.ShapeDtypeStruct((s, nb * LANE), BF16))

    tok = (pl.BlockSpec((NSA_GROUPS, tm, HEAD_DIM), lambda i: (0, i, 0)),
           jax.ShapeDtypeStruct((NSA_GROUPS, s, HEAD_DIM), BF16))
    gate = (pl.BlockSpec((NSA_GROUPS, GATE_ROWS, tm), lambda i: (0, 0, i)),
            jax.ShapeDtypeStruct((NSA_GROUPS, GATE_ROWS, s), F32))
    outs = [_feat_major(NSA_HEADS, s, tm), tok, tok, rows(NSA_GROUPS), _value_tiles(NSA_GROUPS, s, tm, KV_TILE),
            rows(NSA_GROUPS), _value_tiles(NSA_GROUPS, s, tm, LANE), gate,
            _feat_major(FOX_HEADS, s, tm), rows(FOX_HEADS), _value_tiles(FOX_HEADS, s, tm, KV_TILE),
            (pl.BlockSpec((1, 8, LANE), lambda i: (i, 0, 0)), jax.ShapeDtypeStruct((s // tm, 8, LANE), F32))]
    return pl.pallas_call(
        _hy_prep_kernel,
        grid=(s // tm,),
        in_specs=[pl.BlockSpec((tm, HY_BLOCKS * LANE), lambda i: (i, 0)), pl.BlockSpec((tm, 1), lambda i: (i, 0)),
                  vec, vec, vec, vec, vec, vec],
        out_specs=[o[0] for o in outs],
        out_shape=[o[1] for o in outs],
        scratch_shapes=[pltpu.VMEM((1, LANE), F32)],
        compiler_params=_cparams("arbitrary"),
        name="hybrid_prep",
    )(proj, posf, inv128, gq, gk, gfq, gfk, fbias)


def _compress_kernel(kc_ref, vc_ref, wk_ref, wv_ref, pek_ref, pev_ref, gk_ref, ko_ref, vo_ref):
    half = CMP_STRIDE * HEAD_DIM

    def comp(ch_ref, w_ref, pe_ref):
        ch = ch_ref[0]
        nc = ch.shape[0]
        a = jnp.dot(ch, w_ref[:half], preferred_element_type=F32)
        b = jnp.dot(ch, w_ref[half:], preferred_element_type=F32)
        nxt = pltpu.roll(b, nc - 1, 0)
        pe = jnp.dot(jnp.broadcast_to(pe_ref[...], (8, 2 * half)).astype(BF16), w_ref[...],
                     preferred_element_type=F32)[0:1]
        return a + nxt + pe

    ko_ref[0] = _head_rms(comp(kc_ref, wk_ref, pek_ref), gk_ref[...], HEAD_DIM).astype(BF16)
    vo_ref[0] = comp(vc_ref, wv_ref, pev_ref).T.astype(BF16)


def _compress(kct, vct, w_cmp, cmp_pe, k_norm):
    g, s, _ = kct.shape
    nc = s // CMP_STRIDE
    wide = CMP_STRIDE * HEAD_DIM
    kch = kct.reshape(g, nc, wide)
    vch = vct.reshape(g, nc, wide)
    w_pad = jnp.pad(w_cmp, ((0, 0), (0, 0), (0, LANE - HEAD_DIM))).astype(BF16)
    ch = pl.BlockSpec((1, nc, wide), lambda i: (i, 0, 0))
    wspec = pl.BlockSpec((2 * wide, LANE), lambda i: (0, 0))
    pespec = pl.BlockSpec((1, 2 * wide), lambda i: (0, 0))
    return pl.pallas_call(
        _compress_kernel,
        grid=(g,),
        in_specs=[ch, ch, wspec, wspec, pespec, pespec, pl.BlockSpec((1, LANE), lambda i: (0, 0))],
        out_specs=[pl.BlockSpec((1, nc, LANE), lambda i: (i, 0, 0)),
                   pl.BlockSpec((1, LANE, nc), lambda i: (i, 0, 0))],
        out_shape=[jax.ShapeDtypeStruct((g, nc, LANE), BF16), jax.ShapeDtypeStruct((g, LANE, nc), BF16)],
        compiler_params=_cparams("parallel"),
        name="nsa_compress",
    )(kch, vch, w_pad[0], w_pad[1], cmp_pe[0].reshape(1, 2 * wide).astype(F32),
      cmp_pe[1].reshape(1, 2 * wide).astype(F32), _pad_lanes(k_norm))


def _masked_softmax_t(s, mask):
    s = jnp.where(mask, s, NEG)
    m = jnp.max(s, axis=0, keepdims=True)
    e = jnp.where(mask, jnp.exp2(s - m), 0.0)
    return e, 1.0 / jnp.maximum(jnp.sum(e, axis=0, keepdims=True), 1e-30)


def _online_steps(steps, ms, acc_ref):
    ms = list(ms)

    def scores(step):
        k_tile, qa, _, c, mask = step
        s = jnp.dot(k_tile, qa, preferred_element_type=F32)
        if mask is not None:
            s = jnp.where(mask, s, NEG)
        return s, jnp.max(s, axis=0, keepdims=True)

    nxt = scores(steps[0])
    for idx, (_, _, vt, c, _) in enumerate(steps):
        sl = slice(c * Q_STRIP, (c + 1) * Q_STRIP)
        s, s_max = nxt
        if idx + 1 < len(steps):
            nxt = scores(steps[idx + 1])
        m_new = jnp.maximum(ms[c], s_max)
        a = jnp.exp2(ms[c] - m_new)
        p = jnp.exp2((s - m_new).astype(BF16))
        ms[c] = m_new
        acc_ref[:, sl] = a * acc_ref[:, sl] + jnp.dot(vt, p, preferred_element_type=F32)
    return tuple(ms)


def _m_init(n_strips):
    return tuple(jnp.full((1, Q_STRIP), NEG, F32) for _ in range(n_strips))


def _with_ref_rows(qa, m):
    hi, mid, lo = _split3(-m)
    r = _row_iota((LANE - REF_SLAB, Q_STRIP)) + REF_SLAB
    slab = jnp.where(r == REF_ROW, hi, jnp.where(r == REF_ROW + 1, mid, jnp.where(r == REF_ROW + 2, lo, 0.0)))
    return jnp.concatenate([qa[:REF_SLAB], slab.astype(BF16)], axis=0)


def _first_tile_max(k_tile, qa_strips, masks):
    return tuple(jnp.max(jnp.where(mask, jnp.dot(k_tile, qa, preferred_element_type=F32), NEG), axis=0, keepdims=True)
                 for qa, mask in zip(qa_strips, masks))


def _fast_steps(steps, state, acc_ref):
    state = list(state)
    for k, step in enumerate(steps):
        assert all(prev[3] != step[3] for prev in steps[max(k - SCORE_LOOKAHEAD + 1, 0):k])

    def scores(step):
        k_tile, qa, _, c, mask = step
        s = jnp.dot(k_tile, _with_ref_rows(qa, state[c][0]), preferred_element_type=F32)
        if mask is not None:
            s = jnp.where(mask, s, NEG)
        return s

    ahead = [scores(st) for st in steps[:SCORE_LOOKAHEAD]]
    for idx, (_, _, vt, c, _) in enumerate(steps):
        sl = slice(c * Q_STRIP, (c + 1) * Q_STRIP)
        s = ahead.pop(0)
        m, worst = state[c]
        cm = jnp.max(s, axis=0, keepdims=True)
        inc = jnp.maximum(cm, 0.0)
        state[c] = (m + inc, jnp.maximum(worst, cm))
        if idx + SCORE_LOOKAHEAD < len(steps):
            ahead.append(scores(steps[idx + SCORE_LOOKAHEAD]))
        p = jnp.exp2(s).astype(BF16)
        acc_ref[:, sl] = jnp.exp2(-inc) * (acc_ref[:, sl] + jnp.dot(vt, p, preferred_element_type=F32))
    return tuple(state)


def _flat(state):
    return tuple(x for pair in state for x in pair)


def _nest(flat):
    return tuple((flat[2 * c], flat[2 * c + 1]) for c in range(len(flat) // 2))


def _nsa_kernel(qt_ref, kc_ref, vct_ref, ks_ref, vst_ref, kw_ref, vwt_ref, gate_ref, ovt_ref, o_ref,
                qaug_ref, acc_ref, *, n_sel):
    i = pl.program_id(1)
    tq = Q_TILE_NSA
    cols = NSA_HPG * tq
    qs = i * tq
    nc = kc_ref.shape[1]
    nslc = ovt_ref.shape[0]
    n_super = nslc // SUPER_BLOCKS

    qt = jnp.concatenate([qt_ref[h * LANE:(h + 1) * LANE, :] for h in range(NSA_HPG)], axis=1)
    tq_row = qs + (_lane_iota((1, cols)) % tq)

    s = jnp.dot(kc_ref[0], qt, preferred_element_type=F32)
    cmp_end = _row_iota((nc, 1)) * CMP_STRIDE + (CMP_LEN - 1)
    e, inv_l = _masked_softmax_t(s, cmp_end <= tq_row)
    o_cmp = jnp.dot(vct_ref[0], e.astype(BF16), preferred_element_type=F32) * inv_l

    psum = e[:, 0:tq] * inv_l[:, 0:tq]
    for h in range(1, NSA_HPG):
        psum = psum + e[:, h * tq:(h + 1) * tq] * inv_l[:, h * tq:(h + 1) * tq]
    p_hi = psum.astype(BF16)
    p_lo = (psum - p_hi.astype(F32)).astype(BF16)
    ovt = ovt_ref[...]
    imp = (jnp.dot(ovt, p_hi, preferred_element_type=F32)
           + jnp.dot(ovt, p_lo, preferred_element_type=F32))

    jj = _row_iota((nslc, tq))
    tq_blk = qs + _lane_iota((nslc, tq))
    cur = tq_blk // SLC_BLOCK
    forced = (jj == 0) | (jj == cur) | (jj == cur - 1)
    causal_blk = jj * SLC_BLOCK <= tq_blk
    val = jnp.where(forced, imp + BIG, imp)
    val = jnp.where(causal_blk, val, NEG)

    jjf = jj.astype(F32)

    def pick(_, carry):
        val, sel = carry
        mx = jnp.max(val, axis=0, keepdims=True)
        idx = jnp.min(jnp.where(val == mx, jjf, float(nslc)), axis=0, keepdims=True)
        hit = jjf == idx
        return jnp.where(hit, -jnp.inf, val), jnp.where(hit, 1.0, sel)

    _, sel = lax.fori_loop(0, n_sel, pick, (val, jnp.zeros((nslc, tq), F32)))
    bias_t = jnp.where((sel > 0.0) & causal_blk, 0.0, NEG)

    q_rows = qt[:HEAD_DIM].astype(F32)
    spare = jnp.zeros((LANE - HEAD_DIM - SUPER_BLOCKS, cols), F32)
    for st in range(n_super):
        b = bias_t[st * SUPER_BLOCKS:(st + 1) * SUPER_BLOCKS]
        b = jnp.concatenate([b] * NSA_HPG, axis=1)
        qaug_ref[st] = jnp.concatenate([q_rows, b, spare], axis=0).astype(BF16)

    tk = KV_TILE
    per_super = SUPER_BLOCKS * SLC_BLOCK // tk
    j_last = (qs + tq - 1) // tk
    n_strips = cols // Q_STRIP
    strips = [slice(c * Q_STRIP, (c + 1) * Q_STRIP) for c in range(n_strips)]

    def causal_masks(j):
        kpos = j * tk + _row_iota((tk, 1))
        return [kpos <= tq_row[:, sl] for sl in strips]

    def slc_steps(j, masks=None):
        k0 = pl.multiple_of(j * tk, tk)
        k_tile, vt, st = ks_ref[pl.ds(k0, tk), :], vst_ref[0, j], j // per_super
        return [(k_tile, qaug_ref[st, :, strips[c]], vt, c, None if masks is None else masks[c])
                for c in range(n_strips)]

    group = 4

    def grouped(jj, flat):
        steps = [st for t in range(group) for st in slc_steps(group * jj + t)]
        return _flat(_fast_steps(steps, _nest(flat), acc_ref))

    def single(j, flat):
        return _flat(_fast_steps(slc_steps(j), _nest(flat), acc_ref))

    acc_ref[...] = jnp.zeros(acc_ref.shape, F32)
    head_rows = 16
    head_masks = [_row_iota((head_rows, 1)) <= tq_row[:, sl] for sl in strips]
    m0 = _first_tile_max(ks_ref[0:head_rows, :], [qaug_ref[0, :, sl] for sl in strips], head_masks)
    n_groups = j_last // group
    flat = lax.fori_loop(0, n_groups, grouped, _flat(tuple((m, jnp.zeros_like(m)) for m in m0)))
    flat = lax.fori_loop(group * n_groups, j_last, single, flat)
    state = _fast_steps(slc_steps(j_last, causal_masks(j_last)), _nest(flat), acc_ref)
    worst = jnp.max(jnp.concatenate([w for _, w in state], axis=1))

    @pl.when(worst > EXP_GUARD)
    def _():
        acc_ref[...] = jnp.zeros(acc_ref.shape, F32)
        lax.fori_loop(0, j_last + 1, lambda j, ms: _online_steps(slc_steps(j, causal_masks(j)), ms, acc_ref),
                      _m_init(n_strips))

    acc = acc_ref[...]
    o_slc = acc / jnp.maximum(acc[SUM_ROW:SUM_ROW + 1], 1e-30)

    wlen = WINDOW + tq
    ws = pl.multiple_of(jnp.maximum(qs - WINDOW, 0), tq)
    s = jnp.dot(kw_ref[pl.ds(ws, wlen), :], qt, preferred_element_type=F32)
    dist = tq_row - (ws + _row_iota((wlen, 1)))
    e, inv_l = _masked_softmax_t(s, (dist >= 0) & (dist < WINDOW))
    e = e.astype(BF16)
    wb = ws // LANE
    o_win = jnp.zeros((LANE, cols), F32)
    for c in range(wlen // LANE):
        o_win = o_win + jnp.dot(vwt_ref[0, wb + c], e[c * LANE:(c + 1) * LANE], preferred_element_type=F32)
    o_win = o_win * inv_l

    gate = gate_ref[0]
    for h in range(NSA_HPG):
        sl = slice(h * tq, (h + 1) * tq)
        o = (gate[3 * h:3 * h + 1] * o_cmp[:, sl] + gate[3 * h + 1:3 * h + 2] * o_slc[:, sl]
             + gate[3 * h + 2:3 * h + 3] * o_win[:, sl])
        o_ref[:, h * LANE:(h + 1) * LANE] = o.T.astype(BF16)


def _overlap_t(s, nslc_pad):
    nc = s // CMP_STRIDE
    cmp_start = np.arange(nc) * CMP_STRIDE
    slc_start = np.arange(nslc_pad) * SLC_BLOCK
    ov = np.clip(np.minimum(cmp_start[:, None] + CMP_LEN, slc_start[None, :] + SLC_BLOCK)
                 - np.maximum(cmp_start[:, None], slc_start[None, :]), 0, None) / CMP_STRIDE
    ov[nc - CMP_LEN // CMP_STRIDE + 1:, :] = 0.0
    ov[:, s // SLC_BLOCK:] = 0.0
    return jnp.asarray(ov.T, BF16)


def _nsa_attention(qnt, kc, vct, ks, vst, kw, vwt, gates):
    s = qnt.shape[1]
    nc = s // CMP_STRIDE
    n_slc = s // SLC_BLOCK
    nslc_pad = -(-n_slc // LANE) * LANE
    tq = Q_TILE_NSA
    cols = NSA_HPG * tq
    once = pl.Buffered(1)
    res = pl.BlockSpec((s, LANE), lambda g, i: (0, g), pipeline_mode=once)
    return pl.pallas_call(
        functools.partial(_nsa_kernel, n_sel=min(SLC_TOPK, n_slc)),
        grid=(NSA_GROUPS, s // tq),
        in_specs=[pl.BlockSpec((NSA_HPG * LANE, tq), lambda g, i: (g, i)),
                  pl.BlockSpec((1, nc, LANE), lambda g, i: (g, 0, 0), pipeline_mode=once),
                  pl.BlockSpec((1, LANE, nc), lambda g, i: (g, 0, 0), pipeline_mode=once),
                  res, pl.BlockSpec((1, s // KV_TILE, LANE, KV_TILE), lambda g, i: (g, 0, 0, 0),
                                    pipeline_mode=once),
                  res, pl.BlockSpec((1, s // LANE, LANE, LANE), lambda g, i: (g, 0, 0, 0), pipeline_mode=once),
                  pl.BlockSpec((1, GATE_ROWS, tq), lambda g, i: (g, 0, i)),
                  pl.BlockSpec((nslc_pad, nc), lambda g, i: (0, 0), pipeline_mode=once)],
        out_specs=pl.BlockSpec((tq, NSA_HPG * LANE), lambda g, i: (i, g)),
        out_shape=jax.ShapeDtypeStruct((s, NSA_HEADS * LANE), BF16),
        scratch_shapes=[pltpu.VMEM((nslc_pad // SUPER_BLOCKS, LANE, cols), BF16),
                        pltpu.VMEM((LANE, cols), F32)],
        compiler_params=_cparams("parallel", "arbitrary"),
        name="nsa_attention",
    )(qnt, kc, vct, ks, vst, kw, vwt, gates, _overlap_t(s, nslc_pad))


def _flash_kernel(cfirst_ref, clast_ref, slack_ref, qt_ref, k_ref, vt_ref, o_ref, acc_ref, *, tq):
    h = pl.program_id(0)
    i = pl.program_id(1)
    tk = KV_TILE
    n_tiles = k_ref.shape[0] // tk
    acc_ref[...] = jnp.zeros(acc_ref.shape, F32)

    n_strips = tq // Q_STRIP
    per_q = tq // tk
    qas = [qt_ref[:, c * Q_STRIP:(c + 1) * Q_STRIP] for c in range(n_strips)]

    def tile_steps(j, d=None):
        k0 = pl.multiple_of(j * tk, tk)
        k_tile, vt = k_ref[pl.ds(k0, tk), :], vt_ref[0, j]
        steps = []
        for c in range(n_strips):
            mask = None
            if d is not None:
                if d * tk > (c + 1) * Q_STRIP - 1:
                    continue
                if (d + 1) * tk - 1 > c * Q_STRIP:
                    shp = (tk, Q_STRIP)
                    mask = _row_iota(shp) + d * tk <= _lane_iota(shp) + c * Q_STRIP
            steps.append((k_tile, qas[c], vt, c, mask))
        return steps

    def any_tile_masks(j):
        shp = (tk, Q_STRIP)
        return [_row_iota(shp) + j * tk <= _lane_iota(shp) + (i * tq + c * Q_STRIP) for c in range(n_strips)]

    def below(t, flat):
        jj = i - 1 - t
        bound = (slack_ref[0] + cfirst_ref[h * n_tiles + i * per_q]
                 - clast_ref[h * n_tiles + jj * per_q + per_q - 1])

        def run(flat):
            steps = [st for u in range(per_q) for st in tile_steps(jj * per_q + (per_q - 1 - u))]
            return _flat(_fast_steps(steps, _nest(flat), acc_ref))

        return lax.cond(bound >= -SKIP_MARGIN, run, lambda flat: flat, flat)

    diag0 = i * per_q
    own = [(c * Q_STRIP) // tk for c in range(n_strips)]
    m0 = tuple(_first_tile_max(k_ref[pl.ds(pl.multiple_of((diag0 + own[c]) * tk, tk), tk), :], [qas[c]],
                               [any_tile_masks(diag0 + own[c])[c]])[0] for c in range(n_strips))
    steps = [st for d in reversed(range(per_q)) for st in tile_steps(diag0 + d, d)]
    state = _fast_steps(steps, tuple((m, jnp.zeros_like(m)) for m in m0), acc_ref)
    state = _nest(lax.fori_loop(0, i, below, _flat(state)))
    worst = jnp.max(jnp.concatenate([w for _, w in state], axis=1))

    @pl.when(worst > EXP_GUARD)
    def _():
        acc_ref[...] = jnp.zeros(acc_ref.shape, F32)

        def exact(j, ms):
            k0 = pl.multiple_of(j * tk, tk)
            k_tile, vt, masks = k_ref[pl.ds(k0, tk), :], vt_ref[0, j], any_tile_masks(j)
            return _online_steps([(k_tile, qas[c], vt, c, masks[c]) for c in range(n_strips)], ms, acc_ref)

        lax.fori_loop(0, (i + 1) * per_q, exact, _m_init(n_strips))

    acc = acc_ref[...]
    o = acc / acc[SUM_ROW:SUM_ROW + 1]
    for c0 in range(0, tq, LANE):
        o_ref[c0:c0 + LANE, :] = o[:, c0:c0 + LANE].T.astype(BF16)


def _causal_attention(qt, k, vt, tq, bias_edges=None, slack=None):
    s, width = k.shape
    heads = width // LANE
    tq = min(tq, s)
    assert tq % KV_TILE == 0 and s % tq == 0
    n_tiles = s // KV_TILE
    if bias_edges is None:
        first = last = jnp.zeros((heads * n_tiles,), F32)
        slack = jnp.full((1,), -NEG, F32)
    else:
        first, last = (e.reshape(heads * n_tiles).astype(F32) for e in bias_edges)
    grid_spec = pltpu.PrefetchScalarGridSpec(
        num_scalar_prefetch=3,
        grid=(heads, s // tq),
        in_specs=[pl.BlockSpec((LANE, tq), lambda h, i, *_: (h, i)),
                  pl.BlockSpec((s, LANE), lambda h, i, *_: (0, h)),
                  pl.BlockSpec((1, n_tiles, LANE, KV_TILE), lambda h, i, *_: (h, 0, 0, 0))],
        out_specs=pl.BlockSpec((tq, LANE), lambda h, i, *_: (i, h)),
        scratch_shapes=[pltpu.VMEM((LANE, tq), F32)],
    )
    return pl.pallas_call(
        functools.partial(_flash_kernel, tq=tq),
        grid_spec=grid_spec,
        out_shape=jax.ShapeDtypeStruct((s, width), BF16),
        compiler_params=_cparams("parallel", "arbitrary"),
        name="causal_attention",
    )(first, last, slack.astype(F32), qt, k, vt)


def _out_proj_kernel(oa_ref, ob_ref, wa_ref, wb_ref, x_ref, g_ref, o_ref):
    y = jnp.dot(oa_ref[...], wa_ref[...], preferred_element_type=F32)
    y = y + jnp.dot(ob_ref[...], wb_ref[...], preferred_element_type=F32)
    o_ref[...] = x_ref[...] + g_ref[...] * y


def _out_proj(oa, ob, cola, colb, wa, wb, x2, gate):
    s, d = x2.shape
    ka = wa.shape[0]
    tm = ROW_TILE
    return pl.pallas_call(
        _out_proj_kernel,
        grid=(s // tm,),
        in_specs=[pl.BlockSpec((tm, ka), lambda i: (i, cola)), pl.BlockSpec((tm, ka), lambda i: (i, colb)),
                  pl.BlockSpec((ka, d), lambda i: (0, 0)), pl.BlockSpec((ka, d), lambda i: (0, 0)),
                  pl.BlockSpec((tm, d), lambda i: (i, 0)), pl.BlockSpec((1, d), lambda i: (0, 0))],
        out_specs=pl.BlockSpec((tm, d), lambda i: (i, 0)),
        out_shape=jax.ShapeDtypeStruct((s, d), F32),
        compiler_params=_cparams("parallel"),
        name="out_proj",
    )(oa, ob, wa, wb, x2, gate)


def _mla_prep_kernel(p_ref, pos_ref, inv_ref, gqa_ref, gkva_ref, wuq_ref, wuk_ref, wuv_ref,
                     gq_ref, gk_ref, gkr_ref, qt_ref, k_ref, vt_ref):
    shp = (PREP_TILE, LANE)
    lane = _lane_iota(shp)
    nope = lane < QK_NOPE
    rope = (lane >= QK_NOPE) & (lane < QK_NOPE + QK_ROPE)
    ref_ones = jnp.where((lane >= REF_ROW) & (lane < REF_ROW + 3), 1.0, 0.0)
    ang = pos_ref[...] * inv_ref[...]
    cos = jnp.where(rope, jnp.cos(ang), 1.0)
    sin = jnp.where(rope, jnp.sin(ang), 0.0)

    def rope32(x):
        half = QK_ROPE // 2
        rot = jnp.where(lane < QK_NOPE + half, -pltpu.roll(x, LANE - half, 1), pltpu.roll(x, half, 1))
        return x * cos + rot * sin

    def low_rank_norm(x, g):
        ms = jnp.mean(x * x, axis=-1, keepdims=True)
        return (x * lax.rsqrt(ms + EPS) * g).astype(BF16)

    nq = Q_LORA // LANE
    cq = low_rank_norm(p_ref[:, :Q_LORA], gqa_ref[...])
    ckv = low_rank_norm(p_ref[:, Q_LORA:Q_LORA + KV_LORA], gkva_ref[...])
    kr = p_ref[:, (nq + KV_LORA // LANE) * LANE:(nq + KV_LORA // LANE + 1) * LANE]
    k_rope = rope32(_head_rms(kr, gkr_ref[...], QK_ROPE))

    gq, gk = gq_ref[...], gk_ref[...]
    scale = (QK_NOPE + QK_ROPE) ** -0.5 * LOG2E
    pair = 2 * LANE
    for hp in range(MLA_HEADS // 2):
        cols = slice(hp * pair, (hp + 1) * pair)
        q2 = jnp.dot(cq, wuq_ref[:, cols], preferred_element_type=F32)
        k2 = jnp.dot(ckv, wuk_ref[:, cols], preferred_element_type=F32)
        v2 = jnp.dot(ckv, wuv_ref[:, cols], preferred_element_type=F32)
        for sub in range(2):
            head = 2 * hp + sub
            sl = slice(head * LANE, (head + 1) * LANE)
            half = slice(sub * LANE, (sub + 1) * LANE)
            x = q2[:, half]
            ss_n = jnp.sum(jnp.where(nope, x * x, 0.0), axis=-1, keepdims=True)
            ss_r = jnp.sum(jnp.where(rope, x * x, 0.0), axis=-1, keepdims=True)
            inv_rms = jnp.where(nope, lax.rsqrt(ss_n * (1.0 / QK_NOPE) + EPS),
                                lax.rsqrt(ss_r * (1.0 / QK_ROPE) + EPS))
            qt_ref[sl, :] = (rope32(x * inv_rms * gq) * scale).T.astype(BF16)
            kn = _head_rms(k2[:, half], gk, QK_NOPE)
            k_ref[:, sl] = (kn + k_rope + ref_ones).astype(BF16)
            vt_ref[head, 0] = jnp.where(lane == SUM_ROW, 1.0, v2[:, half]).T.astype(BF16)


def _mla_prep(proj, posf, inv128, gqa, gkva, wuq, wuk, wuv, gq, gk, gkr):
    s, n = proj.shape
    tm = PREP_TILE
    assert tm == KV_TILE

    def full(a):
        return pl.BlockSpec(a.shape, lambda i: (0, 0))

    outs = [_feat_major(MLA_HEADS, s, tm),
            (pl.BlockSpec((tm, MLA_HEADS * LANE), lambda i: (i, 0)),
             jax.ShapeDtypeStruct((s, MLA_HEADS * LANE), BF16)),
            _value_tiles(MLA_HEADS, s, tm, KV_TILE)]
    args = (inv128, gqa, gkva, wuq, wuk, wuv, gq, gk, gkr)
    return pl.pallas_call(
        _mla_prep_kernel,
        grid=(s // tm,),
        in_specs=[pl.BlockSpec((tm, n), lambda i: (i, 0)), pl.BlockSpec((tm, 1), lambda i: (i, 0))]
                 + [full(a) for a in args],
        out_specs=[o[0] for o in outs],
        out_shape=[o[1] for o in outs],
        compiler_params=_cparams("parallel"),
        name="mla_prep",
    )(proj, posf, *args)


def _rank_lt(v, k):
    n = v.shape[0]
    row = _row_iota(v.shape)
    rank = jnp.zeros(v.shape, F32)
    for b in range(n):
        vb = v[b:b + 1, :]
        rank = rank + jnp.where((vb > v) | ((vb == v) & (row > b)), 1.0, 0.0)
    return rank < k


def _moe_route_kernel(x_ref, g_ref, sc_ref, sh_ref, wr_ref, rb_ref, h_ref, pos_ref, wt_ref, cnt_ref):
    tm = ROW_TILE
    h = _norm_mod(x_ref[...], g_ref[...], sc_ref[...], sh_ref[...])
    h_ref[...] = h.astype(BF16)
    logits = jnp.dot(h, wr_ref[...], precision=HIGHEST, preferred_element_type=F32)
    lt = logits.T[:N_EXPERTS]
    scores = jax.nn.sigmoid(lt)
    sel = scores + rb_ref[...]

    per = N_EXPERTS // N_GROUPS
    grp = sel.reshape(N_GROUPS, per, tm)
    sub = lax.broadcasted_iota(jnp.int32, grp.shape, 1)
    m1 = jnp.max(grp, axis=1, keepdims=True)
    first = jnp.min(jnp.where(grp == m1, sub, per), axis=1, keepdims=True)
    m2 = jnp.max(jnp.where(sub == first, -jnp.inf, grp), axis=1, keepdims=True)
    gscore = (m1 + m2).reshape(N_GROUPS, tm)
    gmask = _rank_lt(gscore, TOPK_GROUPS)
    emask = jnp.broadcast_to(gmask.reshape(N_GROUPS, 1, tm), grp.shape).reshape(N_EXPERTS, tm)
    chosen = _rank_lt(jnp.where(emask, sel, NEG), TOP_K)

    w = jnp.where(chosen, scores, 0.0)
    wt_ref[...] = w / jnp.sum(w, axis=0, keepdims=True) * ROUTED_SCALE

    upper = jnp.where(_row_iota((tm, tm)) <= _lane_iota((tm, tm)), 1.0, 0.0).astype(BF16)
    incl = jnp.dot(jnp.where(chosen, 1.0, 0.0).astype(BF16), upper, preferred_element_type=F32)
    pos_ref[...] = jnp.where(chosen, incl - 1.0, -1.0)
    cnt_ref[0] = jnp.broadcast_to(incl[:, tm - 1:tm], (N_EXPERTS, LANE))


def _moe_route(x2, g, sc, sh, w_router_pad, router_bias_col):
    s, d = x2.shape
    tm = ROW_TILE
    vec = pl.BlockSpec((1, d), lambda i: (0, 0))
    et = pl.BlockSpec((N_EXPERTS, tm), lambda i: (0, i))
    return pl.pallas_call(
        _moe_route_kernel,
        grid=(s // tm,),
        in_specs=[pl.BlockSpec((tm, d), lambda i: (i, 0)), vec, vec, vec,
                  pl.BlockSpec((d, LANE), lambda i: (0, 0)),
                  pl.BlockSpec((N_EXPERTS, 1), lambda i: (0, 0))],
        out_specs=[pl.BlockSpec((tm, d), lambda i: (i, 0)), et, et,
                   pl.BlockSpec((1, N_EXPERTS, LANE), lambda i: (i, 0, 0))],
        out_shape=[jax.ShapeDtypeStruct((s, d), BF16), jax.ShapeDtypeStruct((N_EXPERTS, s), F32),
                   jax.ShapeDtypeStruct((N_EXPERTS, s), F32),
                   jax.ShapeDtypeStruct((s // tm, N_EXPERTS, LANE), F32)],
        compiler_params=_cparams("parallel"),
        name="moe_route",
    )(x2, g, sc, sh, w_router_pad, router_bias_col)


def _moe_kernel(cnt_ref, x_ref, h_ref, pos_ref, wt_ref, wg_ref, wu_ref, wd_ref, sg_ref, su_ref, sd_ref,
                g2_ref, o_ref, acc_ref):
    i = pl.program_id(0)
    e = pl.program_id(1)
    tm = ROW_TILE
    r = MOE_CHUNK

    @pl.when(e == 0)
    def _():
        h = h_ref[...]
        a = jnp.dot(h, sg_ref[...], preferred_element_type=F32)
        a = a * jax.nn.sigmoid(a) * jnp.dot(h, su_ref[...], preferred_element_type=F32)
        acc_ref[...] = jnp.dot(a.astype(BF16), sd_ref[...], preferred_element_type=F32)

    first = e * MOE_EXPERTS_PER_STEP
    n = cnt_ref[i * N_EXPERTS + first]
    for k in range(1, MOE_EXPERTS_PER_STEP):
        n = jnp.maximum(n, cnt_ref[i * N_EXPERTS + first + k])
    prows = [pos_ref[pl.ds(first + k, 1), :] for k in range(MOE_EXPERTS_PER_STEP)]
    wrows = [wt_ref[pl.ds(first + k, 1), :] for k in range(MOE_EXPERTS_PER_STEP)]

    def chunk(c, _):
        slot = (_row_iota((r, tm)) + c * r).astype(F32)
        hits = [prow == slot for prow in prows]
        onehot = jnp.concatenate([jnp.where(hit, 1.0, 0.0).astype(BF16) for hit in hits], axis=0)
        xg = jnp.dot(onehot, h_ref[...], preferred_element_type=F32).astype(BF16)
        ys = []
        for k in range(MOE_EXPERTS_PER_STEP):
            xk = xg[k * r:(k + 1) * r]
            a = jnp.dot(xk, wg_ref[k], preferred_element_type=F32)
            a = a * jax.nn.sigmoid(a) * jnp.dot(xk, wu_ref[k], preferred_element_type=F32)
            y = jnp.dot(a.astype(BF16), wd_ref[k], preferred_element_type=F32)
            wr = jnp.sum(jnp.where(hits[k], wrows[k], 0.0), axis=-1, keepdims=True)
            ys.append((y * wr).astype(BF16))
        acc_ref[...] += _dot_tn(onehot, jnp.concatenate(ys, axis=0))
        return 0

    lax.fori_loop(0, (n + r - 1) // r, chunk, 0)

    @pl.when(e == N_EXPERTS // MOE_EXPERTS_PER_STEP - 1)
    def _():
        o_ref[...] = x_ref[...] + g2_ref[...] * acc_ref[...]


def _moe_experts(counts, x2, h, pos_t, w_t, wg, wu, wd, sg, su, sd, g2):
    s, d = x2.shape
    tm = ROW_TILE
    ff = wg.shape[2]
    tile = pl.BlockSpec((tm, d), lambda i, e, c: (i, 0))
    et = pl.BlockSpec((N_EXPERTS, tm), lambda i, e, c: (0, i))

    def const(a):
        return pl.BlockSpec(a.shape, lambda i, e, c: (0,) * a.ndim)

    per = MOE_EXPERTS_PER_STEP
    grid_spec = pltpu.PrefetchScalarGridSpec(
        num_scalar_prefetch=1,
        grid=(s // tm, N_EXPERTS // per),
        in_specs=[tile, tile, et, et,
                  pl.BlockSpec((per, d, ff), lambda i, e, c: (e, 0, 0)),
                  pl.BlockSpec((per, d, ff), lambda i, e, c: (e, 0, 0)),
                  pl.BlockSpec((per, ff, d), lambda i, e, c: (e, 0, 0)),
                  const(sg), const(su), const(sd), const(g2)],
        out_specs=tile,
        scratch_shapes=[pltpu.VMEM((tm, d), F32)],
    )
    return pl.pallas_call(
        _moe_kernel,
        grid_spec=grid_spec,
        out_shape=jax.ShapeDtypeStruct((s, d), F32),
        compiler_params=_cparams("parallel", "arbitrary"),
        name="moe_experts",
    )(counts, x2, h, pos_t, w_t, wg, wu, wd, sg, su, sd, g2)


def _pad_lanes(v, width=LANE, offset=0):
    out = jnp.zeros((1, width), F32)
    return out.at[0, offset:offset + v.shape[0]].set(v.astype(F32))


def _head_cols(w, n_heads, dim):
    d = w.shape[0]
    w3 = w.reshape(d, n_heads, dim)
    return jnp.pad(w3, ((0, 0), (0, 0), (0, LANE - dim))).reshape(d, n_heads * LANE)


def _hybrid_w_in(w_in):
    d = w_in.shape[0]
    nq = NSA_HEADS * HEAD_DIM
    nkv = 6 * NSA_GROUPS * HEAD_DIM
    ng = 3 * NSA_HEADS
    nf = 3 * FOX_HEADS * HEAD_DIM
    c0, c1, c2, c3 = nq, nq + nkv, nq + nkv + ng, nq + nkv + ng + nf
    gates = w_in[:, c1:c2].reshape(d, NSA_GROUPS, 3 * NSA_HPG)
    gates = jnp.pad(gates, ((0, 0), (0, 0), (0, LANE - 3 * NSA_HPG))).reshape(d, NSA_GROUPS * LANE)
    ff = jnp.pad(w_in[:, c3:], ((0, 0), (0, 2 * LANE - FOX_HEADS)))
    return jnp.concatenate([
        _head_cols(w_in[:, :c0], NSA_HEADS, HEAD_DIM),
        _head_cols(w_in[:, c0:c1], 6 * NSA_GROUPS, HEAD_DIM),
        _head_cols(w_in[:, c2:c3], 3 * FOX_HEADS, HEAD_DIM),
        gates, ff], axis=1).astype(BF16)


def _pad_head_rows(w, n_heads, dim):
    d = w.shape[1]
    w3 = w.reshape(n_heads, dim, d)
    return jnp.pad(w3, ((0, 0), (0, LANE - dim), (0, 0))).reshape(n_heads * LANE, d).astype(BF16)


def _rope_inv(dim, offset):
    inv = ROPE_THETA ** (-jnp.arange(0, dim, 2, dtype=F32) / dim)
    return _pad_lanes(jnp.concatenate([inv, inv]), offset=offset)


def _hybrid_mixer(x2, posf, mods, norm_g, w_in, fox_f_bias, nsa_q_norm, nsa_k_norm, nsa_cmp_pe, nsa_w_cmp,
                  fox_q_norm, fox_k_norm, w_out):
    sh1, sc1, g1 = mods
    proj = _norm_mod_matmul(x2, norm_g, sc1, sh1, _hybrid_w_in(w_in), tn=HY_COL_TILE)
    (qnt, kct, vct, ks, vst, kw, vwt, gates, fqt, fk, fvt, cedge) = _hy_prep(
        proj, posf, _rope_inv(HEAD_DIM, 0), _pad_lanes(nsa_q_norm), _pad_lanes(nsa_k_norm),
        _pad_lanes(fox_q_norm), _pad_lanes(fox_k_norm), _pad_lanes(fox_f_bias))
    kc, vc_t = _compress(kct, vct, nsa_w_cmp, nsa_cmp_pe, nsa_k_norm)
    o_a = _nsa_attention(qnt, kc, vc_t, ks, vst, kw, vwt, gates)
    slack = (2.0 * HEAD_DIM ** 0.5 * LOG2E) * jnp.max(jnp.abs(fox_q_norm)) * jnp.max(jnp.abs(fox_k_norm))
    edges = (cedge[:, 0, :FOX_HEADS].T, cedge[:, 1, :FOX_HEADS].T)
    o_b = _causal_attention(fqt, fk, fvt, Q_TILE_FOX, edges, slack.reshape(1))
    half = NSA_HEADS * HEAD_DIM
    wa = _pad_head_rows(w_out[:half], NSA_HEADS, HEAD_DIM)
    wb = _pad_head_rows(w_out[half:], FOX_HEADS, HEAD_DIM)
    return _out_proj(o_a, o_b, 0, 0, wa, wb, x2, g1)


def _mla_mixer(x2, posf, mods, norm_g, w_in, q_a_norm, kv_a_norm, w_uq, w_ukv, qn_norm, kn_norm, qr_norm,
               kr_norm, w_out):
    sh1, sc1, g1 = mods
    d = x2.shape[1]
    w_kr = jnp.zeros((d, LANE), F32).at[:, QK_NOPE:QK_NOPE + QK_ROPE].set(w_in[:, Q_LORA + KV_LORA:])
    w_in_p = jnp.concatenate([w_in[:, :Q_LORA + KV_LORA], w_kr], axis=1).astype(BF16)
    proj = _norm_mod_matmul(x2, norm_g, sc1, sh1, w_in_p, tn=w_in_p.shape[1])
    hq = QK_NOPE + QK_ROPE
    wuq = _head_cols(w_uq, MLA_HEADS, hq).astype(BF16)
    wkv3 = w_ukv.reshape(KV_LORA, MLA_HEADS, QK_NOPE + V_HEAD)
    wuk = _head_cols(wkv3[:, :, :QK_NOPE].reshape(KV_LORA, -1), MLA_HEADS, QK_NOPE).astype(BF16)
    wuv = _head_cols(wkv3[:, :, QK_NOPE:].reshape(KV_LORA, -1), MLA_HEADS, V_HEAD).astype(BF16)
    gq = _pad_lanes(jnp.concatenate([qn_norm, qr_norm]))
    qt, k, vt = _mla_prep(proj, posf, _rope_inv(QK_ROPE, QK_NOPE), q_a_norm.reshape(1, -1).astype(F32),
                          kv_a_norm.reshape(1, -1).astype(F32), wuq, wuk, wuv, gq, _pad_lanes(kn_norm),
                          _pad_lanes(kr_norm, offset=QK_NOPE))
    o = _causal_attention(qt, k, vt, Q_TILE_MLA)
    w_pad = _pad_head_rows(w_out, MLA_HEADS, V_HEAD)
    half = w_pad.shape[0] // 2
    return _out_proj(o, o, 0, 1, w_pad[:half], w_pad[half:], x2, g1)


def _moe_ffn(x2, mods, norm_g, w_router, router_bias, w_gate, w_up, w_down, ws_gate, ws_up, ws_down):
    sh2, sc2, g2 = mods
    w_r = jnp.pad(w_router.astype(F32), ((0, 0), (0, LANE - N_EXPERTS)))
    h, pos_t, w_t, cnt = _moe_route(x2, norm_g, sc2, sh2, w_r, router_bias.reshape(N_EXPERTS, 1).astype(F32))
    counts = cnt[:, :, 0].astype(jnp.int32).reshape(-1)
    return _moe_experts(counts, x2, h, pos_t, w_t, w_gate.astype(BF16), w_up.astype(BF16),
                        w_down.astype(BF16), ws_gate.astype(BF16), ws_up.astype(BF16), ws_down.astype(BF16), g2)


def kernel(x, c, positions, norm_attn, norm_ffn, w_ada, b_ada, hy_w_in, fox_f_bias, nsa_q_norm, nsa_k_norm, nsa_cmp_pe, nsa_w_cmp, fox_q_norm, fox_k_norm, hy_w_out, mla_w_in, mla_q_a_norm, mla_kv_a_norm, mla_w_uq, mla_w_ukv, mla_qn_norm, mla_kn_norm, mla_qr_norm, mla_kr_norm, mla_w_out, moe_w_router, moe_router_bias, moe_w_gate, moe_w_up, moe_w_down, moe_ws_gate, moe_ws_up, moe_ws_down):
    b, s, d = x.shape
    assert b == 1 and s % KV_TILE == 0 and s >= WINDOW + Q_TILE_NSA
    depth = w_ada.shape[0]
    x2 = x.reshape(s, d).astype(F32)
    posf = positions.reshape(s, 1).astype(F32)
    mod = _ada_mod(c.astype(F32), w_ada.astype(F32), b_ada.astype(F32))

    for layer in range(depth):
        m = [mod[layer, :, k * d:(k + 1) * d] for k in range(6)]
        i = layer // 2
        g_attn = norm_attn[layer].reshape(1, d).astype(F32)
        if layer % 2 == 0:
            x2 = _hybrid_mixer(x2, posf, m[0:3], g_attn, hy_w_in[i], fox_f_bias[i], nsa_q_norm[i],
                               nsa_k_norm[i], nsa_cmp_pe[i], nsa_w_cmp[i], fox_q_norm[i], fox_k_norm[i],
                               hy_w_out[i])
        else:
            x2 = _mla_mixer(x2, posf, m[0:3], g_attn, mla_w_in[i], mla_q_a_norm[i], mla_kv_a_norm[i],
                            mla_w_uq[i], mla_w_ukv[i], mla_qn_norm[i], mla_kn_norm[i], mla_qr_norm[i],
                            mla_kr_norm[i], mla_w_out[i])
        x2 = _moe_ffn(x2, m[3:6], norm_ffn[layer].reshape(1, d).astype(F32), moe_w_router[layer],
                      moe_router_bias[layer], moe_w_gate[layer], moe_w_up[layer], moe_w_down[layer],
                      moe_ws_gate[layer], moe_ws_up[layer], moe_ws_down[layer])
    return x2.reshape(b, s, d)
```

```python
import functools

import numpy as np
import jax
import jax.numpy as jnp
from jax import lax
from jax.experimental import pallas as pl
from jax.experimental.pallas import tpu as pltpu

F32 = jnp.float32
BF16 = jnp.bfloat16
HIGHEST = lax.Precision.HIGHEST

LANE = 128
VMEM_LIMIT_BYTES = 56 * 1024 * 1024

HEAD_DIM = 64
NSA_HEADS = 8
NSA_GROUPS = 2
NSA_HPG = NSA_HEADS // NSA_GROUPS
CMP_LEN = 32
CMP_STRIDE = 16
SLC_BLOCK = 64
SLC_TOPK = 16
WINDOW = 512
FOX_HEADS = 8
MLA_HEADS = 16
Q_LORA = 384
KV_LORA = 256
QK_NOPE = 64
QK_ROPE = 32
V_HEAD = 64
N_EXPERTS = 64
TOP_K = 8
N_GROUPS = 8
TOPK_GROUPS = 4
EXPERT_FF = 256
ROUTED_SCALE = 2.5
ROPE_THETA = 10000.0
EPS = 1e-6
NEG = -1e30
BIG = 1e6

ROW_TILE = 512
PREP_TILE = 512
GATE_ROWS = 16
Q_TILE_NSA = 256
KV_TILE = 512
Q_TILE_FOX = 1024
Q_TILE_MLA = 2048
Q_STRIP = 256
SUPER_BLOCKS = 32
SUM_ROW = 64
REF_ROW = 104
REF_SLAB = 96
EXP_GUARD = 100.0
SCORE_LOOKAHEAD = 3
SKIP_MARGIN = 160.0
LOG2E = 1.4426950408889634
MOE_CHUNK = 96
MOE_EXPERTS_PER_STEP = 8

HY_Q0 = 0
HY_KV0 = HY_Q0 + NSA_HEADS
HY_F0 = HY_KV0 + 6 * NSA_GROUPS
HY_G0 = HY_F0 + 3 * FOX_HEADS
HY_FF = HY_G0 + NSA_GROUPS
HY_BLOCKS = HY_FF + 2
HY_COL_TILE = 12 * LANE
assert (HY_BLOCKS * LANE) % HY_COL_TILE == 0


def _cparams(*sem):
    return pltpu.CompilerParams(dimension_semantics=sem, vmem_limit_bytes=VMEM_LIMIT_BYTES)


def _lane_iota(shape):
    return lax.broadcasted_iota(jnp.int32, shape, len(shape) - 1)


def _row_iota(shape):
    return lax.broadcasted_iota(jnp.int32, shape, len(shape) - 2)


def _dot_nt(a, b):
    return lax.dot_general(a, b, (((1,), (1,)), ((), ())), preferred_element_type=F32)


def _dot_tn(a, b):
    return lax.dot_general(a, b, (((0,), (0,)), ((), ())), preferred_element_type=F32)


def _ada_kernel(c_ref, w_ref, b_ref, o_ref):
    c = c_ref[...]
    cond = c * jax.nn.sigmoid(c)
    o_ref[0] = jnp.dot(cond, w_ref[0], precision=HIGHEST, preferred_element_type=F32) + b_ref[0]


def _ada_mod(c, w_ada, b_ada):
    depth, d, n = w_ada.shape
    tn = 768
    c8 = jnp.broadcast_to(c.reshape(1, d), (8, d))
    out = pl.pallas_call(
        _ada_kernel,
        grid=(depth, n // tn),
        in_specs=[pl.BlockSpec((8, d), lambda l, j: (0, 0)),
                  pl.BlockSpec((1, d, tn), lambda l, j: (l, 0, j)),
                  pl.BlockSpec((1, 1, tn), lambda l, j: (l, 0, j))],
        out_specs=pl.BlockSpec((1, 8, tn), lambda l, j: (l, 0, j)),
        out_shape=jax.ShapeDtypeStruct((depth, 8, n), F32),
        compiler_params=_cparams("parallel", "parallel"),
        name="ada_mod",
    )(c8, w_ada, b_ada.reshape(depth, 1, n))
    return out[:, 0:1, :]


def _norm_mod(x, g, sc, sh):
    ms = jnp.mean(x * x, axis=-1, keepdims=True)
    return (x * lax.rsqrt(ms + EPS) * g) * (1.0 + sc) + sh


def _nmm_kernel(x_ref, g_ref, sc_ref, sh_ref, w_ref, o_ref, h_scr):
    @pl.when(pl.program_id(1) == 0)
    def _():
        h_scr[...] = _norm_mod(x_ref[...], g_ref[...], sc_ref[...], sh_ref[...]).astype(BF16)

    o_ref[...] = jnp.dot(h_scr[...], w_ref[...], preferred_element_type=F32)


def _norm_mod_matmul(x2, g, sc, sh, w, tn):
    s, d = x2.shape
    n = w.shape[1]
    vec = pl.BlockSpec((1, d), lambda i, j: (0, 0))
    return pl.pallas_call(
        _nmm_kernel,
        grid=(s // ROW_TILE, n // tn),
        in_specs=[pl.BlockSpec((ROW_TILE, d), lambda i, j: (i, 0)), vec, vec, vec,
                  pl.BlockSpec((d, tn), lambda i, j: (0, j))],
        out_specs=pl.BlockSpec((ROW_TILE, tn), lambda i, j: (i, j)),
        out_shape=jax.ShapeDtypeStruct((s, n), F32),
        scratch_shapes=[pltpu.VMEM((ROW_TILE, d), BF16)],
        compiler_params=_cparams("parallel", "arbitrary"),
        name="norm_mod_matmul",
    )(x2, g, sc, sh, w)


def _head_rms(x, gain, n_real):
    ss = jnp.sum(x * x, axis=-1, keepdims=True)
    return x * lax.rsqrt(ss * (1.0 / n_real) + EPS) * gain


def _rope64(x, cos, sin):
    lane = _lane_iota(x.shape)
    rot = jnp.where(lane < 32, -pltpu.roll(x, LANE - 32, 1), pltpu.roll(x, 32, 1))
    return x * cos + rot * sin


def _split3(c):
    hi = c.astype(BF16).astype(F32)
    r1 = c - hi
    mid = r1.astype(BF16).astype(F32)
    lo = (r1 - mid).astype(BF16).astype(F32)
    return hi, mid, lo


def _hy_prep_kernel(p_ref, pos_ref, inv_ref, gq_ref, gk_ref, gfq_ref, gfk_ref, fb_ref,
                    qnt_ref, kct_ref, vct_ref, ks_ref, vst_ref, kw_ref, vwt_ref, gate_ref,
                    fqt_ref, fk_ref, fvt_ref, cedge_ref, carry_ref):
    i = pl.program_id(0)
    tm = PREP_TILE
    shp = (tm, LANE)
    lane = _lane_iota(shp)

    def blk(b):
        return p_ref[:, b * LANE:(b + 1) * LANE]

    ang = pos_ref[...] * inv_ref[...]
    real = lane < HEAD_DIM
    cos = jnp.where(real, jnp.cos(ang), 1.0)
    sin = jnp.where(real, jnp.sin(ang), 0.0)
    gq, gk, gfq, gfk = gq_ref[...], gk_ref[...], gfq_ref[...], gfk_ref[...]
    scale = HEAD_DIM ** -0.5 * LOG2E
    ones_row = lane == SUM_ROW
    ref_ones = jnp.where((lane >= REF_ROW) & (lane < REF_ROW + 3), 1.0, 0.0)

    for h in range(NSA_HEADS):
        q = _rope64(_head_rms(blk(HY_Q0 + h), gq, HEAD_DIM), cos, sin) * scale
        qnt_ref[h * LANE:(h + 1) * LANE, :] = q.T.astype(BF16)

    row = _row_iota(shp) + i * tm
    onehot = jnp.where(lane - HEAD_DIM == ((row // SLC_BLOCK) % SUPER_BLOCKS), 1.0, 0.0)
    for g in range(NSA_GROUPS):
        def kv(r):
            return blk(HY_KV0 + r * NSA_GROUPS + g)
        sl = slice(g * LANE, (g + 1) * LANE)
        kct_ref[g] = _rope64(kv(0), cos, sin)[:, :HEAD_DIM].astype(BF16)
        vct_ref[g] = kv(1)[:, :HEAD_DIM].astype(BF16)
        ks = _rope64(_head_rms(kv(2), gk, HEAD_DIM), cos, sin)
        ks_ref[:, sl] = (ks + onehot + ref_ones).astype(BF16)
        vst_ref[g, 0] = jnp.where(ones_row, 1.0, kv(3)).T.astype(BF16)
        kw_ref[:, sl] = _rope64(_head_rms(kv(4), gk, HEAD_DIM), cos, sin).astype(BF16)
        vwt = kv(5).T.astype(BF16)
        for cidx in range(tm // LANE):
            vwt_ref[g, cidx] = vwt[:, cidx * LANE:(cidx + 1) * LANE]
        gate_ref[g] = jax.nn.sigmoid(blk(HY_G0 + g)).T[:GATE_ROWS]

    @pl.when(i == 0)
    def _():
        carry_ref[...] = jnp.zeros_like(carry_ref)

    z = blk(HY_FF) + fb_ref[...]
    logf = jnp.minimum(z, 0.0) - jnp.log1p(jnp.exp(-jnp.abs(z)))
    tri = jnp.where(_row_iota((tm, tm)) >= _lane_iota((tm, tm)), 1.0, 0.0).astype(F32)
    cum = jnp.dot(tri, logf, precision=HIGHEST, preferred_element_type=F32) + carry_ref[...]
    carry_ref[...] = cum[tm - 1:tm, :]
    cedge_ref[0] = jnp.concatenate([cum[0:1] * LOG2E, cum[tm - 1:tm] * LOG2E, jnp.zeros((6, LANE), F32)], axis=0)

    for h in range(FOX_HEADS):
        c = jnp.broadcast_to(cum[:, h:h + 1], shp) * LOG2E
        hi, mid, lo = _split3(c)
        fq = _head_rms(blk(HY_F0 + h), gfq, HEAD_DIM) * scale
        fq = jnp.where(real, fq, jnp.where(lane == 64, hi, jnp.where(lane == 65, mid, jnp.where(
            lane == 66, lo, jnp.where(lane < 70, 1.0, 0.0)))))
        fk = _head_rms(blk(HY_F0 + FOX_HEADS + h), gfk, HEAD_DIM)
        fk = jnp.where(real, fk, jnp.where(lane < 67, 1.0, jnp.where(lane == 67, -hi, jnp.where(
            lane == 68, -mid, jnp.where(lane == 69, -lo, ref_ones)))))
        sl = slice(h * LANE, (h + 1) * LANE)
        fqt_ref[sl, :] = fq.T.astype(BF16)
        fk_ref[:, sl] = fk.astype(BF16)
        fvt_ref[h, 0] = jnp.where(ones_row, 1.0, blk(HY_F0 + 2 * FOX_HEADS + h)).T.astype(BF16)


def _feat_major(heads, s, tm):
    return (pl.BlockSpec((heads * LANE, tm), lambda i: (0, i)),
            jax.ShapeDtypeStruct((heads * LANE, s), BF16))


def _value_tiles(heads, s, tm, tk):
    return (pl.BlockSpec((heads, tm // tk, LANE, tk), lambda i: (0, i, 0, 0)),
            jax.ShapeDtypeStruct((heads, s // tk, LANE, tk), BF16))


def _hy_prep(proj, posf, inv128, gq, gk, gfq, gfk, fbias):
    s = proj.shape[0]
    tm = PREP_TILE
    assert tm == KV_TILE
    vec = pl.BlockSpec((1, LANE), lambda i: (0, 0))

    def rows(nb):
        return (pl.BlockSpec((tm, nb * LANE), lambda i: (i, 0)), jax.ShapeDtypeStruct((s, nb * LANE), BF16))

    tok = (pl.BlockSpec((NSA_GROUPS, tm, HEAD_DIM), lambda i: (0, i, 0)),
           jax.ShapeDtypeStruct((NSA_GROUPS, s, HEAD_DIM), BF16))
    gate = (pl.BlockSpec((NSA_GROUPS, GATE_ROWS, tm), lambda i: (0, 0, i)),
            jax.ShapeDtypeStruct((NSA_GROUPS, GATE_ROWS, s), F32))
    outs = [_feat_major(NSA_HEADS, s, tm), tok, tok, rows(NSA_GROUPS), _value_tiles(NSA_GROUPS, s, tm, KV_TILE),
            rows(NSA_GROUPS), _value_tiles(NSA_GROUPS, s, tm, LANE), gate,
            _feat_major(FOX_HEADS, s, tm), rows(FOX_HEADS), _value_tiles(FOX_HEADS, s, tm, KV_TILE),
            (pl.BlockSpec((1, 8, LANE), lambda i: (i, 0, 0)), jax.ShapeDtypeStruct((s // tm, 8, LANE), F32))]
    return pl.pallas_call(
        _hy_prep_kernel,
        grid=(s // tm,),
        in_specs=[pl.BlockSpec((tm, HY_BLOCKS * LANE), lambda i: (i, 0)), pl.BlockSpec((tm, 1), lambda i: (i, 0)),
                  vec, vec, vec, vec, vec, vec],
        out_specs=[o[0] for o in outs],
        out_shape=[o[1] for o in outs],
        scratch_shapes=[pltpu.VMEM((1, LANE), F32)],
        compiler_params=_cparams("arbitrary"),
        name="hybrid_prep",
    )(proj, posf, inv128, gq, gk, gfq, gfk, fbias)


def _compress_kernel(kc_ref, vc_ref, wk_ref, wv_ref, pek_ref, pev_ref, gk_ref, ko_ref, vo_ref):
    half = CMP_STRIDE * HEAD_DIM

    def comp(ch_ref, w_ref, pe_ref):
        ch = ch_ref[0]
        nc = ch.shape[0]
        a = jnp.dot(ch, w_ref[:half], preferred_element_type=F32)
        b = jnp.dot(ch, w_ref[half:], preferred_element_type=F32)
        nxt = pltpu.roll(b, nc - 1, 0)
        pe = jnp.dot(jnp.broadcast_to(pe_ref[...], (8, 2 * half)).astype(BF16), w_ref[...],
                     preferred_element_type=F32)[0:1]
        return a + nxt + pe

    ko_ref[0] = _head_rms(comp(kc_ref, wk_ref, pek_ref), gk_ref[...], HEAD_DIM).astype(BF16)
    vo_ref[0] = comp(vc_ref, wv_ref, pev_ref).T.astype(BF16)


def _compress(kct, vct, w_cmp, cmp_pe, k_norm):
    g, s, _ = kct.shape
    nc = s // CMP_STRIDE
    wide = CMP_STRIDE * HEAD_DIM
    kch = kct.reshape(g, nc, wide)
    vch = vct.reshape(g, nc, wide)
    w_pad = jnp.pad(w_cmp, ((0, 0), (0, 0), (0, LANE - HEAD_DIM))).astype(BF16)
    ch = pl.BlockSpec((1, nc, wide), lambda i: (i, 0, 0))
    wspec = pl.BlockSpec((2 * wide, LANE), lambda i: (0, 0))
    pespec = pl.BlockSpec((1, 2 * wide), lambda i: (0, 0))
    return pl.pallas_call(
        _compress_kernel,
        grid=(g,),
        in_specs=[ch, ch, wspec, wspec, pespec, pespec, pl.BlockSpec((1, LANE), lambda i: (0, 0))],
        out_specs=[pl.BlockSpec((1, nc, LANE), lambda i: (i, 0, 0)),
                   pl.BlockSpec((1, LANE, nc), lambda i: (i, 0, 0))],
        out_shape=[jax.ShapeDtypeStruct((g, nc, LANE), BF16), jax.ShapeDtypeStruct((g, LANE, nc), BF16)],
        compiler_params=_cparams("parallel"),
        name="nsa_compress",
    )(kch, vch, w_pad[0], w_pad[1], cmp_pe[0].reshape(1, 2 * wide).astype(F32),
      cmp_pe[1].reshape(1, 2 * wide).astype(F32), _pad_lanes(k_norm))


def _masked_softmax_t(s, mask):
    s = jnp.where(mask, s, NEG)
    m = jnp.max(s, axis=0, keepdims=True)
    e = jnp.where(mask, jnp.exp2(s - m), 0.0)
    return e, 1.0 / jnp.maximum(jnp.sum(e, axis=0, keepdims=True), 1e-30)


def _online_steps(steps, ms, acc_ref):
    ms = list(ms)

    def scores(step):
        k_tile, qa, _, c, mask = step
        s = jnp.dot(k_tile, qa, preferred_element_type=F32)
        if mask is not None:
            s = jnp.where(mask, s, NEG)
        return s, jnp.max(s, axis=0, keepdims=True)

    nxt = scores(steps[0])
    for idx, (_, _, vt, c, _) in enumerate(steps):
        sl = slice(c * Q_STRIP, (c + 1) * Q_STRIP)
        s, s_max = nxt
        if idx + 1 < len(steps):
            nxt = scores(steps[idx + 1])
        m_new = jnp.maximum(ms[c], s_max)
        a = jnp.exp2(ms[c] - m_new)
        p = jnp.exp2((s - m_new).astype(BF16))
        ms[c] = m_new
        acc_ref[:, sl] = a * acc_ref[:, sl] + jnp.dot(vt, p, preferred_element_type=F32)
    return tuple(ms)


def _m_init(n_strips):
    return tuple(jnp.full((1, Q_STRIP), NEG, F32) for _ in range(n_strips))


def _with_ref_rows(qa, m):
    hi, mid, lo = _split3(-m)
    r = _row_iota((LANE - REF_SLAB, Q_STRIP)) + REF_SLAB
    slab = jnp.where(r == REF_ROW, hi, jnp.where(r == REF_ROW + 1, mid, jnp.where(r == REF_ROW + 2, lo, 0.0)))
    return jnp.concatenate([qa[:REF_SLAB], slab.astype(BF16)], axis=0)


def _first_tile_max(k_tile, qa_strips, masks):
    return tuple(jnp.max(jnp.where(mask, jnp.dot(k_tile, qa, preferred_element_type=F32), NEG), axis=0, keepdims=True)
                 for qa, mask in zip(qa_strips, masks))


def _fast_steps(steps, state, acc_ref):
    state = list(state)
    for k, step in enumerate(steps):
        assert all(prev[3] != step[3] for prev in steps[max(k - SCORE_LOOKAHEAD + 1, 0):k])

    def scores(step):
        k_tile, qa, _, c, mask = step
        s = jnp.dot(k_tile, _with_ref_rows(qa, state[c][0]), preferred_element_type=F32)
        if mask is not None:
            s = jnp.where(mask, s, NEG)
        return s

    ahead = [scores(st) for st in steps[:SCORE_LOOKAHEAD]]
    for idx, (_, _, vt, c, _) in enumerate(steps):
        sl = slice(c * Q_STRIP, (c + 1) * Q_STRIP)
        s = ahead.pop(0)
        m, worst = state[c]
        cm = jnp.max(s, axis=0, keepdims=True)
        inc = jnp.maximum(cm, 0.0)
        state[c] = (m + inc, jnp.maximum(worst, cm))
        if idx + SCORE_LOOKAHEAD < len(steps):
            ahead.append(scores(steps[idx + SCORE_LOOKAHEAD]))
        p = jnp.exp2(s).astype(BF16)
        acc_ref[:, sl] = jnp.exp2(-inc) * (acc_ref[:, sl] + jnp.dot(vt, p, preferred_element_type=F32))
    return tuple(state)


def _flat(state):
    return tuple(x for pair in state for x in pair)


def _nest(flat):
    return tuple((flat[2 * c], flat[2 * c + 1]) for c in range(len(flat) // 2))


def _nsa_kernel(qt_ref, kc_ref, vct_ref, ks_ref, vst_ref, kw_ref, vwt_ref, gate_ref, ovt_ref, o_ref,
                qaug_ref, acc_ref, *, n_sel):
    i = pl.program_id(1)
    tq = Q_TILE_NSA
    cols = NSA_HPG * tq
    qs = i * tq
    nc = kc_ref.shape[1]
    nslc = ovt_ref.shape[0]
    n_super = nslc // SUPER_BLOCKS

    qt = jnp.concatenate([qt_ref[h * LANE:(h + 1) * LANE, :] for h in range(NSA_HPG)], axis=1)
    tq_row = qs + (_lane_iota((1, cols)) % tq)

    s = jnp.dot(kc_ref[0], qt, preferred_element_type=F32)
    cmp_end = _row_iota((nc, 1)) * CMP_STRIDE + (CMP_LEN - 1)
    e, inv_l = _masked_softmax_t(s, cmp_end <= tq_row)
    o_cmp = jnp.dot(vct_ref[0], e.astype(BF16), preferred_element_type=F32) * inv_l

    psum = e[:, 0:tq] * inv_l[:, 0:tq]
    for h in range(1, NSA_HPG):
        psum = psum + e[:, h * tq:(h + 1) * tq] * inv_l[:, h * tq:(h + 1) * tq]
    p_hi = psum.astype(BF16)
    p_lo = (psum - p_hi.astype(F32)).astype(BF16)
    ovt = ovt_ref[...]
    imp = (jnp.dot(ovt, p_hi, preferred_element_type=F32)
           + jnp.dot(ovt, p_lo, preferred_element_type=F32))

    jj = _row_iota((nslc, tq))
    tq_blk = qs + _lane_iota((nslc, tq))
    cur = tq_blk // SLC_BLOCK
    forced = (jj == 0) | (jj == cur) | (jj == cur - 1)
    causal_blk = jj * SLC_BLOCK <= tq_blk
    val = jnp.where(forced, imp + BIG, imp)
    val = jnp.where(causal_blk, val, NEG)

    jjf = jj.astype(F32)

    def pick(_, carry):
        val, sel = carry
        mx = jnp.max(val, axis=0, keepdims=True)
        idx = jnp.min(jnp.where(val == mx, jjf, float(nslc)), axis=0, keepdims=True)
        hit = jjf == idx
        return jnp.where(hit, -jnp.inf, val), jnp.where(hit, 1.0, sel)

    _, sel = lax.fori_loop(0, n_sel, pick, (val, jnp.zeros((nslc, tq), F32)))
    bias_t = jnp.where((sel > 0.0) & causal_blk, 0.0, NEG)

    q_rows = qt[:HEAD_DIM].astype(F32)
    spare = jnp.zeros((LANE - HEAD_DIM - SUPER_BLOCKS, cols), F32)
    for st in range(n_super):
        b = bias_t[st * SUPER_BLOCKS:(st + 1) * SUPER_BLOCKS]
        b = jnp.concatenate([b] * NSA_HPG, axis=1)
        qaug_ref[st] = jnp.concatenate([q_rows, b, spare], axis=0).astype(BF16)

    tk = KV_TILE
    per_super = SUPER_BLOCKS * SLC_BLOCK // tk
    j_last = (qs + tq - 1) // tk
    n_strips = cols // Q_STRIP
    strips = [slice(c * Q_STRIP, (c + 1) * Q_STRIP) for c in range(n_strips)]

    def causal_masks(j):
        kpos = j * tk + _row_iota((tk, 1))
        return [kpos <= tq_row[:, sl] for sl in strips]

    def slc_steps(j, masks=None):
        k0 = pl.multiple_of(j * tk, tk)
        k_tile, vt, st = ks_ref[pl.ds(k0, tk), :], vst_ref[0, j], j // per_super
        return [(k_tile, qaug_ref[st, :, strips[c]], vt, c, None if masks is None else masks[c])
                for c in range(n_strips)]

    group = 4

    def grouped(jj, flat):
        steps = [st for t in range(group) for st in slc_steps(group * jj + t)]
        return _flat(_fast_steps(steps, _nest(flat), acc_ref))

    def single(j, flat):
        return _flat(_fast_steps(slc_steps(j), _nest(flat), acc_ref))

    acc_ref[...] = jnp.zeros(acc_ref.shape, F32)
    head_rows = 16
    head_masks = [_row_iota((head_rows, 1)) <= tq_row[:, sl] for sl in strips]
    m0 = _first_tile_max(ks_ref[0:head_rows, :], [qaug_ref[0, :, sl] for sl in strips], head_masks)
    n_groups = j_last // group
    flat = lax.fori_loop(0, n_groups, grouped, _flat(tuple((m, jnp.zeros_like(m)) for m in m0)))
    flat = lax.fori_loop(group * n_groups, j_last, single, flat)
    state = _fast_steps(slc_steps(j_last, causal_masks(j_last)), _nest(flat), acc_ref)
    worst = jnp.max(jnp.concatenate([w for _, w in state], axis=1))

    @pl.when(worst > EXP_GUARD)
    def _():
        acc_ref[...] = jnp.zeros(acc_ref.shape, F32)
        lax.fori_loop(0, j_last + 1, lambda j, ms: _online_steps(slc_steps(j, causal_masks(j)), ms, acc_ref),
                      _m_init(n_strips))

    acc = acc_ref[...]
    o_slc = acc / jnp.maximum(acc[SUM_ROW:SUM_ROW + 1], 1e-30)

    wlen = WINDOW + tq
    ws = pl.multiple_of(jnp.maximum(qs - WINDOW, 0), tq)
    s = jnp.dot(kw_ref[pl.ds(ws, wlen), :], qt, preferred_element_type=F32)
    dist = tq_row - (ws + _row_iota((wlen, 1)))
    e, inv_l = _masked_softmax_t(s, (dist >= 0) & (dist < WINDOW))
    e = e.astype(BF16)
    wb = ws // LANE
    o_win = jnp.zeros((LANE, cols), F32)
    for c in range(wlen // LANE):
        o_win = o_win + jnp.dot(vwt_ref[0, wb + c], e[c * LANE:(c + 1) * LANE], preferred_element_type=F32)
    o_win = o_win * inv_l

    gate = gate_ref[0]
    for h in range(NSA_HPG):
        sl = slice(h * tq, (h + 1) * tq)
        o = (gate[3 * h:3 * h + 1] * o_cmp[:, sl] + gate[3 * h + 1:3 * h + 2] * o_slc[:, sl]
             + gate[3 * h + 2:3 * h + 3] * o_win[:, sl])
        o_ref[:, h * LANE:(h + 1) * LANE] = o.T.astype(BF16)


def _overlap_t(s, nslc_pad):
    nc = s // CMP_STRIDE
    cmp_start = np.arange(nc) * CMP_STRIDE
    slc_start = np.arange(nslc_pad) * SLC_BLOCK
    ov = np.clip(np.minimum(cmp_start[:, None] + CMP_LEN, slc_start[None, :] + SLC_BLOCK)
                 - np.maximum(cmp_start[:, None], slc_start[None, :]), 0, None) / CMP_STRIDE
    ov[nc - CMP_LEN // CMP_STRIDE + 1:, :] = 0.0
    ov[:, s // SLC_BLOCK:] = 0.0
    return jnp.asarray(ov.T, BF16)


def _nsa_attention(qnt, kc, vct, ks, vst, kw, vwt, gates):
    s = qnt.shape[1]
    nc = s // CMP_STRIDE
    n_slc = s // SLC_BLOCK
    nslc_pad = -(-n_slc // LANE) * LANE
    tq = Q_TILE_NSA
    cols = NSA_HPG * tq
    once = pl.Buffered(1)
    res = pl.BlockSpec((s, LANE), lambda g, i: (0, g), pipeline_mode=once)
    return pl.pallas_call(
        functools.partial(_nsa_kernel, n_sel=min(SLC_TOPK, n_slc)),
        grid=(NSA_GROUPS, s // tq),
        in_specs=[pl.BlockSpec((NSA_HPG * LANE, tq), lambda g, i: (g, i)),
                  pl.BlockSpec((1, nc, LANE), lambda g, i: (g, 0, 0), pipeline_mode=once),
                  pl.BlockSpec((1, LANE, nc), lambda g, i: (g, 0, 0), pipeline_mode=once),
                  res, pl.BlockSpec((1, s // KV_TILE, LANE, KV_TILE), lambda g, i: (g, 0, 0, 0),
                                    pipeline_mode=once),
                  res, pl.BlockSpec((1, s // LANE, LANE, LANE), lambda g, i: (g, 0, 0, 0), pipeline_mode=once),
                  pl.BlockSpec((1, GATE_ROWS, tq), lambda g, i: (g, 0, i)),
                  pl.BlockSpec((nslc_pad, nc), lambda g, i: (0, 0), pipeline_mode=once)],
        out_specs=pl.BlockSpec((tq, NSA_HPG * LANE), lambda g, i: (i, g)),
        out_shape=jax.ShapeDtypeStruct((s, NSA_HEADS * LANE), BF16),
        scratch_shapes=[pltpu.VMEM((nslc_pad // SUPER_BLOCKS, LANE, cols), BF16),
                        pltpu.VMEM((LANE, cols), F32)],
        compiler_params=_cparams("parallel", "arbitrary"),
        name="nsa_attention",
    )(qnt, kc, vct, ks, vst, kw, vwt, gates, _overlap_t(s, nslc_pad))


def _flash_kernel(cfirst_ref, clast_ref, slack_ref, qt_ref, k_ref, vt_ref, o_ref, acc_ref, *, tq):
    h = pl.program_id(0)
    i = pl.program_id(1)
    tk = KV_TILE
    n_tiles = k_ref.shape[0] // tk
    acc_ref[...] = jnp.zeros(acc_ref.shape, F32)

    n_strips = tq // Q_STRIP
    per_q = tq // tk
    qas = [qt_ref[:, c * Q_STRIP:(c + 1) * Q_STRIP] for c in range(n_strips)]

    def tile_steps(j, d=None):
        k0 = pl.multiple_of(j * tk, tk)
        k_tile, vt = k_ref[pl.ds(k0, tk), :], vt_ref[0, j]
        steps = []
        for c in range(n_strips):
            mask = None
            if d is not None:
                if d * tk > (c + 1) * Q_STRIP - 1:
                    continue
                if (d + 1) * tk - 1 > c * Q_STRIP:
                    shp = (tk, Q_STRIP)
                    mask = _row_iota(shp) + d * tk <= _lane_iota(shp) + c * Q_STRIP
            steps.append((k_tile, qas[c], vt, c, mask))
        return steps

    def any_tile_masks(j):
        shp = (tk, Q_STRIP)
        return [_row_iota(shp) + j * tk <= _lane_iota(shp) + (i * tq + c * Q_STRIP) for c in range(n_strips)]

    def below(t, flat):
        jj = i - 1 - t
        bound = (slack_ref[0] + cfirst_ref[h * n_tiles + i * per_q]
                 - clast_ref[h * n_tiles + jj * per_q + per_q - 1])

        def run(flat):
            steps = [st for u in range(per_q) for st in tile_steps(jj * per_q + (per_q - 1 - u))]
            return _flat(_fast_steps(steps, _nest(flat), acc_ref))

        return lax.cond(bound >= -SKIP_MARGIN, run, lambda flat: flat, flat)

    diag0 = i * per_q
    own = [(c * Q_STRIP) // tk for c in range(n_strips)]
    m0 = tuple(_first_tile_max(k_ref[pl.ds(pl.multiple_of((diag0 + own[c]) * tk, tk), tk), :], [qas[c]],
                               [any_tile_masks(diag0 + own[c])[c]])[0] for c in range(n_strips))
    steps = [st for d in reversed(range(per_q)) for st in tile_steps(diag0 + d, d)]
    state = _fast_steps(steps, tuple((m, jnp.zeros_like(m)) for m in m0), acc_ref)
    state = _nest(lax.fori_loop(0, i, below, _flat(state)))
    worst = jnp.max(jnp.concatenate([w for _, w in state], axis=1))

    @pl.when(worst > EXP_GUARD)
    def _():
        acc_ref[...] = jnp.zeros(acc_ref.shape, F32)

        def exact(j, ms):
            k0 = pl.multiple_of(j * tk, tk)
            k_tile, vt, masks = k_ref[pl.ds(k0, tk), :], vt_ref[0, j], any_tile_masks(j)
            return _online_steps([(k_tile, qas[c], vt, c, masks[c]) for c in range(n_strips)], ms, acc_ref)

        lax.fori_loop(0, (i + 1) * per_q, exact, _m_init(n_strips))

    acc = acc_ref[...]
    o = acc / acc[SUM_ROW:SUM_ROW + 1]
    for c0 in range(0, tq, LANE):
        o_ref[c0:c0 + LANE, :] = o[:, c0:c0 + LANE].T.astype(BF16)


def _causal_attention(qt, k, vt, tq, bias_edges=None, slack=None):
    s, width = k.shape
    heads = width // LANE
    tq = min(tq, s)
    assert tq % KV_TILE == 0 and s % tq == 0
    n_tiles = s // KV_TILE
    if bias_edges is None:
        first = last = jnp.zeros((heads * n_tiles,), F32)
        slack = jnp.full((1,), -NEG, F32)
    else:
        first, last = (e.reshape(heads * n_tiles).astype(F32) for e in bias_edges)
    grid_spec = pltpu.PrefetchScalarGridSpec(
        num_scalar_prefetch=3,
        grid=(heads, s // tq),
        in_specs=[pl.BlockSpec((LANE, tq), lambda h, i, *_: (h, i)),
                  pl.BlockSpec((s, LANE), lambda h, i, *_: (0, h)),
                  pl.BlockSpec((1, n_tiles, LANE, KV_TILE), lambda h, i, *_: (h, 0, 0, 0))],
        out_specs=pl.BlockSpec((tq, LANE), lambda h, i, *_: (i, h)),
        scratch_shapes=[pltpu.VMEM((LANE, tq), F32)],
    )
    return pl.pallas_call(
        functools.partial(_flash_kernel, tq=tq),
        grid_spec=grid_spec,
        out_shape=jax.ShapeDtypeStruct((s, width), BF16),
        compiler_params=_cparams("parallel", "arbitrary"),
        name="causal_attention",
    )(first, last, slack.astype(F32), qt, k, vt)


def _out_proj_kernel(oa_ref, ob_ref, wa_ref, wb_ref, x_ref, g_ref, o_ref):
    y = jnp.dot(oa_ref[...], wa_ref[...], preferred_element_type=F32)
    y = y + jnp.dot(ob_ref[...], wb_ref[...], preferred_element_type=F32)
    o_ref[...] = x_ref[...] + g_ref[...] * y


def _out_proj(oa, ob, cola, colb, wa, wb, x2, gate):
    s, d = x2.shape
    ka = wa.shape[0]
    tm = ROW_TILE
    return pl.pallas_call(
        _out_proj_kernel,
        grid=(s // tm,),
        in_specs=[pl.BlockSpec((tm, ka), lambda i: (i, cola)), pl.BlockSpec((tm, ka), lambda i: (i, colb)),
                  pl.BlockSpec((ka, d), lambda i: (0, 0)), pl.BlockSpec((ka, d), lambda i: (0, 0)),
                  pl.BlockSpec((tm, d), lambda i: (i, 0)), pl.BlockSpec((1, d), lambda i: (0, 0))],
        out_specs=pl.BlockSpec((tm, d), lambda i: (i, 0)),
        out_shape=jax.ShapeDtypeStruct((s, d), F32),
        compiler_params=_cparams("parallel"),
        name="out_proj",
    )(oa, ob, wa, wb, x2, gate)


def _mla_prep_kernel(p_ref, pos_ref, inv_ref, gqa_ref, gkva_ref, wuq_ref, wuk_ref, wuv_ref,
                     gq_ref, gk_ref, gkr_ref, qt_ref, k_ref, vt_ref):
    shp = (PREP_TILE, LANE)
    lane = _lane_iota(shp)
    nope = lane < QK_NOPE
    rope = (lane >= QK_NOPE) & (lane < QK_NOPE + QK_ROPE)
    ref_ones = jnp.where((lane >= REF_ROW) & (lane < REF_ROW + 3), 1.0, 0.0)
    ang = pos_ref[...] * inv_ref[...]
    cos = jnp.where(rope, jnp.cos(ang), 1.0)
    sin = jnp.where(rope, jnp.sin(ang), 0.0)

    def rope32(x):
        half = QK_ROPE // 2
        rot = jnp.where(lane < QK_NOPE + half, -pltpu.roll(x, LANE - half, 1), pltpu.roll(x, half, 1))
        return x * cos + rot * sin

    def low_rank_norm(x, g):
        ms = jnp.mean(x * x, axis=-1, keepdims=True)
        return (x * lax.rsqrt(ms + EPS) * g).astype(BF16)

    nq = Q_LORA // LANE
    cq = low_rank_norm(p_ref[:, :Q_LORA], gqa_ref[...])
    ckv = low_rank_norm(p_ref[:, Q_LORA:Q_LORA + KV_LORA], gkva_ref[...])
    kr = p_ref[:, (nq + KV_LORA // LANE) * LANE:(nq + KV_LORA // LANE + 1) * LANE]
    k_rope = rope32(_head_rms(kr, gkr_ref[...], QK_ROPE))

    gq, gk = gq_ref[...], gk_ref[...]
    scale = (QK_NOPE + QK_ROPE) ** -0.5 * LOG2E
    pair = 2 * LANE
    for hp in range(MLA_HEADS // 2):
        cols = slice(hp * pair, (hp + 1) * pair)
        q2 = jnp.dot(cq, wuq_ref[:, cols], preferred_element_type=F32)
        k2 = jnp.dot(ckv, wuk_ref[:, cols], preferred_element_type=F32)
        v2 = jnp.dot(ckv, wuv_ref[:, cols], preferred_element_type=F32)
        for sub in range(2):
            head = 2 * hp + sub
            sl = slice(head * LANE, (head + 1) * LANE)
            half = slice(sub * LANE, (sub + 1) * LANE)
            x = q2[:, half]
            ss_n = jnp.sum(jnp.where(nope, x * x, 0.0), axis=-1, keepdims=True)
            ss_r = jnp.sum(jnp.where(rope, x * x, 0.0), axis=-1, keepdims=True)
            inv_rms = jnp.where(nope, lax.rsqrt(ss_n * (1.0 / QK_NOPE) + EPS),
                                lax.rsqrt(ss_r * (1.0 / QK_ROPE) + EPS))
            qt_ref[sl, :] = (rope32(x * inv_rms * gq) * scale).T.astype(BF16)
            kn = _head_rms(k2[:, half], gk, QK_NOPE)
            k_ref[:, sl] = (kn + k_rope + ref_ones).astype(BF16)
            vt_ref[head, 0] = jnp.where(lane == SUM_ROW, 1.0, v2[:, half]).T.astype(BF16)


def _mla_prep(proj, posf, inv128, gqa, gkva, wuq, wuk, wuv, gq, gk, gkr):
    s, n = proj.shape
    tm = PREP_TILE
    assert tm == KV_TILE

    def full(a):
        return pl.BlockSpec(a.shape, lambda i: (0, 0))

    outs = [_feat_major(MLA_HEADS, s, tm),
            (pl.BlockSpec((tm, MLA_HEADS * LANE), lambda i: (i, 0)),
             jax.ShapeDtypeStruct((s, MLA_HEADS * LANE), BF16)),
            _value_tiles(MLA_HEADS, s, tm, KV_TILE)]
    args = (inv128, gqa, gkva, wuq, wuk, wuv, gq, gk, gkr)
    return pl.pallas_call(
        _mla_prep_kernel,
        grid=(s // tm,),
        in_specs=[pl.BlockSpec((tm, n), lambda i: (i, 0)), pl.BlockSpec((tm, 1), lambda i: (i, 0))]
                 + [full(a) for a in args],
        out_specs=[o[0] for o in outs],
        out_shape=[o[1] for o in outs],
        compiler_params=_cparams("parallel"),
        name="mla_prep",
    )(proj, posf, *args)


def _rank_lt(v, k):
    n = v.shape[0]
    row = _row_iota(v.shape)
    rank = jnp.zeros(v.shape, F32)
    for b in range(n):
        vb = v[b:b + 1, :]
        rank = rank + jnp.where((vb > v) | ((vb == v) & (row > b)), 1.0, 0.0)
    return rank < k


def _moe_route_kernel(x_ref, g_ref, sc_ref, sh_ref, wr_ref, rb_ref, h_ref, pos_ref, wt_ref, cnt_ref):
    tm = ROW_TILE
    h = _norm_mod(x_ref[...], g_ref[...], sc_ref[...], sh_ref[...])
    h_ref[...] = h.astype(BF16)
    logits = jnp.dot(h, wr_ref[...], precision=HIGHEST, preferred_element_type=F32)
    lt = logits.T[:N_EXPERTS]
    scores = jax.nn.sigmoid(lt)
    sel = scores + rb_ref[...]

    per = N_EXPERTS // N_GROUPS
    grp = sel.reshape(N_GROUPS, per, tm)
    sub = lax.broadcasted_iota(jnp.int32, grp.shape, 1)
    m1 = jnp.max(grp, axis=1, keepdims=True)
    first = jnp.min(jnp.where(grp == m1, sub, per), axis=1, keepdims=True)
    m2 = jnp.max(jnp.where(sub == first, -jnp.inf, grp), axis=1, keepdims=True)
    gscore = (m1 + m2).reshape(N_GROUPS, tm)
    gmask = _rank_lt(gscore, TOPK_GROUPS)
    emask = jnp.broadcast_to(gmask.reshape(N_GROUPS, 1, tm), grp.shape).reshape(N_EXPERTS, tm)
    chosen = _rank_lt(jnp.where(emask, sel, NEG), TOP_K)

    w = jnp.where(chosen, scores, 0.0)
    wt_ref[...] = w / jnp.sum(w, axis=0, keepdims=True) * ROUTED_SCALE

    upper = jnp.where(_row_iota((tm, tm)) <= _lane_iota((tm, tm)), 1.0, 0.0).astype(BF16)
    incl = jnp.dot(jnp.where(chosen, 1.0, 0.0).astype(BF16), upper, preferred_element_type=F32)
    pos_ref[...] = jnp.where(chosen, incl - 1.0, -1.0)
    cnt_ref[0] = jnp.broadcast_to(incl[:, tm - 1:tm], (N_EXPERTS, LANE))


def _moe_route(x2, g, sc, sh, w_router_pad, router_bias_col):
    s, d = x2.shape
    tm = ROW_TILE
    vec = pl.BlockSpec((1, d), lambda i: (0, 0))
    et = pl.BlockSpec((N_EXPERTS, tm), lambda i: (0, i))
    return pl.pallas_call(
        _moe_route_kernel,
        grid=(s // tm,),
        in_specs=[pl.BlockSpec((tm, d), lambda i: (i, 0)), vec, vec, vec,
                  pl.BlockSpec((d, LANE), lambda i: (0, 0)),
                  pl.BlockSpec((N_EXPERTS, 1), lambda i: (0, 0))],
        out_specs=[pl.BlockSpec((tm, d), lambda i: (i, 0)), et, et,
                   pl.BlockSpec((1, N_EXPERTS, LANE), lambda i: (i, 0, 0))],
        out_shape=[jax.ShapeDtypeStruct((s, d), BF16), jax.ShapeDtypeStruct((N_EXPERTS, s), F32),
                   jax.ShapeDtypeStruct((N_EXPERTS, s), F32),
                   jax.ShapeDtypeStruct((s // tm, N_EXPERTS, LANE), F32)],
        compiler_params=_cparams("parallel"),
        name="moe_route",
    )(x2, g, sc, sh, w_router_pad, router_bias_col)


def _moe_kernel(cnt_ref, x_ref, h_ref, pos_ref, wt_ref, wg_ref, wu_ref, wd_ref, sg_ref, su_ref, sd_ref,
                g2_ref, o_ref, acc_ref):
    i = pl.program_id(0)
    e = pl.program_id(1)
    tm = ROW_TILE
    r = MOE_CHUNK

    @pl.when(e == 0)
    def _():
        h = h_ref[...]
        a = jnp.dot(h, sg_ref[...], preferred_element_type=F32)
        a = a * jax.nn.sigmoid(a) * jnp.dot(h, su_ref[...], preferred_element_type=F32)
        acc_ref[...] = jnp.dot(a.astype(BF16), sd_ref[...], preferred_element_type=F32)

    first = e * MOE_EXPERTS_PER_STEP
    n = cnt_ref[i * N_EXPERTS + first]
    for k in range(1, MOE_EXPERTS_PER_STEP):
        n = jnp.maximum(n, cnt_ref[i * N_EXPERTS + first + k])
    prows = [pos_ref[pl.ds(first + k, 1), :] for k in range(MOE_EXPERTS_PER_STEP)]
    wrows = [wt_ref[pl.ds(first + k, 1), :] for k in range(MOE_EXPERTS_PER_STEP)]

    def chunk(c, _):
        slot = (_row_iota((r, tm)) + c * r).astype(F32)
        hits = [prow == slot for prow in prows]
        onehot = jnp.concatenate([jnp.where(hit, 1.0, 0.0).astype(BF16) for hit in hits], axis=0)
        xg = jnp.dot(onehot, h_ref[...], preferred_element_type=F32).astype(BF16)
        ys = []
        for k in range(MOE_EXPERTS_PER_STEP):
            xk = xg[k * r:(k + 1) * r]
            a = jnp.dot(xk, wg_ref[k], preferred_element_type=F32)
            a = a * jax.nn.sigmoid(a) * jnp.dot(xk, wu_ref[k], preferred_element_type=F32)
            y = jnp.dot(a.astype(BF16), wd_ref[k], preferred_element_type=F32)
            wr = jnp.sum(jnp.where(hits[k], wrows[k], 0.0), axis=-1, keepdims=True)
            ys.append((y * wr).astype(BF16))
        acc_ref[...] += _dot_tn(onehot, jnp.concatenate(ys, axis=0))
        return 0

    lax.fori_loop(0, (n + r - 1) // r, chunk, 0)

    @pl.when(e == N_EXPERTS // MOE_EXPERTS_PER_STEP - 1)
    def _():
        o_ref[...] = x_ref[...] + g2_ref[...] * acc_ref[...]


def _moe_experts(counts, x2, h, pos_t, w_t, wg, wu, wd, sg, su, sd, g2):
    s, d = x2.shape
    tm = ROW_TILE
    ff = wg.shape[2]
    tile = pl.BlockSpec((tm, d), lambda i, e, c: (i, 0))
    et = pl.BlockSpec((N_EXPERTS, tm), lambda i, e, c: (0, i))

    def const(a):
        return pl.BlockSpec(a.shape, lambda i, e, c: (0,) * a.ndim)

    per = MOE_EXPERTS_PER_STEP
    grid_spec = pltpu.PrefetchScalarGridSpec(
        num_scalar_prefetch=1,
        grid=(s // tm, N_EXPERTS // per),
        in_specs=[tile, tile, et, et,
                  pl.BlockSpec((per, d, ff), lambda i, e, c: (e, 0, 0)),
                  pl.BlockSpec((per, d, ff), lambda i, e, c: (e, 0, 0)),
                  pl.BlockSpec((per, ff, d), lambda i, e, c: (e, 0, 0)),
                  const(sg), const(su), const(sd), const(g2)],
        out_specs=tile,
        scratch_shapes=[pltpu.VMEM((tm, d), F32)],
    )
    return pl.pallas_call(
        _moe_kernel,
        grid_spec=grid_spec,
        out_shape=jax.ShapeDtypeStruct((s, d), F32),
        compiler_params=_cparams("parallel", "arbitrary"),
        name="moe_experts",
    )(counts, x2, h, pos_t, w_t, wg, wu, wd, sg, su, sd, g2)


def _pad_lanes(v, width=LANE, offset=0):
    out = jnp.zeros((1, width), F32)
    return out.at[0, offset:offset + v.shape[0]].set(v.astype(F32))


def _head_cols(w, n_heads, dim):
    d = w.shape[0]
    w3 = w.reshape(d, n_heads, dim)
    return jnp.pad(w3, ((0, 0), (0, 0), (0, LANE - dim))).reshape(d, n_heads * LANE)


def _hybrid_w_in(w_in):
    d = w_in.shape[0]
    nq = NSA_HEADS * HEAD_DIM
    nkv = 6 * NSA_GROUPS * HEAD_DIM
    ng = 3 * NSA_HEADS
    nf = 3 * FOX_HEADS * HEAD_DIM
    c0, c1, c2, c3 = nq, nq + nkv, nq + nkv + ng, nq + nkv + ng + nf
    gates = w_in[:, c1:c2].reshape(d, NSA_GROUPS, 3 * NSA_HPG)
    gates = jnp.pad(gates, ((0, 0), (0, 0), (0, LANE - 3 * NSA_HPG))).reshape(d, NSA_GROUPS * LANE)
    ff = jnp.pad(w_in[:, c3:], ((0, 0), (0, 2 * LANE - FOX_HEADS)))
    return jnp.concatenate([
        _head_cols(w_in[:, :c0], NSA_HEADS, HEAD_DIM),
        _head_cols(w_in[:, c0:c1], 6 * NSA_GROUPS, HEAD_DIM),
        _head_cols(w_in[:, c2:c3], 3 * FOX_HEADS, HEAD_DIM),
        gates, ff], axis=1).astype(BF16)


def _pad_head_rows(w, n_heads, dim):
    d = w.shape[1]
    w3 = w.reshape(n_heads, dim, d)
    return jnp.pad(w3, ((0, 0), (0, LANE - dim), (0, 0))).reshape(n_heads * LANE, d).astype(BF16)


def _rope_inv(dim, offset):
    inv = ROPE_THETA ** (-jnp.arange(0, dim, 2, dtype=F32) / dim)
    return _pad_lanes(jnp.concatenate([inv, inv]), offset=offset)


def _hybrid_mixer(x2, posf, mods, norm_g, w_in, fox_f_bias, nsa_q_norm, nsa_k_norm, nsa_cmp_pe, nsa_w_cmp,
                  fox_q_norm, fox_k_norm, w_out):
    sh1, sc1, g1 = mods
    proj = _norm_mod_matmul(x2, norm_g, sc1, sh1, _hybrid_w_in(w_in), tn=HY_COL_TILE)
    (qnt, kct, vct, ks, vst, kw, vwt, gates, fqt, fk, fvt, cedge) = _hy_prep(
        proj, posf, _rope_inv(HEAD_DIM, 0), _pad_lanes(nsa_q_norm), _pad_lanes(nsa_k_norm),
        _pad_lanes(fox_q_norm), _pad_lanes(fox_k_norm), _pad_lanes(fox_f_bias))
    kc, vc_t = _compress(kct, vct, nsa_w_cmp, nsa_cmp_pe, nsa_k_norm)
    o_a = _nsa_attention(qnt, kc, vc_t, ks, vst, kw, vwt, gates)
    slack = (2.0 * HEAD_DIM ** 0.5 * LOG2E) * jnp.max(jnp.abs(fox_q_norm)) * jnp.max(jnp.abs(fox_k_norm))
    edges = (cedge[:, 0, :FOX_HEADS].T, cedge[:, 1, :FOX_HEADS].T)
    o_b = _causal_attention(fqt, fk, fvt, Q_TILE_FOX, edges, slack.reshape(1))
    half = NSA_HEADS * HEAD_DIM
    wa = _pad_head_rows(w_out[:half], NSA_HEADS, HEAD_DIM)
    wb = _pad_head_rows(w_out[half:], FOX_HEADS, HEAD_DIM)
    return _out_proj(o_a, o_b, 0, 0, wa, wb, x2, g1)


def _mla_mixer(x2, posf, mods, norm_g, w_in, q_a_norm, kv_a_norm, w_uq, w_ukv, qn_norm, kn_norm, qr_norm,
               kr_norm, w_out):
    sh1, sc1, g1 = mods
    d = x2.shape[1]
    w_kr = jnp.zeros((d, LANE), F32).at[:, QK_NOPE:QK_NOPE + QK_ROPE].set(w_in[:, Q_LORA + KV_LORA:])
    w_in_p = jnp.concatenate([w_in[:, :Q_LORA + KV_LORA], w_kr], axis=1).astype(BF16)
    proj = _norm_mod_matmul(x2, norm_g, sc1, sh1, w_in_p, tn=w_in_p.shape[1])
    hq = QK_NOPE + QK_ROPE
    wuq = _head_cols(w_uq, MLA_HEADS, hq).astype(BF16)
    wkv3 = w_ukv.reshape(KV_LORA, MLA_HEADS, QK_NOPE + V_HEAD)
    wuk = _head_cols(wkv3[:, :, :QK_NOPE].reshape(KV_LORA, -1), MLA_HEADS, QK_NOPE).astype(BF16)
    wuv = _head_cols(wkv3[:, :, QK_NOPE:].reshape(KV_LORA, -1), MLA_HEADS, V_HEAD).astype(BF16)
    gq = _pad_lanes(jnp.concatenate([qn_norm, qr_norm]))
    qt, k, vt = _mla_prep(proj, posf, _rope_inv(QK_ROPE, QK_NOPE), q_a_norm.reshape(1, -1).astype(F32),
                          kv_a_norm.reshape(1, -1).astype(F32), wuq, wuk, wuv, gq, _pad_lanes(kn_norm),
                          _pad_lanes(kr_norm, offset=QK_NOPE))
    o = _causal_attention(qt, k, vt, Q_TILE_MLA)
    w_pad = _pad_head_rows(w_out, MLA_HEADS, V_HEAD)
    half = w_pad.shape[0] // 2
    return _out_proj(o, o, 0, 1, w_pad[:half], w_pad[half:], x2, g1)


def _moe_ffn(x2, mods, norm_g, w_router, router_bias, w_gate, w_up, w_down, ws_gate, ws_up, ws_down):
    sh2, sc2, g2 = mods
    w_r = jnp.pad(w_router.astype(F32), ((0, 0), (0, LANE - N_EXPERTS)))
    h, pos_t, w_t, cnt = _moe_route(x2, norm_g, sc2, sh2, w_r, router_bias.reshape(N_EXPERTS, 1).astype(F32))
    counts = cnt[:, :, 0].astype(jnp.int32).reshape(-1)
    return _moe_experts(counts, x2, h, pos_t, w_t, w_gate.astype(BF16), w_up.astype(BF16),
                        w_down.astype(BF16), ws_gate.astype(BF16), ws_up.astype(BF16), ws_down.astype(BF16), g2)


def kernel(x, c, positions, norm_attn, norm_ffn, w_ada, b_ada, hy_w_in, fox_f_bias, nsa_q_norm, nsa_k_norm, nsa_cmp_pe, nsa_w_cmp, fox_q_norm, fox_k_norm, hy_w_out, mla_w_in, mla_q_a_norm, mla_kv_a_norm, mla_w_uq, mla_w_ukv, mla_qn_norm, mla_kn_norm, mla_qr_norm, mla_kr_norm, mla_w_out, moe_w_router, moe_router_bias, moe_w_gate, moe_w_up, moe_w_down, moe_ws_gate, moe_ws_up, moe_ws_down):
    b, s, d = x.shape
    assert b == 1 and s % KV_TILE == 0 and s >= WINDOW + Q_TILE_NSA
    depth = w_ada.shape[0]
    x2 = x.reshape(s, d).astype(F32)
    posf = positions.reshape(s, 1).astype(F32)
    mod = _ada_mod(c.astype(F32), w_ada.astype(F32), b_ada.astype(F32))

    for layer in range(depth):
        m = [mod[layer, :, k * d:(k + 1) * d] for k in range(6)]
        i = layer // 2
        g_attn = norm_attn[layer].reshape(1, d).astype(F32)
        if layer % 2 == 0:
            x2 = _hybrid_mixer(x2, posf, m[0:3], g_attn, hy_w_in[i], fox_f_bias[i], nsa_q_norm[i],
                               nsa_k_norm[i], nsa_cmp_pe[i], nsa_w_cmp[i], fox_q_norm[i], fox_k_norm[i],
                               hy_w_out[i])
        else:
            x2 = _mla_mixer(x2, posf, m[0:3], g_attn, mla_w_in[i], mla_q_a_norm[i], mla_kv_a_norm[i],
                            mla_w_uq[i], mla_w_ukv[i], mla_qn_norm[i], mla_kn_norm[i], mla_qr_norm[i],
                            mla_kr_norm[i], mla_w_out[i])
        x2 = _moe_ffn(x2, m[3:6], norm_ffn[layer].reshape(1, d).astype(F32), moe_w_router[layer],
                      moe_router_bias[layer], moe_w_gate[layer], moe_w_up[layer], moe_w_down[layer],
                      moe_ws_gate[layer], moe_ws_up[layer], moe_ws_down[layer])
    return x2.reshape(b, s, d)
```

```python
import functools

import numpy as np
import jax
import jax.numpy as jnp
from jax import lax
from jax.experimental import pallas as pl
from jax.experimental.pallas import tpu as pltpu

F32 = jnp.float32
BF16 = jnp.bfloat16
HIGHEST = lax.Precision.HIGHEST

LANE = 128
VMEM_LIMIT_BYTES = 56 * 1024 * 1024

HEAD_DIM = 64
NSA_HEADS = 8
NSA_GROUPS = 2
NSA_HPG = NSA_HEADS // NSA_GROUPS
CMP_LEN = 32
CMP_STRIDE = 16
SLC_BLOCK = 64
SLC_TOPK = 16
WINDOW = 512
FOX_HEADS = 8
MLA_HEADS = 16
Q_LORA = 384
KV_LORA = 256
QK_NOPE = 64
QK_ROPE = 32
V_HEAD = 64
N_EXPERTS = 64
TOP_K = 8
N_GROUPS = 8
TOPK_GROUPS = 4
EXPERT_FF = 256
ROUTED_SCALE = 2.5
ROPE_THETA = 10000.0
EPS = 1e-6
NEG = -1e30
BIG = 1e6

ROW_TILE = 512
PREP_TILE = 512
GATE_ROWS = 16
Q_TILE_NSA = 256
KV_TILE = 512
Q_TILE_FOX = 1024
Q_TILE_MLA = 2048
Q_STRIP = 256
SUPER_BLOCKS = 32
SUM_ROW = 64
V_ROWS = 80
REF_ROW = 104
REF_SLAB = 96
EXP_GUARD = 100.0
SCORE_LOOKAHEAD = 3
SKIP_MARGIN = 160.0
LOG2E = 1.4426950408889634
MOE_CHUNK = 128
MOE_EXPERTS_PER_STEP = 8

HY_Q0 = 0
HY_KV0 = HY_Q0 + NSA_HEADS
HY_F0 = HY_KV0 + 6 * NSA_GROUPS
HY_G0 = HY_F0 + 3 * FOX_HEADS
HY_FF = HY_G0 + NSA_GROUPS
HY_BLOCKS = HY_FF + 2
HY_COL_TILE = 12 * LANE
assert (HY_BLOCKS * LANE) % HY_COL_TILE == 0


def _cparams(*sem):
    return pltpu.CompilerParams(dimension_semantics=sem, vmem_limit_bytes=VMEM_LIMIT_BYTES)


def _lane_iota(shape):
    return lax.broadcasted_iota(jnp.int32, shape, len(shape) - 1)


def _row_iota(shape):
    return lax.broadcasted_iota(jnp.int32, shape, len(shape) - 2)


def _dot_nt(a, b):
    return lax.dot_general(a, b, (((1,), (1,)), ((), ())), preferred_element_type=F32)


def _dot_tn(a, b):
    return lax.dot_general(a, b, (((0,), (0,)), ((), ())), preferred_element_type=F32)


def _ada_kernel(c_ref, w_ref, b_ref, o_ref):
    c = c_ref[...]
    cond = c * jax.nn.sigmoid(c)
    o_ref[0] = jnp.dot(cond, w_ref[0], precision=HIGHEST, preferred_element_type=F32) + b_ref[0]


def _ada_mod(c, w_ada, b_ada):
    depth, d, n = w_ada.shape
    tn = 768
    c8 = jnp.broadcast_to(c.reshape(1, d), (8, d))
    out = pl.pallas_call(
        _ada_kernel,
        grid=(depth, n // tn),
        in_specs=[pl.BlockSpec((8, d), lambda l, j: (0, 0)),
                  pl.BlockSpec((1, d, tn), lambda l, j: (l, 0, j)),
                  pl.BlockSpec((1, 1, tn), lambda l, j: (l, 0, j))],
        out_specs=pl.BlockSpec((1, 8, tn), lambda l, j: (l, 0, j)),
        out_shape=jax.ShapeDtypeStruct((depth, 8, n), F32),
        compiler_params=_cparams("parallel", "parallel"),
        name="ada_mod",
    )(c8, w_ada, b_ada.reshape(depth, 1, n))
    return out[:, 0:1, :]


def _norm_mod(x, g, sc, sh):
    ms = jnp.mean(x * x, axis=-1, keepdims=True)
    return (x * lax.rsqrt(ms + EPS) * g) * (1.0 + sc) + sh


def _nmm_kernel(x_ref, g_ref, sc_ref, sh_ref, w_ref, o_ref, h_scr):
    @pl.when(pl.program_id(1) == 0)
    def _():
        h_scr[...] = _norm_mod(x_ref[...], g_ref[...], sc_ref[...], sh_ref[...]).astype(BF16)

    o_ref[...] = jnp.dot(h_scr[...], w_ref[...], preferred_element_type=F32)


def _norm_mod_matmul(x2, g, sc, sh, w, tn):
    s, d = x2.shape
    n = w.shape[1]
    vec = pl.BlockSpec((1, d), lambda i, j: (0, 0))
    return pl.pallas_call(
        _nmm_kernel,
        grid=(s // ROW_TILE, n // tn),
        in_specs=[pl.BlockSpec((ROW_TILE, d), lambda i, j: (i, 0)), vec, vec, vec,
                  pl.BlockSpec((d, tn), lambda i, j: (0, j))],
        out_specs=pl.BlockSpec((ROW_TILE, tn), lambda i, j: (i, j)),
        out_shape=jax.ShapeDtypeStruct((s, n), F32),
        scratch_shapes=[pltpu.VMEM((ROW_TILE, d), BF16)],
        compiler_params=_cparams("parallel", "arbitrary"),
        name="norm_mod_matmul",
    )(x2, g, sc, sh, w)


def _head_rms(x, gain, n_real):
    ss = jnp.sum(x * x, axis=-1, keepdims=True)
    return x * lax.rsqrt(ss * (1.0 / n_real) + EPS) * gain


def _rope64(x, cos, sin):
    lane = _lane_iota(x.shape)
    rot = jnp.where(lane < 32, -pltpu.roll(x, LANE - 32, 1), pltpu.roll(x, 32, 1))
    return x * cos + rot * sin


def _split3(c):
    hi = c.astype(BF16).astype(F32)
    r1 = c - hi
    mid = r1.astype(BF16).astype(F32)
    lo = (r1 - mid).astype(BF16).astype(F32)
    return hi, mid, lo


def _hy_prep_kernel(p_ref, pos_ref, inv_ref, gq_ref, gk_ref, gfq_ref, gfk_ref, fb_ref,
                    qnt_ref, kct_ref, vct_ref, ks_ref, vst_ref, kw_ref, vwt_ref, gate_ref,
                    fqt_ref, fk_ref, fvt_ref, cedge_ref, carry_ref):
    i = pl.program_id(0)
    tm = PREP_TILE
    shp = (tm, LANE)
    lane = _lane_iota(shp)

    def blk(b):
        return p_ref[:, b * LANE:(b + 1) * LANE]

    ang = pos_ref[...] * inv_ref[...]
    real = lane < HEAD_DIM
    cos = jnp.where(real, jnp.cos(ang), 1.0)
    sin = jnp.where(real, jnp.sin(ang), 0.0)
    gq, gk, gfq, gfk = gq_ref[...], gk_ref[...], gfq_ref[...], gfk_ref[...]
    scale = HEAD_DIM ** -0.5 * LOG2E
    ones_row = lane == SUM_ROW
    ref_ones = jnp.where((lane >= REF_ROW) & (lane < REF_ROW + 3), 1.0, 0.0)

    for h in range(NSA_HEADS):
        q = _rope64(_head_rms(blk(HY_Q0 + h), gq, HEAD_DIM), cos, sin) * scale
        qnt_ref[h * LANE:(h + 1) * LANE, :] = q.T.astype(BF16)

    row = _row_iota(shp) + i * tm
    onehot = jnp.where(lane - HEAD_DIM == ((row // SLC_BLOCK) % SUPER_BLOCKS), 1.0, 0.0)
    for g in range(NSA_GROUPS):
        def kv(r):
            return blk(HY_KV0 + r * NSA_GROUPS + g)
        sl = slice(g * LANE, (g + 1) * LANE)
        kct_ref[g] = _rope64(kv(0), cos, sin)[:, :HEAD_DIM].astype(BF16)
        vct_ref[g] = kv(1)[:, :HEAD_DIM].astype(BF16)
        ks = _rope64(_head_rms(kv(2), gk, HEAD_DIM), cos, sin)
        ks_ref[:, sl] = (ks + onehot + ref_ones).astype(BF16)
        vst_ref[g, 0] = jnp.where(ones_row, 1.0, kv(3)).T[:V_ROWS].astype(BF16)
        kw_ref[:, sl] = _rope64(_head_rms(kv(4), gk, HEAD_DIM), cos, sin).astype(BF16)
        vwt = kv(5).T.astype(BF16)
        for cidx in range(tm // LANE):
            vwt_ref[g, cidx] = vwt[:, cidx * LANE:(cidx + 1) * LANE]
        gate_ref[g] = jax.nn.sigmoid(blk(HY_G0 + g)).T[:GATE_ROWS]

    @pl.when(i == 0)
    def _():
        carry_ref[...] = jnp.zeros_like(carry_ref)

    z = blk(HY_FF) + fb_ref[...]
    logf = jnp.minimum(z, 0.0) - jnp.log1p(jnp.exp(-jnp.abs(z)))
    tri = jnp.where(_row_iota((tm, tm)) >= _lane_iota((tm, tm)), 1.0, 0.0).astype(F32)
    cum = jnp.dot(tri, logf, precision=HIGHEST, preferred_element_type=F32) + carry_ref[...]
    carry_ref[...] = cum[tm - 1:tm, :]
    cedge_ref[0] = jnp.concatenate([cum[0:1] * LOG2E, cum[tm - 1:tm] * LOG2E, jnp.zeros((6, LANE), F32)], axis=0)

    for h in range(FOX_HEADS):
        c = jnp.broadcast_to(cum[:, h:h + 1], shp) * LOG2E
        hi, mid, lo = _split3(c)
        fq = _head_rms(blk(HY_F0 + h), gfq, HEAD_DIM) * scale
        fq = jnp.where(real, fq, jnp.where(lane == 64, hi, jnp.where(lane == 65, mid, jnp.where(
            lane == 66, lo, jnp.where(lane < 70, 1.0, 0.0)))))
        fk = _head_rms(blk(HY_F0 + FOX_HEADS + h), gfk, HEAD_DIM)
        fk = jnp.where(real, fk, jnp.where(lane < 67, 1.0, jnp.where(lane == 67, -hi, jnp.where(
            lane == 68, -mid, jnp.where(lane == 69, -lo, ref_ones)))))
        sl = slice(h * LANE, (h + 1) * LANE)
        fqt_ref[sl, :] = fq.T.astype(BF16)
        fk_ref[:, sl] = fk.astype(BF16)
        fvt_ref[h, 0] = jnp.where(ones_row, 1.0, blk(HY_F0 + 2 * FOX_HEADS + h)).T[:V_ROWS].astype(BF16)


def _feat_major(heads, s, tm):
    return (pl.BlockSpec((heads * LANE, tm), lambda i: (0, i)),
            jax.ShapeDtypeStruct((heads * LANE, s), BF16))


def _value_tiles(heads, s, tm, tk, rows=LANE):
    return (pl.BlockSpec((heads, tm // tk, rows, tk), lambda i: (0, i, 0, 0)),
            jax.ShapeDtypeStruct((heads, s // tk, rows, tk), BF16))


def _pad_feature_rows(o):
    return jnp.concatenate([o, jnp.zeros((LANE - HEAD_DIM, o.shape[1]), o.dtype)], axis=0)


def _hy_prep(proj, posf, inv128, gq, gk, gfq, gfk, fbias):
    s = proj.shape[0]
    tm = PREP_TILE
    assert tm == KV_TILE
    vec = pl.BlockSpec((1, LANE), lambda i: (0, 0))

    def rows(nb):
        return (pl.BlockSpec((tm, nb * LANE), lambda i: (i, 0)), jax.ShapeDtypeStruct((s, nb * LANE), BF16))

    tok = (pl.BlockSpec((NSA_GROUPS, tm, HEAD_DIM), lambda i: (0, i, 0)),
           jax.ShapeDtypeStruct((NSA_GROUPS, s, HEAD_DIM), BF16))
    gate = (pl.BlockSpec((NSA_GROUPS, GATE_ROWS, tm), lambda i: (0, 0, i)),
            jax.ShapeDtypeStruct((NSA_GROUPS, GATE_ROWS, s), F32))
    outs = [_feat_major(NSA_HEADS, s, tm), tok, tok, rows(NSA_GROUPS),
            _value_tiles(NSA_GROUPS, s, tm, KV_TILE, V_ROWS),
            rows(NSA_GROUPS), _value_tiles(NSA_GROUPS, s, tm, LANE), gate,
            _feat_major(FOX_HEADS, s, tm), rows(FOX_HEADS), _value_tiles(FOX_HEADS, s, tm, KV_TILE, V_ROWS),
            (pl.BlockSpec((1, 8, LANE), lambda i: (i, 0, 0)), jax.ShapeDtypeStruct((s // tm, 8, LANE), F32))]
    return pl.pallas_call(
        _hy_prep_kernel,
        grid=(s // tm,),
        in_specs=[pl.BlockSpec((tm, HY_BLOCKS * LANE), lambda i: (i, 0)), pl.BlockSpec((tm, 1), lambda i: (i, 0)),
                  vec, vec, vec, vec, vec, vec],
        out_specs=[o[0] for o in outs],
        out_shape=[o[1] for o in outs],
        scratch_shapes=[pltpu.VMEM((1, LANE), F32)],
        compiler_params=_cparams("arbitrary"),
        name="hybrid_prep",
    )(proj, posf, inv128, gq, gk, gfq, gfk, fbias)


def _compress_kernel(kc_ref, vc_ref, wk_ref, wv_ref, pek_ref, pev_ref, gk_ref, ko_ref, vo_ref):
    half = CMP_STRIDE * HEAD_DIM

    def comp(ch_ref, w_ref, pe_ref):
        ch = ch_ref[0]
        nc = ch.shape[0]
        a = jnp.dot(ch, w_ref[:half], preferred_element_type=F32)
        b = jnp.dot(ch, w_ref[half:], preferred_element_type=F32)
        nxt = pltpu.roll(b, nc - 1, 0)
        pe = jnp.dot(jnp.broadcast_to(pe_ref[...], (8, 2 * half)).astype(BF16), w_ref[...],
                     preferred_element_type=F32)[0:1]
        return a + nxt + pe

    ko_ref[0] = _head_rms(comp(kc_ref, wk_ref, pek_ref), gk_ref[...], HEAD_DIM).astype(BF16)
    vo_ref[0] = comp(vc_ref, wv_ref, pev_ref).T.astype(BF16)


def _compress(kct, vct, w_cmp, cmp_pe, k_norm):
    g, s, _ = kct.shape
    nc = s // CMP_STRIDE
    wide = CMP_STRIDE * HEAD_DIM
    kch = kct.reshape(g, nc, wide)
    vch = vct.reshape(g, nc, wide)
    w_pad = jnp.pad(w_cmp, ((0, 0), (0, 0), (0, LANE - HEAD_DIM))).astype(BF16)
    ch = pl.BlockSpec((1, nc, wide), lambda i: (i, 0, 0))
    wspec = pl.BlockSpec((2 * wide, LANE), lambda i: (0, 0))
    pespec = pl.BlockSpec((1, 2 * wide), lambda i: (0, 0))
    return pl.pallas_call(
        _compress_kernel,
        grid=(g,),
        in_specs=[ch, ch, wspec, wspec, pespec, pespec, pl.BlockSpec((1, LANE), lambda i: (0, 0))],
        out_specs=[pl.BlockSpec((1, nc, LANE), lambda i: (i, 0, 0)),
                   pl.BlockSpec((1, LANE, nc), lambda i: (i, 0, 0))],
        out_shape=[jax.ShapeDtypeStruct((g, nc, LANE), BF16), jax.ShapeDtypeStruct((g, LANE, nc), BF16)],
        compiler_params=_cparams("parallel"),
        name="nsa_compress",
    )(kch, vch, w_pad[0], w_pad[1], cmp_pe[0].reshape(1, 2 * wide).astype(F32),
      cmp_pe[1].reshape(1, 2 * wide).astype(F32), _pad_lanes(k_norm))


def _masked_softmax_t(s, mask):
    s = jnp.where(mask, s, NEG)
    m = jnp.max(s, axis=0, keepdims=True)
    e = jnp.where(mask, jnp.exp2(s - m), 0.0)
    return e, 1.0 / jnp.maximum(jnp.sum(e, axis=0, keepdims=True), 1e-30)


def _online_steps(steps, ms, acc_ref):
    ms = list(ms)

    def scores(step):
        k_tile, qa, _, c, mask = step
        s = jnp.dot(k_tile, qa, preferred_element_type=F32)
        if mask is not None:
            s = jnp.where(mask, s, NEG)
        return s, jnp.max(s, axis=0, keepdims=True)

    nxt = scores(steps[0])
    for idx, (_, _, vt, c, _) in enumerate(steps):
        sl = slice(c * Q_STRIP, (c + 1) * Q_STRIP)
        s, s_max = nxt
        if idx + 1 < len(steps):
            nxt = scores(steps[idx + 1])
        m_new = jnp.maximum(ms[c], s_max)
        a = jnp.exp2(ms[c] - m_new)
        p = jnp.exp2((s - m_new).astype(BF16))
        ms[c] = m_new
        acc_ref[:, sl] = a * acc_ref[:, sl] + jnp.dot(vt, p, preferred_element_type=F32)
    return tuple(ms)


def _m_init(n_strips):
    return tuple(jnp.full((1, Q_STRIP), NEG, F32) for _ in range(n_strips))


def _with_ref_rows(qa, m):
    hi, mid, lo = _split3(-m)
    r = _row_iota((LANE - REF_SLAB, Q_STRIP)) + REF_SLAB
    slab = jnp.where(r == REF_ROW, hi, jnp.where(r == REF_ROW + 1, mid, jnp.where(r == REF_ROW + 2, lo, 0.0)))
    return jnp.concatenate([qa[:REF_SLAB], slab.astype(BF16)], axis=0)


def _first_tile_max(k_tile, qa_strips, masks):
    return tuple(jnp.max(jnp.where(mask, jnp.dot(k_tile, qa, preferred_element_type=F32), NEG), axis=0, keepdims=True)
                 for qa, mask in zip(qa_strips, masks))


def _fast_steps(steps, state, acc_ref):
    state = list(state)
    for k, step in enumerate(steps):
        assert all(prev[3] != step[3] for prev in steps[max(k - SCORE_LOOKAHEAD + 1, 0):k])

    def scores(step):
        k_tile, qa, _, c, mask = step
        s = jnp.dot(k_tile, _with_ref_rows(qa, state[c][0]), preferred_element_type=F32)
        if mask is not None:
            s = jnp.where(mask, s, NEG)
        return s

    ahead = [scores(st) for st in steps[:SCORE_LOOKAHEAD]]
    for idx, (_, _, vt, c, _) in enumerate(steps):
        sl = slice(c * Q_STRIP, (c + 1) * Q_STRIP)
        s = ahead.pop(0)
        m, worst = state[c]
        cm = jnp.max(s, axis=0, keepdims=True)
        inc = jnp.maximum(cm, 0.0)
        state[c] = (m + inc, jnp.maximum(worst, cm))
        if idx + SCORE_LOOKAHEAD < len(steps):
            ahead.append(scores(steps[idx + SCORE_LOOKAHEAD]))
        p = jnp.exp2(s).astype(BF16)
        acc_ref[:, sl] = jnp.exp2(-inc) * (acc_ref[:, sl] + jnp.dot(vt, p, preferred_element_type=F32))
    return tuple(state)


def _flat(state):
    return tuple(x for pair in state for x in pair)


def _nest(flat):
    return tuple((flat[2 * c], flat[2 * c + 1]) for c in range(len(flat) // 2))


def _nsa_kernel(qt_ref, kc_ref, vct_ref, ks_ref, vst_ref, kw_ref, vwt_ref, gate_ref, ovt_ref, o_ref,
                qaug_ref, acc_ref, *, n_sel):
    i = pl.program_id(1)
    tq = Q_TILE_NSA
    cols = NSA_HPG * tq
    qs = i * tq
    nc = kc_ref.shape[1]
    nslc = ovt_ref.shape[0]
    n_super = nslc // SUPER_BLOCKS

    qt = jnp.concatenate([qt_ref[h * LANE:(h + 1) * LANE, :] for h in range(NSA_HPG)], axis=1)
    tq_row = qs + (_lane_iota((1, cols)) % tq)

    s = jnp.dot(kc_ref[0], qt, preferred_element_type=F32)
    cmp_end = _row_iota((nc, 1)) * CMP_STRIDE + (CMP_LEN - 1)
    e, inv_l = _masked_softmax_t(s, cmp_end <= tq_row)
    o_cmp = jnp.dot(vct_ref[0], e.astype(BF16), preferred_element_type=F32) * inv_l

    psum = e[:, 0:tq] * inv_l[:, 0:tq]
    for h in range(1, NSA_HPG):
        psum = psum + e[:, h * tq:(h + 1) * tq] * inv_l[:, h * tq:(h + 1) * tq]
    p_hi = psum.astype(BF16)
    p_lo = (psum - p_hi.astype(F32)).astype(BF16)
    ovt = ovt_ref[...]
    imp = (jnp.dot(ovt, p_hi, preferred_element_type=F32)
           + jnp.dot(ovt, p_lo, preferred_element_type=F32))

    jj = _row_iota((nslc, tq))
    tq_blk = qs + _lane_iota((nslc, tq))
    cur = tq_blk // SLC_BLOCK
    forced = (jj == 0) | (jj == cur) | (jj == cur - 1)
    causal_blk = jj * SLC_BLOCK <= tq_blk
    val = jnp.where(forced, imp + BIG, imp)
    val = jnp.where(causal_blk, val, NEG)

    jjf = jj.astype(F32)

    def pick(_, carry):
        val, sel = carry
        mx = jnp.max(val, axis=0, keepdims=True)
        idx = jnp.min(jnp.where(val == mx, jjf, float(nslc)), axis=0, keepdims=True)
        hit = jjf == idx
        return jnp.where(hit, -jnp.inf, val), jnp.where(hit, 1.0, sel)

    _, sel = lax.fori_loop(0, n_sel, pick, (val, jnp.zeros((nslc, tq), F32)))
    bias_t = jnp.where((sel > 0.0) & causal_blk, 0.0, NEG)

    q_rows = qt[:HEAD_DIM].astype(F32)
    spare = jnp.zeros((LANE - HEAD_DIM - SUPER_BLOCKS, cols), F32)
    for st in range(n_super):
        b = bias_t[st * SUPER_BLOCKS:(st + 1) * SUPER_BLOCKS]
        b = jnp.concatenate([b] * NSA_HPG, axis=1)
        qaug_ref[st] = jnp.concatenate([q_rows, b, spare], axis=0).astype(BF16)

    tk = KV_TILE
    per_super = SUPER_BLOCKS * SLC_BLOCK // tk
    j_last = (qs + tq - 1) // tk
    n_strips = cols // Q_STRIP
    strips = [slice(c * Q_STRIP, (c + 1) * Q_STRIP) for c in range(n_strips)]

    def causal_masks(j):
        kpos = j * tk + _row_iota((tk, 1))
        return [kpos <= tq_row[:, sl] for sl in strips]

    def slc_steps(j, masks=None):
        k0 = pl.multiple_of(j * tk, tk)
        k_tile, vt, st = ks_ref[pl.ds(k0, tk), :], vst_ref[0, j], j // per_super
        return [(k_tile, qaug_ref[st, :, strips[c]], vt, c, None if masks is None else masks[c])
                for c in range(n_strips)]

    group = 4

    def grouped(jj, flat):
        steps = [st for t in range(group) for st in slc_steps(group * jj + t)]
        return _flat(_fast_steps(steps, _nest(flat), acc_ref))

    def single(j, flat):
        return _flat(_fast_steps(slc_steps(j), _nest(flat), acc_ref))

    acc_ref[...] = jnp.zeros(acc_ref.shape, F32)
    head_rows = 16
    head_masks = [_row_iota((head_rows, 1)) <= tq_row[:, sl] for sl in strips]
    m0 = _first_tile_max(ks_ref[0:head_rows, :], [qaug_ref[0, :, sl] for sl in strips], head_masks)
    n_groups = j_last // group
    flat = lax.fori_loop(0, n_groups, grouped, _flat(tuple((m, jnp.zeros_like(m)) for m in m0)))
    flat = lax.fori_loop(group * n_groups, j_last, single, flat)
    state = _fast_steps(slc_steps(j_last, causal_masks(j_last)), _nest(flat), acc_ref)
    worst = jnp.max(jnp.concatenate([w for _, w in state], axis=1))

    @pl.when(worst > EXP_GUARD)
    def _():
        acc_ref[...] = jnp.zeros(acc_ref.shape, F32)
        lax.fori_loop(0, j_last + 1, lambda j, ms: _online_steps(slc_steps(j, causal_masks(j)), ms, acc_ref),
                      _m_init(n_strips))

    o_slc = acc_ref[:HEAD_DIM] / jnp.maximum(acc_ref[SUM_ROW:SUM_ROW + 1], 1e-30)

    wlen = WINDOW + tq
    ws = pl.multiple_of(jnp.maximum(qs - WINDOW, 0), tq)
    s = jnp.dot(kw_ref[pl.ds(ws, wlen), :], qt, preferred_element_type=F32)
    dist = tq_row - (ws + _row_iota((wlen, 1)))
    e, inv_l = _masked_softmax_t(s, (dist >= 0) & (dist < WINDOW))
    e = e.astype(BF16)
    wb = ws // LANE
    o_win = jnp.zeros((LANE, cols), F32)
    for c in range(wlen // LANE):
        o_win = o_win + jnp.dot(vwt_ref[0, wb + c], e[c * LANE:(c + 1) * LANE], preferred_element_type=F32)
    o_win = o_win * inv_l

    gate = gate_ref[0]
    for h in range(NSA_HPG):
        sl = slice(h * tq, (h + 1) * tq)
        o = (gate[3 * h:3 * h + 1] * o_cmp[:HEAD_DIM, sl] + gate[3 * h + 1:3 * h + 2] * o_slc[:, sl]
             + gate[3 * h + 2:3 * h + 3] * o_win[:HEAD_DIM, sl])
        o_ref[:, h * LANE:(h + 1) * LANE] = _pad_feature_rows(o).T.astype(BF16)


def _overlap_t(s, nslc_pad):
    nc = s // CMP_STRIDE
    cmp_start = np.arange(nc) * CMP_STRIDE
    slc_start = np.arange(nslc_pad) * SLC_BLOCK
    ov = np.clip(np.minimum(cmp_start[:, None] + CMP_LEN, slc_start[None, :] + SLC_BLOCK)
                 - np.maximum(cmp_start[:, None], slc_start[None, :]), 0, None) / CMP_STRIDE
    ov[nc - CMP_LEN // CMP_STRIDE + 1:, :] = 0.0
    ov[:, s // SLC_BLOCK:] = 0.0
    return jnp.asarray(ov.T, BF16)


def _nsa_attention(qnt, kc, vct, ks, vst, kw, vwt, gates):
    s = qnt.shape[1]
    nc = s // CMP_STRIDE
    n_slc = s // SLC_BLOCK
    nslc_pad = -(-n_slc // LANE) * LANE
    tq = Q_TILE_NSA
    cols = NSA_HPG * tq
    once = pl.Buffered(1)
    res = pl.BlockSpec((s, LANE), lambda g, i: (0, g), pipeline_mode=once)
    return pl.pallas_call(
        functools.partial(_nsa_kernel, n_sel=min(SLC_TOPK, n_slc)),
        grid=(NSA_GROUPS, s // tq),
        in_specs=[pl.BlockSpec((NSA_HPG * LANE, tq), lambda g, i: (g, i)),
                  pl.BlockSpec((1, nc, LANE), lambda g, i: (g, 0, 0), pipeline_mode=once),
                  pl.BlockSpec((1, LANE, nc), lambda g, i: (g, 0, 0), pipeline_mode=once),
                  res, pl.BlockSpec((1, s // KV_TILE, V_ROWS, KV_TILE), lambda g, i: (g, 0, 0, 0),
                                    pipeline_mode=once),
                  res, pl.BlockSpec((1, s // LANE, LANE, LANE), lambda g, i: (g, 0, 0, 0), pipeline_mode=once),
                  pl.BlockSpec((1, GATE_ROWS, tq), lambda g, i: (g, 0, i)),
                  pl.BlockSpec((nslc_pad, nc), lambda g, i: (0, 0), pipeline_mode=once)],
        out_specs=pl.BlockSpec((tq, NSA_HPG * LANE), lambda g, i: (i, g)),
        out_shape=jax.ShapeDtypeStruct((s, NSA_HEADS * LANE), BF16),
        scratch_shapes=[pltpu.VMEM((nslc_pad // SUPER_BLOCKS, LANE, cols), BF16),
                        pltpu.VMEM((V_ROWS, cols), F32)],
        compiler_params=_cparams("parallel", "arbitrary"),
        name="nsa_attention",
    )(qnt, kc, vct, ks, vst, kw, vwt, gates, _overlap_t(s, nslc_pad))


def _flash_kernel(cfirst_ref, clast_ref, slack_ref, qt_ref, k_ref, vt_ref, o_ref, acc_ref, *, tq):
    h = pl.program_id(0)
    i = pl.program_id(1)
    tk = KV_TILE
    n_tiles = k_ref.shape[0] // tk
    acc_ref[...] = jnp.zeros(acc_ref.shape, F32)

    n_strips = tq // Q_STRIP
    per_q = tq // tk
    qas = [qt_ref[:, c * Q_STRIP:(c + 1) * Q_STRIP] for c in range(n_strips)]

    def tile_steps(j, d=None):
        k0 = pl.multiple_of(j * tk, tk)
        k_tile, vt = k_ref[pl.ds(k0, tk), :], vt_ref[0, j]
        steps = []
        for c in range(n_strips):
            mask = None
            if d is not None:
                if d * tk > (c + 1) * Q_STRIP - 1:
                    continue
                if (d + 1) * tk - 1 > c * Q_STRIP:
                    shp = (tk, Q_STRIP)
                    mask = _row_iota(shp) + d * tk <= _lane_iota(shp) + c * Q_STRIP
            steps.append((k_tile, qas[c], vt, c, mask))
        return steps

    def any_tile_masks(j):
        shp = (tk, Q_STRIP)
        return [_row_iota(shp) + j * tk <= _lane_iota(shp) + (i * tq + c * Q_STRIP) for c in range(n_strips)]

    def below(t, flat):
        jj = i - 1 - t
        bound = (slack_ref[0] + cfirst_ref[h * n_tiles + i * per_q]
                 - clast_ref[h * n_tiles + jj * per_q + per_q - 1])

        def run(flat):
            steps = [st for u in range(per_q) for st in tile_steps(jj * per_q + (per_q - 1 - u))]
            return _flat(_fast_steps(steps, _nest(flat), acc_ref))

        return lax.cond(bound >= -SKIP_MARGIN, run, lambda flat: flat, flat)

    diag0 = i * per_q
    own = [(c * Q_STRIP) // tk for c in range(n_strips)]
    m0 = tuple(_first_tile_max(k_ref[pl.ds(pl.multiple_of((diag0 + own[c]) * tk, tk), tk), :], [qas[c]],
                               [any_tile_masks(diag0 + own[c])[c]])[0] for c in range(n_strips))
    steps = [st for d in reversed(range(per_q)) for st in tile_steps(diag0 + d, d)]
    state = _fast_steps(steps, tuple((m, jnp.zeros_like(m)) for m in m0), acc_ref)
    state = _nest(lax.fori_loop(0, i, below, _flat(state)))
    worst = jnp.max(jnp.concatenate([w for _, w in state], axis=1))

    @pl.when(worst > EXP_GUARD)
    def _():
        acc_ref[...] = jnp.zeros(acc_ref.shape, F32)

        def exact(j, ms):
            k0 = pl.multiple_of(j * tk, tk)
            k_tile, vt, masks = k_ref[pl.ds(k0, tk), :], vt_ref[0, j], any_tile_masks(j)
            return _online_steps([(k_tile, qas[c], vt, c, masks[c]) for c in range(n_strips)], ms, acc_ref)

        lax.fori_loop(0, (i + 1) * per_q, exact, _m_init(n_strips))

    o = _pad_feature_rows(acc_ref[:HEAD_DIM] / acc_ref[SUM_ROW:SUM_ROW + 1])
    for c0 in range(0, tq, LANE):
        o_ref[c0:c0 + LANE, :] = o[:, c0:c0 + LANE].T.astype(BF16)


def _causal_attention(qt, k, vt, tq, bias_edges=None, slack=None):
    s, width = k.shape
    heads = width // LANE
    tq = min(tq, s)
    assert tq % KV_TILE == 0 and s % tq == 0
    n_tiles = s // KV_TILE
    if bias_edges is None:
        first = last = jnp.zeros((heads * n_tiles,), F32)
        slack = jnp.full((1,), -NEG, F32)
    else:
        first, last = (e.reshape(heads * n_tiles).astype(F32) for e in bias_edges)
    grid_spec = pltpu.PrefetchScalarGridSpec(
        num_scalar_prefetch=3,
        grid=(heads, s // tq),
        in_specs=[pl.BlockSpec((LANE, tq), lambda h, i, *_: (h, i)),
                  pl.BlockSpec((s, LANE), lambda h, i, *_: (0, h)),
                  pl.BlockSpec((1, n_tiles, V_ROWS, KV_TILE), lambda h, i, *_: (h, 0, 0, 0))],
        out_specs=pl.BlockSpec((tq, LANE), lambda h, i, *_: (i, h)),
        scratch_shapes=[pltpu.VMEM((V_ROWS, tq), F32)],
    )
    return pl.pallas_call(
        functools.partial(_flash_kernel, tq=tq),
        grid_spec=grid_spec,
        out_shape=jax.ShapeDtypeStruct((s, width), BF16),
        compiler_params=_cparams("parallel", "arbitrary"),
        name="causal_attention",
    )(first, last, slack.astype(F32), qt, k, vt)


def _out_proj_kernel(oa_ref, ob_ref, wa_ref, wb_ref, x_ref, g_ref, o_ref):
    y = jnp.dot(oa_ref[...], wa_ref[...], preferred_element_type=F32)
    y = y + jnp.dot(ob_ref[...], wb_ref[...], preferred_element_type=F32)
    o_ref[...] = x_ref[...] + g_ref[...] * y


def _out_proj(oa, ob, cola, colb, wa, wb, x2, gate):
    s, d = x2.shape
    ka = wa.shape[0]
    tm = ROW_TILE
    return pl.pallas_call(
        _out_proj_kernel,
        grid=(s // tm,),
        in_specs=[pl.BlockSpec((tm, ka), lambda i: (i, cola)), pl.BlockSpec((tm, ka), lambda i: (i, colb)),
                  pl.BlockSpec((ka, d), lambda i: (0, 0)), pl.BlockSpec((ka, d), lambda i: (0, 0)),
                  pl.BlockSpec((tm, d), lambda i: (i, 0)), pl.BlockSpec((1, d), lambda i: (0, 0))],
        out_specs=pl.BlockSpec((tm, d), lambda i: (i, 0)),
        out_shape=jax.ShapeDtypeStruct((s, d), F32),
        compiler_params=_cparams("parallel"),
        name="out_proj",
    )(oa, ob, wa, wb, x2, gate)


def _mla_prep_kernel(p_ref, pos_ref, inv_ref, gqa_ref, gkva_ref, wuq_ref, wuk_ref, wuv_ref,
                     gq_ref, gk_ref, gkr_ref, qt_ref, k_ref, vt_ref):
    shp = (PREP_TILE, LANE)
    lane = _lane_iota(shp)
    nope = lane < QK_NOPE
    rope = (lane >= QK_NOPE) & (lane < QK_NOPE + QK_ROPE)
    ref_ones = jnp.where((lane >= REF_ROW) & (lane < REF_ROW + 3), 1.0, 0.0)
    ang = pos_ref[...] * inv_ref[...]
    cos = jnp.where(rope, jnp.cos(ang), 1.0)
    sin = jnp.where(rope, jnp.sin(ang), 0.0)

    def rope32(x):
        half = QK_ROPE // 2
        rot = jnp.where(lane < QK_NOPE + half, -pltpu.roll(x, LANE - half, 1), pltpu.roll(x, half, 1))
        return x * cos + rot * sin

    def low_rank_norm(x, g):
        ms = jnp.mean(x * x, axis=-1, keepdims=True)
        return (x * lax.rsqrt(ms + EPS) * g).astype(BF16)

    nq = Q_LORA // LANE
    cq = low_rank_norm(p_ref[:, :Q_LORA], gqa_ref[...])
    ckv = low_rank_norm(p_ref[:, Q_LORA:Q_LORA + KV_LORA], gkva_ref[...])
    kr = p_ref[:, (nq + KV_LORA // LANE) * LANE:(nq + KV_LORA // LANE + 1) * LANE]
    k_rope = rope32(_head_rms(kr, gkr_ref[...], QK_ROPE))

    gq, gk = gq_ref[...], gk_ref[...]
    scale = (QK_NOPE + QK_ROPE) ** -0.5 * LOG2E
    pair = 2 * LANE
    for hp in range(MLA_HEADS // 2):
        cols = slice(hp * pair, (hp + 1) * pair)
        q2 = jnp.dot(cq, wuq_ref[:, cols], preferred_element_type=F32)
        k2 = jnp.dot(ckv, wuk_ref[:, cols], preferred_element_type=F32)
        v2 = jnp.dot(ckv, wuv_ref[:, cols], preferred_element_type=F32)
        for sub in range(2):
            head = 2 * hp + sub
            sl = slice(head * LANE, (head + 1) * LANE)
            half = slice(sub * LANE, (sub + 1) * LANE)
            x = q2[:, half]
            ss_n = jnp.sum(jnp.where(nope, x * x, 0.0), axis=-1, keepdims=True)
            ss_r = jnp.sum(jnp.where(rope, x * x, 0.0), axis=-1, keepdims=True)
            inv_rms = jnp.where(nope, lax.rsqrt(ss_n * (1.0 / QK_NOPE) + EPS),
                                lax.rsqrt(ss_r * (1.0 / QK_ROPE) + EPS))
            qt_ref[sl, :] = (rope32(x * inv_rms * gq) * scale).T.astype(BF16)
            kn = _head_rms(k2[:, half], gk, QK_NOPE)
            k_ref[:, sl] = (kn + k_rope + ref_ones).astype(BF16)
            vt_ref[head, 0] = jnp.where(lane == SUM_ROW, 1.0, v2[:, half]).T[:V_ROWS].astype(BF16)


def _mla_prep(proj, posf, inv128, gqa, gkva, wuq, wuk, wuv, gq, gk, gkr):
    s, n = proj.shape
    tm = PREP_TILE
    assert tm == KV_TILE

    def full(a):
        return pl.BlockSpec(a.shape, lambda i: (0, 0))

    outs = [_feat_major(MLA_HEADS, s, tm),
            (pl.BlockSpec((tm, MLA_HEADS * LANE), lambda i: (i, 0)),
             jax.ShapeDtypeStruct((s, MLA_HEADS * LANE), BF16)),
            _value_tiles(MLA_HEADS, s, tm, KV_TILE, V_ROWS)]
    args = (inv128, gqa, gkva, wuq, wuk, wuv, gq, gk, gkr)
    return pl.pallas_call(
        _mla_prep_kernel,
        grid=(s // tm,),
        in_specs=[pl.BlockSpec((tm, n), lambda i: (i, 0)), pl.BlockSpec((tm, 1), lambda i: (i, 0))]
                 + [full(a) for a in args],
        out_specs=[o[0] for o in outs],
        out_shape=[o[1] for o in outs],
        compiler_params=_cparams("parallel"),
        name="mla_prep",
    )(proj, posf, *args)


def _rank_lt(v, k):
    n = v.shape[0]
    row = _row_iota(v.shape)
    rank = jnp.zeros(v.shape, F32)
    for b in range(n):
        vb = v[b:b + 1, :]
        rank = rank + jnp.where((vb > v) | ((vb == v) & (row > b)), 1.0, 0.0)
    return rank < k


def _moe_route_kernel(x_ref, g_ref, sc_ref, sh_ref, wr_ref, rb_ref, h_ref, pos_ref, wt_ref, cnt_ref):
    tm = ROW_TILE
    h = _norm_mod(x_ref[...], g_ref[...], sc_ref[...], sh_ref[...])
    h_ref[...] = h.astype(BF16)
    logits = jnp.dot(h, wr_ref[...], precision=HIGHEST, preferred_element_type=F32)
    lt = logits.T[:N_EXPERTS]
    scores = jax.nn.sigmoid(lt)
    sel = scores + rb_ref[...]

    per = N_EXPERTS // N_GROUPS
    grp = sel.reshape(N_GROUPS, per, tm)
    sub = lax.broadcasted_iota(jnp.int32, grp.shape, 1)
    m1 = jnp.max(grp, axis=1, keepdims=True)
    first = jnp.min(jnp.where(grp == m1, sub, per), axis=1, keepdims=True)
    m2 = jnp.max(jnp.where(sub == first, -jnp.inf, grp), axis=1, keepdims=True)
    gscore = (m1 + m2).reshape(N_GROUPS, tm)
    gmask = _rank_lt(gscore, TOPK_GROUPS)
    emask = jnp.broadcast_to(gmask.reshape(N_GROUPS, 1, tm), grp.shape).reshape(N_EXPERTS, tm)
    chosen = _rank_lt(jnp.where(emask, sel, NEG), TOP_K)

    w = jnp.where(chosen, scores, 0.0)
    wt_ref[...] = w / jnp.sum(w, axis=0, keepdims=True) * ROUTED_SCALE

    upper = jnp.where(_row_iota((tm, tm)) <= _lane_iota((tm, tm)), 1.0, 0.0).astype(BF16)
    incl = jnp.dot(jnp.where(chosen, 1.0, 0.0).astype(BF16), upper, preferred_element_type=F32)
    pos_ref[...] = jnp.where(chosen, incl - 1.0, -1.0)
    cnt_ref[0] = jnp.broadcast_to(incl[:, tm - 1:tm], (N_EXPERTS, LANE))


def _moe_route(x2, g, sc, sh, w_router_pad, router_bias_col):
    s, d = x2.shape
    tm = ROW_TILE
    vec = pl.BlockSpec((1, d), lambda i: (0, 0))
    et = pl.BlockSpec((N_EXPERTS, tm), lambda i: (0, i))
    return pl.pallas_call(
        _moe_route_kernel,
        grid=(s // tm,),
        in_specs=[pl.BlockSpec((tm, d), lambda i: (i, 0)), vec, vec, vec,
                  pl.BlockSpec((d, LANE), lambda i: (0, 0)),
                  pl.BlockSpec((N_EXPERTS, 1), lambda i: (0, 0))],
        out_specs=[pl.BlockSpec((tm, d), lambda i: (i, 0)), et, et,
                   pl.BlockSpec((1, N_EXPERTS, LANE), lambda i: (i, 0, 0))],
        out_shape=[jax.ShapeDtypeStruct((s, d), BF16), jax.ShapeDtypeStruct((N_EXPERTS, s), F32),
                   jax.ShapeDtypeStruct((N_EXPERTS, s), F32),
                   jax.ShapeDtypeStruct((s // tm, N_EXPERTS, LANE), F32)],
        compiler_params=_cparams("parallel"),
        name="moe_route",
    )(x2, g, sc, sh, w_router_pad, router_bias_col)


def _moe_kernel(cnt_ref, x_ref, h_ref, pos_ref, wt_ref, wg_ref, wu_ref, wd_ref, sg_ref, su_ref, sd_ref,
                g2_ref, o_ref, acc_ref):
    i = pl.program_id(0)
    e = pl.program_id(1)
    tm = ROW_TILE
    r = MOE_CHUNK

    @pl.when(e == 0)
    def _():
        h = h_ref[...]
        a = jnp.dot(h, sg_ref[...], preferred_element_type=F32)
        a = a * jax.nn.sigmoid(a) * jnp.dot(h, su_ref[...], preferred_element_type=F32)
        acc_ref[...] = jnp.dot(a.astype(BF16), sd_ref[...], preferred_element_type=F32)

    first = e * MOE_EXPERTS_PER_STEP
    n = cnt_ref[i * N_EXPERTS + first]
    for k in range(1, MOE_EXPERTS_PER_STEP):
        n = jnp.maximum(n, cnt_ref[i * N_EXPERTS + first + k])
    prows = [pos_ref[pl.ds(first + k, 1), :] for k in range(MOE_EXPERTS_PER_STEP)]
    wrows = [wt_ref[pl.ds(first + k, 1), :] for k in range(MOE_EXPERTS_PER_STEP)]

    def chunk(c, _):
        slot = (_row_iota((r, tm)) + c * r).astype(F32)
        hits = [prow == slot for prow in prows]
        onehot = jnp.concatenate([jnp.where(hit, 1.0, 0.0).astype(BF16) for hit in hits], axis=0)
        xg = jnp.dot(onehot, h_ref[...], preferred_element_type=F32).astype(BF16)
        ys = []
        for k in range(MOE_EXPERTS_PER_STEP):
            xk = xg[k * r:(k + 1) * r]
            a = jnp.dot(xk, wg_ref[k], preferred_element_type=F32)
            a = a * jax.nn.sigmoid(a) * jnp.dot(xk, wu_ref[k], preferred_element_type=F32)
            y = jnp.dot(a.astype(BF16), wd_ref[k], preferred_element_type=F32)
            wr = jnp.sum(jnp.where(hits[k], wrows[k], 0.0), axis=-1, keepdims=True)
            ys.append((y * wr).astype(BF16))
        acc_ref[...] += _dot_tn(onehot, jnp.concatenate(ys, axis=0))
        return 0

    lax.fori_loop(0, (n + r - 1) // r, chunk, 0)

    @pl.when(e == N_EXPERTS // MOE_EXPERTS_PER_STEP - 1)
    def _():
        o_ref[...] = x_ref[...] + g2_ref[...] * acc_ref[...]


def _moe_experts(counts, x2, h, pos_t, w_t, wg, wu, wd, sg, su, sd, g2):
    s, d = x2.shape
    tm = ROW_TILE
    ff = wg.shape[2]
    tile = pl.BlockSpec((tm, d), lambda i, e, c: (i, 0))
    et = pl.BlockSpec((N_EXPERTS, tm), lambda i, e, c: (0, i))

    def const(a):
        return pl.BlockSpec(a.shape, lambda i, e, c: (0,) * a.ndim)

    per = MOE_EXPERTS_PER_STEP
    grid_spec = pltpu.PrefetchScalarGridSpec(
        num_scalar_prefetch=1,
        grid=(s // tm, N_EXPERTS // per),
        in_specs=[tile, tile, et, et,
                  pl.BlockSpec((per, d, ff), lambda i, e, c: (e, 0, 0)),
                  pl.BlockSpec((per, d, ff), lambda i, e, c: (e, 0, 0)),
                  pl.BlockSpec((per, ff, d), lambda i, e, c: (e, 0, 0)),
                  const(sg), const(su), const(sd), const(g2)],
        out_specs=tile,
        scratch_shapes=[pltpu.VMEM((tm, d), F32)],
    )
    return pl.pallas_call(
        _moe_kernel,
        grid_spec=grid_spec,
        out_shape=jax.ShapeDtypeStruct((s, d), F32),
        compiler_params=_cparams("parallel", "arbitrary"),
        name="moe_experts",
    )(counts, x2, h, pos_t, w_t, wg, wu, wd, sg, su, sd, g2)


def _pad_lanes(v, width=LANE, offset=0):
    out = jnp.zeros((1, width), F32)
    return out.at[0, offset:offset + v.shape[0]].set(v.astype(F32))


def _head_cols(w, n_heads, dim):
    d = w.shape[0]
    w3 = w.reshape(d, n_heads, dim)
    return jnp.pad(w3, ((0, 0), (0, 0), (0, LANE - dim))).reshape(d, n_heads * LANE)


def _hybrid_w_in(w_in):
    d = w_in.shape[0]
    nq = NSA_HEADS * HEAD_DIM
    nkv = 6 * NSA_GROUPS * HEAD_DIM
    ng = 3 * NSA_HEADS
    nf = 3 * FOX_HEADS * HEAD_DIM
    c0, c1, c2, c3 = nq, nq + nkv, nq + nkv + ng, nq + nkv + ng + nf
    gates = w_in[:, c1:c2].reshape(d, NSA_GROUPS, 3 * NSA_HPG)
    gates = jnp.pad(gates, ((0, 0), (0, 0), (0, LANE - 3 * NSA_HPG))).reshape(d, NSA_GROUPS * LANE)
    ff = jnp.pad(w_in[:, c3:], ((0, 0), (0, 2 * LANE - FOX_HEADS)))
    return jnp.concatenate([
        _head_cols(w_in[:, :c0], NSA_HEADS, HEAD_DIM),
        _head_cols(w_in[:, c0:c1], 6 * NSA_GROUPS, HEAD_DIM),
        _head_cols(w_in[:, c2:c3], 3 * FOX_HEADS, HEAD_DIM),
        gates, ff], axis=1).astype(BF16)


def _pad_head_rows(w, n_heads, dim):
    d = w.shape[1]
    w3 = w.reshape(n_heads, dim, d)
    return jnp.pad(w3, ((0, 0), (0, LANE - dim), (0, 0))).reshape(n_heads * LANE, d).astype(BF16)


def _rope_inv(dim, offset):
    inv = ROPE_THETA ** (-jnp.arange(0, dim, 2, dtype=F32) / dim)
    return _pad_lanes(jnp.concatenate([inv, inv]), offset=offset)


def _hybrid_mixer(x2, posf, mods, norm_g, w_in, fox_f_bias, nsa_q_norm, nsa_k_norm, nsa_cmp_pe, nsa_w_cmp,
                  fox_q_norm, fox_k_norm, w_out):
    sh1, sc1, g1 = mods
    proj = _norm_mod_matmul(x2, norm_g, sc1, sh1, _hybrid_w_in(w_in), tn=HY_COL_TILE)
    (qnt, kct, vct, ks, vst, kw, vwt, gates, fqt, fk, fvt, cedge) = _hy_prep(
        proj, posf, _rope_inv(HEAD_DIM, 0), _pad_lanes(nsa_q_norm), _pad_lanes(nsa_k_norm),
        _pad_lanes(fox_q_norm), _pad_lanes(fox_k_norm), _pad_lanes(fox_f_bias))
    kc, vc_t = _compress(kct, vct, nsa_w_cmp, nsa_cmp_pe, nsa_k_norm)
    o_a = _nsa_attention(qnt, kc, vc_t, ks, vst, kw, vwt, gates)
    slack = (2.0 * HEAD_DIM ** 0.5 * LOG2E) * jnp.max(jnp.abs(fox_q_norm)) * jnp.max(jnp.abs(fox_k_norm))
    edges = (cedge[:, 0, :FOX_HEADS].T, cedge[:, 1, :FOX_HEADS].T)
    o_b = _causal_attention(fqt, fk, fvt, Q_TILE_FOX, edges, slack.reshape(1))
    half = NSA_HEADS * HEAD_DIM
    wa = _pad_head_rows(w_out[:half], NSA_HEADS, HEAD_DIM)
    wb = _pad_head_rows(w_out[half:], FOX_HEADS, HEAD_DIM)
    return _out_proj(o_a, o_b, 0, 0, wa, wb, x2, g1)


def _mla_mixer(x2, posf, mods, norm_g, w_in, q_a_norm, kv_a_norm, w_uq, w_ukv, qn_norm, kn_norm, qr_norm,
               kr_norm, w_out):
    sh1, sc1, g1 = mods
    d = x2.shape[1]
    w_kr = jnp.zeros((d, LANE), F32).at[:, QK_NOPE:QK_NOPE + QK_ROPE].set(w_in[:, Q_LORA + KV_LORA:])
    w_in_p = jnp.concatenate([w_in[:, :Q_LORA + KV_LORA], w_kr], axis=1).astype(BF16)
    proj = _norm_mod_matmul(x2, norm_g, sc1, sh1, w_in_p, tn=w_in_p.shape[1])
    hq = QK_NOPE + QK_ROPE
    wuq = _head_cols(w_uq, MLA_HEADS, hq).astype(BF16)
    wkv3 = w_ukv.reshape(KV_LORA, MLA_HEADS, QK_NOPE + V_HEAD)
    wuk = _head_cols(wkv3[:, :, :QK_NOPE].reshape(KV_LORA, -1), MLA_HEADS, QK_NOPE).astype(BF16)
    wuv = _head_cols(wkv3[:, :, QK_NOPE:].reshape(KV_LORA, -1), MLA_HEADS, V_HEAD).astype(BF16)
    gq = _pad_lanes(jnp.concatenate([qn_norm, qr_norm]))
    qt, k, vt = _mla_prep(proj, posf, _rope_inv(QK_ROPE, QK_NOPE), q_a_norm.reshape(1, -1).astype(F32),
                          kv_a_norm.reshape(1, -1).astype(F32), wuq, wuk, wuv, gq, _pad_lanes(kn_norm),
                          _pad_lanes(kr_norm, offset=QK_NOPE))
    o = _causal_attention(qt, k, vt, Q_TILE_MLA)
    w_pad = _pad_head_rows(w_out, MLA_HEADS, V_HEAD)
    half = w_pad.shape[0] // 2
    return _out_proj(o, o, 0, 1, w_pad[:half], w_pad[half:], x2, g1)


def _moe_ffn(x2, mods, norm_g, w_router, router_bias, w_gate, w_up, w_down, ws_gate, ws_up, ws_down):
    sh2, sc2, g2 = mods
    w_r = jnp.pad(w_router.astype(F32), ((0, 0), (0, LANE - N_EXPERTS)))
    h, pos_t, w_t, cnt = _moe_route(x2, norm_g, sc2, sh2, w_r, router_bias.reshape(N_EXPERTS, 1).astype(F32))
    counts = cnt[:, :, 0].astype(jnp.int32).reshape(-1)
    return _moe_experts(counts, x2, h, pos_t, w_t, w_gate.astype(BF16), w_up.astype(BF16),
                        w_down.astype(BF16), ws_gate.astype(BF16), ws_up.astype(BF16), ws_down.astype(BF16), g2)


def kernel(x, c, positions, norm_attn, norm_ffn, w_ada, b_ada, hy_w_in, fox_f_bias, nsa_q_norm, nsa_k_norm, nsa_cmp_pe, nsa_w_cmp, fox_q_norm, fox_k_norm, hy_w_out, mla_w_in, mla_q_a_norm, mla_kv_a_norm, mla_w_uq, mla_w_ukv, mla_qn_norm, mla_kn_norm, mla_qr_norm, mla_kr_norm, mla_w_out, moe_w_router, moe_router_bias, moe_w_gate, moe_w_up, moe_w_down, moe_ws_gate, moe_ws_up, moe_ws_down):
    b, s, d = x.shape
    assert b == 1 and s % KV_TILE == 0 and s >= WINDOW + Q_TILE_NSA
    depth = w_ada.shape[0]
    x2 = x.reshape(s, d).astype(F32)
    posf = positions.reshape(s, 1).astype(F32)
    mod = _ada_mod(c.astype(F32), w_ada.astype(F32), b_ada.astype(F32))

    for layer in range(depth):
        m = [mod[layer, :, k * d:(k + 1) * d] for k in range(6)]
        i = layer // 2
        g_attn = norm_attn[layer].reshape(1, d).astype(F32)
        if layer % 2 == 0:
            x2 = _hybrid_mixer(x2, posf, m[0:3], g_attn, hy_w_in[i], fox_f_bias[i], nsa_q_norm[i],
                               nsa_k_norm[i], nsa_cmp_pe[i], nsa_w_cmp[i], fox_q_norm[i], fox_k_norm[i],
                               hy_w_out[i])
        else:
            x2 = _mla_mixer(x2, posf, m[0:3], g_attn, mla_w_in[i], mla_q_a_norm[i], mla_kv_a_norm[i],
                            mla_w_uq[i], mla_w_ukv[i], mla_qn_norm[i], mla_kn_norm[i], mla_qr_norm[i],
                            mla_kr_norm[i], mla_w_out[i])
        x2 = _moe_ffn(x2, m[3:6], norm_ffn[layer].reshape(1, d).astype(F32), moe_w_router[layer],
                      moe_router_bias[layer], moe_w_gate[layer], moe_w_up[layer], moe_w_down[layer],
                      moe_ws_gate[layer], moe_ws_up[layer], moe_ws_down[layer])
    return x2.reshape(b, s, d)
```

```python
import functools

import numpy as np
import jax
import jax.numpy as jnp
from jax import lax
from jax.experimental import pallas as pl
from jax.experimental.pallas import tpu as pltpu

F32 = jnp.float32
BF16 = jnp.bfloat16
HIGHEST = lax.Precision.HIGHEST

LANE = 128
VMEM_LIMIT_BYTES = 56 * 1024 * 1024

HEAD_DIM = 64
NSA_HEADS = 8
NSA_GROUPS = 2
NSA_HPG = NSA_HEADS // NSA_GROUPS
CMP_LEN = 32
CMP_STRIDE = 16
SLC_BLOCK = 64
SLC_TOPK = 16
WINDOW = 512
FOX_HEADS = 8
MLA_HEADS = 16
Q_LORA = 384
KV_LORA = 256
QK_NOPE = 64
QK_ROPE = 32
V_HEAD = 64
N_EXPERTS = 64
TOP_K = 8
N_GROUPS = 8
TOPK_GROUPS = 4
EXPERT_FF = 256
ROUTED_SCALE = 2.5
ROPE_THETA = 10000.0
EPS = 1e-6
NEG = -1e30
BIG = 1e6

ROW_TILE = 512
PREP_TILE = 512
GATE_ROWS = 16
Q_TILE_NSA = 256
KV_TILE = 512
Q_TILE_FOX = 1024
Q_TILE_MLA = 2048
Q_STRIP = 256
SUPER_BLOCKS = 32
SUM_ROW = 64
V_ROWS = 80
REF_ROW = 104
REF_SLAB = 96
EXP_GUARD = 100.0
SCORE_LOOKAHEAD = 3
SKIP_MARGIN = 160.0
LOG2E = 1.4426950408889634
MOE_CHUNK = 128
MOE_EXPERTS_PER_STEP = 8

HY_Q0 = 0
HY_KV0 = HY_Q0 + NSA_HEADS
HY_F0 = HY_KV0 + 6 * NSA_GROUPS
HY_G0 = HY_F0 + 3 * FOX_HEADS
HY_FF = HY_G0 + NSA_GROUPS
HY_BLOCKS = HY_FF + 2
HY_COL_TILE = 12 * LANE
assert (HY_BLOCKS * LANE) % HY_COL_TILE == 0


def _cparams(*sem):
    return pltpu.CompilerParams(dimension_semantics=sem, vmem_limit_bytes=VMEM_LIMIT_BYTES)


def _lane_iota(shape):
    return lax.broadcasted_iota(jnp.int32, shape, len(shape) - 1)


def _row_iota(shape):
    return lax.broadcasted_iota(jnp.int32, shape, len(shape) - 2)


def _dot_nt(a, b):
    return lax.dot_general(a, b, (((1,), (1,)), ((), ())), preferred_element_type=F32)


def _dot_tn(a, b):
    return lax.dot_general(a, b, (((0,), (0,)), ((), ())), preferred_element_type=F32)


def _ada_kernel(c_ref, w_ref, b_ref, o_ref):
    c = c_ref[...]
    cond = c * jax.nn.sigmoid(c)
    o_ref[0] = jnp.dot(cond, w_ref[0], precision=HIGHEST, preferred_element_type=F32) + b_ref[0]


def _ada_mod(c, w_ada, b_ada):
    depth, d, n = w_ada.shape
    tn = 768
    c8 = jnp.broadcast_to(c.reshape(1, d), (8, d))
    out = pl.pallas_call(
        _ada_kernel,
        grid=(depth, n // tn),
        in_specs=[pl.BlockSpec((8, d), lambda l, j: (0, 0)),
                  pl.BlockSpec((1, d, tn), lambda l, j: (l, 0, j)),
                  pl.BlockSpec((1, 1, tn), lambda l, j: (l, 0, j))],
        out_specs=pl.BlockSpec((1, 8, tn), lambda l, j: (l, 0, j)),
        out_shape=jax.ShapeDtypeStruct((depth, 8, n), F32),
        compiler_params=_cparams("parallel", "parallel"),
        name="ada_mod",
    )(c8, w_ada, b_ada.reshape(depth, 1, n))
    return out[:, 0:1, :]


def _norm_mod(x, g, sc, sh):
    ms = jnp.mean(x * x, axis=-1, keepdims=True)
    return (x * lax.rsqrt(ms + EPS) * g) * (1.0 + sc) + sh


def _nmm_kernel(x_ref, g_ref, sc_ref, sh_ref, w_ref, o_ref, h_scr):
    @pl.when(pl.program_id(1) == 0)
    def _():
        h_scr[...] = _norm_mod(x_ref[...], g_ref[...], sc_ref[...], sh_ref[...]).astype(BF16)

    o_ref[...] = jnp.dot(h_scr[...], w_ref[...], preferred_element_type=F32)


def _norm_mod_matmul(x2, g, sc, sh, w, tn):
    s, d = x2.shape
    n = w.shape[1]
    vec = pl.BlockSpec((1, d), lambda i, j: (0, 0))
    return pl.pallas_call(
        _nmm_kernel,
        grid=(s // ROW_TILE, n // tn),
        in_specs=[pl.BlockSpec((ROW_TILE, d), lambda i, j: (i, 0)), vec, vec, vec,
                  pl.BlockSpec((d, tn), lambda i, j: (0, j))],
        out_specs=pl.BlockSpec((ROW_TILE, tn), lambda i, j: (i, j)),
        out_shape=jax.ShapeDtypeStruct((s, n), F32),
        scratch_shapes=[pltpu.VMEM((ROW_TILE, d), BF16)],
        compiler_params=_cparams("parallel", "arbitrary"),
        name="norm_mod_matmul",
    )(x2, g, sc, sh, w)


def _head_rms(x, gain, n_real):
    ss = jnp.sum(x * x, axis=-1, keepdims=True)
    return x * lax.rsqrt(ss * (1.0 / n_real) + EPS) * gain


def _rope64(x, cos, sin):
    lane = _lane_iota(x.shape)
    rot = jnp.where(lane < 32, -pltpu.roll(x, LANE - 32, 1), pltpu.roll(x, 32, 1))
    return x * cos + rot * sin


def _t_bf16(x, rows=LANE):
    return x.T[:rows].astype(BF16)


def _split3(c):
    hi = c.astype(BF16).astype(F32)
    r1 = c - hi
    mid = r1.astype(BF16).astype(F32)
    lo = (r1 - mid).astype(BF16).astype(F32)
    return hi, mid, lo


def _hy_prep_kernel(p_ref, pos_ref, inv_ref, gq_ref, gk_ref, gfq_ref, gfk_ref, fb_ref,
                    qnt_ref, kct_ref, vct_ref, ks_ref, vst_ref, kw_ref, vwt_ref, gate_ref,
                    fqt_ref, fk_ref, fvt_ref, cedge_ref, carry_ref):
    i = pl.program_id(0)
    tm = PREP_TILE
    shp = (tm, LANE)
    lane = _lane_iota(shp)

    def blk(b):
        return p_ref[:, b * LANE:(b + 1) * LANE]

    ang = pos_ref[...] * inv_ref[...]
    real = lane < HEAD_DIM
    cos = jnp.where(real, jnp.cos(ang), 1.0)
    sin = jnp.where(real, jnp.sin(ang), 0.0)
    gq, gk, gfq, gfk = gq_ref[...], gk_ref[...], gfq_ref[...], gfk_ref[...]
    scale = HEAD_DIM ** -0.5 * LOG2E
    ones_row = lane == SUM_ROW
    ref_ones = jnp.where((lane >= REF_ROW) & (lane < REF_ROW + 3), 1.0, 0.0)

    for h in range(NSA_HEADS):
        q = _rope64(_head_rms(blk(HY_Q0 + h), gq, HEAD_DIM), cos, sin) * scale
        qnt_ref[h * LANE:(h + 1) * LANE, :] = _t_bf16(q)

    row = _row_iota(shp) + i * tm
    onehot = jnp.where(lane - HEAD_DIM == ((row // SLC_BLOCK) % SUPER_BLOCKS), 1.0, 0.0)
    for g in range(NSA_GROUPS):
        def kv(r):
            return blk(HY_KV0 + r * NSA_GROUPS + g)
        sl = slice(g * LANE, (g + 1) * LANE)
        kct_ref[g] = _rope64(kv(0), cos, sin)[:, :HEAD_DIM].astype(BF16)
        vct_ref[g] = kv(1)[:, :HEAD_DIM].astype(BF16)
        ks = _rope64(_head_rms(kv(2), gk, HEAD_DIM), cos, sin)
        ks_ref[:, sl] = (ks + onehot + ref_ones).astype(BF16)
        vst_ref[g, 0] = _t_bf16(jnp.where(ones_row, 1.0, kv(3)), V_ROWS)
        kw_ref[:, sl] = _rope64(_head_rms(kv(4), gk, HEAD_DIM), cos, sin).astype(BF16)
        vwt = _t_bf16(kv(5))
        for cidx in range(tm // LANE):
            vwt_ref[g, cidx] = vwt[:, cidx * LANE:(cidx + 1) * LANE]
        gate_ref[g] = jax.nn.sigmoid(blk(HY_G0 + g)).T[:GATE_ROWS]

    @pl.when(i == 0)
    def _():
        carry_ref[...] = jnp.zeros_like(carry_ref)

    z = blk(HY_FF) + fb_ref[...]
    logf = jnp.minimum(z, 0.0) - jnp.log1p(jnp.exp(-jnp.abs(z)))
    tri = jnp.where(_row_iota((tm, tm)) >= _lane_iota((tm, tm)), 1.0, 0.0).astype(F32)
    cum = jnp.dot(tri, logf, precision=HIGHEST, preferred_element_type=F32) + carry_ref[...]
    carry_ref[...] = cum[tm - 1:tm, :]
    cedge_ref[0] = jnp.concatenate([cum[0:1] * LOG2E, cum[tm - 1:tm] * LOG2E, jnp.zeros((6, LANE), F32)], axis=0)

    for h in range(FOX_HEADS):
        c = jnp.broadcast_to(cum[:, h:h + 1], shp) * LOG2E
        hi, mid, lo = _split3(c)
        fq = _head_rms(blk(HY_F0 + h), gfq, HEAD_DIM) * scale
        fq = jnp.where(real, fq, jnp.where(lane == 64, hi, jnp.where(lane == 65, mid, jnp.where(
            lane == 66, lo, jnp.where(lane < 70, 1.0, 0.0)))))
        fk = _head_rms(blk(HY_F0 + FOX_HEADS + h), gfk, HEAD_DIM)
        fk = jnp.where(real, fk, jnp.where(lane < 67, 1.0, jnp.where(lane == 67, -hi, jnp.where(
            lane == 68, -mid, jnp.where(lane == 69, -lo, ref_ones)))))
        sl = slice(h * LANE, (h + 1) * LANE)
        fqt_ref[sl, :] = _t_bf16(fq)
        fk_ref[:, sl] = fk.astype(BF16)
        fvt_ref[h, 0] = _t_bf16(jnp.where(ones_row, 1.0, blk(HY_F0 + 2 * FOX_HEADS + h)), V_ROWS)


def _feat_major(heads, s, tm):
    return (pl.BlockSpec((heads * LANE, tm), lambda i: (0, i)),
            jax.ShapeDtypeStruct((heads * LANE, s), BF16))


def _value_tiles(heads, s, tm, tk, rows=LANE):
    return (pl.BlockSpec((heads, tm // tk, rows, tk), lambda i: (0, i, 0, 0)),
            jax.ShapeDtypeStruct((heads, s // tk, rows, tk), BF16))


def _pad_feature_rows(o):
    return jnp.concatenate([o, jnp.zeros((LANE - HEAD_DIM, o.shape[1]), o.dtype)], axis=0)


def _hy_prep(proj, posf, inv128, gq, gk, gfq, gfk, fbias):
    s = proj.shape[0]
    tm = PREP_TILE
    assert tm == KV_TILE
    vec = pl.BlockSpec((1, LANE), lambda i: (0, 0))

    def rows(nb):
        return (pl.BlockSpec((tm, nb * LANE), lambda i: (i, 0)), jax.ShapeDtypeStruct((s, nb * LANE), BF16))

    tok = (pl.BlockSpec((NSA_GROUPS, tm, HEAD_DIM), lambda i: (0, i, 0)),
           jax.ShapeDtypeStruct((NSA_GROUPS, s, HEAD_DIM), BF16))
    gate = (pl.BlockSpec((NSA_GROUPS, GATE_ROWS, tm), lambda i: (0, 0, i)),
            jax.ShapeDtypeStruct((NSA_GROUPS, GATE_ROWS, s), F32))
    outs = [_feat_major(NSA_HEADS, s, tm), tok, tok, rows(NSA_GROUPS),
            _value_tiles(NSA_GROUPS, s, tm, KV_TILE, V_ROWS),
            rows(NSA_GROUPS), _value_tiles(NSA_GROUPS, s, tm, LANE), gate,
            _feat_major(FOX_HEADS, s, tm), rows(FOX_HEADS), _value_tiles(FOX_HEADS, s, tm, KV_TILE, V_ROWS),
            (pl.BlockSpec((1, 8, LANE), lambda i: (i, 0, 0)), jax.ShapeDtypeStruct((s // tm, 8, LANE), F32))]
    return pl.pallas_call(
        _hy_prep_kernel,
        grid=(s // tm,),
        in_specs=[pl.BlockSpec((tm, HY_BLOCKS * LANE), lambda i: (i, 0)), pl.BlockSpec((tm, 1), lambda i: (i, 0)),
                  vec, vec, vec, vec, vec, vec],
        out_specs=[o[0] for o in outs],
        out_shape=[o[1] for o in outs],
        scratch_shapes=[pltpu.VMEM((1, LANE), F32)],
        compiler_params=_cparams("arbitrary"),
        name="hybrid_prep",
    )(proj, posf, inv128, gq, gk, gfq, gfk, fbias)


def _compress_kernel(kc_ref, vc_ref, wk_ref, wv_ref, pek_ref, pev_ref, gk_ref, ko_ref, vo_ref):
    half = CMP_STRIDE * HEAD_DIM

    def comp(ch_ref, w_ref, pe_ref):
        ch = ch_ref[0]
        nc = ch.shape[0]
        a = jnp.dot(ch, w_ref[:half], preferred_element_type=F32)
        b = jnp.dot(ch, w_ref[half:], preferred_element_type=F32)
        nxt = pltpu.roll(b, nc - 1, 0)
        pe = jnp.dot(jnp.broadcast_to(pe_ref[...], (8, 2 * half)).astype(BF16), w_ref[...],
                     preferred_element_type=F32)[0:1]
        return a + nxt + pe

    ko_ref[0] = _head_rms(comp(kc_ref, wk_ref, pek_ref), gk_ref[...], HEAD_DIM).astype(BF16)
    vo_ref[0] = comp(vc_ref, wv_ref, pev_ref).T.astype(BF16)


def _compress(kct, vct, w_cmp, cmp_pe, k_norm):
    g, s, _ = kct.shape
    nc = s // CMP_STRIDE
    wide = CMP_STRIDE * HEAD_DIM
    kch = kct.reshape(g, nc, wide)
    vch = vct.reshape(g, nc, wide)
    w_pad = jnp.pad(w_cmp, ((0, 0), (0, 0), (0, LANE - HEAD_DIM))).astype(BF16)
    ch = pl.BlockSpec((1, nc, wide), lambda i: (i, 0, 0))
    wspec = pl.BlockSpec((2 * wide, LANE), lambda i: (0, 0))
    pespec = pl.BlockSpec((1, 2 * wide), lambda i: (0, 0))
    return pl.pallas_call(
        _compress_kernel,
        grid=(g,),
        in_specs=[ch, ch, wspec, wspec, pespec, pespec, pl.BlockSpec((1, LANE), lambda i: (0, 0))],
        out_specs=[pl.BlockSpec((1, nc, LANE), lambda i: (i, 0, 0)),
                   pl.BlockSpec((1, LANE, nc), lambda i: (i, 0, 0))],
        out_shape=[jax.ShapeDtypeStruct((g, nc, LANE), BF16), jax.ShapeDtypeStruct((g, LANE, nc), BF16)],
        compiler_params=_cparams("parallel"),
        name="nsa_compress",
    )(kch, vch, w_pad[0], w_pad[1], cmp_pe[0].reshape(1, 2 * wide).astype(F32),
      cmp_pe[1].reshape(1, 2 * wide).astype(F32), _pad_lanes(k_norm))


def _masked_softmax_t(s, mask):
    s = jnp.where(mask, s, NEG)
    m = jnp.max(s, axis=0, keepdims=True)
    e = jnp.exp2(s - m)
    inv = 1.0 / jnp.maximum(jnp.sum(e, axis=0, keepdims=True), 1e-30)
    return e, jnp.where(m > 0.5 * NEG, inv, 0.0)


def _online_steps(steps, ms, acc_ref):
    ms = list(ms)

    def scores(step):
        k_tile, qa, _, c, mask = step
        s = jnp.dot(k_tile, qa, preferred_element_type=F32)
        if mask is not None:
            s = jnp.where(mask, s, NEG)
        return s, jnp.max(s, axis=0, keepdims=True)

    nxt = scores(steps[0])
    for idx, (_, _, vt, c, _) in enumerate(steps):
        sl = slice(c * Q_STRIP, (c + 1) * Q_STRIP)
        s, s_max = nxt
        if idx + 1 < len(steps):
            nxt = scores(steps[idx + 1])
        m_new = jnp.maximum(ms[c], s_max)
        a = jnp.exp2(ms[c] - m_new)
        p = jnp.exp2((s - m_new).astype(BF16))
        ms[c] = m_new
        acc_ref[:, sl] = a * acc_ref[:, sl] + jnp.dot(vt, p, preferred_element_type=F32)
    return tuple(ms)


def _m_init(n_strips):
    return tuple(jnp.full((1, Q_STRIP), NEG, F32) for _ in range(n_strips))


def _with_ref_rows(qa, m):
    hi, mid, lo = _split3(-m)
    r = _row_iota((LANE - REF_SLAB, Q_STRIP)) + REF_SLAB
    slab = jnp.where(r == REF_ROW, hi, jnp.where(r == REF_ROW + 1, mid, jnp.where(r == REF_ROW + 2, lo, 0.0)))
    return jnp.concatenate([qa[:REF_SLAB], slab.astype(BF16)], axis=0)


def _first_tile_max(k_tile, qa_strips, masks):
    return tuple(jnp.max(jnp.where(mask, jnp.dot(k_tile, qa, preferred_element_type=F32), NEG), axis=0, keepdims=True)
                 for qa, mask in zip(qa_strips, masks))


def _fast_steps(steps, state, acc_ref):
    state = list(state)
    for k, step in enumerate(steps):
        assert all(prev[3] != step[3] for prev in steps[max(k - SCORE_LOOKAHEAD + 1, 0):k])

    def scores(step):
        k_tile, qa, _, c, mask = step
        s = jnp.dot(k_tile, _with_ref_rows(qa, state[c][0]), preferred_element_type=F32)
        if mask is not None:
            s = jnp.where(mask, s, NEG)
        return s

    ahead = [scores(st) for st in steps[:SCORE_LOOKAHEAD]]
    for idx, (_, _, vt, c, _) in enumerate(steps):
        sl = slice(c * Q_STRIP, (c + 1) * Q_STRIP)
        s = ahead.pop(0)
        m, worst = state[c]
        cm = jnp.max(s, axis=0, keepdims=True)
        inc = jnp.maximum(cm, 0.0)
        state[c] = (m + inc, jnp.maximum(worst, cm))
        if idx + SCORE_LOOKAHEAD < len(steps):
            ahead.append(scores(steps[idx + SCORE_LOOKAHEAD]))
        p = jnp.exp2(s).astype(BF16)
        acc_ref[:, sl] = jnp.exp2(-inc) * (acc_ref[:, sl] + jnp.dot(vt, p, preferred_element_type=F32))
    return tuple(state)


def _flat(state):
    return tuple(x for pair in state for x in pair)


def _nest(flat):
    return tuple((flat[2 * c], flat[2 * c + 1]) for c in range(len(flat) // 2))


def _nsa_kernel(qt_ref, kc_ref, vct_ref, ks_ref, vst_ref, kw_ref, vwt_ref, gate_ref, ovt_ref, o_ref,
                qaug_ref, acc_ref, *, n_sel):
    i = pl.program_id(1)
    tq = Q_TILE_NSA
    cols = NSA_HPG * tq
    qs = i * tq
    nc = kc_ref.shape[1]
    nslc = ovt_ref.shape[0]
    n_super = nslc // SUPER_BLOCKS

    qt = jnp.concatenate([qt_ref[h * LANE:(h + 1) * LANE, :] for h in range(NSA_HPG)], axis=1)
    tq_row = qs + (_lane_iota((1, cols)) % tq)

    s = jnp.dot(kc_ref[0], qt, preferred_element_type=F32)
    cmp_end = _row_iota((nc, 1)) * CMP_STRIDE + (CMP_LEN - 1)
    e, inv_l = _masked_softmax_t(s, cmp_end <= tq_row)
    o_cmp = jnp.dot(vct_ref[0], e.astype(BF16), preferred_element_type=F32) * inv_l

    psum = e[:, 0:tq] * inv_l[:, 0:tq]
    for h in range(1, NSA_HPG):
        psum = psum + e[:, h * tq:(h + 1) * tq] * inv_l[:, h * tq:(h + 1) * tq]
    p_hi = psum.astype(BF16)
    p_lo = (psum - p_hi.astype(F32)).astype(BF16)
    ovt = ovt_ref[...]
    imp = (jnp.dot(ovt, p_hi, preferred_element_type=F32)
           + jnp.dot(ovt, p_lo, preferred_element_type=F32))

    jj = _row_iota((nslc, tq))
    tq_blk = qs + _lane_iota((nslc, tq))
    cur = tq_blk // SLC_BLOCK
    forced = (jj == 0) | (jj == cur) | (jj == cur - 1)
    causal_blk = jj * SLC_BLOCK <= tq_blk
    val = jnp.where(forced, imp + BIG, imp)
    val = jnp.where(causal_blk, val, NEG)

    jjf = jj.astype(F32)

    def pick(_, carry):
        val, sel = carry
        mx = jnp.max(val, axis=0, keepdims=True)
        idx = jnp.min(jnp.where(val == mx, jjf, float(nslc)), axis=0, keepdims=True)
        hit = jjf == idx
        return jnp.where(hit, -jnp.inf, val), jnp.where(hit, 1.0, sel)

    _, sel = lax.fori_loop(0, n_sel, pick, (val, jnp.zeros((nslc, tq), F32)))
    bias_t = jnp.where((sel > 0.0) & causal_blk, 0.0, NEG)

    q_rows = qt[:HEAD_DIM].astype(F32)
    spare = jnp.zeros((LANE - HEAD_DIM - SUPER_BLOCKS, cols), F32)
    for st in range(n_super):
        b = bias_t[st * SUPER_BLOCKS:(st + 1) * SUPER_BLOCKS]
        b = jnp.concatenate([b] * NSA_HPG, axis=1)
        qaug_ref[st] = jnp.concatenate([q_rows, b, spare], axis=0).astype(BF16)

    tk = KV_TILE
    per_super = SUPER_BLOCKS * SLC_BLOCK // tk
    j_last = (qs + tq - 1) // tk
    n_strips = cols // Q_STRIP
    strips = [slice(c * Q_STRIP, (c + 1) * Q_STRIP) for c in range(n_strips)]

    def causal_masks(j):
        kpos = j * tk + _row_iota((tk, 1))
        return [kpos <= tq_row[:, sl] for sl in strips]

    def slc_steps(j, masks=None):
        k0 = pl.multiple_of(j * tk, tk)
        k_tile, vt, st = ks_ref[pl.ds(k0, tk), :], vst_ref[0, j], j // per_super
        return [(k_tile, qaug_ref[st, :, strips[c]], vt, c, None if masks is None else masks[c])
                for c in range(n_strips)]

    group = 4

    def grouped(jj, flat):
        steps = [st for t in range(group) for st in slc_steps(group * jj + t)]
        return _flat(_fast_steps(steps, _nest(flat), acc_ref))

    def single(j, flat):
        return _flat(_fast_steps(slc_steps(j), _nest(flat), acc_ref))

    acc_ref[...] = jnp.zeros(acc_ref.shape, F32)
    head_rows = 16
    head_masks = [_row_iota((head_rows, 1)) <= tq_row[:, sl] for sl in strips]
    m0 = _first_tile_max(ks_ref[0:head_rows, :], [qaug_ref[0, :, sl] for sl in strips], head_masks)
    n_groups = j_last // group
    flat = lax.fori_loop(0, n_groups, grouped, _flat(tuple((m, jnp.zeros_like(m)) for m in m0)))
    flat = lax.fori_loop(group * n_groups, j_last, single, flat)
    state = _fast_steps(slc_steps(j_last, causal_masks(j_last)), _nest(flat), acc_ref)
    worst = jnp.max(jnp.concatenate([w for _, w in state], axis=1))

    @pl.when(worst > EXP_GUARD)
    def _():
        acc_ref[...] = jnp.zeros(acc_ref.shape, F32)
        lax.fori_loop(0, j_last + 1, lambda j, ms: _online_steps(slc_steps(j, causal_masks(j)), ms, acc_ref),
                      _m_init(n_strips))

    o_slc = acc_ref[:HEAD_DIM] / jnp.maximum(acc_ref[SUM_ROW:SUM_ROW + 1], 1e-30)

    wlen = WINDOW + tq
    ws = pl.multiple_of(jnp.maximum(qs - WINDOW, 0), tq)
    s = jnp.dot(kw_ref[pl.ds(ws, wlen), :], qt, preferred_element_type=F32)
    dist = tq_row - (ws + _row_iota((wlen, 1)))
    e, inv_l = _masked_softmax_t(s, (dist >= 0) & (dist < WINDOW))
    e = e.astype(BF16)
    wb = ws // LANE
    o_win = jnp.zeros((LANE, cols), F32)
    for c in range(wlen // LANE):
        o_win = o_win + jnp.dot(vwt_ref[0, wb + c], e[c * LANE:(c + 1) * LANE], preferred_element_type=F32)
    o_win = o_win * inv_l

    gate = gate_ref[0]
    for h in range(NSA_HPG):
        sl = slice(h * tq, (h + 1) * tq)
        o = (gate[3 * h:3 * h + 1] * o_cmp[:HEAD_DIM, sl] + gate[3 * h + 1:3 * h + 2] * o_slc[:, sl]
             + gate[3 * h + 2:3 * h + 3] * o_win[:HEAD_DIM, sl])
        o_ref[:, h * LANE:(h + 1) * LANE] = _pad_feature_rows(o).T.astype(BF16)


def _overlap_t(s, nslc_pad):
    nc = s // CMP_STRIDE
    cmp_start = np.arange(nc) * CMP_STRIDE
    slc_start = np.arange(nslc_pad) * SLC_BLOCK
    ov = np.clip(np.minimum(cmp_start[:, None] + CMP_LEN, slc_start[None, :] + SLC_BLOCK)
                 - np.maximum(cmp_start[:, None], slc_start[None, :]), 0, None) / CMP_STRIDE
    ov[nc - CMP_LEN // CMP_STRIDE + 1:, :] = 0.0
    ov[:, s // SLC_BLOCK:] = 0.0
    return jnp.asarray(ov.T, BF16)


def _nsa_attention(qnt, kc, vct, ks, vst, kw, vwt, gates):
    s = qnt.shape[1]
    nc = s // CMP_STRIDE
    n_slc = s // SLC_BLOCK
    nslc_pad = -(-n_slc // LANE) * LANE
    tq = Q_TILE_NSA
    cols = NSA_HPG * tq
    once = pl.Buffered(1)
    res = pl.BlockSpec((s, LANE), lambda g, i: (0, g), pipeline_mode=once)
    return pl.pallas_call(
        functools.partial(_nsa_kernel, n_sel=min(SLC_TOPK, n_slc)),
        grid=(NSA_GROUPS, s // tq),
        in_specs=[pl.BlockSpec((NSA_HPG * LANE, tq), lambda g, i: (g, i)),
                  pl.BlockSpec((1, nc, LANE), lambda g, i: (g, 0, 0), pipeline_mode=once),
                  pl.BlockSpec((1, LANE, nc), lambda g, i: (g, 0, 0), pipeline_mode=once),
                  res, pl.BlockSpec((1, s // KV_TILE, V_ROWS, KV_TILE), lambda g, i: (g, 0, 0, 0),
                                    pipeline_mode=once),
                  res, pl.BlockSpec((1, s // LANE, LANE, LANE), lambda g, i: (g, 0, 0, 0), pipeline_mode=once),
                  pl.BlockSpec((1, GATE_ROWS, tq), lambda g, i: (g, 0, i)),
                  pl.BlockSpec((nslc_pad, nc), lambda g, i: (0, 0), pipeline_mode=once)],
        out_specs=pl.BlockSpec((tq, NSA_HPG * LANE), lambda g, i: (i, g)),
        out_shape=jax.ShapeDtypeStruct((s, NSA_HEADS * LANE), BF16),
        scratch_shapes=[pltpu.VMEM((nslc_pad // SUPER_BLOCKS, LANE, cols), BF16),
                        pltpu.VMEM((V_ROWS, cols), F32)],
        compiler_params=_cparams("parallel", "arbitrary"),
        name="nsa_attention",
    )(qnt, kc, vct, ks, vst, kw, vwt, gates, _overlap_t(s, nslc_pad))


def _flash_kernel(cfirst_ref, clast_ref, slack_ref, qt_ref, k_ref, vt_ref, o_ref, acc_ref, *, tq):
    h = pl.program_id(0)
    i = pl.program_id(1)
    tk = KV_TILE
    n_tiles = k_ref.shape[0] // tk
    acc_ref[...] = jnp.zeros(acc_ref.shape, F32)

    n_strips = tq // Q_STRIP
    per_q = tq // tk
    qas = [qt_ref[:, c * Q_STRIP:(c + 1) * Q_STRIP] for c in range(n_strips)]

    def tile_steps(j, d=None):
        k0 = pl.multiple_of(j * tk, tk)
        k_tile, vt = k_ref[pl.ds(k0, tk), :], vt_ref[0, j]
        steps = []
        for c in range(n_strips):
            mask = None
            if d is not None:
                if d * tk > (c + 1) * Q_STRIP - 1:
                    continue
                if (d + 1) * tk - 1 > c * Q_STRIP:
                    shp = (tk, Q_STRIP)
                    mask = _row_iota(shp) + d * tk <= _lane_iota(shp) + c * Q_STRIP
            steps.append((k_tile, qas[c], vt, c, mask))
        return steps

    def any_tile_masks(j):
        shp = (tk, Q_STRIP)
        return [_row_iota(shp) + j * tk <= _lane_iota(shp) + (i * tq + c * Q_STRIP) for c in range(n_strips)]

    def below(t, flat):
        jj = i - 1 - t
        bound = (slack_ref[0] + cfirst_ref[h * n_tiles + i * per_q]
                 - clast_ref[h * n_tiles + jj * per_q + per_q - 1])

        def run(flat):
            steps = [st for u in range(per_q) for st in tile_steps(jj * per_q + (per_q - 1 - u))]
            return _flat(_fast_steps(steps, _nest(flat), acc_ref))

        return lax.cond(bound >= -SKIP_MARGIN, run, lambda flat: flat, flat)

    diag0 = i * per_q
    own = [(c * Q_STRIP) // tk for c in range(n_strips)]
    m0 = tuple(_first_tile_max(k_ref[pl.ds(pl.multiple_of((diag0 + own[c]) * tk, tk), tk), :], [qas[c]],
                               [any_tile_masks(diag0 + own[c])[c]])[0] for c in range(n_strips))
    steps = [st for d in reversed(range(per_q)) for st in tile_steps(diag0 + d, d)]
    state = _fast_steps(steps, tuple((m, jnp.zeros_like(m)) for m in m0), acc_ref)
    state = _nest(lax.fori_loop(0, i, below, _flat(state)))
    worst = jnp.max(jnp.concatenate([w for _, w in state], axis=1))

    @pl.when(worst > EXP_GUARD)
    def _():
        acc_ref[...] = jnp.zeros(acc_ref.shape, F32)

        def exact(j, ms):
            k0 = pl.multiple_of(j * tk, tk)
            k_tile, vt, masks = k_ref[pl.ds(k0, tk), :], vt_ref[0, j], any_tile_masks(j)
            return _online_steps([(k_tile, qas[c], vt, c, masks[c]) for c in range(n_strips)], ms, acc_ref)

        lax.fori_loop(0, (i + 1) * per_q, exact, _m_init(n_strips))

    o = _pad_feature_rows(acc_ref[:HEAD_DIM] / acc_ref[SUM_ROW:SUM_ROW + 1])
    for c0 in range(0, tq, LANE):
        o_ref[c0:c0 + LANE, :] = o[:, c0:c0 + LANE].T.astype(BF16)


def _causal_attention(qt, k, vt, tq, bias_edges=None, slack=None):
    s, width = k.shape
    heads = width // LANE
    tq = min(tq, s)
    assert tq % KV_TILE == 0 and s % tq == 0
    n_tiles = s // KV_TILE
    if bias_edges is None:
        first = last = jnp.zeros((heads * n_tiles,), F32)
        slack = jnp.full((1,), -NEG, F32)
    else:
        first, last = (e.reshape(heads * n_tiles).astype(F32) for e in bias_edges)
    grid_spec = pltpu.PrefetchScalarGridSpec(
        num_scalar_prefetch=3,
        grid=(heads, s // tq),
        in_specs=[pl.BlockSpec((LANE, tq), lambda h, i, *_: (h, i)),
                  pl.BlockSpec((s, LANE), lambda h, i, *_: (0, h)),
                  pl.BlockSpec((1, n_tiles, V_ROWS, KV_TILE), lambda h, i, *_: (h, 0, 0, 0))],
        out_specs=pl.BlockSpec((tq, LANE), lambda h, i, *_: (i, h)),
        scratch_shapes=[pltpu.VMEM((V_ROWS, tq), F32)],
    )
    return pl.pallas_call(
        functools.partial(_flash_kernel, tq=tq),
        grid_spec=grid_spec,
        out_shape=jax.ShapeDtypeStruct((s, width), BF16),
        compiler_params=_cparams("parallel", "arbitrary"),
        name="causal_attention",
    )(first, last, slack.astype(F32), qt, k, vt)


def _out_proj_kernel(oa_ref, ob_ref, wa_ref, wb_ref, x_ref, g_ref, o_ref):
    y = jnp.dot(oa_ref[...], wa_ref[...], preferred_element_type=F32)
    y = y + jnp.dot(ob_ref[...], wb_ref[...], preferred_element_type=F32)
    o_ref[...] = x_ref[...] + g_ref[...] * y


def _out_proj(oa, ob, cola, colb, wa, wb, x2, gate):
    s, d = x2.shape
    ka = wa.shape[0]
    tm = ROW_TILE
    return pl.pallas_call(
        _out_proj_kernel,
        grid=(s // tm,),
        in_specs=[pl.BlockSpec((tm, ka), lambda i: (i, cola)), pl.BlockSpec((tm, ka), lambda i: (i, colb)),
                  pl.BlockSpec((ka, d), lambda i: (0, 0)), pl.BlockSpec((ka, d), lambda i: (0, 0)),
                  pl.BlockSpec((tm, d), lambda i: (i, 0)), pl.BlockSpec((1, d), lambda i: (0, 0))],
        out_specs=pl.BlockSpec((tm, d), lambda i: (i, 0)),
        out_shape=jax.ShapeDtypeStruct((s, d), F32),
        compiler_params=_cparams("parallel"),
        name="out_proj",
    )(oa, ob, wa, wb, x2, gate)


def _mla_prep_kernel(p_ref, pos_ref, inv_ref, gqa_ref, gkva_ref, wuq_ref, wuk_ref, wuv_ref,
                     gq_ref, gk_ref, gkr_ref, qt_ref, k_ref, vt_ref):
    shp = (PREP_TILE, LANE)
    lane = _lane_iota(shp)
    nope = lane < QK_NOPE
    rope = (lane >= QK_NOPE) & (lane < QK_NOPE + QK_ROPE)
    ref_ones = jnp.where((lane >= REF_ROW) & (lane < REF_ROW + 3), 1.0, 0.0)
    ang = pos_ref[...] * inv_ref[...]
    cos = jnp.where(rope, jnp.cos(ang), 1.0)
    sin = jnp.where(rope, jnp.sin(ang), 0.0)

    def rope32(x):
        half = QK_ROPE // 2
        rot = jnp.where(lane < QK_NOPE + half, -pltpu.roll(x, LANE - half, 1), pltpu.roll(x, half, 1))
        return x * cos + rot * sin

    def low_rank_norm(x, g):
        ms = jnp.mean(x * x, axis=-1, keepdims=True)
        return (x * lax.rsqrt(ms + EPS) * g).astype(BF16)

    nq = Q_LORA // LANE
    cq = low_rank_norm(p_ref[:, :Q_LORA], gqa_ref[...])
    ckv = low_rank_norm(p_ref[:, Q_LORA:Q_LORA + KV_LORA], gkva_ref[...])
    kr = p_ref[:, (nq + KV_LORA // LANE) * LANE:(nq + KV_LORA // LANE + 1) * LANE]
    k_rope = rope32(_head_rms(kr, gkr_ref[...], QK_ROPE))

    gq, gk = gq_ref[...], gk_ref[...]
    scale = (QK_NOPE + QK_ROPE) ** -0.5 * LOG2E
    pair = 2 * LANE
    for hp in range(MLA_HEADS // 2):
        cols = slice(hp * pair, (hp + 1) * pair)
        q2 = jnp.dot(cq, wuq_ref[:, cols], preferred_element_type=F32)
        k2 = jnp.dot(ckv, wuk_ref[:, cols], preferred_element_type=F32)
        v2 = jnp.dot(ckv, wuv_ref[:, cols], preferred_element_type=F32)
        for sub in range(2):
            head = 2 * hp + sub
            sl = slice(head * LANE, (head + 1) * LANE)
            half = slice(sub * LANE, (sub + 1) * LANE)
            x = q2[:, half]
            ss_n = jnp.sum(jnp.where(nope, x * x, 0.0), axis=-1, keepdims=True)
            ss_r = jnp.sum(jnp.where(rope, x * x, 0.0), axis=-1, keepdims=True)
            inv_rms = jnp.where(nope, lax.rsqrt(ss_n * (1.0 / QK_NOPE) + EPS),
                                lax.rsqrt(ss_r * (1.0 / QK_ROPE) + EPS))
            qt_ref[sl, :] = _t_bf16(rope32(x * inv_rms * gq) * scale)
            kn = _head_rms(k2[:, half], gk, QK_NOPE)
            k_ref[:, sl] = (kn + k_rope + ref_ones).astype(BF16)
            vt_ref[head, 0] = _t_bf16(jnp.where(lane == SUM_ROW, 1.0, v2[:, half]), V_ROWS)


def _mla_prep(proj, posf, inv128, gqa, gkva, wuq, wuk, wuv, gq, gk, gkr):
    s, n = proj.shape
    tm = PREP_TILE
    assert tm == KV_TILE

    def full(a):
        return pl.BlockSpec(a.shape, lambda i: (0, 0))

    outs = [_feat_major(MLA_HEADS, s, tm),
            (pl.BlockSpec((tm, MLA_HEADS * LANE), lambda i: (i, 0)),
             jax.ShapeDtypeStruct((s, MLA_HEADS * LANE), BF16)),
            _value_tiles(MLA_HEADS, s, tm, KV_TILE, V_ROWS)]
    args = (inv128, gqa, gkva, wuq, wuk, wuv, gq, gk, gkr)
    return pl.pallas_call(
        _mla_prep_kernel,
        grid=(s // tm,),
        in_specs=[pl.BlockSpec((tm, n), lambda i: (i, 0)), pl.BlockSpec((tm, 1), lambda i: (i, 0))]
                 + [full(a) for a in args],
        out_specs=[o[0] for o in outs],
        out_shape=[o[1] for o in outs],
        compiler_params=_cparams("parallel"),
        name="mla_prep",
    )(proj, posf, *args)


def _rank_lt(v, k):
    n = v.shape[0]
    row = _row_iota(v.shape)
    rank = jnp.zeros(v.shape, F32)
    for b in range(n):
        vb = v[b:b + 1, :]
        rank = rank + jnp.where((vb > v) | ((vb == v) & (row > b)), 1.0, 0.0)
    return rank < k


def _top_rows(v, k):
    rowf = _row_iota(v.shape).astype(F32)
    chosen = jnp.zeros(v.shape, F32)
    for _ in range(k):
        mx = jnp.max(v, axis=0, keepdims=True)
        idx = jnp.min(jnp.where(v == mx, rowf, float(v.shape[0])), axis=0, keepdims=True)
        hit = rowf == idx
        chosen = jnp.where(hit, 1.0, chosen)
        v = jnp.where(hit, -jnp.inf, v)
    return chosen > 0.0


def _moe_route_kernel(x_ref, g_ref, sc_ref, sh_ref, wr_ref, rb_ref, h_ref, pos_ref, wt_ref, cnt_ref):
    tm = ROW_TILE
    h = _norm_mod(x_ref[...], g_ref[...], sc_ref[...], sh_ref[...])
    h_ref[...] = h.astype(BF16)
    logits = jnp.dot(h, wr_ref[...], precision=HIGHEST, preferred_element_type=F32)
    lt = logits.T[:N_EXPERTS]
    scores = jax.nn.sigmoid(lt)
    sel = scores + rb_ref[...]

    per = N_EXPERTS // N_GROUPS
    grp = sel.reshape(N_GROUPS, per, tm)
    sub = lax.broadcasted_iota(jnp.int32, grp.shape, 1)
    m1 = jnp.max(grp, axis=1, keepdims=True)
    first = jnp.min(jnp.where(grp == m1, sub, per), axis=1, keepdims=True)
    m2 = jnp.max(jnp.where(sub == first, -jnp.inf, grp), axis=1, keepdims=True)
    gscore = (m1 + m2).reshape(N_GROUPS, tm)
    gmask = _rank_lt(gscore, TOPK_GROUPS)
    emask = jnp.broadcast_to(gmask.reshape(N_GROUPS, 1, tm), grp.shape).reshape(N_EXPERTS, tm)
    chosen = _top_rows(jnp.where(emask, sel, NEG), TOP_K)

    w = jnp.where(chosen, scores, 0.0)
    wt_ref[...] = w / jnp.sum(w, axis=0, keepdims=True) * ROUTED_SCALE

    upper = jnp.where(_row_iota((tm, tm)) <= _lane_iota((tm, tm)), 1.0, 0.0).astype(BF16)
    incl = jnp.dot(jnp.where(chosen, 1.0, 0.0).astype(BF16), upper, preferred_element_type=F32)
    pos_ref[...] = jnp.where(chosen, incl - 1.0, -1.0)
    cnt_ref[0] = jnp.broadcast_to(incl[:, tm - 1:tm], (N_EXPERTS, LANE))


def _moe_route(x2, g, sc, sh, w_router_pad, router_bias_col):
    s, d = x2.shape
    tm = ROW_TILE
    vec = pl.BlockSpec((1, d), lambda i: (0, 0))
    et = pl.BlockSpec((N_EXPERTS, tm), lambda i: (0, i))
    return pl.pallas_call(
        _moe_route_kernel,
        grid=(s // tm,),
        in_specs=[pl.BlockSpec((tm, d), lambda i: (i, 0)), vec, vec, vec,
                  pl.BlockSpec((d, LANE), lambda i: (0, 0)),
                  pl.BlockSpec((N_EXPERTS, 1), lambda i: (0, 0))],
        out_specs=[pl.BlockSpec((tm, d), lambda i: (i, 0)), et, et,
                   pl.BlockSpec((1, N_EXPERTS, LANE), lambda i: (i, 0, 0))],
        out_shape=[jax.ShapeDtypeStruct((s, d), BF16), jax.ShapeDtypeStruct((N_EXPERTS, s), F32),
                   jax.ShapeDtypeStruct((N_EXPERTS, s), F32),
                   jax.ShapeDtypeStruct((s // tm, N_EXPERTS, LANE), F32)],
        compiler_params=_cparams("parallel"),
        name="moe_route",
    )(x2, g, sc, sh, w_router_pad, router_bias_col)


def _moe_kernel(cnt_ref, x_ref, h_ref, pos_ref, wt_ref, wg_ref, wu_ref, wd_ref, sg_ref, su_ref, sd_ref,
                g2_ref, o_ref, acc_ref):
    i = pl.program_id(0)
    e = pl.program_id(1)
    tm = ROW_TILE
    r = MOE_CHUNK

    @pl.when(e == 0)
    def _():
        h = h_ref[...]
        a = jnp.dot(h, sg_ref[...], preferred_element_type=F32)
        a = a * jax.nn.sigmoid(a) * jnp.dot(h, su_ref[...], preferred_element_type=F32)
        acc_ref[...] = jnp.dot(a.astype(BF16), sd_ref[...], preferred_element_type=F32)

    first = e * MOE_EXPERTS_PER_STEP
    n = cnt_ref[i * N_EXPERTS + first]
    for k in range(1, MOE_EXPERTS_PER_STEP):
        n = jnp.maximum(n, cnt_ref[i * N_EXPERTS + first + k])
    prows = [pos_ref[pl.ds(first + k, 1), :] for k in range(MOE_EXPERTS_PER_STEP)]
    wrows = [wt_ref[pl.ds(first + k, 1), :] for k in range(MOE_EXPERTS_PER_STEP)]

    def chunk(c, _):
        slot = (_row_iota((r, tm)) + c * r).astype(F32)
        hits = [prow == slot for prow in prows]
        onehot = jnp.concatenate([jnp.where(hit, 1.0, 0.0).astype(BF16) for hit in hits], axis=0)
        xg = jnp.dot(onehot, h_ref[...], preferred_element_type=F32).astype(BF16)
        ys = []
        for k in range(MOE_EXPERTS_PER_STEP):
            xk = xg[k * r:(k + 1) * r]
            a = jnp.dot(xk, wg_ref[k], preferred_element_type=F32)
            a = a * jax.nn.sigmoid(a) * jnp.dot(xk, wu_ref[k], preferred_element_type=F32)
            y = jnp.dot(a.astype(BF16), wd_ref[k], preferred_element_type=F32)
            wr = jnp.sum(jnp.where(hits[k], wrows[k], 0.0), axis=-1, keepdims=True)
            ys.append((y * wr).astype(BF16))
        acc_ref[...] += _dot_tn(onehot, jnp.concatenate(ys, axis=0))
        return 0

    lax.fori_loop(0, (n + r - 1) // r, chunk, 0)

    @pl.when(e == N_EXPERTS // MOE_EXPERTS_PER_STEP - 1)
    def _():
        o_ref[...] = x_ref[...] + g2_ref[...] * acc_ref[...]


def _moe_experts(counts, x2, h, pos_t, w_t, wg, wu, wd, sg, su, sd, g2):
    s, d = x2.shape
    tm = ROW_TILE
    ff = wg.shape[2]
    tile = pl.BlockSpec((tm, d), lambda i, e, c: (i, 0))
    et = pl.BlockSpec((N_EXPERTS, tm), lambda i, e, c: (0, i))

    def const(a):
        return pl.BlockSpec(a.shape, lambda i, e, c: (0,) * a.ndim)

    per = MOE_EXPERTS_PER_STEP
    grid_spec = pltpu.PrefetchScalarGridSpec(
        num_scalar_prefetch=1,
        grid=(s // tm, N_EXPERTS // per),
        in_specs=[tile, tile, et, et,
                  pl.BlockSpec((per, d, ff), lambda i, e, c: (e, 0, 0)),
                  pl.BlockSpec((per, d, ff), lambda i, e, c: (e, 0, 0)),
                  pl.BlockSpec((per, ff, d), lambda i, e, c: (e, 0, 0)),
                  const(sg), const(su), const(sd), const(g2)],
        out_specs=tile,
        scratch_shapes=[pltpu.VMEM((tm, d), F32)],
    )
    return pl.pallas_call(
        _moe_kernel,
        grid_spec=grid_spec,
        out_shape=jax.ShapeDtypeStruct((s, d), F32),
        compiler_params=_cparams("parallel", "arbitrary"),
        name="moe_experts",
    )(counts, x2, h, pos_t, w_t, wg, wu, wd, sg, su, sd, g2)


def _pad_lanes(v, width=LANE, offset=0):
    out = jnp.zeros((1, width), F32)
    return out.at[0, offset:offset + v.shape[0]].set(v.astype(F32))


def _head_cols(w, n_heads, dim):
    d = w.shape[0]
    w3 = w.reshape(d, n_heads, dim)
    return jnp.pad(w3, ((0, 0), (0, 0), (0, LANE - dim))).reshape(d, n_heads * LANE)


def _hybrid_w_in(w_in):
    d = w_in.shape[0]
    nq = NSA_HEADS * HEAD_DIM
    nkv = 6 * NSA_GROUPS * HEAD_DIM
    ng = 3 * NSA_HEADS
    nf = 3 * FOX_HEADS * HEAD_DIM
    c0, c1, c2, c3 = nq, nq + nkv, nq + nkv + ng, nq + nkv + ng + nf
    gates = w_in[:, c1:c2].reshape(d, NSA_GROUPS, 3 * NSA_HPG)
    gates = jnp.pad(gates, ((0, 0), (0, 0), (0, LANE - 3 * NSA_HPG))).reshape(d, NSA_GROUPS * LANE)
    ff = jnp.pad(w_in[:, c3:], ((0, 0), (0, 2 * LANE - FOX_HEADS)))
    return jnp.concatenate([
        _head_cols(w_in[:, :c0], NSA_HEADS, HEAD_DIM),
        _head_cols(w_in[:, c0:c1], 6 * NSA_GROUPS, HEAD_DIM),
        _head_cols(w_in[:, c2:c3], 3 * FOX_HEADS, HEAD_DIM),
        gates, ff], axis=1).astype(BF16)


def _pad_head_rows(w, n_heads, dim):
    d = w.shape[1]
    w3 = w.reshape(n_heads, dim, d)
    return jnp.pad(w3, ((0, 0), (0, LANE - dim), (0, 0))).reshape(n_heads * LANE, d).astype(BF16)


def _rope_inv(dim, offset):
    inv = ROPE_THETA ** (-jnp.arange(0, dim, 2, dtype=F32) / dim)
    return _pad_lanes(jnp.concatenate([inv, inv]), offset=offset)


def _hybrid_mixer(x2, posf, mods, norm_g, w_in, fox_f_bias, nsa_q_norm, nsa_k_norm, nsa_cmp_pe, nsa_w_cmp,
                  fox_q_norm, fox_k_norm, w_out):
    sh1, sc1, g1 = mods
    proj = _norm_mod_matmul(x2, norm_g, sc1, sh1, _hybrid_w_in(w_in), tn=HY_COL_TILE)
    (qnt, kct, vct, ks, vst, kw, vwt, gates, fqt, fk, fvt, cedge) = _hy_prep(
        proj, posf, _rope_inv(HEAD_DIM, 0), _pad_lanes(nsa_q_norm), _pad_lanes(nsa_k_norm),
        _pad_lanes(fox_q_norm), _pad_lanes(fox_k_norm), _pad_lanes(fox_f_bias))
    kc, vc_t = _compress(kct, vct, nsa_w_cmp, nsa_cmp_pe, nsa_k_norm)
    o_a = _nsa_attention(qnt, kc, vc_t, ks, vst, kw, vwt, gates)
    slack = (2.0 * HEAD_DIM ** 0.5 * LOG2E) * jnp.max(jnp.abs(fox_q_norm)) * jnp.max(jnp.abs(fox_k_norm))
    edges = (cedge[:, 0, :FOX_HEADS].T, cedge[:, 1, :FOX_HEADS].T)
    o_b = _causal_attention(fqt, fk, fvt, Q_TILE_FOX, edges, slack.reshape(1))
    half = NSA_HEADS * HEAD_DIM
    wa = _pad_head_rows(w_out[:half], NSA_HEADS, HEAD_DIM)
    wb = _pad_head_rows(w_out[half:], FOX_HEADS, HEAD_DIM)
    return _out_proj(o_a, o_b, 0, 0, wa, wb, x2, g1)


def _mla_mixer(x2, posf, mods, norm_g, w_in, q_a_norm, kv_a_norm, w_uq, w_ukv, qn_norm, kn_norm, qr_norm,
               kr_norm, w_out):
    sh1, sc1, g1 = mods
    d = x2.shape[1]
    w_kr = jnp.zeros((d, LANE), F32).at[:, QK_NOPE:QK_NOPE + QK_ROPE].set(w_in[:, Q_LORA + KV_LORA:])
    w_in_p = jnp.concatenate([w_in[:, :Q_LORA + KV_LORA], w_kr], axis=1).astype(BF16)
    proj = _norm_mod_matmul(x2, norm_g, sc1, sh1, w_in_p, tn=w_in_p.shape[1])
    hq = QK_NOPE + QK_ROPE
    wuq = _head_cols(w_uq, MLA_HEADS, hq).astype(BF16)
    wkv3 = w_ukv.reshape(KV_LORA, MLA_HEADS, QK_NOPE + V_HEAD)
    wuk = _head_cols(wkv3[:, :, :QK_NOPE].reshape(KV_LORA, -1), MLA_HEADS, QK_NOPE).astype(BF16)
    wuv = _head_cols(wkv3[:, :, QK_NOPE:].reshape(KV_LORA, -1), MLA_HEADS, V_HEAD).astype(BF16)
    gq = _pad_lanes(jnp.concatenate([qn_norm, qr_norm]))
    qt, k, vt = _mla_prep(proj, posf, _rope_inv(QK_ROPE, QK_NOPE), q_a_norm.reshape(1, -1).astype(F32),
                          kv_a_norm.reshape(1, -1).astype(F32), wuq, wuk, wuv, gq, _pad_lanes(kn_norm),
                          _pad_lanes(kr_norm, offset=QK_NOPE))
    o = _causal_attention(qt, k, vt, Q_TILE_MLA)
    w_pad = _pad_head_rows(w_out, MLA_HEADS, V_HEAD)
    half = w_pad.shape[0] // 2
    return _out_proj(o, o, 0, 1, w_pad[:half], w_pad[half:], x2, g1)


def _moe_ffn(x2, mods, norm_g, w_router, router_bias, w_gate, w_up, w_down, ws_gate, ws_up, ws_down):
    sh2, sc2, g2 = mods
    w_r = jnp.pad(w_router.astype(F32), ((0, 0), (0, LANE - N_EXPERTS)))
    h, pos_t, w_t, cnt = _moe_route(x2, norm_g, sc2, sh2, w_r, router_bias.reshape(N_EXPERTS, 1).astype(F32))
    counts = cnt[:, :, 0].astype(jnp.int32).reshape(-1)
    return _moe_experts(counts, x2, h, pos_t, w_t, w_gate.astype(BF16), w_up.astype(BF16),
                        w_down.astype(BF16), ws_gate.astype(BF16), ws_up.astype(BF16), ws_down.astype(BF16), g2)


def kernel(x, c, positions, norm_attn, norm_ffn, w_ada, b_ada, hy_w_in, fox_f_bias, nsa_q_norm, nsa_k_norm, nsa_cmp_pe, nsa_w_cmp, fox_q_norm, fox_k_norm, hy_w_out, mla_w_in, mla_q_a_norm, mla_kv_a_norm, mla_w_uq, mla_w_ukv, mla_qn_norm, mla_kn_norm, mla_qr_norm, mla_kr_norm, mla_w_out, moe_w_router, moe_router_bias, moe_w_gate, moe_w_up, moe_w_down, moe_ws_gate, moe_ws_up, moe_ws_down):
    b, s, d = x.shape
    assert b == 1 and s % KV_TILE == 0 and s >= WINDOW + Q_TILE_NSA
    depth = w_ada.shape[0]
    x2 = x.reshape(s, d).astype(F32)
    posf = positions.reshape(s, 1).astype(F32)
    mod = _ada_mod(c.astype(F32), w_ada.astype(F32), b_ada.astype(F32))

    for layer in range(depth):
        m = [mod[layer, :, k * d:(k + 1) * d] for k in range(6)]
        i = layer // 2
        g_attn = norm_attn[layer].reshape(1, d).astype(F32)
        if layer % 2 == 0:
            x2 = _hybrid_mixer(x2, posf, m[0:3], g_attn, hy_w_in[i], fox_f_bias[i], nsa_q_norm[i],
                               nsa_k_norm[i], nsa_cmp_pe[i], nsa_w_cmp[i], fox_q_norm[i], fox_k_norm[i],
                               hy_w_out[i])
        else:
            x2 = _mla_mixer(x2, posf, m[0:3], g_attn, mla_w_in[i], mla_q_a_norm[i], mla_kv_a_norm[i],
                            mla_w_uq[i], mla_w_ukv[i], mla_qn_norm[i], mla_kn_norm[i], mla_qr_norm[i],
                            mla_kr_norm[i], mla_w_out[i])
        x2 = _moe_ffn(x2, m[3:6], norm_ffn[layer].reshape(1, d).astype(F32), moe_w_router[layer],
                      moe_router_bias[layer], moe_w_gate[layer], moe_w_up[layer], moe_w_down[layer],
                      moe_ws_gate[layer], moe_ws_up[layer], moe_ws_down[layer])
    return x2.reshape(b, s, d)
```

```python
import functools

import numpy as np
import jax
import jax.numpy as jnp
from jax import lax
from jax.experimental import pallas as pl
from jax.experimental.pallas import tpu as pltpu

F32 = jnp.float32
BF16 = jnp.bfloat16
HIGHEST = lax.Precision.HIGHEST

LANE = 128
VMEM_LIMIT_BYTES = 56 * 1024 * 1024

HEAD_DIM = 64
NSA_HEADS = 8
NSA_GROUPS = 2
NSA_HPG = NSA_HEADS // NSA_GROUPS
CMP_LEN = 32
CMP_STRIDE = 16
SLC_BLOCK = 64
SLC_TOPK = 16
WINDOW = 512
FOX_HEADS = 8
MLA_HEADS = 16
Q_LORA = 384
KV_LORA = 256
QK_NOPE = 64
QK_ROPE = 32
V_HEAD = 64
N_EXPERTS = 64
TOP_K = 8
N_GROUPS = 8
TOPK_GROUPS = 4
EXPERT_FF = 256
ROUTED_SCALE = 2.5
ROPE_THETA = 10000.0
EPS = 1e-6
NEG = -1e30
BIG = 1e6

ROW_TILE = 512
PREP_TILE = 512
GATE_ROWS = 16
Q_TILE_NSA = 256
KV_TILE = 512
Q_TILE_FOX = 1024
Q_TILE_MLA = 2048
Q_STRIP = 256
SUPER_BLOCKS = 32
NSA_PREFIX_PARTS = 4
SUM_ROW = 64
V_ROWS = 80
REF_ROW = 104
REF_SLAB = 96
EXP_GUARD = 100.0
SCORE_LOOKAHEAD = 3
SKIP_MARGIN = 160.0
LOG2E = 1.4426950408889634
MOE_CHUNK = 128
MOE_EXPERTS_PER_STEP = 8

HY_Q0 = 0
HY_KV0 = HY_Q0 + NSA_HEADS
HY_F0 = HY_KV0 + 6 * NSA_GROUPS
HY_G0 = HY_F0 + 3 * FOX_HEADS
HY_FF = HY_G0 + NSA_GROUPS
HY_BLOCKS = HY_FF + 2
HY_COL_TILE = 12 * LANE
assert (HY_BLOCKS * LANE) % HY_COL_TILE == 0


def _cparams(*sem):
    return pltpu.CompilerParams(dimension_semantics=sem, vmem_limit_bytes=VMEM_LIMIT_BYTES)


def _lane_iota(shape):
    return lax.broadcasted_iota(jnp.int32, shape, len(shape) - 1)


def _row_iota(shape):
    return lax.broadcasted_iota(jnp.int32, shape, len(shape) - 2)


def _dot_nt(a, b):
    return lax.dot_general(a, b, (((1,), (1,)), ((), ())), preferred_element_type=F32)


def _dot_tn(a, b):
    return lax.dot_general(a, b, (((0,), (0,)), ((), ())), preferred_element_type=F32)


def _ada_kernel(c_ref, w_ref, b_ref, o_ref):
    c = c_ref[...]
    cond = c * jax.nn.sigmoid(c)
    o_ref[0] = jnp.dot(cond, w_ref[0], precision=HIGHEST, preferred_element_type=F32) + b_ref[0]


def _ada_mod(c, w_ada, b_ada):
    depth, d, n = w_ada.shape
    tn = 768
    c8 = jnp.broadcast_to(c.reshape(1, d), (8, d))
    out = pl.pallas_call(
        _ada_kernel,
        grid=(depth, n // tn),
        in_specs=[pl.BlockSpec((8, d), lambda l, j: (0, 0)),
                  pl.BlockSpec((1, d, tn), lambda l, j: (l, 0, j)),
                  pl.BlockSpec((1, 1, tn), lambda l, j: (l, 0, j))],
        out_specs=pl.BlockSpec((1, 8, tn), lambda l, j: (l, 0, j)),
        out_shape=jax.ShapeDtypeStruct((depth, 8, n), F32),
        compiler_params=_cparams("parallel", "parallel"),
        name="ada_mod",
    )(c8, w_ada, b_ada.reshape(depth, 1, n))
    return out[:, 0:1, :]


def _norm_mod(x, g, sc, sh):
    ms = jnp.mean(x * x, axis=-1, keepdims=True)
    return (x * lax.rsqrt(ms + EPS) * g) * (1.0 + sc) + sh


def _nmm_kernel(x_ref, g_ref, sc_ref, sh_ref, w_ref, o_ref, h_scr):
    @pl.when(pl.program_id(1) == 0)
    def _():
        h_scr[...] = _norm_mod(x_ref[...], g_ref[...], sc_ref[...], sh_ref[...]).astype(BF16)

    o_ref[...] = jnp.dot(h_scr[...], w_ref[...], preferred_element_type=F32)


def _norm_mod_matmul(x2, g, sc, sh, w, tn):
    s, d = x2.shape
    n = w.shape[1]
    vec = pl.BlockSpec((1, d), lambda i, j: (0, 0))
    return pl.pallas_call(
        _nmm_kernel,
        grid=(s // ROW_TILE, n // tn),
        in_specs=[pl.BlockSpec((ROW_TILE, d), lambda i, j: (i, 0)), vec, vec, vec,
                  pl.BlockSpec((d, tn), lambda i, j: (0, j))],
        out_specs=pl.BlockSpec((ROW_TILE, tn), lambda i, j: (i, j)),
        out_shape=jax.ShapeDtypeStruct((s, n), F32),
        scratch_shapes=[pltpu.VMEM((ROW_TILE, d), BF16)],
        compiler_params=_cparams("parallel", "arbitrary"),
        name="norm_mod_matmul",
    )(x2, g, sc, sh, w)


def _head_rms(x, gain, n_real):
    ss = jnp.sum(x * x, axis=-1, keepdims=True)
    return x * lax.rsqrt(ss * (1.0 / n_real) + EPS) * gain


def _rope64(x, cos, sin):
    lane = _lane_iota(x.shape)
    rot = jnp.where(lane < 32, -pltpu.roll(x, LANE - 32, 1), pltpu.roll(x, 32, 1))
    return x * cos + rot * sin


def _t_bf16(x, rows=LANE):
    return x.T[:rows].astype(BF16)


def _split3(c):
    hi = c.astype(BF16).astype(F32)
    r1 = c - hi
    mid = r1.astype(BF16).astype(F32)
    lo = (r1 - mid).astype(BF16).astype(F32)
    return hi, mid, lo


def _hy_prep_kernel(p_ref, pos_ref, inv_ref, gq_ref, gk_ref, gfq_ref, gfk_ref, fb_ref,
                    qnt_ref, kct_ref, vct_ref, ks_ref, vst_ref, kw_ref, vwt_ref, gate_ref,
                    fqt_ref, fk_ref, fvt_ref, cedge_ref, carry_ref):
    i = pl.program_id(0)
    tm = PREP_TILE
    shp = (tm, LANE)
    lane = _lane_iota(shp)

    def blk(b):
        return p_ref[:, b * LANE:(b + 1) * LANE]

    ang = pos_ref[...] * inv_ref[...]
    real = lane < HEAD_DIM
    cos = jnp.where(real, jnp.cos(ang), 1.0)
    sin = jnp.where(real, jnp.sin(ang), 0.0)
    gq, gk, gfq, gfk = gq_ref[...], gk_ref[...], gfq_ref[...], gfk_ref[...]
    scale = HEAD_DIM ** -0.5 * LOG2E
    ones_row = lane == SUM_ROW
    ref_ones = jnp.where((lane >= REF_ROW) & (lane < REF_ROW + 3), 1.0, 0.0)

    for h in range(NSA_HEADS):
        q = _rope64(_head_rms(blk(HY_Q0 + h), gq, HEAD_DIM), cos, sin) * scale
        qnt_ref[h * LANE:(h + 1) * LANE, :] = _t_bf16(q)

    row = _row_iota(shp) + i * tm
    onehot = jnp.where(lane - HEAD_DIM == ((row // SLC_BLOCK) % SUPER_BLOCKS), 1.0, 0.0)
    for g in range(NSA_GROUPS):
        def kv(r):
            return blk(HY_KV0 + r * NSA_GROUPS + g)
        sl = slice(g * LANE, (g + 1) * LANE)
        kct_ref[g] = _rope64(kv(0), cos, sin)[:, :HEAD_DIM].astype(BF16)
        vct_ref[g] = kv(1)[:, :HEAD_DIM].astype(BF16)
        ks = _rope64(_head_rms(kv(2), gk, HEAD_DIM), cos, sin)
        ks_ref[:, sl] = (ks + onehot + ref_ones).astype(BF16)
        vst_ref[g, 0] = _t_bf16(jnp.where(ones_row, 1.0, kv(3)), V_ROWS)
        kw_ref[:, sl] = _rope64(_head_rms(kv(4), gk, HEAD_DIM), cos, sin).astype(BF16)
        vwt = _t_bf16(kv(5))
        for cidx in range(tm // LANE):
            vwt_ref[g, cidx] = vwt[:, cidx * LANE:(cidx + 1) * LANE]
        gate_ref[g] = jax.nn.sigmoid(blk(HY_G0 + g)).T[:GATE_ROWS]

    @pl.when(i == 0)
    def _():
        carry_ref[...] = jnp.zeros_like(carry_ref)

    z = blk(HY_FF) + fb_ref[...]
    logf = jnp.minimum(z, 0.0) - jnp.log1p(jnp.exp(-jnp.abs(z)))
    tri = jnp.where(_row_iota((tm, tm)) >= _lane_iota((tm, tm)), 1.0, 0.0).astype(F32)
    cum = jnp.dot(tri, logf, precision=HIGHEST, preferred_element_type=F32) + carry_ref[...]
    carry_ref[...] = cum[tm - 1:tm, :]
    cedge_ref[0] = jnp.concatenate([cum[0:1] * LOG2E, cum[tm - 1:tm] * LOG2E, jnp.zeros((6, LANE), F32)], axis=0)

    for h in range(FOX_HEADS):
        c = jnp.broadcast_to(cum[:, h:h + 1], shp) * LOG2E
        hi, mid, lo = _split3(c)
        fq = _head_rms(blk(HY_F0 + h), gfq, HEAD_DIM) * scale
        fq = jnp.where(real, fq, jnp.where(lane == 64, hi, jnp.where(lane == 65, mid, jnp.where(
            lane == 66, lo, jnp.where(lane < 70, 1.0, 0.0)))))
        fk = _head_rms(blk(HY_F0 + FOX_HEADS + h), gfk, HEAD_DIM)
        fk = jnp.where(real, fk, jnp.where(lane < 67, 1.0, jnp.where(lane == 67, -hi, jnp.where(
            lane == 68, -mid, jnp.where(lane == 69, -lo, ref_ones)))))
        sl = slice(h * LANE, (h + 1) * LANE)
        fqt_ref[sl, :] = _t_bf16(fq)
        fk_ref[:, sl] = fk.astype(BF16)
        fvt_ref[h, 0] = _t_bf16(jnp.where(ones_row, 1.0, blk(HY_F0 + 2 * FOX_HEADS + h)), V_ROWS)


def _feat_major(heads, s, tm):
    return (pl.BlockSpec((heads * LANE, tm), lambda i: (0, i)),
            jax.ShapeDtypeStruct((heads * LANE, s), BF16))


def _value_tiles(heads, s, tm, tk, rows=LANE):
    return (pl.BlockSpec((heads, tm // tk, rows, tk), lambda i: (0, i, 0, 0)),
            jax.ShapeDtypeStruct((heads, s // tk, rows, tk), BF16))


def _pad_feature_rows(o):
    return jnp.concatenate([o, jnp.zeros((LANE - HEAD_DIM, o.shape[1]), o.dtype)], axis=0)


def _hy_prep(proj, posf, inv128, gq, gk, gfq, gfk, fbias):
    s = proj.shape[0]
    tm = PREP_TILE
    assert tm == KV_TILE
    vec = pl.BlockSpec((1, LANE), lambda i: (0, 0))

    def rows(nb):
        return (pl.BlockSpec((tm, nb * LANE), lambda i: (i, 0)), jax.ShapeDtypeStruct((s, nb * LANE), BF16))

    tok = (pl.BlockSpec((NSA_GROUPS, tm, HEAD_DIM), lambda i: (0, i, 0)),
           jax.ShapeDtypeStruct((NSA_GROUPS, s, HEAD_DIM), BF16))
    gate = (pl.BlockSpec((NSA_GROUPS, GATE_ROWS, tm), lambda i: (0, 0, i)),
            jax.ShapeDtypeStruct((NSA_GROUPS, GATE_ROWS, s), F32))
    outs = [_feat_major(NSA_HEADS, s, tm), tok, tok, rows(NSA_GROUPS),
            _value_tiles(NSA_GROUPS, s, tm, KV_TILE, V_ROWS),
            rows(NSA_GROUPS), _value_tiles(NSA_GROUPS, s, tm, LANE), gate,
            _feat_major(FOX_HEADS, s, tm), rows(FOX_HEADS), _value_tiles(FOX_HEADS, s, tm, KV_TILE, V_ROWS),
            (pl.BlockSpec((1, 8, LANE), lambda i: (i, 0, 0)), jax.ShapeDtypeStruct((s // tm, 8, LANE), F32))]
    return pl.pallas_call(
        _hy_prep_kernel,
        grid=(s // tm,),
        in_specs=[pl.BlockSpec((tm, HY_BLOCKS * LANE), lambda i: (i, 0)), pl.BlockSpec((tm, 1), lambda i: (i, 0)),
                  vec, vec, vec, vec, vec, vec],
        out_specs=[o[0] for o in outs],
        out_shape=[o[1] for o in outs],
        scratch_shapes=[pltpu.VMEM((1, LANE), F32)],
        compiler_params=_cparams("arbitrary"),
        name="hybrid_prep",
    )(proj, posf, inv128, gq, gk, gfq, gfk, fbias)


def _compress_kernel(kc_ref, vc_ref, wk_ref, wv_ref, pek_ref, pev_ref, gk_ref, ko_ref, vo_ref):
    half = CMP_STRIDE * HEAD_DIM

    def comp(ch_ref, w_ref, pe_ref):
        ch = ch_ref[0]
        nc = ch.shape[0]
        a = jnp.dot(ch, w_ref[:half], preferred_element_type=F32)
        b = jnp.dot(ch, w_ref[half:], preferred_element_type=F32)
        nxt = pltpu.roll(b, nc - 1, 0)
        pe = jnp.dot(jnp.broadcast_to(pe_ref[...], (8, 2 * half)).astype(BF16), w_ref[...],
                     preferred_element_type=F32)[0:1]
        return a + nxt + pe

    ko_ref[0] = _head_rms(comp(kc_ref, wk_ref, pek_ref), gk_ref[...], HEAD_DIM).astype(BF16)
    vo_ref[0] = comp(vc_ref, wv_ref, pev_ref).T.astype(BF16)


def _compress(kct, vct, w_cmp, cmp_pe, k_norm):
    g, s, _ = kct.shape
    nc = s // CMP_STRIDE
    wide = CMP_STRIDE * HEAD_DIM
    kch = kct.reshape(g, nc, wide)
    vch = vct.reshape(g, nc, wide)
    w_pad = jnp.pad(w_cmp, ((0, 0), (0, 0), (0, LANE - HEAD_DIM))).astype(BF16)
    ch = pl.BlockSpec((1, nc, wide), lambda i: (i, 0, 0))
    wspec = pl.BlockSpec((2 * wide, LANE), lambda i: (0, 0))
    pespec = pl.BlockSpec((1, 2 * wide), lambda i: (0, 0))
    return pl.pallas_call(
        _compress_kernel,
        grid=(g,),
        in_specs=[ch, ch, wspec, wspec, pespec, pespec, pl.BlockSpec((1, LANE), lambda i: (0, 0))],
        out_specs=[pl.BlockSpec((1, nc, LANE), lambda i: (i, 0, 0)),
                   pl.BlockSpec((1, LANE, nc), lambda i: (i, 0, 0))],
        out_shape=[jax.ShapeDtypeStruct((g, nc, LANE), BF16), jax.ShapeDtypeStruct((g, LANE, nc), BF16)],
        compiler_params=_cparams("parallel"),
        name="nsa_compress",
    )(kch, vch, w_pad[0], w_pad[1], cmp_pe[0].reshape(1, 2 * wide).astype(F32),
      cmp_pe[1].reshape(1, 2 * wide).astype(F32), _pad_lanes(k_norm))


def _masked_softmax_t(s, mask):
    s = jnp.where(mask, s, NEG)
    m = jnp.max(s, axis=0, keepdims=True)
    e = jnp.exp2(s - m)
    inv = 1.0 / jnp.maximum(jnp.sum(e, axis=0, keepdims=True), 1e-30)
    return e, jnp.where(m > 0.5 * NEG, inv, 0.0)


def _online_steps(steps, ms, acc_ref):
    ms = list(ms)

    def scores(step):
        k_tile, qa, _, c, mask = step
        s = jnp.dot(k_tile, qa, preferred_element_type=F32)
        if mask is not None:
            s = jnp.where(mask, s, NEG)
        return s, jnp.max(s, axis=0, keepdims=True)

    nxt = scores(steps[0])
    for idx, (_, _, vt, c, _) in enumerate(steps):
        sl = slice(c * Q_STRIP, (c + 1) * Q_STRIP)
        s, s_max = nxt
        if idx + 1 < len(steps):
            nxt = scores(steps[idx + 1])
        m_new = jnp.maximum(ms[c], s_max)
        a = jnp.exp2(ms[c] - m_new)
        p = jnp.exp2((s - m_new).astype(BF16))
        ms[c] = m_new
        acc_ref[:, sl] = a * acc_ref[:, sl] + jnp.dot(vt, p, preferred_element_type=F32)
    return tuple(ms)


def _m_init(n_strips):
    return tuple(jnp.full((1, Q_STRIP), NEG, F32) for _ in range(n_strips))


def _with_ref_rows(qa, m):
    hi, mid, lo = _split3(-m)
    r = _row_iota((LANE - REF_SLAB, Q_STRIP)) + REF_SLAB
    slab = jnp.where(r == REF_ROW, hi, jnp.where(r == REF_ROW + 1, mid, jnp.where(r == REF_ROW + 2, lo, 0.0)))
    return jnp.concatenate([qa[:REF_SLAB], slab.astype(BF16)], axis=0)


def _first_tile_max(k_tile, qa_strips, masks):
    return tuple(jnp.max(jnp.where(mask, jnp.dot(k_tile, qa, preferred_element_type=F32), NEG), axis=0, keepdims=True)
                 for qa, mask in zip(qa_strips, masks))


def _fast_steps(steps, state, acc_ref):
    state = list(state)
    for k, step in enumerate(steps):
        assert all(prev[3] != step[3] for prev in steps[max(k - SCORE_LOOKAHEAD + 1, 0):k])

    def scores(step):
        k_tile, qa, _, c, mask = step
        s = jnp.dot(k_tile, _with_ref_rows(qa, state[c][0]), preferred_element_type=F32)
        if mask is not None:
            s = jnp.where(mask, s, NEG)
        return s

    ahead = [scores(st) for st in steps[:SCORE_LOOKAHEAD]]
    for idx, (_, _, vt, c, _) in enumerate(steps):
        sl = slice(c * Q_STRIP, (c + 1) * Q_STRIP)
        s = ahead.pop(0)
        m, worst = state[c]
        cm = jnp.max(s, axis=0, keepdims=True)
        inc = jnp.maximum(cm, 0.0)
        state[c] = (m + inc, jnp.maximum(worst, cm))
        if idx + SCORE_LOOKAHEAD < len(steps):
            ahead.append(scores(steps[idx + SCORE_LOOKAHEAD]))
        p = jnp.exp2(s).astype(BF16)
        acc_ref[:, sl] = jnp.exp2(-inc) * (acc_ref[:, sl] + jnp.dot(vt, p, preferred_element_type=F32))
    return tuple(state)


def _flat(state):
    return tuple(x for pair in state for x in pair)


def _nest(flat):
    return tuple((flat[2 * c], flat[2 * c + 1]) for c in range(len(flat) // 2))


def _nsa_kernel(qt_ref, kc_ref, vct_ref, ks_ref, vst_ref, kw_ref, vwt_ref, gate_ref, ovt_ref, o_ref,
                qaug_ref, acc_ref, *, n_sel):
    i = pl.program_id(1)
    tq = Q_TILE_NSA
    cols = NSA_HPG * tq
    qs = i * tq
    nc = kc_ref.shape[1]
    nslc = ovt_ref.shape[0]
    n_super = nslc // SUPER_BLOCKS

    qt = jnp.concatenate([qt_ref[h * LANE:(h + 1) * LANE, :] for h in range(NSA_HPG)], axis=1)
    tq_row = qs + (_lane_iota((1, cols)) % tq)

    def compress_and_select(n_c, n_b):
        s = jnp.dot(kc_ref[0, :n_c], qt, preferred_element_type=F32)
        cmp_end = _row_iota((n_c, 1)) * CMP_STRIDE + (CMP_LEN - 1)
        e, inv_l = _masked_softmax_t(s, cmp_end <= tq_row)
        o_cmp = jnp.dot(vct_ref[0, :, :n_c], e.astype(BF16), preferred_element_type=F32) * inv_l

        psum = e[:, 0:tq] * inv_l[:, 0:tq]
        for h in range(1, NSA_HPG):
            psum = psum + e[:, h * tq:(h + 1) * tq] * inv_l[:, h * tq:(h + 1) * tq]
        p_hi = psum.astype(BF16)
        p_lo = (psum - p_hi.astype(F32)).astype(BF16)
        ovt = ovt_ref[:n_b, :n_c]
        imp = (jnp.dot(ovt, p_hi, preferred_element_type=F32)
               + jnp.dot(ovt, p_lo, preferred_element_type=F32))

        jj = _row_iota((n_b, tq))
        tq_blk = qs + _lane_iota((n_b, tq))
        cur = tq_blk // SLC_BLOCK
        forced = (jj == 0) | (jj == cur) | (jj == cur - 1)
        causal_blk = jj * SLC_BLOCK <= tq_blk
        val = jnp.where(forced, imp + BIG, imp)
        val = jnp.where(causal_blk, val, NEG)
        jjf = jj.astype(F32)

        def pick(_, carry):
            val, sel = carry
            mx = jnp.max(val, axis=0, keepdims=True)
            idx = jnp.min(jnp.where(val == mx, jjf, float(n_b)), axis=0, keepdims=True)
            hit = jjf == idx
            return jnp.where(hit, -jnp.inf, val), jnp.where(hit, 1.0, sel)

        _, sel = lax.fori_loop(0, min(n_sel, n_b), pick, (val, jnp.zeros((n_b, tq), F32)))
        bias = jnp.where((sel > 0.0) & causal_blk, 0.0, NEG)
        if n_b < nslc:
            bias = jnp.concatenate([bias, jnp.full((nslc - n_b, tq), NEG, F32)], axis=0)
        return o_cmp, bias

    parts = max(1, min(NSA_PREFIX_PARTS, nc // (2 * LANE)))
    seq = nc * CMP_STRIDE

    def dispatch(k):
        full = lambda: compress_and_select(nc * k // parts, nslc * k // parts)
        if k == parts:
            return full()
        return lax.cond(qs + tq <= seq * k // parts, full, lambda: dispatch(k + 1))

    o_cmp, bias_t = dispatch(1)

    q_rows = qt[:HEAD_DIM].astype(F32)
    spare = jnp.zeros((LANE - HEAD_DIM - SUPER_BLOCKS, cols), F32)
    for st in range(n_super):
        b = bias_t[st * SUPER_BLOCKS:(st + 1) * SUPER_BLOCKS]
        b = jnp.concatenate([b] * NSA_HPG, axis=1)
        qaug_ref[st] = jnp.concatenate([q_rows, b, spare], axis=0).astype(BF16)

    tk = KV_TILE
    per_super = SUPER_BLOCKS * SLC_BLOCK // tk
    j_last = (qs + tq - 1) // tk
    n_strips = cols // Q_STRIP
    strips = [slice(c * Q_STRIP, (c + 1) * Q_STRIP) for c in range(n_strips)]

    def causal_masks(j):
        kpos = j * tk + _row_iota((tk, 1))
        return [kpos <= tq_row[:, sl] for sl in strips]

    def slc_steps(j, masks=None):
        k0 = pl.multiple_of(j * tk, tk)
        k_tile, vt, st = ks_ref[pl.ds(k0, tk), :], vst_ref[0, j], j // per_super
        return [(k_tile, qaug_ref[st, :, strips[c]], vt, c, None if masks is None else masks[c])
                for c in range(n_strips)]

    group = 4

    def grouped(jj, flat):
        steps = [st for t in range(group) for st in slc_steps(group * jj + t)]
        return _flat(_fast_steps(steps, _nest(flat), acc_ref))

    def single(j, flat):
        return _flat(_fast_steps(slc_steps(j), _nest(flat), acc_ref))

    acc_ref[...] = jnp.zeros(acc_ref.shape, F32)
    head_rows = 16
    head_masks = [_row_iota((head_rows, 1)) <= tq_row[:, sl] for sl in strips]
    m0 = _first_tile_max(ks_ref[0:head_rows, :], [qaug_ref[0, :, sl] for sl in strips], head_masks)
    n_groups = j_last // group
    flat = lax.fori_loop(0, n_groups, grouped, _flat(tuple((m, jnp.zeros_like(m)) for m in m0)))
    flat = lax.fori_loop(group * n_groups, j_last, single, flat)
    state = _fast_steps(slc_steps(j_last, causal_masks(j_last)), _nest(flat), acc_ref)
    worst = jnp.max(jnp.concatenate([w for _, w in state], axis=1))

    @pl.when(worst > EXP_GUARD)
    def _():
        acc_ref[...] = jnp.zeros(acc_ref.shape, F32)
        lax.fori_loop(0, j_last + 1, lambda j, ms: _online_steps(slc_steps(j, causal_masks(j)), ms, acc_ref),
                      _m_init(n_strips))

    o_slc = acc_ref[:HEAD_DIM] / jnp.maximum(acc_ref[SUM_ROW:SUM_ROW + 1], 1e-30)

    wlen = WINDOW + tq
    ws = pl.multiple_of(jnp.maximum(qs - WINDOW, 0), tq)
    s = jnp.dot(kw_ref[pl.ds(ws, wlen), :], qt, preferred_element_type=F32)
    dist = tq_row - (ws + _row_iota((wlen, 1)))
    e, inv_l = _masked_softmax_t(s, (dist >= 0) & (dist < WINDOW))
    e = e.astype(BF16)
    wb = ws // LANE
    o_win = jnp.zeros((LANE, cols), F32)
    for c in range(wlen // LANE):
        o_win = o_win + jnp.dot(vwt_ref[0, wb + c], e[c * LANE:(c + 1) * LANE], preferred_element_type=F32)
    o_win = o_win * inv_l

    gate = gate_ref[0]
    for h in range(NSA_HPG):
        sl = slice(h * tq, (h + 1) * tq)
        o = (gate[3 * h:3 * h + 1] * o_cmp[:HEAD_DIM, sl] + gate[3 * h + 1:3 * h + 2] * o_slc[:, sl]
             + gate[3 * h + 2:3 * h + 3] * o_win[:HEAD_DIM, sl])
        o_ref[:, h * LANE:(h + 1) * LANE] = _pad_feature_rows(o).T.astype(BF16)


def _overlap_t(s, nslc_pad):
    nc = s // CMP_STRIDE
    cmp_start = np.arange(nc) * CMP_STRIDE
    slc_start = np.arange(nslc_pad) * SLC_BLOCK
    ov = np.clip(np.minimum(cmp_start[:, None] + CMP_LEN, slc_start[None, :] + SLC_BLOCK)
                 - np.maximum(cmp_start[:, None], slc_start[None, :]), 0, None) / CMP_STRIDE
    ov[nc - CMP_LEN // CMP_STRIDE + 1:, :] = 0.0
    ov[:, s // SLC_BLOCK:] = 0.0
    return jnp.asarray(ov.T, BF16)


def _nsa_attention(qnt, kc, vct, ks, vst, kw, vwt, gates):
    s = qnt.shape[1]
    nc = s // CMP_STRIDE
    n_slc = s // SLC_BLOCK
    nslc_pad = -(-n_slc // LANE) * LANE
    tq = Q_TILE_NSA
    cols = NSA_HPG * tq
    once = pl.Buffered(1)
    res = pl.BlockSpec((s, LANE), lambda g, i: (0, g), pipeline_mode=once)
    return pl.pallas_call(
        functools.partial(_nsa_kernel, n_sel=min(SLC_TOPK, n_slc)),
        grid=(NSA_GROUPS, s // tq),
        in_specs=[pl.BlockSpec((NSA_HPG * LANE, tq), lambda g, i: (g, i)),
                  pl.BlockSpec((1, nc, LANE), lambda g, i: (g, 0, 0), pipeline_mode=once),
                  pl.BlockSpec((1, LANE, nc), lambda g, i: (g, 0, 0), pipeline_mode=once),
                  res, pl.BlockSpec((1, s // KV_TILE, V_ROWS, KV_TILE), lambda g, i: (g, 0, 0, 0),
                                    pipeline_mode=once),
                  res, pl.BlockSpec((1, s // LANE, LANE, LANE), lambda g, i: (g, 0, 0, 0), pipeline_mode=once),
                  pl.BlockSpec((1, GATE_ROWS, tq), lambda g, i: (g, 0, i)),
                  pl.BlockSpec((nslc_pad, nc), lambda g, i: (0, 0), pipeline_mode=once)],
        out_specs=pl.BlockSpec((tq, NSA_HPG * LANE), lambda g, i: (i, g)),
        out_shape=jax.ShapeDtypeStruct((s, NSA_HEADS * LANE), BF16),
        scratch_shapes=[pltpu.VMEM((nslc_pad // SUPER_BLOCKS, LANE, cols), BF16),
                        pltpu.VMEM((V_ROWS, cols), F32)],
        compiler_params=_cparams("parallel", "arbitrary"),
        name="nsa_attention",
    )(qnt, kc, vct, ks, vst, kw, vwt, gates, _overlap_t(s, nslc_pad))


def _flash_kernel(cfirst_ref, clast_ref, slack_ref, qt_ref, k_ref, vt_ref, o_ref, acc_ref, *, tq):
    h = pl.program_id(0)
    i = pl.program_id(1)
    tk = KV_TILE
    n_tiles = k_ref.shape[0] // tk
    acc_ref[...] = jnp.zeros(acc_ref.shape, F32)

    n_strips = tq // Q_STRIP
    per_q = tq // tk
    qas = [qt_ref[:, c * Q_STRIP:(c + 1) * Q_STRIP] for c in range(n_strips)]

    def tile_steps(j, d=None):
        k0 = pl.multiple_of(j * tk, tk)
        k_tile, vt = k_ref[pl.ds(k0, tk), :], vt_ref[0, j]
        steps = []
        for c in range(n_strips):
            mask = None
            if d is not None:
                if d * tk > (c + 1) * Q_STRIP - 1:
                    continue
                if (d + 1) * tk - 1 > c * Q_STRIP:
                    shp = (tk, Q_STRIP)
                    mask = _row_iota(shp) + d * tk <= _lane_iota(shp) + c * Q_STRIP
            steps.append((k_tile, qas[c], vt, c, mask))
        return steps

    def any_tile_masks(j):
        shp = (tk, Q_STRIP)
        return [_row_iota(shp) + j * tk <= _lane_iota(shp) + (i * tq + c * Q_STRIP) for c in range(n_strips)]

    def below(t, flat):
        jj = i - 1 - t
        bound = (slack_ref[0] + cfirst_ref[h * n_tiles + i * per_q]
                 - clast_ref[h * n_tiles + jj * per_q + per_q - 1])

        def run(flat):
            steps = [st for u in range(per_q) for st in tile_steps(jj * per_q + (per_q - 1 - u))]
            return _flat(_fast_steps(steps, _nest(flat), acc_ref))

        return lax.cond(bound >= -SKIP_MARGIN, run, lambda flat: flat, flat)

    diag0 = i * per_q
    own = [(c * Q_STRIP) // tk for c in range(n_strips)]
    m0 = tuple(_first_tile_max(k_ref[pl.ds(pl.multiple_of((diag0 + own[c]) * tk, tk), tk), :], [qas[c]],
                               [any_tile_masks(diag0 + own[c])[c]])[0] for c in range(n_strips))
    steps = [st for d in reversed(range(per_q)) for st in tile_steps(diag0 + d, d)]
    state = _fast_steps(steps, tuple((m, jnp.zeros_like(m)) for m in m0), acc_ref)
    state = _nest(lax.fori_loop(0, i, below, _flat(state)))
    worst = jnp.max(jnp.concatenate([w for _, w in state], axis=1))

    @pl.when(worst > EXP_GUARD)
    def _():
        acc_ref[...] = jnp.zeros(acc_ref.shape, F32)

        def exact(j, ms):
            k0 = pl.multiple_of(j * tk, tk)
            k_tile, vt, masks = k_ref[pl.ds(k0, tk), :], vt_ref[0, j], any_tile_masks(j)
            return _online_steps([(k_tile, qas[c], vt, c, masks[c]) for c in range(n_strips)], ms, acc_ref)

        lax.fori_loop(0, (i + 1) * per_q, exact, _m_init(n_strips))

    o = _pad_feature_rows(acc_ref[:HEAD_DIM] / acc_ref[SUM_ROW:SUM_ROW + 1])
    for c0 in range(0, tq, LANE):
        o_ref[c0:c0 + LANE, :] = o[:, c0:c0 + LANE].T.astype(BF16)


def _causal_attention(qt, k, vt, tq, bias_edges=None, slack=None):
    s, width = k.shape
    heads = width // LANE
    tq = min(tq, s)
    assert tq % KV_TILE == 0 and s % tq == 0
    n_tiles = s // KV_TILE
    if bias_edges is None:
        first = last = jnp.zeros((heads * n_tiles,), F32)
        slack = jnp.full((1,), -NEG, F32)
    else:
        first, last = (e.reshape(heads * n_tiles).astype(F32) for e in bias_edges)
    grid_spec = pltpu.PrefetchScalarGridSpec(
        num_scalar_prefetch=3,
        grid=(heads, s // tq),
        in_specs=[pl.BlockSpec((LANE, tq), lambda h, i, *_: (h, i)),
                  pl.BlockSpec((s, LANE), lambda h, i, *_: (0, h)),
                  pl.BlockSpec((1, n_tiles, V_ROWS, KV_TILE), lambda h, i, *_: (h, 0, 0, 0))],
        out_specs=pl.BlockSpec((tq, LANE), lambda h, i, *_: (i, h)),
        scratch_shapes=[pltpu.VMEM((V_ROWS, tq), F32)],
    )
    return pl.pallas_call(
        functools.partial(_flash_kernel, tq=tq),
        grid_spec=grid_spec,
        out_shape=jax.ShapeDtypeStruct((s, width), BF16),
        compiler_params=_cparams("parallel", "arbitrary"),
        name="causal_attention",
    )(first, last, slack.astype(F32), qt, k, vt)


def _out_proj_kernel(oa_ref, ob_ref, wa_ref, wb_ref, x_ref, g_ref, o_ref):
    y = jnp.dot(oa_ref[...], wa_ref[...], preferred_element_type=F32)
    y = y + jnp.dot(ob_ref[...], wb_ref[...], preferred_element_type=F32)
    o_ref[...] = x_ref[...] + g_ref[...] * y


def _out_proj(oa, ob, cola, colb, wa, wb, x2, gate):
    s, d = x2.shape
    ka = wa.shape[0]
    tm = ROW_TILE
    return pl.pallas_call(
        _out_proj_kernel,
        grid=(s // tm,),
        in_specs=[pl.BlockSpec((tm, ka), lambda i: (i, cola)), pl.BlockSpec((tm, ka), lambda i: (i, colb)),
                  pl.BlockSpec((ka, d), lambda i: (0, 0)), pl.BlockSpec((ka, d), lambda i: (0, 0)),
                  pl.BlockSpec((tm, d), lambda i: (i, 0)), pl.BlockSpec((1, d), lambda i: (0, 0))],
        out_specs=pl.BlockSpec((tm, d), lambda i: (i, 0)),
        out_shape=jax.ShapeDtypeStruct((s, d), F32),
        compiler_params=_cparams("parallel"),
        name="out_proj",
    )(oa, ob, wa, wb, x2, gate)


def _mla_prep_kernel(p_ref, pos_ref, inv_ref, gqa_ref, gkva_ref, wuq_ref, wuk_ref, wuv_ref,
                     gq_ref, gk_ref, gkr_ref, qt_ref, k_ref, vt_ref):
    shp = (PREP_TILE, LANE)
    lane = _lane_iota(shp)
    nope = lane < QK_NOPE
    rope = (lane >= QK_NOPE) & (lane < QK_NOPE + QK_ROPE)
    ref_ones = jnp.where((lane >= REF_ROW) & (lane < REF_ROW + 3), 1.0, 0.0)
    ang = pos_ref[...] * inv_ref[...]
    cos = jnp.where(rope, jnp.cos(ang), 1.0)
    sin = jnp.where(rope, jnp.sin(ang), 0.0)

    def rope32(x):
        half = QK_ROPE // 2
        rot = jnp.where(lane < QK_NOPE + half, -pltpu.roll(x, LANE - half, 1), pltpu.roll(x, half, 1))
        return x * cos + rot * sin

    def low_rank_norm(x, g):
        ms = jnp.mean(x * x, axis=-1, keepdims=True)
        return (x * lax.rsqrt(ms + EPS) * g).astype(BF16)

    nq = Q_LORA // LANE
    cq = low_rank_norm(p_ref[:, :Q_LORA], gqa_ref[...])
    ckv = low_rank_norm(p_ref[:, Q_LORA:Q_LORA + KV_LORA], gkva_ref[...])
    kr = p_ref[:, (nq + KV_LORA // LANE) * LANE:(nq + KV_LORA // LANE + 1) * LANE]
    k_rope = rope32(_head_rms(kr, gkr_ref[...], QK_ROPE))

    gq, gk = gq_ref[...], gk_ref[...]
    scale = (QK_NOPE + QK_ROPE) ** -0.5 * LOG2E
    pair = 2 * LANE
    for hp in range(MLA_HEADS // 2):
        cols = slice(hp * pair, (hp + 1) * pair)
        q2 = jnp.dot(cq, wuq_ref[:, cols], preferred_element_type=F32)
        k2 = jnp.dot(ckv, wuk_ref[:, cols], preferred_element_type=F32)
        v2 = jnp.dot(ckv, wuv_ref[:, cols], preferred_element_type=F32)
        for sub in range(2):
            head = 2 * hp + sub
            sl = slice(head * LANE, (head + 1) * LANE)
            half = slice(sub * LANE, (sub + 1) * LANE)
            x = q2[:, half]
            ss_n = jnp.sum(jnp.where(nope, x * x, 0.0), axis=-1, keepdims=True)
            ss_r = jnp.sum(jnp.where(rope, x * x, 0.0), axis=-1, keepdims=True)
            inv_rms = jnp.where(nope, lax.rsqrt(ss_n * (1.0 / QK_NOPE) + EPS),
                                lax.rsqrt(ss_r * (1.0 / QK_ROPE) + EPS))
            qt_ref[sl, :] = _t_bf16(rope32(x * inv_rms * gq) * scale)
            kn = _head_rms(k2[:, half], gk, QK_NOPE)
            k_ref[:, sl] = (kn + k_rope + ref_ones).astype(BF16)
            vt_ref[head, 0] = _t_bf16(jnp.where(lane == SUM_ROW, 1.0, v2[:, half]), V_ROWS)


def _mla_prep(proj, posf, inv128, gqa, gkva, wuq, wuk, wuv, gq, gk, gkr):
    s, n = proj.shape
    tm = PREP_TILE
    assert tm == KV_TILE

    def full(a):
        return pl.BlockSpec(a.shape, lambda i: (0, 0))

    outs = [_feat_major(MLA_HEADS, s, tm),
            (pl.BlockSpec((tm, MLA_HEADS * LANE), lambda i: (i, 0)),
             jax.ShapeDtypeStruct((s, MLA_HEADS * LANE), BF16)),
            _value_tiles(MLA_HEADS, s, tm, KV_TILE, V_ROWS)]
    args = (inv128, gqa, gkva, wuq, wuk, wuv, gq, gk, gkr)
    return pl.pallas_call(
        _mla_prep_kernel,
        grid=(s // tm,),
        in_specs=[pl.BlockSpec((tm, n), lambda i: (i, 0)), pl.BlockSpec((tm, 1), lambda i: (i, 0))]
                 + [full(a) for a in args],
        out_specs=[o[0] for o in outs],
        out_shape=[o[1] for o in outs],
        compiler_params=_cparams("parallel"),
        name="mla_prep",
    )(proj, posf, *args)


def _rank_lt(v, k):
    n = v.shape[0]
    row = _row_iota(v.shape)
    rank = jnp.zeros(v.shape, F32)
    for b in range(n):
        vb = v[b:b + 1, :]
        rank = rank + jnp.where((vb > v) | ((vb == v) & (row > b)), 1.0, 0.0)
    return rank < k


def _top_rows(v, k):
    rowf = _row_iota(v.shape).astype(F32)
    chosen = jnp.zeros(v.shape, F32)
    for _ in range(k):
        mx = jnp.max(v, axis=0, keepdims=True)
        idx = jnp.min(jnp.where(v == mx, rowf, float(v.shape[0])), axis=0, keepdims=True)
        hit = rowf == idx
        chosen = jnp.where(hit, 1.0, chosen)
        v = jnp.where(hit, -jnp.inf, v)
    return chosen > 0.0


def _moe_route_kernel(x_ref, g_ref, sc_ref, sh_ref, wr_ref, rb_ref, h_ref, pos_ref, wt_ref, cnt_ref):
    tm = ROW_TILE
    h = _norm_mod(x_ref[...], g_ref[...], sc_ref[...], sh_ref[...])
    h_ref[...] = h.astype(BF16)
    logits = jnp.dot(h, wr_ref[...], precision=HIGHEST, preferred_element_type=F32)
    lt = logits.T[:N_EXPERTS]
    scores = jax.nn.sigmoid(lt)
    sel = scores + rb_ref[...]

    per = N_EXPERTS // N_GROUPS
    grp = sel.reshape(N_GROUPS, per, tm)
    sub = lax.broadcasted_iota(jnp.int32, grp.shape, 1)
    m1 = jnp.max(grp, axis=1, keepdims=True)
    first = jnp.min(jnp.where(grp == m1, sub, per), axis=1, keepdims=True)
    m2 = jnp.max(jnp.where(sub == first, -jnp.inf, grp), axis=1, keepdims=True)
    gscore = (m1 + m2).reshape(N_GROUPS, tm)
    gmask = _rank_lt(gscore, TOPK_GROUPS)
    emask = jnp.broadcast_to(gmask.reshape(N_GROUPS, 1, tm), grp.shape).reshape(N_EXPERTS, tm)
    chosen = _top_rows(jnp.where(emask, sel, NEG), TOP_K)

    w = jnp.where(chosen, scores, 0.0)
    wt_ref[...] = w / jnp.sum(w, axis=0, keepdims=True) * ROUTED_SCALE

    upper = jnp.where(_row_iota((tm, tm)) <= _lane_iota((tm, tm)), 1.0, 0.0).astype(BF16)
    incl = jnp.dot(jnp.where(chosen, 1.0, 0.0).astype(BF16), upper, preferred_element_type=F32)
    pos_ref[...] = jnp.where(chosen, incl - 1.0, -1.0)
    cnt_ref[0] = jnp.broadcast_to(incl[:, tm - 1:tm], (N_EXPERTS, LANE))


def _moe_route(x2, g, sc, sh, w_router_pad, router_bias_col):
    s, d = x2.shape
    tm = ROW_TILE
    vec = pl.BlockSpec((1, d), lambda i: (0, 0))
    et = pl.BlockSpec((N_EXPERTS, tm), lambda i: (0, i))
    return pl.pallas_call(
        _moe_route_kernel,
        grid=(s // tm,),
        in_specs=[pl.BlockSpec((tm, d), lambda i: (i, 0)), vec, vec, vec,
                  pl.BlockSpec((d, LANE), lambda i: (0, 0)),
                  pl.BlockSpec((N_EXPERTS, 1), lambda i: (0, 0))],
        out_specs=[pl.BlockSpec((tm, d), lambda i: (i, 0)), et, et,
                   pl.BlockSpec((1, N_EXPERTS, LANE), lambda i: (i, 0, 0))],
        out_shape=[jax.ShapeDtypeStruct((s, d), BF16), jax.ShapeDtypeStruct((N_EXPERTS, s), F32),
                   jax.ShapeDtypeStruct((N_EXPERTS, s), F32),
                   jax.ShapeDtypeStruct((s // tm, N_EXPERTS, LANE), F32)],
        compiler_params=_cparams("parallel"),
        name="moe_route",
    )(x2, g, sc, sh, w_router_pad, router_bias_col)


def _moe_kernel(cnt_ref, x_ref, h_ref, pos_ref, wt_ref, wg_ref, wu_ref, wd_ref, sg_ref, su_ref, sd_ref,
                g2_ref, o_ref, acc_ref):
    i = pl.program_id(0)
    e = pl.program_id(1)
    tm = ROW_TILE
    r = MOE_CHUNK

    @pl.when(e == 0)
    def _():
        h = h_ref[...]
        a = jnp.dot(h, sg_ref[...], preferred_element_type=F32)
        a = a * jax.nn.sigmoid(a) * jnp.dot(h, su_ref[...], preferred_element_type=F32)
        acc_ref[...] = jnp.dot(a.astype(BF16), sd_ref[...], preferred_element_type=F32)

    first = e * MOE_EXPERTS_PER_STEP
    n = cnt_ref[i * N_EXPERTS + first]
    for k in range(1, MOE_EXPERTS_PER_STEP):
        n = jnp.maximum(n, cnt_ref[i * N_EXPERTS + first + k])
    prows = [pos_ref[pl.ds(first + k, 1), :] for k in range(MOE_EXPERTS_PER_STEP)]
    wrows = [wt_ref[pl.ds(first + k, 1), :] for k in range(MOE_EXPERTS_PER_STEP)]

    def chunk(c, _):
        slot = (_row_iota((r, tm)) + c * r).astype(F32)
        hits = [prow == slot for prow in prows]
        onehot = jnp.concatenate([jnp.where(hit, 1.0, 0.0).astype(BF16) for hit in hits], axis=0)
        xg = jnp.dot(onehot, h_ref[...], preferred_element_type=F32).astype(BF16)
        ys = []
        for k in range(MOE_EXPERTS_PER_STEP):
            xk = xg[k * r:(k + 1) * r]
            a = jnp.dot(xk, wg_ref[k], preferred_element_type=F32)
            a = a * jax.nn.sigmoid(a) * jnp.dot(xk, wu_ref[k], preferred_element_type=F32)
            y = jnp.dot(a.astype(BF16), wd_ref[k], preferred_element_type=F32)
            wr = jnp.sum(jnp.where(hits[k], wrows[k], 0.0), axis=-1, keepdims=True)
            ys.append((y * wr).astype(BF16))
        acc_ref[...] += _dot_tn(onehot, jnp.concatenate(ys, axis=0))
        return 0

    lax.fori_loop(0, (n + r - 1) // r, chunk, 0)

    @pl.when(e == N_EXPERTS // MOE_EXPERTS_PER_STEP - 1)
    def _():
        o_ref[...] = x_ref[...] + g2_ref[...] * acc_ref[...]


def _moe_experts(counts, x2, h, pos_t, w_t, wg, wu, wd, sg, su, sd, g2):
    s, d = x2.shape
    tm = ROW_TILE
    ff = wg.shape[2]
    tile = pl.BlockSpec((tm, d), lambda i, e, c: (i, 0))
    et = pl.BlockSpec((N_EXPERTS, tm), lambda i, e, c: (0, i))

    def const(a):
        return pl.BlockSpec(a.shape, lambda i, e, c: (0,) * a.ndim)

    per = MOE_EXPERTS_PER_STEP
    grid_spec = pltpu.PrefetchScalarGridSpec(
        num_scalar_prefetch=1,
        grid=(s // tm, N_EXPERTS // per),
        in_specs=[tile, tile, et, et,
                  pl.BlockSpec((per, d, ff), lambda i, e, c: (e, 0, 0)),
                  pl.BlockSpec((per, d, ff), lambda i, e, c: (e, 0, 0)),
                  pl.BlockSpec((per, ff, d), lambda i, e, c: (e, 0, 0)),
                  const(sg), const(su), const(sd), const(g2)],
        out_specs=tile,
        scratch_shapes=[pltpu.VMEM((tm, d), F32)],
    )
    return pl.pallas_call(
        _moe_kernel,
        grid_spec=grid_spec,
        out_shape=jax.ShapeDtypeStruct((s, d), F32),
        compiler_params=_cparams("parallel", "arbitrary"),
        name="moe_experts",
    )(counts, x2, h, pos_t, w_t, wg, wu, wd, sg, su, sd, g2)


def _pad_lanes(v, width=LANE, offset=0):
    out = jnp.zeros((1, width), F32)
    return out.at[0, offset:offset + v.shape[0]].set(v.astype(F32))


def _head_cols(w, n_heads, dim):
    d = w.shape[0]
    w3 = w.reshape(d, n_heads, dim)
    return jnp.pad(w3, ((0, 0), (0, 0), (0, LANE - dim))).reshape(d, n_heads * LANE)


def _hybrid_w_in(w_in):
    d = w_in.shape[0]
    nq = NSA_HEADS * HEAD_DIM
    nkv = 6 * NSA_GROUPS * HEAD_DIM
    ng = 3 * NSA_HEADS
    nf = 3 * FOX_HEADS * HEAD_DIM
    c0, c1, c2, c3 = nq, nq + nkv, nq + nkv + ng, nq + nkv + ng + nf
    gates = w_in[:, c1:c2].reshape(d, NSA_GROUPS, 3 * NSA_HPG)
    gates = jnp.pad(gates, ((0, 0), (0, 0), (0, LANE - 3 * NSA_HPG))).reshape(d, NSA_GROUPS * LANE)
    ff = jnp.pad(w_in[:, c3:], ((0, 0), (0, 2 * LANE - FOX_HEADS)))
    return jnp.concatenate([
        _head_cols(w_in[:, :c0], NSA_HEADS, HEAD_DIM),
        _head_cols(w_in[:, c0:c1], 6 * NSA_GROUPS, HEAD_DIM),
        _head_cols(w_in[:, c2:c3], 3 * FOX_HEADS, HEAD_DIM),
        gates, ff], axis=1).astype(BF16)


def _pad_head_rows(w, n_heads, dim):
    d = w.shape[1]
    w3 = w.reshape(n_heads, dim, d)
    return jnp.pad(w3, ((0, 0), (0, LANE - dim), (0, 0))).reshape(n_heads * LANE, d).astype(BF16)


def _rope_inv(dim, offset):
    inv = ROPE_THETA ** (-jnp.arange(0, dim, 2, dtype=F32) / dim)
    return _pad_lanes(jnp.concatenate([inv, inv]), offset=offset)


def _hybrid_mixer(x2, posf, mods, norm_g, w_in, fox_f_bias, nsa_q_norm, nsa_k_norm, nsa_cmp_pe, nsa_w_cmp,
                  fox_q_norm, fox_k_norm, w_out):
    sh1, sc1, g1 = mods
    proj = _norm_mod_matmul(x2, norm_g, sc1, sh1, _hybrid_w_in(w_in), tn=HY_COL_TILE)
    (qnt, kct, vct, ks, vst, kw, vwt, gates, fqt, fk, fvt, cedge) = _hy_prep(
        proj, posf, _rope_inv(HEAD_DIM, 0), _pad_lanes(nsa_q_norm), _pad_lanes(nsa_k_norm),
        _pad_lanes(fox_q_norm), _pad_lanes(fox_k_norm), _pad_lanes(fox_f_bias))
    kc, vc_t = _compress(kct, vct, nsa_w_cmp, nsa_cmp_pe, nsa_k_norm)
    o_a = _nsa_attention(qnt, kc, vc_t, ks, vst, kw, vwt, gates)
    slack = (2.0 * HEAD_DIM ** 0.5 * LOG2E) * jnp.max(jnp.abs(fox_q_norm)) * jnp.max(jnp.abs(fox_k_norm))
    edges = (cedge[:, 0, :FOX_HEADS].T, cedge[:, 1, :FOX_HEADS].T)
    o_b = _causal_attention(fqt, fk, fvt, Q_TILE_FOX, edges, slack.reshape(1))
    half = NSA_HEADS * HEAD_DIM
    wa = _pad_head_rows(w_out[:half], NSA_HEADS, HEAD_DIM)
    wb = _pad_head_rows(w_out[half:], FOX_HEADS, HEAD_DIM)
    return _out_proj(o_a, o_b, 0, 0, wa, wb, x2, g1)


def _mla_mixer(x2, posf, mods, norm_g, w_in, q_a_norm, kv_a_norm, w_uq, w_ukv, qn_norm, kn_norm, qr_norm,
               kr_norm, w_out):
    sh1, sc1, g1 = mods
    d = x2.shape[1]
    w_kr = jnp.zeros((d, LANE), F32).at[:, QK_NOPE:QK_NOPE + QK_ROPE].set(w_in[:, Q_LORA + KV_LORA:])
    w_in_p = jnp.concatenate([w_in[:, :Q_LORA + KV_LORA], w_kr], axis=1).astype(BF16)
    proj = _norm_mod_matmul(x2, norm_g, sc1, sh1, w_in_p, tn=w_in_p.shape[1])
    hq = QK_NOPE + QK_ROPE
    wuq = _head_cols(w_uq, MLA_HEADS, hq).astype(BF16)
    wkv3 = w_ukv.reshape(KV_LORA, MLA_HEADS, QK_NOPE + V_HEAD)
    wuk = _head_cols(wkv3[:, :, :QK_NOPE].reshape(KV_LORA, -1), MLA_HEADS, QK_NOPE).astype(BF16)
    wuv = _head_cols(wkv3[:, :, QK_NOPE:].reshape(KV_LORA, -1), MLA_HEADS, V_HEAD).astype(BF16)
    gq = _pad_lanes(jnp.concatenate([qn_norm, qr_norm]))
    qt, k, vt = _mla_prep(proj, posf, _rope_inv(QK_ROPE, QK_NOPE), q_a_norm.reshape(1, -1).astype(F32),
                          kv_a_norm.reshape(1, -1).astype(F32), wuq, wuk, wuv, gq, _pad_lanes(kn_norm),
                          _pad_lanes(kr_norm, offset=QK_NOPE))
    o = _causal_attention(qt, k, vt, Q_TILE_MLA)
    w_pad = _pad_head_rows(w_out, MLA_HEADS, V_HEAD)
    half = w_pad.shape[0] // 2
    return _out_proj(o, o, 0, 1, w_pad[:half], w_pad[half:], x2, g1)


def _moe_ffn(x2, mods, norm_g, w_router, router_bias, w_gate, w_up, w_down, ws_gate, ws_up, ws_down):
    sh2, sc2, g2 = mods
    w_r = jnp.pad(w_router.astype(F32), ((0, 0), (0, LANE - N_EXPERTS)))
    h, pos_t, w_t, cnt = _moe_route(x2, norm_g, sc2, sh2, w_r, router_bias.reshape(N_EXPERTS, 1).astype(F32))
    counts = cnt[:, :, 0].astype(jnp.int32).reshape(-1)
    return _moe_experts(counts, x2, h, pos_t, w_t, w_gate.astype(BF16), w_up.astype(BF16),
                        w_down.astype(BF16), ws_gate.astype(BF16), ws_up.astype(BF16), ws_down.astype(BF16), g2)


def kernel(x, c, positions, norm_attn, norm_ffn, w_ada, b_ada, hy_w_in, fox_f_bias, nsa_q_norm, nsa_k_norm, nsa_cmp_pe, nsa_w_cmp, fox_q_norm, fox_k_norm, hy_w_out, mla_w_in, mla_q_a_norm, mla_kv_a_norm, mla_w_uq, mla_w_ukv, mla_qn_norm, mla_kn_norm, mla_qr_norm, mla_kr_norm, mla_w_out, moe_w_router, moe_router_bias, moe_w_gate, moe_w_up, moe_w_down, moe_ws_gate, moe_ws_up, moe_ws_down):
    b, s, d = x.shape
    assert b == 1 and s % KV_TILE == 0 and s >= WINDOW + Q_TILE_NSA
    depth = w_ada.shape[0]
    x2 = x.reshape(s, d).astype(F32)
    posf = positions.reshape(s, 1).astype(F32)
    mod = _ada_mod(c.astype(F32), w_ada.astype(F32), b_ada.astype(F32))

    for layer in range(depth):
        m = [mod[layer, :, k * d:(k + 1) * d] for k in range(6)]
        i = layer // 2
        g_attn = norm_attn[layer].reshape(1, d).astype(F32)
        if layer % 2 == 0:
            x2 = _hybrid_mixer(x2, posf, m[0:3], g_attn, hy_w_in[i], fox_f_bias[i], nsa_q_norm[i],
                               nsa_k_norm[i], nsa_cmp_pe[i], nsa_w_cmp[i], fox_q_norm[i], fox_k_norm[i],
                               hy_w_out[i])
        else:
            x2 = _mla_mixer(x2, posf, m[0:3], g_attn, mla_w_in[i], mla_q_a_norm[i], mla_kv_a_norm[i],
                            mla_w_uq[i], mla_w_ukv[i], mla_qn_norm[i], mla_kn_norm[i], mla_qr_norm[i],
                            mla_kr_norm[i], mla_w_out[i])
        x2 = _moe_ffn(x2, m[3:6], norm_ffn[layer].reshape(1, d).astype(F32), moe_w_router[layer],
                      moe_router_bias[layer], moe_w_gate[layer], moe_w_up[layer], moe_w_down[layer],
                      moe_ws_gate[layer], moe_ws_up[layer], moe_ws_down[layer])
    return x2.reshape(b, s, d)
```

```python
import functools

import numpy as np
import jax
import jax.numpy as jnp
from jax import lax
from jax.experimental import pallas as pl
from jax.experimental.pallas import tpu as pltpu

F32 = jnp.float32
BF16 = jnp.bfloat16
HIGHEST = lax.Precision.HIGHEST

LANE = 128
VMEM_LIMIT_BYTES = 56 * 1024 * 1024

HEAD_DIM = 64
NSA_HEADS = 8
NSA_GROUPS = 2
NSA_HPG = NSA_HEADS // NSA_GROUPS
CMP_LEN = 32
CMP_STRIDE = 16
SLC_BLOCK = 64
SLC_TOPK = 16
WINDOW = 512
FOX_HEADS = 8
MLA_HEADS = 16
Q_LORA = 384
KV_LORA = 256
QK_NOPE = 64
QK_ROPE = 32
V_HEAD = 64
N_EXPERTS = 64
TOP_K = 8
N_GROUPS = 8
TOPK_GROUPS = 4
EXPERT_FF = 256
ROUTED_SCALE = 2.5
ROPE_THETA = 10000.0
EPS = 1e-6
NEG = -1e30
BIG = 1e6

ROW_TILE = 512
PREP_TILE = 512
GATE_ROWS = 16
Q_TILE_NSA = 256
KV_TILE = 512
Q_TILE_FOX = 1024
Q_TILE_MLA = 2048
Q_STRIP = 256
SUPER_BLOCKS = 32
NSA_PREFIX_PARTS = 4
SUM_ROW = 64
V_ROWS = 80
REF_ROW = 104
REF_SLAB = 96
EXP_GUARD = 100.0
SCORE_LOOKAHEAD = 3
SKIP_MARGIN = 160.0
LOG2E = 1.4426950408889634
MOE_CHUNK = 128
MOE_EXPERTS_PER_STEP = 8

HY_Q0 = 0
HY_KV0 = HY_Q0 + NSA_HEADS
HY_F0 = HY_KV0 + 6 * NSA_GROUPS
HY_G0 = HY_F0 + 3 * FOX_HEADS
HY_FF = HY_G0 + NSA_GROUPS
HY_BLOCKS = HY_FF + 2
HY_COL_TILE = 12 * LANE
assert (HY_BLOCKS * LANE) % HY_COL_TILE == 0


def _cparams(*sem):
    return pltpu.CompilerParams(dimension_semantics=sem, vmem_limit_bytes=VMEM_LIMIT_BYTES)


def _lane_iota(shape):
    return lax.broadcasted_iota(jnp.int32, shape, len(shape) - 1)


def _row_iota(shape):
    return lax.broadcasted_iota(jnp.int32, shape, len(shape) - 2)


def _dot_nt(a, b):
    return lax.dot_general(a, b, (((1,), (1,)), ((), ())), preferred_element_type=F32)


def _dot_tn(a, b):
    return lax.dot_general(a, b, (((0,), (0,)), ((), ())), preferred_element_type=F32)


def _ada_kernel(c_ref, w_ref, b_ref, o_ref):
    c = c_ref[...]
    cond = c * jax.nn.sigmoid(c)
    o_ref[0] = jnp.dot(cond, w_ref[0], precision=HIGHEST, preferred_element_type=F32) + b_ref[0]


def _ada_mod(c, w_ada, b_ada):
    depth, d, n = w_ada.shape
    tn = 768
    c8 = jnp.broadcast_to(c.reshape(1, d), (8, d))
    out = pl.pallas_call(
        _ada_kernel,
        grid=(depth, n // tn),
        in_specs=[pl.BlockSpec((8, d), lambda l, j: (0, 0)),
                  pl.BlockSpec((1, d, tn), lambda l, j: (l, 0, j)),
                  pl.BlockSpec((1, 1, tn), lambda l, j: (l, 0, j))],
        out_specs=pl.BlockSpec((1, 8, tn), lambda l, j: (l, 0, j)),
        out_shape=jax.ShapeDtypeStruct((depth, 8, n), F32),
        compiler_params=_cparams("parallel", "parallel"),
        name="ada_mod",
    )(c8, w_ada, b_ada.reshape(depth, 1, n))
    return out[:, 0:1, :]


def _norm_mod(x, g, sc, sh):
    ms = jnp.mean(x * x, axis=-1, keepdims=True)
    return (x * lax.rsqrt(ms + EPS) * g) * (1.0 + sc) + sh


def _nmm_kernel(x_ref, g_ref, sc_ref, sh_ref, w_ref, o_ref, h_scr):
    @pl.when(pl.program_id(1) == 0)
    def _():
        h_scr[...] = _norm_mod(x_ref[...], g_ref[...], sc_ref[...], sh_ref[...]).astype(BF16)

    o_ref[...] = jnp.dot(h_scr[...], w_ref[...], preferred_element_type=F32)


def _norm_mod_matmul(x2, g, sc, sh, w, tn):
    s, d = x2.shape
    n = w.shape[1]
    vec = pl.BlockSpec((1, d), lambda i, j: (0, 0))
    return pl.pallas_call(
        _nmm_kernel,
        grid=(s // ROW_TILE, n // tn),
        in_specs=[pl.BlockSpec((ROW_TILE, d), lambda i, j: (i, 0)), vec, vec, vec,
                  pl.BlockSpec((d, tn), lambda i, j: (0, j))],
        out_specs=pl.BlockSpec((ROW_TILE, tn), lambda i, j: (i, j)),
        out_shape=jax.ShapeDtypeStruct((s, n), F32),
        scratch_shapes=[pltpu.VMEM((ROW_TILE, d), BF16)],
        compiler_params=_cparams("parallel", "arbitrary"),
        name="norm_mod_matmul",
    )(x2, g, sc, sh, w)


def _head_rms(x, gain, n_real):
    ss = jnp.sum(x * x, axis=-1, keepdims=True)
    return x * lax.rsqrt(ss * (1.0 / n_real) + EPS) * gain


def _rope64(x, cos, sin):
    lane = _lane_iota(x.shape)
    rot = jnp.where(lane < 32, -pltpu.roll(x, LANE - 32, 1), pltpu.roll(x, 32, 1))
    return x * cos + rot * sin


def _t_bf16(x, rows=LANE):
    return x.T[:rows].astype(BF16)


def _split3(c):
    hi = c.astype(BF16).astype(F32)
    r1 = c - hi
    mid = r1.astype(BF16).astype(F32)
    lo = (r1 - mid).astype(BF16).astype(F32)
    return hi, mid, lo


def _hy_prep_kernel(x_ref, ng_ref, sc_ref, sh_ref, w_ref, pos_ref, inv_ref, gq_ref, gk_ref, gfq_ref, gfk_ref,
                    fb_ref, qnt_ref, kct_ref, vct_ref, ks_ref, vst_ref, kw_ref, vwt_ref, gate_ref,
                    fqt_ref, fk_ref, fvt_ref, cedge_ref, carry_ref):
    i = pl.program_id(0)
    tm = PREP_TILE
    shp = (tm, LANE)
    lane = _lane_iota(shp)

    act = _norm_mod(x_ref[...], ng_ref[...], sc_ref[...], sh_ref[...]).astype(BF16)
    pairs = {}

    def blk(b):
        if b // 2 not in pairs:
            cols = slice((b // 2) * 2 * LANE, (b // 2 + 1) * 2 * LANE)
            pairs[b // 2] = jnp.dot(act, w_ref[:, cols], preferred_element_type=F32)
        return pairs[b // 2][:, (b % 2) * LANE:(b % 2 + 1) * LANE]

    ang = pos_ref[...] * inv_ref[...]
    real = lane < HEAD_DIM
    cos = jnp.where(real, jnp.cos(ang), 1.0)
    sin = jnp.where(real, jnp.sin(ang), 0.0)
    gq, gk, gfq, gfk = gq_ref[...], gk_ref[...], gfq_ref[...], gfk_ref[...]
    scale = HEAD_DIM ** -0.5 * LOG2E
    ones_row = lane == SUM_ROW
    ref_ones = jnp.where((lane >= REF_ROW) & (lane < REF_ROW + 3), 1.0, 0.0)

    for h in range(NSA_HEADS):
        q = _rope64(_head_rms(blk(HY_Q0 + h), gq, HEAD_DIM), cos, sin) * scale
        qnt_ref[h * LANE:(h + 1) * LANE, :] = _t_bf16(q)

    row = _row_iota(shp) + i * tm
    onehot = jnp.where(lane - HEAD_DIM == ((row // SLC_BLOCK) % SUPER_BLOCKS), 1.0, 0.0)
    for g in range(NSA_GROUPS):
        def kv(r):
            return blk(HY_KV0 + r * NSA_GROUPS + g)
        sl = slice(g * LANE, (g + 1) * LANE)
        kct_ref[g] = _rope64(kv(0), cos, sin)[:, :HEAD_DIM].astype(BF16)
        vct_ref[g] = kv(1)[:, :HEAD_DIM].astype(BF16)
        ks = _rope64(_head_rms(kv(2), gk, HEAD_DIM), cos, sin)
        ks_ref[:, sl] = (ks + onehot + ref_ones).astype(BF16)
        vst_ref[g, 0] = _t_bf16(jnp.where(ones_row, 1.0, kv(3)), V_ROWS)
        kw_ref[:, sl] = _rope64(_head_rms(kv(4), gk, HEAD_DIM), cos, sin).astype(BF16)
        vwt = _t_bf16(kv(5))
        for cidx in range(tm // LANE):
            vwt_ref[g, cidx] = vwt[:, cidx * LANE:(cidx + 1) * LANE]
        gate_ref[g] = jax.nn.sigmoid(blk(HY_G0 + g)).T[:GATE_ROWS]

    @pl.when(i == 0)
    def _():
        carry_ref[...] = jnp.zeros_like(carry_ref)

    z = blk(HY_FF) + fb_ref[...]
    logf = jnp.minimum(z, 0.0) - jnp.log1p(jnp.exp(-jnp.abs(z)))
    tri = jnp.where(_row_iota((tm, tm)) >= _lane_iota((tm, tm)), 1.0, 0.0).astype(F32)
    cum = jnp.dot(tri, logf, precision=HIGHEST, preferred_element_type=F32) + carry_ref[...]
    carry_ref[...] = cum[tm - 1:tm, :]
    cedge_ref[0] = jnp.concatenate([cum[0:1] * LOG2E, cum[tm - 1:tm] * LOG2E, jnp.zeros((6, LANE), F32)], axis=0)

    for h in range(FOX_HEADS):
        c = jnp.broadcast_to(cum[:, h:h + 1], shp) * LOG2E
        hi, mid, lo = _split3(c)
        fq = _head_rms(blk(HY_F0 + h), gfq, HEAD_DIM) * scale
        fq = jnp.where(real, fq, jnp.where(lane == 64, hi, jnp.where(lane == 65, mid, jnp.where(
            lane == 66, lo, jnp.where(lane < 70, 1.0, 0.0)))))
        fk = _head_rms(blk(HY_F0 + FOX_HEADS + h), gfk, HEAD_DIM)
        fk = jnp.where(real, fk, jnp.where(lane < 67, 1.0, jnp.where(lane == 67, -hi, jnp.where(
            lane == 68, -mid, jnp.where(lane == 69, -lo, ref_ones)))))
        sl = slice(h * LANE, (h + 1) * LANE)
        fqt_ref[sl, :] = _t_bf16(fq)
        fk_ref[:, sl] = fk.astype(BF16)
        fvt_ref[h, 0] = _t_bf16(jnp.where(ones_row, 1.0, blk(HY_F0 + 2 * FOX_HEADS + h)), V_ROWS)


def _feat_major(heads, s, tm):
    return (pl.BlockSpec((heads * LANE, tm), lambda i: (0, i)),
            jax.ShapeDtypeStruct((heads * LANE, s), BF16))


def _value_tiles(heads, s, tm, tk, rows=LANE):
    return (pl.BlockSpec((heads, tm // tk, rows, tk), lambda i: (0, i, 0, 0)),
            jax.ShapeDtypeStruct((heads, s // tk, rows, tk), BF16))


def _pad_feature_rows(o):
    return jnp.concatenate([o, jnp.zeros((LANE - HEAD_DIM, o.shape[1]), o.dtype)], axis=0)


def _hy_prep(x2, norm_g, sc, sh, w_in, posf, inv128, gq, gk, gfq, gfk, fbias):
    s, d = x2.shape
    tm = PREP_TILE
    assert tm == KV_TILE and w_in.shape == (d, HY_BLOCKS * LANE) and HY_BLOCKS % 2 == 0
    vec = pl.BlockSpec((1, LANE), lambda i: (0, 0))
    dvec = pl.BlockSpec((1, d), lambda i: (0, 0))

    def rows(nb):
        return (pl.BlockSpec((tm, nb * LANE), lambda i: (i, 0)), jax.ShapeDtypeStruct((s, nb * LANE), BF16))

    tok = (pl.BlockSpec((NSA_GROUPS, tm, HEAD_DIM), lambda i: (0, i, 0)),
           jax.ShapeDtypeStruct((NSA_GROUPS, s, HEAD_DIM), BF16))
    gate = (pl.BlockSpec((NSA_GROUPS, GATE_ROWS, tm), lambda i: (0, 0, i)),
            jax.ShapeDtypeStruct((NSA_GROUPS, GATE_ROWS, s), F32))
    outs = [_feat_major(NSA_HEADS, s, tm), tok, tok, rows(NSA_GROUPS),
            _value_tiles(NSA_GROUPS, s, tm, KV_TILE, V_ROWS),
            rows(NSA_GROUPS), _value_tiles(NSA_GROUPS, s, tm, LANE), gate,
            _feat_major(FOX_HEADS, s, tm), rows(FOX_HEADS), _value_tiles(FOX_HEADS, s, tm, KV_TILE, V_ROWS),
            (pl.BlockSpec((1, 8, LANE), lambda i: (i, 0, 0)), jax.ShapeDtypeStruct((s // tm, 8, LANE), F32))]
    return pl.pallas_call(
        _hy_prep_kernel,
        grid=(s // tm,),
        in_specs=[pl.BlockSpec((tm, d), lambda i: (i, 0)), dvec, dvec, dvec,
                  pl.BlockSpec((d, HY_BLOCKS * LANE), lambda i: (0, 0), pipeline_mode=pl.Buffered(1)),
                  pl.BlockSpec((tm, 1), lambda i: (i, 0)), vec, vec, vec, vec, vec, vec],
        out_specs=[o[0] for o in outs],
        out_shape=[o[1] for o in outs],
        scratch_shapes=[pltpu.VMEM((1, LANE), F32)],
        compiler_params=_cparams("arbitrary"),
        name="hybrid_prep",
    )(x2, norm_g, sc, sh, w_in, posf, inv128, gq, gk, gfq, gfk, fbias)


def _compress_kernel(kc_ref, vc_ref, wk_ref, wv_ref, pek_ref, pev_ref, gk_ref, ko_ref, vo_ref):
    half = CMP_STRIDE * HEAD_DIM

    def comp(ch_ref, w_ref, pe_ref):
        ch = ch_ref[0]
        nc = ch.shape[0]
        a = jnp.dot(ch, w_ref[:half], preferred_element_type=F32)
        b = jnp.dot(ch, w_ref[half:], preferred_element_type=F32)
        nxt = pltpu.roll(b, nc - 1, 0)
        pe = jnp.dot(jnp.broadcast_to(pe_ref[...], (8, 2 * half)).astype(BF16), w_ref[...],
                     preferred_element_type=F32)[0:1]
        return a + nxt + pe

    ko_ref[0] = _head_rms(comp(kc_ref, wk_ref, pek_ref), gk_ref[...], HEAD_DIM).astype(BF16)
    vo_ref[0] = comp(vc_ref, wv_ref, pev_ref).T.astype(BF16)


def _compress(kct, vct, w_cmp, cmp_pe, k_norm):
    g, s, _ = kct.shape
    nc = s // CMP_STRIDE
    wide = CMP_STRIDE * HEAD_DIM
    kch = kct.reshape(g, nc, wide)
    vch = vct.reshape(g, nc, wide)
    w_pad = jnp.pad(w_cmp, ((0, 0), (0, 0), (0, LANE - HEAD_DIM))).astype(BF16)
    ch = pl.BlockSpec((1, nc, wide), lambda i: (i, 0, 0))
    wspec = pl.BlockSpec((2 * wide, LANE), lambda i: (0, 0))
    pespec = pl.BlockSpec((1, 2 * wide), lambda i: (0, 0))
    return pl.pallas_call(
        _compress_kernel,
        grid=(g,),
        in_specs=[ch, ch, wspec, wspec, pespec, pespec, pl.BlockSpec((1, LANE), lambda i: (0, 0))],
        out_specs=[pl.BlockSpec((1, nc, LANE), lambda i: (i, 0, 0)),
                   pl.BlockSpec((1, LANE, nc), lambda i: (i, 0, 0))],
        out_shape=[jax.ShapeDtypeStruct((g, nc, LANE), BF16), jax.ShapeDtypeStruct((g, LANE, nc), BF16)],
        compiler_params=_cparams("parallel"),
        name="nsa_compress",
    )(kch, vch, w_pad[0], w_pad[1], cmp_pe[0].reshape(1, 2 * wide).astype(F32),
      cmp_pe[1].reshape(1, 2 * wide).astype(F32), _pad_lanes(k_norm))


def _masked_softmax_t(s, mask):
    s = jnp.where(mask, s, NEG)
    m = jnp.max(s, axis=0, keepdims=True)
    e = jnp.exp2(s - m)
    inv = 1.0 / jnp.maximum(jnp.sum(e, axis=0, keepdims=True), 1e-30)
    return e, jnp.where(m > 0.5 * NEG, inv, 0.0)


def _online_steps(steps, ms, acc_ref):
    ms = list(ms)

    def scores(step):
        k_tile, qa, _, c, mask = step
        s = jnp.dot(k_tile, qa, preferred_element_type=F32)
        if mask is not None:
            s = jnp.where(mask, s, NEG)
        return s, jnp.max(s, axis=0, keepdims=True)

    nxt = scores(steps[0])
    for idx, (_, _, vt, c, _) in enumerate(steps):
        sl = slice(c * Q_STRIP, (c + 1) * Q_STRIP)
        s, s_max = nxt
        if idx + 1 < len(steps):
            nxt = scores(steps[idx + 1])
        m_new = jnp.maximum(ms[c], s_max)
        a = jnp.exp2(ms[c] - m_new)
        p = jnp.exp2((s - m_new).astype(BF16))
        ms[c] = m_new
        acc_ref[:, sl] = a * acc_ref[:, sl] + jnp.dot(vt, p, preferred_element_type=F32)
    return tuple(ms)


def _m_init(n_strips):
    return tuple(jnp.full((1, Q_STRIP), NEG, F32) for _ in range(n_strips))


def _with_ref_rows(qa, m):
    hi, mid, lo = _split3(-m)
    r = _row_iota((LANE - REF_SLAB, Q_STRIP)) + REF_SLAB
    slab = jnp.where(r == REF_ROW, hi, jnp.where(r == REF_ROW + 1, mid, jnp.where(r == REF_ROW + 2, lo, 0.0)))
    return jnp.concatenate([qa[:REF_SLAB], slab.astype(BF16)], axis=0)


def _first_tile_max(k_tile, qa_strips, masks):
    return tuple(jnp.max(jnp.where(mask, jnp.dot(k_tile, qa, preferred_element_type=F32), NEG), axis=0, keepdims=True)
                 for qa, mask in zip(qa_strips, masks))


def _fast_steps(steps, state, acc_ref):
    state = list(state)
    for k, step in enumerate(steps):
        assert all(prev[3] != step[3] for prev in steps[max(k - SCORE_LOOKAHEAD + 1, 0):k])

    def scores(step):
        k_tile, qa, _, c, mask = step
        s = jnp.dot(k_tile, _with_ref_rows(qa, state[c][0]), preferred_element_type=F32)
        if mask is not None:
            s = jnp.where(mask, s, NEG)
        return s

    ahead = [scores(st) for st in steps[:SCORE_LOOKAHEAD]]
    for idx, (_, _, vt, c, _) in enumerate(steps):
        sl = slice(c * Q_STRIP, (c + 1) * Q_STRIP)
        s = ahead.pop(0)
        m, worst = state[c]
        cm = jnp.max(s, axis=0, keepdims=True)
        inc = jnp.maximum(cm, 0.0)
        state[c] = (m + inc, jnp.maximum(worst, cm))
        if idx + SCORE_LOOKAHEAD < len(steps):
            ahead.append(scores(steps[idx + SCORE_LOOKAHEAD]))
        p = jnp.exp2(s).astype(BF16)
        acc_ref[:, sl] = jnp.exp2(-inc) * (acc_ref[:, sl] + jnp.dot(vt, p, preferred_element_type=F32))
    return tuple(state)


def _flat(state):
    return tuple(x for pair in state for x in pair)


def _nest(flat):
    return tuple((flat[2 * c], flat[2 * c + 1]) for c in range(len(flat) // 2))


def _nsa_kernel(qt_ref, kc_ref, vct_ref, ks_ref, vst_ref, kw_ref, vwt_ref, gate_ref, ovt_ref, o_ref,
                qaug_ref, acc_ref, *, n_sel):
    i = pl.program_id(1)
    tq = Q_TILE_NSA
    cols = NSA_HPG * tq
    qs = i * tq
    nc = kc_ref.shape[1]
    nslc = ovt_ref.shape[0]
    n_super = nslc // SUPER_BLOCKS

    qt = jnp.concatenate([qt_ref[h * LANE:(h + 1) * LANE, :] for h in range(NSA_HPG)], axis=1)
    tq_row = qs + (_lane_iota((1, cols)) % tq)

    def compress_and_select(n_c, n_b):
        s = jnp.dot(kc_ref[0, :n_c], qt, preferred_element_type=F32)
        cmp_end = _row_iota((n_c, 1)) * CMP_STRIDE + (CMP_LEN - 1)
        e, inv_l = _masked_softmax_t(s, cmp_end <= tq_row)
        o_cmp = jnp.dot(vct_ref[0, :, :n_c], e.astype(BF16), preferred_element_type=F32) * inv_l

        psum = e[:, 0:tq] * inv_l[:, 0:tq]
        for h in range(1, NSA_HPG):
            psum = psum + e[:, h * tq:(h + 1) * tq] * inv_l[:, h * tq:(h + 1) * tq]
        p_hi = psum.astype(BF16)
        p_lo = (psum - p_hi.astype(F32)).astype(BF16)
        ovt = ovt_ref[:n_b, :n_c]
        imp = (jnp.dot(ovt, p_hi, preferred_element_type=F32)
               + jnp.dot(ovt, p_lo, preferred_element_type=F32))

        jj = _row_iota((n_b, tq))
        tq_blk = qs + _lane_iota((n_b, tq))
        cur = tq_blk // SLC_BLOCK
        forced = (jj == 0) | (jj == cur) | (jj == cur - 1)
        causal_blk = jj * SLC_BLOCK <= tq_blk
        val = jnp.where(forced, imp + BIG, imp)
        val = jnp.where(causal_blk, val, NEG)
        jjf = jj.astype(F32)

        def pick(_, carry):
            val, sel = carry
            mx = jnp.max(val, axis=0, keepdims=True)
            idx = jnp.min(jnp.where(val == mx, jjf, float(n_b)), axis=0, keepdims=True)
            hit = jjf == idx
            return jnp.where(hit, -jnp.inf, val), jnp.where(hit, 1.0, sel)

        _, sel = lax.fori_loop(0, min(n_sel, n_b), pick, (val, jnp.zeros((n_b, tq), F32)))
        bias = jnp.where((sel > 0.0) & causal_blk, 0.0, NEG)
        if n_b < nslc:
            bias = jnp.concatenate([bias, jnp.full((nslc - n_b, tq), NEG, F32)], axis=0)
        return o_cmp, bias

    parts = max(1, min(NSA_PREFIX_PARTS, nc // (2 * LANE)))
    seq = nc * CMP_STRIDE

    def dispatch(k):
        full = lambda: compress_and_select(nc * k // parts, nslc * k // parts)
        if k == parts:
            return full()
        return lax.cond(qs + tq <= seq * k // parts, full, lambda: dispatch(k + 1))

    o_cmp, bias_t = dispatch(1)

    q_rows = qt[:HEAD_DIM].astype(F32)
    spare = jnp.zeros((LANE - HEAD_DIM - SUPER_BLOCKS, cols), F32)
    for st in range(n_super):
        b = bias_t[st * SUPER_BLOCKS:(st + 1) * SUPER_BLOCKS]
        b = jnp.concatenate([b] * NSA_HPG, axis=1)
        qaug_ref[st] = jnp.concatenate([q_rows, b, spare], axis=0).astype(BF16)

    tk = KV_TILE
    per_super = SUPER_BLOCKS * SLC_BLOCK // tk
    j_last = (qs + tq - 1) // tk
    n_strips = cols // Q_STRIP
    strips = [slice(c * Q_STRIP, (c + 1) * Q_STRIP) for c in range(n_strips)]

    def causal_masks(j):
        kpos = j * tk + _row_iota((tk, 1))
        return [kpos <= tq_row[:, sl] for sl in strips]

    def slc_steps(j, masks=None):
        k0 = pl.multiple_of(j * tk, tk)
        k_tile, vt, st = ks_ref[pl.ds(k0, tk), :], vst_ref[0, j], j // per_super
        return [(k_tile, qaug_ref[st, :, strips[c]], vt, c, None if masks is None else masks[c])
                for c in range(n_strips)]

    group = 4

    def grouped(jj, flat):
        steps = [st for t in range(group) for st in slc_steps(group * jj + t)]
        return _flat(_fast_steps(steps, _nest(flat), acc_ref))

    def single(j, flat):
        return _flat(_fast_steps(slc_steps(j), _nest(flat), acc_ref))

    acc_ref[...] = jnp.zeros(acc_ref.shape, F32)
    head_rows = 16
    head_masks = [_row_iota((head_rows, 1)) <= tq_row[:, sl] for sl in strips]
    m0 = _first_tile_max(ks_ref[0:head_rows, :], [qaug_ref[0, :, sl] for sl in strips], head_masks)
    n_groups = j_last // group
    flat = lax.fori_loop(0, n_groups, grouped, _flat(tuple((m, jnp.zeros_like(m)) for m in m0)))
    flat = lax.fori_loop(group * n_groups, j_last, single, flat)
    state = _fast_steps(slc_steps(j_last, causal_masks(j_last)), _nest(flat), acc_ref)
    worst = jnp.max(jnp.concatenate([w for _, w in state], axis=1))

    @pl.when(worst > EXP_GUARD)
    def _():
        acc_ref[...] = jnp.zeros(acc_ref.shape, F32)
        lax.fori_loop(0, j_last + 1, lambda j, ms: _online_steps(slc_steps(j, causal_masks(j)), ms, acc_ref),
                      _m_init(n_strips))

    o_slc = acc_ref[:HEAD_DIM] / jnp.maximum(acc_ref[SUM_ROW:SUM_ROW + 1], 1e-30)

    wlen = WINDOW + tq
    ws = pl.multiple_of(jnp.maximum(qs - WINDOW, 0), tq)
    s = jnp.dot(kw_ref[pl.ds(ws, wlen), :], qt, preferred_element_type=F32)
    dist = tq_row - (ws + _row_iota((wlen, 1)))
    e, inv_l = _masked_softmax_t(s, (dist >= 0) & (dist < WINDOW))
    e = e.astype(BF16)
    wb = ws // LANE
    o_win = jnp.zeros((LANE, cols), F32)
    for c in range(wlen // LANE):
        o_win = o_win + jnp.dot(vwt_ref[0, wb + c], e[c * LANE:(c + 1) * LANE], preferred_element_type=F32)
    o_win = o_win * inv_l

    gate = gate_ref[0]
    for h in range(NSA_HPG):
        sl = slice(h * tq, (h + 1) * tq)
        o = (gate[3 * h:3 * h + 1] * o_cmp[:HEAD_DIM, sl] + gate[3 * h + 1:3 * h + 2] * o_slc[:, sl]
             + gate[3 * h + 2:3 * h + 3] * o_win[:HEAD_DIM, sl])
        o_ref[:, h * LANE:(h + 1) * LANE] = _pad_feature_rows(o).T.astype(BF16)


def _overlap_t(s, nslc_pad):
    nc = s // CMP_STRIDE
    cmp_start = np.arange(nc) * CMP_STRIDE
    slc_start = np.arange(nslc_pad) * SLC_BLOCK
    ov = np.clip(np.minimum(cmp_start[:, None] + CMP_LEN, slc_start[None, :] + SLC_BLOCK)
                 - np.maximum(cmp_start[:, None], slc_start[None, :]), 0, None) / CMP_STRIDE
    ov[nc - CMP_LEN // CMP_STRIDE + 1:, :] = 0.0
    ov[:, s // SLC_BLOCK:] = 0.0
    return jnp.asarray(ov.T, BF16)


def _nsa_attention(qnt, kc, vct, ks, vst, kw, vwt, gates):
    s = qnt.shape[1]
    nc = s // CMP_STRIDE
    n_slc = s // SLC_BLOCK
    nslc_pad = -(-n_slc // LANE) * LANE
    tq = Q_TILE_NSA
    cols = NSA_HPG * tq
    once = pl.Buffered(1)
    res = pl.BlockSpec((s, LANE), lambda g, i: (0, g), pipeline_mode=once)
    return pl.pallas_call(
        functools.partial(_nsa_kernel, n_sel=min(SLC_TOPK, n_slc)),
        grid=(NSA_GROUPS, s // tq),
        in_specs=[pl.BlockSpec((NSA_HPG * LANE, tq), lambda g, i: (g, i)),
                  pl.BlockSpec((1, nc, LANE), lambda g, i: (g, 0, 0), pipeline_mode=once),
                  pl.BlockSpec((1, LANE, nc), lambda g, i: (g, 0, 0), pipeline_mode=once),
                  res, pl.BlockSpec((1, s // KV_TILE, V_ROWS, KV_TILE), lambda g, i: (g, 0, 0, 0),
                                    pipeline_mode=once),
                  res, pl.BlockSpec((1, s // LANE, LANE, LANE), lambda g, i: (g, 0, 0, 0), pipeline_mode=once),
                  pl.BlockSpec((1, GATE_ROWS, tq), lambda g, i: (g, 0, i)),
                  pl.BlockSpec((nslc_pad, nc), lambda g, i: (0, 0), pipeline_mode=once)],
        out_specs=pl.BlockSpec((tq, NSA_HPG * LANE), lambda g, i: (i, g)),
        out_shape=jax.ShapeDtypeStruct((s, NSA_HEADS * LANE), BF16),
        scratch_shapes=[pltpu.VMEM((nslc_pad // SUPER_BLOCKS, LANE, cols), BF16),
                        pltpu.VMEM((V_ROWS, cols), F32)],
        compiler_params=_cparams("parallel", "arbitrary"),
        name="nsa_attention",
    )(qnt, kc, vct, ks, vst, kw, vwt, gates, _overlap_t(s, nslc_pad))


def _flash_kernel(cfirst_ref, clast_ref, slack_ref, qt_ref, k_ref, vt_ref, o_ref, acc_ref, *, tq):
    h = pl.program_id(0)
    i = pl.program_id(1)
    tk = KV_TILE
    n_tiles = k_ref.shape[0] // tk
    acc_ref[...] = jnp.zeros(acc_ref.shape, F32)

    n_strips = tq // Q_STRIP
    per_q = tq // tk
    qas = [qt_ref[:, c * Q_STRIP:(c + 1) * Q_STRIP] for c in range(n_strips)]

    def tile_steps(j, d=None):
        k0 = pl.multiple_of(j * tk, tk)
        k_tile, vt = k_ref[pl.ds(k0, tk), :], vt_ref[0, j]
        steps = []
        for c in range(n_strips):
            mask = None
            if d is not None:
                if d * tk > (c + 1) * Q_STRIP - 1:
                    continue
                if (d + 1) * tk - 1 > c * Q_STRIP:
                    shp = (tk, Q_STRIP)
                    mask = _row_iota(shp) + d * tk <= _lane_iota(shp) + c * Q_STRIP
            steps.append((k_tile, qas[c], vt, c, mask))
        return steps

    def any_tile_masks(j):
        shp = (tk, Q_STRIP)
        return [_row_iota(shp) + j * tk <= _lane_iota(shp) + (i * tq + c * Q_STRIP) for c in range(n_strips)]

    def below(t, flat):
        jj = i - 1 - t
        bound = (slack_ref[0] + cfirst_ref[h * n_tiles + i * per_q]
                 - clast_ref[h * n_tiles + jj * per_q + per_q - 1])

        def run(flat):
            steps = [st for u in range(per_q) for st in tile_steps(jj * per_q + (per_q - 1 - u))]
            return _flat(_fast_steps(steps, _nest(flat), acc_ref))

        return lax.cond(bound >= -SKIP_MARGIN, run, lambda flat: flat, flat)

    diag0 = i * per_q
    own = [(c * Q_STRIP) // tk for c in range(n_strips)]
    m0 = tuple(_first_tile_max(k_ref[pl.ds(pl.multiple_of((diag0 + own[c]) * tk, tk), tk), :], [qas[c]],
                               [any_tile_masks(diag0 + own[c])[c]])[0] for c in range(n_strips))
    steps = [st for d in reversed(range(per_q)) for st in tile_steps(diag0 + d, d)]
    state = _fast_steps(steps, tuple((m, jnp.zeros_like(m)) for m in m0), acc_ref)
    state = _nest(lax.fori_loop(0, i, below, _flat(state)))
    worst = jnp.max(jnp.concatenate([w for _, w in state], axis=1))

    @pl.when(worst > EXP_GUARD)
    def _():
        acc_ref[...] = jnp.zeros(acc_ref.shape, F32)

        def exact(j, ms):
            k0 = pl.multiple_of(j * tk, tk)
            k_tile, vt, masks = k_ref[pl.ds(k0, tk), :], vt_ref[0, j], any_tile_masks(j)
            return _online_steps([(k_tile, qas[c], vt, c, masks[c]) for c in range(n_strips)], ms, acc_ref)

        lax.fori_loop(0, (i + 1) * per_q, exact, _m_init(n_strips))

    o = _pad_feature_rows(acc_ref[:HEAD_DIM] / acc_ref[SUM_ROW:SUM_ROW + 1])
    for c0 in range(0, tq, LANE):
        o_ref[c0:c0 + LANE, :] = o[:, c0:c0 + LANE].T.astype(BF16)


def _causal_attention(qt, k, vt, tq, bias_edges=None, slack=None):
    s, width = k.shape
    heads = width // LANE
    tq = min(tq, s)
    assert tq % KV_TILE == 0 and s % tq == 0
    n_tiles = s // KV_TILE
    if bias_edges is None:
        first = last = jnp.zeros((heads * n_tiles,), F32)
        slack = jnp.full((1,), -NEG, F32)
    else:
        first, last = (e.reshape(heads * n_tiles).astype(F32) for e in bias_edges)
    grid_spec = pltpu.PrefetchScalarGridSpec(
        num_scalar_prefetch=3,
        grid=(heads, s // tq),
        in_specs=[pl.BlockSpec((LANE, tq), lambda h, i, *_: (h, i)),
                  pl.BlockSpec((s, LANE), lambda h, i, *_: (0, h)),
                  pl.BlockSpec((1, n_tiles, V_ROWS, KV_TILE), lambda h, i, *_: (h, 0, 0, 0))],
        out_specs=pl.BlockSpec((tq, LANE), lambda h, i, *_: (i, h)),
        scratch_shapes=[pltpu.VMEM((V_ROWS, tq), F32)],
    )
    return pl.pallas_call(
        functools.partial(_flash_kernel, tq=tq),
        grid_spec=grid_spec,
        out_shape=jax.ShapeDtypeStruct((s, width), BF16),
        compiler_params=_cparams("parallel", "arbitrary"),
        name="causal_attention",
    )(first, last, slack.astype(F32), qt, k, vt)


def _out_proj_kernel(oa_ref, ob_ref, wa_ref, wb_ref, x_ref, g_ref, o_ref):
    y = jnp.dot(oa_ref[...], wa_ref[...], preferred_element_type=F32)
    y = y + jnp.dot(ob_ref[...], wb_ref[...], preferred_element_type=F32)
    o_ref[...] = x_ref[...] + g_ref[...] * y


def _out_proj(oa, ob, cola, colb, wa, wb, x2, gate):
    s, d = x2.shape
    ka = wa.shape[0]
    tm = ROW_TILE
    return pl.pallas_call(
        _out_proj_kernel,
        grid=(s // tm,),
        in_specs=[pl.BlockSpec((tm, ka), lambda i: (i, cola)), pl.BlockSpec((tm, ka), lambda i: (i, colb)),
                  pl.BlockSpec((ka, d), lambda i: (0, 0)), pl.BlockSpec((ka, d), lambda i: (0, 0)),
                  pl.BlockSpec((tm, d), lambda i: (i, 0)), pl.BlockSpec((1, d), lambda i: (0, 0))],
        out_specs=pl.BlockSpec((tm, d), lambda i: (i, 0)),
        out_shape=jax.ShapeDtypeStruct((s, d), F32),
        compiler_params=_cparams("parallel"),
        name="out_proj",
    )(oa, ob, wa, wb, x2, gate)


def _mla_prep_kernel(p_ref, pos_ref, inv_ref, gqa_ref, gkva_ref, wuq_ref, wuk_ref, wuv_ref,
                     gq_ref, gk_ref, gkr_ref, qt_ref, k_ref, vt_ref):
    shp = (PREP_TILE, LANE)
    lane = _lane_iota(shp)
    nope = lane < QK_NOPE
    rope = (lane >= QK_NOPE) & (lane < QK_NOPE + QK_ROPE)
    ref_ones = jnp.where((lane >= REF_ROW) & (lane < REF_ROW + 3), 1.0, 0.0)
    ang = pos_ref[...] * inv_ref[...]
    cos = jnp.where(rope, jnp.cos(ang), 1.0)
    sin = jnp.where(rope, jnp.sin(ang), 0.0)

    def rope32(x):
        half = QK_ROPE // 2
        rot = jnp.where(lane < QK_NOPE + half, -pltpu.roll(x, LANE - half, 1), pltpu.roll(x, half, 1))
        return x * cos + rot * sin

    def low_rank_norm(x, g):
        ms = jnp.mean(x * x, axis=-1, keepdims=True)
        return (x * lax.rsqrt(ms + EPS) * g).astype(BF16)

    nq = Q_LORA // LANE
    cq = low_rank_norm(p_ref[:, :Q_LORA], gqa_ref[...])
    ckv = low_rank_norm(p_ref[:, Q_LORA:Q_LORA + KV_LORA], gkva_ref[...])
    kr = p_ref[:, (nq + KV_LORA // LANE) * LANE:(nq + KV_LORA // LANE + 1) * LANE]
    k_rope = rope32(_head_rms(kr, gkr_ref[...], QK_ROPE))

    gq, gk = gq_ref[...], gk_ref[...]
    scale = (QK_NOPE + QK_ROPE) ** -0.5 * LOG2E
    pair = 2 * LANE
    for hp in range(MLA_HEADS // 2):
        cols = slice(hp * pair, (hp + 1) * pair)
        q2 = jnp.dot(cq, wuq_ref[:, cols], preferred_element_type=F32)
        k2 = jnp.dot(ckv, wuk_ref[:, cols], preferred_element_type=F32)
        v2 = jnp.dot(ckv, wuv_ref[:, cols], preferred_element_type=F32)
        for sub in range(2):
            head = 2 * hp + sub
            sl = slice(head * LANE, (head + 1) * LANE)
            half = slice(sub * LANE, (sub + 1) * LANE)
            x = q2[:, half]
            ss_n = jnp.sum(jnp.where(nope, x * x, 0.0), axis=-1, keepdims=True)
            ss_r = jnp.sum(jnp.where(rope, x * x, 0.0), axis=-1, keepdims=True)
            inv_rms = jnp.where(nope, lax.rsqrt(ss_n * (1.0 / QK_NOPE) + EPS),
                                lax.rsqrt(ss_r * (1.0 / QK_ROPE) + EPS))
            qt_ref[sl, :] = _t_bf16(rope32(x * inv_rms * gq) * scale)
            kn = _head_rms(k2[:, half], gk, QK_NOPE)
            k_ref[:, sl] = (kn + k_rope + ref_ones).astype(BF16)
            vt_ref[head, 0] = _t_bf16(jnp.where(lane == SUM_ROW, 1.0, v2[:, half]), V_ROWS)


def _mla_prep(proj, posf, inv128, gqa, gkva, wuq, wuk, wuv, gq, gk, gkr):
    s, n = proj.shape
    tm = PREP_TILE
    assert tm == KV_TILE

    def full(a):
        return pl.BlockSpec(a.shape, lambda i: (0, 0))

    outs = [_feat_major(MLA_HEADS, s, tm),
            (pl.BlockSpec((tm, MLA_HEADS * LANE), lambda i: (i, 0)),
             jax.ShapeDtypeStruct((s, MLA_HEADS * LANE), BF16)),
            _value_tiles(MLA_HEADS, s, tm, KV_TILE, V_ROWS)]
    args = (inv128, gqa, gkva, wuq, wuk, wuv, gq, gk, gkr)
    return pl.pallas_call(
        _mla_prep_kernel,
        grid=(s // tm,),
        in_specs=[pl.BlockSpec((tm, n), lambda i: (i, 0)), pl.BlockSpec((tm, 1), lambda i: (i, 0))]
                 + [full(a) for a in args],
        out_specs=[o[0] for o in outs],
        out_shape=[o[1] for o in outs],
        compiler_params=_cparams("parallel"),
        name="mla_prep",
    )(proj, posf, *args)


def _rank_lt(v, k):
    n = v.shape[0]
    row = _row_iota(v.shape)
    rank = jnp.zeros(v.shape, F32)
    for b in range(n):
        vb = v[b:b + 1, :]
        rank = rank + jnp.where((vb > v) | ((vb == v) & (row > b)), 1.0, 0.0)
    return rank < k


def _top_rows(v, k):
    rowf = _row_iota(v.shape).astype(F32)
    chosen = jnp.zeros(v.shape, F32)
    for _ in range(k):
        mx = jnp.max(v, axis=0, keepdims=True)
        idx = jnp.min(jnp.where(v == mx, rowf, float(v.shape[0])), axis=0, keepdims=True)
        hit = rowf == idx
        chosen = jnp.where(hit, 1.0, chosen)
        v = jnp.where(hit, -jnp.inf, v)
    return chosen > 0.0


def _moe_route_kernel(x_ref, g_ref, sc_ref, sh_ref, wr_ref, rb_ref, h_ref, pos_ref, wt_ref, cnt_ref):
    tm = ROW_TILE
    h = _norm_mod(x_ref[...], g_ref[...], sc_ref[...], sh_ref[...])
    h_ref[...] = h.astype(BF16)
    logits = jnp.dot(h, wr_ref[...], precision=HIGHEST, preferred_element_type=F32)
    lt = logits.T[:N_EXPERTS]
    scores = jax.nn.sigmoid(lt)
    sel = scores + rb_ref[...]

    per = N_EXPERTS // N_GROUPS
    grp = sel.reshape(N_GROUPS, per, tm)
    sub = lax.broadcasted_iota(jnp.int32, grp.shape, 1)
    m1 = jnp.max(grp, axis=1, keepdims=True)
    first = jnp.min(jnp.where(grp == m1, sub, per), axis=1, keepdims=True)
    m2 = jnp.max(jnp.where(sub == first, -jnp.inf, grp), axis=1, keepdims=True)
    gscore = (m1 + m2).reshape(N_GROUPS, tm)
    gmask = _rank_lt(gscore, TOPK_GROUPS)
    emask = jnp.broadcast_to(gmask.reshape(N_GROUPS, 1, tm), grp.shape).reshape(N_EXPERTS, tm)
    chosen = _top_rows(jnp.where(emask, sel, NEG), TOP_K)

    w = jnp.where(chosen, scores, 0.0)
    wt_ref[...] = w / jnp.sum(w, axis=0, keepdims=True) * ROUTED_SCALE

    upper = jnp.where(_row_iota((tm, tm)) <= _lane_iota((tm, tm)), 1.0, 0.0).astype(BF16)
    incl = jnp.dot(jnp.where(chosen, 1.0, 0.0).astype(BF16), upper, preferred_element_type=F32)
    pos_ref[...] = jnp.where(chosen, incl - 1.0, -1.0)
    cnt_ref[0] = jnp.broadcast_to(incl[:, tm - 1:tm], (N_EXPERTS, LANE))


def _moe_route(x2, g, sc, sh, w_router_pad, router_bias_col):
    s, d = x2.shape
    tm = ROW_TILE
    vec = pl.BlockSpec((1, d), lambda i: (0, 0))
    et = pl.BlockSpec((N_EXPERTS, tm), lambda i: (0, i))
    return pl.pallas_call(
        _moe_route_kernel,
        grid=(s // tm,),
        in_specs=[pl.BlockSpec((tm, d), lambda i: (i, 0)), vec, vec, vec,
                  pl.BlockSpec((d, LANE), lambda i: (0, 0)),
                  pl.BlockSpec((N_EXPERTS, 1), lambda i: (0, 0))],
        out_specs=[pl.BlockSpec((tm, d), lambda i: (i, 0)), et, et,
                   pl.BlockSpec((1, N_EXPERTS, LANE), lambda i: (i, 0, 0))],
        out_shape=[jax.ShapeDtypeStruct((s, d), BF16), jax.ShapeDtypeStruct((N_EXPERTS, s), F32),
                   jax.ShapeDtypeStruct((N_EXPERTS, s), F32),
                   jax.ShapeDtypeStruct((s // tm, N_EXPERTS, LANE), F32)],
        compiler_params=_cparams("parallel"),
        name="moe_route",
    )(x2, g, sc, sh, w_router_pad, router_bias_col)


def _moe_kernel(cnt_ref, x_ref, h_ref, pos_ref, wt_ref, wg_ref, wu_ref, wd_ref, sg_ref, su_ref, sd_ref,
                g2_ref, o_ref, acc_ref):
    i = pl.program_id(0)
    e = pl.program_id(1)
    tm = ROW_TILE
    r = MOE_CHUNK

    @pl.when(e == 0)
    def _():
        h = h_ref[...]
        a = jnp.dot(h, sg_ref[...], preferred_element_type=F32)
        a = a * jax.nn.sigmoid(a) * jnp.dot(h, su_ref[...], preferred_element_type=F32)
        acc_ref[...] = jnp.dot(a.astype(BF16), sd_ref[...], preferred_element_type=F32)

    first = e * MOE_EXPERTS_PER_STEP
    n = cnt_ref[i * N_EXPERTS + first]
    for k in range(1, MOE_EXPERTS_PER_STEP):
        n = jnp.maximum(n, cnt_ref[i * N_EXPERTS + first + k])
    prows = [pos_ref[pl.ds(first + k, 1), :] for k in range(MOE_EXPERTS_PER_STEP)]
    wrows = [wt_ref[pl.ds(first + k, 1), :] for k in range(MOE_EXPERTS_PER_STEP)]

    def chunk(c, _):
        slot = (_row_iota((r, tm)) + c * r).astype(F32)
        hits = [prow == slot for prow in prows]
        onehot = jnp.concatenate([jnp.where(hit, 1.0, 0.0).astype(BF16) for hit in hits], axis=0)
        xg = jnp.dot(onehot, h_ref[...], preferred_element_type=F32).astype(BF16)
        ys = []
        for k in range(MOE_EXPERTS_PER_STEP):
            xk = xg[k * r:(k + 1) * r]
            a = jnp.dot(xk, wg_ref[k], preferred_element_type=F32)
            a = a * jax.nn.sigmoid(a) * jnp.dot(xk, wu_ref[k], preferred_element_type=F32)
            y = jnp.dot(a.astype(BF16), wd_ref[k], preferred_element_type=F32)
            wr = jnp.sum(jnp.where(hits[k], wrows[k], 0.0), axis=-1, keepdims=True)
            ys.append((y * wr).astype(BF16))
        acc_ref[...] += _dot_tn(onehot, jnp.concatenate(ys, axis=0))
        return 0

    lax.fori_loop(0, (n + r - 1) // r, chunk, 0)

    @pl.when(e == N_EXPERTS // MOE_EXPERTS_PER_STEP - 1)
    def _():
        o_ref[...] = x_ref[...] + g2_ref[...] * acc_ref[...]


def _moe_experts(counts, x2, h, pos_t, w_t, wg, wu, wd, sg, su, sd, g2):
    s, d = x2.shape
    tm = ROW_TILE
    ff = wg.shape[2]
    tile = pl.BlockSpec((tm, d), lambda i, e, c: (i, 0))
    et = pl.BlockSpec((N_EXPERTS, tm), lambda i, e, c: (0, i))

    def const(a):
        return pl.BlockSpec(a.shape, lambda i, e, c: (0,) * a.ndim)

    per = MOE_EXPERTS_PER_STEP
    grid_spec = pltpu.PrefetchScalarGridSpec(
        num_scalar_prefetch=1,
        grid=(s // tm, N_EXPERTS // per),
        in_specs=[tile, tile, et, et,
                  pl.BlockSpec((per, d, ff), lambda i, e, c: (e, 0, 0)),
                  pl.BlockSpec((per, d, ff), lambda i, e, c: (e, 0, 0)),
                  pl.BlockSpec((per, ff, d), lambda i, e, c: (e, 0, 0)),
                  const(sg), const(su), const(sd), const(g2)],
        out_specs=tile,
        scratch_shapes=[pltpu.VMEM((tm, d), F32)],
    )
    return pl.pallas_call(
        _moe_kernel,
        grid_spec=grid_spec,
        out_shape=jax.ShapeDtypeStruct((s, d), F32),
        compiler_params=_cparams("parallel", "arbitrary"),
        name="moe_experts",
    )(counts, x2, h, pos_t, w_t, wg, wu, wd, sg, su, sd, g2)


def _pad_lanes(v, width=LANE, offset=0):
    out = jnp.zeros((1, width), F32)
    return out.at[0, offset:offset + v.shape[0]].set(v.astype(F32))


def _head_cols(w, n_heads, dim):
    d = w.shape[0]
    w3 = w.reshape(d, n_heads, dim)
    return jnp.pad(w3, ((0, 0), (0, 0), (0, LANE - dim))).reshape(d, n_heads * LANE)


def _hybrid_w_in(w_in):
    d = w_in.shape[0]
    nq = NSA_HEADS * HEAD_DIM
    nkv = 6 * NSA_GROUPS * HEAD_DIM
    ng = 3 * NSA_HEADS
    nf = 3 * FOX_HEADS * HEAD_DIM
    c0, c1, c2, c3 = nq, nq + nkv, nq + nkv + ng, nq + nkv + ng + nf
    gates = w_in[:, c1:c2].reshape(d, NSA_GROUPS, 3 * NSA_HPG)
    gates = jnp.pad(gates, ((0, 0), (0, 0), (0, LANE - 3 * NSA_HPG))).reshape(d, NSA_GROUPS * LANE)
    ff = jnp.pad(w_in[:, c3:], ((0, 0), (0, 2 * LANE - FOX_HEADS)))
    return jnp.concatenate([
        _head_cols(w_in[:, :c0], NSA_HEADS, HEAD_DIM),
        _head_cols(w_in[:, c0:c1], 6 * NSA_GROUPS, HEAD_DIM),
        _head_cols(w_in[:, c2:c3], 3 * FOX_HEADS, HEAD_DIM),
        gates, ff], axis=1).astype(BF16)


def _pad_head_rows(w, n_heads, dim):
    d = w.shape[1]
    w3 = w.reshape(n_heads, dim, d)
    return jnp.pad(w3, ((0, 0), (0, LANE - dim), (0, 0))).reshape(n_heads * LANE, d).astype(BF16)


def _rope_inv(dim, offset):
    inv = ROPE_THETA ** (-jnp.arange(0, dim, 2, dtype=F32) / dim)
    return _pad_lanes(jnp.concatenate([inv, inv]), offset=offset)


def _hybrid_mixer(x2, posf, mods, norm_g, w_in, fox_f_bias, nsa_q_norm, nsa_k_norm, nsa_cmp_pe, nsa_w_cmp,
                  fox_q_norm, fox_k_norm, w_out):
    sh1, sc1, g1 = mods
    (qnt, kct, vct, ks, vst, kw, vwt, gates, fqt, fk, fvt, cedge) = _hy_prep(
        x2, norm_g, sc1, sh1, _hybrid_w_in(w_in), posf, _rope_inv(HEAD_DIM, 0), _pad_lanes(nsa_q_norm),
        _pad_lanes(nsa_k_norm),
        _pad_lanes(fox_q_norm), _pad_lanes(fox_k_norm), _pad_lanes(fox_f_bias))
    kc, vc_t = _compress(kct, vct, nsa_w_cmp, nsa_cmp_pe, nsa_k_norm)
    o_a = _nsa_attention(qnt, kc, vc_t, ks, vst, kw, vwt, gates)
    slack = (2.0 * HEAD_DIM ** 0.5 * LOG2E) * jnp.max(jnp.abs(fox_q_norm)) * jnp.max(jnp.abs(fox_k_norm))
    edges = (cedge[:, 0, :FOX_HEADS].T, cedge[:, 1, :FOX_HEADS].T)
    o_b = _causal_attention(fqt, fk, fvt, Q_TILE_FOX, edges, slack.reshape(1))
    half = NSA_HEADS * HEAD_DIM
    wa = _pad_head_rows(w_out[:half], NSA_HEADS, HEAD_DIM)
    wb = _pad_head_rows(w_out[half:], FOX_HEADS, HEAD_DIM)
    return _out_proj(o_a, o_b, 0, 0, wa, wb, x2, g1)


def _mla_mixer(x2, posf, mods, norm_g, w_in, q_a_norm, kv_a_norm, w_uq, w_ukv, qn_norm, kn_norm, qr_norm,
               kr_norm, w_out):
    sh1, sc1, g1 = mods
    d = x2.shape[1]
    w_kr = jnp.zeros((d, LANE), F32).at[:, QK_NOPE:QK_NOPE + QK_ROPE].set(w_in[:, Q_LORA + KV_LORA:])
    w_in_p = jnp.concatenate([w_in[:, :Q_LORA + KV_LORA], w_kr], axis=1).astype(BF16)
    proj = _norm_mod_matmul(x2, norm_g, sc1, sh1, w_in_p, tn=w_in_p.shape[1])
    hq = QK_NOPE + QK_ROPE
    wuq = _head_cols(w_uq, MLA_HEADS, hq).astype(BF16)
    wkv3 = w_ukv.reshape(KV_LORA, MLA_HEADS, QK_NOPE + V_HEAD)
    wuk = _head_cols(wkv3[:, :, :QK_NOPE].reshape(KV_LORA, -1), MLA_HEADS, QK_NOPE).astype(BF16)
    wuv = _head_cols(wkv3[:, :, QK_NOPE:].reshape(KV_LORA, -1), MLA_HEADS, V_HEAD).astype(BF16)
    gq = _pad_lanes(jnp.concatenate([qn_norm, qr_norm]))
    qt, k, vt = _mla_prep(proj, posf, _rope_inv(QK_ROPE, QK_NOPE), q_a_norm.reshape(1, -1).astype(F32),
                          kv_a_norm.reshape(1, -1).astype(F32), wuq, wuk, wuv, gq, _pad_lanes(kn_norm),
                          _pad_lanes(kr_norm, offset=QK_NOPE))
    o = _causal_attention(qt, k, vt, Q_TILE_MLA)
    w_pad = _pad_head_rows(w_out, MLA_HEADS, V_HEAD)
    half = w_pad.shape[0] // 2
    return _out_proj(o, o, 0, 1, w_pad[:half], w_pad[half:], x2, g1)


def _moe_ffn(x2, mods, norm_g, w_router, router_bias, w_gate, w_up, w_down, ws_gate, ws_up, ws_down):
    sh2, sc2, g2 = mods
    w_r = jnp.pad(w_router.astype(F32), ((0, 0), (0, LANE - N_EXPERTS)))
    h, pos_t, w_t, cnt = _moe_route(x2, norm_g, sc2, sh2, w_r, router_bias.reshape(N_EXPERTS, 1).astype(F32))
    counts = cnt[:, :, 0].astype(jnp.int32).reshape(-1)
    return _moe_experts(counts, x2, h, pos_t, w_t, w_gate.astype(BF16), w_up.astype(BF16),
                        w_down.astype(BF16), ws_gate.astype(BF16), ws_up.astype(BF16), ws_down.astype(BF16), g2)


def kernel(x, c, positions, norm_attn, norm_ffn, w_ada, b_ada, hy_w_in, fox_f_bias, nsa_q_norm, nsa_k_norm, nsa_cmp_pe, nsa_w_cmp, fox_q_norm, fox_k_norm, hy_w_out, mla_w_in, mla_q_a_norm, mla_kv_a_norm, mla_w_uq, mla_w_ukv, mla_qn_norm, mla_kn_norm, mla_qr_norm, mla_kr_norm, mla_w_out, moe_w_router, moe_router_bias, moe_w_gate, moe_w_up, moe_w_down, moe_ws_gate, moe_ws_up, moe_ws_down):
    b, s, d = x.shape
    assert b == 1 and s % KV_TILE == 0 and s >= WINDOW + Q_TILE_NSA
    depth = w_ada.shape[0]
    x2 = x.reshape(s, d).astype(F32)
    posf = positions.reshape(s, 1).astype(F32)
    mod = _ada_mod(c.astype(F32), w_ada.astype(F32), b_ada.astype(F32))

    for layer in range(depth):
        m = [mod[layer, :, k * d:(k + 1) * d] for k in range(6)]
        i = layer // 2
        g_attn = norm_attn[layer].reshape(1, d).astype(F32)
        if layer % 2 == 0:
            x2 = _hybrid_mixer(x2, posf, m[0:3], g_attn, hy_w_in[i], fox_f_bias[i], nsa_q_norm[i],
                               nsa_k_norm[i], nsa_cmp_pe[i], nsa_w_cmp[i], fox_q_norm[i], fox_k_norm[i],
                               hy_w_out[i])
        else:
            x2 = _mla_mixer(x2, posf, m[0:3], g_attn, mla_w_in[i], mla_q_a_norm[i], mla_kv_a_norm[i],
                            mla_w_uq[i], mla_w_ukv[i], mla_qn_norm[i], mla_kn_norm[i], mla_qr_norm[i],
                            mla_kr_norm[i], mla_w_out[i])
        x2 = _moe_ffn(x2, m[3:6], norm_ffn[layer].reshape(1, d).astype(F32), moe_w_router[layer],
                      moe_router_bias[layer], moe_w_gate[layer], moe_w_up[layer], moe_w_down[layer],
                      moe_ws_gate[layer], moe_ws_up[layer], moe_ws_down[layer])
    return x2.reshape(b, s, d)
```

```python
import functools

import numpy as np
import jax
import jax.numpy as jnp
from jax import lax
from jax.experimental import pallas as pl
from jax.experimental.pallas import tpu as pltpu

F32 = jnp.float32
BF16 = jnp.bfloat16
HIGHEST = lax.Precision.HIGHEST

LANE = 128
VMEM_LIMIT_BYTES = 56 * 1024 * 1024

HEAD_DIM = 64
NSA_HEADS = 8
NSA_GROUPS = 2
NSA_HPG = NSA_HEADS // NSA_GROUPS
CMP_LEN = 32
CMP_STRIDE = 16
SLC_BLOCK = 64
SLC_TOPK = 16
WINDOW = 512
FOX_HEADS = 8
MLA_HEADS = 16
Q_LORA = 384
KV_LORA = 256
QK_NOPE = 64
QK_ROPE = 32
V_HEAD = 64
N_EXPERTS = 64
TOP_K = 8
N_GROUPS = 8
TOPK_GROUPS = 4
EXPERT_FF = 256
ROUTED_SCALE = 2.5
ROPE_THETA = 10000.0
EPS = 1e-6
NEG = -1e30
BIG = 1e6

ROW_TILE = 512
PREP_TILE = 512
GATE_ROWS = 16
Q_TILE_NSA = 256
KV_TILE = 512
Q_TILE_FOX = 1024
Q_TILE_MLA = 2048
Q_STRIP = 256
SUPER_BLOCKS = 32
NSA_PREFIX_PARTS = 4
SUM_ROW = 64
V_ROWS = 80
REF_ROW = 104
REF_SLAB = 96
EXP_GUARD = 100.0
SCORE_LOOKAHEAD = 3
SKIP_MARGIN = 160.0
LOG2E = 1.4426950408889634
MOE_CHUNK = 128
MOE_EXPERTS_PER_STEP = 8

HY_Q0 = 0
HY_KV0 = HY_Q0 + NSA_HEADS
HY_F0 = HY_KV0 + 6 * NSA_GROUPS
HY_G0 = HY_F0 + 3 * FOX_HEADS
HY_FF = HY_G0 + NSA_GROUPS
HY_BLOCKS = HY_FF + 2


def _cparams(*sem):
    return pltpu.CompilerParams(dimension_semantics=sem, vmem_limit_bytes=VMEM_LIMIT_BYTES)


def _lane_iota(shape):
    return lax.broadcasted_iota(jnp.int32, shape, len(shape) - 1)


def _row_iota(shape):
    return lax.broadcasted_iota(jnp.int32, shape, len(shape) - 2)


def _dot_tn(a, b):
    return lax.dot_general(a, b, (((0,), (0,)), ((), ())), preferred_element_type=F32)


def _ada_kernel(c_ref, w_ref, b_ref, o_ref):
    c = c_ref[...]
    cond = c * jax.nn.sigmoid(c)
    o_ref[0] = jnp.dot(cond, w_ref[0], precision=HIGHEST, preferred_element_type=F32) + b_ref[0]


def _ada_mod(c, w_ada, b_ada):
    depth, d, n = w_ada.shape
    tn = 768
    c8 = jnp.broadcast_to(c.reshape(1, d), (8, d))
    out = pl.pallas_call(
        _ada_kernel,
        grid=(depth, n // tn),
        in_specs=[pl.BlockSpec((8, d), lambda l, j: (0, 0)),
                  pl.BlockSpec((1, d, tn), lambda l, j: (l, 0, j)),
                  pl.BlockSpec((1, 1, tn), lambda l, j: (l, 0, j))],
        out_specs=pl.BlockSpec((1, 8, tn), lambda l, j: (l, 0, j)),
        out_shape=jax.ShapeDtypeStruct((depth, 8, n), F32),
        compiler_params=_cparams("parallel", "parallel"),
        name="ada_mod",
    )(c8, w_ada, b_ada.reshape(depth, 1, n))
    return out[:, 0:1, :]


def _norm_mod(x, g, sc, sh):
    ms = jnp.mean(x * x, axis=-1, keepdims=True)
    return (x * lax.rsqrt(ms + EPS) * g) * (1.0 + sc) + sh


def _head_rms(x, gain, n_real):
    ss = jnp.sum(x * x, axis=-1, keepdims=True)
    return x * lax.rsqrt(ss * (1.0 / n_real) + EPS) * gain


def _rope64(x, cos, sin):
    lane = _lane_iota(x.shape)
    rot = jnp.where(lane < 32, -pltpu.roll(x, LANE - 32, 1), pltpu.roll(x, 32, 1))
    return x * cos + rot * sin


def _t_bf16(x, rows=LANE):
    return x.T[:rows].astype(BF16)


def _split3(c):
    hi = c.astype(BF16).astype(F32)
    r1 = c - hi
    mid = r1.astype(BF16).astype(F32)
    lo = (r1 - mid).astype(BF16).astype(F32)
    return hi, mid, lo


def _hy_prep_kernel(x_ref, ng_ref, sc_ref, sh_ref, w_ref, pos_ref, inv_ref, gq_ref, gk_ref, gfq_ref, gfk_ref,
                    fb_ref, qnt_ref, kct_ref, vct_ref, ks_ref, vst_ref, kw_ref, vwt_ref, gate_ref,
                    fqt_ref, fk_ref, fvt_ref, cedge_ref, carry_ref):
    i = pl.program_id(0)
    tm = PREP_TILE
    shp = (tm, LANE)
    lane = _lane_iota(shp)

    act = _norm_mod(x_ref[...], ng_ref[...], sc_ref[...], sh_ref[...]).astype(BF16)
    pairs = {}

    def blk(b):
        if b // 2 not in pairs:
            cols = slice((b // 2) * 2 * LANE, (b // 2 + 1) * 2 * LANE)
            pairs[b // 2] = jnp.dot(act, w_ref[:, cols], preferred_element_type=F32)
        return pairs[b // 2][:, (b % 2) * LANE:(b % 2 + 1) * LANE]

    ang = pos_ref[...] * inv_ref[...]
    real = lane < HEAD_DIM
    cos = jnp.where(real, jnp.cos(ang), 1.0)
    sin = jnp.where(real, jnp.sin(ang), 0.0)
    gq, gk, gfq, gfk = gq_ref[...], gk_ref[...], gfq_ref[...], gfk_ref[...]
    scale = HEAD_DIM ** -0.5 * LOG2E
    ones_row = lane == SUM_ROW
    ref_ones = jnp.where((lane >= REF_ROW) & (lane < REF_ROW + 3), 1.0, 0.0)

    for h in range(NSA_HEADS):
        q = _rope64(_head_rms(blk(HY_Q0 + h), gq, HEAD_DIM), cos, sin) * scale
        qnt_ref[h * LANE:(h + 1) * LANE, :] = _t_bf16(q)

    row = _row_iota(shp) + i * tm
    onehot = jnp.where(lane - HEAD_DIM == ((row // SLC_BLOCK) % SUPER_BLOCKS), 1.0, 0.0)
    for g in range(NSA_GROUPS):
        def kv(r):
            return blk(HY_KV0 + r * NSA_GROUPS + g)
        sl = slice(g * LANE, (g + 1) * LANE)
        kct_ref[g] = _rope64(kv(0), cos, sin)[:, :HEAD_DIM].astype(BF16)
        vct_ref[g] = kv(1)[:, :HEAD_DIM].astype(BF16)
        ks = _rope64(_head_rms(kv(2), gk, HEAD_DIM), cos, sin)
        ks_ref[:, sl] = (ks + onehot + ref_ones).astype(BF16)
        vst_ref[g, 0] = _t_bf16(jnp.where(ones_row, 1.0, kv(3)), V_ROWS)
        kw_ref[:, sl] = _rope64(_head_rms(kv(4), gk, HEAD_DIM), cos, sin).astype(BF16)
        vwt = _t_bf16(kv(5))
        for cidx in range(tm // LANE):
            vwt_ref[g, cidx] = vwt[:, cidx * LANE:(cidx + 1) * LANE]
        gate_ref[g] = jax.nn.sigmoid(blk(HY_G0 + g)).T[:GATE_ROWS]

    @pl.when(i == 0)
    def _():
        carry_ref[...] = jnp.zeros_like(carry_ref)

    z = blk(HY_FF) + fb_ref[...]
    logf = jnp.minimum(z, 0.0) - jnp.log1p(jnp.exp(-jnp.abs(z)))
    tri = jnp.where(_row_iota((tm, tm)) >= _lane_iota((tm, tm)), 1.0, 0.0).astype(F32)
    cum = jnp.dot(tri, logf, precision=HIGHEST, preferred_element_type=F32) + carry_ref[...]
    carry_ref[...] = cum[tm - 1:tm, :]
    cedge_ref[0] = jnp.concatenate([cum[0:1] * LOG2E, cum[tm - 1:tm] * LOG2E, jnp.zeros((6, LANE), F32)], axis=0)

    for h in range(FOX_HEADS):
        c = jnp.broadcast_to(cum[:, h:h + 1], shp) * LOG2E
        hi, mid, lo = _split3(c)
        fq = _head_rms(blk(HY_F0 + h), gfq, HEAD_DIM) * scale
        fq = jnp.where(real, fq, jnp.where(lane == 64, hi, jnp.where(lane == 65, mid, jnp.where(
            lane == 66, lo, jnp.where(lane < 70, 1.0, 0.0)))))
        fk = _head_rms(blk(HY_F0 + FOX_HEADS + h), gfk, HEAD_DIM)
        fk = jnp.where(real, fk, jnp.where(lane < 67, 1.0, jnp.where(lane == 67, -hi, jnp.where(
            lane == 68, -mid, jnp.where(lane == 69, -lo, ref_ones)))))
        sl = slice(h * LANE, (h + 1) * LANE)
        fqt_ref[sl, :] = _t_bf16(fq)
        fk_ref[:, sl] = fk.astype(BF16)
        fvt_ref[h, 0] = _t_bf16(jnp.where(ones_row, 1.0, blk(HY_F0 + 2 * FOX_HEADS + h)), V_ROWS)


def _feat_major(heads, s, tm):
    return (pl.BlockSpec((heads * LANE, tm), lambda i: (0, i)),
            jax.ShapeDtypeStruct((heads * LANE, s), BF16))


def _value_tiles(heads, s, tm, tk, rows=LANE):
    return (pl.BlockSpec((heads, tm // tk, rows, tk), lambda i: (0, i, 0, 0)),
            jax.ShapeDtypeStruct((heads, s // tk, rows, tk), BF16))


def _pad_feature_rows(o):
    return jnp.concatenate([o, jnp.zeros((LANE - HEAD_DIM, o.shape[1]), o.dtype)], axis=0)


def _hy_prep(x2, norm_g, sc, sh, w_in, posf, inv128, gq, gk, gfq, gfk, fbias):
    s, d = x2.shape
    tm = PREP_TILE
    assert tm == KV_TILE and w_in.shape == (d, HY_BLOCKS * LANE) and HY_BLOCKS % 2 == 0
    vec = pl.BlockSpec((1, LANE), lambda i: (0, 0))
    dvec = pl.BlockSpec((1, d), lambda i: (0, 0))

    def rows(nb):
        return (pl.BlockSpec((tm, nb * LANE), lambda i: (i, 0)), jax.ShapeDtypeStruct((s, nb * LANE), BF16))

    tok = (pl.BlockSpec((NSA_GROUPS, tm, HEAD_DIM), lambda i: (0, i, 0)),
           jax.ShapeDtypeStruct((NSA_GROUPS, s, HEAD_DIM), BF16))
    gate = (pl.BlockSpec((NSA_GROUPS, GATE_ROWS, tm), lambda i: (0, 0, i)),
            jax.ShapeDtypeStruct((NSA_GROUPS, GATE_ROWS, s), F32))
    outs = [_feat_major(NSA_HEADS, s, tm), tok, tok, rows(NSA_GROUPS),
            _value_tiles(NSA_GROUPS, s, tm, KV_TILE, V_ROWS),
            rows(NSA_GROUPS), _value_tiles(NSA_GROUPS, s, tm, LANE), gate,
            _feat_major(FOX_HEADS, s, tm), rows(FOX_HEADS), _value_tiles(FOX_HEADS, s, tm, KV_TILE, V_ROWS),
            (pl.BlockSpec((1, 8, LANE), lambda i: (i, 0, 0)), jax.ShapeDtypeStruct((s // tm, 8, LANE), F32))]
    return pl.pallas_call(
        _hy_prep_kernel,
        grid=(s // tm,),
        in_specs=[pl.BlockSpec((tm, d), lambda i: (i, 0)), dvec, dvec, dvec,
                  pl.BlockSpec((d, HY_BLOCKS * LANE), lambda i: (0, 0), pipeline_mode=pl.Buffered(1)),
                  pl.BlockSpec((tm, 1), lambda i: (i, 0)), vec, vec, vec, vec, vec, vec],
        out_specs=[o[0] for o in outs],
        out_shape=[o[1] for o in outs],
        scratch_shapes=[pltpu.VMEM((1, LANE), F32)],
        compiler_params=_cparams("arbitrary"),
        name="hybrid_prep",
    )(x2, norm_g, sc, sh, w_in, posf, inv128, gq, gk, gfq, gfk, fbias)


def _compress_kernel(kc_ref, vc_ref, wk_ref, wv_ref, pek_ref, pev_ref, gk_ref, ko_ref, vo_ref):
    half = CMP_STRIDE * HEAD_DIM

    def comp(ch_ref, w_ref, pe_ref):
        ch = ch_ref[0]
        nc = ch.shape[0]
        a = jnp.dot(ch, w_ref[:half], preferred_element_type=F32)
        b = jnp.dot(ch, w_ref[half:], preferred_element_type=F32)
        nxt = pltpu.roll(b, nc - 1, 0)
        pe = jnp.dot(jnp.broadcast_to(pe_ref[...], (8, 2 * half)).astype(BF16), w_ref[...],
                     preferred_element_type=F32)[0:1]
        return a + nxt + pe

    ko_ref[0] = _head_rms(comp(kc_ref, wk_ref, pek_ref), gk_ref[...], HEAD_DIM).astype(BF16)
    vo_ref[0] = comp(vc_ref, wv_ref, pev_ref).T.astype(BF16)


def _compress(kct, vct, w_cmp, cmp_pe, k_norm):
    g, s, _ = kct.shape
    nc = s // CMP_STRIDE
    wide = CMP_STRIDE * HEAD_DIM
    kch = kct.reshape(g, nc, wide)
    vch = vct.reshape(g, nc, wide)
    w_pad = jnp.pad(w_cmp, ((0, 0), (0, 0), (0, LANE - HEAD_DIM))).astype(BF16)
    ch = pl.BlockSpec((1, nc, wide), lambda i: (i, 0, 0))
    wspec = pl.BlockSpec((2 * wide, LANE), lambda i: (0, 0))
    pespec = pl.BlockSpec((1, 2 * wide), lambda i: (0, 0))
    return pl.pallas_call(
        _compress_kernel,
        grid=(g,),
        in_specs=[ch, ch, wspec, wspec, pespec, pespec, pl.BlockSpec((1, LANE), lambda i: (0, 0))],
        out_specs=[pl.BlockSpec((1, nc, LANE), lambda i: (i, 0, 0)),
                   pl.BlockSpec((1, LANE, nc), lambda i: (i, 0, 0))],
        out_shape=[jax.ShapeDtypeStruct((g, nc, LANE), BF16), jax.ShapeDtypeStruct((g, LANE, nc), BF16)],
        compiler_params=_cparams("parallel"),
        name="nsa_compress",
    )(kch, vch, w_pad[0], w_pad[1], cmp_pe[0].reshape(1, 2 * wide).astype(F32),
      cmp_pe[1].reshape(1, 2 * wide).astype(F32), _pad_lanes(k_norm))


def _masked_softmax_t(s, mask):
    s = jnp.where(mask, s, NEG)
    m = jnp.max(s, axis=0, keepdims=True)
    e = jnp.exp2(s - m)
    inv = 1.0 / jnp.maximum(jnp.sum(e, axis=0, keepdims=True), 1e-30)
    return e, jnp.where(m > 0.5 * NEG, inv, 0.0)


def _online_steps(steps, ms, acc_ref):
    ms = list(ms)

    def scores(step):
        k_tile, qa, _, c, mask = step
        s = jnp.dot(k_tile, qa, preferred_element_type=F32)
        if mask is not None:
            s = jnp.where(mask, s, NEG)
        return s, jnp.max(s, axis=0, keepdims=True)

    nxt = scores(steps[0])
    for idx, (_, _, vt, c, _) in enumerate(steps):
        sl = slice(c * Q_STRIP, (c + 1) * Q_STRIP)
        s, s_max = nxt
        if idx + 1 < len(steps):
            nxt = scores(steps[idx + 1])
        m_new = jnp.maximum(ms[c], s_max)
        a = jnp.exp2(ms[c] - m_new)
        p = jnp.exp2((s - m_new).astype(BF16))
        ms[c] = m_new
        acc_ref[:, sl] = a * acc_ref[:, sl] + jnp.dot(vt, p, preferred_element_type=F32)
    return tuple(ms)


def _m_init(n_strips):
    return tuple(jnp.full((1, Q_STRIP), NEG, F32) for _ in range(n_strips))


def _with_ref_rows(qa, m):
    hi, mid, lo = _split3(-m)
    r = _row_iota((LANE - REF_SLAB, Q_STRIP)) + REF_SLAB
    slab = jnp.where(r == REF_ROW, hi, jnp.where(r == REF_ROW + 1, mid, jnp.where(r == REF_ROW + 2, lo, 0.0)))
    return jnp.concatenate([qa[:REF_SLAB], slab.astype(BF16)], axis=0)


def _first_tile_max(k_tile, qa_strips, masks):
    return tuple(jnp.max(jnp.where(mask, jnp.dot(k_tile, qa, preferred_element_type=F32), NEG), axis=0, keepdims=True)
                 for qa, mask in zip(qa_strips, masks))


def _fast_steps(steps, state, acc_ref):
    state = list(state)
    for k, step in enumerate(steps):
        assert all(prev[3] != step[3] for prev in steps[max(k - SCORE_LOOKAHEAD + 1, 0):k])

    def scores(step):
        k_tile, qa, _, c, mask = step
        s = jnp.dot(k_tile, _with_ref_rows(qa, state[c][0]), preferred_element_type=F32)
        if mask is not None:
            s = jnp.where(mask, s, NEG)
        return s

    ahead = [scores(st) for st in steps[:SCORE_LOOKAHEAD]]
    for idx, (_, _, vt, c, _) in enumerate(steps):
        sl = slice(c * Q_STRIP, (c + 1) * Q_STRIP)
        s = ahead.pop(0)
        m, worst = state[c]
        cm = jnp.max(s, axis=0, keepdims=True)
        inc = jnp.maximum(cm, 0.0)
        state[c] = (m + inc, jnp.maximum(worst, cm))
        if idx + SCORE_LOOKAHEAD < len(steps):
            ahead.append(scores(steps[idx + SCORE_LOOKAHEAD]))
        p = jnp.exp2(s).astype(BF16)
        acc_ref[:, sl] = jnp.exp2(-inc) * (acc_ref[:, sl] + jnp.dot(vt, p, preferred_element_type=F32))
    return tuple(state)


def _flat(state):
    return tuple(x for pair in state for x in pair)


def _nest(flat):
    return tuple((flat[2 * c], flat[2 * c + 1]) for c in range(len(flat) // 2))


def _nsa_kernel(qt_ref, kc_ref, vct_ref, ks_ref, vst_ref, kw_ref, vwt_ref, gate_ref, ovt_ref, o_ref,
                qaug_ref, acc_ref, *, n_sel):
    i = pl.program_id(1)
    tq = Q_TILE_NSA
    cols = NSA_HPG * tq
    qs = i * tq
    nc = kc_ref.shape[1]
    nslc = ovt_ref.shape[0]
    n_super = nslc // SUPER_BLOCKS

    qt = jnp.concatenate([qt_ref[h * LANE:(h + 1) * LANE, :] for h in range(NSA_HPG)], axis=1)
    tq_row = qs + (_lane_iota((1, cols)) % tq)

    def compress_and_select(n_c, n_b):
        s = jnp.dot(kc_ref[0, :n_c], qt, preferred_element_type=F32)
        cmp_end = _row_iota((n_c, 1)) * CMP_STRIDE + (CMP_LEN - 1)
        e, inv_l = _masked_softmax_t(s, cmp_end <= tq_row)
        o_cmp = jnp.dot(vct_ref[0, :, :n_c], e.astype(BF16), preferred_element_type=F32) * inv_l

        psum = e[:, 0:tq] * inv_l[:, 0:tq]
        for h in range(1, NSA_HPG):
            psum = psum + e[:, h * tq:(h + 1) * tq] * inv_l[:, h * tq:(h + 1) * tq]
        p_hi = psum.astype(BF16)
        p_lo = (psum - p_hi.astype(F32)).astype(BF16)
        ovt = ovt_ref[:n_b, :n_c]
        imp = (jnp.dot(ovt, p_hi, preferred_element_type=F32)
               + jnp.dot(ovt, p_lo, preferred_element_type=F32))

        jj = _row_iota((n_b, tq))
        tq_blk = qs + _lane_iota((n_b, tq))
        cur = tq_blk // SLC_BLOCK
        forced = (jj == 0) | (jj == cur) | (jj == cur - 1)
        causal_blk = jj * SLC_BLOCK <= tq_blk
        val = jnp.where(forced, imp + BIG, imp)
        val = jnp.where(causal_blk, val, NEG)
        jjf = jj.astype(F32)

        def pick(_, carry):
            val, sel = carry
            mx = jnp.max(val, axis=0, keepdims=True)
            idx = jnp.min(jnp.where(val == mx, jjf, float(n_b)), axis=0, keepdims=True)
            hit = jjf == idx
            return jnp.where(hit, -jnp.inf, val), jnp.where(hit, 1.0, sel)

        _, sel = lax.fori_loop(0, min(n_sel, n_b), pick, (val, jnp.zeros((n_b, tq), F32)))
        bias = jnp.where((sel > 0.0) & causal_blk, 0.0, NEG)
        if n_b < nslc:
            bias = jnp.concatenate([bias, jnp.full((nslc - n_b, tq), NEG, F32)], axis=0)
        return o_cmp, bias

    parts = max(1, min(NSA_PREFIX_PARTS, nc // (2 * LANE)))
    seq = nc * CMP_STRIDE

    def dispatch(k):
        full = lambda: compress_and_select(nc * k // parts, nslc * k // parts)
        if k == parts:
            return full()
        return lax.cond(qs + tq <= seq * k // parts, full, lambda: dispatch(k + 1))

    o_cmp, bias_t = dispatch(1)

    q_rows = qt[:HEAD_DIM].astype(F32)
    spare = jnp.zeros((LANE - HEAD_DIM - SUPER_BLOCKS, cols), F32)
    for st in range(n_super):
        b = bias_t[st * SUPER_BLOCKS:(st + 1) * SUPER_BLOCKS]
        b = jnp.concatenate([b] * NSA_HPG, axis=1)
        qaug_ref[st] = jnp.concatenate([q_rows, b, spare], axis=0).astype(BF16)

    tk = KV_TILE
    per_super = SUPER_BLOCKS * SLC_BLOCK // tk
    j_last = (qs + tq - 1) // tk
    n_strips = cols // Q_STRIP
    strips = [slice(c * Q_STRIP, (c + 1) * Q_STRIP) for c in range(n_strips)]

    def causal_masks(j):
        kpos = j * tk + _row_iota((tk, 1))
        return [kpos <= tq_row[:, sl] for sl in strips]

    def slc_steps(j, masks=None):
        k0 = pl.multiple_of(j * tk, tk)
        k_tile, vt, st = ks_ref[pl.ds(k0, tk), :], vst_ref[0, j], j // per_super
        return [(k_tile, qaug_ref[st, :, strips[c]], vt, c, None if masks is None else masks[c])
                for c in range(n_strips)]

    group = 4

    def grouped(jj, flat):
        steps = [st for t in range(group) for st in slc_steps(group * jj + t)]
        return _flat(_fast_steps(steps, _nest(flat), acc_ref))

    def single(j, flat):
        return _flat(_fast_steps(slc_steps(j), _nest(flat), acc_ref))

    acc_ref[...] = jnp.zeros(acc_ref.shape, F32)
    head_rows = 16
    head_masks = [_row_iota((head_rows, 1)) <= tq_row[:, sl] for sl in strips]
    m0 = _first_tile_max(ks_ref[0:head_rows, :], [qaug_ref[0, :, sl] for sl in strips], head_masks)
    n_groups = j_last // group
    flat = lax.fori_loop(0, n_groups, grouped, _flat(tuple((m, jnp.zeros_like(m)) for m in m0)))
    flat = lax.fori_loop(group * n_groups, j_last, single, flat)
    state = _fast_steps(slc_steps(j_last, causal_masks(j_last)), _nest(flat), acc_ref)
    worst = jnp.max(jnp.concatenate([w for _, w in state], axis=1))

    @pl.when(worst > EXP_GUARD)
    def _():
        acc_ref[...] = jnp.zeros(acc_ref.shape, F32)
        lax.fori_loop(0, j_last + 1, lambda j, ms: _online_steps(slc_steps(j, causal_masks(j)), ms, acc_ref),
                      _m_init(n_strips))

    o_slc = acc_ref[:HEAD_DIM] / jnp.maximum(acc_ref[SUM_ROW:SUM_ROW + 1], 1e-30)

    wlen = WINDOW + tq
    ws = pl.multiple_of(jnp.maximum(qs - WINDOW, 0), tq)
    s = jnp.dot(kw_ref[pl.ds(ws, wlen), :], qt, preferred_element_type=F32)
    dist = tq_row - (ws + _row_iota((wlen, 1)))
    e, inv_l = _masked_softmax_t(s, (dist >= 0) & (dist < WINDOW))
    e = e.astype(BF16)
    wb = ws // LANE
    o_win = jnp.zeros((LANE, cols), F32)
    for c in range(wlen // LANE):
        o_win = o_win + jnp.dot(vwt_ref[0, wb + c], e[c * LANE:(c + 1) * LANE], preferred_element_type=F32)
    o_win = o_win * inv_l

    gate = gate_ref[0]
    for h in range(NSA_HPG):
        sl = slice(h * tq, (h + 1) * tq)
        o = (gate[3 * h:3 * h + 1] * o_cmp[:HEAD_DIM, sl] + gate[3 * h + 1:3 * h + 2] * o_slc[:, sl]
             + gate[3 * h + 2:3 * h + 3] * o_win[:HEAD_DIM, sl])
        o_ref[:, h * LANE:(h + 1) * LANE] = _pad_feature_rows(o).T.astype(BF16)


def _overlap_t(s, nslc_pad):
    nc = s // CMP_STRIDE
    cmp_start = np.arange(nc) * CMP_STRIDE
    slc_start = np.arange(nslc_pad) * SLC_BLOCK
    ov = np.clip(np.minimum(cmp_start[:, None] + CMP_LEN, slc_start[None, :] + SLC_BLOCK)
                 - np.maximum(cmp_start[:, None], slc_start[None, :]), 0, None) / CMP_STRIDE
    ov[nc - CMP_LEN // CMP_STRIDE + 1:, :] = 0.0
    ov[:, s // SLC_BLOCK:] = 0.0
    return jnp.asarray(ov.T, BF16)


def _nsa_attention(qnt, kc, vct, ks, vst, kw, vwt, gates):
    s = qnt.shape[1]
    nc = s // CMP_STRIDE
    n_slc = s // SLC_BLOCK
    nslc_pad = -(-n_slc // LANE) * LANE
    tq = Q_TILE_NSA
    cols = NSA_HPG * tq
    once = pl.Buffered(1)
    res = pl.BlockSpec((s, LANE), lambda g, i: (0, g), pipeline_mode=once)
    return pl.pallas_call(
        functools.partial(_nsa_kernel, n_sel=min(SLC_TOPK, n_slc)),
        grid=(NSA_GROUPS, s // tq),
        in_specs=[pl.BlockSpec((NSA_HPG * LANE, tq), lambda g, i: (g, i)),
                  pl.BlockSpec((1, nc, LANE), lambda g, i: (g, 0, 0), pipeline_mode=once),
                  pl.BlockSpec((1, LANE, nc), lambda g, i: (g, 0, 0), pipeline_mode=once),
                  res, pl.BlockSpec((1, s // KV_TILE, V_ROWS, KV_TILE), lambda g, i: (g, 0, 0, 0),
                                    pipeline_mode=once),
                  res, pl.BlockSpec((1, s // LANE, LANE, LANE), lambda g, i: (g, 0, 0, 0), pipeline_mode=once),
                  pl.BlockSpec((1, GATE_ROWS, tq), lambda g, i: (g, 0, i)),
                  pl.BlockSpec((nslc_pad, nc), lambda g, i: (0, 0), pipeline_mode=once)],
        out_specs=pl.BlockSpec((tq, NSA_HPG * LANE), lambda g, i: (i, g)),
        out_shape=jax.ShapeDtypeStruct((s, NSA_HEADS * LANE), BF16),
        scratch_shapes=[pltpu.VMEM((nslc_pad // SUPER_BLOCKS, LANE, cols), BF16),
                        pltpu.VMEM((V_ROWS, cols), F32)],
        compiler_params=_cparams("parallel", "arbitrary"),
        name="nsa_attention",
    )(qnt, kc, vct, ks, vst, kw, vwt, gates, _overlap_t(s, nslc_pad))


def _flash_kernel(cfirst_ref, clast_ref, slack_ref, qt_ref, k_ref, vt_ref, o_ref, acc_ref, *, tq):
    h = pl.program_id(0)
    i = pl.program_id(1)
    tk = KV_TILE
    n_tiles = k_ref.shape[0] // tk
    acc_ref[...] = jnp.zeros(acc_ref.shape, F32)

    n_strips = tq // Q_STRIP
    per_q = tq // tk
    qas = [qt_ref[:, c * Q_STRIP:(c + 1) * Q_STRIP] for c in range(n_strips)]

    def tile_steps(j, d=None):
        k0 = pl.multiple_of(j * tk, tk)
        k_tile, vt = k_ref[pl.ds(k0, tk), :], vt_ref[0, j]
        steps = []
        for c in range(n_strips):
            mask = None
            if d is not None:
                if d * tk > (c + 1) * Q_STRIP - 1:
                    continue
                if (d + 1) * tk - 1 > c * Q_STRIP:
                    shp = (tk, Q_STRIP)
                    mask = _row_iota(shp) + d * tk <= _lane_iota(shp) + c * Q_STRIP
            steps.append((k_tile, qas[c], vt, c, mask))
        return steps

    def any_tile_masks(j):
        shp = (tk, Q_STRIP)
        return [_row_iota(shp) + j * tk <= _lane_iota(shp) + (i * tq + c * Q_STRIP) for c in range(n_strips)]

    def below(t, flat):
        jj = i - 1 - t
        bound = (slack_ref[0] + cfirst_ref[h * n_tiles + i * per_q]
                 - clast_ref[h * n_tiles + jj * per_q + per_q - 1])

        def run(flat):
            steps = [st for u in range(per_q) for st in tile_steps(jj * per_q + (per_q - 1 - u))]
            return _flat(_fast_steps(steps, _nest(flat), acc_ref))

        return lax.cond(bound >= -SKIP_MARGIN, run, lambda flat: flat, flat)

    diag0 = i * per_q
    own = [(c * Q_STRIP) // tk for c in range(n_strips)]
    m0 = tuple(_first_tile_max(k_ref[pl.ds(pl.multiple_of((diag0 + own[c]) * tk, tk), tk), :], [qas[c]],
                               [any_tile_masks(diag0 + own[c])[c]])[0] for c in range(n_strips))
    steps = [st for d in reversed(range(per_q)) for st in tile_steps(diag0 + d, d)]
    state = _fast_steps(steps, tuple((m, jnp.zeros_like(m)) for m in m0), acc_ref)
    state = _nest(lax.fori_loop(0, i, below, _flat(state)))
    worst = jnp.max(jnp.concatenate([w for _, w in state], axis=1))

    @pl.when(worst > EXP_GUARD)
    def _():
        acc_ref[...] = jnp.zeros(acc_ref.shape, F32)

        def exact(j, ms):
            k0 = pl.multiple_of(j * tk, tk)
            k_tile, vt, masks = k_ref[pl.ds(k0, tk), :], vt_ref[0, j], any_tile_masks(j)
            return _online_steps([(k_tile, qas[c], vt, c, masks[c]) for c in range(n_strips)], ms, acc_ref)

        lax.fori_loop(0, (i + 1) * per_q, exact, _m_init(n_strips))

    o = _pad_feature_rows(acc_ref[:HEAD_DIM] / acc_ref[SUM_ROW:SUM_ROW + 1])
    for c0 in range(0, tq, LANE):
        o_ref[c0:c0 + LANE, :] = o[:, c0:c0 + LANE].T.astype(BF16)


def _causal_attention(qt, k, vt, tq, bias_edges=None, slack=None):
    s, width = k.shape
    heads = width // LANE
    tq = min(tq, s)
    assert tq % KV_TILE == 0 and s % tq == 0
    n_tiles = s // KV_TILE
    if bias_edges is None:
        first = last = jnp.zeros((heads * n_tiles,), F32)
        slack = jnp.full((1,), -NEG, F32)
    else:
        first, last = (e.reshape(heads * n_tiles).astype(F32) for e in bias_edges)
    grid_spec = pltpu.PrefetchScalarGridSpec(
        num_scalar_prefetch=3,
        grid=(heads, s // tq),
        in_specs=[pl.BlockSpec((LANE, tq), lambda h, i, *_: (h, i)),
                  pl.BlockSpec((s, LANE), lambda h, i, *_: (0, h)),
                  pl.BlockSpec((1, n_tiles, V_ROWS, KV_TILE), lambda h, i, *_: (h, 0, 0, 0))],
        out_specs=pl.BlockSpec((tq, LANE), lambda h, i, *_: (i, h)),
        scratch_shapes=[pltpu.VMEM((V_ROWS, tq), F32)],
    )
    return pl.pallas_call(
        functools.partial(_flash_kernel, tq=tq),
        grid_spec=grid_spec,
        out_shape=jax.ShapeDtypeStruct((s, width), BF16),
        compiler_params=_cparams("parallel", "arbitrary"),
        name="causal_attention",
    )(first, last, slack.astype(F32), qt, k, vt)


def _out_proj_kernel(oa_ref, ob_ref, wa_ref, wb_ref, x_ref, g_ref, o_ref):
    y = jnp.dot(oa_ref[...], wa_ref[...], preferred_element_type=F32)
    y = y + jnp.dot(ob_ref[...], wb_ref[...], preferred_element_type=F32)
    o_ref[...] = x_ref[...] + g_ref[...] * y


def _out_proj(oa, ob, cola, colb, wa, wb, x2, gate):
    s, d = x2.shape
    ka = wa.shape[0]
    tm = ROW_TILE
    return pl.pallas_call(
        _out_proj_kernel,
        grid=(s // tm,),
        in_specs=[pl.BlockSpec((tm, ka), lambda i: (i, cola)), pl.BlockSpec((tm, ka), lambda i: (i, colb)),
                  pl.BlockSpec((ka, d), lambda i: (0, 0)), pl.BlockSpec((ka, d), lambda i: (0, 0)),
                  pl.BlockSpec((tm, d), lambda i: (i, 0)), pl.BlockSpec((1, d), lambda i: (0, 0))],
        out_specs=pl.BlockSpec((tm, d), lambda i: (i, 0)),
        out_shape=jax.ShapeDtypeStruct((s, d), F32),
        compiler_params=_cparams("parallel"),
        name="out_proj",
    )(oa, ob, wa, wb, x2, gate)


def _mla_prep_kernel(x_ref, ng_ref, sc_ref, sh_ref, win_ref, pos_ref, inv_ref, gqa_ref, gkva_ref, wuq_ref, wuk_ref,
                     wuv_ref, gq_ref, gk_ref, gkr_ref, qt_ref, k_ref, vt_ref):
    shp = (PREP_TILE, LANE)
    act = _norm_mod(x_ref[...], ng_ref[...], sc_ref[...], sh_ref[...]).astype(BF16)
    proj = jnp.dot(act, win_ref[...], preferred_element_type=F32)
    lane = _lane_iota(shp)
    nope = lane < QK_NOPE
    rope = (lane >= QK_NOPE) & (lane < QK_NOPE + QK_ROPE)
    ref_ones = jnp.where((lane >= REF_ROW) & (lane < REF_ROW + 3), 1.0, 0.0)
    ang = pos_ref[...] * inv_ref[...]
    cos = jnp.where(rope, jnp.cos(ang), 1.0)
    sin = jnp.where(rope, jnp.sin(ang), 0.0)

    def rope32(x):
        half = QK_ROPE // 2
        rot = jnp.where(lane < QK_NOPE + half, -pltpu.roll(x, LANE - half, 1), pltpu.roll(x, half, 1))
        return x * cos + rot * sin

    def low_rank_norm(x, g):
        ms = jnp.mean(x * x, axis=-1, keepdims=True)
        return (x * lax.rsqrt(ms + EPS) * g).astype(BF16)

    nq = Q_LORA // LANE
    cq = low_rank_norm(proj[:, :Q_LORA], gqa_ref[...])
    ckv = low_rank_norm(proj[:, Q_LORA:Q_LORA + KV_LORA], gkva_ref[...])
    kr = proj[:, (nq + KV_LORA // LANE) * LANE:(nq + KV_LORA // LANE + 1) * LANE]
    k_rope = rope32(_head_rms(kr, gkr_ref[...], QK_ROPE))

    gq, gk = gq_ref[...], gk_ref[...]
    scale = (QK_NOPE + QK_ROPE) ** -0.5 * LOG2E
    pair = 2 * LANE
    for hp in range(MLA_HEADS // 2):
        cols = slice(hp * pair, (hp + 1) * pair)
        q2 = jnp.dot(cq, wuq_ref[:, cols], preferred_element_type=F32)
        k2 = jnp.dot(ckv, wuk_ref[:, cols], preferred_element_type=F32)
        v2 = jnp.dot(ckv, wuv_ref[:, cols], preferred_element_type=F32)
        for sub in range(2):
            head = 2 * hp + sub
            sl = slice(head * LANE, (head + 1) * LANE)
            half = slice(sub * LANE, (sub + 1) * LANE)
            x = q2[:, half]
            ss_n = jnp.sum(jnp.where(nope, x * x, 0.0), axis=-1, keepdims=True)
            ss_r = jnp.sum(jnp.where(rope, x * x, 0.0), axis=-1, keepdims=True)
            inv_rms = jnp.where(nope, lax.rsqrt(ss_n * (1.0 / QK_NOPE) + EPS),
                                lax.rsqrt(ss_r * (1.0 / QK_ROPE) + EPS))
            qt_ref[sl, :] = _t_bf16(rope32(x * inv_rms * gq) * scale)
            kn = _head_rms(k2[:, half], gk, QK_NOPE)
            k_ref[:, sl] = (kn + k_rope + ref_ones).astype(BF16)
            vt_ref[head, 0] = _t_bf16(jnp.where(lane == SUM_ROW, 1.0, v2[:, half]), V_ROWS)


def _mla_prep(x2, norm_g, sc, sh, w_in, posf, inv128, gqa, gkva, wuq, wuk, wuv, gq, gk, gkr):
    s, d = x2.shape
    tm = PREP_TILE
    assert tm == KV_TILE

    def full(a):
        return pl.BlockSpec(a.shape, lambda i: (0, 0))

    outs = [_feat_major(MLA_HEADS, s, tm),
            (pl.BlockSpec((tm, MLA_HEADS * LANE), lambda i: (i, 0)),
             jax.ShapeDtypeStruct((s, MLA_HEADS * LANE), BF16)),
            _value_tiles(MLA_HEADS, s, tm, KV_TILE, V_ROWS)]
    head = (norm_g, sc, sh, w_in)
    tail = (inv128, gqa, gkva, wuq, wuk, wuv, gq, gk, gkr)
    return pl.pallas_call(
        _mla_prep_kernel,
        grid=(s // tm,),
        in_specs=[pl.BlockSpec((tm, d), lambda i: (i, 0))] + [full(a) for a in head]
                 + [pl.BlockSpec((tm, 1), lambda i: (i, 0))] + [full(a) for a in tail],
        out_specs=[o[0] for o in outs],
        out_shape=[o[1] for o in outs],
        compiler_params=_cparams("parallel"),
        name="mla_prep",
    )(x2, *head, posf, *tail)


def _rank_lt(v, k):
    n = v.shape[0]
    row = _row_iota(v.shape)
    rank = jnp.zeros(v.shape, F32)
    for b in range(n):
        vb = v[b:b + 1, :]
        rank = rank + jnp.where((vb > v) | ((vb == v) & (row > b)), 1.0, 0.0)
    return rank < k


def _top_rows(v, k):
    rowf = _row_iota(v.shape).astype(F32)
    chosen = jnp.zeros(v.shape, F32)
    for _ in range(k):
        mx = jnp.max(v, axis=0, keepdims=True)
        idx = jnp.min(jnp.where(v == mx, rowf, float(v.shape[0])), axis=0, keepdims=True)
        hit = rowf == idx
        chosen = jnp.where(hit, 1.0, chosen)
        v = jnp.where(hit, -jnp.inf, v)
    return chosen > 0.0


def _moe_route_kernel(x_ref, g_ref, sc_ref, sh_ref, wr_ref, rb_ref, h_ref, pos_ref, wt_ref, cnt_ref):
    tm = ROW_TILE
    h = _norm_mod(x_ref[...], g_ref[...], sc_ref[...], sh_ref[...])
    h_ref[...] = h.astype(BF16)
    logits = jnp.dot(h, wr_ref[...], precision=HIGHEST, preferred_element_type=F32)
    lt = logits.T[:N_EXPERTS]
    scores = jax.nn.sigmoid(lt)
    sel = scores + rb_ref[...]

    per = N_EXPERTS // N_GROUPS
    grp = sel.reshape(N_GROUPS, per, tm)
    sub = lax.broadcasted_iota(jnp.int32, grp.shape, 1)
    m1 = jnp.max(grp, axis=1, keepdims=True)
    first = jnp.min(jnp.where(grp == m1, sub, per), axis=1, keepdims=True)
    m2 = jnp.max(jnp.where(sub == first, -jnp.inf, grp), axis=1, keepdims=True)
    gscore = (m1 + m2).reshape(N_GROUPS, tm)
    gmask = _rank_lt(gscore, TOPK_GROUPS)
    emask = jnp.broadcast_to(gmask.reshape(N_GROUPS, 1, tm), grp.shape).reshape(N_EXPERTS, tm)
    chosen = _top_rows(jnp.where(emask, sel, NEG), TOP_K)

    w = jnp.where(chosen, scores, 0.0)
    wt_ref[...] = w / jnp.sum(w, axis=0, keepdims=True) * ROUTED_SCALE

    upper = jnp.where(_row_iota((tm, tm)) <= _lane_iota((tm, tm)), 1.0, 0.0).astype(BF16)
    incl = jnp.dot(jnp.where(chosen, 1.0, 0.0).astype(BF16), upper, preferred_element_type=F32)
    pos_ref[...] = jnp.where(chosen, incl - 1.0, -1.0)
    cnt_ref[0] = jnp.broadcast_to(incl[:, tm - 1:tm], (N_EXPERTS, LANE))


def _moe_route(x2, g, sc, sh, w_router_pad, router_bias_col):
    s, d = x2.shape
    tm = ROW_TILE
    vec = pl.BlockSpec((1, d), lambda i: (0, 0))
    et = pl.BlockSpec((N_EXPERTS, tm), lambda i: (0, i))
    return pl.pallas_call(
        _moe_route_kernel,
        grid=(s // tm,),
        in_specs=[pl.BlockSpec((tm, d), lambda i: (i, 0)), vec, vec, vec,
                  pl.BlockSpec((d, LANE), lambda i: (0, 0)),
                  pl.BlockSpec((N_EXPERTS, 1), lambda i: (0, 0))],
        out_specs=[pl.BlockSpec((tm, d), lambda i: (i, 0)), et, et,
                   pl.BlockSpec((1, N_EXPERTS, LANE), lambda i: (i, 0, 0))],
        out_shape=[jax.ShapeDtypeStruct((s, d), BF16), jax.ShapeDtypeStruct((N_EXPERTS, s), F32),
                   jax.ShapeDtypeStruct((N_EXPERTS, s), F32),
                   jax.ShapeDtypeStruct((s // tm, N_EXPERTS, LANE), F32)],
        compiler_params=_cparams("parallel"),
        name="moe_route",
    )(x2, g, sc, sh, w_router_pad, router_bias_col)


def _moe_kernel(cnt_ref, x_ref, h_ref, pos_ref, wt_ref, wg_ref, wu_ref, wd_ref, sg_ref, su_ref, sd_ref,
                g2_ref, o_ref, acc_ref):
    i = pl.program_id(0)
    e = pl.program_id(1)
    tm = ROW_TILE
    r = MOE_CHUNK

    @pl.when(e == 0)
    def _():
        h = h_ref[...]
        a = jnp.dot(h, sg_ref[...], preferred_element_type=F32)
        a = a * jax.nn.sigmoid(a) * jnp.dot(h, su_ref[...], preferred_element_type=F32)
        acc_ref[...] = jnp.dot(a.astype(BF16), sd_ref[...], preferred_element_type=F32)

    first = e * MOE_EXPERTS_PER_STEP
    n = cnt_ref[i * N_EXPERTS + first]
    for k in range(1, MOE_EXPERTS_PER_STEP):
        n = jnp.maximum(n, cnt_ref[i * N_EXPERTS + first + k])
    prows = [pos_ref[pl.ds(first + k, 1), :] for k in range(MOE_EXPERTS_PER_STEP)]
    wrows = [wt_ref[pl.ds(first + k, 1), :] for k in range(MOE_EXPERTS_PER_STEP)]

    def chunk(c, _):
        slot = (_row_iota((r, tm)) + c * r).astype(F32)
        hits = [prow == slot for prow in prows]
        onehot = jnp.concatenate([jnp.where(hit, 1.0, 0.0).astype(BF16) for hit in hits], axis=0)
        xg = jnp.dot(onehot, h_ref[...], preferred_element_type=F32).astype(BF16)
        ys = []
        for k in range(MOE_EXPERTS_PER_STEP):
            xk = xg[k * r:(k + 1) * r]
            a = jnp.dot(xk, wg_ref[k], preferred_element_type=F32)
            a = a * jax.nn.sigmoid(a) * jnp.dot(xk, wu_ref[k], preferred_element_type=F32)
            y = jnp.dot(a.astype(BF16), wd_ref[k], preferred_element_type=F32)
            wr = jnp.sum(jnp.where(hits[k], wrows[k], 0.0), axis=-1, keepdims=True)
            ys.append((y * wr).astype(BF16))
        acc_ref[...] += _dot_tn(onehot, jnp.concatenate(ys, axis=0))
        return 0

    lax.fori_loop(0, (n + r - 1) // r, chunk, 0)

    @pl.when(e == N_EXPERTS // MOE_EXPERTS_PER_STEP - 1)
    def _():
        o_ref[...] = x_ref[...] + g2_ref[...] * acc_ref[...]


def _moe_experts(counts, x2, h, pos_t, w_t, wg, wu, wd, sg, su, sd, g2):
    s, d = x2.shape
    tm = ROW_TILE
    ff = wg.shape[2]
    tile = pl.BlockSpec((tm, d), lambda i, e, c: (i, 0))
    et = pl.BlockSpec((N_EXPERTS, tm), lambda i, e, c: (0, i))

    def const(a):
        return pl.BlockSpec(a.shape, lambda i, e, c: (0,) * a.ndim)

    per = MOE_EXPERTS_PER_STEP
    grid_spec = pltpu.PrefetchScalarGridSpec(
        num_scalar_prefetch=1,
        grid=(s // tm, N_EXPERTS // per),
        in_specs=[tile, tile, et, et,
                  pl.BlockSpec((per, d, ff), lambda i, e, c: (e, 0, 0)),
                  pl.BlockSpec((per, d, ff), lambda i, e, c: (e, 0, 0)),
                  pl.BlockSpec((per, ff, d), lambda i, e, c: (e, 0, 0)),
                  const(sg), const(su), const(sd), const(g2)],
        out_specs=tile,
        scratch_shapes=[pltpu.VMEM((tm, d), F32)],
    )
    return pl.pallas_call(
        _moe_kernel,
        grid_spec=grid_spec,
        out_shape=jax.ShapeDtypeStruct((s, d), F32),
        compiler_params=_cparams("parallel", "arbitrary"),
        name="moe_experts",
    )(counts, x2, h, pos_t, w_t, wg, wu, wd, sg, su, sd, g2)


def _pad_lanes(v, width=LANE, offset=0):
    out = jnp.zeros((1, width), F32)
    return out.at[0, offset:offset + v.shape[0]].set(v.astype(F32))


def _head_cols(w, n_heads, dim):
    d = w.shape[0]
    w3 = w.reshape(d, n_heads, dim)
    return jnp.pad(w3, ((0, 0), (0, 0), (0, LANE - dim))).reshape(d, n_heads * LANE)


def _hybrid_w_in(w_in):
    d = w_in.shape[0]
    nq = NSA_HEADS * HEAD_DIM
    nkv = 6 * NSA_GROUPS * HEAD_DIM
    ng = 3 * NSA_HEADS
    nf = 3 * FOX_HEADS * HEAD_DIM
    c0, c1, c2, c3 = nq, nq + nkv, nq + nkv + ng, nq + nkv + ng + nf
    gates = w_in[:, c1:c2].reshape(d, NSA_GROUPS, 3 * NSA_HPG)
    gates = jnp.pad(gates, ((0, 0), (0, 0), (0, LANE - 3 * NSA_HPG))).reshape(d, NSA_GROUPS * LANE)
    ff = jnp.pad(w_in[:, c3:], ((0, 0), (0, 2 * LANE - FOX_HEADS)))
    return jnp.concatenate([
        _head_cols(w_in[:, :c0], NSA_HEADS, HEAD_DIM),
        _head_cols(w_in[:, c0:c1], 6 * NSA_GROUPS, HEAD_DIM),
        _head_cols(w_in[:, c2:c3], 3 * FOX_HEADS, HEAD_DIM),
        gates, ff], axis=1).astype(BF16)


def _pad_head_rows(w, n_heads, dim):
    d = w.shape[1]
    w3 = w.reshape(n_heads, dim, d)
    return jnp.pad(w3, ((0, 0), (0, LANE - dim), (0, 0))).reshape(n_heads * LANE, d).astype(BF16)


def _rope_inv(dim, offset):
    inv = ROPE_THETA ** (-jnp.arange(0, dim, 2, dtype=F32) / dim)
    return _pad_lanes(jnp.concatenate([inv, inv]), offset=offset)


def _hybrid_mixer(x2, posf, mods, norm_g, w_in, fox_f_bias, nsa_q_norm, nsa_k_norm, nsa_cmp_pe, nsa_w_cmp,
                  fox_q_norm, fox_k_norm, w_out):
    sh1, sc1, g1 = mods
    (qnt, kct, vct, ks, vst, kw, vwt, gates, fqt, fk, fvt, cedge) = _hy_prep(
        x2, norm_g, sc1, sh1, _hybrid_w_in(w_in), posf, _rope_inv(HEAD_DIM, 0), _pad_lanes(nsa_q_norm),
        _pad_lanes(nsa_k_norm),
        _pad_lanes(fox_q_norm), _pad_lanes(fox_k_norm), _pad_lanes(fox_f_bias))
    kc, vc_t = _compress(kct, vct, nsa_w_cmp, nsa_cmp_pe, nsa_k_norm)
    o_a = _nsa_attention(qnt, kc, vc_t, ks, vst, kw, vwt, gates)
    slack = (2.0 * HEAD_DIM ** 0.5 * LOG2E) * jnp.max(jnp.abs(fox_q_norm)) * jnp.max(jnp.abs(fox_k_norm))
    edges = (cedge[:, 0, :FOX_HEADS].T, cedge[:, 1, :FOX_HEADS].T)
    o_b = _causal_attention(fqt, fk, fvt, Q_TILE_FOX, edges, slack.reshape(1))
    half = NSA_HEADS * HEAD_DIM
    wa = _pad_head_rows(w_out[:half], NSA_HEADS, HEAD_DIM)
    wb = _pad_head_rows(w_out[half:], FOX_HEADS, HEAD_DIM)
    return _out_proj(o_a, o_b, 0, 0, wa, wb, x2, g1)


def _mla_mixer(x2, posf, mods, norm_g, w_in, q_a_norm, kv_a_norm, w_uq, w_ukv, qn_norm, kn_norm, qr_norm,
               kr_norm, w_out):
    sh1, sc1, g1 = mods
    d = x2.shape[1]
    w_kr = jnp.zeros((d, LANE), F32).at[:, QK_NOPE:QK_NOPE + QK_ROPE].set(w_in[:, Q_LORA + KV_LORA:])
    w_in_p = jnp.concatenate([w_in[:, :Q_LORA + KV_LORA], w_kr], axis=1).astype(BF16)
    hq = QK_NOPE + QK_ROPE
    wuq = _head_cols(w_uq, MLA_HEADS, hq).astype(BF16)
    wkv3 = w_ukv.reshape(KV_LORA, MLA_HEADS, QK_NOPE + V_HEAD)
    wuk = _head_cols(wkv3[:, :, :QK_NOPE].reshape(KV_LORA, -1), MLA_HEADS, QK_NOPE).astype(BF16)
    wuv = _head_cols(wkv3[:, :, QK_NOPE:].reshape(KV_LORA, -1), MLA_HEADS, V_HEAD).astype(BF16)
    gq = _pad_lanes(jnp.concatenate([qn_norm, qr_norm]))
    qt, k, vt = _mla_prep(x2, norm_g, sc1, sh1, w_in_p, posf, _rope_inv(QK_ROPE, QK_NOPE),
                          q_a_norm.reshape(1, -1).astype(F32),
                          kv_a_norm.reshape(1, -1).astype(F32), wuq, wuk, wuv, gq, _pad_lanes(kn_norm),
                          _pad_lanes(kr_norm, offset=QK_NOPE))
    o = _causal_attention(qt, k, vt, Q_TILE_MLA)
    w_pad = _pad_head_rows(w_out, MLA_HEADS, V_HEAD)
    half = w_pad.shape[0] // 2
    return _out_proj(o, o, 0, 1, w_pad[:half], w_pad[half:], x2, g1)


def _moe_ffn(x2, mods, norm_g, w_router, router_bias, w_gate, w_up, w_down, ws_gate, ws_up, ws_down):
    sh2, sc2, g2 = mods
    w_r = jnp.pad(w_router.astype(F32), ((0, 0), (0, LANE - N_EXPERTS)))
    h, pos_t, w_t, cnt = _moe_route(x2, norm_g, sc2, sh2, w_r, router_bias.reshape(N_EXPERTS, 1).astype(F32))
    counts = cnt[:, :, 0].astype(jnp.int32).reshape(-1)
    return _moe_experts(counts, x2, h, pos_t, w_t, w_gate.astype(BF16), w_up.astype(BF16),
                        w_down.astype(BF16), ws_gate.astype(BF16), ws_up.astype(BF16), ws_down.astype(BF16), g2)


def kernel(x, c, positions, norm_attn, norm_ffn, w_ada, b_ada, hy_w_in, fox_f_bias, nsa_q_norm, nsa_k_norm, nsa_cmp_pe, nsa_w_cmp, fox_q_norm, fox_k_norm, hy_w_out, mla_w_in, mla_q_a_norm, mla_kv_a_norm, mla_w_uq, mla_w_ukv, mla_qn_norm, mla_kn_norm, mla_qr_norm, mla_kr_norm, mla_w_out, moe_w_router, moe_router_bias, moe_w_gate, moe_w_up, moe_w_down, moe_ws_gate, moe_ws_up, moe_ws_down):
    b, s, d = x.shape
    assert b == 1 and s % KV_TILE == 0 and s >= WINDOW + Q_TILE_NSA
    depth = w_ada.shape[0]
    x2 = x.reshape(s, d).astype(F32)
    posf = positions.reshape(s, 1).astype(F32)
    mod = _ada_mod(c.astype(F32), w_ada.astype(F32), b_ada.astype(F32))

    for layer in range(depth):
        m = [mod[layer, :, k * d:(k + 1) * d] for k in range(6)]
        i = layer // 2
        g_attn = norm_attn[layer].reshape(1, d).astype(F32)
        if layer % 2 == 0:
            x2 = _hybrid_mixer(x2, posf, m[0:3], g_attn, hy_w_in[i], fox_f_bias[i], nsa_q_norm[i],
                               nsa_k_norm[i], nsa_cmp_pe[i], nsa_w_cmp[i], fox_q_norm[i], fox_k_norm[i],
                               hy_w_out[i])
        else:
            x2 = _mla_mixer(x2, posf, m[0:3], g_attn, mla_w_in[i], mla_q_a_norm[i], mla_kv_a_norm[i],
                            mla_w_uq[i], mla_w_ukv[i], mla_qn_norm[i], mla_kn_norm[i], mla_qr_norm[i],
                            mla_kr_norm[i], mla_w_out[i])
        x2 = _moe_ffn(x2, m[3:6], norm_ffn[layer].reshape(1, d).astype(F32), moe_w_router[layer],
                      moe_router_bias[layer], moe_w_gate[layer], moe_w_up[layer], moe_w_down[layer],
                      moe_ws_gate[layer], moe_ws_up[layer], moe_ws_down[layer])
    return x2.reshape(b, s, d)
```

```python
import functools

import numpy as np
import jax
import jax.numpy as jnp
from jax import lax
from jax.experimental import pallas as pl
from jax.experimental.pallas import tpu as pltpu

F32 = jnp.float32
BF16 = jnp.bfloat16
HIGHEST = lax.Precision.HIGHEST

LANE = 128
VMEM_LIMIT_BYTES = 56 * 1024 * 1024

HEAD_DIM = 64
NSA_HEADS = 8
NSA_GROUPS = 2
NSA_HPG = NSA_HEADS // NSA_GROUPS
CMP_LEN = 32
CMP_STRIDE = 16
SLC_BLOCK = 64
SLC_TOPK = 16
WINDOW = 512
FOX_HEADS = 8
MLA_HEADS = 16
Q_LORA = 384
KV_LORA = 256
QK_NOPE = 64
QK_ROPE = 32
V_HEAD = 64
N_EXPERTS = 64
TOP_K = 8
N_GROUPS = 8
TOPK_GROUPS = 4
EXPERT_FF = 256
ROUTED_SCALE = 2.5
ROPE_THETA = 10000.0
EPS = 1e-6
NEG = -1e30
BIG = 1e6

ROW_TILE = 512
PREP_TILE = 512
GATE_ROWS = 16
Q_TILE_NSA = 256
KV_TILE = 512
Q_TILE_FOX = 1024
Q_TILE_MLA = 2048
Q_STRIP = 256
SUPER_BLOCKS = 32
NSA_PREFIX_PARTS = 4
SUM_ROW = 64
V_ROWS = 80
REF_ROW = 104
REF_SLAB = 96
EXP_GUARD = 100.0
SCORE_LOOKAHEAD = 3
SKIP_MARGIN = 160.0
LOG2E = 1.4426950408889634
MOE_CHUNK = 128
MOE_EXPERTS_PER_STEP = 8

HY_Q0 = 0
HY_KV0 = HY_Q0 + NSA_HEADS
HY_F0 = HY_KV0 + 6 * NSA_GROUPS
HY_G0 = HY_F0 + 3 * FOX_HEADS
HY_FF = HY_G0 + NSA_GROUPS
HY_BLOCKS = HY_FF + 2


def _cparams(*sem):
    return pltpu.CompilerParams(dimension_semantics=sem, vmem_limit_bytes=VMEM_LIMIT_BYTES)


def _lane_iota(shape):
    return lax.broadcasted_iota(jnp.int32, shape, len(shape) - 1)


def _row_iota(shape):
    return lax.broadcasted_iota(jnp.int32, shape, len(shape) - 2)


def _dot_tn(a, b):
    return lax.dot_general(a, b, (((0,), (0,)), ((), ())), preferred_element_type=F32)


def _ada_kernel(c_ref, w_ref, b_ref, o_ref):
    c = c_ref[...]
    cond = c * jax.nn.sigmoid(c)
    o_ref[0] = jnp.dot(cond, w_ref[0], precision=HIGHEST, preferred_element_type=F32) + b_ref[0]


def _ada_mod(c, w_ada, b_ada):
    depth, d, n = w_ada.shape
    tn = 768
    c8 = jnp.broadcast_to(c.reshape(1, d), (8, d))
    out = pl.pallas_call(
        _ada_kernel,
        grid=(depth, n // tn),
        in_specs=[pl.BlockSpec((8, d), lambda l, j: (0, 0)),
                  pl.BlockSpec((1, d, tn), lambda l, j: (l, 0, j)),
                  pl.BlockSpec((1, 1, tn), lambda l, j: (l, 0, j))],
        out_specs=pl.BlockSpec((1, 8, tn), lambda l, j: (l, 0, j)),
        out_shape=jax.ShapeDtypeStruct((depth, 8, n), F32),
        compiler_params=_cparams("parallel", "parallel"),
        name="ada_mod",
    )(c8, w_ada, b_ada.reshape(depth, 1, n))
    return out[:, 0:1, :]


def _norm_mod(x, g, sc, sh):
    ms = jnp.mean(x * x, axis=-1, keepdims=True)
    return (x * lax.rsqrt(ms + EPS) * g) * (1.0 + sc) + sh


def _head_rms(x, gain, n_real):
    ss = jnp.sum(x * x, axis=-1, keepdims=True)
    return x * lax.rsqrt(ss * (1.0 / n_real) + EPS) * gain


def _rope64(x, cos, sin):
    lane = _lane_iota(x.shape)
    rot = jnp.where(lane < 32, -pltpu.roll(x, LANE - 32, 1), pltpu.roll(x, 32, 1))
    return x * cos + rot * sin


def _t_bf16(x, rows=LANE):
    return x.T[:rows].astype(BF16)


def _split3(c):
    hi = c.astype(BF16).astype(F32)
    r1 = c - hi
    mid = r1.astype(BF16).astype(F32)
    lo = (r1 - mid).astype(BF16).astype(F32)
    return hi, mid, lo


def _hy_prep_kernel(x_ref, ng_ref, sc_ref, sh_ref, w_ref, pos_ref, inv_ref, gq_ref, gk_ref, gfq_ref, gfk_ref,
                    fb_ref, qnt_ref, kct_ref, vct_ref, ks_ref, vst_ref, kw_ref, vwt_ref, gate_ref,
                    fqt_ref, fk_ref, fvt_ref, cedge_ref, carry_ref):
    i = pl.program_id(0)
    tm = PREP_TILE
    shp = (tm, LANE)
    lane = _lane_iota(shp)

    act = _norm_mod(x_ref[...], ng_ref[...], sc_ref[...], sh_ref[...]).astype(BF16)
    pairs = {}

    def blk(b):
        if b // 2 not in pairs:
            cols = slice((b // 2) * 2 * LANE, (b // 2 + 1) * 2 * LANE)
            pairs[b // 2] = jnp.dot(act, w_ref[:, cols], preferred_element_type=F32)
        return pairs[b // 2][:, (b % 2) * LANE:(b % 2 + 1) * LANE]

    ang = pos_ref[...] * inv_ref[...]
    real = lane < HEAD_DIM
    cos = jnp.where(real, jnp.cos(ang), 1.0)
    sin = jnp.where(real, jnp.sin(ang), 0.0)
    gq, gk, gfq, gfk = gq_ref[...], gk_ref[...], gfq_ref[...], gfk_ref[...]
    scale = HEAD_DIM ** -0.5 * LOG2E
    ones_row = lane == SUM_ROW
    ref_ones = jnp.where((lane >= REF_ROW) & (lane < REF_ROW + 3), 1.0, 0.0)

    for h in range(NSA_HEADS):
        q = _rope64(_head_rms(blk(HY_Q0 + h), gq, HEAD_DIM), cos, sin) * scale
        qnt_ref[h * LANE:(h + 1) * LANE, :] = _t_bf16(q)

    row = _row_iota(shp) + i * tm
    onehot = jnp.where(lane - HEAD_DIM == ((row // SLC_BLOCK) % SUPER_BLOCKS), 1.0, 0.0)
    for g in range(NSA_GROUPS):
        def kv(r):
            return blk(HY_KV0 + r * NSA_GROUPS + g)
        sl = slice(g * LANE, (g + 1) * LANE)
        kct_ref[g] = _rope64(kv(0), cos, sin)[:, :HEAD_DIM].astype(BF16)
        vct_ref[g] = kv(1)[:, :HEAD_DIM].astype(BF16)
        ks = _rope64(_head_rms(kv(2), gk, HEAD_DIM), cos, sin)
        ks_ref[:, sl] = (ks + onehot + ref_ones).astype(BF16)
        vst_ref[g, 0] = _t_bf16(jnp.where(ones_row, 1.0, kv(3)), V_ROWS)
        kw_ref[:, sl] = _rope64(_head_rms(kv(4), gk, HEAD_DIM), cos, sin).astype(BF16)
        vwt = _t_bf16(kv(5))
        for cidx in range(tm // LANE):
            vwt_ref[g, cidx] = vwt[:, cidx * LANE:(cidx + 1) * LANE]
        gate_ref[g] = jax.nn.sigmoid(blk(HY_G0 + g)).T[:GATE_ROWS]

    @pl.when(i == 0)
    def _():
        carry_ref[...] = jnp.zeros_like(carry_ref)

    z = blk(HY_FF) + fb_ref[...]
    logf = jnp.minimum(z, 0.0) - jnp.log1p(jnp.exp(-jnp.abs(z)))
    tri = jnp.where(_row_iota((tm, tm)) >= _lane_iota((tm, tm)), 1.0, 0.0).astype(F32)
    cum = jnp.dot(tri, logf, precision=HIGHEST, preferred_element_type=F32) + carry_ref[...]
    carry_ref[...] = cum[tm - 1:tm, :]
    cedge_ref[0] = jnp.concatenate([cum[0:1] * LOG2E, cum[tm - 1:tm] * LOG2E, jnp.zeros((6, LANE), F32)], axis=0)

    for h in range(FOX_HEADS):
        c = jnp.broadcast_to(cum[:, h:h + 1], shp) * LOG2E
        hi, mid, lo = _split3(c)
        fq = _head_rms(blk(HY_F0 + h), gfq, HEAD_DIM) * scale
        fq = jnp.where(real, fq, jnp.where(lane == 64, hi, jnp.where(lane == 65, mid, jnp.where(
            lane == 66, lo, jnp.where(lane < 70, 1.0, 0.0)))))
        fk = _head_rms(blk(HY_F0 + FOX_HEADS + h), gfk, HEAD_DIM)
        fk = jnp.where(real, fk, jnp.where(lane < 67, 1.0, jnp.where(lane == 67, -hi, jnp.where(
            lane == 68, -mid, jnp.where(lane == 69, -lo, ref_ones)))))
        sl = slice(h * LANE, (h + 1) * LANE)
        fqt_ref[sl, :] = _t_bf16(fq)
        fk_ref[:, sl] = fk.astype(BF16)
        fvt_ref[h, 0] = _t_bf16(jnp.where(ones_row, 1.0, blk(HY_F0 + 2 * FOX_HEADS + h)), V_ROWS)


def _feat_major(heads, s, tm):
    return (pl.BlockSpec((heads * LANE, tm), lambda i: (0, i)),
            jax.ShapeDtypeStruct((heads * LANE, s), BF16))


def _value_tiles(heads, s, tm, tk, rows=LANE):
    return (pl.BlockSpec((heads, tm // tk, rows, tk), lambda i: (0, i, 0, 0)),
            jax.ShapeDtypeStruct((heads, s // tk, rows, tk), BF16))


def _pad_feature_rows(o):
    return jnp.concatenate([o, jnp.zeros((LANE - HEAD_DIM, o.shape[1]), o.dtype)], axis=0)


def _hy_prep(x2, norm_g, sc, sh, w_in, posf, inv128, gq, gk, gfq, gfk, fbias):
    s, d = x2.shape
    tm = PREP_TILE
    assert tm == KV_TILE and w_in.shape == (d, HY_BLOCKS * LANE) and HY_BLOCKS % 2 == 0
    vec = pl.BlockSpec((1, LANE), lambda i: (0, 0))
    dvec = pl.BlockSpec((1, d), lambda i: (0, 0))

    def rows(nb):
        return (pl.BlockSpec((tm, nb * LANE), lambda i: (i, 0)), jax.ShapeDtypeStruct((s, nb * LANE), BF16))

    tok = (pl.BlockSpec((NSA_GROUPS, tm, HEAD_DIM), lambda i: (0, i, 0)),
           jax.ShapeDtypeStruct((NSA_GROUPS, s, HEAD_DIM), BF16))
    gate = (pl.BlockSpec((NSA_GROUPS, GATE_ROWS, tm), lambda i: (0, 0, i)),
            jax.ShapeDtypeStruct((NSA_GROUPS, GATE_ROWS, s), F32))
    outs = [_feat_major(NSA_HEADS, s, tm), tok, tok, rows(NSA_GROUPS),
            _value_tiles(NSA_GROUPS, s, tm, KV_TILE, V_ROWS),
            rows(NSA_GROUPS), _value_tiles(NSA_GROUPS, s, tm, LANE), gate,
            _feat_major(FOX_HEADS, s, tm), rows(FOX_HEADS), _value_tiles(FOX_HEADS, s, tm, KV_TILE, V_ROWS),
            (pl.BlockSpec((1, 8, LANE), lambda i: (i, 0, 0)), jax.ShapeDtypeStruct((s // tm, 8, LANE), F32))]
    return pl.pallas_call(
        _hy_prep_kernel,
        grid=(s // tm,),
        in_specs=[pl.BlockSpec((tm, d), lambda i: (i, 0)), dvec, dvec, dvec,
                  pl.BlockSpec((d, HY_BLOCKS * LANE), lambda i: (0, 0), pipeline_mode=pl.Buffered(1)),
                  pl.BlockSpec((tm, 1), lambda i: (i, 0)), vec, vec, vec, vec, vec, vec],
        out_specs=[o[0] for o in outs],
        out_shape=[o[1] for o in outs],
        scratch_shapes=[pltpu.VMEM((1, LANE), F32)],
        compiler_params=_cparams("arbitrary"),
        name="hybrid_prep",
    )(x2, norm_g, sc, sh, w_in, posf, inv128, gq, gk, gfq, gfk, fbias)


def _compress_kernel(kc_ref, vc_ref, wk_ref, wv_ref, pek_ref, pev_ref, gk_ref, ko_ref, vo_ref):
    half = CMP_STRIDE * HEAD_DIM

    def comp(ch_ref, w_ref, pe_ref):
        ch = ch_ref[0]
        nc = ch.shape[0]
        a = jnp.dot(ch, w_ref[:half], preferred_element_type=F32)
        b = jnp.dot(ch, w_ref[half:], preferred_element_type=F32)
        nxt = pltpu.roll(b, nc - 1, 0)
        pe = jnp.dot(jnp.broadcast_to(pe_ref[...], (8, 2 * half)).astype(BF16), w_ref[...],
                     preferred_element_type=F32)[0:1]
        return a + nxt + pe

    ko_ref[0] = _head_rms(comp(kc_ref, wk_ref, pek_ref), gk_ref[...], HEAD_DIM).astype(BF16)
    vo_ref[0] = comp(vc_ref, wv_ref, pev_ref).T.astype(BF16)


def _compress(kct, vct, w_cmp, cmp_pe, k_norm):
    g, s, _ = kct.shape
    nc = s // CMP_STRIDE
    wide = CMP_STRIDE * HEAD_DIM
    kch = kct.reshape(g, nc, wide)
    vch = vct.reshape(g, nc, wide)
    w_pad = jnp.pad(w_cmp, ((0, 0), (0, 0), (0, LANE - HEAD_DIM))).astype(BF16)
    ch = pl.BlockSpec((1, nc, wide), lambda i: (i, 0, 0))
    wspec = pl.BlockSpec((2 * wide, LANE), lambda i: (0, 0))
    pespec = pl.BlockSpec((1, 2 * wide), lambda i: (0, 0))
    return pl.pallas_call(
        _compress_kernel,
        grid=(g,),
        in_specs=[ch, ch, wspec, wspec, pespec, pespec, pl.BlockSpec((1, LANE), lambda i: (0, 0))],
        out_specs=[pl.BlockSpec((1, nc, LANE), lambda i: (i, 0, 0)),
                   pl.BlockSpec((1, LANE, nc), lambda i: (i, 0, 0))],
        out_shape=[jax.ShapeDtypeStruct((g, nc, LANE), BF16), jax.ShapeDtypeStruct((g, LANE, nc), BF16)],
        compiler_params=_cparams("parallel"),
        name="nsa_compress",
    )(kch, vch, w_pad[0], w_pad[1], cmp_pe[0].reshape(1, 2 * wide).astype(F32),
      cmp_pe[1].reshape(1, 2 * wide).astype(F32), _pad_lanes(k_norm))


def _masked_softmax_t(s, mask):
    s = jnp.where(mask, s, NEG)
    m = jnp.max(s, axis=0, keepdims=True)
    e = jnp.exp2(s - m)
    inv = 1.0 / jnp.maximum(jnp.sum(e, axis=0, keepdims=True), 1e-30)
    return e, jnp.where(m > 0.5 * NEG, inv, 0.0)


def _online_steps(steps, ms, acc_ref):
    ms = list(ms)

    def scores(step):
        k_tile, qa, _, c, mask = step
        s = jnp.dot(k_tile, qa, preferred_element_type=F32)
        if mask is not None:
            s = jnp.where(mask, s, NEG)
        return s, jnp.max(s, axis=0, keepdims=True)

    nxt = scores(steps[0])
    for idx, (_, _, vt, c, _) in enumerate(steps):
        sl = slice(c * Q_STRIP, (c + 1) * Q_STRIP)
        s, s_max = nxt
        if idx + 1 < len(steps):
            nxt = scores(steps[idx + 1])
        m_new = jnp.maximum(ms[c], s_max)
        a = jnp.exp2(ms[c] - m_new)
        p = jnp.exp2((s - m_new).astype(BF16))
        ms[c] = m_new
        acc_ref[:, sl] = a * acc_ref[:, sl] + jnp.dot(vt, p, preferred_element_type=F32)
    return tuple(ms)


def _m_init(n_strips):
    return tuple(jnp.full((1, Q_STRIP), NEG, F32) for _ in range(n_strips))


def _with_ref_rows(qa, m):
    hi, mid, lo = _split3(-m)
    r = _row_iota((LANE - REF_SLAB, Q_STRIP)) + REF_SLAB
    slab = jnp.where(r == REF_ROW, hi, jnp.where(r == REF_ROW + 1, mid, jnp.where(r == REF_ROW + 2, lo, 0.0)))
    return jnp.concatenate([qa[:REF_SLAB], slab.astype(BF16)], axis=0)


def _first_tile_max(k_tile, qa_strips, masks):
    return tuple(jnp.max(jnp.where(mask, jnp.dot(k_tile, qa, preferred_element_type=F32), NEG), axis=0, keepdims=True)
                 for qa, mask in zip(qa_strips, masks))


def _fast_steps(steps, state, acc_ref):
    state = list(state)
    for k, step in enumerate(steps):
        assert all(prev[3] != step[3] for prev in steps[max(k - SCORE_LOOKAHEAD + 1, 0):k])

    def scores(step):
        k_tile, qa, _, c, mask = step
        s = jnp.dot(k_tile, _with_ref_rows(qa, state[c][0]), preferred_element_type=F32)
        if mask is not None:
            s = jnp.where(mask, s, NEG)
        return s

    ahead = [scores(st) for st in steps[:SCORE_LOOKAHEAD]]
    for idx, (_, _, vt, c, _) in enumerate(steps):
        sl = slice(c * Q_STRIP, (c + 1) * Q_STRIP)
        s = ahead.pop(0)
        m, worst = state[c]
        cm = jnp.max(s, axis=0, keepdims=True)
        inc = jnp.maximum(cm, 0.0)
        state[c] = (m + inc, jnp.maximum(worst, cm))
        if idx + SCORE_LOOKAHEAD < len(steps):
            ahead.append(scores(steps[idx + SCORE_LOOKAHEAD]))
        p = jnp.exp2(s).astype(BF16)
        acc_ref[:, sl] = jnp.exp2(-inc) * (acc_ref[:, sl] + jnp.dot(vt, p, preferred_element_type=F32))
    return tuple(state)


def _flat(state):
    return tuple(x for pair in state for x in pair)


def _nest(flat):
    return tuple((flat[2 * c], flat[2 * c + 1]) for c in range(len(flat) // 2))


def _nsa_kernel(qt_ref, kc_ref, vct_ref, ks_ref, vst_ref, kw_ref, vwt_ref, gate_ref, ovt_ref, o_ref,
                qaug_ref, acc_ref, *, n_sel):
    i = pl.program_id(1)
    tq = Q_TILE_NSA
    cols = NSA_HPG * tq
    qs = i * tq
    nc = kc_ref.shape[1]
    nslc = ovt_ref.shape[0]
    n_super = nslc // SUPER_BLOCKS

    qt = jnp.concatenate([qt_ref[h * LANE:(h + 1) * LANE, :] for h in range(NSA_HPG)], axis=1)
    tq_row = qs + (_lane_iota((1, cols)) % tq)

    def compress_and_select(n_c, n_b):
        s = jnp.dot(kc_ref[0, :n_c], qt, preferred_element_type=F32)
        cmp_end = _row_iota((n_c, 1)) * CMP_STRIDE + (CMP_LEN - 1)
        e, inv_l = _masked_softmax_t(s, cmp_end <= tq_row)
        o_cmp = jnp.dot(vct_ref[0, :, :n_c], e.astype(BF16), preferred_element_type=F32) * inv_l

        psum = e[:, 0:tq] * inv_l[:, 0:tq]
        for h in range(1, NSA_HPG):
            psum = psum + e[:, h * tq:(h + 1) * tq] * inv_l[:, h * tq:(h + 1) * tq]
        p_hi = psum.astype(BF16)
        p_lo = (psum - p_hi.astype(F32)).astype(BF16)
        ovt = ovt_ref[:n_b, :n_c]
        imp = (jnp.dot(ovt, p_hi, preferred_element_type=F32)
               + jnp.dot(ovt, p_lo, preferred_element_type=F32))

        jj = _row_iota((n_b, tq))
        tq_blk = qs + _lane_iota((n_b, tq))
        cur = tq_blk // SLC_BLOCK
        forced = (jj == 0) | (jj == cur) | (jj == cur - 1)
        causal_blk = jj * SLC_BLOCK <= tq_blk
        val = jnp.where(forced, imp + BIG, imp)
        val = jnp.where(causal_blk, val, NEG)
        jjf = jj.astype(F32)

        def pick(_, carry):
            val, sel = carry
            mx = jnp.max(val, axis=0, keepdims=True)
            idx = jnp.min(jnp.where(val == mx, jjf, float(n_b)), axis=0, keepdims=True)
            hit = jjf == idx
            return jnp.where(hit, -jnp.inf, val), jnp.where(hit, 1.0, sel)

        _, sel = lax.fori_loop(0, min(n_sel, n_b), pick, (val, jnp.zeros((n_b, tq), F32)))
        bias = jnp.where((sel > 0.0) & causal_blk, 0.0, NEG)
        if n_b < nslc:
            bias = jnp.concatenate([bias, jnp.full((nslc - n_b, tq), NEG, F32)], axis=0)
        return o_cmp, bias

    parts = max(1, min(NSA_PREFIX_PARTS, nc // (2 * LANE)))
    seq = nc * CMP_STRIDE

    def dispatch(k):
        full = lambda: compress_and_select(nc * k // parts, nslc * k // parts)
        if k == parts:
            return full()
        return lax.cond(qs + tq <= seq * k // parts, full, lambda: dispatch(k + 1))

    o_cmp, bias_t = dispatch(1)

    q_rows = qt[:HEAD_DIM].astype(F32)
    spare = jnp.zeros((LANE - HEAD_DIM - SUPER_BLOCKS, cols), F32)
    for st in range(n_super):
        b = bias_t[st * SUPER_BLOCKS:(st + 1) * SUPER_BLOCKS]
        b = jnp.concatenate([b] * NSA_HPG, axis=1)
        qaug_ref[st] = jnp.concatenate([q_rows, b, spare], axis=0).astype(BF16)

    tk = KV_TILE
    per_super = SUPER_BLOCKS * SLC_BLOCK // tk
    j_last = (qs + tq - 1) // tk
    n_strips = cols // Q_STRIP
    strips = [slice(c * Q_STRIP, (c + 1) * Q_STRIP) for c in range(n_strips)]

    def causal_masks(j):
        kpos = j * tk + _row_iota((tk, 1))
        return [kpos <= tq_row[:, sl] for sl in strips]

    def slc_steps(j, masks=None):
        k0 = pl.multiple_of(j * tk, tk)
        k_tile, vt, st = ks_ref[pl.ds(k0, tk), :], vst_ref[0, j], j // per_super
        return [(k_tile, qaug_ref[st, :, strips[c]], vt, c, None if masks is None else masks[c])
                for c in range(n_strips)]

    group = 4

    def grouped(jj, flat):
        steps = [st for t in range(group) for st in slc_steps(group * jj + t)]
        return _flat(_fast_steps(steps, _nest(flat), acc_ref))

    def single(j, flat):
        return _flat(_fast_steps(slc_steps(j), _nest(flat), acc_ref))

    acc_ref[...] = jnp.zeros(acc_ref.shape, F32)
    head_rows = 16
    head_masks = [_row_iota((head_rows, 1)) <= tq_row[:, sl] for sl in strips]
    m0 = _first_tile_max(ks_ref[0:head_rows, :], [qaug_ref[0, :, sl] for sl in strips], head_masks)
    n_groups = j_last // group
    flat = lax.fori_loop(0, n_groups, grouped, _flat(tuple((m, jnp.zeros_like(m)) for m in m0)))
    flat = lax.fori_loop(group * n_groups, j_last, single, flat)
    state = _fast_steps(slc_steps(j_last, causal_masks(j_last)), _nest(flat), acc_ref)
    worst = jnp.max(jnp.concatenate([w for _, w in state], axis=1))

    @pl.when(worst > EXP_GUARD)
    def _():
        acc_ref[...] = jnp.zeros(acc_ref.shape, F32)
        lax.fori_loop(0, j_last + 1, lambda j, ms: _online_steps(slc_steps(j, causal_masks(j)), ms, acc_ref),
                      _m_init(n_strips))

    o_slc = acc_ref[:HEAD_DIM] / jnp.maximum(acc_ref[SUM_ROW:SUM_ROW + 1], 1e-30)

    wlen = WINDOW + tq
    ws = pl.multiple_of(jnp.maximum(qs - WINDOW, 0), tq)
    s = jnp.dot(kw_ref[pl.ds(ws, wlen), :], qt, preferred_element_type=F32)
    dist = tq_row - (ws + _row_iota((wlen, 1)))
    e, inv_l = _masked_softmax_t(s, (dist >= 0) & (dist < WINDOW))
    e = e.astype(BF16)
    wb = ws // LANE
    o_win = jnp.zeros((LANE, cols), F32)
    for c in range(wlen // LANE):
        o_win = o_win + jnp.dot(vwt_ref[0, wb + c], e[c * LANE:(c + 1) * LANE], preferred_element_type=F32)
    o_win = o_win * inv_l

    gate = gate_ref[0]
    for h in range(NSA_HPG):
        sl = slice(h * tq, (h + 1) * tq)
        o = (gate[3 * h:3 * h + 1] * o_cmp[:HEAD_DIM, sl] + gate[3 * h + 1:3 * h + 2] * o_slc[:, sl]
             + gate[3 * h + 2:3 * h + 3] * o_win[:HEAD_DIM, sl])
        o_ref[:, h * LANE:(h + 1) * LANE] = _pad_feature_rows(o).T.astype(BF16)


def _overlap_t(s, nslc_pad):
    nc = s // CMP_STRIDE
    cmp_start = np.arange(nc) * CMP_STRIDE
    slc_start = np.arange(nslc_pad) * SLC_BLOCK
    ov = np.clip(np.minimum(cmp_start[:, None] + CMP_LEN, slc_start[None, :] + SLC_BLOCK)
                 - np.maximum(cmp_start[:, None], slc_start[None, :]), 0, None) / CMP_STRIDE
    ov[nc - CMP_LEN // CMP_STRIDE + 1:, :] = 0.0
    ov[:, s // SLC_BLOCK:] = 0.0
    return jnp.asarray(ov.T, BF16)


def _nsa_attention(qnt, kc, vct, ks, vst, kw, vwt, gates):
    s = qnt.shape[1]
    nc = s // CMP_STRIDE
    n_slc = s // SLC_BLOCK
    nslc_pad = -(-n_slc // LANE) * LANE
    tq = Q_TILE_NSA
    cols = NSA_HPG * tq
    once = pl.Buffered(1)
    res = pl.BlockSpec((s, LANE), lambda g, i: (0, g), pipeline_mode=once)
    return pl.pallas_call(
        functools.partial(_nsa_kernel, n_sel=min(SLC_TOPK, n_slc)),
        grid=(NSA_GROUPS, s // tq),
        in_specs=[pl.BlockSpec((NSA_HPG * LANE, tq), lambda g, i: (g, i)),
                  pl.BlockSpec((1, nc, LANE), lambda g, i: (g, 0, 0), pipeline_mode=once),
                  pl.BlockSpec((1, LANE, nc), lambda g, i: (g, 0, 0), pipeline_mode=once),
                  res, pl.BlockSpec((1, s // KV_TILE, V_ROWS, KV_TILE), lambda g, i: (g, 0, 0, 0),
                                    pipeline_mode=once),
                  res, pl.BlockSpec((1, s // LANE, LANE, LANE), lambda g, i: (g, 0, 0, 0), pipeline_mode=once),
                  pl.BlockSpec((1, GATE_ROWS, tq), lambda g, i: (g, 0, i)),
                  pl.BlockSpec((nslc_pad, nc), lambda g, i: (0, 0), pipeline_mode=once)],
        out_specs=pl.BlockSpec((tq, NSA_HPG * LANE), lambda g, i: (i, g)),
        out_shape=jax.ShapeDtypeStruct((s, NSA_HEADS * LANE), BF16),
        scratch_shapes=[pltpu.VMEM((nslc_pad // SUPER_BLOCKS, LANE, cols), BF16),
                        pltpu.VMEM((V_ROWS, cols), F32)],
        compiler_params=_cparams("parallel", "arbitrary"),
        name="nsa_attention",
    )(qnt, kc, vct, ks, vst, kw, vwt, gates, _overlap_t(s, nslc_pad))


def _flash_kernel(cfirst_ref, clast_ref, slack_ref, qt_ref, k_ref, vt_ref, o_ref, acc_ref, *, tq, decay):
    h = pl.program_id(0)
    i = pl.program_id(1)
    tk = KV_TILE
    n_tiles = k_ref.shape[0] // tk
    acc_ref[...] = jnp.zeros(acc_ref.shape, F32)

    n_strips = tq // Q_STRIP
    per_q = tq // tk
    qas = [qt_ref[:, c * Q_STRIP:(c + 1) * Q_STRIP] for c in range(n_strips)]

    def tile_steps(j, d=None):
        k0 = pl.multiple_of(j * tk, tk)
        k_tile, vt = k_ref[pl.ds(k0, tk), :], vt_ref[0, j]
        steps = []
        for c in range(n_strips):
            mask = None
            if d is not None:
                if d * tk > (c + 1) * Q_STRIP - 1:
                    continue
                if (d + 1) * tk - 1 > c * Q_STRIP:
                    shp = (tk, Q_STRIP)
                    mask = _row_iota(shp) + d * tk <= _lane_iota(shp) + c * Q_STRIP
            steps.append((k_tile, qas[c], vt, c, mask))
        return steps

    def any_tile_masks(j):
        shp = (tk, Q_STRIP)
        return [_row_iota(shp) + j * tk <= _lane_iota(shp) + (i * tq + c * Q_STRIP) for c in range(n_strips)]

    def below(t, flat):
        jj = i - 1 - t
        bound = (slack_ref[0] + cfirst_ref[h * n_tiles + i * per_q]
                 - clast_ref[h * n_tiles + jj * per_q + per_q - 1])

        def run(flat):
            steps = [st for u in range(per_q) for st in tile_steps(jj * per_q + (per_q - 1 - u))]
            return _flat(_fast_steps(steps, _nest(flat), acc_ref))

        return lax.cond(bound >= -SKIP_MARGIN, run, lambda flat: flat, flat)

    diag0 = i * per_q
    own = [(c * Q_STRIP) // tk for c in range(n_strips)]
    head = tk if decay else 16

    def start_max(c):
        j = diag0 + own[c]
        keys = k_ref[pl.ds(pl.multiple_of(j * tk, tk), head), :]
        shp = (head, Q_STRIP)
        mask = _row_iota(shp) + j * tk <= _lane_iota(shp) + (i * tq + c * Q_STRIP)
        return _first_tile_max(keys, [qas[c]], [mask])[0]

    m0 = tuple(start_max(c) for c in range(n_strips))
    steps = [st for d in reversed(range(per_q)) for st in tile_steps(diag0 + d, d)]
    state = _fast_steps(steps, tuple((m, jnp.zeros_like(m)) for m in m0), acc_ref)
    state = _nest(lax.fori_loop(0, i, below, _flat(state)))
    worst = jnp.max(jnp.concatenate([w for _, w in state], axis=1))

    @pl.when(worst > EXP_GUARD)
    def _():
        acc_ref[...] = jnp.zeros(acc_ref.shape, F32)

        def exact(j, ms):
            k0 = pl.multiple_of(j * tk, tk)
            k_tile, vt, masks = k_ref[pl.ds(k0, tk), :], vt_ref[0, j], any_tile_masks(j)
            return _online_steps([(k_tile, qas[c], vt, c, masks[c]) for c in range(n_strips)], ms, acc_ref)

        lax.fori_loop(0, (i + 1) * per_q, exact, _m_init(n_strips))

    o = _pad_feature_rows(acc_ref[:HEAD_DIM] / acc_ref[SUM_ROW:SUM_ROW + 1])
    for c0 in range(0, tq, LANE):
        o_ref[c0:c0 + LANE, :] = o[:, c0:c0 + LANE].T.astype(BF16)


def _causal_attention(qt, k, vt, tq, bias_edges=None, slack=None):
    s, width = k.shape
    heads = width // LANE
    tq = min(tq, s)
    assert tq % KV_TILE == 0 and s % tq == 0
    n_tiles = s // KV_TILE
    if bias_edges is None:
        first = last = jnp.zeros((heads * n_tiles,), F32)
        slack = jnp.full((1,), -NEG, F32)
    else:
        first, last = (e.reshape(heads * n_tiles).astype(F32) for e in bias_edges)
    grid_spec = pltpu.PrefetchScalarGridSpec(
        num_scalar_prefetch=3,
        grid=(heads, s // tq),
        in_specs=[pl.BlockSpec((LANE, tq), lambda h, i, *_: (h, i)),
                  pl.BlockSpec((s, LANE), lambda h, i, *_: (0, h)),
                  pl.BlockSpec((1, n_tiles, V_ROWS, KV_TILE), lambda h, i, *_: (h, 0, 0, 0))],
        out_specs=pl.BlockSpec((tq, LANE), lambda h, i, *_: (i, h)),
        scratch_shapes=[pltpu.VMEM((V_ROWS, tq), F32)],
    )
    return pl.pallas_call(
        functools.partial(_flash_kernel, tq=tq, decay=bias_edges is not None),
        grid_spec=grid_spec,
        out_shape=jax.ShapeDtypeStruct((s, width), BF16),
        compiler_params=_cparams("parallel", "arbitrary"),
        name="causal_attention",
    )(first, last, slack.astype(F32), qt, k, vt)


def _out_proj_kernel(oa_ref, ob_ref, wa_ref, wb_ref, x_ref, g_ref, o_ref):
    y = jnp.dot(oa_ref[...], wa_ref[...], preferred_element_type=F32)
    y = y + jnp.dot(ob_ref[...], wb_ref[...], preferred_element_type=F32)
    o_ref[...] = x_ref[...] + g_ref[...] * y


def _out_proj(oa, ob, cola, colb, wa, wb, x2, gate):
    s, d = x2.shape
    ka = wa.shape[0]
    tm = ROW_TILE
    return pl.pallas_call(
        _out_proj_kernel,
        grid=(s // tm,),
        in_specs=[pl.BlockSpec((tm, ka), lambda i: (i, cola)), pl.BlockSpec((tm, ka), lambda i: (i, colb)),
                  pl.BlockSpec((ka, d), lambda i: (0, 0)), pl.BlockSpec((ka, d), lambda i: (0, 0)),
                  pl.BlockSpec((tm, d), lambda i: (i, 0)), pl.BlockSpec((1, d), lambda i: (0, 0))],
        out_specs=pl.BlockSpec((tm, d), lambda i: (i, 0)),
        out_shape=jax.ShapeDtypeStruct((s, d), F32),
        compiler_params=_cparams("parallel"),
        name="out_proj",
    )(oa, ob, wa, wb, x2, gate)


def _mla_prep_kernel(x_ref, ng_ref, sc_ref, sh_ref, win_ref, pos_ref, inv_ref, gqa_ref, gkva_ref, wuq_ref, wuk_ref,
                     wuv_ref, gq_ref, gk_ref, gkr_ref, qt_ref, k_ref, vt_ref):
    shp = (PREP_TILE, LANE)
    act = _norm_mod(x_ref[...], ng_ref[...], sc_ref[...], sh_ref[...]).astype(BF16)
    proj = jnp.dot(act, win_ref[...], preferred_element_type=F32)
    lane = _lane_iota(shp)
    nope = lane < QK_NOPE
    rope = (lane >= QK_NOPE) & (lane < QK_NOPE + QK_ROPE)
    ref_ones = jnp.where((lane >= REF_ROW) & (lane < REF_ROW + 3), 1.0, 0.0)
    ang = pos_ref[...] * inv_ref[...]
    cos = jnp.where(rope, jnp.cos(ang), 1.0)
    sin = jnp.where(rope, jnp.sin(ang), 0.0)

    def rope32(x):
        half = QK_ROPE // 2
        rot = jnp.where(lane < QK_NOPE + half, -pltpu.roll(x, LANE - half, 1), pltpu.roll(x, half, 1))
        return x * cos + rot * sin

    def low_rank_norm(x, g):
        ms = jnp.mean(x * x, axis=-1, keepdims=True)
        return (x * lax.rsqrt(ms + EPS) * g).astype(BF16)

    nq = Q_LORA // LANE
    cq = low_rank_norm(proj[:, :Q_LORA], gqa_ref[...])
    ckv = low_rank_norm(proj[:, Q_LORA:Q_LORA + KV_LORA], gkva_ref[...])
    kr = proj[:, (nq + KV_LORA // LANE) * LANE:(nq + KV_LORA // LANE + 1) * LANE]
    k_rope = rope32(_head_rms(kr, gkr_ref[...], QK_ROPE))

    gq, gk = gq_ref[...], gk_ref[...]
    scale = (QK_NOPE + QK_ROPE) ** -0.5 * LOG2E
    pair = 2 * LANE
    for hp in range(MLA_HEADS // 2):
        cols = slice(hp * pair, (hp + 1) * pair)
        q2 = jnp.dot(cq, wuq_ref[:, cols], preferred_element_type=F32)
        k2 = jnp.dot(ckv, wuk_ref[:, cols], preferred_element_type=F32)
        v2 = jnp.dot(ckv, wuv_ref[:, cols], preferred_element_type=F32)
        for sub in range(2):
            head = 2 * hp + sub
            sl = slice(head * LANE, (head + 1) * LANE)
            half = slice(sub * LANE, (sub + 1) * LANE)
            x = q2[:, half]
            ss_n = jnp.sum(jnp.where(nope, x * x, 0.0), axis=-1, keepdims=True)
            ss_r = jnp.sum(jnp.where(rope, x * x, 0.0), axis=-1, keepdims=True)
            inv_rms = jnp.where(nope, lax.rsqrt(ss_n * (1.0 / QK_NOPE) + EPS),
                                lax.rsqrt(ss_r * (1.0 / QK_ROPE) + EPS))
            qt_ref[sl, :] = _t_bf16(rope32(x * inv_rms * gq) * scale)
            kn = _head_rms(k2[:, half], gk, QK_NOPE)
            k_ref[:, sl] = (kn + k_rope + ref_ones).astype(BF16)
            vt_ref[head, 0] = _t_bf16(jnp.where(lane == SUM_ROW, 1.0, v2[:, half]), V_ROWS)


def _mla_prep(x2, norm_g, sc, sh, w_in, posf, inv128, gqa, gkva, wuq, wuk, wuv, gq, gk, gkr):
    s, d = x2.shape
    tm = PREP_TILE
    assert tm == KV_TILE

    def full(a):
        return pl.BlockSpec(a.shape, lambda i: (0, 0))

    outs = [_feat_major(MLA_HEADS, s, tm),
            (pl.BlockSpec((tm, MLA_HEADS * LANE), lambda i: (i, 0)),
             jax.ShapeDtypeStruct((s, MLA_HEADS * LANE), BF16)),
            _value_tiles(MLA_HEADS, s, tm, KV_TILE, V_ROWS)]
    head = (norm_g, sc, sh, w_in)
    tail = (inv128, gqa, gkva, wuq, wuk, wuv, gq, gk, gkr)
    return pl.pallas_call(
        _mla_prep_kernel,
        grid=(s // tm,),
        in_specs=[pl.BlockSpec((tm, d), lambda i: (i, 0))] + [full(a) for a in head]
                 + [pl.BlockSpec((tm, 1), lambda i: (i, 0))] + [full(a) for a in tail],
        out_specs=[o[0] for o in outs],
        out_shape=[o[1] for o in outs],
        compiler_params=_cparams("parallel"),
        name="mla_prep",
    )(x2, *head, posf, *tail)


def _rank_lt(v, k):
    n = v.shape[0]
    row = _row_iota(v.shape)
    rank = jnp.zeros(v.shape, F32)
    for b in range(n):
        vb = v[b:b + 1, :]
        rank = rank + jnp.where((vb > v) | ((vb == v) & (row > b)), 1.0, 0.0)
    return rank < k


def _top_rows(v, k):
    rowf = _row_iota(v.shape).astype(F32)
    chosen = jnp.zeros(v.shape, F32)
    for _ in range(k):
        mx = jnp.max(v, axis=0, keepdims=True)
        idx = jnp.min(jnp.where(v == mx, rowf, float(v.shape[0])), axis=0, keepdims=True)
        hit = rowf == idx
        chosen = jnp.where(hit, 1.0, chosen)
        v = jnp.where(hit, -jnp.inf, v)
    return chosen > 0.0


def _moe_route_kernel(x_ref, g_ref, sc_ref, sh_ref, wr_ref, rb_ref, h_ref, pos_ref, wt_ref, cnt_ref):
    tm = ROW_TILE
    h = _norm_mod(x_ref[...], g_ref[...], sc_ref[...], sh_ref[...])
    h_ref[...] = h.astype(BF16)
    logits = jnp.dot(h, wr_ref[...], precision=HIGHEST, preferred_element_type=F32)
    lt = logits.T[:N_EXPERTS]
    scores = jax.nn.sigmoid(lt)
    sel = scores + rb_ref[...]

    per = N_EXPERTS // N_GROUPS
    grp = sel.reshape(N_GROUPS, per, tm)
    sub = lax.broadcasted_iota(jnp.int32, grp.shape, 1)
    m1 = jnp.max(grp, axis=1, keepdims=True)
    first = jnp.min(jnp.where(grp == m1, sub, per), axis=1, keepdims=True)
    m2 = jnp.max(jnp.where(sub == first, -jnp.inf, grp), axis=1, keepdims=True)
    gscore = (m1 + m2).reshape(N_GROUPS, tm)
    gmask = _rank_lt(gscore, TOPK_GROUPS)
    emask = jnp.broadcast_to(gmask.reshape(N_GROUPS, 1, tm), grp.shape).reshape(N_EXPERTS, tm)
    chosen = _top_rows(jnp.where(emask, sel, NEG), TOP_K)

    w = jnp.where(chosen, scores, 0.0)
    wt_ref[...] = w / jnp.sum(w, axis=0, keepdims=True) * ROUTED_SCALE

    upper = jnp.where(_row_iota((tm, tm)) <= _lane_iota((tm, tm)), 1.0, 0.0).astype(BF16)
    incl = jnp.dot(jnp.where(chosen, 1.0, 0.0).astype(BF16), upper, preferred_element_type=F32)
    pos_ref[...] = jnp.where(chosen, incl - 1.0, -1.0)
    cnt_ref[0] = jnp.broadcast_to(incl[:, tm - 1:tm], (N_EXPERTS, LANE))


def _moe_route(x2, g, sc, sh, w_router_pad, router_bias_col):
    s, d = x2.shape
    tm = ROW_TILE
    vec = pl.BlockSpec((1, d), lambda i: (0, 0))
    et = pl.BlockSpec((N_EXPERTS, tm), lambda i: (0, i))
    return pl.pallas_call(
        _moe_route_kernel,
        grid=(s // tm,),
        in_specs=[pl.BlockSpec((tm, d), lambda i: (i, 0)), vec, vec, vec,
                  pl.BlockSpec((d, LANE), lambda i: (0, 0)),
                  pl.BlockSpec((N_EXPERTS, 1), lambda i: (0, 0))],
        out_specs=[pl.BlockSpec((tm, d), lambda i: (i, 0)), et, et,
                   pl.BlockSpec((1, N_EXPERTS, LANE), lambda i: (i, 0, 0))],
        out_shape=[jax.ShapeDtypeStruct((s, d), BF16), jax.ShapeDtypeStruct((N_EXPERTS, s), F32),
                   jax.ShapeDtypeStruct((N_EXPERTS, s), F32),
                   jax.ShapeDtypeStruct((s // tm, N_EXPERTS, LANE), F32)],
        compiler_params=_cparams("parallel"),
        name="moe_route",
    )(x2, g, sc, sh, w_router_pad, router_bias_col)


def _moe_kernel(cnt_ref, x_ref, h_ref, pos_ref, wt_ref, wg_ref, wu_ref, wd_ref, sg_ref, su_ref, sd_ref,
                g2_ref, o_ref, acc_ref):
    i = pl.program_id(0)
    e = pl.program_id(1)
    tm = ROW_TILE
    r = MOE_CHUNK

    @pl.when(e == 0)
    def _():
        h = h_ref[...]
        a = jnp.dot(h, sg_ref[...], preferred_element_type=F32)
        a = a * jax.nn.sigmoid(a) * jnp.dot(h, su_ref[...], preferred_element_type=F32)
        acc_ref[...] = jnp.dot(a.astype(BF16), sd_ref[...], preferred_element_type=F32)

    first = e * MOE_EXPERTS_PER_STEP
    n = cnt_ref[i * N_EXPERTS + first]
    for k in range(1, MOE_EXPERTS_PER_STEP):
        n = jnp.maximum(n, cnt_ref[i * N_EXPERTS + first + k])
    prows = [pos_ref[pl.ds(first + k, 1), :] for k in range(MOE_EXPERTS_PER_STEP)]
    wrows = [wt_ref[pl.ds(first + k, 1), :] for k in range(MOE_EXPERTS_PER_STEP)]

    def chunk(c, _):
        slot = (_row_iota((r, tm)) + c * r).astype(F32)
        hits = [prow == slot for prow in prows]
        onehot = jnp.concatenate([jnp.where(hit, 1.0, 0.0).astype(BF16) for hit in hits], axis=0)
        xg = jnp.dot(onehot, h_ref[...], preferred_element_type=F32).astype(BF16)
        ys = []
        for k in range(MOE_EXPERTS_PER_STEP):
            xk = xg[k * r:(k + 1) * r]
            a = jnp.dot(xk, wg_ref[k], preferred_element_type=F32)
            a = a * jax.nn.sigmoid(a) * jnp.dot(xk, wu_ref[k], preferred_element_type=F32)
            y = jnp.dot(a.astype(BF16), wd_ref[k], preferred_element_type=F32)
            wr = jnp.sum(jnp.where(hits[k], wrows[k], 0.0), axis=-1, keepdims=True)
            ys.append((y * wr).astype(BF16))
        acc_ref[...] += _dot_tn(onehot, jnp.concatenate(ys, axis=0))
        return 0

    lax.fori_loop(0, (n + r - 1) // r, chunk, 0)

    @pl.when(e == N_EXPERTS // MOE_EXPERTS_PER_STEP - 1)
    def _():
        o_ref[...] = x_ref[...] + g2_ref[...] * acc_ref[...]


def _moe_experts(counts, x2, h, pos_t, w_t, wg, wu, wd, sg, su, sd, g2):
    s, d = x2.shape
    tm = ROW_TILE
    ff = wg.shape[2]
    tile = pl.BlockSpec((tm, d), lambda i, e, c: (i, 0))
    et = pl.BlockSpec((N_EXPERTS, tm), lambda i, e, c: (0, i))

    def const(a):
        return pl.BlockSpec(a.shape, lambda i, e, c: (0,) * a.ndim)

    per = MOE_EXPERTS_PER_STEP
    grid_spec = pltpu.PrefetchScalarGridSpec(
        num_scalar_prefetch=1,
        grid=(s // tm, N_EXPERTS // per),
        in_specs=[tile, tile, et, et,
                  pl.BlockSpec((per, d, ff), lambda i, e, c: (e, 0, 0)),
                  pl.BlockSpec((per, d, ff), lambda i, e, c: (e, 0, 0)),
                  pl.BlockSpec((per, ff, d), lambda i, e, c: (e, 0, 0)),
                  const(sg), const(su), const(sd), const(g2)],
        out_specs=tile,
        scratch_shapes=[pltpu.VMEM((tm, d), F32)],
    )
    return pl.pallas_call(
        _moe_kernel,
        grid_spec=grid_spec,
        out_shape=jax.ShapeDtypeStruct((s, d), F32),
        compiler_params=_cparams("parallel", "arbitrary"),
        name="moe_experts",
    )(counts, x2, h, pos_t, w_t, wg, wu, wd, sg, su, sd, g2)


def _pad_lanes(v, width=LANE, offset=0):
    out = jnp.zeros((1, width), F32)
    return out.at[0, offset:offset + v.shape[0]].set(v.astype(F32))


def _head_cols(w, n_heads, dim):
    d = w.shape[0]
    w3 = w.reshape(d, n_heads, dim)
    return jnp.pad(w3, ((0, 0), (0, 0), (0, LANE - dim))).reshape(d, n_heads * LANE)


def _hybrid_w_in(w_in):
    d = w_in.shape[0]
    nq = NSA_HEADS * HEAD_DIM
    nkv = 6 * NSA_GROUPS * HEAD_DIM
    ng = 3 * NSA_HEADS
    nf = 3 * FOX_HEADS * HEAD_DIM
    c0, c1, c2, c3 = nq, nq + nkv, nq + nkv + ng, nq + nkv + ng + nf
    gates = w_in[:, c1:c2].reshape(d, NSA_GROUPS, 3 * NSA_HPG)
    gates = jnp.pad(gates, ((0, 0), (0, 0), (0, LANE - 3 * NSA_HPG))).reshape(d, NSA_GROUPS * LANE)
    ff = jnp.pad(w_in[:, c3:], ((0, 0), (0, 2 * LANE - FOX_HEADS)))
    return jnp.concatenate([
        _head_cols(w_in[:, :c0], NSA_HEADS, HEAD_DIM),
        _head_cols(w_in[:, c0:c1], 6 * NSA_GROUPS, HEAD_DIM),
        _head_cols(w_in[:, c2:c3], 3 * FOX_HEADS, HEAD_DIM),
        gates, ff], axis=1).astype(BF16)


def _pad_head_rows(w, n_heads, dim):
    d = w.shape[1]
    w3 = w.reshape(n_heads, dim, d)
    return jnp.pad(w3, ((0, 0), (0, LANE - dim), (0, 0))).reshape(n_heads * LANE, d).astype(BF16)


def _rope_inv(dim, offset):
    inv = ROPE_THETA ** (-jnp.arange(0, dim, 2, dtype=F32) / dim)
    return _pad_lanes(jnp.concatenate([inv, inv]), offset=offset)


def _hybrid_mixer(x2, posf, mods, norm_g, w_in, fox_f_bias, nsa_q_norm, nsa_k_norm, nsa_cmp_pe, nsa_w_cmp,
                  fox_q_norm, fox_k_norm, w_out):
    sh1, sc1, g1 = mods
    (qnt, kct, vct, ks, vst, kw, vwt, gates, fqt, fk, fvt, cedge) = _hy_prep(
        x2, norm_g, sc1, sh1, _hybrid_w_in(w_in), posf, _rope_inv(HEAD_DIM, 0), _pad_lanes(nsa_q_norm),
        _pad_lanes(nsa_k_norm),
        _pad_lanes(fox_q_norm), _pad_lanes(fox_k_norm), _pad_lanes(fox_f_bias))
    kc, vc_t = _compress(kct, vct, nsa_w_cmp, nsa_cmp_pe, nsa_k_norm)
    o_a = _nsa_attention(qnt, kc, vc_t, ks, vst, kw, vwt, gates)
    slack = (2.0 * HEAD_DIM ** 0.5 * LOG2E) * jnp.max(jnp.abs(fox_q_norm)) * jnp.max(jnp.abs(fox_k_norm))
    edges = (cedge[:, 0, :FOX_HEADS].T, cedge[:, 1, :FOX_HEADS].T)
    o_b = _causal_attention(fqt, fk, fvt, Q_TILE_FOX, edges, slack.reshape(1))
    half = NSA_HEADS * HEAD_DIM
    wa = _pad_head_rows(w_out[:half], NSA_HEADS, HEAD_DIM)
    wb = _pad_head_rows(w_out[half:], FOX_HEADS, HEAD_DIM)
    return _out_proj(o_a, o_b, 0, 0, wa, wb, x2, g1)


def _mla_mixer(x2, posf, mods, norm_g, w_in, q_a_norm, kv_a_norm, w_uq, w_ukv, qn_norm, kn_norm, qr_norm,
               kr_norm, w_out):
    sh1, sc1, g1 = mods
    d = x2.shape[1]
    w_kr = jnp.zeros((d, LANE), F32).at[:, QK_NOPE:QK_NOPE + QK_ROPE].set(w_in[:, Q_LORA + KV_LORA:])
    w_in_p = jnp.concatenate([w_in[:, :Q_LORA + KV_LORA], w_kr], axis=1).astype(BF16)
    hq = QK_NOPE + QK_ROPE
    wuq = _head_cols(w_uq, MLA_HEADS, hq).astype(BF16)
    wkv3 = w_ukv.reshape(KV_LORA, MLA_HEADS, QK_NOPE + V_HEAD)
    wuk = _head_cols(wkv3[:, :, :QK_NOPE].reshape(KV_LORA, -1), MLA_HEADS, QK_NOPE).astype(BF16)
    wuv = _head_cols(wkv3[:, :, QK_NOPE:].reshape(KV_LORA, -1), MLA_HEADS, V_HEAD).astype(BF16)
    gq = _pad_lanes(jnp.concatenate([qn_norm, qr_norm]))
    qt, k, vt = _mla_prep(x2, norm_g, sc1, sh1, w_in_p, posf, _rope_inv(QK_ROPE, QK_NOPE),
                          q_a_norm.reshape(1, -1).astype(F32),
                          kv_a_norm.reshape(1, -1).astype(F32), wuq, wuk, wuv, gq, _pad_lanes(kn_norm),
                          _pad_lanes(kr_norm, offset=QK_NOPE))
    o = _causal_attention(qt, k, vt, Q_TILE_MLA)
    w_pad = _pad_head_rows(w_out, MLA_HEADS, V_HEAD)
    half = w_pad.shape[0] // 2
    return _out_proj(o, o, 0, 1, w_pad[:half], w_pad[half:], x2, g1)


def _moe_ffn(x2, mods, norm_g, w_router, router_bias, w_gate, w_up, w_down, ws_gate, ws_up, ws_down):
    sh2, sc2, g2 = mods
    w_r = jnp.pad(w_router.astype(F32), ((0, 0), (0, LANE - N_EXPERTS)))
    h, pos_t, w_t, cnt = _moe_route(x2, norm_g, sc2, sh2, w_r, router_bias.reshape(N_EXPERTS, 1).astype(F32))
    counts = cnt[:, :, 0].astype(jnp.int32).reshape(-1)
    return _moe_experts(counts, x2, h, pos_t, w_t, w_gate.astype(BF16), w_up.astype(BF16),
                        w_down.astype(BF16), ws_gate.astype(BF16), ws_up.astype(BF16), ws_down.astype(BF16), g2)


def kernel(x, c, positions, norm_attn, norm_ffn, w_ada, b_ada, hy_w_in, fox_f_bias, nsa_q_norm, nsa_k_norm, nsa_cmp_pe, nsa_w_cmp, fox_q_norm, fox_k_norm, hy_w_out, mla_w_in, mla_q_a_norm, mla_kv_a_norm, mla_w_uq, mla_w_ukv, mla_qn_norm, mla_kn_norm, mla_qr_norm, mla_kr_norm, mla_w_out, moe_w_router, moe_router_bias, moe_w_gate, moe_w_up, moe_w_down, moe_ws_gate, moe_ws_up, moe_ws_down):
    b, s, d = x.shape
    assert b == 1 and s % KV_TILE == 0 and s >= WINDOW + Q_TILE_NSA
    depth = w_ada.shape[0]
    x2 = x.reshape(s, d).astype(F32)
    posf = positions.reshape(s, 1).astype(F32)
    mod = _ada_mod(c.astype(F32), w_ada.astype(F32), b_ada.astype(F32))

    for layer in range(depth):
        m = [mod[layer, :, k * d:(k + 1) * d] for k in range(6)]
        i = layer // 2
        g_attn = norm_attn[layer].reshape(1, d).astype(F32)
        if layer % 2 == 0:
            x2 = _hybrid_mixer(x2, posf, m[0:3], g_attn, hy_w_in[i], fox_f_bias[i], nsa_q_norm[i],
                               nsa_k_norm[i], nsa_cmp_pe[i], nsa_w_cmp[i], fox_q_norm[i], fox_k_norm[i],
                               hy_w_out[i])
        else:
            x2 = _mla_mixer(x2, posf, m[0:3], g_attn, mla_w_in[i], mla_q_a_norm[i], mla_kv_a_norm[i],
                            mla_w_uq[i], mla_w_ukv[i], mla_qn_norm[i], mla_kn_norm[i], mla_qr_norm[i],
                            mla_kr_norm[i], mla_w_out[i])
        x2 = _moe_ffn(x2, m[3:6], norm_ffn[layer].reshape(1, d).astype(F32), moe_w_router[layer],
                      moe_router_bias[layer], moe_w_gate[layer], moe_w_up[layer], moe_w_down[layer],
                      moe_ws_gate[layer], moe_ws_up[layer], moe_ws_down[layer])
    return x2.reshape(b, s, d)
```

```python
import functools

import numpy as np
import jax
import jax.numpy as jnp
from jax import lax
from jax.experimental import pallas as pl
from jax.experimental.pallas import tpu as pltpu

F32 = jnp.float32
BF16 = jnp.bfloat16
HIGHEST = lax.Precision.HIGHEST

LANE = 128
VMEM_LIMIT_BYTES = 56 * 1024 * 1024

HEAD_DIM = 64
NSA_HEADS = 8
NSA_GROUPS = 2
NSA_HPG = NSA_HEADS // NSA_GROUPS
CMP_LEN = 32
CMP_STRIDE = 16
SLC_BLOCK = 64
SLC_TOPK = 16
WINDOW = 512
FOX_HEADS = 8
MLA_HEADS = 16
Q_LORA = 384
KV_LORA = 256
QK_NOPE = 64
QK_ROPE = 32
V_HEAD = 64
N_EXPERTS = 64
TOP_K = 8
N_GROUPS = 8
TOPK_GROUPS = 4
EXPERT_FF = 256
ROUTED_SCALE = 2.5
ROPE_THETA = 10000.0
EPS = 1e-6
NEG = -1e30
BIG = 1e6

ROW_TILE = 512
PREP_TILE = 512
GATE_ROWS = 16
Q_TILE_NSA = 256
KV_TILE = 512
Q_TILE_FOX = 1024
Q_TILE_MLA = 2048
Q_STRIP = 256
SUPER_BLOCKS = 32
NSA_PREFIX_PARTS = 8
SUM_ROW = 64
V_ROWS = 80
REF_ROW = 104
REF_SLAB = 96
EXP_GUARD = 100.0
SCORE_LOOKAHEAD = 4
SKIP_MARGIN = 160.0
LOG2E = 1.4426950408889634
MOE_CHUNK = 128
MOE_EXPERTS_PER_STEP = 8

HY_Q0 = 0
HY_KV0 = HY_Q0 + NSA_HEADS
HY_F0 = HY_KV0 + 6 * NSA_GROUPS
HY_G0 = HY_F0 + 3 * FOX_HEADS
HY_FF = HY_G0 + NSA_GROUPS
HY_BLOCKS = HY_FF + 2


def _cparams(*sem):
    return pltpu.CompilerParams(dimension_semantics=sem, vmem_limit_bytes=VMEM_LIMIT_BYTES)


def _lane_iota(shape):
    return lax.broadcasted_iota(jnp.int32, shape, len(shape) - 1)


def _row_iota(shape):
    return lax.broadcasted_iota(jnp.int32, shape, len(shape) - 2)


def _dot_tn(a, b):
    return lax.dot_general(a, b, (((0,), (0,)), ((), ())), preferred_element_type=F32)


def _ada_kernel(c_ref, w_ref, b_ref, o_ref):
    c = c_ref[...]
    cond = c * jax.nn.sigmoid(c)
    o_ref[0] = jnp.dot(cond, w_ref[0], precision=HIGHEST, preferred_element_type=F32) + b_ref[0]


def _ada_mod(c, w_ada, b_ada):
    depth, d, n = w_ada.shape
    tn = 768
    c8 = jnp.broadcast_to(c.reshape(1, d), (8, d))
    out = pl.pallas_call(
        _ada_kernel,
        grid=(depth, n // tn),
        in_specs=[pl.BlockSpec((8, d), lambda l, j: (0, 0)),
                  pl.BlockSpec((1, d, tn), lambda l, j: (l, 0, j)),
                  pl.BlockSpec((1, 1, tn), lambda l, j: (l, 0, j))],
        out_specs=pl.BlockSpec((1, 8, tn), lambda l, j: (l, 0, j)),
        out_shape=jax.ShapeDtypeStruct((depth, 8, n), F32),
        compiler_params=_cparams("parallel", "parallel"),
        name="ada_mod",
    )(c8, w_ada, b_ada.reshape(depth, 1, n))
    return out[:, 0:1, :]


def _norm_mod(x, g, sc, sh):
    ms = jnp.mean(x * x, axis=-1, keepdims=True)
    return (x * lax.rsqrt(ms + EPS) * g) * (1.0 + sc) + sh


def _head_rms(x, gain, n_real):
    ss = jnp.sum(x * x, axis=-1, keepdims=True)
    return x * lax.rsqrt(ss * (1.0 / n_real) + EPS) * gain


def _rope64(x, cos, sin):
    lane = _lane_iota(x.shape)
    rot = jnp.where(lane < 32, -pltpu.roll(x, LANE - 32, 1), pltpu.roll(x, 32, 1))
    return x * cos + rot * sin


def _t_bf16(x, rows=LANE):
    return x.T[:rows].astype(BF16)


def _split3(c):
    hi = c.astype(BF16).astype(F32)
    r1 = c - hi
    mid = r1.astype(BF16).astype(F32)
    lo = (r1 - mid).astype(BF16).astype(F32)
    return hi, mid, lo


def _hy_prep_kernel(x_ref, ng_ref, sc_ref, sh_ref, w_ref, pos_ref, inv_ref, gq_ref, gk_ref, gfq_ref, gfk_ref,
                    fb_ref, qnt_ref, kct_ref, vct_ref, ks_ref, vst_ref, kw_ref, vwt_ref, gate_ref,
                    fqt_ref, fk_ref, fvt_ref, cedge_ref, carry_ref):
    i = pl.program_id(0)
    tm = PREP_TILE
    shp = (tm, LANE)
    lane = _lane_iota(shp)

    act = _norm_mod(x_ref[...], ng_ref[...], sc_ref[...], sh_ref[...]).astype(BF16)
    pairs = {}

    def blk(b):
        if b // 2 not in pairs:
            cols = slice((b // 2) * 2 * LANE, (b // 2 + 1) * 2 * LANE)
            pairs[b // 2] = jnp.dot(act, w_ref[:, cols], preferred_element_type=F32)
        return pairs[b // 2][:, (b % 2) * LANE:(b % 2 + 1) * LANE]

    ang = pos_ref[...] * inv_ref[...]
    real = lane < HEAD_DIM
    cos = jnp.where(real, jnp.cos(ang), 1.0)
    sin = jnp.where(real, jnp.sin(ang), 0.0)
    gq, gk, gfq, gfk = gq_ref[...], gk_ref[...], gfq_ref[...], gfk_ref[...]
    scale = HEAD_DIM ** -0.5 * LOG2E
    ones_row = lane == SUM_ROW
    ref_ones = jnp.where((lane >= REF_ROW) & (lane < REF_ROW + 3), 1.0, 0.0)

    for h in range(NSA_HEADS):
        q = _rope64(_head_rms(blk(HY_Q0 + h), gq, HEAD_DIM), cos, sin) * scale
        qnt_ref[h * LANE:(h + 1) * LANE, :] = _t_bf16(q)

    row = _row_iota(shp) + i * tm
    onehot = jnp.where(lane - HEAD_DIM == ((row // SLC_BLOCK) % SUPER_BLOCKS), 1.0, 0.0)
    for g in range(NSA_GROUPS):
        def kv(r):
            return blk(HY_KV0 + r * NSA_GROUPS + g)
        sl = slice(g * LANE, (g + 1) * LANE)
        kct_ref[g] = _rope64(kv(0), cos, sin)[:, :HEAD_DIM].astype(BF16)
        vct_ref[g] = kv(1)[:, :HEAD_DIM].astype(BF16)
        ks = _rope64(_head_rms(kv(2), gk, HEAD_DIM), cos, sin)
        ks_ref[:, sl] = (ks + onehot + ref_ones).astype(BF16)
        vst_ref[g, 0] = _t_bf16(jnp.where(ones_row, 1.0, kv(3)), V_ROWS)
        kw_ref[:, sl] = _rope64(_head_rms(kv(4), gk, HEAD_DIM), cos, sin).astype(BF16)
        vwt = _t_bf16(kv(5))
        for cidx in range(tm // LANE):
            vwt_ref[g, cidx] = vwt[:, cidx * LANE:(cidx + 1) * LANE]
        gate_ref[g] = jax.nn.sigmoid(blk(HY_G0 + g)).T[:GATE_ROWS]

    @pl.when(i == 0)
    def _():
        carry_ref[...] = jnp.zeros_like(carry_ref)

    z = blk(HY_FF) + fb_ref[...]
    logf = jnp.minimum(z, 0.0) - jnp.log1p(jnp.exp(-jnp.abs(z)))
    tri = jnp.where(_row_iota((tm, tm)) >= _lane_iota((tm, tm)), 1.0, 0.0).astype(F32)
    cum = jnp.dot(tri, logf, precision=HIGHEST, preferred_element_type=F32) + carry_ref[...]
    carry_ref[...] = cum[tm - 1:tm, :]
    cedge_ref[0] = jnp.concatenate([cum[0:1] * LOG2E, cum[tm - 1:tm] * LOG2E, jnp.zeros((6, LANE), F32)], axis=0)

    for h in range(FOX_HEADS):
        c = jnp.broadcast_to(cum[:, h:h + 1], shp) * LOG2E
        hi, mid, lo = _split3(c)
        fq = _head_rms(blk(HY_F0 + h), gfq, HEAD_DIM) * scale
        fq = jnp.where(real, fq, jnp.where(lane == 64, hi, jnp.where(lane == 65, mid, jnp.where(
            lane == 66, lo, jnp.where(lane < 70, 1.0, 0.0)))))
        fk = _head_rms(blk(HY_F0 + FOX_HEADS + h), gfk, HEAD_DIM)
        fk = jnp.where(real, fk, jnp.where(lane < 67, 1.0, jnp.where(lane == 67, -hi, jnp.where(
            lane == 68, -mid, jnp.where(lane == 69, -lo, ref_ones)))))
        sl = slice(h * LANE, (h + 1) * LANE)
        fqt_ref[sl, :] = _t_bf16(fq)
        fk_ref[:, sl] = fk.astype(BF16)
        fvt_ref[h, 0] = _t_bf16(jnp.where(ones_row, 1.0, blk(HY_F0 + 2 * FOX_HEADS + h)), V_ROWS)


def _feat_major(heads, s, tm):
    return (pl.BlockSpec((heads * LANE, tm), lambda i: (0, i)),
            jax.ShapeDtypeStruct((heads * LANE, s), BF16))


def _value_tiles(heads, s, tm, tk, rows=LANE):
    return (pl.BlockSpec((heads, tm // tk, rows, tk), lambda i: (0, i, 0, 0)),
            jax.ShapeDtypeStruct((heads, s // tk, rows, tk), BF16))


def _pad_feature_rows(o):
    return jnp.concatenate([o, jnp.zeros((LANE - HEAD_DIM, o.shape[1]), o.dtype)], axis=0)


def _hy_prep(x2, norm_g, sc, sh, w_in, posf, inv128, gq, gk, gfq, gfk, fbias):
    s, d = x2.shape
    tm = PREP_TILE
    assert tm == KV_TILE and w_in.shape == (d, HY_BLOCKS * LANE) and HY_BLOCKS % 2 == 0
    vec = pl.BlockSpec((1, LANE), lambda i: (0, 0))
    dvec = pl.BlockSpec((1, d), lambda i: (0, 0))

    def rows(nb):
        return (pl.BlockSpec((tm, nb * LANE), lambda i: (i, 0)), jax.ShapeDtypeStruct((s, nb * LANE), BF16))

    tok = (pl.BlockSpec((NSA_GROUPS, tm, HEAD_DIM), lambda i: (0, i, 0)),
           jax.ShapeDtypeStruct((NSA_GROUPS, s, HEAD_DIM), BF16))
    gate = (pl.BlockSpec((NSA_GROUPS, GATE_ROWS, tm), lambda i: (0, 0, i)),
            jax.ShapeDtypeStruct((NSA_GROUPS, GATE_ROWS, s), F32))
    outs = [_feat_major(NSA_HEADS, s, tm), tok, tok, rows(NSA_GROUPS),
            _value_tiles(NSA_GROUPS, s, tm, KV_TILE, V_ROWS),
            rows(NSA_GROUPS), _value_tiles(NSA_GROUPS, s, tm, LANE), gate,
            _feat_major(FOX_HEADS, s, tm), rows(FOX_HEADS), _value_tiles(FOX_HEADS, s, tm, KV_TILE, V_ROWS),
            (pl.BlockSpec((1, 8, LANE), lambda i: (i, 0, 0)), jax.ShapeDtypeStruct((s // tm, 8, LANE), F32))]
    return pl.pallas_call(
        _hy_prep_kernel,
        grid=(s // tm,),
        in_specs=[pl.BlockSpec((tm, d), lambda i: (i, 0)), dvec, dvec, dvec,
                  pl.BlockSpec((d, HY_BLOCKS * LANE), lambda i: (0, 0), pipeline_mode=pl.Buffered(1)),
                  pl.BlockSpec((tm, 1), lambda i: (i, 0)), vec, vec, vec, vec, vec, vec],
        out_specs=[o[0] for o in outs],
        out_shape=[o[1] for o in outs],
        scratch_shapes=[pltpu.VMEM((1, LANE), F32)],
        compiler_params=_cparams("arbitrary"),
        name="hybrid_prep",
    )(x2, norm_g, sc, sh, w_in, posf, inv128, gq, gk, gfq, gfk, fbias)


def _compress_kernel(kc_ref, vc_ref, wk_ref, wv_ref, pek_ref, pev_ref, gk_ref, ko_ref, vo_ref):
    half = CMP_STRIDE * HEAD_DIM

    def comp(ch_ref, w_ref, pe_ref):
        ch = ch_ref[0]
        nc = ch.shape[0]
        a = jnp.dot(ch, w_ref[:half], preferred_element_type=F32)
        b = jnp.dot(ch, w_ref[half:], preferred_element_type=F32)
        nxt = pltpu.roll(b, nc - 1, 0)
        pe = jnp.dot(jnp.broadcast_to(pe_ref[...], (8, 2 * half)).astype(BF16), w_ref[...],
                     preferred_element_type=F32)[0:1]
        return a + nxt + pe

    ko_ref[0] = _head_rms(comp(kc_ref, wk_ref, pek_ref), gk_ref[...], HEAD_DIM).astype(BF16)
    vo_ref[0] = comp(vc_ref, wv_ref, pev_ref).T.astype(BF16)


def _compress(kct, vct, w_cmp, cmp_pe, k_norm):
    g, s, _ = kct.shape
    nc = s // CMP_STRIDE
    wide = CMP_STRIDE * HEAD_DIM
    kch = kct.reshape(g, nc, wide)
    vch = vct.reshape(g, nc, wide)
    w_pad = jnp.pad(w_cmp, ((0, 0), (0, 0), (0, LANE - HEAD_DIM))).astype(BF16)
    ch = pl.BlockSpec((1, nc, wide), lambda i: (i, 0, 0))
    wspec = pl.BlockSpec((2 * wide, LANE), lambda i: (0, 0))
    pespec = pl.BlockSpec((1, 2 * wide), lambda i: (0, 0))
    return pl.pallas_call(
        _compress_kernel,
        grid=(g,),
        in_specs=[ch, ch, wspec, wspec, pespec, pespec, pl.BlockSpec((1, LANE), lambda i: (0, 0))],
        out_specs=[pl.BlockSpec((1, nc, LANE), lambda i: (i, 0, 0)),
                   pl.BlockSpec((1, LANE, nc), lambda i: (i, 0, 0))],
        out_shape=[jax.ShapeDtypeStruct((g, nc, LANE), BF16), jax.ShapeDtypeStruct((g, LANE, nc), BF16)],
        compiler_params=_cparams("parallel"),
        name="nsa_compress",
    )(kch, vch, w_pad[0], w_pad[1], cmp_pe[0].reshape(1, 2 * wide).astype(F32),
      cmp_pe[1].reshape(1, 2 * wide).astype(F32), _pad_lanes(k_norm))


def _masked_softmax_t(s, mask):
    s = jnp.where(mask, s, NEG)
    m = jnp.max(s, axis=0, keepdims=True)
    e = jnp.exp2(s - m)
    inv = 1.0 / jnp.maximum(jnp.sum(e, axis=0, keepdims=True), 1e-30)
    return e, jnp.where(m > 0.5 * NEG, inv, 0.0)


def _online_steps(steps, ms, acc_ref):
    ms = list(ms)

    def scores(step):
        k_tile, qa, _, c, mask = step
        s = jnp.dot(k_tile, qa, preferred_element_type=F32)
        if mask is not None:
            s = jnp.where(mask, s, NEG)
        return s, jnp.max(s, axis=0, keepdims=True)

    nxt = scores(steps[0])
    for idx, (_, _, vt, c, _) in enumerate(steps):
        sl = slice(c * Q_STRIP, (c + 1) * Q_STRIP)
        s, s_max = nxt
        if idx + 1 < len(steps):
            nxt = scores(steps[idx + 1])
        m_new = jnp.maximum(ms[c], s_max)
        a = jnp.exp2(ms[c] - m_new)
        p = jnp.exp2((s - m_new).astype(BF16))
        ms[c] = m_new
        acc_ref[:, sl] = a * acc_ref[:, sl] + jnp.dot(vt, p, preferred_element_type=F32)
    return tuple(ms)


def _m_init(n_strips):
    return tuple(jnp.full((1, Q_STRIP), NEG, F32) for _ in range(n_strips))


def _with_ref_rows(qa, m):
    hi, mid, lo = _split3(-m)
    r = _row_iota((LANE - REF_SLAB, Q_STRIP)) + REF_SLAB
    slab = jnp.where(r == REF_ROW, hi, jnp.where(r == REF_ROW + 1, mid, jnp.where(r == REF_ROW + 2, lo, 0.0)))
    return jnp.concatenate([qa[:REF_SLAB], slab.astype(BF16)], axis=0)


def _first_tile_max(k_tile, qa_strips, masks):
    return tuple(jnp.max(jnp.where(mask, jnp.dot(k_tile, qa, preferred_element_type=F32), NEG), axis=0, keepdims=True)
                 for qa, mask in zip(qa_strips, masks))


def _fast_steps(steps, state, acc_ref):
    state = list(state)
    depth, last = min(SCORE_LOOKAHEAD, len(steps)), {}
    for k, step in enumerate(steps):
        depth = min(depth, k - last.get(step[3], k - depth))
        last[step[3]] = k

    def scores(step):
        k_tile, qa, _, c, mask = step
        s = jnp.dot(k_tile, _with_ref_rows(qa, state[c][0]), preferred_element_type=F32)
        if mask is not None:
            s = jnp.where(mask, s, NEG)
        return s

    ahead = [scores(st) for st in steps[:depth]]
    for idx, (_, _, vt, c, _) in enumerate(steps):
        sl = slice(c * Q_STRIP, (c + 1) * Q_STRIP)
        s = ahead.pop(0)
        m, worst = state[c]
        cm = jnp.max(s, axis=0, keepdims=True)
        inc = jnp.maximum(cm, 0.0)
        state[c] = (m + inc, jnp.maximum(worst, cm))
        if idx + depth < len(steps):
            ahead.append(scores(steps[idx + depth]))
        p = jnp.exp2(s).astype(BF16)
        acc_ref[:, sl] = jnp.exp2(-inc) * (acc_ref[:, sl] + jnp.dot(vt, p, preferred_element_type=F32))
    return tuple(state)


def _flat(state):
    return tuple(x for pair in state for x in pair)


def _nest(flat):
    return tuple((flat[2 * c], flat[2 * c + 1]) for c in range(len(flat) // 2))


def _nsa_kernel(qt_ref, kc_ref, vct_ref, ks_ref, vst_ref, kw_ref, vwt_ref, gate_ref, ovt_ref, o_ref,
                qaug_ref, acc_ref, *, n_sel):
    i = pl.program_id(1)
    tq = Q_TILE_NSA
    cols = NSA_HPG * tq
    qs = i * tq
    nc = kc_ref.shape[1]
    nslc = ovt_ref.shape[0]
    n_super = nslc // SUPER_BLOCKS

    qt = jnp.concatenate([qt_ref[h * LANE:(h + 1) * LANE, :] for h in range(NSA_HPG)], axis=1)
    tq_row = qs + (_lane_iota((1, cols)) % tq)

    def compress_and_select(n_c, n_b):
        s = jnp.dot(kc_ref[0, :n_c], qt, preferred_element_type=F32)
        cmp_end = _row_iota((n_c, 1)) * CMP_STRIDE + (CMP_LEN - 1)
        e, inv_l = _masked_softmax_t(s, cmp_end <= tq_row)
        o_cmp = jnp.dot(vct_ref[0, :, :n_c], e.astype(BF16), preferred_element_type=F32) * inv_l

        psum = e[:, 0:tq] * inv_l[:, 0:tq]
        for h in range(1, NSA_HPG):
            psum = psum + e[:, h * tq:(h + 1) * tq] * inv_l[:, h * tq:(h + 1) * tq]
        p_hi = psum.astype(BF16)
        p_lo = (psum - p_hi.astype(F32)).astype(BF16)
        ovt = ovt_ref[:n_b, :n_c]
        imp = (jnp.dot(ovt, p_hi, preferred_element_type=F32)
               + jnp.dot(ovt, p_lo, preferred_element_type=F32))

        jj = _row_iota((n_b, tq))
        tq_blk = qs + _lane_iota((n_b, tq))
        cur = tq_blk // SLC_BLOCK
        forced = (jj == 0) | (jj == cur) | (jj == cur - 1)
        causal_blk = jj * SLC_BLOCK <= tq_blk
        val = jnp.where(forced, imp + BIG, imp)
        val = jnp.where(causal_blk, val, NEG)
        jjf = jj.astype(F32)

        def pick(_, carry):
            val, sel = carry
            mx = jnp.max(val, axis=0, keepdims=True)
            idx = jnp.min(jnp.where(val == mx, jjf, float(n_b)), axis=0, keepdims=True)
            hit = jjf == idx
            return jnp.where(hit, -jnp.inf, val), jnp.where(hit, 1.0, sel)

        _, sel = lax.fori_loop(0, min(n_sel, n_b), pick, (val, jnp.zeros((n_b, tq), F32)))
        bias = jnp.where((sel > 0.0) & causal_blk, 0.0, NEG)
        if n_b < nslc:
            bias = jnp.concatenate([bias, jnp.full((nslc - n_b, tq), NEG, F32)], axis=0)
        return o_cmp, bias

    parts = max(1, min(NSA_PREFIX_PARTS, nc // LANE))
    seq = nc * CMP_STRIDE

    def dispatch(k):
        full = lambda: compress_and_select(nc * k // parts, nslc * k // parts)
        if k == parts:
            return full()
        return lax.cond(qs + tq <= seq * k // parts, full, lambda: dispatch(k + 1))

    o_cmp, bias_t = dispatch(1)

    q_rows = qt[:HEAD_DIM].astype(F32)
    spare = jnp.zeros((LANE - HEAD_DIM - SUPER_BLOCKS, cols), F32)
    for st in range(n_super):
        b = bias_t[st * SUPER_BLOCKS:(st + 1) * SUPER_BLOCKS]
        b = jnp.concatenate([b] * NSA_HPG, axis=1)
        qaug_ref[st] = jnp.concatenate([q_rows, b, spare], axis=0).astype(BF16)

    tk = KV_TILE
    per_super = SUPER_BLOCKS * SLC_BLOCK // tk
    j_last = (qs + tq - 1) // tk
    n_strips = cols // Q_STRIP
    strips = [slice(c * Q_STRIP, (c + 1) * Q_STRIP) for c in range(n_strips)]

    def causal_masks(j):
        kpos = j * tk + _row_iota((tk, 1))
        return [kpos <= tq_row[:, sl] for sl in strips]

    def slc_steps(j, masks=None):
        k0 = pl.multiple_of(j * tk, tk)
        k_tile, vt, st = ks_ref[pl.ds(k0, tk), :], vst_ref[0, j], j // per_super
        return [(k_tile, qaug_ref[st, :, strips[c]], vt, c, None if masks is None else masks[c])
                for c in range(n_strips)]

    group = 4

    def grouped(jj, flat):
        steps = [st for t in range(group) for st in slc_steps(group * jj + t)]
        return _flat(_fast_steps(steps, _nest(flat), acc_ref))

    def single(j, flat):
        return _flat(_fast_steps(slc_steps(j), _nest(flat), acc_ref))

    acc_ref[...] = jnp.zeros(acc_ref.shape, F32)
    head_rows = 16
    head_masks = [_row_iota((head_rows, 1)) <= tq_row[:, sl] for sl in strips]
    m0 = _first_tile_max(ks_ref[0:head_rows, :], [qaug_ref[0, :, sl] for sl in strips], head_masks)
    n_groups = j_last // group
    flat = lax.fori_loop(0, n_groups, grouped, _flat(tuple((m, jnp.zeros_like(m)) for m in m0)))
    flat = lax.fori_loop(group * n_groups, j_last, single, flat)
    state = _fast_steps(slc_steps(j_last, causal_masks(j_last)), _nest(flat), acc_ref)
    worst = jnp.max(jnp.concatenate([w for _, w in state], axis=1))

    @pl.when(worst > EXP_GUARD)
    def _():
        acc_ref[...] = jnp.zeros(acc_ref.shape, F32)
        lax.fori_loop(0, j_last + 1, lambda j, ms: _online_steps(slc_steps(j, causal_masks(j)), ms, acc_ref),
                      _m_init(n_strips))

    o_slc = acc_ref[:HEAD_DIM] / jnp.maximum(acc_ref[SUM_ROW:SUM_ROW + 1], 1e-30)

    wlen = WINDOW + tq
    ws = pl.multiple_of(jnp.maximum(qs - WINDOW, 0), tq)
    s = jnp.dot(kw_ref[pl.ds(ws, wlen), :], qt, preferred_element_type=F32)
    dist = tq_row - (ws + _row_iota((wlen, 1)))
    e, inv_l = _masked_softmax_t(s, (dist >= 0) & (dist < WINDOW))
    e = e.astype(BF16)
    wb = ws // LANE
    o_win = jnp.zeros((LANE, cols), F32)
    for c in range(wlen // LANE):
        o_win = o_win + jnp.dot(vwt_ref[0, wb + c], e[c * LANE:(c + 1) * LANE], preferred_element_type=F32)
    o_win = o_win * inv_l

    gate = gate_ref[0]
    for h in range(NSA_HPG):
        sl = slice(h * tq, (h + 1) * tq)
        o = (gate[3 * h:3 * h + 1] * o_cmp[:HEAD_DIM, sl] + gate[3 * h + 1:3 * h + 2] * o_slc[:, sl]
             + gate[3 * h + 2:3 * h + 3] * o_win[:HEAD_DIM, sl])
        o_ref[:, h * LANE:(h + 1) * LANE] = _pad_feature_rows(o).T.astype(BF16)


def _overlap_t(s, nslc_pad):
    nc = s // CMP_STRIDE
    cmp_start = np.arange(nc) * CMP_STRIDE
    slc_start = np.arange(nslc_pad) * SLC_BLOCK
    ov = np.clip(np.minimum(cmp_start[:, None] + CMP_LEN, slc_start[None, :] + SLC_BLOCK)
                 - np.maximum(cmp_start[:, None], slc_start[None, :]), 0, None) / CMP_STRIDE
    ov[nc - CMP_LEN // CMP_STRIDE + 1:, :] = 0.0
    ov[:, s // SLC_BLOCK:] = 0.0
    return jnp.asarray(ov.T, BF16)


def _nsa_attention(qnt, kc, vct, ks, vst, kw, vwt, gates):
    s = qnt.shape[1]
    nc = s // CMP_STRIDE
    n_slc = s // SLC_BLOCK
    nslc_pad = -(-n_slc // LANE) * LANE
    tq = Q_TILE_NSA
    cols = NSA_HPG * tq
    once = pl.Buffered(1)
    res = pl.BlockSpec((s, LANE), lambda g, i: (0, g), pipeline_mode=once)
    return pl.pallas_call(
        functools.partial(_nsa_kernel, n_sel=min(SLC_TOPK, n_slc)),
        grid=(NSA_GROUPS, s // tq),
        in_specs=[pl.BlockSpec((NSA_HPG * LANE, tq), lambda g, i: (g, i)),
                  pl.BlockSpec((1, nc, LANE), lambda g, i: (g, 0, 0), pipeline_mode=once),
                  pl.BlockSpec((1, LANE, nc), lambda g, i: (g, 0, 0), pipeline_mode=once),
                  res, pl.BlockSpec((1, s // KV_TILE, V_ROWS, KV_TILE), lambda g, i: (g, 0, 0, 0),
                                    pipeline_mode=once),
                  res, pl.BlockSpec((1, s // LANE, LANE, LANE), lambda g, i: (g, 0, 0, 0), pipeline_mode=once),
                  pl.BlockSpec((1, GATE_ROWS, tq), lambda g, i: (g, 0, i)),
                  pl.BlockSpec((nslc_pad, nc), lambda g, i: (0, 0), pipeline_mode=once)],
        out_specs=pl.BlockSpec((tq, NSA_HPG * LANE), lambda g, i: (i, g)),
        out_shape=jax.ShapeDtypeStruct((s, NSA_HEADS * LANE), BF16),
        scratch_shapes=[pltpu.VMEM((nslc_pad // SUPER_BLOCKS, LANE, cols), BF16),
                        pltpu.VMEM((V_ROWS, cols), F32)],
        compiler_params=_cparams("parallel", "arbitrary"),
        name="nsa_attention",
    )(qnt, kc, vct, ks, vst, kw, vwt, gates, _overlap_t(s, nslc_pad))


def _flash_kernel(cfirst_ref, clast_ref, slack_ref, qt_ref, k_ref, vt_ref, o_ref, acc_ref, *, tq, decay):
    h = pl.program_id(0)
    i = pl.program_id(1)
    tk = KV_TILE
    n_tiles = k_ref.shape[0] // tk
    acc_ref[...] = jnp.zeros(acc_ref.shape, F32)

    n_strips = tq // Q_STRIP
    per_q = tq // tk
    qas = [qt_ref[:, c * Q_STRIP:(c + 1) * Q_STRIP] for c in range(n_strips)]

    def tile_steps(j, d=None):
        k0 = pl.multiple_of(j * tk, tk)
        k_tile, vt = k_ref[pl.ds(k0, tk), :], vt_ref[0, j]
        steps = []
        for c in range(n_strips):
            mask = None
            if d is not None:
                if d * tk > (c + 1) * Q_STRIP - 1:
                    continue
                if (d + 1) * tk - 1 > c * Q_STRIP:
                    shp = (tk, Q_STRIP)
                    mask = _row_iota(shp) + d * tk <= _lane_iota(shp) + c * Q_STRIP
            steps.append((k_tile, qas[c], vt, c, mask))
        return steps

    def any_tile_masks(j):
        shp = (tk, Q_STRIP)
        return [_row_iota(shp) + j * tk <= _lane_iota(shp) + (i * tq + c * Q_STRIP) for c in range(n_strips)]

    def below(t, flat):
        jj = i - 1 - t
        bound = (slack_ref[0] + cfirst_ref[h * n_tiles + i * per_q]
                 - clast_ref[h * n_tiles + jj * per_q + per_q - 1])

        def run(flat):
            steps = [st for u in range(per_q) for st in tile_steps(jj * per_q + (per_q - 1 - u))]
            return _flat(_fast_steps(steps, _nest(flat), acc_ref))

        return lax.cond(bound >= -SKIP_MARGIN, run, lambda flat: flat, flat)

    diag0 = i * per_q
    own = [(c * Q_STRIP) // tk for c in range(n_strips)]
    head = tk if decay else 16

    def start_max(c):
        j = diag0 + own[c]
        keys = k_ref[pl.ds(pl.multiple_of(j * tk, tk), head), :]
        shp = (head, Q_STRIP)
        mask = _row_iota(shp) + j * tk <= _lane_iota(shp) + (i * tq + c * Q_STRIP)
        return _first_tile_max(keys, [qas[c]], [mask])[0]

    m0 = tuple(start_max(c) for c in range(n_strips))
    steps = [st for d in reversed(range(per_q)) for st in tile_steps(diag0 + d, d)]
    state = _fast_steps(steps, tuple((m, jnp.zeros_like(m)) for m in m0), acc_ref)
    state = _nest(lax.fori_loop(0, i, below, _flat(state)))
    worst = jnp.max(jnp.concatenate([w for _, w in state], axis=1))

    @pl.when(worst > EXP_GUARD)
    def _():
        acc_ref[...] = jnp.zeros(acc_ref.shape, F32)

        def exact(j, ms):
            k0 = pl.multiple_of(j * tk, tk)
            k_tile, vt, masks = k_ref[pl.ds(k0, tk), :], vt_ref[0, j], any_tile_masks(j)
            return _online_steps([(k_tile, qas[c], vt, c, masks[c]) for c in range(n_strips)], ms, acc_ref)

        lax.fori_loop(0, (i + 1) * per_q, exact, _m_init(n_strips))

    o = _pad_feature_rows(acc_ref[:HEAD_DIM] / acc_ref[SUM_ROW:SUM_ROW + 1])
    for c0 in range(0, tq, LANE):
        o_ref[c0:c0 + LANE, :] = o[:, c0:c0 + LANE].T.astype(BF16)


def _causal_attention(qt, k, vt, tq, bias_edges=None, slack=None):
    s, width = k.shape
    heads = width // LANE
    tq = min(tq, s)
    assert tq % KV_TILE == 0 and s % tq == 0
    n_tiles = s // KV_TILE
    if bias_edges is None:
        first = last = jnp.zeros((heads * n_tiles,), F32)
        slack = jnp.full((1,), -NEG, F32)
    else:
        first, last = (e.reshape(heads * n_tiles).astype(F32) for e in bias_edges)
    grid_spec = pltpu.PrefetchScalarGridSpec(
        num_scalar_prefetch=3,
        grid=(heads, s // tq),
        in_specs=[pl.BlockSpec((LANE, tq), lambda h, i, *_: (h, i)),
                  pl.BlockSpec((s, LANE), lambda h, i, *_: (0, h)),
                  pl.BlockSpec((1, n_tiles, V_ROWS, KV_TILE), lambda h, i, *_: (h, 0, 0, 0))],
        out_specs=pl.BlockSpec((tq, LANE), lambda h, i, *_: (i, h)),
        scratch_shapes=[pltpu.VMEM((V_ROWS, tq), F32)],
    )
    return pl.pallas_call(
        functools.partial(_flash_kernel, tq=tq, decay=bias_edges is not None),
        grid_spec=grid_spec,
        out_shape=jax.ShapeDtypeStruct((s, width), BF16),
        compiler_params=_cparams("parallel", "arbitrary"),
        name="causal_attention",
    )(first, last, slack.astype(F32), qt, k, vt)


def _out_proj_kernel(oa_ref, ob_ref, wa_ref, wb_ref, x_ref, g_ref, o_ref):
    y = jnp.dot(oa_ref[...], wa_ref[...], preferred_element_type=F32)
    y = y + jnp.dot(ob_ref[...], wb_ref[...], preferred_element_type=F32)
    o_ref[...] = x_ref[...] + g_ref[...] * y


def _out_proj(oa, ob, cola, colb, wa, wb, x2, gate):
    s, d = x2.shape
    ka = wa.shape[0]
    tm = ROW_TILE
    return pl.pallas_call(
        _out_proj_kernel,
        grid=(s // tm,),
        in_specs=[pl.BlockSpec((tm, ka), lambda i: (i, cola)), pl.BlockSpec((tm, ka), lambda i: (i, colb)),
                  pl.BlockSpec((ka, d), lambda i: (0, 0)), pl.BlockSpec((ka, d), lambda i: (0, 0)),
                  pl.BlockSpec((tm, d), lambda i: (i, 0)), pl.BlockSpec((1, d), lambda i: (0, 0))],
        out_specs=pl.BlockSpec((tm, d), lambda i: (i, 0)),
        out_shape=jax.ShapeDtypeStruct((s, d), F32),
        compiler_params=_cparams("parallel"),
        name="out_proj",
    )(oa, ob, wa, wb, x2, gate)


def _mla_prep_kernel(x_ref, ng_ref, sc_ref, sh_ref, win_ref, pos_ref, inv_ref, gqa_ref, gkva_ref, wuq_ref, wuk_ref,
                     wuv_ref, gq_ref, gk_ref, gkr_ref, qt_ref, k_ref, vt_ref):
    shp = (PREP_TILE, LANE)
    act = _norm_mod(x_ref[...], ng_ref[...], sc_ref[...], sh_ref[...]).astype(BF16)
    proj = jnp.dot(act, win_ref[...], preferred_element_type=F32)
    lane = _lane_iota(shp)
    nope = lane < QK_NOPE
    rope = (lane >= QK_NOPE) & (lane < QK_NOPE + QK_ROPE)
    ref_ones = jnp.where((lane >= REF_ROW) & (lane < REF_ROW + 3), 1.0, 0.0)
    ang = pos_ref[...] * inv_ref[...]
    cos = jnp.where(rope, jnp.cos(ang), 1.0)
    sin = jnp.where(rope, jnp.sin(ang), 0.0)

    def rope32(x):
        half = QK_ROPE // 2
        rot = jnp.where(lane < QK_NOPE + half, -pltpu.roll(x, LANE - half, 1), pltpu.roll(x, half, 1))
        return x * cos + rot * sin

    def low_rank_norm(x, g):
        ms = jnp.mean(x * x, axis=-1, keepdims=True)
        return (x * lax.rsqrt(ms + EPS) * g).astype(BF16)

    nq = Q_LORA // LANE
    cq = low_rank_norm(proj[:, :Q_LORA], gqa_ref[...])
    ckv = low_rank_norm(proj[:, Q_LORA:Q_LORA + KV_LORA], gkva_ref[...])
    kr = proj[:, (nq + KV_LORA // LANE) * LANE:(nq + KV_LORA // LANE + 1) * LANE]
    k_rope = rope32(_head_rms(kr, gkr_ref[...], QK_ROPE))

    gq, gk = gq_ref[...], gk_ref[...]
    scale = (QK_NOPE + QK_ROPE) ** -0.5 * LOG2E
    pair = 2 * LANE
    for hp in range(MLA_HEADS // 2):
        cols = slice(hp * pair, (hp + 1) * pair)
        q2 = jnp.dot(cq, wuq_ref[:, cols], preferred_element_type=F32)
        k2 = jnp.dot(ckv, wuk_ref[:, cols], preferred_element_type=F32)
        v2 = jnp.dot(ckv, wuv_ref[:, cols], preferred_element_type=F32)
        for sub in range(2):
            head = 2 * hp + sub
            sl = slice(head * LANE, (head + 1) * LANE)
            half = slice(sub * LANE, (sub + 1) * LANE)
            x = q2[:, half]
            ss_n = jnp.sum(jnp.where(nope, x * x, 0.0), axis=-1, keepdims=True)
            ss_r = jnp.sum(jnp.where(rope, x * x, 0.0), axis=-1, keepdims=True)
            inv_rms = jnp.where(nope, lax.rsqrt(ss_n * (1.0 / QK_NOPE) + EPS),
                                lax.rsqrt(ss_r * (1.0 / QK_ROPE) + EPS))
            qt_ref[sl, :] = _t_bf16(rope32(x * inv_rms * gq) * scale)
            kn = _head_rms(k2[:, half], gk, QK_NOPE)
            k_ref[:, sl] = (kn + k_rope + ref_ones).astype(BF16)
            vt_ref[head, 0] = _t_bf16(jnp.where(lane == SUM_ROW, 1.0, v2[:, half]), V_ROWS)


def _mla_prep(x2, norm_g, sc, sh, w_in, posf, inv128, gqa, gkva, wuq, wuk, wuv, gq, gk, gkr):
    s, d = x2.shape
    tm = PREP_TILE
    assert tm == KV_TILE

    def full(a):
        return pl.BlockSpec(a.shape, lambda i: (0, 0))

    outs = [_feat_major(MLA_HEADS, s, tm),
            (pl.BlockSpec((tm, MLA_HEADS * LANE), lambda i: (i, 0)),
             jax.ShapeDtypeStruct((s, MLA_HEADS * LANE), BF16)),
            _value_tiles(MLA_HEADS, s, tm, KV_TILE, V_ROWS)]
    head = (norm_g, sc, sh, w_in)
    tail = (inv128, gqa, gkva, wuq, wuk, wuv, gq, gk, gkr)
    return pl.pallas_call(
        _mla_prep_kernel,
        grid=(s // tm,),
        in_specs=[pl.BlockSpec((tm, d), lambda i: (i, 0))] + [full(a) for a in head]
                 + [pl.BlockSpec((tm, 1), lambda i: (i, 0))] + [full(a) for a in tail],
        out_specs=[o[0] for o in outs],
        out_shape=[o[1] for o in outs],
        compiler_params=_cparams("parallel"),
        name="mla_prep",
    )(x2, *head, posf, *tail)


def _rank_lt(v, k):
    n = v.shape[0]
    row = _row_iota(v.shape)
    rank = jnp.zeros(v.shape, F32)
    for b in range(n):
        vb = v[b:b + 1, :]
        rank = rank + jnp.where((vb > v) | ((vb == v) & (row > b)), 1.0, 0.0)
    return rank < k


def _top_rows(v, k):
    rowf = _row_iota(v.shape).astype(F32)
    chosen = jnp.zeros(v.shape, F32)
    for _ in range(k):
        mx = jnp.max(v, axis=0, keepdims=True)
        idx = jnp.min(jnp.where(v == mx, rowf, float(v.shape[0])), axis=0, keepdims=True)
        hit = rowf == idx
        chosen = jnp.where(hit, 1.0, chosen)
        v = jnp.where(hit, -jnp.inf, v)
    return chosen > 0.0


def _moe_route_kernel(x_ref, g_ref, sc_ref, sh_ref, wr_ref, rb_ref, h_ref, pos_ref, wt_ref, cnt_ref):
    tm = ROW_TILE
    h = _norm_mod(x_ref[...], g_ref[...], sc_ref[...], sh_ref[...])
    h_ref[...] = h.astype(BF16)
    logits = jnp.dot(h, wr_ref[...], precision=HIGHEST, preferred_element_type=F32)
    lt = logits.T[:N_EXPERTS]
    scores = jax.nn.sigmoid(lt)
    sel = scores + rb_ref[...]

    per = N_EXPERTS // N_GROUPS
    grp = sel.reshape(N_GROUPS, per, tm)
    sub = lax.broadcasted_iota(jnp.int32, grp.shape, 1)
    m1 = jnp.max(grp, axis=1, keepdims=True)
    first = jnp.min(jnp.where(grp == m1, sub, per), axis=1, keepdims=True)
    m2 = jnp.max(jnp.where(sub == first, -jnp.inf, grp), axis=1, keepdims=True)
    gscore = (m1 + m2).reshape(N_GROUPS, tm)
    gmask = _rank_lt(gscore, TOPK_GROUPS)
    emask = jnp.broadcast_to(gmask.reshape(N_GROUPS, 1, tm), grp.shape).reshape(N_EXPERTS, tm)
    chosen = _top_rows(jnp.where(emask, sel, NEG), TOP_K)

    w = jnp.where(chosen, scores, 0.0)
    wt_ref[...] = w / jnp.sum(w, axis=0, keepdims=True) * ROUTED_SCALE

    upper = jnp.where(_row_iota((tm, tm)) <= _lane_iota((tm, tm)), 1.0, 0.0).astype(BF16)
    incl = jnp.dot(jnp.where(chosen, 1.0, 0.0).astype(BF16), upper, preferred_element_type=F32)
    pos_ref[...] = jnp.where(chosen, incl - 1.0, -1.0)
    cnt_ref[0] = jnp.broadcast_to(incl[:, tm - 1:tm], (N_EXPERTS, LANE))


def _moe_route(x2, g, sc, sh, w_router_pad, router_bias_col):
    s, d = x2.shape
    tm = ROW_TILE
    vec = pl.BlockSpec((1, d), lambda i: (0, 0))
    et = pl.BlockSpec((N_EXPERTS, tm), lambda i: (0, i))
    return pl.pallas_call(
        _moe_route_kernel,
        grid=(s // tm,),
        in_specs=[pl.BlockSpec((tm, d), lambda i: (i, 0)), vec, vec, vec,
                  pl.BlockSpec((d, LANE), lambda i: (0, 0)),
                  pl.BlockSpec((N_EXPERTS, 1), lambda i: (0, 0))],
        out_specs=[pl.BlockSpec((tm, d), lambda i: (i, 0)), et, et,
                   pl.BlockSpec((1, N_EXPERTS, LANE), lambda i: (i, 0, 0))],
        out_shape=[jax.ShapeDtypeStruct((s, d), BF16), jax.ShapeDtypeStruct((N_EXPERTS, s), F32),
                   jax.ShapeDtypeStruct((N_EXPERTS, s), F32),
                   jax.ShapeDtypeStruct((s // tm, N_EXPERTS, LANE), F32)],
        compiler_params=_cparams("parallel"),
        name="moe_route",
    )(x2, g, sc, sh, w_router_pad, router_bias_col)


def _moe_kernel(cnt_ref, x_ref, h_ref, pos_ref, wt_ref, wg_ref, wu_ref, wd_ref, sg_ref, su_ref, sd_ref,
                g2_ref, o_ref, acc_ref):
    i = pl.program_id(0)
    e = pl.program_id(1)
    tm = ROW_TILE
    r = MOE_CHUNK

    @pl.when(e == 0)
    def _():
        h = h_ref[...]
        a = jnp.dot(h, sg_ref[...], preferred_element_type=F32)
        a = a * jax.nn.sigmoid(a) * jnp.dot(h, su_ref[...], preferred_element_type=F32)
        acc_ref[...] = jnp.dot(a.astype(BF16), sd_ref[...], preferred_element_type=F32)

    first = e * MOE_EXPERTS_PER_STEP
    n = cnt_ref[i * N_EXPERTS + first]
    for k in range(1, MOE_EXPERTS_PER_STEP):
        n = jnp.maximum(n, cnt_ref[i * N_EXPERTS + first + k])
    prows = [pos_ref[pl.ds(first + k, 1), :] for k in range(MOE_EXPERTS_PER_STEP)]
    wrows = [wt_ref[pl.ds(first + k, 1), :] for k in range(MOE_EXPERTS_PER_STEP)]

    def chunk(c, _):
        slot = (_row_iota((r, tm)) + c * r).astype(F32)
        hits = [prow == slot for prow in prows]
        onehot = jnp.concatenate([jnp.where(hit, 1.0, 0.0).astype(BF16) for hit in hits], axis=0)
        xg = jnp.dot(onehot, h_ref[...], preferred_element_type=F32).astype(BF16)
        ys = []
        for k in range(MOE_EXPERTS_PER_STEP):
            xk = xg[k * r:(k + 1) * r]
            a = jnp.dot(xk, wg_ref[k], preferred_element_type=F32)
            a = a * jax.nn.sigmoid(a) * jnp.dot(xk, wu_ref[k], preferred_element_type=F32)
            y = jnp.dot(a.astype(BF16), wd_ref[k], preferred_element_type=F32)
            wr = jnp.sum(jnp.where(hits[k], wrows[k], 0.0), axis=-1, keepdims=True)
            ys.append((y * wr).astype(BF16))
        acc_ref[...] += _dot_tn(onehot, jnp.concatenate(ys, axis=0))
        return 0

    lax.fori_loop(0, (n + r - 1) // r, chunk, 0)

    @pl.when(e == N_EXPERTS // MOE_EXPERTS_PER_STEP - 1)
    def _():
        o_ref[...] = x_ref[...] + g2_ref[...] * acc_ref[...]


def _moe_experts(counts, x2, h, pos_t, w_t, wg, wu, wd, sg, su, sd, g2):
    s, d = x2.shape
    tm = ROW_TILE
    ff = wg.shape[2]
    tile = pl.BlockSpec((tm, d), lambda i, e, c: (i, 0))
    et = pl.BlockSpec((N_EXPERTS, tm), lambda i, e, c: (0, i))

    def const(a):
        return pl.BlockSpec(a.shape, lambda i, e, c: (0,) * a.ndim)

    per = MOE_EXPERTS_PER_STEP
    grid_spec = pltpu.PrefetchScalarGridSpec(
        num_scalar_prefetch=1,
        grid=(s // tm, N_EXPERTS // per),
        in_specs=[tile, tile, et, et,
                  pl.BlockSpec((per, d, ff), lambda i, e, c: (e, 0, 0)),
                  pl.BlockSpec((per, d, ff), lambda i, e, c: (e, 0, 0)),
                  pl.BlockSpec((per, ff, d), lambda i, e, c: (e, 0, 0)),
                  const(sg), const(su), const(sd), const(g2)],
        out_specs=tile,
        scratch_shapes=[pltpu.VMEM((tm, d), F32)],
    )
    return pl.pallas_call(
        _moe_kernel,
        grid_spec=grid_spec,
        out_shape=jax.ShapeDtypeStruct((s, d), F32),
        compiler_params=_cparams("parallel", "arbitrary"),
        name="moe_experts",
    )(counts, x2, h, pos_t, w_t, wg, wu, wd, sg, su, sd, g2)


def _pad_lanes(v, width=LANE, offset=0):
    out = jnp.zeros((1, width), F32)
    return out.at[0, offset:offset + v.shape[0]].set(v.astype(F32))


def _head_cols(w, n_heads, dim):
    d = w.shape[0]
    w3 = w.reshape(d, n_heads, dim)
    return jnp.pad(w3, ((0, 0), (0, 0), (0, LANE - dim))).reshape(d, n_heads * LANE)


def _hybrid_w_in(w_in):
    d = w_in.shape[0]
    nq = NSA_HEADS * HEAD_DIM
    nkv = 6 * NSA_GROUPS * HEAD_DIM
    ng = 3 * NSA_HEADS
    nf = 3 * FOX_HEADS * HEAD_DIM
    c0, c1, c2, c3 = nq, nq + nkv, nq + nkv + ng, nq + nkv + ng + nf
    gates = w_in[:, c1:c2].reshape(d, NSA_GROUPS, 3 * NSA_HPG)
    gates = jnp.pad(gates, ((0, 0), (0, 0), (0, LANE - 3 * NSA_HPG))).reshape(d, NSA_GROUPS * LANE)
    ff = jnp.pad(w_in[:, c3:], ((0, 0), (0, 2 * LANE - FOX_HEADS)))
    return jnp.concatenate([
        _head_cols(w_in[:, :c0], NSA_HEADS, HEAD_DIM),
        _head_cols(w_in[:, c0:c1], 6 * NSA_GROUPS, HEAD_DIM),
        _head_cols(w_in[:, c2:c3], 3 * FOX_HEADS, HEAD_DIM),
        gates, ff], axis=1).astype(BF16)


def _pad_head_rows(w, n_heads, dim):
    d = w.shape[1]
    w3 = w.reshape(n_heads, dim, d)
    return jnp.pad(w3, ((0, 0), (0, LANE - dim), (0, 0))).reshape(n_heads * LANE, d).astype(BF16)


def _rope_inv(dim, offset):
    inv = ROPE_THETA ** (-jnp.arange(0, dim, 2, dtype=F32) / dim)
    return _pad_lanes(jnp.concatenate([inv, inv]), offset=offset)


def _hybrid_mixer(x2, posf, mods, norm_g, w_in, fox_f_bias, nsa_q_norm, nsa_k_norm, nsa_cmp_pe, nsa_w_cmp,
                  fox_q_norm, fox_k_norm, w_out):
    sh1, sc1, g1 = mods
    (qnt, kct, vct, ks, vst, kw, vwt, gates, fqt, fk, fvt, cedge) = _hy_prep(
        x2, norm_g, sc1, sh1, _hybrid_w_in(w_in), posf, _rope_inv(HEAD_DIM, 0), _pad_lanes(nsa_q_norm),
        _pad_lanes(nsa_k_norm),
        _pad_lanes(fox_q_norm), _pad_lanes(fox_k_norm), _pad_lanes(fox_f_bias))
    kc, vc_t = _compress(kct, vct, nsa_w_cmp, nsa_cmp_pe, nsa_k_norm)
    o_a = _nsa_attention(qnt, kc, vc_t, ks, vst, kw, vwt, gates)
    slack = (2.0 * HEAD_DIM ** 0.5 * LOG2E) * jnp.max(jnp.abs(fox_q_norm)) * jnp.max(jnp.abs(fox_k_norm))
    edges = (cedge[:, 0, :FOX_HEADS].T, cedge[:, 1, :FOX_HEADS].T)
    o_b = _causal_attention(fqt, fk, fvt, Q_TILE_FOX, edges, slack.reshape(1))
    half = NSA_HEADS * HEAD_DIM
    wa = _pad_head_rows(w_out[:half], NSA_HEADS, HEAD_DIM)
    wb = _pad_head_rows(w_out[half:], FOX_HEADS, HEAD_DIM)
    return _out_proj(o_a, o_b, 0, 0, wa, wb, x2, g1)


def _mla_mixer(x2, posf, mods, norm_g, w_in, q_a_norm, kv_a_norm, w_uq, w_ukv, qn_norm, kn_norm, qr_norm,
               kr_norm, w_out):
    sh1, sc1, g1 = mods
    d = x2.shape[1]
    w_kr = jnp.zeros((d, LANE), F32).at[:, QK_NOPE:QK_NOPE + QK_ROPE].set(w_in[:, Q_LORA + KV_LORA:])
    w_in_p = jnp.concatenate([w_in[:, :Q_LORA + KV_LORA], w_kr], axis=1).astype(BF16)
    hq = QK_NOPE + QK_ROPE
    wuq = _head_cols(w_uq, MLA_HEADS, hq).astype(BF16)
    wkv3 = w_ukv.reshape(KV_LORA, MLA_HEADS, QK_NOPE + V_HEAD)
    wuk = _head_cols(wkv3[:, :, :QK_NOPE].reshape(KV_LORA, -1), MLA_HEADS, QK_NOPE).astype(BF16)
    wuv = _head_cols(wkv3[:, :, QK_NOPE:].reshape(KV_LORA, -1), MLA_HEADS, V_HEAD).astype(BF16)
    gq = _pad_lanes(jnp.concatenate([qn_norm, qr_norm]))
    qt, k, vt = _mla_prep(x2, norm_g, sc1, sh1, w_in_p, posf, _rope_inv(QK_ROPE, QK_NOPE),
                          q_a_norm.reshape(1, -1).astype(F32),
                          kv_a_norm.reshape(1, -1).astype(F32), wuq, wuk, wuv, gq, _pad_lanes(kn_norm),
                          _pad_lanes(kr_norm, offset=QK_NOPE))
    o = _causal_attention(qt, k, vt, Q_TILE_MLA)
    w_pad = _pad_head_rows(w_out, MLA_HEADS, V_HEAD)
    half = w_pad.shape[0] // 2
    return _out_proj(o, o, 0, 1, w_pad[:half], w_pad[half:], x2, g1)


def _moe_ffn(x2, mods, norm_g, w_router, router_bias, w_gate, w_up, w_down, ws_gate, ws_up, ws_down):
    sh2, sc2, g2 = mods
    w_r = jnp.pad(w_router.astype(F32), ((0, 0), (0, LANE - N_EXPERTS)))
    h, pos_t, w_t, cnt = _moe_route(x2, norm_g, sc2, sh2, w_r, router_bias.reshape(N_EXPERTS, 1).astype(F32))
    counts = cnt[:, :, 0].astype(jnp.int32).reshape(-1)
    return _moe_experts(counts, x2, h, pos_t, w_t, w_gate.astype(BF16), w_up.astype(BF16),
                        w_down.astype(BF16), ws_gate.astype(BF16), ws_up.astype(BF16), ws_down.astype(BF16), g2)


def kernel(x, c, positions, norm_attn, norm_ffn, w_ada, b_ada, hy_w_in, fox_f_bias, nsa_q_norm, nsa_k_norm, nsa_cmp_pe, nsa_w_cmp, fox_q_norm, fox_k_norm, hy_w_out, mla_w_in, mla_q_a_norm, mla_kv_a_norm, mla_w_uq, mla_w_ukv, mla_qn_norm, mla_kn_norm, mla_qr_norm, mla_kr_norm, mla_w_out, moe_w_router, moe_router_bias, moe_w_gate, moe_w_up, moe_w_down, moe_ws_gate, moe_ws_up, moe_ws_down):
    b, s, d = x.shape
    assert b == 1 and s % KV_TILE == 0 and s >= WINDOW + Q_TILE_NSA
    depth = w_ada.shape[0]
    x2 = x.reshape(s, d).astype(F32)
    posf = positions.reshape(s, 1).astype(F32)
    mod = _ada_mod(c.astype(F32), w_ada.astype(F32), b_ada.astype(F32))

    for layer in range(depth):
        m = [mod[layer, :, k * d:(k + 1) * d] for k in range(6)]
        i = layer // 2
        g_attn = norm_attn[layer].reshape(1, d).astype(F32)
        if layer % 2 == 0:
            x2 = _hybrid_mixer(x2, posf, m[0:3], g_attn, hy_w_in[i], fox_f_bias[i], nsa_q_norm[i],
                               nsa_k_norm[i], nsa_cmp_pe[i], nsa_w_cmp[i], fox_q_norm[i], fox_k_norm[i],
                               hy_w_out[i])
        else:
            x2 = _mla_mixer(x2, posf, m[0:3], g_attn, mla_w_in[i], mla_q_a_norm[i], mla_kv_a_norm[i],
                            mla_w_uq[i], mla_w_ukv[i], mla_qn_norm[i], mla_kn_norm[i], mla_qr_norm[i],
                            mla_kr_norm[i], mla_w_out[i])
        x2 = _moe_ffn(x2, m[3:6], norm_ffn[layer].reshape(1, d).astype(F32), moe_w_router[layer],
                      moe_router_bias[layer], moe_w_gate[layer], moe_w_up[layer], moe_w_down[layer],
                      moe_ws_gate[layer], moe_ws_up[layer], moe_ws_down[layer])
    return x2.reshape(b, s, d)
```

```python
import functools

import numpy as np
import jax
import jax.numpy as jnp
from jax import lax
from jax.experimental import pallas as pl
from jax.experimental.pallas import tpu as pltpu

F32 = jnp.float32
BF16 = jnp.bfloat16
HIGHEST = lax.Precision.HIGHEST

LANE = 128
VMEM_LIMIT_BYTES = 56 * 1024 * 1024

HEAD_DIM = 64
NSA_HEADS = 8
NSA_GROUPS = 2
NSA_HPG = NSA_HEADS // NSA_GROUPS
CMP_LEN = 32
CMP_STRIDE = 16
SLC_BLOCK = 64
SLC_TOPK = 16
WINDOW = 512
FOX_HEADS = 8
MLA_HEADS = 16
Q_LORA = 384
KV_LORA = 256
QK_NOPE = 64
QK_ROPE = 32
V_HEAD = 64
N_EXPERTS = 64
TOP_K = 8
N_GROUPS = 8
TOPK_GROUPS = 4
EXPERT_FF = 256
ROUTED_SCALE = 2.5
ROPE_THETA = 10000.0
EPS = 1e-6
NEG = -1e30
BIG = 1e6

ROW_TILE = 512
PREP_TILE = 512
GATE_ROWS = 16
Q_TILE_NSA = 256
KV_TILE = 512
Q_TILE_FOX = 1024
Q_TILE_MLA = 2048
Q_STRIP = 256
WIN_TILE = 256
SUPER_BLOCKS = 32
NSA_PREFIX_PARTS = 8
SUM_ROW = 64
V_ROWS = 80
REF_ROW = 104
REF_SLAB = 96
EXP_GUARD = 100.0
SCORE_LOOKAHEAD = 4
SKIP_MARGIN = 160.0
LOG2E = 1.4426950408889634
MOE_CHUNK = 128
MOE_EXPERTS_PER_STEP = 8

HY_Q0 = 0
HY_KV0 = HY_Q0 + NSA_HEADS
HY_F0 = HY_KV0 + 6 * NSA_GROUPS
HY_G0 = HY_F0 + 3 * FOX_HEADS
HY_FF = HY_G0 + NSA_GROUPS
HY_BLOCKS = HY_FF + 2


def _cparams(*sem):
    return pltpu.CompilerParams(dimension_semantics=sem, vmem_limit_bytes=VMEM_LIMIT_BYTES)


def _lane_iota(shape):
    return lax.broadcasted_iota(jnp.int32, shape, len(shape) - 1)


def _row_iota(shape):
    return lax.broadcasted_iota(jnp.int32, shape, len(shape) - 2)


def _dot_tn(a, b):
    return lax.dot_general(a, b, (((0,), (0,)), ((), ())), preferred_element_type=F32)


def _ada_kernel(c_ref, w_ref, b_ref, o_ref):
    c = c_ref[...]
    cond = c * jax.nn.sigmoid(c)
    o_ref[0] = jnp.dot(cond, w_ref[0], precision=HIGHEST, preferred_element_type=F32) + b_ref[0]


def _ada_mod(c, w_ada, b_ada):
    depth, d, n = w_ada.shape
    tn = 768
    c8 = jnp.broadcast_to(c.reshape(1, d), (8, d))
    out = pl.pallas_call(
        _ada_kernel,
        grid=(depth, n // tn),
        in_specs=[pl.BlockSpec((8, d), lambda l, j: (0, 0)),
                  pl.BlockSpec((1, d, tn), lambda l, j: (l, 0, j)),
                  pl.BlockSpec((1, 1, tn), lambda l, j: (l, 0, j))],
        out_specs=pl.BlockSpec((1, 8, tn), lambda l, j: (l, 0, j)),
        out_shape=jax.ShapeDtypeStruct((depth, 8, n), F32),
        compiler_params=_cparams("parallel", "parallel"),
        name="ada_mod",
    )(c8, w_ada, b_ada.reshape(depth, 1, n))
    return out[:, 0:1, :]


def _norm_mod(x, g, sc, sh):
    ms = jnp.mean(x * x, axis=-1, keepdims=True)
    return (x * lax.rsqrt(ms + EPS) * g) * (1.0 + sc) + sh


def _head_rms(x, gain, n_real):
    ss = jnp.sum(x * x, axis=-1, keepdims=True)
    return x * lax.rsqrt(ss * (1.0 / n_real) + EPS) * gain


def _rope64(x, cos, sin):
    lane = _lane_iota(x.shape)
    rot = jnp.where(lane < 32, -pltpu.roll(x, LANE - 32, 1), pltpu.roll(x, 32, 1))
    return x * cos + rot * sin


def _t_bf16(x, rows=LANE):
    return x.T[:rows].astype(BF16)


def _split3(c):
    hi = c.astype(BF16).astype(F32)
    r1 = c - hi
    mid = r1.astype(BF16).astype(F32)
    lo = (r1 - mid).astype(BF16).astype(F32)
    return hi, mid, lo


def _hy_prep_kernel(x_ref, ng_ref, sc_ref, sh_ref, w_ref, pos_ref, inv_ref, gq_ref, gk_ref, gfq_ref, gfk_ref,
                    fb_ref, qnt_ref, kct_ref, vct_ref, ks_ref, vst_ref, kw_ref, vwt_ref, gate_ref,
                    fqt_ref, fk_ref, fvt_ref, cedge_ref, carry_ref):
    i = pl.program_id(0)
    tm = PREP_TILE
    shp = (tm, LANE)
    lane = _lane_iota(shp)

    act = _norm_mod(x_ref[...], ng_ref[...], sc_ref[...], sh_ref[...]).astype(BF16)
    pairs = {}

    def blk(b):
        if b // 2 not in pairs:
            cols = slice((b // 2) * 2 * LANE, (b // 2 + 1) * 2 * LANE)
            pairs[b // 2] = jnp.dot(act, w_ref[:, cols], preferred_element_type=F32)
        return pairs[b // 2][:, (b % 2) * LANE:(b % 2 + 1) * LANE]

    ang = pos_ref[...] * inv_ref[...]
    real = lane < HEAD_DIM
    cos = jnp.where(real, jnp.cos(ang), 1.0)
    sin = jnp.where(real, jnp.sin(ang), 0.0)
    gq, gk, gfq, gfk = gq_ref[...], gk_ref[...], gfq_ref[...], gfk_ref[...]
    scale = HEAD_DIM ** -0.5 * LOG2E
    ones_row = lane == SUM_ROW
    ref_ones = jnp.where((lane >= REF_ROW) & (lane < REF_ROW + 3), 1.0, 0.0)

    for h in range(NSA_HEADS):
        q = _rope64(_head_rms(blk(HY_Q0 + h), gq, HEAD_DIM), cos, sin) * scale
        qnt_ref[h * LANE:(h + 1) * LANE, :] = _t_bf16(q)

    row = _row_iota(shp) + i * tm
    onehot = jnp.where(lane - HEAD_DIM == ((row // SLC_BLOCK) % SUPER_BLOCKS), 1.0, 0.0)
    for g in range(NSA_GROUPS):
        def kv(r):
            return blk(HY_KV0 + r * NSA_GROUPS + g)
        sl = slice(g * LANE, (g + 1) * LANE)
        kct_ref[g] = _rope64(kv(0), cos, sin)[:, :HEAD_DIM].astype(BF16)
        vct_ref[g] = kv(1)[:, :HEAD_DIM].astype(BF16)
        ks = _rope64(_head_rms(kv(2), gk, HEAD_DIM), cos, sin)
        ks_ref[:, sl] = (ks + onehot + ref_ones).astype(BF16)
        vst_ref[g, 0] = _t_bf16(jnp.where(ones_row, 1.0, kv(3)), V_ROWS)
        kw_ref[:, sl] = (_rope64(_head_rms(kv(4), gk, HEAD_DIM), cos, sin) + ref_ones).astype(BF16)
        vwt = _t_bf16(jnp.where(ones_row, 1.0, kv(5)), V_ROWS)
        for cidx in range(tm // WIN_TILE):
            vwt_ref[g, cidx] = vwt[:, cidx * WIN_TILE:(cidx + 1) * WIN_TILE]
        gate_ref[g] = jax.nn.sigmoid(blk(HY_G0 + g)).T[:GATE_ROWS]

    @pl.when(i == 0)
    def _():
        carry_ref[...] = jnp.zeros_like(carry_ref)

    z = blk(HY_FF) + fb_ref[...]
    logf = jnp.minimum(z, 0.0) - jnp.log1p(jnp.exp(-jnp.abs(z)))
    tri = jnp.where(_row_iota((tm, tm)) >= _lane_iota((tm, tm)), 1.0, 0.0).astype(F32)
    cum = jnp.dot(tri, logf, precision=HIGHEST, preferred_element_type=F32) + carry_ref[...]
    carry_ref[...] = cum[tm - 1:tm, :]
    cedge_ref[0] = jnp.concatenate([cum[0:1] * LOG2E, cum[tm - 1:tm] * LOG2E, jnp.zeros((6, LANE), F32)], axis=0)

    for h in range(FOX_HEADS):
        c = jnp.broadcast_to(cum[:, h:h + 1], shp) * LOG2E
        hi, mid, lo = _split3(c)
        fq = _head_rms(blk(HY_F0 + h), gfq, HEAD_DIM) * scale
        fq = jnp.where(real, fq, jnp.where(lane == 64, hi, jnp.where(lane == 65, mid, jnp.where(
            lane == 66, lo, jnp.where(lane < 70, 1.0, 0.0)))))
        fk = _head_rms(blk(HY_F0 + FOX_HEADS + h), gfk, HEAD_DIM)
        fk = jnp.where(real, fk, jnp.where(lane < 67, 1.0, jnp.where(lane == 67, -hi, jnp.where(
            lane == 68, -mid, jnp.where(lane == 69, -lo, ref_ones)))))
        sl = slice(h * LANE, (h + 1) * LANE)
        fqt_ref[sl, :] = _t_bf16(fq)
        fk_ref[:, sl] = fk.astype(BF16)
        fvt_ref[h, 0] = _t_bf16(jnp.where(ones_row, 1.0, blk(HY_F0 + 2 * FOX_HEADS + h)), V_ROWS)


def _feat_major(heads, s, tm):
    return (pl.BlockSpec((heads * LANE, tm), lambda i: (0, i)),
            jax.ShapeDtypeStruct((heads * LANE, s), BF16))


def _value_tiles(heads, s, tm, tk, rows=LANE):
    return (pl.BlockSpec((heads, tm // tk, rows, tk), lambda i: (0, i, 0, 0)),
            jax.ShapeDtypeStruct((heads, s // tk, rows, tk), BF16))


def _pad_feature_rows(o):
    return jnp.concatenate([o, jnp.zeros((LANE - HEAD_DIM, o.shape[1]), o.dtype)], axis=0)


def _hy_prep(x2, norm_g, sc, sh, w_in, posf, inv128, gq, gk, gfq, gfk, fbias):
    s, d = x2.shape
    tm = PREP_TILE
    assert tm == KV_TILE and w_in.shape == (d, HY_BLOCKS * LANE) and HY_BLOCKS % 2 == 0
    vec = pl.BlockSpec((1, LANE), lambda i: (0, 0))
    dvec = pl.BlockSpec((1, d), lambda i: (0, 0))

    def rows(nb):
        return (pl.BlockSpec((tm, nb * LANE), lambda i: (i, 0)), jax.ShapeDtypeStruct((s, nb * LANE), BF16))

    tok = (pl.BlockSpec((NSA_GROUPS, tm, HEAD_DIM), lambda i: (0, i, 0)),
           jax.ShapeDtypeStruct((NSA_GROUPS, s, HEAD_DIM), BF16))
    gate = (pl.BlockSpec((NSA_GROUPS, GATE_ROWS, tm), lambda i: (0, 0, i)),
            jax.ShapeDtypeStruct((NSA_GROUPS, GATE_ROWS, s), F32))
    outs = [_feat_major(NSA_HEADS, s, tm), tok, tok, rows(NSA_GROUPS),
            _value_tiles(NSA_GROUPS, s, tm, KV_TILE, V_ROWS),
            rows(NSA_GROUPS), _value_tiles(NSA_GROUPS, s, tm, WIN_TILE, V_ROWS), gate,
            _feat_major(FOX_HEADS, s, tm), rows(FOX_HEADS), _value_tiles(FOX_HEADS, s, tm, KV_TILE, V_ROWS),
            (pl.BlockSpec((1, 8, LANE), lambda i: (i, 0, 0)), jax.ShapeDtypeStruct((s // tm, 8, LANE), F32))]
    return pl.pallas_call(
        _hy_prep_kernel,
        grid=(s // tm,),
        in_specs=[pl.BlockSpec((tm, d), lambda i: (i, 0)), dvec, dvec, dvec,
                  pl.BlockSpec((d, HY_BLOCKS * LANE), lambda i: (0, 0), pipeline_mode=pl.Buffered(1)),
                  pl.BlockSpec((tm, 1), lambda i: (i, 0)), vec, vec, vec, vec, vec, vec],
        out_specs=[o[0] for o in outs],
        out_shape=[o[1] for o in outs],
        scratch_shapes=[pltpu.VMEM((1, LANE), F32)],
        compiler_params=_cparams("arbitrary"),
        name="hybrid_prep",
    )(x2, norm_g, sc, sh, w_in, posf, inv128, gq, gk, gfq, gfk, fbias)


def _compress_kernel(kc_ref, vc_ref, wk_ref, wv_ref, pek_ref, pev_ref, gk_ref, ko_ref, vo_ref):
    half = CMP_STRIDE * HEAD_DIM

    def comp(ch_ref, w_ref, pe_ref):
        ch = ch_ref[0]
        nc = ch.shape[0]
        a = jnp.dot(ch, w_ref[:half], preferred_element_type=F32)
        b = jnp.dot(ch, w_ref[half:], preferred_element_type=F32)
        nxt = pltpu.roll(b, nc - 1, 0)
        pe = jnp.dot(jnp.broadcast_to(pe_ref[...], (8, 2 * half)).astype(BF16), w_ref[...],
                     preferred_element_type=F32)[0:1]
        return a + nxt + pe

    ko_ref[0] = _head_rms(comp(kc_ref, wk_ref, pek_ref), gk_ref[...], HEAD_DIM).astype(BF16)
    vo_ref[0] = comp(vc_ref, wv_ref, pev_ref).T.astype(BF16)


def _compress(kct, vct, w_cmp, cmp_pe, k_norm):
    g, s, _ = kct.shape
    nc = s // CMP_STRIDE
    wide = CMP_STRIDE * HEAD_DIM
    kch = kct.reshape(g, nc, wide)
    vch = vct.reshape(g, nc, wide)
    w_pad = jnp.pad(w_cmp, ((0, 0), (0, 0), (0, LANE - HEAD_DIM))).astype(BF16)
    ch = pl.BlockSpec((1, nc, wide), lambda i: (i, 0, 0))
    wspec = pl.BlockSpec((2 * wide, LANE), lambda i: (0, 0))
    pespec = pl.BlockSpec((1, 2 * wide), lambda i: (0, 0))
    return pl.pallas_call(
        _compress_kernel,
        grid=(g,),
        in_specs=[ch, ch, wspec, wspec, pespec, pespec, pl.BlockSpec((1, LANE), lambda i: (0, 0))],
        out_specs=[pl.BlockSpec((1, nc, LANE), lambda i: (i, 0, 0)),
                   pl.BlockSpec((1, LANE, nc), lambda i: (i, 0, 0))],
        out_shape=[jax.ShapeDtypeStruct((g, nc, LANE), BF16), jax.ShapeDtypeStruct((g, LANE, nc), BF16)],
        compiler_params=_cparams("parallel"),
        name="nsa_compress",
    )(kch, vch, w_pad[0], w_pad[1], cmp_pe[0].reshape(1, 2 * wide).astype(F32),
      cmp_pe[1].reshape(1, 2 * wide).astype(F32), _pad_lanes(k_norm))


def _masked_softmax_t(s, mask):
    s = jnp.where(mask, s, NEG)
    m = jnp.max(s, axis=0, keepdims=True)
    e = jnp.exp2(s - m)
    inv = 1.0 / jnp.maximum(jnp.sum(e, axis=0, keepdims=True), 1e-30)
    return e, jnp.where(m > 0.5 * NEG, inv, 0.0)


def _online_steps(steps, ms, acc_ref):
    ms = list(ms)

    def scores(step):
        k_tile, qa, _, c, mask = step
        s = jnp.dot(k_tile, qa, preferred_element_type=F32)
        if mask is not None:
            s = jnp.where(mask, s, NEG)
        return s, jnp.max(s, axis=0, keepdims=True)

    nxt = scores(steps[0])
    for idx, (_, _, vt, c, _) in enumerate(steps):
        sl = slice(c * Q_STRIP, (c + 1) * Q_STRIP)
        s, s_max = nxt
        if idx + 1 < len(steps):
            nxt = scores(steps[idx + 1])
        m_new = jnp.maximum(ms[c], s_max)
        a = jnp.exp2(ms[c] - m_new)
        p = jnp.exp2((s - m_new).astype(BF16))
        ms[c] = m_new
        acc_ref[:, sl] = a * acc_ref[:, sl] + jnp.dot(vt, p, preferred_element_type=F32)
    return tuple(ms)


def _m_init(n_strips):
    return tuple(jnp.full((1, Q_STRIP), NEG, F32) for _ in range(n_strips))


def _with_ref_rows(qa, m):
    hi, mid, lo = _split3(-m)
    r = _row_iota((LANE - REF_SLAB, Q_STRIP)) + REF_SLAB
    slab = jnp.where(r == REF_ROW, hi, jnp.where(r == REF_ROW + 1, mid, jnp.where(r == REF_ROW + 2, lo, 0.0)))
    return jnp.concatenate([qa[:REF_SLAB], slab.astype(BF16)], axis=0)


def _first_tile_max(k_tile, qa_strips, masks):
    return tuple(jnp.max(jnp.where(mask, jnp.dot(k_tile, qa, preferred_element_type=F32), NEG), axis=0, keepdims=True)
                 for qa, mask in zip(qa_strips, masks))


def _fast_steps(steps, state, acc_ref):
    state = list(state)
    depth, last = min(SCORE_LOOKAHEAD, len(steps)), {}
    for k, step in enumerate(steps):
        depth = min(depth, k - last.get(step[3], k - depth))
        last[step[3]] = k

    def scores(step):
        k_tile, qa, _, c, mask = step
        s = jnp.dot(k_tile, _with_ref_rows(qa, state[c][0]), preferred_element_type=F32)
        if mask is not None:
            s = jnp.where(mask, s, NEG)
        return s

    ahead = [scores(st) for st in steps[:depth]]
    for idx, (_, _, vt, c, _) in enumerate(steps):
        sl = slice(c * Q_STRIP, (c + 1) * Q_STRIP)
        s = ahead.pop(0)
        m, worst = state[c]
        cm = jnp.max(s, axis=0, keepdims=True)
        inc = jnp.maximum(cm, 0.0)
        state[c] = (m + inc, jnp.maximum(worst, cm))
        if idx + depth < len(steps):
            ahead.append(scores(steps[idx + depth]))
        p = jnp.exp2(s).astype(BF16)
        acc_ref[:, sl] = jnp.exp2(-inc) * (acc_ref[:, sl] + jnp.dot(vt, p, preferred_element_type=F32))
    return tuple(state)


def _flat(state):
    return tuple(x for pair in state for x in pair)


def _nest(flat):
    return tuple((flat[2 * c], flat[2 * c + 1]) for c in range(len(flat) // 2))


def _nsa_kernel(qt_ref, kc_ref, vct_ref, ks_ref, vst_ref, kw_ref, vwt_ref, gate_ref, ovt_ref, o_ref,
                qaug_ref, acc_ref, wacc_ref, *, n_sel):
    i = pl.program_id(1)
    tq = Q_TILE_NSA
    cols = NSA_HPG * tq
    qs = i * tq
    nc = kc_ref.shape[1]
    nslc = ovt_ref.shape[0]
    n_super = nslc // SUPER_BLOCKS

    qt = jnp.concatenate([qt_ref[h * LANE:(h + 1) * LANE, :] for h in range(NSA_HPG)], axis=1)
    tq_row = qs + (_lane_iota((1, cols)) % tq)

    def compress_and_select(n_c, n_b):
        s = jnp.dot(kc_ref[0, :n_c], qt, preferred_element_type=F32)
        cmp_end = _row_iota((n_c, 1)) * CMP_STRIDE + (CMP_LEN - 1)
        e, inv_l = _masked_softmax_t(s, cmp_end <= tq_row)
        o_cmp = jnp.dot(vct_ref[0, :, :n_c], e.astype(BF16), preferred_element_type=F32) * inv_l

        psum = e[:, 0:tq] * inv_l[:, 0:tq]
        for h in range(1, NSA_HPG):
            psum = psum + e[:, h * tq:(h + 1) * tq] * inv_l[:, h * tq:(h + 1) * tq]
        p_hi = psum.astype(BF16)
        p_lo = (psum - p_hi.astype(F32)).astype(BF16)
        ovt = ovt_ref[:n_b, :n_c]
        imp = (jnp.dot(ovt, p_hi, preferred_element_type=F32)
               + jnp.dot(ovt, p_lo, preferred_element_type=F32))

        jj = _row_iota((n_b, tq))
        tq_blk = qs + _lane_iota((n_b, tq))
        cur = tq_blk // SLC_BLOCK
        forced = (jj == 0) | (jj == cur) | (jj == cur - 1)
        causal_blk = jj * SLC_BLOCK <= tq_blk
        val = jnp.where(forced, imp + BIG, imp)
        val = jnp.where(causal_blk, val, NEG)
        jjf = jj.astype(F32)

        def pick(_, carry):
            val, sel = carry
            mx = jnp.max(val, axis=0, keepdims=True)
            idx = jnp.min(jnp.where(val == mx, jjf, float(n_b)), axis=0, keepdims=True)
            hit = jjf == idx
            return jnp.where(hit, -jnp.inf, val), jnp.where(hit, 1.0, sel)

        _, sel = lax.fori_loop(0, min(n_sel, n_b), pick, (val, jnp.zeros((n_b, tq), F32)))
        bias = jnp.where((sel > 0.0) & causal_blk, 0.0, NEG)
        if n_b < nslc:
            bias = jnp.concatenate([bias, jnp.full((nslc - n_b, tq), NEG, F32)], axis=0)
        return o_cmp, bias

    parts = max(1, min(NSA_PREFIX_PARTS, nc // LANE))
    seq = nc * CMP_STRIDE

    def dispatch(k):
        full = lambda: compress_and_select(nc * k // parts, nslc * k // parts)
        if k == parts:
            return full()
        return lax.cond(qs + tq <= seq * k // parts, full, lambda: dispatch(k + 1))

    o_cmp, bias_t = dispatch(1)

    q_rows = qt[:HEAD_DIM].astype(F32)
    spare = jnp.zeros((LANE - HEAD_DIM - SUPER_BLOCKS, cols), F32)
    for st in range(n_super):
        b = bias_t[st * SUPER_BLOCKS:(st + 1) * SUPER_BLOCKS]
        b = jnp.concatenate([b] * NSA_HPG, axis=1)
        qaug_ref[st] = jnp.concatenate([q_rows, b, spare], axis=0).astype(BF16)

    tk = KV_TILE
    per_super = SUPER_BLOCKS * SLC_BLOCK // tk
    j_last = (qs + tq - 1) // tk
    n_strips = cols // Q_STRIP
    strips = [slice(c * Q_STRIP, (c + 1) * Q_STRIP) for c in range(n_strips)]

    def causal_masks(j):
        kpos = j * tk + _row_iota((tk, 1))
        return [kpos <= tq_row[:, sl] for sl in strips]

    def slc_steps(j, masks=None):
        k0 = pl.multiple_of(j * tk, tk)
        k_tile, vt, st = ks_ref[pl.ds(k0, tk), :], vst_ref[0, j], j // per_super
        return [(k_tile, qaug_ref[st, :, strips[c]], vt, c, None if masks is None else masks[c])
                for c in range(n_strips)]

    group = 4

    def grouped(jj, flat):
        steps = [st for t in range(group) for st in slc_steps(group * jj + t)]
        return _flat(_fast_steps(steps, _nest(flat), acc_ref))

    def single(j, flat):
        return _flat(_fast_steps(slc_steps(j), _nest(flat), acc_ref))

    acc_ref[...] = jnp.zeros(acc_ref.shape, F32)
    head_rows = 16
    head_masks = [_row_iota((head_rows, 1)) <= tq_row[:, sl] for sl in strips]
    m0 = _first_tile_max(ks_ref[0:head_rows, :], [qaug_ref[0, :, sl] for sl in strips], head_masks)
    n_groups = j_last // group
    flat = lax.fori_loop(0, n_groups, grouped, _flat(tuple((m, jnp.zeros_like(m)) for m in m0)))
    flat = lax.fori_loop(group * n_groups, j_last, single, flat)
    state = _fast_steps(slc_steps(j_last, causal_masks(j_last)), _nest(flat), acc_ref)
    worst = jnp.max(jnp.concatenate([w for _, w in state], axis=1))

    @pl.when(worst > EXP_GUARD)
    def _():
        acc_ref[...] = jnp.zeros(acc_ref.shape, F32)
        lax.fori_loop(0, j_last + 1, lambda j, ms: _online_steps(slc_steps(j, causal_masks(j)), ms, acc_ref),
                      _m_init(n_strips))

    o_slc = acc_ref[:HEAD_DIM] / jnp.maximum(acc_ref[SUM_ROW:SUM_ROW + 1], 1e-30)

    wt = WIN_TILE
    n_wt = (WINDOW + tq) // wt
    ws = pl.multiple_of(jnp.maximum(qs - WINDOW, 0), wt)

    def win_masks(k0, rows):
        dist = tq_row - (k0 + _row_iota((rows, 1)))
        in_win = (dist >= 0) & (dist < WINDOW)
        return [in_win[:, sl] for sl in strips]

    def win_steps(t):
        k0 = pl.multiple_of(ws + t * wt, wt)
        k_tile, vt, masks = kw_ref[pl.ds(k0, wt), :], vwt_ref[0, ws // wt + t], win_masks(k0, wt)
        return [(k_tile, qt[:, strips[c]], vt, c, masks[c]) for c in range(n_strips)]

    k_start = pl.multiple_of(jnp.maximum(qs - tq, 0), tq)
    m0 = _first_tile_max(kw_ref[pl.ds(k_start, head_rows), :], [qt[:, sl] for sl in strips],
                         win_masks(k_start, head_rows))
    wacc_ref[...] = jnp.zeros(wacc_ref.shape, F32)
    state = _fast_steps([st for t in range(n_wt) for st in win_steps(t)],
                        tuple((m, jnp.zeros_like(m)) for m in m0), wacc_ref)
    worst = jnp.max(jnp.concatenate([w for _, w in state], axis=1))

    @pl.when(worst > EXP_GUARD)
    def _():
        wlen = n_wt * wt
        s = jnp.dot(kw_ref[pl.ds(ws, wlen), :], qt, preferred_element_type=F32)
        dist = tq_row - (ws + _row_iota((wlen, 1)))
        e, _ = _masked_softmax_t(s, (dist >= 0) & (dist < WINDOW))
        e = e.astype(BF16)
        acc = jnp.zeros(wacc_ref.shape, F32)
        for t in range(n_wt):
            acc = acc + jnp.dot(vwt_ref[0, ws // wt + t], e[t * wt:(t + 1) * wt], preferred_element_type=F32)
        wacc_ref[...] = acc

    o_win = wacc_ref[:HEAD_DIM] / jnp.maximum(wacc_ref[SUM_ROW:SUM_ROW + 1], 1e-30)

    gate = gate_ref[0]
    for h in range(NSA_HPG):
        sl = slice(h * tq, (h + 1) * tq)
        o = (gate[3 * h:3 * h + 1] * o_cmp[:HEAD_DIM, sl] + gate[3 * h + 1:3 * h + 2] * o_slc[:, sl]
             + gate[3 * h + 2:3 * h + 3] * o_win[:, sl])
        o_ref[:, h * LANE:(h + 1) * LANE] = _pad_feature_rows(o).T.astype(BF16)


def _overlap_t(s, nslc_pad):
    nc = s // CMP_STRIDE
    cmp_start = np.arange(nc) * CMP_STRIDE
    slc_start = np.arange(nslc_pad) * SLC_BLOCK
    ov = np.clip(np.minimum(cmp_start[:, None] + CMP_LEN, slc_start[None, :] + SLC_BLOCK)
                 - np.maximum(cmp_start[:, None], slc_start[None, :]), 0, None) / CMP_STRIDE
    ov[nc - CMP_LEN // CMP_STRIDE + 1:, :] = 0.0
    ov[:, s // SLC_BLOCK:] = 0.0
    return jnp.asarray(ov.T, BF16)


def _nsa_attention(qnt, kc, vct, ks, vst, kw, vwt, gates):
    s = qnt.shape[1]
    nc = s // CMP_STRIDE
    n_slc = s // SLC_BLOCK
    nslc_pad = -(-n_slc // LANE) * LANE
    tq = Q_TILE_NSA
    cols = NSA_HPG * tq
    once = pl.Buffered(1)
    res = pl.BlockSpec((s, LANE), lambda g, i: (0, g), pipeline_mode=once)
    return pl.pallas_call(
        functools.partial(_nsa_kernel, n_sel=min(SLC_TOPK, n_slc)),
        grid=(NSA_GROUPS, s // tq),
        in_specs=[pl.BlockSpec((NSA_HPG * LANE, tq), lambda g, i: (g, i)),
                  pl.BlockSpec((1, nc, LANE), lambda g, i: (g, 0, 0), pipeline_mode=once),
                  pl.BlockSpec((1, LANE, nc), lambda g, i: (g, 0, 0), pipeline_mode=once),
                  res, pl.BlockSpec((1, s // KV_TILE, V_ROWS, KV_TILE), lambda g, i: (g, 0, 0, 0),
                                    pipeline_mode=once),
                  res, pl.BlockSpec((1, s // WIN_TILE, V_ROWS, WIN_TILE), lambda g, i: (g, 0, 0, 0),
                                    pipeline_mode=once),
                  pl.BlockSpec((1, GATE_ROWS, tq), lambda g, i: (g, 0, i)),
                  pl.BlockSpec((nslc_pad, nc), lambda g, i: (0, 0), pipeline_mode=once)],
        out_specs=pl.BlockSpec((tq, NSA_HPG * LANE), lambda g, i: (i, g)),
        out_shape=jax.ShapeDtypeStruct((s, NSA_HEADS * LANE), BF16),
        scratch_shapes=[pltpu.VMEM((nslc_pad // SUPER_BLOCKS, LANE, cols), BF16),
                        pltpu.VMEM((V_ROWS, cols), F32), pltpu.VMEM((V_ROWS, cols), F32)],
        compiler_params=_cparams("parallel", "arbitrary"),
        name="nsa_attention",
    )(qnt, kc, vct, ks, vst, kw, vwt, gates, _overlap_t(s, nslc_pad))


def _flash_kernel(cfirst_ref, clast_ref, slack_ref, qt_ref, k_ref, vt_ref, o_ref, acc_ref, *, tq, decay):
    h = pl.program_id(0)
    i = pl.program_id(1)
    tk = KV_TILE
    n_tiles = k_ref.shape[0] // tk
    acc_ref[...] = jnp.zeros(acc_ref.shape, F32)

    n_strips = tq // Q_STRIP
    per_q = tq // tk
    qas = [qt_ref[:, c * Q_STRIP:(c + 1) * Q_STRIP] for c in range(n_strips)]

    def tile_steps(j, d=None):
        k0 = pl.multiple_of(j * tk, tk)
        k_tile, vt = k_ref[pl.ds(k0, tk), :], vt_ref[0, j]
        steps = []
        for c in range(n_strips):
            mask = None
            if d is not None:
                if d * tk > (c + 1) * Q_STRIP - 1:
                    continue
                if (d + 1) * tk - 1 > c * Q_STRIP:
                    shp = (tk, Q_STRIP)
                    mask = _row_iota(shp) + d * tk <= _lane_iota(shp) + c * Q_STRIP
            steps.append((k_tile, qas[c], vt, c, mask))
        return steps

    def any_tile_masks(j):
        shp = (tk, Q_STRIP)
        return [_row_iota(shp) + j * tk <= _lane_iota(shp) + (i * tq + c * Q_STRIP) for c in range(n_strips)]

    def below(t, flat):
        jj = i - 1 - t
        bound = (slack_ref[0] + cfirst_ref[h * n_tiles + i * per_q]
                 - clast_ref[h * n_tiles + jj * per_q + per_q - 1])

        def run(flat):
            steps = [st for u in range(per_q) for st in tile_steps(jj * per_q + (per_q - 1 - u))]
            return _flat(_fast_steps(steps, _nest(flat), acc_ref))

        return lax.cond(bound >= -SKIP_MARGIN, run, lambda flat: flat, flat)

    diag0 = i * per_q
    own = [(c * Q_STRIP) // tk for c in range(n_strips)]
    head = tk if decay else 16

    def start_max(c):
        j = diag0 + own[c]
        keys = k_ref[pl.ds(pl.multiple_of(j * tk, tk), head), :]
        shp = (head, Q_STRIP)
        mask = _row_iota(shp) + j * tk <= _lane_iota(shp) + (i * tq + c * Q_STRIP)
        return _first_tile_max(keys, [qas[c]], [mask])[0]

    m0 = tuple(start_max(c) for c in range(n_strips))
    steps = [st for d in reversed(range(per_q)) for st in tile_steps(diag0 + d, d)]
    state = _fast_steps(steps, tuple((m, jnp.zeros_like(m)) for m in m0), acc_ref)
    state = _nest(lax.fori_loop(0, i, below, _flat(state)))
    worst = jnp.max(jnp.concatenate([w for _, w in state], axis=1))

    @pl.when(worst > EXP_GUARD)
    def _():
        acc_ref[...] = jnp.zeros(acc_ref.shape, F32)

        def exact(j, ms):
            k0 = pl.multiple_of(j * tk, tk)
            k_tile, vt, masks = k_ref[pl.ds(k0, tk), :], vt_ref[0, j], any_tile_masks(j)
            return _online_steps([(k_tile, qas[c], vt, c, masks[c]) for c in range(n_strips)], ms, acc_ref)

        lax.fori_loop(0, (i + 1) * per_q, exact, _m_init(n_strips))

    o = _pad_feature_rows(acc_ref[:HEAD_DIM] / acc_ref[SUM_ROW:SUM_ROW + 1])
    for c0 in range(0, tq, LANE):
        o_ref[c0:c0 + LANE, :] = o[:, c0:c0 + LANE].T.astype(BF16)


def _causal_attention(qt, k, vt, tq, bias_edges=None, slack=None):
    s, width = k.shape
    heads = width // LANE
    tq = min(tq, s)
    assert tq % KV_TILE == 0 and s % tq == 0
    n_tiles = s // KV_TILE
    if bias_edges is None:
        first = last = jnp.zeros((heads * n_tiles,), F32)
        slack = jnp.full((1,), -NEG, F32)
    else:
        first, last = (e.reshape(heads * n_tiles).astype(F32) for e in bias_edges)
    grid_spec = pltpu.PrefetchScalarGridSpec(
        num_scalar_prefetch=3,
        grid=(heads, s // tq),
        in_specs=[pl.BlockSpec((LANE, tq), lambda h, i, *_: (h, i)),
                  pl.BlockSpec((s, LANE), lambda h, i, *_: (0, h)),
                  pl.BlockSpec((1, n_tiles, V_ROWS, KV_TILE), lambda h, i, *_: (h, 0, 0, 0))],
        out_specs=pl.BlockSpec((tq, LANE), lambda h, i, *_: (i, h)),
        scratch_shapes=[pltpu.VMEM((V_ROWS, tq), F32)],
    )
    return pl.pallas_call(
        functools.partial(_flash_kernel, tq=tq, decay=bias_edges is not None),
        grid_spec=grid_spec,
        out_shape=jax.ShapeDtypeStruct((s, width), BF16),
        compiler_params=_cparams("parallel", "arbitrary"),
        name="causal_attention",
    )(first, last, slack.astype(F32), qt, k, vt)


def _out_proj_kernel(oa_ref, ob_ref, wa_ref, wb_ref, x_ref, g_ref, o_ref):
    y = jnp.dot(oa_ref[...], wa_ref[...], preferred_element_type=F32)
    y = y + jnp.dot(ob_ref[...], wb_ref[...], preferred_element_type=F32)
    o_ref[...] = x_ref[...] + g_ref[...] * y


def _out_proj(oa, ob, cola, colb, wa, wb, x2, gate):
    s, d = x2.shape
    ka = wa.shape[0]
    tm = ROW_TILE
    return pl.pallas_call(
        _out_proj_kernel,
        grid=(s // tm,),
        in_specs=[pl.BlockSpec((tm, ka), lambda i: (i, cola)), pl.BlockSpec((tm, ka), lambda i: (i, colb)),
                  pl.BlockSpec((ka, d), lambda i: (0, 0)), pl.BlockSpec((ka, d), lambda i: (0, 0)),
                  pl.BlockSpec((tm, d), lambda i: (i, 0)), pl.BlockSpec((1, d), lambda i: (0, 0))],
        out_specs=pl.BlockSpec((tm, d), lambda i: (i, 0)),
        out_shape=jax.ShapeDtypeStruct((s, d), F32),
        compiler_params=_cparams("parallel"),
        name="out_proj",
    )(oa, ob, wa, wb, x2, gate)


def _mla_prep_kernel(x_ref, ng_ref, sc_ref, sh_ref, win_ref, pos_ref, inv_ref, gqa_ref, gkva_ref, wuq_ref, wuk_ref,
                     wuv_ref, gq_ref, gk_ref, gkr_ref, qt_ref, k_ref, vt_ref):
    shp = (PREP_TILE, LANE)
    act = _norm_mod(x_ref[...], ng_ref[...], sc_ref[...], sh_ref[...]).astype(BF16)
    proj = jnp.dot(act, win_ref[...], preferred_element_type=F32)
    lane = _lane_iota(shp)
    nope = lane < QK_NOPE
    rope = (lane >= QK_NOPE) & (lane < QK_NOPE + QK_ROPE)
    ref_ones = jnp.where((lane >= REF_ROW) & (lane < REF_ROW + 3), 1.0, 0.0)
    ang = pos_ref[...] * inv_ref[...]
    cos = jnp.where(rope, jnp.cos(ang), 1.0)
    sin = jnp.where(rope, jnp.sin(ang), 0.0)

    def rope32(x):
        half = QK_ROPE // 2
        rot = jnp.where(lane < QK_NOPE + half, -pltpu.roll(x, LANE - half, 1), pltpu.roll(x, half, 1))
        return x * cos + rot * sin

    def low_rank_norm(x, g):
        ms = jnp.mean(x * x, axis=-1, keepdims=True)
        return (x * lax.rsqrt(ms + EPS) * g).astype(BF16)

    nq = Q_LORA // LANE
    cq = low_rank_norm(proj[:, :Q_LORA], gqa_ref[...])
    ckv = low_rank_norm(proj[:, Q_LORA:Q_LORA + KV_LORA], gkva_ref[...])
    kr = proj[:, (nq + KV_LORA // LANE) * LANE:(nq + KV_LORA // LANE + 1) * LANE]
    k_rope = rope32(_head_rms(kr, gkr_ref[...], QK_ROPE))

    gq, gk = gq_ref[...], gk_ref[...]
    scale = (QK_NOPE + QK_ROPE) ** -0.5 * LOG2E
    pair = 2 * LANE
    for hp in range(MLA_HEADS // 2):
        cols = slice(hp * pair, (hp + 1) * pair)
        q2 = jnp.dot(cq, wuq_ref[:, cols], preferred_element_type=F32)
        k2 = jnp.dot(ckv, wuk_ref[:, cols], preferred_element_type=F32)
        v2 = jnp.dot(ckv, wuv_ref[:, cols], preferred_element_type=F32)
        for sub in range(2):
            head = 2 * hp + sub
            sl = slice(head * LANE, (head + 1) * LANE)
            half = slice(sub * LANE, (sub + 1) * LANE)
            x = q2[:, half]
            ss_n = jnp.sum(jnp.where(nope, x * x, 0.0), axis=-1, keepdims=True)
            ss_r = jnp.sum(jnp.where(rope, x * x, 0.0), axis=-1, keepdims=True)
            inv_rms = jnp.where(nope, lax.rsqrt(ss_n * (1.0 / QK_NOPE) + EPS),
                                lax.rsqrt(ss_r * (1.0 / QK_ROPE) + EPS))
            qt_ref[sl, :] = _t_bf16(rope32(x * inv_rms * gq) * scale)
            kn = _head_rms(k2[:, half], gk, QK_NOPE)
            k_ref[:, sl] = (kn + k_rope + ref_ones).astype(BF16)
            vt_ref[head, 0] = _t_bf16(jnp.where(lane == SUM_ROW, 1.0, v2[:, half]), V_ROWS)


def _mla_prep(x2, norm_g, sc, sh, w_in, posf, inv128, gqa, gkva, wuq, wuk, wuv, gq, gk, gkr):
    s, d = x2.shape
    tm = PREP_TILE
    assert tm == KV_TILE

    def full(a):
        return pl.BlockSpec(a.shape, lambda i: (0, 0))

    outs = [_feat_major(MLA_HEADS, s, tm),
            (pl.BlockSpec((tm, MLA_HEADS * LANE), lambda i: (i, 0)),
             jax.ShapeDtypeStruct((s, MLA_HEADS * LANE), BF16)),
            _value_tiles(MLA_HEADS, s, tm, KV_TILE, V_ROWS)]
    head = (norm_g, sc, sh, w_in)
    tail = (inv128, gqa, gkva, wuq, wuk, wuv, gq, gk, gkr)
    return pl.pallas_call(
        _mla_prep_kernel,
        grid=(s // tm,),
        in_specs=[pl.BlockSpec((tm, d), lambda i: (i, 0))] + [full(a) for a in head]
                 + [pl.BlockSpec((tm, 1), lambda i: (i, 0))] + [full(a) for a in tail],
        out_specs=[o[0] for o in outs],
        out_shape=[o[1] for o in outs],
        compiler_params=_cparams("parallel"),
        name="mla_prep",
    )(x2, *head, posf, *tail)


def _rank_lt(v, k):
    n = v.shape[0]
    row = _row_iota(v.shape)
    rank = jnp.zeros(v.shape, F32)
    for b in range(n):
        vb = v[b:b + 1, :]
        rank = rank + jnp.where((vb > v) | ((vb == v) & (row > b)), 1.0, 0.0)
    return rank < k


def _top_rows(v, k):
    rowf = _row_iota(v.shape).astype(F32)
    chosen = jnp.zeros(v.shape, F32)
    for _ in range(k):
        mx = jnp.max(v, axis=0, keepdims=True)
        idx = jnp.min(jnp.where(v == mx, rowf, float(v.shape[0])), axis=0, keepdims=True)
        hit = rowf == idx
        chosen = jnp.where(hit, 1.0, chosen)
        v = jnp.where(hit, -jnp.inf, v)
    return chosen > 0.0


def _moe_route_kernel(x_ref, g_ref, sc_ref, sh_ref, wr_ref, rb_ref, h_ref, pos_ref, wt_ref, cnt_ref):
    tm = ROW_TILE
    h = _norm_mod(x_ref[...], g_ref[...], sc_ref[...], sh_ref[...])
    h_ref[...] = h.astype(BF16)
    logits = jnp.dot(h, wr_ref[...], precision=HIGHEST, preferred_element_type=F32)
    lt = logits.T[:N_EXPERTS]
    scores = jax.nn.sigmoid(lt)
    sel = scores + rb_ref[...]

    per = N_EXPERTS // N_GROUPS
    grp = sel.reshape(N_GROUPS, per, tm)
    sub = lax.broadcasted_iota(jnp.int32, grp.shape, 1)
    m1 = jnp.max(grp, axis=1, keepdims=True)
    first = jnp.min(jnp.where(grp == m1, sub, per), axis=1, keepdims=True)
    m2 = jnp.max(jnp.where(sub == first, -jnp.inf, grp), axis=1, keepdims=True)
    gscore = (m1 + m2).reshape(N_GROUPS, tm)
    gmask = _rank_lt(gscore, TOPK_GROUPS)
    emask = jnp.broadcast_to(gmask.reshape(N_GROUPS, 1, tm), grp.shape).reshape(N_EXPERTS, tm)
    chosen = _top_rows(jnp.where(emask, sel, NEG), TOP_K)

    w = jnp.where(chosen, scores, 0.0)
    wt_ref[...] = w / jnp.sum(w, axis=0, keepdims=True) * ROUTED_SCALE

    upper = jnp.where(_row_iota((tm, tm)) <= _lane_iota((tm, tm)), 1.0, 0.0).astype(BF16)
    incl = jnp.dot(jnp.where(chosen, 1.0, 0.0).astype(BF16), upper, preferred_element_type=F32)
    pos_ref[...] = jnp.where(chosen, incl - 1.0, -1.0)
    cnt_ref[0] = jnp.broadcast_to(incl[:, tm - 1:tm], (N_EXPERTS, LANE))


def _moe_route(x2, g, sc, sh, w_router_pad, router_bias_col):
    s, d = x2.shape
    tm = ROW_TILE
    vec = pl.BlockSpec((1, d), lambda i: (0, 0))
    et = pl.BlockSpec((N_EXPERTS, tm), lambda i: (0, i))
    return pl.pallas_call(
        _moe_route_kernel,
        grid=(s // tm,),
        in_specs=[pl.BlockSpec((tm, d), lambda i: (i, 0)), vec, vec, vec,
                  pl.BlockSpec((d, LANE), lambda i: (0, 0)),
                  pl.BlockSpec((N_EXPERTS, 1), lambda i: (0, 0))],
        out_specs=[pl.BlockSpec((tm, d), lambda i: (i, 0)), et, et,
                   pl.BlockSpec((1, N_EXPERTS, LANE), lambda i: (i, 0, 0))],
        out_shape=[jax.ShapeDtypeStruct((s, d), BF16), jax.ShapeDtypeStruct((N_EXPERTS, s), F32),
                   jax.ShapeDtypeStruct((N_EXPERTS, s), F32),
                   jax.ShapeDtypeStruct((s // tm, N_EXPERTS, LANE), F32)],
        compiler_params=_cparams("parallel"),
        name="moe_route",
    )(x2, g, sc, sh, w_router_pad, router_bias_col)


def _moe_kernel(cnt_ref, x_ref, h_ref, pos_ref, wt_ref, wg_ref, wu_ref, wd_ref, sg_ref, su_ref, sd_ref,
                g2_ref, o_ref, acc_ref):
    i = pl.program_id(0)
    e = pl.program_id(1)
    tm = ROW_TILE
    r = MOE_CHUNK

    @pl.when(e == 0)
    def _():
        h = h_ref[...]
        a = jnp.dot(h, sg_ref[...], preferred_element_type=F32)
        a = a * jax.nn.sigmoid(a) * jnp.dot(h, su_ref[...], preferred_element_type=F32)
        acc_ref[...] = jnp.dot(a.astype(BF16), sd_ref[...], preferred_element_type=F32)

    first = e * MOE_EXPERTS_PER_STEP
    n = cnt_ref[i * N_EXPERTS + first]
    for k in range(1, MOE_EXPERTS_PER_STEP):
        n = jnp.maximum(n, cnt_ref[i * N_EXPERTS + first + k])
    prows = [pos_ref[pl.ds(first + k, 1), :] for k in range(MOE_EXPERTS_PER_STEP)]
    wrows = [wt_ref[pl.ds(first + k, 1), :] for k in range(MOE_EXPERTS_PER_STEP)]

    def chunk(c, _):
        slot = (_row_iota((r, tm)) + c * r).astype(F32)
        hits = [prow == slot for prow in prows]
        onehot = jnp.concatenate([jnp.where(hit, 1.0, 0.0).astype(BF16) for hit in hits], axis=0)
        xg = jnp.dot(onehot, h_ref[...], preferred_element_type=F32).astype(BF16)
        ys = []
        for k in range(MOE_EXPERTS_PER_STEP):
            xk = xg[k * r:(k + 1) * r]
            a = jnp.dot(xk, wg_ref[k], preferred_element_type=F32)
            a = a * jax.nn.sigmoid(a) * jnp.dot(xk, wu_ref[k], preferred_element_type=F32)
            y = jnp.dot(a.astype(BF16), wd_ref[k], preferred_element_type=F32)
            wr = jnp.sum(jnp.where(hits[k], wrows[k], 0.0), axis=-1, keepdims=True)
            ys.append((y * wr).astype(BF16))
        acc_ref[...] += _dot_tn(onehot, jnp.concatenate(ys, axis=0))
        return 0

    lax.fori_loop(0, (n + r - 1) // r, chunk, 0)

    @pl.when(e == N_EXPERTS // MOE_EXPERTS_PER_STEP - 1)
    def _():
        o_ref[...] = x_ref[...] + g2_ref[...] * acc_ref[...]


def _moe_experts(counts, x2, h, pos_t, w_t, wg, wu, wd, sg, su, sd, g2):
    s, d = x2.shape
    tm = ROW_TILE
    ff = wg.shape[2]
    tile = pl.BlockSpec((tm, d), lambda i, e, c: (i, 0))
    et = pl.BlockSpec((N_EXPERTS, tm), lambda i, e, c: (0, i))

    def const(a):
        return pl.BlockSpec(a.shape, lambda i, e, c: (0,) * a.ndim)

    per = MOE_EXPERTS_PER_STEP
    grid_spec = pltpu.PrefetchScalarGridSpec(
        num_scalar_prefetch=1,
        grid=(s // tm, N_EXPERTS // per),
        in_specs=[tile, tile, et, et,
                  pl.BlockSpec((per, d, ff), lambda i, e, c: (e, 0, 0)),
                  pl.BlockSpec((per, d, ff), lambda i, e, c: (e, 0, 0)),
                  pl.BlockSpec((per, ff, d), lambda i, e, c: (e, 0, 0)),
                  const(sg), const(su), const(sd), const(g2)],
        out_specs=tile,
        scratch_shapes=[pltpu.VMEM((tm, d), F32)],
    )
    return pl.pallas_call(
        _moe_kernel,
        grid_spec=grid_spec,
        out_shape=jax.ShapeDtypeStruct((s, d), F32),
        compiler_params=_cparams("parallel", "arbitrary"),
        name="moe_experts",
    )(counts, x2, h, pos_t, w_t, wg, wu, wd, sg, su, sd, g2)


def _pad_lanes(v, width=LANE, offset=0):
    out = jnp.zeros((1, width), F32)
    return out.at[0, offset:offset + v.shape[0]].set(v.astype(F32))


def _head_cols(w, n_heads, dim):
    d = w.shape[0]
    w3 = w.reshape(d, n_heads, dim)
    return jnp.pad(w3, ((0, 0), (0, 0), (0, LANE - dim))).reshape(d, n_heads * LANE)


def _hybrid_w_in(w_in):
    d = w_in.shape[0]
    nq = NSA_HEADS * HEAD_DIM
    nkv = 6 * NSA_GROUPS * HEAD_DIM
    ng = 3 * NSA_HEADS
    nf = 3 * FOX_HEADS * HEAD_DIM
    c0, c1, c2, c3 = nq, nq + nkv, nq + nkv + ng, nq + nkv + ng + nf
    gates = w_in[:, c1:c2].reshape(d, NSA_GROUPS, 3 * NSA_HPG)
    gates = jnp.pad(gates, ((0, 0), (0, 0), (0, LANE - 3 * NSA_HPG))).reshape(d, NSA_GROUPS * LANE)
    ff = jnp.pad(w_in[:, c3:], ((0, 0), (0, 2 * LANE - FOX_HEADS)))
    return jnp.concatenate([
        _head_cols(w_in[:, :c0], NSA_HEADS, HEAD_DIM),
        _head_cols(w_in[:, c0:c1], 6 * NSA_GROUPS, HEAD_DIM),
        _head_cols(w_in[:, c2:c3], 3 * FOX_HEADS, HEAD_DIM),
        gates, ff], axis=1).astype(BF16)


def _pad_head_rows(w, n_heads, dim):
    d = w.shape[1]
    w3 = w.reshape(n_heads, dim, d)
    return jnp.pad(w3, ((0, 0), (0, LANE - dim), (0, 0))).reshape(n_heads * LANE, d).astype(BF16)


def _rope_inv(dim, offset):
    inv = ROPE_THETA ** (-jnp.arange(0, dim, 2, dtype=F32) / dim)
    return _pad_lanes(jnp.concatenate([inv, inv]), offset=offset)


def _hybrid_mixer(x2, posf, mods, norm_g, w_in, fox_f_bias, nsa_q_norm, nsa_k_norm, nsa_cmp_pe, nsa_w_cmp,
                  fox_q_norm, fox_k_norm, w_out):
    sh1, sc1, g1 = mods
    (qnt, kct, vct, ks, vst, kw, vwt, gates, fqt, fk, fvt, cedge) = _hy_prep(
        x2, norm_g, sc1, sh1, _hybrid_w_in(w_in), posf, _rope_inv(HEAD_DIM, 0), _pad_lanes(nsa_q_norm),
        _pad_lanes(nsa_k_norm),
        _pad_lanes(fox_q_norm), _pad_lanes(fox_k_norm), _pad_lanes(fox_f_bias))
    kc, vc_t = _compress(kct, vct, nsa_w_cmp, nsa_cmp_pe, nsa_k_norm)
    o_a = _nsa_attention(qnt, kc, vc_t, ks, vst, kw, vwt, gates)
    slack = (2.0 * HEAD_DIM ** 0.5 * LOG2E) * jnp.max(jnp.abs(fox_q_norm)) * jnp.max(jnp.abs(fox_k_norm))
    edges = (cedge[:, 0, :FOX_HEADS].T, cedge[:, 1, :FOX_HEADS].T)
    o_b = _causal_attention(fqt, fk, fvt, Q_TILE_FOX, edges, slack.reshape(1))
    half = NSA_HEADS * HEAD_DIM
    wa = _pad_head_rows(w_out[:half], NSA_HEADS, HEAD_DIM)
    wb = _pad_head_rows(w_out[half:], FOX_HEADS, HEAD_DIM)
    return _out_proj(o_a, o_b, 0, 0, wa, wb, x2, g1)


def _mla_mixer(x2, posf, mods, norm_g, w_in, q_a_norm, kv_a_norm, w_uq, w_ukv, qn_norm, kn_norm, qr_norm,
               kr_norm, w_out):
    sh1, sc1, g1 = mods
    d = x2.shape[1]
    w_kr = jnp.zeros((d, LANE), F32).at[:, QK_NOPE:QK_NOPE + QK_ROPE].set(w_in[:, Q_LORA + KV_LORA:])
    w_in_p = jnp.concatenate([w_in[:, :Q_LORA + KV_LORA], w_kr], axis=1).astype(BF16)
    hq = QK_NOPE + QK_ROPE
    wuq = _head_cols(w_uq, MLA_HEADS, hq).astype(BF16)
    wkv3 = w_ukv.reshape(KV_LORA, MLA_HEADS, QK_NOPE + V_HEAD)
    wuk = _head_cols(wkv3[:, :, :QK_NOPE].reshape(KV_LORA, -1), MLA_HEADS, QK_NOPE).astype(BF16)
    wuv = _head_cols(wkv3[:, :, QK_NOPE:].reshape(KV_LORA, -1), MLA_HEADS, V_HEAD).astype(BF16)
    gq = _pad_lanes(jnp.concatenate([qn_norm, qr_norm]))
    qt, k, vt = _mla_prep(x2, norm_g, sc1, sh1, w_in_p, posf, _rope_inv(QK_ROPE, QK_NOPE),
                          q_a_norm.reshape(1, -1).astype(F32),
                          kv_a_norm.reshape(1, -1).astype(F32), wuq, wuk, wuv, gq, _pad_lanes(kn_norm),
                          _pad_lanes(kr_norm, offset=QK_NOPE))
    o = _causal_attention(qt, k, vt, Q_TILE_MLA)
    w_pad = _pad_head_rows(w_out, MLA_HEADS, V_HEAD)
    half = w_pad.shape[0] // 2
    return _out_proj(o, o, 0, 1, w_pad[:half], w_pad[half:], x2, g1)


def _moe_ffn(x2, mods, norm_g, w_router, router_bias, w_gate, w_up, w_down, ws_gate, ws_up, ws_down):
    sh2, sc2, g2 = mods
    w_r = jnp.pad(w_router.astype(F32), ((0, 0), (0, LANE - N_EXPERTS)))
    h, pos_t, w_t, cnt = _moe_route(x2, norm_g, sc2, sh2, w_r, router_bias.reshape(N_EXPERTS, 1).astype(F32))
    counts = cnt[:, :, 0].astype(jnp.int32).reshape(-1)
    return _moe_experts(counts, x2, h, pos_t, w_t, w_gate.astype(BF16), w_up.astype(BF16),
                        w_down.astype(BF16), ws_gate.astype(BF16), ws_up.astype(BF16), ws_down.astype(BF16), g2)


def kernel(x, c, positions, norm_attn, norm_ffn, w_ada, b_ada, hy_w_in, fox_f_bias, nsa_q_norm, nsa_k_norm, nsa_cmp_pe, nsa_w_cmp, fox_q_norm, fox_k_norm, hy_w_out, mla_w_in, mla_q_a_norm, mla_kv_a_norm, mla_w_uq, mla_w_ukv, mla_qn_norm, mla_kn_norm, mla_qr_norm, mla_kr_norm, mla_w_out, moe_w_router, moe_router_bias, moe_w_gate, moe_w_up, moe_w_down, moe_ws_gate, moe_ws_up, moe_ws_down):
    b, s, d = x.shape
    assert b == 1 and s % KV_TILE == 0 and s >= WINDOW + Q_TILE_NSA
    depth = w_ada.shape[0]
    x2 = x.reshape(s, d).astype(F32)
    posf = positions.reshape(s, 1).astype(F32)
    mod = _ada_mod(c.astype(F32), w_ada.astype(F32), b_ada.astype(F32))

    for layer in range(depth):
        m = [mod[layer, :, k * d:(k + 1) * d] for k in range(6)]
        i = layer // 2
        g_attn = norm_attn[layer].reshape(1, d).astype(F32)
        if layer % 2 == 0:
            x2 = _hybrid_mixer(x2, posf, m[0:3], g_attn, hy_w_in[i], fox_f_bias[i], nsa_q_norm[i],
                               nsa_k_norm[i], nsa_cmp_pe[i], nsa_w_cmp[i], fox_q_norm[i], fox_k_norm[i],
                               hy_w_out[i])
        else:
            x2 = _mla_mixer(x2, posf, m[0:3], g_attn, mla_w_in[i], mla_q_a_norm[i], mla_kv_a_norm[i],
                            mla_w_uq[i], mla_w_ukv[i], mla_qn_norm[i], mla_kn_norm[i], mla_qr_norm[i],
                            mla_kr_norm[i], mla_w_out[i])
        x2 = _moe_ffn(x2, m[3:6], norm_ffn[layer].reshape(1, d).astype(F32), moe_w_router[layer],
                      moe_router_bias[layer], moe_w_gate[layer], moe_w_up[layer], moe_w_down[layer],
                      moe_ws_gate[layer], moe_ws_up[layer], moe_ws_down[layer])
    return x2.reshape(b, s, d)
```

```python
import functools

import numpy as np
import jax
import jax.numpy as jnp
from jax import lax
from jax.experimental import pallas as pl
from jax.experimental.pallas import tpu as pltpu

F32 = jnp.float32
BF16 = jnp.bfloat16
HIGHEST = lax.Precision.HIGHEST

LANE = 128
VMEM_LIMIT_BYTES = 56 * 1024 * 1024

HEAD_DIM = 64
NSA_HEADS = 8
NSA_GROUPS = 2
NSA_HPG = NSA_HEADS // NSA_GROUPS
CMP_LEN = 32
CMP_STRIDE = 16
SLC_BLOCK = 64
SLC_TOPK = 16
WINDOW = 512
FOX_HEADS = 8
MLA_HEADS = 16
Q_LORA = 384
KV_LORA = 256
QK_NOPE = 64
QK_ROPE = 32
V_HEAD = 64
N_EXPERTS = 64
TOP_K = 8
N_GROUPS = 8
TOPK_GROUPS = 4
EXPERT_FF = 256
ROUTED_SCALE = 2.5
ROPE_THETA = 10000.0
EPS = 1e-6
NEG = -1e30
BIG = 1e6

ROW_TILE = 512
PREP_TILE = 512
GATE_ROWS = 16
Q_TILE_NSA = 256
KV_TILE = 512
Q_TILE_FOX = 1024
Q_TILE_MLA = 2048
Q_STRIP = 256
WIN_TILE = 256
SUPER_BLOCKS = 32
NSA_PREFIX_PARTS = 8
SUM_ROW = 64
V_ROWS = 80
REF_ROW = 104
REF_SLAB = 96
EXP_GUARD = 100.0
SCORE_LOOKAHEAD = 4
SKIP_MARGIN = 160.0
LOG2E = 1.4426950408889634
MOE_CHUNK = 128
MOE_EXPERTS_PER_STEP = 8

HY_Q0 = 0
HY_KV0 = HY_Q0 + NSA_HEADS
HY_F0 = HY_KV0 + 6 * NSA_GROUPS
HY_G0 = HY_F0 + 3 * FOX_HEADS
HY_FF = HY_G0 + NSA_GROUPS
HY_BLOCKS = HY_FF + 2


def _cparams(*sem):
    return pltpu.CompilerParams(dimension_semantics=sem, vmem_limit_bytes=VMEM_LIMIT_BYTES)


def _lane_iota(shape):
    return lax.broadcasted_iota(jnp.int32, shape, len(shape) - 1)


def _row_iota(shape):
    return lax.broadcasted_iota(jnp.int32, shape, len(shape) - 2)


def _dot_tn(a, b):
    return lax.dot_general(a, b, (((0,), (0,)), ((), ())), preferred_element_type=F32)


def _ada_kernel(c_ref, w_ref, b_ref, o_ref):
    c = c_ref[...]
    cond = c * jax.nn.sigmoid(c)
    o_ref[0] = jnp.dot(cond, w_ref[0], precision=HIGHEST, preferred_element_type=F32) + b_ref[0]


def _ada_mod(c, w_ada, b_ada):
    depth, d, n = w_ada.shape
    tn = 768
    c8 = jnp.broadcast_to(c.reshape(1, d), (8, d))
    out = pl.pallas_call(
        _ada_kernel,
        grid=(depth, n // tn),
        in_specs=[pl.BlockSpec((8, d), lambda l, j: (0, 0)),
                  pl.BlockSpec((1, d, tn), lambda l, j: (l, 0, j)),
                  pl.BlockSpec((1, 1, tn), lambda l, j: (l, 0, j))],
        out_specs=pl.BlockSpec((1, 8, tn), lambda l, j: (l, 0, j)),
        out_shape=jax.ShapeDtypeStruct((depth, 8, n), F32),
        compiler_params=_cparams("parallel", "parallel"),
        name="ada_mod",
    )(c8, w_ada, b_ada.reshape(depth, 1, n))
    return out[:, 0:1, :]


def _norm_mod(x, g, sc, sh):
    ms = jnp.mean(x * x, axis=-1, keepdims=True)
    return (x * lax.rsqrt(ms + EPS) * g) * (1.0 + sc) + sh


def _head_rms(x, gain, n_real):
    ss = jnp.sum(x * x, axis=-1, keepdims=True)
    return x * lax.rsqrt(ss * (1.0 / n_real) + EPS) * gain


def _rope64(x, cos, sin):
    lane = _lane_iota(x.shape)
    rot = jnp.where(lane < 32, -pltpu.roll(x, LANE - 32, 1), pltpu.roll(x, 32, 1))
    return x * cos + rot * sin


def _t_bf16(x, rows=LANE):
    return x.T[:rows].astype(BF16)


def _split3(c):
    hi = c.astype(BF16).astype(F32)
    r1 = c - hi
    mid = r1.astype(BF16).astype(F32)
    lo = (r1 - mid).astype(BF16).astype(F32)
    return hi, mid, lo


def _hy_prep_kernel(x_ref, ng_ref, sc_ref, sh_ref, w_ref, pos_ref, inv_ref, gq_ref, gk_ref, gfq_ref, gfk_ref,
                    fb_ref, qnt_ref, kct_ref, vct_ref, ks_ref, vst_ref, kw_ref, vwt_ref, gate_ref,
                    fqt_ref, fk_ref, fvt_ref, cedge_ref, carry_ref):
    i = pl.program_id(0)
    tm = PREP_TILE
    shp = (tm, LANE)
    lane = _lane_iota(shp)

    act = _norm_mod(x_ref[...], ng_ref[...], sc_ref[...], sh_ref[...]).astype(BF16)
    pairs = {}

    def blk(b):
        if b // 2 not in pairs:
            cols = slice((b // 2) * 2 * LANE, (b // 2 + 1) * 2 * LANE)
            pairs[b // 2] = jnp.dot(act, w_ref[:, cols], preferred_element_type=F32)
        return pairs[b // 2][:, (b % 2) * LANE:(b % 2 + 1) * LANE]

    ang = pos_ref[...] * inv_ref[...]
    real = lane < HEAD_DIM
    cos = jnp.where(real, jnp.cos(ang), 1.0)
    sin = jnp.where(real, jnp.sin(ang), 0.0)
    gq, gk, gfq, gfk = gq_ref[...], gk_ref[...], gfq_ref[...], gfk_ref[...]
    scale = HEAD_DIM ** -0.5 * LOG2E
    ones_row = lane == SUM_ROW
    ref_ones = jnp.where((lane >= REF_ROW) & (lane < REF_ROW + 3), 1.0, 0.0)

    for h in range(NSA_HEADS):
        q = _rope64(_head_rms(blk(HY_Q0 + h), gq, HEAD_DIM), cos, sin) * scale
        qnt_ref[h * LANE:(h + 1) * LANE, :] = _t_bf16(q)

    row = _row_iota(shp) + i * tm
    onehot = jnp.where(lane - HEAD_DIM == ((row // SLC_BLOCK) % SUPER_BLOCKS), 1.0, 0.0)
    for g in range(NSA_GROUPS):
        def kv(r):
            return blk(HY_KV0 + r * NSA_GROUPS + g)
        sl = slice(g * LANE, (g + 1) * LANE)
        kct_ref[g] = _rope64(kv(0), cos, sin)[:, :HEAD_DIM].astype(BF16)
        vct_ref[g] = kv(1)[:, :HEAD_DIM].astype(BF16)
        ks = _rope64(_head_rms(kv(2), gk, HEAD_DIM), cos, sin)
        ks_ref[:, sl] = (ks + onehot + ref_ones).astype(BF16)
        vst_ref[g, 0] = _t_bf16(jnp.where(ones_row, 1.0, kv(3)), V_ROWS)
        kw_ref[:, sl] = (_rope64(_head_rms(kv(4), gk, HEAD_DIM), cos, sin) + ref_ones).astype(BF16)
        vwt = _t_bf16(jnp.where(ones_row, 1.0, kv(5)), V_ROWS)
        for cidx in range(tm // WIN_TILE):
            vwt_ref[g, cidx] = vwt[:, cidx * WIN_TILE:(cidx + 1) * WIN_TILE]
        gate_ref[g] = jax.nn.sigmoid(blk(HY_G0 + g)).T[:GATE_ROWS]

    @pl.when(i == 0)
    def _():
        carry_ref[...] = jnp.zeros_like(carry_ref)

    z = blk(HY_FF) + fb_ref[...]
    logf = jnp.minimum(z, 0.0) - jnp.log1p(jnp.exp(-jnp.abs(z)))
    tri = jnp.where(_row_iota((tm, tm)) >= _lane_iota((tm, tm)), 1.0, 0.0).astype(F32)
    cum = jnp.dot(tri, logf, precision=HIGHEST, preferred_element_type=F32) + carry_ref[...]
    carry_ref[...] = cum[tm - 1:tm, :]
    cedge_ref[0] = jnp.concatenate([cum[0:1] * LOG2E, cum[tm - 1:tm] * LOG2E, jnp.zeros((6, LANE), F32)], axis=0)

    for h in range(FOX_HEADS):
        c = jnp.broadcast_to(cum[:, h:h + 1], shp) * LOG2E
        hi, mid, lo = _split3(c)
        fq = _head_rms(blk(HY_F0 + h), gfq, HEAD_DIM) * scale
        fq = jnp.where(real, fq, jnp.where(lane == 64, hi, jnp.where(lane == 65, mid, jnp.where(
            lane == 66, lo, jnp.where(lane < 70, 1.0, 0.0)))))
        fk = _head_rms(blk(HY_F0 + FOX_HEADS + h), gfk, HEAD_DIM)
        fk = jnp.where(real, fk, jnp.where(lane < 67, 1.0, jnp.where(lane == 67, -hi, jnp.where(
            lane == 68, -mid, jnp.where(lane == 69, -lo, ref_ones)))))
        sl = slice(h * LANE, (h + 1) * LANE)
        fqt_ref[sl, :] = _t_bf16(fq)
        fk_ref[:, sl] = fk.astype(BF16)
        fvt_ref[h, 0] = _t_bf16(jnp.where(ones_row, 1.0, blk(HY_F0 + 2 * FOX_HEADS + h)), V_ROWS)


def _feat_major(heads, s, tm):
    return (pl.BlockSpec((heads * LANE, tm), lambda i: (0, i)),
            jax.ShapeDtypeStruct((heads * LANE, s), BF16))


def _value_tiles(heads, s, tm, tk, rows=LANE):
    return (pl.BlockSpec((heads, tm // tk, rows, tk), lambda i: (0, i, 0, 0)),
            jax.ShapeDtypeStruct((heads, s // tk, rows, tk), BF16))


def _pad_feature_rows(o):
    return jnp.concatenate([o, jnp.zeros((LANE - HEAD_DIM, o.shape[1]), o.dtype)], axis=0)


def _hy_prep(x2, norm_g, sc, sh, w_in, posf, inv128, gq, gk, gfq, gfk, fbias):
    s, d = x2.shape
    tm = PREP_TILE
    assert tm == KV_TILE and w_in.shape == (d, HY_BLOCKS * LANE) and HY_BLOCKS % 2 == 0
    vec = pl.BlockSpec((1, LANE), lambda i: (0, 0))
    dvec = pl.BlockSpec((1, d), lambda i: (0, 0))

    def rows(nb):
        return (pl.BlockSpec((tm, nb * LANE), lambda i: (i, 0)), jax.ShapeDtypeStruct((s, nb * LANE), BF16))

    tok = (pl.BlockSpec((NSA_GROUPS, tm, HEAD_DIM), lambda i: (0, i, 0)),
           jax.ShapeDtypeStruct((NSA_GROUPS, s, HEAD_DIM), BF16))
    gate = (pl.BlockSpec((NSA_GROUPS, GATE_ROWS, tm), lambda i: (0, 0, i)),
            jax.ShapeDtypeStruct((NSA_GROUPS, GATE_ROWS, s), F32))
    outs = [_feat_major(NSA_HEADS, s, tm), tok, tok, rows(NSA_GROUPS),
            _value_tiles(NSA_GROUPS, s, tm, KV_TILE, V_ROWS),
            rows(NSA_GROUPS), _value_tiles(NSA_GROUPS, s, tm, WIN_TILE, V_ROWS), gate,
            _feat_major(FOX_HEADS, s, tm), rows(FOX_HEADS), _value_tiles(FOX_HEADS, s, tm, KV_TILE, V_ROWS),
            (pl.BlockSpec((1, 8, LANE), lambda i: (i, 0, 0)), jax.ShapeDtypeStruct((s // tm, 8, LANE), F32))]
    return pl.pallas_call(
        _hy_prep_kernel,
        grid=(s // tm,),
        in_specs=[pl.BlockSpec((tm, d), lambda i: (i, 0)), dvec, dvec, dvec,
                  pl.BlockSpec((d, HY_BLOCKS * LANE), lambda i: (0, 0), pipeline_mode=pl.Buffered(1)),
                  pl.BlockSpec((tm, 1), lambda i: (i, 0)), vec, vec, vec, vec, vec, vec],
        out_specs=[o[0] for o in outs],
        out_shape=[o[1] for o in outs],
        scratch_shapes=[pltpu.VMEM((1, LANE), F32)],
        compiler_params=_cparams("arbitrary"),
        name="hybrid_prep",
    )(x2, norm_g, sc, sh, w_in, posf, inv128, gq, gk, gfq, gfk, fbias)


def _compress_kernel(kc_ref, vc_ref, wk_ref, wv_ref, pek_ref, pev_ref, gk_ref, ko_ref, vo_ref):
    half = CMP_STRIDE * HEAD_DIM

    def comp(ch_ref, w_ref, pe_ref):
        ch = ch_ref[0]
        nc = ch.shape[0]
        a = jnp.dot(ch, w_ref[:half], preferred_element_type=F32)
        b = jnp.dot(ch, w_ref[half:], preferred_element_type=F32)
        nxt = pltpu.roll(b, nc - 1, 0)
        pe = jnp.dot(jnp.broadcast_to(pe_ref[...], (8, 2 * half)).astype(BF16), w_ref[...],
                     preferred_element_type=F32)[0:1]
        return a + nxt + pe

    ko_ref[0] = _head_rms(comp(kc_ref, wk_ref, pek_ref), gk_ref[...], HEAD_DIM).astype(BF16)
    vo_ref[0] = comp(vc_ref, wv_ref, pev_ref).T.astype(BF16)


def _compress(kct, vct, w_cmp, cmp_pe, k_norm):
    g, s, _ = kct.shape
    nc = s // CMP_STRIDE
    wide = CMP_STRIDE * HEAD_DIM
    kch = kct.reshape(g, nc, wide)
    vch = vct.reshape(g, nc, wide)
    w_pad = jnp.pad(w_cmp, ((0, 0), (0, 0), (0, LANE - HEAD_DIM))).astype(BF16)
    ch = pl.BlockSpec((1, nc, wide), lambda i: (i, 0, 0))
    wspec = pl.BlockSpec((2 * wide, LANE), lambda i: (0, 0))
    pespec = pl.BlockSpec((1, 2 * wide), lambda i: (0, 0))
    return pl.pallas_call(
        _compress_kernel,
        grid=(g,),
        in_specs=[ch, ch, wspec, wspec, pespec, pespec, pl.BlockSpec((1, LANE), lambda i: (0, 0))],
        out_specs=[pl.BlockSpec((1, nc, LANE), lambda i: (i, 0, 0)),
                   pl.BlockSpec((1, LANE, nc), lambda i: (i, 0, 0))],
        out_shape=[jax.ShapeDtypeStruct((g, nc, LANE), BF16), jax.ShapeDtypeStruct((g, LANE, nc), BF16)],
        compiler_params=_cparams("parallel"),
        name="nsa_compress",
    )(kch, vch, w_pad[0], w_pad[1], cmp_pe[0].reshape(1, 2 * wide).astype(F32),
      cmp_pe[1].reshape(1, 2 * wide).astype(F32), _pad_lanes(k_norm))


def _masked_softmax_t(s, mask):
    s = jnp.where(mask, s, NEG)
    m = jnp.max(s, axis=0, keepdims=True)
    e = jnp.exp2(s - m)
    inv = 1.0 / jnp.maximum(jnp.sum(e, axis=0, keepdims=True), 1e-30)
    return e, jnp.where(m > 0.5 * NEG, inv, 0.0)


def _online_steps(steps, ms, acc_ref):
    ms = list(ms)

    def scores(step):
        k_tile, qa, _, c, mask = step
        s = jnp.dot(k_tile, qa, preferred_element_type=F32)
        if mask is not None:
            s = jnp.where(mask, s, NEG)
        return s, jnp.max(s, axis=0, keepdims=True)

    nxt = scores(steps[0])
    for idx, (_, _, vt, c, _) in enumerate(steps):
        sl = slice(c * Q_STRIP, (c + 1) * Q_STRIP)
        s, s_max = nxt
        if idx + 1 < len(steps):
            nxt = scores(steps[idx + 1])
        m_new = jnp.maximum(ms[c], s_max)
        a = jnp.exp2(ms[c] - m_new)
        p = jnp.exp2((s - m_new).astype(BF16))
        ms[c] = m_new
        acc_ref[:, sl] = a * acc_ref[:, sl] + jnp.dot(vt, p, preferred_element_type=F32)
    return tuple(ms)


def _m_init(n_strips):
    return tuple(jnp.full((1, Q_STRIP), NEG, F32) for _ in range(n_strips))


def _with_ref_rows(qa, m):
    hi, mid, lo = _split3(-m)
    r = _row_iota((LANE - REF_SLAB, Q_STRIP)) + REF_SLAB
    slab = jnp.where(r == REF_ROW, hi, jnp.where(r == REF_ROW + 1, mid, jnp.where(r == REF_ROW + 2, lo, 0.0)))
    return jnp.concatenate([qa[:REF_SLAB], slab.astype(BF16)], axis=0)


def _first_tile_max(k_tile, qa_strips, masks):
    return tuple(jnp.max(jnp.where(mask, jnp.dot(k_tile, qa, preferred_element_type=F32), NEG), axis=0, keepdims=True)
                 for qa, mask in zip(qa_strips, masks))


def _fast_steps(steps, state, acc_ref):
    state = list(state)
    depth, last = min(SCORE_LOOKAHEAD, len(steps)), {}
    for k, step in enumerate(steps):
        depth = min(depth, k - last.get(step[3], k - depth))
        last[step[3]] = k

    def scores(step):
        k_tile, qa, _, c, mask = step
        s = jnp.dot(k_tile, _with_ref_rows(qa, state[c][0]), preferred_element_type=F32)
        if mask is not None:
            s = jnp.where(mask, s, NEG)
        return s

    ahead = [scores(st) for st in steps[:depth]]
    for idx, (_, _, vt, c, _) in enumerate(steps):
        sl = slice(c * Q_STRIP, (c + 1) * Q_STRIP)
        s = ahead.pop(0)
        m, worst = state[c]
        cm = jnp.max(s, axis=0, keepdims=True)
        inc = jnp.maximum(cm, 0.0)
        state[c] = (m + inc, jnp.maximum(worst, cm))
        if idx + depth < len(steps):
            ahead.append(scores(steps[idx + depth]))
        p = jnp.exp2(s).astype(BF16)
        acc_ref[:, sl] = jnp.exp2(-inc) * (acc_ref[:, sl] + jnp.dot(vt, p, preferred_element_type=F32))
    return tuple(state)


def _flat(state):
    return tuple(x for pair in state for x in pair)


def _nest(flat):
    return tuple((flat[2 * c], flat[2 * c + 1]) for c in range(len(flat) // 2))


def _nsa_kernel(qt_ref, kc_ref, vct_ref, ks_ref, vst_ref, kw_ref, vwt_ref, gate_ref, ovt_ref, o_ref,
                qaug_ref, acc_ref, wacc_ref, *, n_sel):
    i = pl.program_id(1)
    tq = Q_TILE_NSA
    cols = NSA_HPG * tq
    qs = i * tq
    nc = kc_ref.shape[1]
    nslc = ovt_ref.shape[0]
    n_super = nslc // SUPER_BLOCKS

    qt = jnp.concatenate([qt_ref[h * LANE:(h + 1) * LANE, :] for h in range(NSA_HPG)], axis=1)
    tq_row = qs + (_lane_iota((1, cols)) % tq)

    def compress_and_select(n_c, n_b):
        s = jnp.dot(kc_ref[0, :n_c], qt, preferred_element_type=F32)
        cmp_end = _row_iota((n_c, 1)) * CMP_STRIDE + (CMP_LEN - 1)
        e, inv_l = _masked_softmax_t(s, cmp_end <= tq_row)
        o_cmp = jnp.dot(vct_ref[0, :, :n_c], e.astype(BF16), preferred_element_type=F32) * inv_l

        psum = e[:, 0:tq] * inv_l[:, 0:tq]
        for h in range(1, NSA_HPG):
            psum = psum + e[:, h * tq:(h + 1) * tq] * inv_l[:, h * tq:(h + 1) * tq]
        p_hi = psum.astype(BF16)
        p_lo = (psum - p_hi.astype(F32)).astype(BF16)
        ovt = ovt_ref[:n_b, :n_c]
        imp = (jnp.dot(ovt, p_hi, preferred_element_type=F32)
               + jnp.dot(ovt, p_lo, preferred_element_type=F32))

        jj = _row_iota((n_b, tq))
        tq_blk = qs + _lane_iota((n_b, tq))
        cur = tq_blk // SLC_BLOCK
        forced = (jj == 0) | (jj == cur) | (jj == cur - 1)
        causal_blk = jj * SLC_BLOCK <= tq_blk
        val = jnp.where(forced, imp + BIG, imp)
        val = jnp.where(causal_blk, val, NEG)
        jjf = jj.astype(F32)

        def pick(_, carry):
            val, sel = carry
            mx = jnp.max(val, axis=0, keepdims=True)
            idx = jnp.min(jnp.where(val == mx, jjf, float(n_b)), axis=0, keepdims=True)
            hit = jjf == idx
            return jnp.where(hit, -jnp.inf, val), jnp.where(hit, 1.0, sel)

        _, sel = lax.fori_loop(0, min(n_sel, n_b), pick, (val, jnp.zeros((n_b, tq), F32)))
        bias = jnp.where((sel > 0.0) & causal_blk, 0.0, NEG)
        if n_b < nslc:
            bias = jnp.concatenate([bias, jnp.full((nslc - n_b, tq), NEG, F32)], axis=0)
        return o_cmp, bias

    parts = max(1, min(NSA_PREFIX_PARTS, nc // LANE))
    seq = nc * CMP_STRIDE

    def dispatch(k):
        full = lambda: compress_and_select(nc * k // parts, nslc * k // parts)
        if k == parts:
            return full()
        return lax.cond(qs + tq <= seq * k // parts, full, lambda: dispatch(k + 1))

    o_cmp, bias_t = dispatch(1)

    q_rows = qt[:HEAD_DIM].astype(F32)
    spare = jnp.zeros((LANE - HEAD_DIM - SUPER_BLOCKS, cols), F32)
    for st in range(n_super):
        b = bias_t[st * SUPER_BLOCKS:(st + 1) * SUPER_BLOCKS]
        b = jnp.concatenate([b] * NSA_HPG, axis=1)
        qaug_ref[st] = jnp.concatenate([q_rows, b, spare], axis=0).astype(BF16)

    tk = KV_TILE
    per_super = SUPER_BLOCKS * SLC_BLOCK // tk
    j_last = (qs + tq - 1) // tk
    n_strips = cols // Q_STRIP
    strips = [slice(c * Q_STRIP, (c + 1) * Q_STRIP) for c in range(n_strips)]

    def causal_masks(j):
        kpos = j * tk + _row_iota((tk, 1))
        return [kpos <= tq_row[:, sl] for sl in strips]

    def slc_steps(j, masks=None):
        k0 = pl.multiple_of(j * tk, tk)
        k_tile, vt, st = ks_ref[pl.ds(k0, tk), :], vst_ref[0, j], j // per_super
        return [(k_tile, qaug_ref[st, :, strips[c]], vt, c, None if masks is None else masks[c])
                for c in range(n_strips)]

    group = 4

    def grouped(jj, flat):
        steps = [st for t in range(group) for st in slc_steps(group * jj + t)]
        return _flat(_fast_steps(steps, _nest(flat), acc_ref))

    def single(j, flat):
        return _flat(_fast_steps(slc_steps(j), _nest(flat), acc_ref))

    acc_ref[...] = jnp.zeros(acc_ref.shape, F32)
    head_rows = 16
    head_masks = [_row_iota((head_rows, 1)) <= tq_row[:, sl] for sl in strips]
    m0 = _first_tile_max(ks_ref[0:head_rows, :], [qaug_ref[0, :, sl] for sl in strips], head_masks)
    n_groups = j_last // group
    flat = lax.fori_loop(0, n_groups, grouped, _flat(tuple((m, jnp.zeros_like(m)) for m in m0)))
    flat = lax.fori_loop(group * n_groups, j_last, single, flat)
    state = _fast_steps(slc_steps(j_last, causal_masks(j_last)), _nest(flat), acc_ref)
    worst = jnp.max(jnp.concatenate([w for _, w in state], axis=1))

    @pl.when(worst > EXP_GUARD)
    def _():
        acc_ref[...] = jnp.zeros(acc_ref.shape, F32)
        lax.fori_loop(0, j_last + 1, lambda j, ms: _online_steps(slc_steps(j, causal_masks(j)), ms, acc_ref),
                      _m_init(n_strips))

    o_slc = acc_ref[:HEAD_DIM] / jnp.maximum(acc_ref[SUM_ROW:SUM_ROW + 1], 1e-30)

    wt = WIN_TILE
    n_wt = (WINDOW + tq) // wt
    ws = pl.multiple_of(jnp.maximum(qs - WINDOW, 0), wt)

    def win_masks(k0, rows):
        dist = tq_row - (k0 + _row_iota((rows, 1)))
        in_win = (dist >= 0) & (dist < WINDOW)
        return [in_win[:, sl] for sl in strips]

    def win_steps(t):
        k0 = pl.multiple_of(ws + t * wt, wt)
        k_tile, vt, masks = kw_ref[pl.ds(k0, wt), :], vwt_ref[0, ws // wt + t], win_masks(k0, wt)
        return [(k_tile, qt[:, strips[c]], vt, c, masks[c]) for c in range(n_strips)]

    k_start = pl.multiple_of(jnp.maximum(qs - tq, 0), tq)
    m0 = _first_tile_max(kw_ref[pl.ds(k_start, head_rows), :], [qt[:, sl] for sl in strips],
                         win_masks(k_start, head_rows))
    wacc_ref[...] = jnp.zeros(wacc_ref.shape, F32)
    state = _fast_steps([st for t in range(n_wt) for st in win_steps(t)],
                        tuple((m, jnp.zeros_like(m)) for m in m0), wacc_ref)
    worst = jnp.max(jnp.concatenate([w for _, w in state], axis=1))

    @pl.when(worst > EXP_GUARD)
    def _():
        wlen = n_wt * wt
        s = jnp.dot(kw_ref[pl.ds(ws, wlen), :], qt, preferred_element_type=F32)
        dist = tq_row - (ws + _row_iota((wlen, 1)))
        e, _ = _masked_softmax_t(s, (dist >= 0) & (dist < WINDOW))
        e = e.astype(BF16)
        acc = jnp.zeros(wacc_ref.shape, F32)
        for t in range(n_wt):
            acc = acc + jnp.dot(vwt_ref[0, ws // wt + t], e[t * wt:(t + 1) * wt], preferred_element_type=F32)
        wacc_ref[...] = acc

    o_win = wacc_ref[:HEAD_DIM] / jnp.maximum(wacc_ref[SUM_ROW:SUM_ROW + 1], 1e-30)

    gate = gate_ref[0]
    for h in range(NSA_HPG):
        sl = slice(h * tq, (h + 1) * tq)
        o = (gate[3 * h:3 * h + 1] * o_cmp[:HEAD_DIM, sl] + gate[3 * h + 1:3 * h + 2] * o_slc[:, sl]
             + gate[3 * h + 2:3 * h + 3] * o_win[:, sl])
        o_ref[:, h * LANE:(h + 1) * LANE] = _pad_feature_rows(o).T.astype(BF16)


def _overlap_t(s, nslc_pad):
    nc = s // CMP_STRIDE
    cmp_start = np.arange(nc) * CMP_STRIDE
    slc_start = np.arange(nslc_pad) * SLC_BLOCK
    ov = np.clip(np.minimum(cmp_start[:, None] + CMP_LEN, slc_start[None, :] + SLC_BLOCK)
                 - np.maximum(cmp_start[:, None], slc_start[None, :]), 0, None) / CMP_STRIDE
    ov[nc - CMP_LEN // CMP_STRIDE + 1:, :] = 0.0
    ov[:, s // SLC_BLOCK:] = 0.0
    return jnp.asarray(ov.T, BF16)


def _nsa_attention(qnt, kc, vct, ks, vst, kw, vwt, gates):
    s = qnt.shape[1]
    nc = s // CMP_STRIDE
    n_slc = s // SLC_BLOCK
    nslc_pad = -(-n_slc // LANE) * LANE
    tq = Q_TILE_NSA
    cols = NSA_HPG * tq
    once = pl.Buffered(1)
    res = pl.BlockSpec((s, LANE), lambda g, i: (0, g), pipeline_mode=once)
    return pl.pallas_call(
        functools.partial(_nsa_kernel, n_sel=min(SLC_TOPK, n_slc)),
        grid=(NSA_GROUPS, s // tq),
        in_specs=[pl.BlockSpec((NSA_HPG * LANE, tq), lambda g, i: (g, i)),
                  pl.BlockSpec((1, nc, LANE), lambda g, i: (g, 0, 0), pipeline_mode=once),
                  pl.BlockSpec((1, LANE, nc), lambda g, i: (g, 0, 0), pipeline_mode=once),
                  res, pl.BlockSpec((1, s // KV_TILE, V_ROWS, KV_TILE), lambda g, i: (g, 0, 0, 0),
                                    pipeline_mode=once),
                  res, pl.BlockSpec((1, s // WIN_TILE, V_ROWS, WIN_TILE), lambda g, i: (g, 0, 0, 0),
                                    pipeline_mode=once),
                  pl.BlockSpec((1, GATE_ROWS, tq), lambda g, i: (g, 0, i)),
                  pl.BlockSpec((nslc_pad, nc), lambda g, i: (0, 0), pipeline_mode=once)],
        out_specs=pl.BlockSpec((tq, NSA_HPG * LANE), lambda g, i: (i, g)),
        out_shape=jax.ShapeDtypeStruct((s, NSA_HEADS * LANE), BF16),
        scratch_shapes=[pltpu.VMEM((nslc_pad // SUPER_BLOCKS, LANE, cols), BF16),
                        pltpu.VMEM((V_ROWS, cols), F32), pltpu.VMEM((V_ROWS, cols), F32)],
        compiler_params=_cparams("parallel", "arbitrary"),
        name="nsa_attention",
    )(qnt, kc, vct, ks, vst, kw, vwt, gates, _overlap_t(s, nslc_pad))


def _flash_kernel(cfirst_ref, clast_ref, slack_ref, qt_ref, k_ref, vt_ref, o_ref, acc_ref, *, tq, decay):
    h = pl.program_id(0)
    i = pl.program_id(1)
    tk = KV_TILE
    n_tiles = k_ref.shape[0] // tk
    acc_ref[...] = jnp.zeros(acc_ref.shape, F32)

    n_strips = tq // Q_STRIP
    per_q = tq // tk
    qas = [qt_ref[:, c * Q_STRIP:(c + 1) * Q_STRIP] for c in range(n_strips)]

    def tile_steps(j, d=None):
        k0 = pl.multiple_of(j * tk, tk)
        k_tile, vt = k_ref[pl.ds(k0, tk), :], vt_ref[0, j]
        steps = []
        for c in range(n_strips):
            mask, rows = None, tk
            if d is not None:
                if d * tk > (c + 1) * Q_STRIP - 1:
                    continue
                if (d + 1) * tk - 1 > c * Q_STRIP:
                    rows = min(tk, (c + 1) * Q_STRIP - d * tk)
                    shp = (rows, Q_STRIP)
                    mask = _row_iota(shp) + d * tk <= _lane_iota(shp) + c * Q_STRIP
            steps.append((k_tile[:rows], qas[c], vt[:, :rows], c, mask))
        return steps

    def any_tile_masks(j):
        shp = (tk, Q_STRIP)
        return [_row_iota(shp) + j * tk <= _lane_iota(shp) + (i * tq + c * Q_STRIP) for c in range(n_strips)]

    def below(t, flat):
        jj = i - 1 - t
        bound = (slack_ref[0] + cfirst_ref[h * n_tiles + i * per_q]
                 - clast_ref[h * n_tiles + jj * per_q + per_q - 1])

        def run(flat):
            steps = [st for u in range(per_q) for st in tile_steps(jj * per_q + (per_q - 1 - u))]
            return _flat(_fast_steps(steps, _nest(flat), acc_ref))

        return lax.cond(bound >= -SKIP_MARGIN, run, lambda flat: flat, flat)

    diag0 = i * per_q
    own = [(c * Q_STRIP) // tk for c in range(n_strips)]
    head = tk if decay else 16

    def start_max(c):
        j = diag0 + own[c]
        keys = k_ref[pl.ds(pl.multiple_of(j * tk, tk), head), :]
        shp = (head, Q_STRIP)
        mask = _row_iota(shp) + j * tk <= _lane_iota(shp) + (i * tq + c * Q_STRIP)
        return _first_tile_max(keys, [qas[c]], [mask])[0]

    m0 = tuple(start_max(c) for c in range(n_strips))
    steps = [st for d in reversed(range(per_q)) for st in tile_steps(diag0 + d, d)]
    state = _fast_steps(steps, tuple((m, jnp.zeros_like(m)) for m in m0), acc_ref)
    state = _nest(lax.fori_loop(0, i, below, _flat(state)))
    worst = jnp.max(jnp.concatenate([w for _, w in state], axis=1))

    @pl.when(worst > EXP_GUARD)
    def _():
        acc_ref[...] = jnp.zeros(acc_ref.shape, F32)

        def exact(j, ms):
            k0 = pl.multiple_of(j * tk, tk)
            k_tile, vt, masks = k_ref[pl.ds(k0, tk), :], vt_ref[0, j], any_tile_masks(j)
            return _online_steps([(k_tile, qas[c], vt, c, masks[c]) for c in range(n_strips)], ms, acc_ref)

        lax.fori_loop(0, (i + 1) * per_q, exact, _m_init(n_strips))

    o = _pad_feature_rows(acc_ref[:HEAD_DIM] / acc_ref[SUM_ROW:SUM_ROW + 1])
    for c0 in range(0, tq, LANE):
        o_ref[c0:c0 + LANE, :] = o[:, c0:c0 + LANE].T.astype(BF16)


def _causal_attention(qt, k, vt, tq, bias_edges=None, slack=None):
    s, width = k.shape
    heads = width // LANE
    tq = min(tq, s)
    assert tq % KV_TILE == 0 and s % tq == 0
    n_tiles = s // KV_TILE
    if bias_edges is None:
        first = last = jnp.zeros((heads * n_tiles,), F32)
        slack = jnp.full((1,), -NEG, F32)
    else:
        first, last = (e.reshape(heads * n_tiles).astype(F32) for e in bias_edges)
    grid_spec = pltpu.PrefetchScalarGridSpec(
        num_scalar_prefetch=3,
        grid=(heads, s // tq),
        in_specs=[pl.BlockSpec((LANE, tq), lambda h, i, *_: (h, i)),
                  pl.BlockSpec((s, LANE), lambda h, i, *_: (0, h)),
                  pl.BlockSpec((1, n_tiles, V_ROWS, KV_TILE), lambda h, i, *_: (h, 0, 0, 0))],
        out_specs=pl.BlockSpec((tq, LANE), lambda h, i, *_: (i, h)),
        scratch_shapes=[pltpu.VMEM((V_ROWS, tq), F32)],
    )
    return pl.pallas_call(
        functools.partial(_flash_kernel, tq=tq, decay=bias_edges is not None),
        grid_spec=grid_spec,
        out_shape=jax.ShapeDtypeStruct((s, width), BF16),
        compiler_params=_cparams("parallel", "arbitrary"),
        name="causal_attention",
    )(first, last, slack.astype(F32), qt, k, vt)


def _out_proj_kernel(oa_ref, ob_ref, wa_ref, wb_ref, x_ref, g_ref, o_ref):
    y = jnp.dot(oa_ref[...], wa_ref[...], preferred_element_type=F32)
    y = y + jnp.dot(ob_ref[...], wb_ref[...], preferred_element_type=F32)
    o_ref[...] = x_ref[...] + g_ref[...] * y


def _out_proj(oa, ob, cola, colb, wa, wb, x2, gate):
    s, d = x2.shape
    ka = wa.shape[0]
    tm = ROW_TILE
    return pl.pallas_call(
        _out_proj_kernel,
        grid=(s // tm,),
        in_specs=[pl.BlockSpec((tm, ka), lambda i: (i, cola)), pl.BlockSpec((tm, ka), lambda i: (i, colb)),
                  pl.BlockSpec((ka, d), lambda i: (0, 0)), pl.BlockSpec((ka, d), lambda i: (0, 0)),
                  pl.BlockSpec((tm, d), lambda i: (i, 0)), pl.BlockSpec((1, d), lambda i: (0, 0))],
        out_specs=pl.BlockSpec((tm, d), lambda i: (i, 0)),
        out_shape=jax.ShapeDtypeStruct((s, d), F32),
        compiler_params=_cparams("parallel"),
        name="out_proj",
    )(oa, ob, wa, wb, x2, gate)


def _mla_prep_kernel(x_ref, ng_ref, sc_ref, sh_ref, win_ref, pos_ref, inv_ref, gqa_ref, gkva_ref, wuq_ref, wuk_ref,
                     wuv_ref, gq_ref, gk_ref, gkr_ref, qt_ref, k_ref, vt_ref):
    shp = (PREP_TILE, LANE)
    act = _norm_mod(x_ref[...], ng_ref[...], sc_ref[...], sh_ref[...]).astype(BF16)
    proj = jnp.dot(act, win_ref[...], preferred_element_type=F32)
    lane = _lane_iota(shp)
    nope = lane < QK_NOPE
    rope = (lane >= QK_NOPE) & (lane < QK_NOPE + QK_ROPE)
    ref_ones = jnp.where((lane >= REF_ROW) & (lane < REF_ROW + 3), 1.0, 0.0)
    ang = pos_ref[...] * inv_ref[...]
    cos = jnp.where(rope, jnp.cos(ang), 1.0)
    sin = jnp.where(rope, jnp.sin(ang), 0.0)

    def rope32(x):
        half = QK_ROPE // 2
        rot = jnp.where(lane < QK_NOPE + half, -pltpu.roll(x, LANE - half, 1), pltpu.roll(x, half, 1))
        return x * cos + rot * sin

    def low_rank_norm(x, g):
        ms = jnp.mean(x * x, axis=-1, keepdims=True)
        return (x * lax.rsqrt(ms + EPS) * g).astype(BF16)

    nq = Q_LORA // LANE
    cq = low_rank_norm(proj[:, :Q_LORA], gqa_ref[...])
    ckv = low_rank_norm(proj[:, Q_LORA:Q_LORA + KV_LORA], gkva_ref[...])
    kr = proj[:, (nq + KV_LORA // LANE) * LANE:(nq + KV_LORA // LANE + 1) * LANE]
    k_rope = rope32(_head_rms(kr, gkr_ref[...], QK_ROPE))

    gq, gk = gq_ref[...], gk_ref[...]
    scale = (QK_NOPE + QK_ROPE) ** -0.5 * LOG2E
    pair = 2 * LANE
    for hp in range(MLA_HEADS // 2):
        cols = slice(hp * pair, (hp + 1) * pair)
        q2 = jnp.dot(cq, wuq_ref[:, cols], preferred_element_type=F32)
        k2 = jnp.dot(ckv, wuk_ref[:, cols], preferred_element_type=F32)
        v2 = jnp.dot(ckv, wuv_ref[:, cols], preferred_element_type=F32)
        for sub in range(2):
            head = 2 * hp + sub
            sl = slice(head * LANE, (head + 1) * LANE)
            half = slice(sub * LANE, (sub + 1) * LANE)
            x = q2[:, half]
            ss_n = jnp.sum(jnp.where(nope, x * x, 0.0), axis=-1, keepdims=True)
            ss_r = jnp.sum(jnp.where(rope, x * x, 0.0), axis=-1, keepdims=True)
            inv_rms = jnp.where(nope, lax.rsqrt(ss_n * (1.0 / QK_NOPE) + EPS),
                                lax.rsqrt(ss_r * (1.0 / QK_ROPE) + EPS))
            qt_ref[sl, :] = _t_bf16(rope32(x * inv_rms * gq) * scale)
            kn = _head_rms(k2[:, half], gk, QK_NOPE)
            k_ref[:, sl] = (kn + k_rope + ref_ones).astype(BF16)
            vt_ref[head, 0] = _t_bf16(jnp.where(lane == SUM_ROW, 1.0, v2[:, half]), V_ROWS)


def _mla_prep(x2, norm_g, sc, sh, w_in, posf, inv128, gqa, gkva, wuq, wuk, wuv, gq, gk, gkr):
    s, d = x2.shape
    tm = PREP_TILE
    assert tm == KV_TILE

    def full(a):
        return pl.BlockSpec(a.shape, lambda i: (0, 0))

    outs = [_feat_major(MLA_HEADS, s, tm),
            (pl.BlockSpec((tm, MLA_HEADS * LANE), lambda i: (i, 0)),
             jax.ShapeDtypeStruct((s, MLA_HEADS * LANE), BF16)),
            _value_tiles(MLA_HEADS, s, tm, KV_TILE, V_ROWS)]
    head = (norm_g, sc, sh, w_in)
    tail = (inv128, gqa, gkva, wuq, wuk, wuv, gq, gk, gkr)
    return pl.pallas_call(
        _mla_prep_kernel,
        grid=(s // tm,),
        in_specs=[pl.BlockSpec((tm, d), lambda i: (i, 0))] + [full(a) for a in head]
                 + [pl.BlockSpec((tm, 1), lambda i: (i, 0))] + [full(a) for a in tail],
        out_specs=[o[0] for o in outs],
        out_shape=[o[1] for o in outs],
        compiler_params=_cparams("parallel"),
        name="mla_prep",
    )(x2, *head, posf, *tail)


def _rank_lt(v, k):
    n = v.shape[0]
    row = _row_iota(v.shape)
    rank = jnp.zeros(v.shape, F32)
    for b in range(n):
        vb = v[b:b + 1, :]
        rank = rank + jnp.where((vb > v) | ((vb == v) & (row > b)), 1.0, 0.0)
    return rank < k


def _top_rows(v, k):
    rowf = _row_iota(v.shape).astype(F32)
    chosen = jnp.zeros(v.shape, F32)
    for _ in range(k):
        mx = jnp.max(v, axis=0, keepdims=True)
        idx = jnp.min(jnp.where(v == mx, rowf, float(v.shape[0])), axis=0, keepdims=True)
        hit = rowf == idx
        chosen = jnp.where(hit, 1.0, chosen)
        v = jnp.where(hit, -jnp.inf, v)
    return chosen > 0.0


def _moe_route_kernel(x_ref, g_ref, sc_ref, sh_ref, wr_ref, rb_ref, h_ref, pos_ref, wt_ref, cnt_ref):
    tm = ROW_TILE
    h = _norm_mod(x_ref[...], g_ref[...], sc_ref[...], sh_ref[...])
    h_ref[...] = h.astype(BF16)
    logits = jnp.dot(h, wr_ref[...], precision=HIGHEST, preferred_element_type=F32)
    lt = logits.T[:N_EXPERTS]
    scores = jax.nn.sigmoid(lt)
    sel = scores + rb_ref[...]

    per = N_EXPERTS // N_GROUPS
    grp = sel.reshape(N_GROUPS, per, tm)
    sub = lax.broadcasted_iota(jnp.int32, grp.shape, 1)
    m1 = jnp.max(grp, axis=1, keepdims=True)
    first = jnp.min(jnp.where(grp == m1, sub, per), axis=1, keepdims=True)
    m2 = jnp.max(jnp.where(sub == first, -jnp.inf, grp), axis=1, keepdims=True)
    gscore = (m1 + m2).reshape(N_GROUPS, tm)
    gmask = _rank_lt(gscore, TOPK_GROUPS)
    emask = jnp.broadcast_to(gmask.reshape(N_GROUPS, 1, tm), grp.shape).reshape(N_EXPERTS, tm)
    chosen = _top_rows(jnp.where(emask, sel, NEG), TOP_K)

    w = jnp.where(chosen, scores, 0.0)
    wt_ref[...] = w / jnp.sum(w, axis=0, keepdims=True) * ROUTED_SCALE

    upper = jnp.where(_row_iota((tm, tm)) <= _lane_iota((tm, tm)), 1.0, 0.0).astype(BF16)
    incl = jnp.dot(jnp.where(chosen, 1.0, 0.0).astype(BF16), upper, preferred_element_type=F32)
    pos_ref[...] = jnp.where(chosen, incl - 1.0, -1.0)
    cnt_ref[0] = jnp.broadcast_to(incl[:, tm - 1:tm], (N_EXPERTS, LANE))


def _moe_route(x2, g, sc, sh, w_router_pad, router_bias_col):
    s, d = x2.shape
    tm = ROW_TILE
    vec = pl.BlockSpec((1, d), lambda i: (0, 0))
    et = pl.BlockSpec((N_EXPERTS, tm), lambda i: (0, i))
    return pl.pallas_call(
        _moe_route_kernel,
        grid=(s // tm,),
        in_specs=[pl.BlockSpec((tm, d), lambda i: (i, 0)), vec, vec, vec,
                  pl.BlockSpec((d, LANE), lambda i: (0, 0)),
                  pl.BlockSpec((N_EXPERTS, 1), lambda i: (0, 0))],
        out_specs=[pl.BlockSpec((tm, d), lambda i: (i, 0)), et, et,
                   pl.BlockSpec((1, N_EXPERTS, LANE), lambda i: (i, 0, 0))],
        out_shape=[jax.ShapeDtypeStruct((s, d), BF16), jax.ShapeDtypeStruct((N_EXPERTS, s), F32),
                   jax.ShapeDtypeStruct((N_EXPERTS, s), F32),
                   jax.ShapeDtypeStruct((s // tm, N_EXPERTS, LANE), F32)],
        compiler_params=_cparams("parallel"),
        name="moe_route",
    )(x2, g, sc, sh, w_router_pad, router_bias_col)


def _moe_kernel(cnt_ref, x_ref, h_ref, pos_ref, wt_ref, wg_ref, wu_ref, wd_ref, sg_ref, su_ref, sd_ref,
                g2_ref, o_ref, acc_ref):
    i = pl.program_id(0)
    e = pl.program_id(1)
    tm = ROW_TILE
    r = MOE_CHUNK

    @pl.when(e == 0)
    def _():
        h = h_ref[...]
        a = jnp.dot(h, sg_ref[...], preferred_element_type=F32)
        a = a * jax.nn.sigmoid(a) * jnp.dot(h, su_ref[...], preferred_element_type=F32)
        acc_ref[...] = jnp.dot(a.astype(BF16), sd_ref[...], preferred_element_type=F32)

    first = e * MOE_EXPERTS_PER_STEP
    n = cnt_ref[i * N_EXPERTS + first]
    for k in range(1, MOE_EXPERTS_PER_STEP):
        n = jnp.maximum(n, cnt_ref[i * N_EXPERTS + first + k])
    prows = [pos_ref[pl.ds(first + k, 1), :] for k in range(MOE_EXPERTS_PER_STEP)]
    wrows = [wt_ref[pl.ds(first + k, 1), :] for k in range(MOE_EXPERTS_PER_STEP)]

    def chunk(c, _):
        slot = (_row_iota((r, tm)) + c * r).astype(F32)
        hits = [prow == slot for prow in prows]
        onehot = jnp.concatenate([jnp.where(hit, 1.0, 0.0).astype(BF16) for hit in hits], axis=0)
        xg = jnp.dot(onehot, h_ref[...], preferred_element_type=F32).astype(BF16)
        ys = []
        for k in range(MOE_EXPERTS_PER_STEP):
            xk = xg[k * r:(k + 1) * r]
            a = jnp.dot(xk, wg_ref[k], preferred_element_type=F32)
            a = a * jax.nn.sigmoid(a) * jnp.dot(xk, wu_ref[k], preferred_element_type=F32)
            y = jnp.dot(a.astype(BF16), wd_ref[k], preferred_element_type=F32)
            wr = jnp.sum(jnp.where(hits[k], wrows[k], 0.0), axis=-1, keepdims=True)
            ys.append((y * wr).astype(BF16))
        acc_ref[...] += _dot_tn(onehot, jnp.concatenate(ys, axis=0))
        return 0

    lax.fori_loop(0, (n + r - 1) // r, chunk, 0)

    @pl.when(e == N_EXPERTS // MOE_EXPERTS_PER_STEP - 1)
    def _():
        o_ref[...] = x_ref[...] + g2_ref[...] * acc_ref[...]


def _moe_experts(counts, x2, h, pos_t, w_t, wg, wu, wd, sg, su, sd, g2):
    s, d = x2.shape
    tm = ROW_TILE
    ff = wg.shape[2]
    tile = pl.BlockSpec((tm, d), lambda i, e, c: (i, 0))
    et = pl.BlockSpec((N_EXPERTS, tm), lambda i, e, c: (0, i))

    def const(a):
        return pl.BlockSpec(a.shape, lambda i, e, c: (0,) * a.ndim)

    per = MOE_EXPERTS_PER_STEP
    grid_spec = pltpu.PrefetchScalarGridSpec(
        num_scalar_prefetch=1,
        grid=(s // tm, N_EXPERTS // per),
        in_specs=[tile, tile, et, et,
                  pl.BlockSpec((per, d, ff), lambda i, e, c: (e, 0, 0)),
                  pl.BlockSpec((per, d, ff), lambda i, e, c: (e, 0, 0)),
                  pl.BlockSpec((per, ff, d), lambda i, e, c: (e, 0, 0)),
                  const(sg), const(su), const(sd), const(g2)],
        out_specs=tile,
        scratch_shapes=[pltpu.VMEM((tm, d), F32)],
    )
    return pl.pallas_call(
        _moe_kernel,
        grid_spec=grid_spec,
        out_shape=jax.ShapeDtypeStruct((s, d), F32),
        compiler_params=_cparams("parallel", "arbitrary"),
        name="moe_experts",
    )(counts, x2, h, pos_t, w_t, wg, wu, wd, sg, su, sd, g2)


def _pad_lanes(v, width=LANE, offset=0):
    out = jnp.zeros((1, width), F32)
    return out.at[0, offset:offset + v.shape[0]].set(v.astype(F32))


def _head_cols(w, n_heads, dim):
    d = w.shape[0]
    w3 = w.reshape(d, n_heads, dim)
    return jnp.pad(w3, ((0, 0), (0, 0), (0, LANE - dim))).reshape(d, n_heads * LANE)


def _hybrid_w_in(w_in):
    d = w_in.shape[0]
    nq = NSA_HEADS * HEAD_DIM
    nkv = 6 * NSA_GROUPS * HEAD_DIM
    ng = 3 * NSA_HEADS
    nf = 3 * FOX_HEADS * HEAD_DIM
    c0, c1, c2, c3 = nq, nq + nkv, nq + nkv + ng, nq + nkv + ng + nf
    gates = w_in[:, c1:c2].reshape(d, NSA_GROUPS, 3 * NSA_HPG)
    gates = jnp.pad(gates, ((0, 0), (0, 0), (0, LANE - 3 * NSA_HPG))).reshape(d, NSA_GROUPS * LANE)
    ff = jnp.pad(w_in[:, c3:], ((0, 0), (0, 2 * LANE - FOX_HEADS)))
    return jnp.concatenate([
        _head_cols(w_in[:, :c0], NSA_HEADS, HEAD_DIM),
        _head_cols(w_in[:, c0:c1], 6 * NSA_GROUPS, HEAD_DIM),
        _head_cols(w_in[:, c2:c3], 3 * FOX_HEADS, HEAD_DIM),
        gates, ff], axis=1).astype(BF16)


def _pad_head_rows(w, n_heads, dim):
    d = w.shape[1]
    w3 = w.reshape(n_heads, dim, d)
    return jnp.pad(w3, ((0, 0), (0, LANE - dim), (0, 0))).reshape(n_heads * LANE, d).astype(BF16)


def _rope_inv(dim, offset):
    inv = ROPE_THETA ** (-jnp.arange(0, dim, 2, dtype=F32) / dim)
    return _pad_lanes(jnp.concatenate([inv, inv]), offset=offset)


def _hybrid_mixer(x2, posf, mods, norm_g, w_in, fox_f_bias, nsa_q_norm, nsa_k_norm, nsa_cmp_pe, nsa_w_cmp,
                  fox_q_norm, fox_k_norm, w_out):
    sh1, sc1, g1 = mods
    (qnt, kct, vct, ks, vst, kw, vwt, gates, fqt, fk, fvt, cedge) = _hy_prep(
        x2, norm_g, sc1, sh1, _hybrid_w_in(w_in), posf, _rope_inv(HEAD_DIM, 0), _pad_lanes(nsa_q_norm),
        _pad_lanes(nsa_k_norm),
        _pad_lanes(fox_q_norm), _pad_lanes(fox_k_norm), _pad_lanes(fox_f_bias))
    kc, vc_t = _compress(kct, vct, nsa_w_cmp, nsa_cmp_pe, nsa_k_norm)
    o_a = _nsa_attention(qnt, kc, vc_t, ks, vst, kw, vwt, gates)
    slack = (2.0 * HEAD_DIM ** 0.5 * LOG2E) * jnp.max(jnp.abs(fox_q_norm)) * jnp.max(jnp.abs(fox_k_norm))
    edges = (cedge[:, 0, :FOX_HEADS].T, cedge[:, 1, :FOX_HEADS].T)
    o_b = _causal_attention(fqt, fk, fvt, Q_TILE_FOX, edges, slack.reshape(1))
    half = NSA_HEADS * HEAD_DIM
    wa = _pad_head_rows(w_out[:half], NSA_HEADS, HEAD_DIM)
    wb = _pad_head_rows(w_out[half:], FOX_HEADS, HEAD_DIM)
    return _out_proj(o_a, o_b, 0, 0, wa, wb, x2, g1)


def _mla_mixer(x2, posf, mods, norm_g, w_in, q_a_norm, kv_a_norm, w_uq, w_ukv, qn_norm, kn_norm, qr_norm,
               kr_norm, w_out):
    sh1, sc1, g1 = mods
    d = x2.shape[1]
    w_kr = jnp.zeros((d, LANE), F32).at[:, QK_NOPE:QK_NOPE + QK_ROPE].set(w_in[:, Q_LORA + KV_LORA:])
    w_in_p = jnp.concatenate([w_in[:, :Q_LORA + KV_LORA], w_kr], axis=1).astype(BF16)
    hq = QK_NOPE + QK_ROPE
    wuq = _head_cols(w_uq, MLA_HEADS, hq).astype(BF16)
    wkv3 = w_ukv.reshape(KV_LORA, MLA_HEADS, QK_NOPE + V_HEAD)
    wuk = _head_cols(wkv3[:, :, :QK_NOPE].reshape(KV_LORA, -1), MLA_HEADS, QK_NOPE).astype(BF16)
    wuv = _head_cols(wkv3[:, :, QK_NOPE:].reshape(KV_LORA, -1), MLA_HEADS, V_HEAD).astype(BF16)
    gq = _pad_lanes(jnp.concatenate([qn_norm, qr_norm]))
    qt, k, vt = _mla_prep(x2, norm_g, sc1, sh1, w_in_p, posf, _rope_inv(QK_ROPE, QK_NOPE),
                          q_a_norm.reshape(1, -1).astype(F32),
                          kv_a_norm.reshape(1, -1).astype(F32), wuq, wuk, wuv, gq, _pad_lanes(kn_norm),
                          _pad_lanes(kr_norm, offset=QK_NOPE))
    o = _causal_attention(qt, k, vt, Q_TILE_MLA)
    w_pad = _pad_head_rows(w_out, MLA_HEADS, V_HEAD)
    half = w_pad.shape[0] // 2
    return _out_proj(o, o, 0, 1, w_pad[:half], w_pad[half:], x2, g1)


def _moe_ffn(x2, mods, norm_g, w_router, router_bias, w_gate, w_up, w_down, ws_gate, ws_up, ws_down):
    sh2, sc2, g2 = mods
    w_r = jnp.pad(w_router.astype(F32), ((0, 0), (0, LANE - N_EXPERTS)))
    h, pos_t, w_t, cnt = _moe_route(x2, norm_g, sc2, sh2, w_r, router_bias.reshape(N_EXPERTS, 1).astype(F32))
    counts = cnt[:, :, 0].astype(jnp.int32).reshape(-1)
    return _moe_experts(counts, x2, h, pos_t, w_t, w_gate.astype(BF16), w_up.astype(BF16),
                        w_down.astype(BF16), ws_gate.astype(BF16), ws_up.astype(BF16), ws_down.astype(BF16), g2)


def kernel(x, c, positions, norm_attn, norm_ffn, w_ada, b_ada, hy_w_in, fox_f_bias, nsa_q_norm, nsa_k_norm, nsa_cmp_pe, nsa_w_cmp, fox_q_norm, fox_k_norm, hy_w_out, mla_w_in, mla_q_a_norm, mla_kv_a_norm, mla_w_uq, mla_w_ukv, mla_qn_norm, mla_kn_norm, mla_qr_norm, mla_kr_norm, mla_w_out, moe_w_router, moe_router_bias, moe_w_gate, moe_w_up, moe_w_down, moe_ws_gate, moe_ws_up, moe_ws_down):
    b, s, d = x.shape
    assert b == 1 and s % KV_TILE == 0 and s >= WINDOW + Q_TILE_NSA
    depth = w_ada.shape[0]
    x2 = x.reshape(s, d).astype(F32)
    posf = positions.reshape(s, 1).astype(F32)
    mod = _ada_mod(c.astype(F32), w_ada.astype(F32), b_ada.astype(F32))

    for layer in range(depth):
        m = [mod[layer, :, k * d:(k + 1) * d] for k in range(6)]
        i = layer // 2
        g_attn = norm_attn[layer].reshape(1, d).astype(F32)
        if layer % 2 == 0:
            x2 = _hybrid_mixer(x2, posf, m[0:3], g_attn, hy_w_in[i], fox_f_bias[i], nsa_q_norm[i],
                               nsa_k_norm[i], nsa_cmp_pe[i], nsa_w_cmp[i], fox_q_norm[i], fox_k_norm[i],
                               hy_w_out[i])
        else:
            x2 = _mla_mixer(x2, posf, m[0:3], g_attn, mla_w_in[i], mla_q_a_norm[i], mla_kv_a_norm[i],
                            mla_w_uq[i], mla_w_ukv[i], mla_qn_norm[i], mla_kn_norm[i], mla_qr_norm[i],
                            mla_kr_norm[i], mla_w_out[i])
        x2 = _moe_ffn(x2, m[3:6], norm_ffn[layer].reshape(1, d).astype(F32), moe_w_router[layer],
                      moe_router_bias[layer], moe_w_gate[layer], moe_w_up[layer], moe_w_down[layer],
                      moe_ws_gate[layer], moe_ws_up[layer], moe_ws_down[layer])
    return x2.reshape(b, s, d)
```

```python
import functools

import numpy as np
import jax
import jax.numpy as jnp
from jax import lax
from jax.experimental import pallas as pl
from jax.experimental.pallas import tpu as pltpu

F32 = jnp.float32
BF16 = jnp.bfloat16
HIGHEST = lax.Precision.HIGHEST

LANE = 128
VMEM_LIMIT_BYTES = 56 * 1024 * 1024

HEAD_DIM = 64
NSA_HEADS = 8
NSA_GROUPS = 2
NSA_HPG = NSA_HEADS // NSA_GROUPS
CMP_LEN = 32
CMP_STRIDE = 16
SLC_BLOCK = 64
SLC_TOPK = 16
WINDOW = 512
FOX_HEADS = 8
MLA_HEADS = 16
Q_LORA = 384
KV_LORA = 256
QK_NOPE = 64
QK_ROPE = 32
V_HEAD = 64
N_EXPERTS = 64
TOP_K = 8
N_GROUPS = 8
TOPK_GROUPS = 4
EXPERT_FF = 256
ROUTED_SCALE = 2.5
ROPE_THETA = 10000.0
EPS = 1e-6
NEG = -1e30
BIG = 1e6

ROW_TILE = 512
PREP_TILE = 512
GATE_ROWS = 16
Q_TILE_NSA = 256
KV_TILE = 512
Q_TILE_FOX = 1024
Q_TILE_MLA = 2048
Q_STRIP = 256
WIN_TILE = 256
SUPER_BLOCKS = 32
NSA_PREFIX_PARTS = 8
SUM_ROW = 64
V_ROWS = 80
REF_ROW = 104
REF_SLAB = 96
EXP_GUARD = 100.0
SCORE_LOOKAHEAD = 4
SKIP_MARGIN = 160.0
LOG2E = 1.4426950408889634
MOE_CHUNK = 128
MOE_EXPERTS_PER_STEP = 8

HY_Q0 = 0
HY_KV0 = HY_Q0 + NSA_HEADS
HY_F0 = HY_KV0 + 6 * NSA_GROUPS
HY_G0 = HY_F0 + 3 * FOX_HEADS
HY_FF = HY_G0 + NSA_GROUPS
HY_BLOCKS = HY_FF + 2


def _cparams(*sem):
    return pltpu.CompilerParams(dimension_semantics=sem, vmem_limit_bytes=VMEM_LIMIT_BYTES)


def _lane_iota(shape):
    return lax.broadcasted_iota(jnp.int32, shape, len(shape) - 1)


def _row_iota(shape):
    return lax.broadcasted_iota(jnp.int32, shape, len(shape) - 2)


def _dot_tn(a, b):
    return lax.dot_general(a, b, (((0,), (0,)), ((), ())), preferred_element_type=F32)


def _ada_kernel(c_ref, w_ref, b_ref, o_ref):
    c = c_ref[...]
    cond = c * jax.nn.sigmoid(c)
    o_ref[0] = jnp.dot(cond, w_ref[0], precision=HIGHEST, preferred_element_type=F32) + b_ref[0]


def _ada_mod(c, w_ada, b_ada):
    depth, d, n = w_ada.shape
    tn = 768
    c8 = jnp.broadcast_to(c.reshape(1, d), (8, d))
    out = pl.pallas_call(
        _ada_kernel,
        grid=(depth, n // tn),
        in_specs=[pl.BlockSpec((8, d), lambda l, j: (0, 0)),
                  pl.BlockSpec((1, d, tn), lambda l, j: (l, 0, j)),
                  pl.BlockSpec((1, 1, tn), lambda l, j: (l, 0, j))],
        out_specs=pl.BlockSpec((1, 8, tn), lambda l, j: (l, 0, j)),
        out_shape=jax.ShapeDtypeStruct((depth, 8, n), F32),
        compiler_params=_cparams("parallel", "parallel"),
        name="ada_mod",
    )(c8, w_ada, b_ada.reshape(depth, 1, n))
    return out[:, 0:1, :]


def _norm_mod(x, g, sc, sh):
    ms = jnp.mean(x * x, axis=-1, keepdims=True)
    return (x * lax.rsqrt(ms + EPS) * g) * (1.0 + sc) + sh


def _head_rms(x, gain, n_real):
    ss = jnp.sum(x * x, axis=-1, keepdims=True)
    return x * lax.rsqrt(ss * (1.0 / n_real) + EPS) * gain


def _rope64(x, cos, sin):
    lane = _lane_iota(x.shape)
    rot = jnp.where(lane < 32, -pltpu.roll(x, LANE - 32, 1), pltpu.roll(x, 32, 1))
    return x * cos + rot * sin


def _t_bf16(x, rows=LANE):
    return x.T[:rows].astype(BF16)


def _split3(c):
    hi = c.astype(BF16).astype(F32)
    r1 = c - hi
    mid = r1.astype(BF16).astype(F32)
    lo = (r1 - mid).astype(BF16).astype(F32)
    return hi, mid, lo


def _hy_prep_kernel(x_ref, ng_ref, sc_ref, sh_ref, w_ref, pos_ref, inv_ref, gq_ref, gk_ref, gfq_ref, gfk_ref,
                    fb_ref, qnt_ref, kct_ref, vct_ref, ks_ref, vst_ref, kw_ref, vwt_ref, gate_ref,
                    fqt_ref, fk_ref, fvt_ref, cedge_ref, carry_ref):
    i = pl.program_id(0)
    tm = PREP_TILE
    shp = (tm, LANE)
    lane = _lane_iota(shp)

    act = _norm_mod(x_ref[...], ng_ref[...], sc_ref[...], sh_ref[...]).astype(BF16)
    pairs = {}

    def blk(b):
        if b // 2 not in pairs:
            cols = slice((b // 2) * 2 * LANE, (b // 2 + 1) * 2 * LANE)
            pairs[b // 2] = jnp.dot(act, w_ref[:, cols], preferred_element_type=F32)
        return pairs[b // 2][:, (b % 2) * LANE:(b % 2 + 1) * LANE]

    ang = pos_ref[...] * inv_ref[...]
    real = lane < HEAD_DIM
    cos = jnp.where(real, jnp.cos(ang), 1.0)
    sin = jnp.where(real, jnp.sin(ang), 0.0)
    gq, gk, gfq, gfk = gq_ref[...], gk_ref[...], gfq_ref[...], gfk_ref[...]
    scale = HEAD_DIM ** -0.5 * LOG2E
    ones_row = lane == SUM_ROW
    ref_ones = jnp.where((lane >= REF_ROW) & (lane < REF_ROW + 3), 1.0, 0.0)

    for h in range(NSA_HEADS):
        q = _rope64(_head_rms(blk(HY_Q0 + h), gq, HEAD_DIM), cos, sin) * scale
        qnt_ref[h * LANE:(h + 1) * LANE, :] = _t_bf16(q)

    row = _row_iota(shp) + i * tm
    onehot = jnp.where(lane - HEAD_DIM == ((row // SLC_BLOCK) % SUPER_BLOCKS), 1.0, 0.0)
    for g in range(NSA_GROUPS):
        def kv(r):
            return blk(HY_KV0 + r * NSA_GROUPS + g)
        sl = slice(g * LANE, (g + 1) * LANE)
        kct_ref[g] = _rope64(kv(0), cos, sin)[:, :HEAD_DIM].astype(BF16)
        vct_ref[g] = kv(1)[:, :HEAD_DIM].astype(BF16)
        ks = _rope64(_head_rms(kv(2), gk, HEAD_DIM), cos, sin)
        ks_ref[:, sl] = (ks + onehot + ref_ones).astype(BF16)
        vst_ref[g, 0] = _t_bf16(jnp.where(ones_row, 1.0, kv(3)), V_ROWS)
        kw_ref[:, sl] = (_rope64(_head_rms(kv(4), gk, HEAD_DIM), cos, sin) + ref_ones).astype(BF16)
        vwt = _t_bf16(jnp.where(ones_row, 1.0, kv(5)), V_ROWS)
        for cidx in range(tm // WIN_TILE):
            vwt_ref[g, cidx] = vwt[:, cidx * WIN_TILE:(cidx + 1) * WIN_TILE]
        gate_ref[g] = jax.nn.sigmoid(blk(HY_G0 + g)).T[:GATE_ROWS]

    @pl.when(i == 0)
    def _():
        carry_ref[...] = jnp.zeros_like(carry_ref)

    z = blk(HY_FF) + fb_ref[...]
    logf = jnp.minimum(z, 0.0) - jnp.log1p(jnp.exp(-jnp.abs(z)))
    tri = jnp.where(_row_iota((tm, tm)) >= _lane_iota((tm, tm)), 1.0, 0.0).astype(F32)
    cum = jnp.dot(tri, logf, precision=HIGHEST, preferred_element_type=F32) + carry_ref[...]
    carry_ref[...] = cum[tm - 1:tm, :]
    cedge_ref[0] = jnp.concatenate([cum[0:1] * LOG2E, cum[tm - 1:tm] * LOG2E, jnp.zeros((6, LANE), F32)], axis=0)

    for h in range(FOX_HEADS):
        c = jnp.broadcast_to(cum[:, h:h + 1], shp) * LOG2E
        hi, mid, lo = _split3(c)
        fq = _head_rms(blk(HY_F0 + h), gfq, HEAD_DIM) * scale
        fq = jnp.where(real, fq, jnp.where(lane == 64, hi, jnp.where(lane == 65, mid, jnp.where(
            lane == 66, lo, jnp.where(lane < 70, 1.0, 0.0)))))
        fk = _head_rms(blk(HY_F0 + FOX_HEADS + h), gfk, HEAD_DIM)
        fk = jnp.where(real, fk, jnp.where(lane < 67, 1.0, jnp.where(lane == 67, -hi, jnp.where(
            lane == 68, -mid, jnp.where(lane == 69, -lo, ref_ones)))))
        sl = slice(h * LANE, (h + 1) * LANE)
        fqt_ref[sl, :] = _t_bf16(fq)
        fk_ref[:, sl] = fk.astype(BF16)
        fvt_ref[h, 0] = _t_bf16(jnp.where(ones_row, 1.0, blk(HY_F0 + 2 * FOX_HEADS + h)), V_ROWS)


def _feat_major(heads, s, tm):
    return (pl.BlockSpec((heads * LANE, tm), lambda i: (0, i)),
            jax.ShapeDtypeStruct((heads * LANE, s), BF16))


def _value_tiles(heads, s, tm, tk, rows=LANE):
    return (pl.BlockSpec((heads, tm // tk, rows, tk), lambda i: (0, i, 0, 0)),
            jax.ShapeDtypeStruct((heads, s // tk, rows, tk), BF16))


def _pad_feature_rows(o):
    return jnp.concatenate([o, jnp.zeros((LANE - HEAD_DIM, o.shape[1]), o.dtype)], axis=0)


def _hy_prep(x2, norm_g, sc, sh, w_in, posf, inv128, gq, gk, gfq, gfk, fbias):
    s, d = x2.shape
    tm = PREP_TILE
    assert tm == KV_TILE and w_in.shape == (d, HY_BLOCKS * LANE) and HY_BLOCKS % 2 == 0
    vec = pl.BlockSpec((1, LANE), lambda i: (0, 0))
    dvec = pl.BlockSpec((1, d), lambda i: (0, 0))

    def rows(nb):
        return (pl.BlockSpec((tm, nb * LANE), lambda i: (i, 0)), jax.ShapeDtypeStruct((s, nb * LANE), BF16))

    tok = (pl.BlockSpec((NSA_GROUPS, tm, HEAD_DIM), lambda i: (0, i, 0)),
           jax.ShapeDtypeStruct((NSA_GROUPS, s, HEAD_DIM), BF16))
    gate = (pl.BlockSpec((NSA_GROUPS, GATE_ROWS, tm), lambda i: (0, 0, i)),
            jax.ShapeDtypeStruct((NSA_GROUPS, GATE_ROWS, s), F32))
    outs = [_feat_major(NSA_HEADS, s, tm), tok, tok, rows(NSA_GROUPS),
            _value_tiles(NSA_GROUPS, s, tm, KV_TILE, V_ROWS),
            rows(NSA_GROUPS), _value_tiles(NSA_GROUPS, s, tm, WIN_TILE, V_ROWS), gate,
            _feat_major(FOX_HEADS, s, tm), rows(FOX_HEADS), _value_tiles(FOX_HEADS, s, tm, KV_TILE, V_ROWS),
            (pl.BlockSpec((1, 8, LANE), lambda i: (i, 0, 0)), jax.ShapeDtypeStruct((s // tm, 8, LANE), F32))]
    return pl.pallas_call(
        _hy_prep_kernel,
        grid=(s // tm,),
        in_specs=[pl.BlockSpec((tm, d), lambda i: (i, 0)), dvec, dvec, dvec,
                  pl.BlockSpec((d, HY_BLOCKS * LANE), lambda i: (0, 0), pipeline_mode=pl.Buffered(1)),
                  pl.BlockSpec((tm, 1), lambda i: (i, 0)), vec, vec, vec, vec, vec, vec],
        out_specs=[o[0] for o in outs],
        out_shape=[o[1] for o in outs],
        scratch_shapes=[pltpu.VMEM((1, LANE), F32)],
        compiler_params=_cparams("arbitrary"),
        name="hybrid_prep",
    )(x2, norm_g, sc, sh, w_in, posf, inv128, gq, gk, gfq, gfk, fbias)


def _compress_kernel(kc_ref, vc_ref, wk_ref, wv_ref, pek_ref, pev_ref, gk_ref, ko_ref, vo_ref):
    half = CMP_STRIDE * HEAD_DIM

    def comp(ch_ref, w_ref, pe_ref):
        ch = ch_ref[0]
        nc = ch.shape[0]
        a = jnp.dot(ch, w_ref[:half], preferred_element_type=F32)
        b = jnp.dot(ch, w_ref[half:], preferred_element_type=F32)
        nxt = pltpu.roll(b, nc - 1, 0)
        pe = jnp.dot(jnp.broadcast_to(pe_ref[...], (8, 2 * half)).astype(BF16), w_ref[...],
                     preferred_element_type=F32)[0:1]
        return a + nxt + pe

    ko_ref[0] = _head_rms(comp(kc_ref, wk_ref, pek_ref), gk_ref[...], HEAD_DIM).astype(BF16)
    vo_ref[0] = comp(vc_ref, wv_ref, pev_ref).T.astype(BF16)


def _compress(kct, vct, w_cmp, cmp_pe, k_norm):
    g, s, _ = kct.shape
    nc = s // CMP_STRIDE
    wide = CMP_STRIDE * HEAD_DIM
    kch = kct.reshape(g, nc, wide)
    vch = vct.reshape(g, nc, wide)
    w_pad = jnp.pad(w_cmp, ((0, 0), (0, 0), (0, LANE - HEAD_DIM))).astype(BF16)
    ch = pl.BlockSpec((1, nc, wide), lambda i: (i, 0, 0))
    wspec = pl.BlockSpec((2 * wide, LANE), lambda i: (0, 0))
    pespec = pl.BlockSpec((1, 2 * wide), lambda i: (0, 0))
    return pl.pallas_call(
        _compress_kernel,
        grid=(g,),
        in_specs=[ch, ch, wspec, wspec, pespec, pespec, pl.BlockSpec((1, LANE), lambda i: (0, 0))],
        out_specs=[pl.BlockSpec((1, nc, LANE), lambda i: (i, 0, 0)),
                   pl.BlockSpec((1, LANE, nc), lambda i: (i, 0, 0))],
        out_shape=[jax.ShapeDtypeStruct((g, nc, LANE), BF16), jax.ShapeDtypeStruct((g, LANE, nc), BF16)],
        compiler_params=_cparams("parallel"),
        name="nsa_compress",
    )(kch, vch, w_pad[0], w_pad[1], cmp_pe[0].reshape(1, 2 * wide).astype(F32),
      cmp_pe[1].reshape(1, 2 * wide).astype(F32), _pad_lanes(k_norm))


def _masked_softmax_t(s, mask):
    s = jnp.where(mask, s, NEG)
    m = jnp.max(s, axis=0, keepdims=True)
    e = jnp.exp2(s - m)
    inv = 1.0 / jnp.maximum(jnp.sum(e, axis=0, keepdims=True), 1e-30)
    return e, jnp.where(m > 0.5 * NEG, inv, 0.0)


def _online_steps(steps, ms, acc_ref):
    ms = list(ms)

    def scores(step):
        k_tile, qa, _, c, mask = step
        s = jnp.dot(k_tile, qa, preferred_element_type=F32)
        if mask is not None:
            s = jnp.where(mask, s, NEG)
        return s, jnp.max(s, axis=0, keepdims=True)

    nxt = scores(steps[0])
    for idx, (_, _, vt, c, _) in enumerate(steps):
        sl = slice(c * Q_STRIP, (c + 1) * Q_STRIP)
        s, s_max = nxt
        if idx + 1 < len(steps):
            nxt = scores(steps[idx + 1])
        m_new = jnp.maximum(ms[c], s_max)
        a = jnp.exp2(ms[c] - m_new)
        p = jnp.exp2((s - m_new).astype(BF16))
        ms[c] = m_new
        acc_ref[:, sl] = a * acc_ref[:, sl] + jnp.dot(vt, p, preferred_element_type=F32)
    return tuple(ms)


def _m_init(n_strips):
    return tuple(jnp.full((1, Q_STRIP), NEG, F32) for _ in range(n_strips))


def _with_ref_rows(qa, m):
    hi, mid, lo = _split3(-m)
    r = _row_iota((LANE - REF_SLAB, Q_STRIP)) + REF_SLAB
    slab = jnp.where(r == REF_ROW, hi, jnp.where(r == REF_ROW + 1, mid, jnp.where(r == REF_ROW + 2, lo, 0.0)))
    return jnp.concatenate([qa[:REF_SLAB], slab.astype(BF16)], axis=0)


def _first_tile_max(k_tile, qa_strips, masks):
    return tuple(jnp.max(jnp.where(mask, jnp.dot(k_tile, qa, preferred_element_type=F32), NEG), axis=0, keepdims=True)
                 for qa, mask in zip(qa_strips, masks))


def _fast_steps(steps, state, acc_ref):
    state = list(state)
    depth, last = min(SCORE_LOOKAHEAD, len(steps)), {}
    for k, step in enumerate(steps):
        depth = min(depth, k - last.get(step[3], k - depth))
        last[step[3]] = k

    def scores(step):
        k_tile, qa, _, c, mask = step
        s = jnp.dot(k_tile, _with_ref_rows(qa, state[c][0]), preferred_element_type=F32)
        if mask is not None:
            s = jnp.where(mask, s, NEG)
        return s

    ahead = [scores(st) for st in steps[:depth]]
    for idx, (_, _, vt, c, _) in enumerate(steps):
        sl = slice(c * Q_STRIP, (c + 1) * Q_STRIP)
        s = ahead.pop(0)
        m, worst = state[c]
        cm = jnp.max(s, axis=0, keepdims=True)
        inc = jnp.maximum(cm, 0.0)
        state[c] = (m + inc, jnp.maximum(worst, cm))
        if idx + depth < len(steps):
            ahead.append(scores(steps[idx + depth]))
        p = jnp.exp2(s).astype(BF16)
        acc_ref[:, sl] = jnp.exp2(-inc) * (acc_ref[:, sl] + jnp.dot(vt, p, preferred_element_type=F32))
    return tuple(state)


def _flat(state):
    return tuple(x for pair in state for x in pair)


def _nest(flat):
    return tuple((flat[2 * c], flat[2 * c + 1]) for c in range(len(flat) // 2))


def _nsa_kernel(qt_ref, kc_ref, vct_ref, ks_ref, vst_ref, kw_ref, vwt_ref, gate_ref, ovt_ref, o_ref,
                qaug_ref, acc_ref, wacc_ref, *, n_sel):
    i = pl.program_id(1)
    tq = Q_TILE_NSA
    cols = NSA_HPG * tq
    qs = i * tq
    nc = kc_ref.shape[1]
    nslc = ovt_ref.shape[0]
    n_super = nslc // SUPER_BLOCKS

    qt = jnp.concatenate([qt_ref[h * LANE:(h + 1) * LANE, :] for h in range(NSA_HPG)], axis=1)
    tq_row = qs + (_lane_iota((1, cols)) % tq)

    def compress_and_select(n_c, n_b):
        s = jnp.dot(kc_ref[0, :n_c], qt, preferred_element_type=F32)
        cmp_end = _row_iota((n_c, 1)) * CMP_STRIDE + (CMP_LEN - 1)
        e, inv_l = _masked_softmax_t(s, cmp_end <= tq_row)
        o_cmp = jnp.dot(vct_ref[0, :, :n_c], e.astype(BF16), preferred_element_type=F32) * inv_l

        psum = e[:, 0:tq] * inv_l[:, 0:tq]
        for h in range(1, NSA_HPG):
            psum = psum + e[:, h * tq:(h + 1) * tq] * inv_l[:, h * tq:(h + 1) * tq]
        p_hi = psum.astype(BF16)
        p_lo = (psum - p_hi.astype(F32)).astype(BF16)
        ovt = ovt_ref[:n_b, :n_c]
        imp = (jnp.dot(ovt, p_hi, preferred_element_type=F32)
               + jnp.dot(ovt, p_lo, preferred_element_type=F32))

        jj = _row_iota((n_b, tq))
        tq_blk = qs + _lane_iota((n_b, tq))
        cur = tq_blk // SLC_BLOCK
        forced = (jj == 0) | (jj == cur) | (jj == cur - 1)
        causal_blk = jj * SLC_BLOCK <= tq_blk
        val = jnp.where(forced, imp + BIG, imp)
        val = jnp.where(causal_blk, val, NEG)
        jjf = jj.astype(F32)

        def pick(_, carry):
            val, sel = carry
            mx = jnp.max(val, axis=0, keepdims=True)
            idx = jnp.min(jnp.where(val == mx, jjf, float(n_b)), axis=0, keepdims=True)
            hit = jjf == idx
            return jnp.where(hit, -jnp.inf, val), jnp.where(hit, 1.0, sel)

        _, sel = lax.fori_loop(0, min(n_sel, n_b), pick, (val, jnp.zeros((n_b, tq), F32)))
        bias = jnp.where((sel > 0.0) & causal_blk, 0.0, NEG)
        if n_b < nslc:
            bias = jnp.concatenate([bias, jnp.full((nslc - n_b, tq), NEG, F32)], axis=0)
        return o_cmp, bias

    parts = max(1, min(NSA_PREFIX_PARTS, nc // LANE))
    seq = nc * CMP_STRIDE

    def dispatch(k):
        full = lambda: compress_and_select(nc * k // parts, nslc * k // parts)
        if k == parts:
            return full()
        return lax.cond(qs + tq <= seq * k // parts, full, lambda: dispatch(k + 1))

    o_cmp, bias_t = dispatch(1)

    tk = KV_TILE
    per_super = SUPER_BLOCKS * SLC_BLOCK // tk
    j_last = (qs + tq - 1) // tk
    q_rows = qt[:HEAD_DIM].astype(F32)
    spare = jnp.zeros((LANE - HEAD_DIM - SUPER_BLOCKS, cols), F32)
    for st in range(n_super):
        @pl.when(st * per_super <= j_last)
        def _(st=st):
            b = bias_t[st * SUPER_BLOCKS:(st + 1) * SUPER_BLOCKS]
            b = jnp.concatenate([b] * NSA_HPG, axis=1)
            qaug_ref[st] = jnp.concatenate([q_rows, b, spare], axis=0).astype(BF16)

    n_strips = cols // Q_STRIP
    strips = [slice(c * Q_STRIP, (c + 1) * Q_STRIP) for c in range(n_strips)]

    def causal_masks(j):
        kpos = j * tk + _row_iota((tk, 1))
        return [kpos <= tq_row[:, sl] for sl in strips]

    def slc_steps(j, masks=None):
        k0 = pl.multiple_of(j * tk, tk)
        k_tile, vt, st = ks_ref[pl.ds(k0, tk), :], vst_ref[0, j], j // per_super
        return [(k_tile, qaug_ref[st, :, strips[c]], vt, c, None if masks is None else masks[c])
                for c in range(n_strips)]

    group = 4

    def grouped(jj, flat):
        steps = [st for t in range(group) for st in slc_steps(group * jj + t)]
        return _flat(_fast_steps(steps, _nest(flat), acc_ref))

    def single(j, flat):
        return _flat(_fast_steps(slc_steps(j), _nest(flat), acc_ref))

    acc_ref[...] = jnp.zeros(acc_ref.shape, F32)
    head_rows = 16
    head_masks = [_row_iota((head_rows, 1)) <= tq_row[:, sl] for sl in strips]
    m0 = _first_tile_max(ks_ref[0:head_rows, :], [qaug_ref[0, :, sl] for sl in strips], head_masks)
    n_groups = j_last // group
    flat = lax.fori_loop(0, n_groups, grouped, _flat(tuple((m, jnp.zeros_like(m)) for m in m0)))
    flat = lax.fori_loop(group * n_groups, j_last, single, flat)
    state = _fast_steps(slc_steps(j_last, causal_masks(j_last)), _nest(flat), acc_ref)
    worst = jnp.max(jnp.concatenate([w for _, w in state], axis=1))

    @pl.when(worst > EXP_GUARD)
    def _():
        acc_ref[...] = jnp.zeros(acc_ref.shape, F32)
        lax.fori_loop(0, j_last + 1, lambda j, ms: _online_steps(slc_steps(j, causal_masks(j)), ms, acc_ref),
                      _m_init(n_strips))

    o_slc = acc_ref[:HEAD_DIM] / jnp.maximum(acc_ref[SUM_ROW:SUM_ROW + 1], 1e-30)

    wt = WIN_TILE
    n_wt = (WINDOW + tq) // wt
    ws = pl.multiple_of(jnp.maximum(qs - WINDOW, 0), wt)

    def win_masks(k0, rows):
        dist = tq_row - (k0 + _row_iota((rows, 1)))
        in_win = (dist >= 0) & (dist < WINDOW)
        return [in_win[:, sl] for sl in strips]

    def win_steps(t):
        k0 = pl.multiple_of(ws + t * wt, wt)
        k_tile, vt, masks = kw_ref[pl.ds(k0, wt), :], vwt_ref[0, ws // wt + t], win_masks(k0, wt)
        return [(k_tile, qt[:, strips[c]], vt, c, masks[c]) for c in range(n_strips)]

    k_start = pl.multiple_of(jnp.maximum(qs - tq, 0), tq)
    m0 = _first_tile_max(kw_ref[pl.ds(k_start, head_rows), :], [qt[:, sl] for sl in strips],
                         win_masks(k_start, head_rows))
    wacc_ref[...] = jnp.zeros(wacc_ref.shape, F32)
    state = _fast_steps([st for t in range(n_wt) for st in win_steps(t)],
                        tuple((m, jnp.zeros_like(m)) for m in m0), wacc_ref)
    worst = jnp.max(jnp.concatenate([w for _, w in state], axis=1))

    @pl.when(worst > EXP_GUARD)
    def _():
        wlen = n_wt * wt
        s = jnp.dot(kw_ref[pl.ds(ws, wlen), :], qt, preferred_element_type=F32)
        dist = tq_row - (ws + _row_iota((wlen, 1)))
        e, _ = _masked_softmax_t(s, (dist >= 0) & (dist < WINDOW))
        e = e.astype(BF16)
        acc = jnp.zeros(wacc_ref.shape, F32)
        for t in range(n_wt):
            acc = acc + jnp.dot(vwt_ref[0, ws // wt + t], e[t * wt:(t + 1) * wt], preferred_element_type=F32)
        wacc_ref[...] = acc

    o_win = wacc_ref[:HEAD_DIM] / jnp.maximum(wacc_ref[SUM_ROW:SUM_ROW + 1], 1e-30)

    gate = gate_ref[0]
    for h in range(NSA_HPG):
        sl = slice(h * tq, (h + 1) * tq)
        o = (gate[3 * h:3 * h + 1] * o_cmp[:HEAD_DIM, sl] + gate[3 * h + 1:3 * h + 2] * o_slc[:, sl]
             + gate[3 * h + 2:3 * h + 3] * o_win[:, sl])
        o_ref[:, h * LANE:(h + 1) * LANE] = _pad_feature_rows(o).T.astype(BF16)


def _overlap_t(s, nslc_pad):
    nc = s // CMP_STRIDE
    cmp_start = np.arange(nc) * CMP_STRIDE
    slc_start = np.arange(nslc_pad) * SLC_BLOCK
    ov = np.clip(np.minimum(cmp_start[:, None] + CMP_LEN, slc_start[None, :] + SLC_BLOCK)
                 - np.maximum(cmp_start[:, None], slc_start[None, :]), 0, None) / CMP_STRIDE
    ov[nc - CMP_LEN // CMP_STRIDE + 1:, :] = 0.0
    ov[:, s // SLC_BLOCK:] = 0.0
    return jnp.asarray(ov.T, BF16)


def _nsa_attention(qnt, kc, vct, ks, vst, kw, vwt, gates):
    s = qnt.shape[1]
    nc = s // CMP_STRIDE
    n_slc = s // SLC_BLOCK
    nslc_pad = -(-n_slc // LANE) * LANE
    tq = Q_TILE_NSA
    cols = NSA_HPG * tq
    once = pl.Buffered(1)
    res = pl.BlockSpec((s, LANE), lambda g, i: (0, g), pipeline_mode=once)
    return pl.pallas_call(
        functools.partial(_nsa_kernel, n_sel=min(SLC_TOPK, n_slc)),
        grid=(NSA_GROUPS, s // tq),
        in_specs=[pl.BlockSpec((NSA_HPG * LANE, tq), lambda g, i: (g, i)),
                  pl.BlockSpec((1, nc, LANE), lambda g, i: (g, 0, 0), pipeline_mode=once),
                  pl.BlockSpec((1, LANE, nc), lambda g, i: (g, 0, 0), pipeline_mode=once),
                  res, pl.BlockSpec((1, s // KV_TILE, V_ROWS, KV_TILE), lambda g, i: (g, 0, 0, 0),
                                    pipeline_mode=once),
                  res, pl.BlockSpec((1, s // WIN_TILE, V_ROWS, WIN_TILE), lambda g, i: (g, 0, 0, 0),
                                    pipeline_mode=once),
                  pl.BlockSpec((1, GATE_ROWS, tq), lambda g, i: (g, 0, i)),
                  pl.BlockSpec((nslc_pad, nc), lambda g, i: (0, 0), pipeline_mode=once)],
        out_specs=pl.BlockSpec((tq, NSA_HPG * LANE), lambda g, i: (i, g)),
        out_shape=jax.ShapeDtypeStruct((s, NSA_HEADS * LANE), BF16),
        scratch_shapes=[pltpu.VMEM((nslc_pad // SUPER_BLOCKS, LANE, cols), BF16),
                        pltpu.VMEM((V_ROWS, cols), F32), pltpu.VMEM((V_ROWS, cols), F32)],
        compiler_params=_cparams("parallel", "arbitrary"),
        name="nsa_attention",
    )(qnt, kc, vct, ks, vst, kw, vwt, gates, _overlap_t(s, nslc_pad))


def _flash_kernel(cfirst_ref, clast_ref, slack_ref, qt_ref, k_ref, vt_ref, o_ref, acc_ref, *, tq, decay):
    h = pl.program_id(0)
    i = pl.program_id(1)
    tk = KV_TILE
    n_tiles = k_ref.shape[0] // tk
    acc_ref[...] = jnp.zeros(acc_ref.shape, F32)

    n_strips = tq // Q_STRIP
    per_q = tq // tk
    qas = [qt_ref[:, c * Q_STRIP:(c + 1) * Q_STRIP] for c in range(n_strips)]

    def tile_steps(j, d=None):
        k0 = pl.multiple_of(j * tk, tk)
        k_tile, vt = k_ref[pl.ds(k0, tk), :], vt_ref[0, j]
        steps = []
        for c in range(n_strips):
            mask = None
            if d is not None:
                if d * tk > (c + 1) * Q_STRIP - 1:
                    continue
                if (d + 1) * tk - 1 > c * Q_STRIP:
                    shp = (tk, Q_STRIP)
                    mask = _row_iota(shp) + d * tk <= _lane_iota(shp) + c * Q_STRIP
            steps.append((k_tile, qas[c], vt, c, mask))
        return steps

    def any_tile_masks(j):
        shp = (tk, Q_STRIP)
        return [_row_iota(shp) + j * tk <= _lane_iota(shp) + (i * tq + c * Q_STRIP) for c in range(n_strips)]

    def below(t, flat):
        jj = i - 1 - t
        bound = (slack_ref[0] + cfirst_ref[h * n_tiles + i * per_q]
                 - clast_ref[h * n_tiles + jj * per_q + per_q - 1])

        def run(flat):
            steps = [st for u in range(per_q) for st in tile_steps(jj * per_q + (per_q - 1 - u))]
            return _flat(_fast_steps(steps, _nest(flat), acc_ref))

        return lax.cond(bound >= -SKIP_MARGIN, run, lambda flat: flat, flat)

    diag0 = i * per_q
    own = [(c * Q_STRIP) // tk for c in range(n_strips)]
    head = tk if decay else 16

    def start_max(c):
        j = diag0 + own[c]
        keys = k_ref[pl.ds(pl.multiple_of(j * tk, tk), head), :]
        shp = (head, Q_STRIP)
        mask = _row_iota(shp) + j * tk <= _lane_iota(shp) + (i * tq + c * Q_STRIP)
        return _first_tile_max(keys, [qas[c]], [mask])[0]

    m0 = tuple(start_max(c) for c in range(n_strips))
    steps = [st for d in reversed(range(per_q)) for st in tile_steps(diag0 + d, d)]
    state = _fast_steps(steps, tuple((m, jnp.zeros_like(m)) for m in m0), acc_ref)
    state = _nest(lax.fori_loop(0, i, below, _flat(state)))
    worst = jnp.max(jnp.concatenate([w for _, w in state], axis=1))

    @pl.when(worst > EXP_GUARD)
    def _():
        acc_ref[...] = jnp.zeros(acc_ref.shape, F32)

        def exact(j, ms):
            k0 = pl.multiple_of(j * tk, tk)
            k_tile, vt, masks = k_ref[pl.ds(k0, tk), :], vt_ref[0, j], any_tile_masks(j)
            return _online_steps([(k_tile, qas[c], vt, c, masks[c]) for c in range(n_strips)], ms, acc_ref)

        lax.fori_loop(0, (i + 1) * per_q, exact, _m_init(n_strips))

    o = _pad_feature_rows(acc_ref[:HEAD_DIM] / acc_ref[SUM_ROW:SUM_ROW + 1])
    for c0 in range(0, tq, LANE):
        o_ref[c0:c0 + LANE, :] = o[:, c0:c0 + LANE].T.astype(BF16)


def _causal_attention(qt, k, vt, tq, bias_edges=None, slack=None):
    s, width = k.shape
    heads = width // LANE
    tq = min(tq, s)
    assert tq % KV_TILE == 0 and s % tq == 0
    n_tiles = s // KV_TILE
    if bias_edges is None:
        first = last = jnp.zeros((heads * n_tiles,), F32)
        slack = jnp.full((1,), -NEG, F32)
    else:
        first, last = (e.reshape(heads * n_tiles).astype(F32) for e in bias_edges)
    grid_spec = pltpu.PrefetchScalarGridSpec(
        num_scalar_prefetch=3,
        grid=(heads, s // tq),
        in_specs=[pl.BlockSpec((LANE, tq), lambda h, i, *_: (h, i)),
                  pl.BlockSpec((s, LANE), lambda h, i, *_: (0, h)),
                  pl.BlockSpec((1, n_tiles, V_ROWS, KV_TILE), lambda h, i, *_: (h, 0, 0, 0))],
        out_specs=pl.BlockSpec((tq, LANE), lambda h, i, *_: (i, h)),
        scratch_shapes=[pltpu.VMEM((V_ROWS, tq), F32)],
    )
    return pl.pallas_call(
        functools.partial(_flash_kernel, tq=tq, decay=bias_edges is not None),
        grid_spec=grid_spec,
        out_shape=jax.ShapeDtypeStruct((s, width), BF16),
        compiler_params=_cparams("parallel", "arbitrary"),
        name="causal_attention",
    )(first, last, slack.astype(F32), qt, k, vt)


def _out_proj_kernel(oa_ref, ob_ref, wa_ref, wb_ref, x_ref, g_ref, o_ref):
    y = jnp.dot(oa_ref[...], wa_ref[...], preferred_element_type=F32)
    y = y + jnp.dot(ob_ref[...], wb_ref[...], preferred_element_type=F32)
    o_ref[...] = x_ref[...] + g_ref[...] * y


def _out_proj(oa, ob, cola, colb, wa, wb, x2, gate):
    s, d = x2.shape
    ka = wa.shape[0]
    tm = ROW_TILE
    return pl.pallas_call(
        _out_proj_kernel,
        grid=(s // tm,),
        in_specs=[pl.BlockSpec((tm, ka), lambda i: (i, cola)), pl.BlockSpec((tm, ka), lambda i: (i, colb)),
                  pl.BlockSpec((ka, d), lambda i: (0, 0)), pl.BlockSpec((ka, d), lambda i: (0, 0)),
                  pl.BlockSpec((tm, d), lambda i: (i, 0)), pl.BlockSpec((1, d), lambda i: (0, 0))],
        out_specs=pl.BlockSpec((tm, d), lambda i: (i, 0)),
        out_shape=jax.ShapeDtypeStruct((s, d), F32),
        compiler_params=_cparams("parallel"),
        name="out_proj",
    )(oa, ob, wa, wb, x2, gate)


def _mla_prep_kernel(x_ref, ng_ref, sc_ref, sh_ref, win_ref, pos_ref, inv_ref, gqa_ref, gkva_ref, wuq_ref, wuk_ref,
                     wuv_ref, gq_ref, gk_ref, gkr_ref, qt_ref, k_ref, vt_ref):
    shp = (PREP_TILE, LANE)
    act = _norm_mod(x_ref[...], ng_ref[...], sc_ref[...], sh_ref[...]).astype(BF16)
    proj = jnp.dot(act, win_ref[...], preferred_element_type=F32)
    lane = _lane_iota(shp)
    nope = lane < QK_NOPE
    rope = (lane >= QK_NOPE) & (lane < QK_NOPE + QK_ROPE)
    ref_ones = jnp.where((lane >= REF_ROW) & (lane < REF_ROW + 3), 1.0, 0.0)
    ang = pos_ref[...] * inv_ref[...]
    cos = jnp.where(rope, jnp.cos(ang), 1.0)
    sin = jnp.where(rope, jnp.sin(ang), 0.0)

    def rope32(x):
        half = QK_ROPE // 2
        rot = jnp.where(lane < QK_NOPE + half, -pltpu.roll(x, LANE - half, 1), pltpu.roll(x, half, 1))
        return x * cos + rot * sin

    def low_rank_norm(x, g):
        ms = jnp.mean(x * x, axis=-1, keepdims=True)
        return (x * lax.rsqrt(ms + EPS) * g).astype(BF16)

    nq = Q_LORA // LANE
    cq = low_rank_norm(proj[:, :Q_LORA], gqa_ref[...])
    ckv = low_rank_norm(proj[:, Q_LORA:Q_LORA + KV_LORA], gkva_ref[...])
    kr = proj[:, (nq + KV_LORA // LANE) * LANE:(nq + KV_LORA // LANE + 1) * LANE]
    k_rope = rope32(_head_rms(kr, gkr_ref[...], QK_ROPE))

    gq, gk = gq_ref[...], gk_ref[...]
    scale = (QK_NOPE + QK_ROPE) ** -0.5 * LOG2E
    pair = 2 * LANE
    for hp in range(MLA_HEADS // 2):
        cols = slice(hp * pair, (hp + 1) * pair)
        q2 = jnp.dot(cq, wuq_ref[:, cols], preferred_element_type=F32)
        k2 = jnp.dot(ckv, wuk_ref[:, cols], preferred_element_type=F32)
        v2 = jnp.dot(ckv, wuv_ref[:, cols], preferred_element_type=F32)
        for sub in range(2):
            head = 2 * hp + sub
            sl = slice(head * LANE, (head + 1) * LANE)
            half = slice(sub * LANE, (sub + 1) * LANE)
            x = q2[:, half]
            ss_n = jnp.sum(jnp.where(nope, x * x, 0.0), axis=-1, keepdims=True)
            ss_r = jnp.sum(jnp.where(rope, x * x, 0.0), axis=-1, keepdims=True)
            inv_rms = jnp.where(nope, lax.rsqrt(ss_n * (1.0 / QK_NOPE) + EPS),
                                lax.rsqrt(ss_r * (1.0 / QK_ROPE) + EPS))
            qt_ref[sl, :] = _t_bf16(rope32(x * inv_rms * gq) * scale)
            kn = _head_rms(k2[:, half], gk, QK_NOPE)
            k_ref[:, sl] = (kn + k_rope + ref_ones).astype(BF16)
            vt_ref[head, 0] = _t_bf16(jnp.where(lane == SUM_ROW, 1.0, v2[:, half]), V_ROWS)


def _mla_prep(x2, norm_g, sc, sh, w_in, posf, inv128, gqa, gkva, wuq, wuk, wuv, gq, gk, gkr):
    s, d = x2.shape
    tm = PREP_TILE
    assert tm == KV_TILE

    def full(a):
        return pl.BlockSpec(a.shape, lambda i: (0, 0))

    outs = [_feat_major(MLA_HEADS, s, tm),
            (pl.BlockSpec((tm, MLA_HEADS * LANE), lambda i: (i, 0)),
             jax.ShapeDtypeStruct((s, MLA_HEADS * LANE), BF16)),
            _value_tiles(MLA_HEADS, s, tm, KV_TILE, V_ROWS)]
    head = (norm_g, sc, sh, w_in)
    tail = (inv128, gqa, gkva, wuq, wuk, wuv, gq, gk, gkr)
    return pl.pallas_call(
        _mla_prep_kernel,
        grid=(s // tm,),
        in_specs=[pl.BlockSpec((tm, d), lambda i: (i, 0))] + [full(a) for a in head]
                 + [pl.BlockSpec((tm, 1), lambda i: (i, 0))] + [full(a) for a in tail],
        out_specs=[o[0] for o in outs],
        out_shape=[o[1] for o in outs],
        compiler_params=_cparams("parallel"),
        name="mla_prep",
    )(x2, *head, posf, *tail)


def _rank_lt(v, k):
    n = v.shape[0]
    row = _row_iota(v.shape)
    rank = jnp.zeros(v.shape, F32)
    for b in range(n):
        vb = v[b:b + 1, :]
        rank = rank + jnp.where((vb > v) | ((vb == v) & (row > b)), 1.0, 0.0)
    return rank < k


def _top_rows(v, k):
    rowf = _row_iota(v.shape).astype(F32)
    chosen = jnp.zeros(v.shape, F32)
    for _ in range(k):
        mx = jnp.max(v, axis=0, keepdims=True)
        idx = jnp.min(jnp.where(v == mx, rowf, float(v.shape[0])), axis=0, keepdims=True)
        hit = rowf == idx
        chosen = jnp.where(hit, 1.0, chosen)
        v = jnp.where(hit, -jnp.inf, v)
    return chosen > 0.0


def _moe_route_kernel(x_ref, g_ref, sc_ref, sh_ref, wr_ref, rb_ref, h_ref, pos_ref, wt_ref, cnt_ref):
    tm = ROW_TILE
    h = _norm_mod(x_ref[...], g_ref[...], sc_ref[...], sh_ref[...])
    h_ref[...] = h.astype(BF16)
    logits = jnp.dot(h, wr_ref[...], precision=HIGHEST, preferred_element_type=F32)
    lt = logits.T[:N_EXPERTS]
    scores = jax.nn.sigmoid(lt)
    sel = scores + rb_ref[...]

    per = N_EXPERTS // N_GROUPS
    grp = sel.reshape(N_GROUPS, per, tm)
    sub = lax.broadcasted_iota(jnp.int32, grp.shape, 1)
    m1 = jnp.max(grp, axis=1, keepdims=True)
    first = jnp.min(jnp.where(grp == m1, sub, per), axis=1, keepdims=True)
    m2 = jnp.max(jnp.where(sub == first, -jnp.inf, grp), axis=1, keepdims=True)
    gscore = (m1 + m2).reshape(N_GROUPS, tm)
    gmask = _rank_lt(gscore, TOPK_GROUPS)
    emask = jnp.broadcast_to(gmask.reshape(N_GROUPS, 1, tm), grp.shape).reshape(N_EXPERTS, tm)
    chosen = _top_rows(jnp.where(emask, sel, NEG), TOP_K)

    w = jnp.where(chosen, scores, 0.0)
    wt_ref[...] = w / jnp.sum(w, axis=0, keepdims=True) * ROUTED_SCALE

    upper = jnp.where(_row_iota((tm, tm)) <= _lane_iota((tm, tm)), 1.0, 0.0).astype(BF16)
    incl = jnp.dot(jnp.where(chosen, 1.0, 0.0).astype(BF16), upper, preferred_element_type=F32)
    pos_ref[...] = jnp.where(chosen, incl - 1.0, -1.0)
    cnt_ref[0] = jnp.broadcast_to(incl[:, tm - 1:tm], (N_EXPERTS, LANE))


def _moe_route(x2, g, sc, sh, w_router_pad, router_bias_col):
    s, d = x2.shape
    tm = ROW_TILE
    vec = pl.BlockSpec((1, d), lambda i: (0, 0))
    et = pl.BlockSpec((N_EXPERTS, tm), lambda i: (0, i))
    return pl.pallas_call(
        _moe_route_kernel,
        grid=(s // tm,),
        in_specs=[pl.BlockSpec((tm, d), lambda i: (i, 0)), vec, vec, vec,
                  pl.BlockSpec((d, LANE), lambda i: (0, 0)),
                  pl.BlockSpec((N_EXPERTS, 1), lambda i: (0, 0))],
        out_specs=[pl.BlockSpec((tm, d), lambda i: (i, 0)), et, et,
                   pl.BlockSpec((1, N_EXPERTS, LANE), lambda i: (i, 0, 0))],
        out_shape=[jax.ShapeDtypeStruct((s, d), BF16), jax.ShapeDtypeStruct((N_EXPERTS, s), F32),
                   jax.ShapeDtypeStruct((N_EXPERTS, s), F32),
                   jax.ShapeDtypeStruct((s // tm, N_EXPERTS, LANE), F32)],
        compiler_params=_cparams("parallel"),
        name="moe_route",
    )(x2, g, sc, sh, w_router_pad, router_bias_col)


def _moe_kernel(cnt_ref, x_ref, h_ref, pos_ref, wt_ref, wg_ref, wu_ref, wd_ref, sg_ref, su_ref, sd_ref,
                g2_ref, o_ref, acc_ref):
    i = pl.program_id(0)
    e = pl.program_id(1)
    tm = ROW_TILE
    r = MOE_CHUNK

    @pl.when(e == 0)
    def _():
        h = h_ref[...]
        a = jnp.dot(h, sg_ref[...], preferred_element_type=F32)
        a = a * jax.nn.sigmoid(a) * jnp.dot(h, su_ref[...], preferred_element_type=F32)
        acc_ref[...] = jnp.dot(a.astype(BF16), sd_ref[...], preferred_element_type=F32)

    first = e * MOE_EXPERTS_PER_STEP
    n = cnt_ref[i * N_EXPERTS + first]
    for k in range(1, MOE_EXPERTS_PER_STEP):
        n = jnp.maximum(n, cnt_ref[i * N_EXPERTS + first + k])
    prows = [pos_ref[pl.ds(first + k, 1), :] for k in range(MOE_EXPERTS_PER_STEP)]
    wrows = [wt_ref[pl.ds(first + k, 1), :] for k in range(MOE_EXPERTS_PER_STEP)]

    def chunk(c, _):
        slot = (_row_iota((r, tm)) + c * r).astype(F32)
        hits = [prow == slot for prow in prows]
        onehot = jnp.concatenate([jnp.where(hit, 1.0, 0.0).astype(BF16) for hit in hits], axis=0)
        xg = jnp.dot(onehot, h_ref[...], preferred_element_type=F32).astype(BF16)
        ys = []
        for k in range(MOE_EXPERTS_PER_STEP):
            xk = xg[k * r:(k + 1) * r]
            a = jnp.dot(xk, wg_ref[k], preferred_element_type=F32)
            a = a * jax.nn.sigmoid(a) * jnp.dot(xk, wu_ref[k], preferred_element_type=F32)
            y = jnp.dot(a.astype(BF16), wd_ref[k], preferred_element_type=F32)
            wr = jnp.sum(jnp.where(hits[k], wrows[k], 0.0), axis=-1, keepdims=True)
            ys.append((y * wr).astype(BF16))
        acc_ref[...] += _dot_tn(onehot, jnp.concatenate(ys, axis=0))
        return 0

    lax.fori_loop(0, (n + r - 1) // r, chunk, 0)

    @pl.when(e == N_EXPERTS // MOE_EXPERTS_PER_STEP - 1)
    def _():
        o_ref[...] = x_ref[...] + g2_ref[...] * acc_ref[...]


def _moe_experts(counts, x2, h, pos_t, w_t, wg, wu, wd, sg, su, sd, g2):
    s, d = x2.shape
    tm = ROW_TILE
    ff = wg.shape[2]
    tile = pl.BlockSpec((tm, d), lambda i, e, c: (i, 0))
    et = pl.BlockSpec((N_EXPERTS, tm), lambda i, e, c: (0, i))

    def const(a):
        return pl.BlockSpec(a.shape, lambda i, e, c: (0,) * a.ndim)

    per = MOE_EXPERTS_PER_STEP
    grid_spec = pltpu.PrefetchScalarGridSpec(
        num_scalar_prefetch=1,
        grid=(s // tm, N_EXPERTS // per),
        in_specs=[tile, tile, et, et,
                  pl.BlockSpec((per, d, ff), lambda i, e, c: (e, 0, 0)),
                  pl.BlockSpec((per, d, ff), lambda i, e, c: (e, 0, 0)),
                  pl.BlockSpec((per, ff, d), lambda i, e, c: (e, 0, 0)),
                  const(sg), const(su), const(sd), const(g2)],
        out_specs=tile,
        scratch_shapes=[pltpu.VMEM((tm, d), F32)],
    )
    return pl.pallas_call(
        _moe_kernel,
        grid_spec=grid_spec,
        out_shape=jax.ShapeDtypeStruct((s, d), F32),
        compiler_params=_cparams("parallel", "arbitrary"),
        name="moe_experts",
    )(counts, x2, h, pos_t, w_t, wg, wu, wd, sg, su, sd, g2)


def _pad_lanes(v, width=LANE, offset=0):
    out = jnp.zeros((1, width), F32)
    return out.at[0, offset:offset + v.shape[0]].set(v.astype(F32))


def _head_cols(w, n_heads, dim):
    d = w.shape[0]
    w3 = w.reshape(d, n_heads, dim)
    return jnp.pad(w3, ((0, 0), (0, 0), (0, LANE - dim))).reshape(d, n_heads * LANE)


def _hybrid_w_in(w_in):
    d = w_in.shape[0]
    nq = NSA_HEADS * HEAD_DIM
    nkv = 6 * NSA_GROUPS * HEAD_DIM
    ng = 3 * NSA_HEADS
    nf = 3 * FOX_HEADS * HEAD_DIM
    c0, c1, c2, c3 = nq, nq + nkv, nq + nkv + ng, nq + nkv + ng + nf
    gates = w_in[:, c1:c2].reshape(d, NSA_GROUPS, 3 * NSA_HPG)
    gates = jnp.pad(gates, ((0, 0), (0, 0), (0, LANE - 3 * NSA_HPG))).reshape(d, NSA_GROUPS * LANE)
    ff = jnp.pad(w_in[:, c3:], ((0, 0), (0, 2 * LANE - FOX_HEADS)))
    return jnp.concatenate([
        _head_cols(w_in[:, :c0], NSA_HEADS, HEAD_DIM),
        _head_cols(w_in[:, c0:c1], 6 * NSA_GROUPS, HEAD_DIM),
        _head_cols(w_in[:, c2:c3], 3 * FOX_HEADS, HEAD_DIM),
        gates, ff], axis=1).astype(BF16)


def _pad_head_rows(w, n_heads, dim):
    d = w.shape[1]
    w3 = w.reshape(n_heads, dim, d)
    return jnp.pad(w3, ((0, 0), (0, LANE - dim), (0, 0))).reshape(n_heads * LANE, d).astype(BF16)


def _rope_inv(dim, offset):
    inv = ROPE_THETA ** (-jnp.arange(0, dim, 2, dtype=F32) / dim)
    return _pad_lanes(jnp.concatenate([inv, inv]), offset=offset)


def _hybrid_mixer(x2, posf, mods, norm_g, w_in, fox_f_bias, nsa_q_norm, nsa_k_norm, nsa_cmp_pe, nsa_w_cmp,
                  fox_q_norm, fox_k_norm, w_out):
    sh1, sc1, g1 = mods
    (qnt, kct, vct, ks, vst, kw, vwt, gates, fqt, fk, fvt, cedge) = _hy_prep(
        x2, norm_g, sc1, sh1, _hybrid_w_in(w_in), posf, _rope_inv(HEAD_DIM, 0), _pad_lanes(nsa_q_norm),
        _pad_lanes(nsa_k_norm),
        _pad_lanes(fox_q_norm), _pad_lanes(fox_k_norm), _pad_lanes(fox_f_bias))
    kc, vc_t = _compress(kct, vct, nsa_w_cmp, nsa_cmp_pe, nsa_k_norm)
    o_a = _nsa_attention(qnt, kc, vc_t, ks, vst, kw, vwt, gates)
    slack = (2.0 * HEAD_DIM ** 0.5 * LOG2E) * jnp.max(jnp.abs(fox_q_norm)) * jnp.max(jnp.abs(fox_k_norm))
    edges = (cedge[:, 0, :FOX_HEADS].T, cedge[:, 1, :FOX_HEADS].T)
    o_b = _causal_attention(fqt, fk, fvt, Q_TILE_FOX, edges, slack.reshape(1))
    half = NSA_HEADS * HEAD_DIM
    wa = _pad_head_rows(w_out[:half], NSA_HEADS, HEAD_DIM)
    wb = _pad_head_rows(w_out[half:], FOX_HEADS, HEAD_DIM)
    return _out_proj(o_a, o_b, 0, 0, wa, wb, x2, g1)


def _mla_mixer(x2, posf, mods, norm_g, w_in, q_a_norm, kv_a_norm, w_uq, w_ukv, qn_norm, kn_norm, qr_norm,
               kr_norm, w_out):
    sh1, sc1, g1 = mods
    d = x2.shape[1]
    w_kr = jnp.zeros((d, LANE), F32).at[:, QK_NOPE:QK_NOPE + QK_ROPE].set(w_in[:, Q_LORA + KV_LORA:])
    w_in_p = jnp.concatenate([w_in[:, :Q_LORA + KV_LORA], w_kr], axis=1).astype(BF16)
    hq = QK_NOPE + QK_ROPE
    wuq = _head_cols(w_uq, MLA_HEADS, hq).astype(BF16)
    wkv3 = w_ukv.reshape(KV_LORA, MLA_HEADS, QK_NOPE + V_HEAD)
    wuk = _head_cols(wkv3[:, :, :QK_NOPE].reshape(KV_LORA, -1), MLA_HEADS, QK_NOPE).astype(BF16)
    wuv = _head_cols(wkv3[:, :, QK_NOPE:].reshape(KV_LORA, -1), MLA_HEADS, V_HEAD).astype(BF16)
    gq = _pad_lanes(jnp.concatenate([qn_norm, qr_norm]))
    qt, k, vt = _mla_prep(x2, norm_g, sc1, sh1, w_in_p, posf, _rope_inv(QK_ROPE, QK_NOPE),
                          q_a_norm.reshape(1, -1).astype(F32),
                          kv_a_norm.reshape(1, -1).astype(F32), wuq, wuk, wuv, gq, _pad_lanes(kn_norm),
                          _pad_lanes(kr_norm, offset=QK_NOPE))
    o = _causal_attention(qt, k, vt, Q_TILE_MLA)
    w_pad = _pad_head_rows(w_out, MLA_HEADS, V_HEAD)
    half = w_pad.shape[0] // 2
    return _out_proj(o, o, 0, 1, w_pad[:half], w_pad[half:], x2, g1)


def _moe_ffn(x2, mods, norm_g, w_router, router_bias, w_gate, w_up, w_down, ws_gate, ws_up, ws_down):
    sh2, sc2, g2 = mods
    w_r = jnp.pad(w_router.astype(F32), ((0, 0), (0, LANE - N_EXPERTS)))
    h, pos_t, w_t, cnt = _moe_route(x2, norm_g, sc2, sh2, w_r, router_bias.reshape(N_EXPERTS, 1).astype(F32))
    counts = cnt[:, :, 0].astype(jnp.int32).reshape(-1)
    return _moe_experts(counts, x2, h, pos_t, w_t, w_gate.astype(BF16), w_up.astype(BF16),
                        w_down.astype(BF16), ws_gate.astype(BF16), ws_up.astype(BF16), ws_down.astype(BF16), g2)


def kernel(x, c, positions, norm_attn, norm_ffn, w_ada, b_ada, hy_w_in, fox_f_bias, nsa_q_norm, nsa_k_norm, nsa_cmp_pe, nsa_w_cmp, fox_q_norm, fox_k_norm, hy_w_out, mla_w_in, mla_q_a_norm, mla_kv_a_norm, mla_w_uq, mla_w_ukv, mla_qn_norm, mla_kn_norm, mla_qr_norm, mla_kr_norm, mla_w_out, moe_w_router, moe_router_bias, moe_w_gate, moe_w_up, moe_w_down, moe_ws_gate, moe_ws_up, moe_ws_down):
    b, s, d = x.shape
    assert b == 1 and s % KV_TILE == 0 and s >= WINDOW + Q_TILE_NSA
    depth = w_ada.shape[0]
    x2 = x.reshape(s, d).astype(F32)
    posf = positions.reshape(s, 1).astype(F32)
    mod = _ada_mod(c.astype(F32), w_ada.astype(F32), b_ada.astype(F32))

    for layer in range(depth):
        m = [mod[layer, :, k * d:(k + 1) * d] for k in range(6)]
        i = layer // 2
        g_attn = norm_attn[layer].reshape(1, d).astype(F32)
        if layer % 2 == 0:
            x2 = _hybrid_mixer(x2, posf, m[0:3], g_attn, hy_w_in[i], fox_f_bias[i], nsa_q_norm[i],
                               nsa_k_norm[i], nsa_cmp_pe[i], nsa_w_cmp[i], fox_q_norm[i], fox_k_norm[i],
                               hy_w_out[i])
        else:
            x2 = _mla_mixer(x2, posf, m[0:3], g_attn, mla_w_in[i], mla_q_a_norm[i], mla_kv_a_norm[i],
                            mla_w_uq[i], mla_w_ukv[i], mla_qn_norm[i], mla_kn_norm[i], mla_qr_norm[i],
                            mla_kr_norm[i], mla_w_out[i])
        x2 = _moe_ffn(x2, m[3:6], norm_ffn[layer].reshape(1, d).astype(F32), moe_w_router[layer],
                      moe_router_bias[layer], moe_w_gate[layer], moe_w_up[layer], moe_w_down[layer],
                      moe_ws_gate[layer], moe_ws_up[layer], moe_ws_down[layer])
    return x2.reshape(b, s, d)
```

```python
import functools

import numpy as np
import jax
import jax.numpy as jnp
from jax import lax
from jax.experimental import pallas as pl
from jax.experimental.pallas import tpu as pltpu

F32 = jnp.float32
BF16 = jnp.bfloat16
HIGHEST = lax.Precision.HIGHEST

LANE = 128
VMEM_LIMIT_BYTES = 56 * 1024 * 1024

HEAD_DIM = 64
NSA_HEADS = 8
NSA_GROUPS = 2
NSA_HPG = NSA_HEADS // NSA_GROUPS
CMP_LEN = 32
CMP_STRIDE = 16
SLC_BLOCK = 64
SLC_TOPK = 16
WINDOW = 512
FOX_HEADS = 8
MLA_HEADS = 16
Q_LORA = 384
KV_LORA = 256
QK_NOPE = 64
QK_ROPE = 32
V_HEAD = 64
N_EXPERTS = 64
TOP_K = 8
N_GROUPS = 8
TOPK_GROUPS = 4
EXPERT_FF = 256
ROUTED_SCALE = 2.5
ROPE_THETA = 10000.0
EPS = 1e-6
NEG = -1e30
BIG = 1e6

ROW_TILE = 512
PREP_TILE = 512
GATE_ROWS = 16
Q_TILE_NSA = 256
KV_TILE = 512
Q_TILE_FOX = 1024
Q_TILE_MLA = 2048
Q_STRIP = 256
WIN_TILE = 256
SUPER_BLOCKS = 32
NSA_PREFIX_PARTS = 8
SUM_ROW = 64
V_ROWS = 80
REF_ROW = 104
REF_SLAB = 96
EXP_GUARD = 100.0
SCORE_LOOKAHEAD = 4
SKIP_MARGIN = 160.0
LOG2E = 1.4426950408889634
MOE_CHUNK = 128
MOE_EXPERTS_PER_STEP = 8

HY_Q0 = 0
HY_KV0 = HY_Q0 + NSA_HEADS
HY_F0 = HY_KV0 + 6 * NSA_GROUPS
HY_G0 = HY_F0 + 3 * FOX_HEADS
HY_FF = HY_G0 + NSA_GROUPS
HY_BLOCKS = HY_FF + 2


def _cparams(*sem):
    return pltpu.CompilerParams(dimension_semantics=sem, vmem_limit_bytes=VMEM_LIMIT_BYTES)


def _lane_iota(shape):
    return lax.broadcasted_iota(jnp.int32, shape, len(shape) - 1)


def _row_iota(shape):
    return lax.broadcasted_iota(jnp.int32, shape, len(shape) - 2)


def _dot_tn(a, b):
    return lax.dot_general(a, b, (((0,), (0,)), ((), ())), preferred_element_type=F32)


def _ada_kernel(c_ref, w_ref, b_ref, o_ref):
    c = c_ref[...]
    cond = c * jax.nn.sigmoid(c)
    o_ref[0] = jnp.dot(cond, w_ref[0], precision=HIGHEST, preferred_element_type=F32) + b_ref[0]


def _ada_mod(c, w_ada, b_ada):
    depth, d, n = w_ada.shape
    tn = 768
    c8 = jnp.broadcast_to(c.reshape(1, d), (8, d))
    out = pl.pallas_call(
        _ada_kernel,
        grid=(depth, n // tn),
        in_specs=[pl.BlockSpec((8, d), lambda l, j: (0, 0)),
                  pl.BlockSpec((1, d, tn), lambda l, j: (l, 0, j)),
                  pl.BlockSpec((1, 1, tn), lambda l, j: (l, 0, j))],
        out_specs=pl.BlockSpec((1, 8, tn), lambda l, j: (l, 0, j)),
        out_shape=jax.ShapeDtypeStruct((depth, 8, n), F32),
        compiler_params=_cparams("parallel", "parallel"),
        name="ada_mod",
    )(c8, w_ada, b_ada.reshape(depth, 1, n))
    return out[:, 0:1, :]


def _norm_mod(x, g, sc, sh):
    ms = jnp.mean(x * x, axis=-1, keepdims=True)
    return (x * lax.rsqrt(ms + EPS) * g) * (1.0 + sc) + sh


def _head_rms(x, gain, n_real):
    ss = jnp.sum(x * x, axis=-1, keepdims=True)
    return x * lax.rsqrt(ss * (1.0 / n_real) + EPS) * gain


def _rope64(x, cos, sin):
    lane = _lane_iota(x.shape)
    rot = jnp.where(lane < 32, -pltpu.roll(x, LANE - 32, 1), pltpu.roll(x, 32, 1))
    return x * cos + rot * sin


def _t_bf16(x, rows=LANE):
    return x.T[:rows].astype(BF16)


def _split3(c):
    hi = c.astype(BF16).astype(F32)
    r1 = c - hi
    mid = r1.astype(BF16).astype(F32)
    lo = (r1 - mid).astype(BF16).astype(F32)
    return hi, mid, lo


def _hy_prep_kernel(x_ref, ng_ref, sc_ref, sh_ref, w_ref, pos_ref, inv_ref, gq_ref, gk_ref, gfq_ref, gfk_ref,
                    fb_ref, qnt_ref, kct_ref, vct_ref, ks_ref, vst_ref, kw_ref, vwt_ref, gate_ref,
                    fqt_ref, fk_ref, fvt_ref, cedge_ref, carry_ref):
    i = pl.program_id(0)
    tm = PREP_TILE
    shp = (tm, LANE)
    lane = _lane_iota(shp)

    act = _norm_mod(x_ref[...], ng_ref[...], sc_ref[...], sh_ref[...]).astype(BF16)
    pairs = {}

    def blk(b):
        if b // 2 not in pairs:
            cols = slice((b // 2) * 2 * LANE, (b // 2 + 1) * 2 * LANE)
            pairs[b // 2] = jnp.dot(act, w_ref[:, cols], preferred_element_type=F32)
        return pairs[b // 2][:, (b % 2) * LANE:(b % 2 + 1) * LANE]

    ang = pos_ref[...] * inv_ref[...]
    real = lane < HEAD_DIM
    cos = jnp.where(real, jnp.cos(ang), 1.0)
    sin = jnp.where(real, jnp.sin(ang), 0.0)
    gq, gk, gfq, gfk = gq_ref[...], gk_ref[...], gfq_ref[...], gfk_ref[...]
    scale = HEAD_DIM ** -0.5 * LOG2E
    ones_row = lane == SUM_ROW
    ref_ones = jnp.where((lane >= REF_ROW) & (lane < REF_ROW + 3), 1.0, 0.0)

    for h in range(NSA_HEADS):
        q = _rope64(_head_rms(blk(HY_Q0 + h), gq, HEAD_DIM), cos, sin) * scale
        qnt_ref[h * LANE:(h + 1) * LANE, :] = _t_bf16(q)

    row = _row_iota(shp) + i * tm
    onehot = jnp.where(lane - HEAD_DIM == ((row // SLC_BLOCK) % SUPER_BLOCKS), 1.0, 0.0)
    for g in range(NSA_GROUPS):
        def kv(r):
            return blk(HY_KV0 + r * NSA_GROUPS + g)
        sl = slice(g * LANE, (g + 1) * LANE)
        kct_ref[g] = _rope64(kv(0), cos, sin)[:, :HEAD_DIM].astype(BF16)
        vct_ref[g] = kv(1)[:, :HEAD_DIM].astype(BF16)
        ks = _rope64(_head_rms(kv(2), gk, HEAD_DIM), cos, sin)
        ks_ref[:, sl] = (ks + onehot + ref_ones).astype(BF16)
        vst_ref[g, 0] = _t_bf16(jnp.where(ones_row, 1.0, kv(3)), V_ROWS)
        kw_ref[:, sl] = (_rope64(_head_rms(kv(4), gk, HEAD_DIM), cos, sin) + ref_ones).astype(BF16)
        vwt = _t_bf16(jnp.where(ones_row, 1.0, kv(5)), V_ROWS)
        for cidx in range(tm // WIN_TILE):
            vwt_ref[g, cidx] = vwt[:, cidx * WIN_TILE:(cidx + 1) * WIN_TILE]
        gate_ref[g] = jax.nn.sigmoid(blk(HY_G0 + g)).T[:GATE_ROWS]

    @pl.when(i == 0)
    def _():
        carry_ref[...] = jnp.zeros_like(carry_ref)

    z = blk(HY_FF) + fb_ref[...]
    logf = jnp.minimum(z, 0.0) - jnp.log1p(jnp.exp(-jnp.abs(z)))
    tri = jnp.where(_row_iota((tm, tm)) >= _lane_iota((tm, tm)), 1.0, 0.0).astype(F32)
    cum = jnp.dot(tri, logf, precision=HIGHEST, preferred_element_type=F32) + carry_ref[...]
    carry_ref[...] = cum[tm - 1:tm, :]
    cedge_ref[0] = jnp.concatenate([cum[0:1] * LOG2E, cum[tm - 1:tm] * LOG2E, jnp.zeros((6, LANE), F32)], axis=0)

    for h in range(FOX_HEADS):
        c = jnp.broadcast_to(cum[:, h:h + 1], shp) * LOG2E
        hi, mid, lo = _split3(c)
        fq = _head_rms(blk(HY_F0 + h), gfq, HEAD_DIM) * scale
        fq = jnp.where(real, fq, jnp.where(lane == 64, hi, jnp.where(lane == 65, mid, jnp.where(
            lane == 66, lo, jnp.where(lane < 70, 1.0, 0.0)))))
        fk = _head_rms(blk(HY_F0 + FOX_HEADS + h), gfk, HEAD_DIM)
        fk = jnp.where(real, fk, jnp.where(lane < 67, 1.0, jnp.where(lane == 67, -hi, jnp.where(
            lane == 68, -mid, jnp.where(lane == 69, -lo, ref_ones)))))
        sl = slice(h * LANE, (h + 1) * LANE)
        fqt_ref[sl, :] = _t_bf16(fq)
        fk_ref[:, sl] = fk.astype(BF16)
        fvt_ref[h, 0] = _t_bf16(jnp.where(ones_row, 1.0, blk(HY_F0 + 2 * FOX_HEADS + h)), V_ROWS)


def _feat_major(heads, s, tm):
    return (pl.BlockSpec((heads * LANE, tm), lambda i: (0, i)),
            jax.ShapeDtypeStruct((heads * LANE, s), BF16))


def _value_tiles(heads, s, tm, tk, rows=LANE):
    return (pl.BlockSpec((heads, tm // tk, rows, tk), lambda i: (0, i, 0, 0)),
            jax.ShapeDtypeStruct((heads, s // tk, rows, tk), BF16))


def _pad_feature_rows(o):
    return jnp.concatenate([o, jnp.zeros((LANE - HEAD_DIM, o.shape[1]), o.dtype)], axis=0)


def _hy_prep(x2, norm_g, sc, sh, w_in, posf, inv128, gq, gk, gfq, gfk, fbias):
    s, d = x2.shape
    tm = PREP_TILE
    assert tm == KV_TILE and w_in.shape == (d, HY_BLOCKS * LANE) and HY_BLOCKS % 2 == 0
    vec = pl.BlockSpec((1, LANE), lambda i: (0, 0))
    dvec = pl.BlockSpec((1, d), lambda i: (0, 0))

    def rows(nb):
        return (pl.BlockSpec((tm, nb * LANE), lambda i: (i, 0)), jax.ShapeDtypeStruct((s, nb * LANE), BF16))

    tok = (pl.BlockSpec((NSA_GROUPS, tm, HEAD_DIM), lambda i: (0, i, 0)),
           jax.ShapeDtypeStruct((NSA_GROUPS, s, HEAD_DIM), BF16))
    gate = (pl.BlockSpec((NSA_GROUPS, GATE_ROWS, tm), lambda i: (0, 0, i)),
            jax.ShapeDtypeStruct((NSA_GROUPS, GATE_ROWS, s), F32))
    outs = [_feat_major(NSA_HEADS, s, tm), tok, tok, rows(NSA_GROUPS),
            _value_tiles(NSA_GROUPS, s, tm, KV_TILE, V_ROWS),
            rows(NSA_GROUPS), _value_tiles(NSA_GROUPS, s, tm, WIN_TILE, V_ROWS), gate,
            _feat_major(FOX_HEADS, s, tm), rows(FOX_HEADS), _value_tiles(FOX_HEADS, s, tm, KV_TILE, V_ROWS),
            (pl.BlockSpec((1, 8, LANE), lambda i: (i, 0, 0)), jax.ShapeDtypeStruct((s // tm, 8, LANE), F32))]
    return pl.pallas_call(
        _hy_prep_kernel,
        grid=(s // tm,),
        in_specs=[pl.BlockSpec((tm, d), lambda i: (i, 0)), dvec, dvec, dvec,
                  pl.BlockSpec((d, HY_BLOCKS * LANE), lambda i: (0, 0), pipeline_mode=pl.Buffered(1)),
                  pl.BlockSpec((tm, 1), lambda i: (i, 0)), vec, vec, vec, vec, vec, vec],
        out_specs=[o[0] for o in outs],
        out_shape=[o[1] for o in outs],
        scratch_shapes=[pltpu.VMEM((1, LANE), F32)],
        compiler_params=_cparams("arbitrary"),
        name="hybrid_prep",
    )(x2, norm_g, sc, sh, w_in, posf, inv128, gq, gk, gfq, gfk, fbias)


def _compress_kernel(kc_ref, vc_ref, wk_ref, wv_ref, pek_ref, pev_ref, gk_ref, ko_ref, vo_ref):
    half = CMP_STRIDE * HEAD_DIM

    def comp(ch_ref, w_ref, pe_ref):
        ch = ch_ref[0]
        nc = ch.shape[0]
        a = jnp.dot(ch, w_ref[:half], preferred_element_type=F32)
        b = jnp.dot(ch, w_ref[half:], preferred_element_type=F32)
        nxt = pltpu.roll(b, nc - 1, 0)
        pe = jnp.dot(jnp.broadcast_to(pe_ref[...], (8, 2 * half)).astype(BF16), w_ref[...],
                     preferred_element_type=F32)[0:1]
        return a + nxt + pe

    ko_ref[0] = _head_rms(comp(kc_ref, wk_ref, pek_ref), gk_ref[...], HEAD_DIM).astype(BF16)
    vo_ref[0] = comp(vc_ref, wv_ref, pev_ref).T.astype(BF16)


def _compress(kct, vct, w_cmp, cmp_pe, k_norm):
    g, s, _ = kct.shape
    nc = s // CMP_STRIDE
    wide = CMP_STRIDE * HEAD_DIM
    kch = kct.reshape(g, nc, wide)
    vch = vct.reshape(g, nc, wide)
    w_pad = jnp.pad(w_cmp, ((0, 0), (0, 0), (0, LANE - HEAD_DIM))).astype(BF16)
    ch = pl.BlockSpec((1, nc, wide), lambda i: (i, 0, 0))
    wspec = pl.BlockSpec((2 * wide, LANE), lambda i: (0, 0))
    pespec = pl.BlockSpec((1, 2 * wide), lambda i: (0, 0))
    return pl.pallas_call(
        _compress_kernel,
        grid=(g,),
        in_specs=[ch, ch, wspec, wspec, pespec, pespec, pl.BlockSpec((1, LANE), lambda i: (0, 0))],
        out_specs=[pl.BlockSpec((1, nc, LANE), lambda i: (i, 0, 0)),
                   pl.BlockSpec((1, LANE, nc), lambda i: (i, 0, 0))],
        out_shape=[jax.ShapeDtypeStruct((g, nc, LANE), BF16), jax.ShapeDtypeStruct((g, LANE, nc), BF16)],
        compiler_params=_cparams("parallel"),
        name="nsa_compress",
    )(kch, vch, w_pad[0], w_pad[1], cmp_pe[0].reshape(1, 2 * wide).astype(F32),
      cmp_pe[1].reshape(1, 2 * wide).astype(F32), _pad_lanes(k_norm))


def _masked_softmax_t(s, mask):
    s = jnp.where(mask, s, NEG)
    m = jnp.max(s, axis=0, keepdims=True)
    e = jnp.exp2(s - m)
    inv = 1.0 / jnp.maximum(jnp.sum(e, axis=0, keepdims=True), 1e-30)
    return e, jnp.where(m > 0.5 * NEG, inv, 0.0)


def _online_steps(steps, ms, acc_ref):
    ms = list(ms)

    def scores(step):
        k_tile, qa, _, c, mask = step
        s = jnp.dot(k_tile, qa, preferred_element_type=F32)
        if mask is not None:
            s = jnp.where(mask, s, NEG)
        return s, jnp.max(s, axis=0, keepdims=True)

    nxt = scores(steps[0])
    for idx, (_, _, vt, c, _) in enumerate(steps):
        sl = slice(c * Q_STRIP, (c + 1) * Q_STRIP)
        s, s_max = nxt
        if idx + 1 < len(steps):
            nxt = scores(steps[idx + 1])
        m_new = jnp.maximum(ms[c], s_max)
        a = jnp.exp2(ms[c] - m_new)
        p = jnp.exp2((s - m_new).astype(BF16))
        ms[c] = m_new
        acc_ref[:, sl] = a * acc_ref[:, sl] + jnp.dot(vt, p, preferred_element_type=F32)
    return tuple(ms)


def _m_init(n_strips):
    return tuple(jnp.full((1, Q_STRIP), NEG, F32) for _ in range(n_strips))


def _with_ref_rows(qa, m):
    hi, mid, lo = _split3(-m)
    r = _row_iota((LANE - REF_SLAB, Q_STRIP)) + REF_SLAB
    slab = jnp.where(r == REF_ROW, hi, jnp.where(r == REF_ROW + 1, mid, jnp.where(r == REF_ROW + 2, lo, 0.0)))
    return jnp.concatenate([qa[:REF_SLAB], slab.astype(BF16)], axis=0)


def _first_tile_max(k_tile, qa_strips, masks):
    return tuple(jnp.max(jnp.where(mask, jnp.dot(k_tile, qa, preferred_element_type=F32), NEG), axis=0, keepdims=True)
                 for qa, mask in zip(qa_strips, masks))


def _fast_steps(steps, state, acc_ref):
    state = list(state)
    depth, last = min(SCORE_LOOKAHEAD, len(steps)), {}
    for k, step in enumerate(steps):
        depth = min(depth, k - last.get(step[3], k - depth))
        last[step[3]] = k

    def scores(step):
        k_tile, qa, _, c, mask = step
        s = jnp.dot(k_tile, _with_ref_rows(qa, state[c][0]), preferred_element_type=F32)
        if mask is not None:
            s = jnp.where(mask, s, NEG)
        return s

    ahead = [scores(st) for st in steps[:depth]]
    for idx, (_, _, vt, c, _) in enumerate(steps):
        sl = slice(c * Q_STRIP, (c + 1) * Q_STRIP)
        s = ahead.pop(0)
        m, worst = state[c]
        cm = jnp.max(s, axis=0, keepdims=True)
        inc = jnp.maximum(cm, 0.0)
        state[c] = (m + inc, jnp.maximum(worst, cm))
        if idx + depth < len(steps):
            ahead.append(scores(steps[idx + depth]))
        p = jnp.exp2(s).astype(BF16)
        acc_ref[:, sl] = jnp.exp2(-inc) * (acc_ref[:, sl] + jnp.dot(vt, p, preferred_element_type=F32))
    return tuple(state)


def _flat(state):
    return tuple(x for pair in state for x in pair)


def _nest(flat):
    return tuple((flat[2 * c], flat[2 * c + 1]) for c in range(len(flat) // 2))


def _nsa_kernel(qt_ref, kc_ref, vct_ref, ks_ref, vst_ref, kw_ref, vwt_ref, gate_ref, ovt_ref, o_ref,
                qaug_ref, acc_ref, wacc_ref, *, n_sel):
    i = pl.program_id(1)
    tq = Q_TILE_NSA
    cols = NSA_HPG * tq
    qs = i * tq
    nc = kc_ref.shape[1]
    nslc = ovt_ref.shape[0]
    n_super = nslc // SUPER_BLOCKS

    qt = jnp.concatenate([qt_ref[h * LANE:(h + 1) * LANE, :] for h in range(NSA_HPG)], axis=1)
    tq_row = qs + (_lane_iota((1, cols)) % tq)

    def compress_and_select(n_c, n_b):
        s = jnp.dot(kc_ref[0, :n_c], qt, preferred_element_type=F32)
        cmp_end = _row_iota((n_c, 1)) * CMP_STRIDE + (CMP_LEN - 1)
        e, inv_l = _masked_softmax_t(s, cmp_end <= tq_row)
        o_cmp = jnp.dot(vct_ref[0, :, :n_c], e.astype(BF16), preferred_element_type=F32) * inv_l

        psum = e[:, 0:tq] * inv_l[:, 0:tq]
        for h in range(1, NSA_HPG):
            psum = psum + e[:, h * tq:(h + 1) * tq] * inv_l[:, h * tq:(h + 1) * tq]
        p_hi = psum.astype(BF16)
        p_lo = (psum - p_hi.astype(F32)).astype(BF16)
        ovt = ovt_ref[:n_b, :n_c]
        imp = (jnp.dot(ovt, p_hi, preferred_element_type=F32)
               + jnp.dot(ovt, p_lo, preferred_element_type=F32))

        jj = _row_iota((n_b, tq))
        tq_blk = qs + _lane_iota((n_b, tq))
        cur = tq_blk // SLC_BLOCK
        forced = (jj == 0) | (jj == cur) | (jj == cur - 1)
        causal_blk = jj * SLC_BLOCK <= tq_blk
        val = jnp.where(forced, imp + BIG, imp)
        val = jnp.where(causal_blk, val, NEG)
        jjf = jj.astype(F32)

        def pick(_, carry):
            val, sel = carry
            mx = jnp.max(val, axis=0, keepdims=True)
            idx = jnp.min(jnp.where(val == mx, jjf, float(n_b)), axis=0, keepdims=True)
            hit = jjf == idx
            return jnp.where(hit, -jnp.inf, val), jnp.where(hit, 1.0, sel)

        _, sel = lax.fori_loop(0, min(n_sel, n_b), pick, (val, jnp.zeros((n_b, tq), F32)))
        bias = jnp.where((sel > 0.0) & causal_blk, 0.0, NEG)
        if n_b < nslc:
            bias = jnp.concatenate([bias, jnp.full((nslc - n_b, tq), NEG, F32)], axis=0)
        return o_cmp, bias

    parts = max(1, min(NSA_PREFIX_PARTS, nc // LANE))
    seq = nc * CMP_STRIDE

    def dispatch(k):
        full = lambda: compress_and_select(nc * k // parts, nslc * k // parts)
        if k == parts:
            return full()
        return lax.cond(qs + tq <= seq * k // parts, full, lambda: dispatch(k + 1))

    o_cmp, bias_t = dispatch(1)

    q_rows = qt[:HEAD_DIM].astype(F32)
    spare = jnp.zeros((LANE - HEAD_DIM - SUPER_BLOCKS, cols), F32)
    for st in range(n_super):
        b = bias_t[st * SUPER_BLOCKS:(st + 1) * SUPER_BLOCKS]
        b = jnp.concatenate([b] * NSA_HPG, axis=1)
        qaug_ref[st] = jnp.concatenate([q_rows, b, spare], axis=0).astype(BF16)

    tk = KV_TILE
    per_super = SUPER_BLOCKS * SLC_BLOCK // tk
    j_last = (qs + tq - 1) // tk
    n_strips = cols // Q_STRIP
    strips = [slice(c * Q_STRIP, (c + 1) * Q_STRIP) for c in range(n_strips)]

    def causal_masks(j):
        kpos = j * tk + _row_iota((tk, 1))
        return [kpos <= tq_row[:, sl] for sl in strips]

    def slc_steps(j, masks=None):
        k0 = pl.multiple_of(j * tk, tk)
        k_tile, vt, st = ks_ref[pl.ds(k0, tk), :], vst_ref[0, j], j // per_super
        return [(k_tile, qaug_ref[st, :, strips[c]], vt, c, None if masks is None else masks[c])
                for c in range(n_strips)]

    group = 4

    def grouped(jj, flat):
        steps = [st for t in range(group) for st in slc_steps(group * jj + t)]
        return _flat(_fast_steps(steps, _nest(flat), acc_ref))

    def single(j, flat):
        return _flat(_fast_steps(slc_steps(j), _nest(flat), acc_ref))

    acc_ref[...] = jnp.zeros(acc_ref.shape, F32)
    head_rows = 16
    head_masks = [_row_iota((head_rows, 1)) <= tq_row[:, sl] for sl in strips]
    m0 = _first_tile_max(ks_ref[0:head_rows, :], [qaug_ref[0, :, sl] for sl in strips], head_masks)
    n_groups = j_last // group
    flat = lax.fori_loop(0, n_groups, grouped, _flat(tuple((m, jnp.zeros_like(m)) for m in m0)))
    flat = lax.fori_loop(group * n_groups, j_last, single, flat)
    state = _fast_steps(slc_steps(j_last, causal_masks(j_last)), _nest(flat), acc_ref)
    worst = jnp.max(jnp.concatenate([w for _, w in state], axis=1))

    @pl.when(worst > EXP_GUARD)
    def _():
        acc_ref[...] = jnp.zeros(acc_ref.shape, F32)
        lax.fori_loop(0, j_last + 1, lambda j, ms: _online_steps(slc_steps(j, causal_masks(j)), ms, acc_ref),
                      _m_init(n_strips))

    o_slc = acc_ref[:HEAD_DIM] / jnp.maximum(acc_ref[SUM_ROW:SUM_ROW + 1], 1e-30)

    wt = WIN_TILE
    n_wt = (WINDOW + tq) // wt
    ws = pl.multiple_of(jnp.maximum(qs - WINDOW, 0), wt)

    def win_masks(k0, rows):
        dist = tq_row - (k0 + _row_iota((rows, 1)))
        in_win = (dist >= 0) & (dist < WINDOW)
        return [in_win[:, sl] for sl in strips]

    def win_steps(t):
        k0 = pl.multiple_of(ws + t * wt, wt)
        k_tile, vt, masks = kw_ref[pl.ds(k0, wt), :], vwt_ref[0, ws // wt + t], win_masks(k0, wt)
        return [(k_tile, qt[:, strips[c]], vt, c, masks[c]) for c in range(n_strips)]

    k_start = pl.multiple_of(jnp.maximum(qs - tq, 0), tq)
    m0 = _first_tile_max(kw_ref[pl.ds(k_start, head_rows), :], [qt[:, sl] for sl in strips],
                         win_masks(k_start, head_rows))
    wacc_ref[...] = jnp.zeros(wacc_ref.shape, F32)
    state = _fast_steps([st for t in range(n_wt) for st in win_steps(t)],
                        tuple((m, jnp.zeros_like(m)) for m in m0), wacc_ref)
    worst = jnp.max(jnp.concatenate([w for _, w in state], axis=1))

    @pl.when(worst > EXP_GUARD)
    def _():
        wlen = n_wt * wt
        s = jnp.dot(kw_ref[pl.ds(ws, wlen), :], qt, preferred_element_type=F32)
        dist = tq_row - (ws + _row_iota((wlen, 1)))
        e, _ = _masked_softmax_t(s, (dist >= 0) & (dist < WINDOW))
        e = e.astype(BF16)
        acc = jnp.zeros(wacc_ref.shape, F32)
        for t in range(n_wt):
            acc = acc + jnp.dot(vwt_ref[0, ws // wt + t], e[t * wt:(t + 1) * wt], preferred_element_type=F32)
        wacc_ref[...] = acc

    o_win = wacc_ref[:HEAD_DIM] / jnp.maximum(wacc_ref[SUM_ROW:SUM_ROW + 1], 1e-30)

    gate = gate_ref[0]
    for h in range(NSA_HPG):
        sl = slice(h * tq, (h + 1) * tq)
        o = (gate[3 * h:3 * h + 1] * o_cmp[:HEAD_DIM, sl] + gate[3 * h + 1:3 * h + 2] * o_slc[:, sl]
             + gate[3 * h + 2:3 * h + 3] * o_win[:, sl])
        o_ref[:, h * LANE:(h + 1) * LANE] = _pad_feature_rows(o).T.astype(BF16)


def _overlap_t(s, nslc_pad):
    nc = s // CMP_STRIDE
    cmp_start = np.arange(nc) * CMP_STRIDE
    slc_start = np.arange(nslc_pad) * SLC_BLOCK
    ov = np.clip(np.minimum(cmp_start[:, None] + CMP_LEN, slc_start[None, :] + SLC_BLOCK)
                 - np.maximum(cmp_start[:, None], slc_start[None, :]), 0, None) / CMP_STRIDE
    ov[nc - CMP_LEN // CMP_STRIDE + 1:, :] = 0.0
    ov[:, s // SLC_BLOCK:] = 0.0
    return jnp.asarray(ov.T, BF16)


def _nsa_attention(qnt, kc, vct, ks, vst, kw, vwt, gates):
    s = qnt.shape[1]
    nc = s // CMP_STRIDE
    n_slc = s // SLC_BLOCK
    nslc_pad = -(-n_slc // LANE) * LANE
    tq = Q_TILE_NSA
    cols = NSA_HPG * tq
    once = pl.Buffered(1)
    res = pl.BlockSpec((s, LANE), lambda g, i: (0, g), pipeline_mode=once)
    return pl.pallas_call(
        functools.partial(_nsa_kernel, n_sel=min(SLC_TOPK, n_slc)),
        grid=(NSA_GROUPS, s // tq),
        in_specs=[pl.BlockSpec((NSA_HPG * LANE, tq), lambda g, i: (g, i)),
                  pl.BlockSpec((1, nc, LANE), lambda g, i: (g, 0, 0), pipeline_mode=once),
                  pl.BlockSpec((1, LANE, nc), lambda g, i: (g, 0, 0), pipeline_mode=once),
                  res, pl.BlockSpec((1, s // KV_TILE, V_ROWS, KV_TILE), lambda g, i: (g, 0, 0, 0),
                                    pipeline_mode=once),
                  res, pl.BlockSpec((1, s // WIN_TILE, V_ROWS, WIN_TILE), lambda g, i: (g, 0, 0, 0),
                                    pipeline_mode=once),
                  pl.BlockSpec((1, GATE_ROWS, tq), lambda g, i: (g, 0, i)),
                  pl.BlockSpec((nslc_pad, nc), lambda g, i: (0, 0), pipeline_mode=once)],
        out_specs=pl.BlockSpec((tq, NSA_HPG * LANE), lambda g, i: (i, g)),
        out_shape=jax.ShapeDtypeStruct((s, NSA_HEADS * LANE), BF16),
        scratch_shapes=[pltpu.VMEM((nslc_pad // SUPER_BLOCKS, LANE, cols), BF16),
                        pltpu.VMEM((V_ROWS, cols), F32), pltpu.VMEM((V_ROWS, cols), F32)],
        compiler_params=_cparams("parallel", "arbitrary"),
        name="nsa_attention",
    )(qnt, kc, vct, ks, vst, kw, vwt, gates, _overlap_t(s, nslc_pad))


def _flash_kernel(cfirst_ref, clast_ref, slack_ref, qt_ref, k_ref, vt_ref, o_ref, acc_ref, *, tq, decay):
    h = pl.program_id(0)
    i = pl.program_id(1)
    tk = KV_TILE
    n_tiles = k_ref.shape[0] // tk
    acc_ref[...] = jnp.zeros(acc_ref.shape, F32)

    n_strips = tq // Q_STRIP
    per_q = tq // tk
    qas = [qt_ref[:, c * Q_STRIP:(c + 1) * Q_STRIP] for c in range(n_strips)]

    def tile_steps(j, d=None):
        k0 = pl.multiple_of(j * tk, tk)
        k_tile, vt = k_ref[pl.ds(k0, tk), :], vt_ref[0, j]
        steps = []
        for c in range(n_strips):
            mask = None
            if d is not None:
                if d * tk > (c + 1) * Q_STRIP - 1:
                    continue
                if (d + 1) * tk - 1 > c * Q_STRIP:
                    shp = (tk, Q_STRIP)
                    mask = _row_iota(shp) + d * tk <= _lane_iota(shp) + c * Q_STRIP
            steps.append((k_tile, qas[c], vt, c, mask))
        return steps

    def any_tile_masks(j):
        shp = (tk, Q_STRIP)
        return [_row_iota(shp) + j * tk <= _lane_iota(shp) + (i * tq + c * Q_STRIP) for c in range(n_strips)]

    def below(t, flat):
        jj = i - 1 - t
        bound = (slack_ref[0] + cfirst_ref[h * n_tiles + i * per_q]
                 - clast_ref[h * n_tiles + jj * per_q + per_q - 1])

        def run(flat):
            steps = [st for u in range(per_q) for st in tile_steps(jj * per_q + (per_q - 1 - u))]
            return _flat(_fast_steps(steps, _nest(flat), acc_ref))

        return lax.cond(bound >= -SKIP_MARGIN, run, lambda flat: flat, flat)

    diag0 = i * per_q
    own = [(c * Q_STRIP) // tk for c in range(n_strips)]
    head = tk if decay else 16

    def start_max(c):
        j = diag0 + own[c]
        keys = k_ref[pl.ds(pl.multiple_of(j * tk, tk), head), :]
        shp = (head, Q_STRIP)
        mask = _row_iota(shp) + j * tk <= _lane_iota(shp) + (i * tq + c * Q_STRIP)
        return _first_tile_max(keys, [qas[c]], [mask])[0]

    m0 = tuple(start_max(c) for c in range(n_strips))
    steps = [st for d in reversed(range(per_q)) for st in tile_steps(diag0 + d, d)]
    state = _fast_steps(steps, tuple((m, jnp.zeros_like(m)) for m in m0), acc_ref)
    state = _nest(lax.fori_loop(0, i, below, _flat(state)))
    worst = jnp.max(jnp.concatenate([w for _, w in state], axis=1))

    @pl.when(worst > EXP_GUARD)
    def _():
        acc_ref[...] = jnp.zeros(acc_ref.shape, F32)

        def exact(j, ms):
            k0 = pl.multiple_of(j * tk, tk)
            k_tile, vt, masks = k_ref[pl.ds(k0, tk), :], vt_ref[0, j], any_tile_masks(j)
            return _online_steps([(k_tile, qas[c], vt, c, masks[c]) for c in range(n_strips)], ms, acc_ref)

        lax.fori_loop(0, (i + 1) * per_q, exact, _m_init(n_strips))

    o = _pad_feature_rows(acc_ref[:HEAD_DIM] / acc_ref[SUM_ROW:SUM_ROW + 1])
    for c0 in range(0, tq, LANE):
        o_ref[c0:c0 + LANE, :] = o[:, c0:c0 + LANE].T.astype(BF16)


def _causal_attention(qt, k, vt, tq, bias_edges=None, slack=None):
    s, width = k.shape
    heads = width // LANE
    tq = min(tq, s)
    assert tq % KV_TILE == 0 and s % tq == 0
    n_tiles = s // KV_TILE
    if bias_edges is None:
        first = last = jnp.zeros((heads * n_tiles,), F32)
        slack = jnp.full((1,), -NEG, F32)
    else:
        first, last = (e.reshape(heads * n_tiles).astype(F32) for e in bias_edges)
    grid_spec = pltpu.PrefetchScalarGridSpec(
        num_scalar_prefetch=3,
        grid=(heads, s // tq),
        in_specs=[pl.BlockSpec((LANE, tq), lambda h, i, *_: (h, i)),
                  pl.BlockSpec((s, LANE), lambda h, i, *_: (0, h)),
                  pl.BlockSpec((1, n_tiles, V_ROWS, KV_TILE), lambda h, i, *_: (h, 0, 0, 0))],
        out_specs=pl.BlockSpec((tq, LANE), lambda h, i, *_: (i, h)),
        scratch_shapes=[pltpu.VMEM((V_ROWS, tq), F32)],
    )
    return pl.pallas_call(
        functools.partial(_flash_kernel, tq=tq, decay=bias_edges is not None),
        grid_spec=grid_spec,
        out_shape=jax.ShapeDtypeStruct((s, width), BF16),
        compiler_params=_cparams("parallel", "arbitrary"),
        name="causal_attention",
    )(first, last, slack.astype(F32), qt, k, vt)


def _out_proj_kernel(oa_ref, ob_ref, wa_ref, wb_ref, x_ref, g_ref, o_ref):
    y = jnp.dot(oa_ref[...], wa_ref[...], preferred_element_type=F32)
    y = y + jnp.dot(ob_ref[...], wb_ref[...], preferred_element_type=F32)
    o_ref[...] = x_ref[...] + g_ref[...] * y


def _out_proj(oa, ob, cola, colb, wa, wb, x2, gate):
    s, d = x2.shape
    ka = wa.shape[0]
    tm = ROW_TILE
    return pl.pallas_call(
        _out_proj_kernel,
        grid=(s // tm,),
        in_specs=[pl.BlockSpec((tm, ka), lambda i: (i, cola)), pl.BlockSpec((tm, ka), lambda i: (i, colb)),
                  pl.BlockSpec((ka, d), lambda i: (0, 0)), pl.BlockSpec((ka, d), lambda i: (0, 0)),
                  pl.BlockSpec((tm, d), lambda i: (i, 0)), pl.BlockSpec((1, d), lambda i: (0, 0))],
        out_specs=pl.BlockSpec((tm, d), lambda i: (i, 0)),
        out_shape=jax.ShapeDtypeStruct((s, d), F32),
        compiler_params=_cparams("parallel"),
        name="out_proj",
    )(oa, ob, wa, wb, x2, gate)


def _mla_prep_kernel(x_ref, ng_ref, sc_ref, sh_ref, win_ref, pos_ref, inv_ref, gqa_ref, gkva_ref, wuq_ref, wuk_ref,
                     wuv_ref, gq_ref, gk_ref, gkr_ref, qt_ref, k_ref, vt_ref):
    shp = (PREP_TILE, LANE)
    act = _norm_mod(x_ref[...], ng_ref[...], sc_ref[...], sh_ref[...]).astype(BF16)
    proj = jnp.dot(act, win_ref[...], preferred_element_type=F32)
    lane = _lane_iota(shp)
    nope = lane < QK_NOPE
    rope = (lane >= QK_NOPE) & (lane < QK_NOPE + QK_ROPE)
    ref_ones = jnp.where((lane >= REF_ROW) & (lane < REF_ROW + 3), 1.0, 0.0)
    ang = pos_ref[...] * inv_ref[...]
    cos = jnp.where(rope, jnp.cos(ang), 1.0)
    sin = jnp.where(rope, jnp.sin(ang), 0.0)

    def rope32(x):
        half = QK_ROPE // 2
        rot = jnp.where(lane < QK_NOPE + half, -pltpu.roll(x, LANE - half, 1), pltpu.roll(x, half, 1))
        return x * cos + rot * sin

    def low_rank_norm(x, g):
        ms = jnp.mean(x * x, axis=-1, keepdims=True)
        return (x * lax.rsqrt(ms + EPS) * g).astype(BF16)

    nq = Q_LORA // LANE
    cq = low_rank_norm(proj[:, :Q_LORA], gqa_ref[...])
    ckv = low_rank_norm(proj[:, Q_LORA:Q_LORA + KV_LORA], gkva_ref[...])
    kr = proj[:, (nq + KV_LORA // LANE) * LANE:(nq + KV_LORA // LANE + 1) * LANE]
    k_rope = rope32(_head_rms(kr, gkr_ref[...], QK_ROPE))

    gq, gk = gq_ref[...], gk_ref[...]
    scale = (QK_NOPE + QK_ROPE) ** -0.5 * LOG2E
    pair = 2 * LANE
    for hp in range(MLA_HEADS // 2):
        cols = slice(hp * pair, (hp + 1) * pair)
        q2 = jnp.dot(cq, wuq_ref[:, cols], preferred_element_type=F32)
        k2 = jnp.dot(ckv, wuk_ref[:, cols], preferred_element_type=F32)
        v2 = jnp.dot(ckv, wuv_ref[:, cols], preferred_element_type=F32)
        for sub in range(2):
            head = 2 * hp + sub
            sl = slice(head * LANE, (head + 1) * LANE)
            half = slice(sub * LANE, (sub + 1) * LANE)
            x = q2[:, half]
            ss_n = jnp.sum(jnp.where(nope, x * x, 0.0), axis=-1, keepdims=True)
            ss_r = jnp.sum(jnp.where(rope, x * x, 0.0), axis=-1, keepdims=True)
            inv_rms = jnp.where(nope, lax.rsqrt(ss_n * (1.0 / QK_NOPE) + EPS),
                                lax.rsqrt(ss_r * (1.0 / QK_ROPE) + EPS))
            qt_ref[sl, :] = _t_bf16(rope32(x * inv_rms * gq) * scale)
            kn = _head_rms(k2[:, half], gk, QK_NOPE)
            k_ref[:, sl] = (kn + k_rope + ref_ones).astype(BF16)
            vt_ref[head, 0] = _t_bf16(jnp.where(lane == SUM_ROW, 1.0, v2[:, half]), V_ROWS)


def _mla_prep(x2, norm_g, sc, sh, w_in, posf, inv128, gqa, gkva, wuq, wuk, wuv, gq, gk, gkr):
    s, d = x2.shape
    tm = PREP_TILE
    assert tm == KV_TILE

    def full(a):
        return pl.BlockSpec(a.shape, lambda i: (0, 0))

    outs = [_feat_major(MLA_HEADS, s, tm),
            (pl.BlockSpec((tm, MLA_HEADS * LANE), lambda i: (i, 0)),
             jax.ShapeDtypeStruct((s, MLA_HEADS * LANE), BF16)),
            _value_tiles(MLA_HEADS, s, tm, KV_TILE, V_ROWS)]
    head = (norm_g, sc, sh, w_in)
    tail = (inv128, gqa, gkva, wuq, wuk, wuv, gq, gk, gkr)
    return pl.pallas_call(
        _mla_prep_kernel,
        grid=(s // tm,),
        in_specs=[pl.BlockSpec((tm, d), lambda i: (i, 0))] + [full(a) for a in head]
                 + [pl.BlockSpec((tm, 1), lambda i: (i, 0))] + [full(a) for a in tail],
        out_specs=[o[0] for o in outs],
        out_shape=[o[1] for o in outs],
        compiler_params=_cparams("parallel"),
        name="mla_prep",
    )(x2, *head, posf, *tail)


def _rank_lt(v, k):
    n = v.shape[0]
    row = _row_iota(v.shape)
    rank = jnp.zeros(v.shape, F32)
    for b in range(n):
        vb = v[b:b + 1, :]
        rank = rank + jnp.where((vb > v) | ((vb == v) & (row > b)), 1.0, 0.0)
    return rank < k


def _top_rows(v, k):
    rowf = _row_iota(v.shape).astype(F32)
    chosen = jnp.zeros(v.shape, F32)
    for _ in range(k):
        mx = jnp.max(v, axis=0, keepdims=True)
        idx = jnp.min(jnp.where(v == mx, rowf, float(v.shape[0])), axis=0, keepdims=True)
        hit = rowf == idx
        chosen = jnp.where(hit, 1.0, chosen)
        v = jnp.where(hit, -jnp.inf, v)
    return chosen > 0.0


def _moe_route_kernel(x_ref, g_ref, sc_ref, sh_ref, wr_ref, rb_ref, h_ref, pos_ref, wt_ref, cnt_ref):
    tm = ROW_TILE
    h = _norm_mod(x_ref[...], g_ref[...], sc_ref[...], sh_ref[...])
    h_ref[...] = h.astype(BF16)
    logits = jnp.dot(h, wr_ref[...], precision=HIGHEST, preferred_element_type=F32)
    lt = logits.T[:N_EXPERTS]
    scores = jax.nn.sigmoid(lt)
    sel = scores + rb_ref[...]

    per = N_EXPERTS // N_GROUPS
    grp = sel.reshape(N_GROUPS, per, tm)
    sub = lax.broadcasted_iota(jnp.int32, grp.shape, 1)
    m1 = jnp.max(grp, axis=1, keepdims=True)
    first = jnp.min(jnp.where(grp == m1, sub, per), axis=1, keepdims=True)
    m2 = jnp.max(jnp.where(sub == first, -jnp.inf, grp), axis=1, keepdims=True)
    gscore = (m1 + m2).reshape(N_GROUPS, tm)
    gmask = _rank_lt(gscore, TOPK_GROUPS)
    emask = jnp.broadcast_to(gmask.reshape(N_GROUPS, 1, tm), grp.shape).reshape(N_EXPERTS, tm)
    chosen = _top_rows(jnp.where(emask, sel, NEG), TOP_K)

    w = jnp.where(chosen, scores, 0.0)
    wt_ref[...] = w / jnp.sum(w, axis=0, keepdims=True) * ROUTED_SCALE

    upper = jnp.where(_row_iota((tm, tm)) <= _lane_iota((tm, tm)), 1.0, 0.0).astype(BF16)
    incl = jnp.dot(jnp.where(chosen, 1.0, 0.0).astype(BF16), upper, preferred_element_type=F32)
    pos_ref[...] = jnp.where(chosen, incl - 1.0, -1.0)
    cnt_ref[0] = jnp.broadcast_to(incl[:, tm - 1:tm], (N_EXPERTS, LANE))


def _moe_route(x2, g, sc, sh, w_router_pad, router_bias_col):
    s, d = x2.shape
    tm = ROW_TILE
    vec = pl.BlockSpec((1, d), lambda i: (0, 0))
    et = pl.BlockSpec((N_EXPERTS, tm), lambda i: (0, i))
    return pl.pallas_call(
        _moe_route_kernel,
        grid=(s // tm,),
        in_specs=[pl.BlockSpec((tm, d), lambda i: (i, 0)), vec, vec, vec,
                  pl.BlockSpec((d, LANE), lambda i: (0, 0)),
                  pl.BlockSpec((N_EXPERTS, 1), lambda i: (0, 0))],
        out_specs=[pl.BlockSpec((tm, d), lambda i: (i, 0)), et, et,
                   pl.BlockSpec((1, N_EXPERTS, LANE), lambda i: (i, 0, 0))],
        out_shape=[jax.ShapeDtypeStruct((s, d), BF16), jax.ShapeDtypeStruct((N_EXPERTS, s), F32),
                   jax.ShapeDtypeStruct((N_EXPERTS, s), F32),
                   jax.ShapeDtypeStruct((s // tm, N_EXPERTS, LANE), F32)],
        compiler_params=_cparams("parallel"),
        name="moe_route",
    )(x2, g, sc, sh, w_router_pad, router_bias_col)


def _moe_kernel(cnt_ref, x_ref, h_ref, pos_ref, wt_ref, wg_ref, wu_ref, wd_ref, sg_ref, su_ref, sd_ref,
                g2_ref, o_ref, acc_ref):
    i = pl.program_id(0)
    e = pl.program_id(1)
    tm = ROW_TILE
    r = MOE_CHUNK

    @pl.when(e == 0)
    def _():
        h = h_ref[...]
        a = jnp.dot(h, sg_ref[...], preferred_element_type=F32)
        a = a * jax.nn.sigmoid(a) * jnp.dot(h, su_ref[...], preferred_element_type=F32)
        acc_ref[...] = jnp.dot(a.astype(BF16), sd_ref[...], preferred_element_type=F32)

    first = e * MOE_EXPERTS_PER_STEP
    n = cnt_ref[i * N_EXPERTS + first]
    for k in range(1, MOE_EXPERTS_PER_STEP):
        n = jnp.maximum(n, cnt_ref[i * N_EXPERTS + first + k])
    prows = [pos_ref[pl.ds(first + k, 1), :] for k in range(MOE_EXPERTS_PER_STEP)]
    wrows = [wt_ref[pl.ds(first + k, 1), :] for k in range(MOE_EXPERTS_PER_STEP)]

    def chunk(c, _):
        slot = (_row_iota((r, tm)) + c * r).astype(F32)
        hits = [prow == slot for prow in prows]
        onehot = jnp.concatenate([jnp.where(hit, 1.0, 0.0).astype(BF16) for hit in hits], axis=0)
        xg = jnp.dot(onehot, h_ref[...], preferred_element_type=F32).astype(BF16)
        ys = []
        for k in range(MOE_EXPERTS_PER_STEP):
            xk = xg[k * r:(k + 1) * r]
            a = jnp.dot(xk, wg_ref[k], preferred_element_type=F32)
            a = a * jax.nn.sigmoid(a) * jnp.dot(xk, wu_ref[k], preferred_element_type=F32)
            y = jnp.dot(a.astype(BF16), wd_ref[k].astype(BF16), preferred_element_type=F32)
            wr = jnp.sum(jnp.where(hits[k], wrows[k], 0.0), axis=-1, keepdims=True)
            ys.append((y * wr).astype(BF16))
        acc_ref[...] += _dot_tn(onehot, jnp.concatenate(ys, axis=0))
        return 0

    lax.fori_loop(0, (n + r - 1) // r, chunk, 0)

    @pl.when(e == N_EXPERTS // MOE_EXPERTS_PER_STEP - 1)
    def _():
        o_ref[...] = x_ref[...] + g2_ref[...] * acc_ref[...]


def _moe_experts(counts, x2, h, pos_t, w_t, wg, wu, wd, sg, su, sd, g2):
    s, d = x2.shape
    tm = ROW_TILE
    ff = wg.shape[2]
    tile = pl.BlockSpec((tm, d), lambda i, e, c: (i, 0))
    et = pl.BlockSpec((N_EXPERTS, tm), lambda i, e, c: (0, i))

    def const(a):
        return pl.BlockSpec(a.shape, lambda i, e, c: (0,) * a.ndim)

    per = MOE_EXPERTS_PER_STEP
    grid_spec = pltpu.PrefetchScalarGridSpec(
        num_scalar_prefetch=1,
        grid=(s // tm, N_EXPERTS // per),
        in_specs=[tile, tile, et, et,
                  pl.BlockSpec((per, d, ff), lambda i, e, c: (e, 0, 0)),
                  pl.BlockSpec((per, d, ff), lambda i, e, c: (e, 0, 0)),
                  pl.BlockSpec((per, ff, d), lambda i, e, c: (e, 0, 0)),
                  const(sg), const(su), const(sd), const(g2)],
        out_specs=tile,
        scratch_shapes=[pltpu.VMEM((tm, d), F32)],
    )
    return pl.pallas_call(
        _moe_kernel,
        grid_spec=grid_spec,
        out_shape=jax.ShapeDtypeStruct((s, d), F32),
        compiler_params=_cparams("parallel", "arbitrary"),
        name="moe_experts",
    )(counts, x2, h, pos_t, w_t, wg, wu, wd, sg, su, sd, g2)


def _pad_lanes(v, width=LANE, offset=0):
    out = jnp.zeros((1, width), F32)
    return out.at[0, offset:offset + v.shape[0]].set(v.astype(F32))


def _head_cols(w, n_heads, dim):
    d = w.shape[0]
    w3 = w.reshape(d, n_heads, dim)
    return jnp.pad(w3, ((0, 0), (0, 0), (0, LANE - dim))).reshape(d, n_heads * LANE)


def _hybrid_w_in(w_in):
    d = w_in.shape[0]
    nq = NSA_HEADS * HEAD_DIM
    nkv = 6 * NSA_GROUPS * HEAD_DIM
    ng = 3 * NSA_HEADS
    nf = 3 * FOX_HEADS * HEAD_DIM
    c0, c1, c2, c3 = nq, nq + nkv, nq + nkv + ng, nq + nkv + ng + nf
    gates = w_in[:, c1:c2].reshape(d, NSA_GROUPS, 3 * NSA_HPG)
    gates = jnp.pad(gates, ((0, 0), (0, 0), (0, LANE - 3 * NSA_HPG))).reshape(d, NSA_GROUPS * LANE)
    ff = jnp.pad(w_in[:, c3:], ((0, 0), (0, 2 * LANE - FOX_HEADS)))
    return jnp.concatenate([
        _head_cols(w_in[:, :c0], NSA_HEADS, HEAD_DIM),
        _head_cols(w_in[:, c0:c1], 6 * NSA_GROUPS, HEAD_DIM),
        _head_cols(w_in[:, c2:c3], 3 * FOX_HEADS, HEAD_DIM),
        gates, ff], axis=1).astype(BF16)


def _pad_head_rows(w, n_heads, dim):
    d = w.shape[1]
    w3 = w.reshape(n_heads, dim, d)
    return jnp.pad(w3, ((0, 0), (0, LANE - dim), (0, 0))).reshape(n_heads * LANE, d).astype(BF16)


def _rope_inv(dim, offset):
    inv = ROPE_THETA ** (-jnp.arange(0, dim, 2, dtype=F32) / dim)
    return _pad_lanes(jnp.concatenate([inv, inv]), offset=offset)


def _hybrid_mixer(x2, posf, mods, norm_g, w_in, fox_f_bias, nsa_q_norm, nsa_k_norm, nsa_cmp_pe, nsa_w_cmp,
                  fox_q_norm, fox_k_norm, w_out):
    sh1, sc1, g1 = mods
    (qnt, kct, vct, ks, vst, kw, vwt, gates, fqt, fk, fvt, cedge) = _hy_prep(
        x2, norm_g, sc1, sh1, _hybrid_w_in(w_in), posf, _rope_inv(HEAD_DIM, 0), _pad_lanes(nsa_q_norm),
        _pad_lanes(nsa_k_norm),
        _pad_lanes(fox_q_norm), _pad_lanes(fox_k_norm), _pad_lanes(fox_f_bias))
    kc, vc_t = _compress(kct, vct, nsa_w_cmp, nsa_cmp_pe, nsa_k_norm)
    o_a = _nsa_attention(qnt, kc, vc_t, ks, vst, kw, vwt, gates)
    slack = (2.0 * HEAD_DIM ** 0.5 * LOG2E) * jnp.max(jnp.abs(fox_q_norm)) * jnp.max(jnp.abs(fox_k_norm))
    edges = (cedge[:, 0, :FOX_HEADS].T, cedge[:, 1, :FOX_HEADS].T)
    o_b = _causal_attention(fqt, fk, fvt, Q_TILE_FOX, edges, slack.reshape(1))
    half = NSA_HEADS * HEAD_DIM
    wa = _pad_head_rows(w_out[:half], NSA_HEADS, HEAD_DIM)
    wb = _pad_head_rows(w_out[half:], FOX_HEADS, HEAD_DIM)
    return _out_proj(o_a, o_b, 0, 0, wa, wb, x2, g1)


def _mla_mixer(x2, posf, mods, norm_g, w_in, q_a_norm, kv_a_norm, w_uq, w_ukv, qn_norm, kn_norm, qr_norm,
               kr_norm, w_out):
    sh1, sc1, g1 = mods
    d = x2.shape[1]
    w_kr = jnp.zeros((d, LANE), F32).at[:, QK_NOPE:QK_NOPE + QK_ROPE].set(w_in[:, Q_LORA + KV_LORA:])
    w_in_p = jnp.concatenate([w_in[:, :Q_LORA + KV_LORA], w_kr], axis=1).astype(BF16)
    hq = QK_NOPE + QK_ROPE
    wuq = _head_cols(w_uq, MLA_HEADS, hq).astype(BF16)
    wkv3 = w_ukv.reshape(KV_LORA, MLA_HEADS, QK_NOPE + V_HEAD)
    wuk = _head_cols(wkv3[:, :, :QK_NOPE].reshape(KV_LORA, -1), MLA_HEADS, QK_NOPE).astype(BF16)
    wuv = _head_cols(wkv3[:, :, QK_NOPE:].reshape(KV_LORA, -1), MLA_HEADS, V_HEAD).astype(BF16)
    gq = _pad_lanes(jnp.concatenate([qn_norm, qr_norm]))
    qt, k, vt = _mla_prep(x2, norm_g, sc1, sh1, w_in_p, posf, _rope_inv(QK_ROPE, QK_NOPE),
                          q_a_norm.reshape(1, -1).astype(F32),
                          kv_a_norm.reshape(1, -1).astype(F32), wuq, wuk, wuv, gq, _pad_lanes(kn_norm),
                          _pad_lanes(kr_norm, offset=QK_NOPE))
    o = _causal_attention(qt, k, vt, Q_TILE_MLA)
    w_pad = _pad_head_rows(w_out, MLA_HEADS, V_HEAD)
    half = w_pad.shape[0] // 2
    return _out_proj(o, o, 0, 1, w_pad[:half], w_pad[half:], x2, g1)


def _moe_ffn(x2, mods, norm_g, w_router, router_bias, w_gate, w_up, w_down, ws_gate, ws_up, ws_down):
    sh2, sc2, g2 = mods
    w_r = jnp.pad(w_router.astype(F32), ((0, 0), (0, LANE - N_EXPERTS)))
    h, pos_t, w_t, cnt = _moe_route(x2, norm_g, sc2, sh2, w_r, router_bias.reshape(N_EXPERTS, 1).astype(F32))
    counts = cnt[:, :, 0].astype(jnp.int32).reshape(-1)
    return _moe_experts(counts, x2, h, pos_t, w_t, w_gate.astype(BF16), w_up.astype(BF16),
                        w_down.astype(F32), ws_gate.astype(BF16), ws_up.astype(BF16), ws_down.astype(BF16), g2)


def kernel(x, c, positions, norm_attn, norm_ffn, w_ada, b_ada, hy_w_in, fox_f_bias, nsa_q_norm, nsa_k_norm, nsa_cmp_pe, nsa_w_cmp, fox_q_norm, fox_k_norm, hy_w_out, mla_w_in, mla_q_a_norm, mla_kv_a_norm, mla_w_uq, mla_w_ukv, mla_qn_norm, mla_kn_norm, mla_qr_norm, mla_kr_norm, mla_w_out, moe_w_router, moe_router_bias, moe_w_gate, moe_w_up, moe_w_down, moe_ws_gate, moe_ws_up, moe_ws_down):
    b, s, d = x.shape
    assert b == 1 and s % KV_TILE == 0 and s >= WINDOW + Q_TILE_NSA
    depth = w_ada.shape[0]
    x2 = x.reshape(s, d).astype(F32)
    posf = positions.reshape(s, 1).astype(F32)
    mod = _ada_mod(c.astype(F32), w_ada.astype(F32), b_ada.astype(F32))

    for layer in range(depth):
        m = [mod[layer, :, k * d:(k + 1) * d] for k in range(6)]
        i = layer // 2
        g_attn = norm_attn[layer].reshape(1, d).astype(F32)
        if layer % 2 == 0:
            x2 = _hybrid_mixer(x2, posf, m[0:3], g_attn, hy_w_in[i], fox_f_bias[i], nsa_q_norm[i],
                               nsa_k_norm[i], nsa_cmp_pe[i], nsa_w_cmp[i], fox_q_norm[i], fox_k_norm[i],
                               hy_w_out[i])
        else:
            x2 = _mla_mixer(x2, posf, m[0:3], g_attn, mla_w_in[i], mla_q_a_norm[i], mla_kv_a_norm[i],
                            mla_w_uq[i], mla_w_ukv[i], mla_qn_norm[i], mla_kn_norm[i], mla_qr_norm[i],
                            mla_kr_norm[i], mla_w_out[i])
        x2 = _moe_ffn(x2, m[3:6], norm_ffn[layer].reshape(1, d).astype(F32), moe_w_router[layer],
                      moe_router_bias[layer], moe_w_gate[layer], moe_w_up[layer], moe_w_down[layer],
                      moe_ws_gate[layer], moe_ws_up[layer], moe_ws_down[layer])
    return x2.reshape(b, s, d)
```

```python
import functools

import numpy as np
import jax
import jax.numpy as jnp
from jax import lax
from jax.experimental import pallas as pl
from jax.experimental.pallas import tpu as pltpu

F32 = jnp.float32
BF16 = jnp.bfloat16
HIGHEST = lax.Precision.HIGHEST

LANE = 128
VMEM_LIMIT_BYTES = 56 * 1024 * 1024

HEAD_DIM = 64
NSA_HEADS = 8
NSA_GROUPS = 2
NSA_HPG = NSA_HEADS // NSA_GROUPS
CMP_LEN = 32
CMP_STRIDE = 16
SLC_BLOCK = 64
SLC_TOPK = 16
WINDOW = 512
FOX_HEADS = 8
MLA_HEADS = 16
Q_LORA = 384
KV_LORA = 256
QK_NOPE = 64
QK_ROPE = 32
V_HEAD = 64
N_EXPERTS = 64
TOP_K = 8
N_GROUPS = 8
TOPK_GROUPS = 4
EXPERT_FF = 256
ROUTED_SCALE = 2.5
ROPE_THETA = 10000.0
EPS = 1e-6
NEG = -1e30
BIG = 1e6

ROW_TILE = 512
PREP_TILE = 512
GATE_ROWS = 16
Q_TILE_NSA = 256
KV_TILE = 512
Q_TILE_FOX = 1024
Q_TILE_MLA = 2048
Q_STRIP = 256
WIN_TILE = 256
SUPER_BLOCKS = 32
NSA_PREFIX_PARTS = 8
SUM_ROW = 64
V_ROWS = 80
REF_ROW = 104
REF_SLAB = 96
EXP_GUARD = 100.0
PROJ_LOOKAHEAD = 2
SCORE_LOOKAHEAD = 4
SKIP_MARGIN = 160.0
LOG2E = 1.4426950408889634
MOE_CHUNK = 128
MOE_EXPERTS_PER_STEP = 8
MOE_MLP_LOOKAHEAD = 2

HY_Q0 = 0
HY_KV0 = HY_Q0 + NSA_HEADS
HY_F0 = HY_KV0 + 6 * NSA_GROUPS
HY_G0 = HY_F0 + 3 * FOX_HEADS
HY_FF = HY_G0 + NSA_GROUPS
HY_BLOCKS = HY_FF + 2


def _cparams(*sem):
    return pltpu.CompilerParams(dimension_semantics=sem, vmem_limit_bytes=VMEM_LIMIT_BYTES)


def _lane_iota(shape):
    return lax.broadcasted_iota(jnp.int32, shape, len(shape) - 1)


def _row_iota(shape):
    return lax.broadcasted_iota(jnp.int32, shape, len(shape) - 2)


def _dot_tn(a, b):
    return lax.dot_general(a, b, (((0,), (0,)), ((), ())), preferred_element_type=F32)


def _ada_kernel(c_ref, w_ref, b_ref, o_ref):
    c = c_ref[...]
    cond = c * jax.nn.sigmoid(c)
    o_ref[0] = jnp.dot(cond, w_ref[0], precision=HIGHEST, preferred_element_type=F32) + b_ref[0]


def _ada_mod(c, w_ada, b_ada):
    depth, d, n = w_ada.shape
    tn = 768
    c8 = jnp.broadcast_to(c.reshape(1, d), (8, d))
    out = pl.pallas_call(
        _ada_kernel,
        grid=(depth, n // tn),
        in_specs=[pl.BlockSpec((8, d), lambda l, j: (0, 0)),
                  pl.BlockSpec((1, d, tn), lambda l, j: (l, 0, j)),
                  pl.BlockSpec((1, 1, tn), lambda l, j: (l, 0, j))],
        out_specs=pl.BlockSpec((1, 8, tn), lambda l, j: (l, 0, j)),
        out_shape=jax.ShapeDtypeStruct((depth, 8, n), F32),
        compiler_params=_cparams("parallel", "parallel"),
        name="ada_mod",
    )(c8, w_ada, b_ada.reshape(depth, 1, n))
    return out[:, 0:1, :]


def _norm_mod(x, g, sc, sh):
    ms = jnp.mean(x * x, axis=-1, keepdims=True)
    return (x * lax.rsqrt(ms + EPS) * g) * (1.0 + sc) + sh


def _head_rms(x, gain, n_real):
    ss = jnp.sum(x * x, axis=-1, keepdims=True)
    return x * lax.rsqrt(ss * (1.0 / n_real) + EPS) * gain


def _rope64(x, cos, sin):
    lane = _lane_iota(x.shape)
    rot = jnp.where(lane < 32, -pltpu.roll(x, LANE - 32, 1), pltpu.roll(x, 32, 1))
    return x * cos + rot * sin


def _t_bf16(x, rows=LANE):
    return x.T[:rows].astype(BF16)


def _split3(c):
    hi = c.astype(BF16).astype(F32)
    r1 = c - hi
    mid = r1.astype(BF16).astype(F32)
    lo = (r1 - mid).astype(BF16).astype(F32)
    return hi, mid, lo


def _hy_prep_kernel(x_ref, ng_ref, sc_ref, sh_ref, w_ref, pos_ref, inv_ref, gq_ref, gk_ref, gfq_ref, gfk_ref,
                    fb_ref, qnt_ref, kct_ref, vct_ref, ks_ref, vst_ref, kw_ref, vwt_ref, gate_ref,
                    fqt_ref, fk_ref, fvt_ref, cedge_ref, carry_ref):
    i = pl.program_id(0)
    tm = PREP_TILE
    shp = (tm, LANE)
    lane = _lane_iota(shp)

    act = _norm_mod(x_ref[...], ng_ref[...], sc_ref[...], sh_ref[...]).astype(BF16)
    pairs = {}

    def blk(b):
        for p in range(b // 2, min(b // 2 + 1 + PROJ_LOOKAHEAD, HY_BLOCKS // 2)):
            if p not in pairs:
                pairs[p] = jnp.dot(act, w_ref[:, p * 2 * LANE:(p + 1) * 2 * LANE], preferred_element_type=F32)
        return pairs[b // 2][:, (b % 2) * LANE:(b % 2 + 1) * LANE]

    ang = pos_ref[...] * inv_ref[...]
    real = lane < HEAD_DIM
    cos = jnp.where(real, jnp.cos(ang), 1.0)
    sin = jnp.where(real, jnp.sin(ang), 0.0)
    gq, gk, gfq, gfk = gq_ref[...], gk_ref[...], gfq_ref[...], gfk_ref[...]
    scale = HEAD_DIM ** -0.5 * LOG2E
    ones_row = lane == SUM_ROW
    ref_ones = jnp.where((lane >= REF_ROW) & (lane < REF_ROW + 3), 1.0, 0.0)

    qnt_ref[...] = jnp.concatenate(
        [_t_bf16(_rope64(_head_rms(blk(HY_Q0 + h), gq, HEAD_DIM), cos, sin) * scale) for h in range(NSA_HEADS)],
        axis=0)

    row = _row_iota(shp) + i * tm
    onehot = jnp.where(lane - HEAD_DIM == ((row // SLC_BLOCK) % SUPER_BLOCKS), 1.0, 0.0)
    for g in range(NSA_GROUPS):
        def kv(r):
            return blk(HY_KV0 + r * NSA_GROUPS + g)
        sl = slice(g * LANE, (g + 1) * LANE)
        kct_ref[g] = _rope64(kv(0), cos, sin)[:, :HEAD_DIM].astype(BF16)
        vct_ref[g] = kv(1)[:, :HEAD_DIM].astype(BF16)
        ks = _rope64(_head_rms(kv(2), gk, HEAD_DIM), cos, sin)
        ks_ref[:, sl] = (ks + onehot + ref_ones).astype(BF16)
        vst_ref[g, 0] = _t_bf16(jnp.where(ones_row, 1.0, kv(3)), V_ROWS)
        kw_ref[:, sl] = (_rope64(_head_rms(kv(4), gk, HEAD_DIM), cos, sin) + ref_ones).astype(BF16)
        vwt = _t_bf16(jnp.where(ones_row, 1.0, kv(5)), V_ROWS)
        for cidx in range(tm // WIN_TILE):
            vwt_ref[g, cidx] = vwt[:, cidx * WIN_TILE:(cidx + 1) * WIN_TILE]
        gate_ref[g] = jax.nn.sigmoid(blk(HY_G0 + g)).T[:GATE_ROWS]

    @pl.when(i == 0)
    def _():
        carry_ref[...] = jnp.zeros_like(carry_ref)

    z = blk(HY_FF) + fb_ref[...]
    logf = jnp.minimum(z, 0.0) - jnp.log1p(jnp.exp(-jnp.abs(z)))
    tri = jnp.where(_row_iota((tm, tm)) >= _lane_iota((tm, tm)), 1.0, 0.0).astype(F32)
    cum = jnp.dot(tri, logf, precision=HIGHEST, preferred_element_type=F32) + carry_ref[...]
    carry_ref[...] = cum[tm - 1:tm, :]
    cedge_ref[0] = jnp.concatenate([cum[0:1] * LOG2E, cum[tm - 1:tm] * LOG2E, jnp.zeros((6, LANE), F32)], axis=0)

    fqs, fks, fvs = [], [], []
    for h in range(FOX_HEADS):
        c = jnp.broadcast_to(cum[:, h:h + 1], shp) * LOG2E
        hi, mid, lo = _split3(c)
        fq = _head_rms(blk(HY_F0 + h), gfq, HEAD_DIM) * scale
        fq = jnp.where(real, fq, jnp.where(lane == 64, hi, jnp.where(lane == 65, mid, jnp.where(
            lane == 66, lo, jnp.where(lane < 70, 1.0, 0.0)))))
        fk = _head_rms(blk(HY_F0 + FOX_HEADS + h), gfk, HEAD_DIM)
        fk = jnp.where(real, fk, jnp.where(lane < 67, 1.0, jnp.where(lane == 67, -hi, jnp.where(
            lane == 68, -mid, jnp.where(lane == 69, -lo, ref_ones)))))
        fqs.append(_t_bf16(fq))
        fks.append(fk.astype(BF16))
        fvs.append(_t_bf16(jnp.where(ones_row, 1.0, blk(HY_F0 + 2 * FOX_HEADS + h)), V_ROWS))
    fqt_ref[...] = jnp.concatenate(fqs, axis=0)
    fk_ref[...] = jnp.concatenate(fks, axis=1)
    fvt_ref[...] = jnp.stack(fvs)[:, None]


def _feat_major(heads, s, tm):
    return (pl.BlockSpec((heads * LANE, tm), lambda i: (0, i)),
            jax.ShapeDtypeStruct((heads * LANE, s), BF16))


def _value_tiles(heads, s, tm, tk, rows=LANE):
    return (pl.BlockSpec((heads, tm // tk, rows, tk), lambda i: (0, i, 0, 0)),
            jax.ShapeDtypeStruct((heads, s // tk, rows, tk), BF16))


def _pad_feature_rows(o):
    return jnp.concatenate([o, jnp.zeros((LANE - HEAD_DIM, o.shape[1]), o.dtype)], axis=0)


def _hy_prep(x2, norm_g, sc, sh, w_in, posf, inv128, gq, gk, gfq, gfk, fbias):
    s, d = x2.shape
    tm = PREP_TILE
    assert tm == KV_TILE and w_in.shape == (d, HY_BLOCKS * LANE) and HY_BLOCKS % 2 == 0
    vec = pl.BlockSpec((1, LANE), lambda i: (0, 0))
    dvec = pl.BlockSpec((1, d), lambda i: (0, 0))

    def rows(nb):
        return (pl.BlockSpec((tm, nb * LANE), lambda i: (i, 0)), jax.ShapeDtypeStruct((s, nb * LANE), BF16))

    tok = (pl.BlockSpec((NSA_GROUPS, tm, HEAD_DIM), lambda i: (0, i, 0)),
           jax.ShapeDtypeStruct((NSA_GROUPS, s, HEAD_DIM), BF16))
    gate = (pl.BlockSpec((NSA_GROUPS, GATE_ROWS, tm), lambda i: (0, 0, i)),
            jax.ShapeDtypeStruct((NSA_GROUPS, GATE_ROWS, s), F32))
    outs = [_feat_major(NSA_HEADS, s, tm), tok, tok, rows(NSA_GROUPS),
            _value_tiles(NSA_GROUPS, s, tm, KV_TILE, V_ROWS),
            rows(NSA_GROUPS), _value_tiles(NSA_GROUPS, s, tm, WIN_TILE, V_ROWS), gate,
            _feat_major(FOX_HEADS, s, tm), rows(FOX_HEADS), _value_tiles(FOX_HEADS, s, tm, KV_TILE, V_ROWS),
            (pl.BlockSpec((1, 8, LANE), lambda i: (i, 0, 0)), jax.ShapeDtypeStruct((s // tm, 8, LANE), F32))]
    return pl.pallas_call(
        _hy_prep_kernel,
        grid=(s // tm,),
        in_specs=[pl.BlockSpec((tm, d), lambda i: (i, 0)), dvec, dvec, dvec,
                  pl.BlockSpec((d, HY_BLOCKS * LANE), lambda i: (0, 0), pipeline_mode=pl.Buffered(1)),
                  pl.BlockSpec((tm, 1), lambda i: (i, 0)), vec, vec, vec, vec, vec, vec],
        out_specs=[o[0] for o in outs],
        out_shape=[o[1] for o in outs],
        scratch_shapes=[pltpu.VMEM((1, LANE), F32)],
        compiler_params=_cparams("arbitrary"),
        name="hybrid_prep",
    )(x2, norm_g, sc, sh, w_in, posf, inv128, gq, gk, gfq, gfk, fbias)


def _compress_kernel(kc_ref, vc_ref, wk_ref, wv_ref, pek_ref, pev_ref, gk_ref, ko_ref, vo_ref):
    half = CMP_STRIDE * HEAD_DIM

    def comp(ch_ref, w_ref, pe_ref):
        ch = ch_ref[0]
        nc = ch.shape[0]
        a = jnp.dot(ch, w_ref[:half], preferred_element_type=F32)
        b = jnp.dot(ch, w_ref[half:], preferred_element_type=F32)
        nxt = pltpu.roll(b, nc - 1, 0)
        pe = jnp.dot(jnp.broadcast_to(pe_ref[...], (8, 2 * half)).astype(BF16), w_ref[...],
                     preferred_element_type=F32)[0:1]
        return a + nxt + pe

    ko_ref[0] = _head_rms(comp(kc_ref, wk_ref, pek_ref), gk_ref[...], HEAD_DIM).astype(BF16)
    vo_ref[0] = comp(vc_ref, wv_ref, pev_ref).T.astype(BF16)


def _compress(kct, vct, w_cmp, cmp_pe, k_norm):
    g, s, _ = kct.shape
    nc = s // CMP_STRIDE
    wide = CMP_STRIDE * HEAD_DIM
    kch = kct.reshape(g, nc, wide)
    vch = vct.reshape(g, nc, wide)
    w_pad = jnp.pad(w_cmp, ((0, 0), (0, 0), (0, LANE - HEAD_DIM))).astype(BF16)
    ch = pl.BlockSpec((1, nc, wide), lambda i: (i, 0, 0))
    wspec = pl.BlockSpec((2 * wide, LANE), lambda i: (0, 0))
    pespec = pl.BlockSpec((1, 2 * wide), lambda i: (0, 0))
    return pl.pallas_call(
        _compress_kernel,
        grid=(g,),
        in_specs=[ch, ch, wspec, wspec, pespec, pespec, pl.BlockSpec((1, LANE), lambda i: (0, 0))],
        out_specs=[pl.BlockSpec((1, nc, LANE), lambda i: (i, 0, 0)),
                   pl.BlockSpec((1, LANE, nc), lambda i: (i, 0, 0))],
        out_shape=[jax.ShapeDtypeStruct((g, nc, LANE), BF16), jax.ShapeDtypeStruct((g, LANE, nc), BF16)],
        compiler_params=_cparams("parallel"),
        name="nsa_compress",
    )(kch, vch, w_pad[0], w_pad[1], cmp_pe[0].reshape(1, 2 * wide).astype(F32),
      cmp_pe[1].reshape(1, 2 * wide).astype(F32), _pad_lanes(k_norm))


def _masked_softmax_t(s, mask):
    s = jnp.where(mask, s, NEG)
    m = jnp.max(s, axis=0, keepdims=True)
    e = jnp.exp2(s - m)
    inv = 1.0 / jnp.maximum(jnp.sum(e, axis=0, keepdims=True), 1e-30)
    return e, jnp.where(m > 0.5 * NEG, inv, 0.0)


def _online_steps(steps, ms, acc_ref):
    ms = list(ms)

    def scores(step):
        k_tile, qa, _, c, mask = step
        s = jnp.dot(k_tile, qa, preferred_element_type=F32)
        if mask is not None:
            s = jnp.where(mask, s, NEG)
        return s, jnp.max(s, axis=0, keepdims=True)

    nxt = scores(steps[0])
    for idx, (_, _, vt, c, _) in enumerate(steps):
        sl = slice(c * Q_STRIP, (c + 1) * Q_STRIP)
        s, s_max = nxt
        if idx + 1 < len(steps):
            nxt = scores(steps[idx + 1])
        m_new = jnp.maximum(ms[c], s_max)
        a = jnp.exp2(ms[c] - m_new)
        p = jnp.exp2((s - m_new).astype(BF16))
        ms[c] = m_new
        acc_ref[:, sl] = a * acc_ref[:, sl] + jnp.dot(vt, p, preferred_element_type=F32)
    return tuple(ms)


def _m_init(n_strips):
    return tuple(jnp.full((1, Q_STRIP), NEG, F32) for _ in range(n_strips))


def _with_ref_rows(qa, m):
    hi, mid, lo = _split3(-m)
    r = _row_iota((LANE - REF_SLAB, Q_STRIP)) + REF_SLAB
    slab = jnp.where(r == REF_ROW, hi, jnp.where(r == REF_ROW + 1, mid, jnp.where(r == REF_ROW + 2, lo, 0.0)))
    return jnp.concatenate([qa[:REF_SLAB], slab.astype(BF16)], axis=0)


def _first_tile_max(k_tile, qa_strips, masks):
    return tuple(jnp.max(jnp.where(mask, jnp.dot(k_tile, qa, preferred_element_type=F32), NEG), axis=0, keepdims=True)
                 for qa, mask in zip(qa_strips, masks))


def _fast_steps(steps, state, acc_ref):
    state = list(state)
    depth, last = min(SCORE_LOOKAHEAD, len(steps)), {}
    for k, step in enumerate(steps):
        depth = min(depth, k - last.get(step[3], k - depth))
        last[step[3]] = k

    def scores(step):
        k_tile, qa, _, c, mask = step
        s = jnp.dot(k_tile, _with_ref_rows(qa, state[c][0]), preferred_element_type=F32)
        if mask is not None:
            s = jnp.where(mask, s, NEG)
        return s

    ahead = [scores(st) for st in steps[:depth]]
    for idx, (_, _, vt, c, _) in enumerate(steps):
        sl = slice(c * Q_STRIP, (c + 1) * Q_STRIP)
        s = ahead.pop(0)
        m, worst = state[c]
        cm = jnp.max(s, axis=0, keepdims=True)
        inc = jnp.maximum(cm, 0.0)
        state[c] = (m + inc, jnp.maximum(worst, cm))
        if idx + depth < len(steps):
            ahead.append(scores(steps[idx + depth]))
        p = jnp.exp2(s).astype(BF16)
        acc_ref[:, sl] = jnp.exp2(-inc) * (acc_ref[:, sl] + jnp.dot(vt, p, preferred_element_type=F32))
    return tuple(state)


def _flat(state):
    return tuple(x for pair in state for x in pair)


def _nest(flat):
    return tuple((flat[2 * c], flat[2 * c + 1]) for c in range(len(flat) // 2))


def _nsa_kernel(qt_ref, kc_ref, vct_ref, ks_ref, vst_ref, kw_ref, vwt_ref, gate_ref, ovt_ref, o_ref,
                qaug_ref, acc_ref, wacc_ref, *, n_sel):
    i = pl.program_id(1)
    tq = Q_TILE_NSA
    cols = NSA_HPG * tq
    qs = i * tq
    nc = kc_ref.shape[1]
    nslc = ovt_ref.shape[0]
    n_super = nslc // SUPER_BLOCKS

    qt = jnp.concatenate([qt_ref[h * LANE:(h + 1) * LANE, :] for h in range(NSA_HPG)], axis=1)
    tq_row = qs + (_lane_iota((1, cols)) % tq)

    def compress_and_select(n_c, n_b):
        s = jnp.dot(kc_ref[0, :n_c], qt, preferred_element_type=F32)
        cmp_end = _row_iota((n_c, 1)) * CMP_STRIDE + (CMP_LEN - 1)
        e, inv_l = _masked_softmax_t(s, cmp_end <= tq_row)
        o_cmp = jnp.dot(vct_ref[0, :, :n_c], e.astype(BF16), preferred_element_type=F32) * inv_l

        psum = e[:, 0:tq] * inv_l[:, 0:tq]
        for h in range(1, NSA_HPG):
            psum = psum + e[:, h * tq:(h + 1) * tq] * inv_l[:, h * tq:(h + 1) * tq]
        p_hi = psum.astype(BF16)
        p_lo = (psum - p_hi.astype(F32)).astype(BF16)
        ovt = ovt_ref[:n_b, :n_c]
        imp = (jnp.dot(ovt, p_hi, preferred_element_type=F32)
               + jnp.dot(ovt, p_lo, preferred_element_type=F32))

        jj = _row_iota((n_b, tq))
        tq_blk = qs + _lane_iota((n_b, tq))
        cur = tq_blk // SLC_BLOCK
        forced = (jj == 0) | (jj == cur) | (jj == cur - 1)
        causal_blk = jj * SLC_BLOCK <= tq_blk
        val = jnp.where(forced, imp + BIG, imp)
        val = jnp.where(causal_blk, val, NEG)
        jjf = jj.astype(F32)

        def pick(_, carry):
            val, sel = carry
            mx = jnp.max(val, axis=0, keepdims=True)
            idx = jnp.min(jnp.where(val == mx, jjf, float(n_b)), axis=0, keepdims=True)
            hit = jjf == idx
            return jnp.where(hit, -jnp.inf, val), jnp.where(hit, 1.0, sel)

        _, sel = lax.fori_loop(0, min(n_sel, n_b), pick, (val, jnp.zeros((n_b, tq), F32)))
        bias = jnp.where((sel > 0.0) & causal_blk, 0.0, NEG)
        if n_b < nslc:
            bias = jnp.concatenate([bias, jnp.full((nslc - n_b, tq), NEG, F32)], axis=0)
        return o_cmp, bias

    parts = max(1, min(NSA_PREFIX_PARTS, nc // LANE))
    seq = nc * CMP_STRIDE

    def dispatch(k):
        full = lambda: compress_and_select(nc * k // parts, nslc * k // parts)
        if k == parts:
            return full()
        return lax.cond(qs + tq <= seq * k // parts, full, lambda: dispatch(k + 1))

    o_cmp, bias_t = dispatch(1)

    q_rows = qt[:HEAD_DIM].astype(F32)
    spare = jnp.zeros((LANE - HEAD_DIM - SUPER_BLOCKS, cols), F32)
    for st in range(n_super):
        b = bias_t[st * SUPER_BLOCKS:(st + 1) * SUPER_BLOCKS]
        b = jnp.concatenate([b] * NSA_HPG, axis=1)
        qaug_ref[st] = jnp.concatenate([q_rows, b, spare], axis=0).astype(BF16)

    tk = KV_TILE
    per_super = SUPER_BLOCKS * SLC_BLOCK // tk
    j_last = (qs + tq - 1) // tk
    n_strips = cols // Q_STRIP
    strips = [slice(c * Q_STRIP, (c + 1) * Q_STRIP) for c in range(n_strips)]

    def causal_masks(j):
        kpos = j * tk + _row_iota((tk, 1))
        return [kpos <= tq_row[:, sl] for sl in strips]

    def slc_steps(j, masks=None):
        k0 = pl.multiple_of(j * tk, tk)
        k_tile, vt, st = ks_ref[pl.ds(k0, tk), :], vst_ref[0, j], j // per_super
        return [(k_tile, qaug_ref[st, :, strips[c]], vt, c, None if masks is None else masks[c])
                for c in range(n_strips)]

    group = 4

    def grouped(jj, flat):
        steps = [st for t in range(group) for st in slc_steps(group * jj + t)]
        return _flat(_fast_steps(steps, _nest(flat), acc_ref))

    def single(j, flat):
        return _flat(_fast_steps(slc_steps(j), _nest(flat), acc_ref))

    acc_ref[...] = jnp.zeros(acc_ref.shape, F32)
    head_rows = 16
    head_masks = [_row_iota((head_rows, 1)) <= tq_row[:, sl] for sl in strips]
    m0 = _first_tile_max(ks_ref[0:head_rows, :], [qaug_ref[0, :, sl] for sl in strips], head_masks)
    n_groups = j_last // group
    flat = lax.fori_loop(0, n_groups, grouped, _flat(tuple((m, jnp.zeros_like(m)) for m in m0)))
    flat = lax.fori_loop(group * n_groups, j_last, single, flat)
    state = _fast_steps(slc_steps(j_last, causal_masks(j_last)), _nest(flat), acc_ref)
    worst = jnp.max(jnp.concatenate([w for _, w in state], axis=1))

    @pl.when(worst > EXP_GUARD)
    def _():
        acc_ref[...] = jnp.zeros(acc_ref.shape, F32)
        lax.fori_loop(0, j_last + 1, lambda j, ms: _online_steps(slc_steps(j, causal_masks(j)), ms, acc_ref),
                      _m_init(n_strips))

    o_slc = acc_ref[:HEAD_DIM] / jnp.maximum(acc_ref[SUM_ROW:SUM_ROW + 1], 1e-30)

    wt = WIN_TILE
    n_wt = (WINDOW + tq) // wt
    ws = pl.multiple_of(jnp.maximum(qs - WINDOW, 0), wt)

    def win_masks(k0, rows):
        dist = tq_row - (k0 + _row_iota((rows, 1)))
        in_win = (dist >= 0) & (dist < WINDOW)
        return [in_win[:, sl] for sl in strips]

    def win_steps(t):
        k0 = pl.multiple_of(ws + t * wt, wt)
        k_tile, vt, masks = kw_ref[pl.ds(k0, wt), :], vwt_ref[0, ws // wt + t], win_masks(k0, wt)
        return [(k_tile, qt[:, strips[c]], vt, c, masks[c]) for c in range(n_strips)]

    k_start = pl.multiple_of(jnp.maximum(qs - tq, 0), tq)
    m0 = _first_tile_max(kw_ref[pl.ds(k_start, head_rows), :], [qt[:, sl] for sl in strips],
                         win_masks(k_start, head_rows))
    wacc_ref[...] = jnp.zeros(wacc_ref.shape, F32)
    state = _fast_steps([st for t in range(n_wt) for st in win_steps(t)],
                        tuple((m, jnp.zeros_like(m)) for m in m0), wacc_ref)
    worst = jnp.max(jnp.concatenate([w for _, w in state], axis=1))

    @pl.when(worst > EXP_GUARD)
    def _():
        wlen = n_wt * wt
        s = jnp.dot(kw_ref[pl.ds(ws, wlen), :], qt, preferred_element_type=F32)
        dist = tq_row - (ws + _row_iota((wlen, 1)))
        e, _ = _masked_softmax_t(s, (dist >= 0) & (dist < WINDOW))
        e = e.astype(BF16)
        acc = jnp.zeros(wacc_ref.shape, F32)
        for t in range(n_wt):
            acc = acc + jnp.dot(vwt_ref[0, ws // wt + t], e[t * wt:(t + 1) * wt], preferred_element_type=F32)
        wacc_ref[...] = acc

    o_win = wacc_ref[:HEAD_DIM] / jnp.maximum(wacc_ref[SUM_ROW:SUM_ROW + 1], 1e-30)

    gate = gate_ref[0]
    for h in range(NSA_HPG):
        sl = slice(h * tq, (h + 1) * tq)
        o = (gate[3 * h:3 * h + 1] * o_cmp[:HEAD_DIM, sl] + gate[3 * h + 1:3 * h + 2] * o_slc[:, sl]
             + gate[3 * h + 2:3 * h + 3] * o_win[:, sl])
        o_ref[:, h * LANE:(h + 1) * LANE] = _pad_feature_rows(o).T.astype(BF16)


def _overlap_t(s, nslc_pad):
    nc = s // CMP_STRIDE
    cmp_start = np.arange(nc) * CMP_STRIDE
    slc_start = np.arange(nslc_pad) * SLC_BLOCK
    ov = np.clip(np.minimum(cmp_start[:, None] + CMP_LEN, slc_start[None, :] + SLC_BLOCK)
                 - np.maximum(cmp_start[:, None], slc_start[None, :]), 0, None) / CMP_STRIDE
    ov[nc - CMP_LEN // CMP_STRIDE + 1:, :] = 0.0
    ov[:, s // SLC_BLOCK:] = 0.0
    return jnp.asarray(ov.T, BF16)


def _nsa_attention(qnt, kc, vct, ks, vst, kw, vwt, gates):
    s = qnt.shape[1]
    nc = s // CMP_STRIDE
    n_slc = s // SLC_BLOCK
    nslc_pad = -(-n_slc // LANE) * LANE
    tq = Q_TILE_NSA
    cols = NSA_HPG * tq
    once = pl.Buffered(1)
    res = pl.BlockSpec((s, LANE), lambda g, i: (0, g), pipeline_mode=once)
    return pl.pallas_call(
        functools.partial(_nsa_kernel, n_sel=min(SLC_TOPK, n_slc)),
        grid=(NSA_GROUPS, s // tq),
        in_specs=[pl.BlockSpec((NSA_HPG * LANE, tq), lambda g, i: (g, i)),
                  pl.BlockSpec((1, nc, LANE), lambda g, i: (g, 0, 0), pipeline_mode=once),
                  pl.BlockSpec((1, LANE, nc), lambda g, i: (g, 0, 0), pipeline_mode=once),
                  res, pl.BlockSpec((1, s // KV_TILE, V_ROWS, KV_TILE), lambda g, i: (g, 0, 0, 0),
                                    pipeline_mode=once),
                  res, pl.BlockSpec((1, s // WIN_TILE, V_ROWS, WIN_TILE), lambda g, i: (g, 0, 0, 0),
                                    pipeline_mode=once),
                  pl.BlockSpec((1, GATE_ROWS, tq), lambda g, i: (g, 0, i)),
                  pl.BlockSpec((nslc_pad, nc), lambda g, i: (0, 0), pipeline_mode=once)],
        out_specs=pl.BlockSpec((tq, NSA_HPG * LANE), lambda g, i: (i, g)),
        out_shape=jax.ShapeDtypeStruct((s, NSA_HEADS * LANE), BF16),
        scratch_shapes=[pltpu.VMEM((nslc_pad // SUPER_BLOCKS, LANE, cols), BF16),
                        pltpu.VMEM((V_ROWS, cols), F32), pltpu.VMEM((V_ROWS, cols), F32)],
        compiler_params=_cparams("parallel", "arbitrary"),
        name="nsa_attention",
    )(qnt, kc, vct, ks, vst, kw, vwt, gates, _overlap_t(s, nslc_pad))


def _flash_kernel(cfirst_ref, clast_ref, slack_ref, qt_ref, k_ref, vt_ref, o_ref, acc_ref, *, tq, decay):
    h = pl.program_id(0)
    i = pl.program_id(1)
    tk = KV_TILE
    n_tiles = k_ref.shape[0] // tk
    acc_ref[...] = jnp.zeros(acc_ref.shape, F32)

    n_strips = tq // Q_STRIP
    per_q = tq // tk
    qas = [qt_ref[:, c * Q_STRIP:(c + 1) * Q_STRIP] for c in range(n_strips)]

    def tile_steps(j, d=None):
        k0 = pl.multiple_of(j * tk, tk)
        k_tile, vt = k_ref[pl.ds(k0, tk), :], vt_ref[0, j]
        steps = []
        for c in range(n_strips):
            mask = None
            if d is not None:
                if d * tk > (c + 1) * Q_STRIP - 1:
                    continue
                if (d + 1) * tk - 1 > c * Q_STRIP:
                    shp = (tk, Q_STRIP)
                    mask = _row_iota(shp) + d * tk <= _lane_iota(shp) + c * Q_STRIP
            steps.append((k_tile, qas[c], vt, c, mask))
        return steps

    def any_tile_masks(j):
        shp = (tk, Q_STRIP)
        return [_row_iota(shp) + j * tk <= _lane_iota(shp) + (i * tq + c * Q_STRIP) for c in range(n_strips)]

    def below(t, flat):
        jj = i - 1 - t
        bound = (slack_ref[0] + cfirst_ref[h * n_tiles + i * per_q]
                 - clast_ref[h * n_tiles + jj * per_q + per_q - 1])

        def run(flat):
            steps = [st for u in range(per_q) for st in tile_steps(jj * per_q + (per_q - 1 - u))]
            return _flat(_fast_steps(steps, _nest(flat), acc_ref))

        return lax.cond(bound >= -SKIP_MARGIN, run, lambda flat: flat, flat)

    diag0 = i * per_q
    own = [(c * Q_STRIP) // tk for c in range(n_strips)]
    head = tk if decay else 16

    def start_max(c):
        j = diag0 + own[c]
        keys = k_ref[pl.ds(pl.multiple_of(j * tk, tk), head), :]
        shp = (head, Q_STRIP)
        mask = _row_iota(shp) + j * tk <= _lane_iota(shp) + (i * tq + c * Q_STRIP)
        return _first_tile_max(keys, [qas[c]], [mask])[0]

    m0 = tuple(start_max(c) for c in range(n_strips))
    steps = [st for d in reversed(range(per_q)) for st in tile_steps(diag0 + d, d)]
    state = _fast_steps(steps, tuple((m, jnp.zeros_like(m)) for m in m0), acc_ref)
    state = _nest(lax.fori_loop(0, i, below, _flat(state)))
    worst = jnp.max(jnp.concatenate([w for _, w in state], axis=1))

    @pl.when(worst > EXP_GUARD)
    def _():
        acc_ref[...] = jnp.zeros(acc_ref.shape, F32)

        def exact(j, ms):
            k0 = pl.multiple_of(j * tk, tk)
            k_tile, vt, masks = k_ref[pl.ds(k0, tk), :], vt_ref[0, j], any_tile_masks(j)
            return _online_steps([(k_tile, qas[c], vt, c, masks[c]) for c in range(n_strips)], ms, acc_ref)

        lax.fori_loop(0, (i + 1) * per_q, exact, _m_init(n_strips))

    o = _pad_feature_rows(acc_ref[:HEAD_DIM] / acc_ref[SUM_ROW:SUM_ROW + 1])
    for c0 in range(0, tq, LANE):
        o_ref[c0:c0 + LANE, :] = o[:, c0:c0 + LANE].T.astype(BF16)


def _causal_attention(qt, k, vt, tq, bias_edges=None, slack=None):
    s, width = k.shape
    heads = width // LANE
    tq = min(tq, s)
    assert tq % KV_TILE == 0 and s % tq == 0
    n_tiles = s // KV_TILE
    if bias_edges is None:
        first = last = jnp.zeros((heads * n_tiles,), F32)
        slack = jnp.full((1,), -NEG, F32)
    else:
        first, last = (e.reshape(heads * n_tiles).astype(F32) for e in bias_edges)
    grid_spec = pltpu.PrefetchScalarGridSpec(
        num_scalar_prefetch=3,
        grid=(heads, s // tq),
        in_specs=[pl.BlockSpec((LANE, tq), lambda h, i, *_: (h, i)),
                  pl.BlockSpec((s, LANE), lambda h, i, *_: (0, h)),
                  pl.BlockSpec((1, n_tiles, V_ROWS, KV_TILE), lambda h, i, *_: (h, 0, 0, 0))],
        out_specs=pl.BlockSpec((tq, LANE), lambda h, i, *_: (i, h)),
        scratch_shapes=[pltpu.VMEM((V_ROWS, tq), F32)],
    )
    return pl.pallas_call(
        functools.partial(_flash_kernel, tq=tq, decay=bias_edges is not None),
        grid_spec=grid_spec,
        out_shape=jax.ShapeDtypeStruct((s, width), BF16),
        compiler_params=_cparams("parallel", "arbitrary"),
        name="causal_attention",
    )(first, last, slack.astype(F32), qt, k, vt)


def _out_proj_kernel(oa_ref, ob_ref, wa_ref, wb_ref, x_ref, g_ref, o_ref):
    y = jnp.dot(oa_ref[...], wa_ref[...], preferred_element_type=F32)
    y = y + jnp.dot(ob_ref[...], wb_ref[...], preferred_element_type=F32)
    o_ref[...] = x_ref[...] + g_ref[...] * y


def _out_proj(oa, ob, cola, colb, wa, wb, x2, gate):
    s, d = x2.shape
    ka = wa.shape[0]
    tm = ROW_TILE
    return pl.pallas_call(
        _out_proj_kernel,
        grid=(s // tm,),
        in_specs=[pl.BlockSpec((tm, ka), lambda i: (i, cola)), pl.BlockSpec((tm, ka), lambda i: (i, colb)),
                  pl.BlockSpec((ka, d), lambda i: (0, 0)), pl.BlockSpec((ka, d), lambda i: (0, 0)),
                  pl.BlockSpec((tm, d), lambda i: (i, 0)), pl.BlockSpec((1, d), lambda i: (0, 0))],
        out_specs=pl.BlockSpec((tm, d), lambda i: (i, 0)),
        out_shape=jax.ShapeDtypeStruct((s, d), F32),
        compiler_params=_cparams("parallel"),
        name="out_proj",
    )(oa, ob, wa, wb, x2, gate)


def _mla_prep_kernel(x_ref, ng_ref, sc_ref, sh_ref, win_ref, pos_ref, inv_ref, gqa_ref, gkva_ref, wuq_ref, wuk_ref,
                     wuv_ref, gq_ref, gk_ref, gkr_ref, qt_ref, k_ref, vt_ref):
    shp = (PREP_TILE, LANE)
    act = _norm_mod(x_ref[...], ng_ref[...], sc_ref[...], sh_ref[...]).astype(BF16)
    proj = jnp.dot(act, win_ref[...], preferred_element_type=F32)
    lane = _lane_iota(shp)
    nope = lane < QK_NOPE
    rope = (lane >= QK_NOPE) & (lane < QK_NOPE + QK_ROPE)
    ref_ones = jnp.where((lane >= REF_ROW) & (lane < REF_ROW + 3), 1.0, 0.0)
    ang = pos_ref[...] * inv_ref[...]
    cos = jnp.where(rope, jnp.cos(ang), 1.0)
    sin = jnp.where(rope, jnp.sin(ang), 0.0)

    def rope32(x):
        half = QK_ROPE // 2
        rot = jnp.where(lane < QK_NOPE + half, -pltpu.roll(x, LANE - half, 1), pltpu.roll(x, half, 1))
        return x * cos + rot * sin

    def low_rank_norm(x, g):
        ms = jnp.mean(x * x, axis=-1, keepdims=True)
        return (x * lax.rsqrt(ms + EPS) * g).astype(BF16)

    nq = Q_LORA // LANE
    cq = low_rank_norm(proj[:, :Q_LORA], gqa_ref[...])
    ckv = low_rank_norm(proj[:, Q_LORA:Q_LORA + KV_LORA], gkva_ref[...])
    kr = proj[:, (nq + KV_LORA // LANE) * LANE:(nq + KV_LORA // LANE + 1) * LANE]
    k_rope = rope32(_head_rms(kr, gkr_ref[...], QK_ROPE))

    gq, gk = gq_ref[...], gk_ref[...]
    scale = (QK_NOPE + QK_ROPE) ** -0.5 * LOG2E
    pair = 2 * LANE
    for hp in range(MLA_HEADS // 2):
        cols = slice(hp * pair, (hp + 1) * pair)
        q2 = jnp.dot(cq, wuq_ref[:, cols], preferred_element_type=F32)
        k2 = jnp.dot(ckv, wuk_ref[:, cols], preferred_element_type=F32)
        v2 = jnp.dot(ckv, wuv_ref[:, cols], preferred_element_type=F32)
        for sub in range(2):
            head = 2 * hp + sub
            sl = slice(head * LANE, (head + 1) * LANE)
            half = slice(sub * LANE, (sub + 1) * LANE)
            x = q2[:, half]
            ss_n = jnp.sum(jnp.where(nope, x * x, 0.0), axis=-1, keepdims=True)
            ss_r = jnp.sum(jnp.where(rope, x * x, 0.0), axis=-1, keepdims=True)
            inv_rms = jnp.where(nope, lax.rsqrt(ss_n * (1.0 / QK_NOPE) + EPS),
                                lax.rsqrt(ss_r * (1.0 / QK_ROPE) + EPS))
            qt_ref[sl, :] = _t_bf16(rope32(x * inv_rms * gq) * scale)
            kn = _head_rms(k2[:, half], gk, QK_NOPE)
            k_ref[:, sl] = (kn + k_rope + ref_ones).astype(BF16)
            vt_ref[head, 0] = _t_bf16(jnp.where(lane == SUM_ROW, 1.0, v2[:, half]), V_ROWS)


def _mla_prep(x2, norm_g, sc, sh, w_in, posf, inv128, gqa, gkva, wuq, wuk, wuv, gq, gk, gkr):
    s, d = x2.shape
    tm = PREP_TILE
    assert tm == KV_TILE

    def full(a):
        return pl.BlockSpec(a.shape, lambda i: (0, 0))

    outs = [_feat_major(MLA_HEADS, s, tm),
            (pl.BlockSpec((tm, MLA_HEADS * LANE), lambda i: (i, 0)),
             jax.ShapeDtypeStruct((s, MLA_HEADS * LANE), BF16)),
            _value_tiles(MLA_HEADS, s, tm, KV_TILE, V_ROWS)]
    head = (norm_g, sc, sh, w_in)
    tail = (inv128, gqa, gkva, wuq, wuk, wuv, gq, gk, gkr)
    return pl.pallas_call(
        _mla_prep_kernel,
        grid=(s // tm,),
        in_specs=[pl.BlockSpec((tm, d), lambda i: (i, 0))] + [full(a) for a in head]
                 + [pl.BlockSpec((tm, 1), lambda i: (i, 0))] + [full(a) for a in tail],
        out_specs=[o[0] for o in outs],
        out_shape=[o[1] for o in outs],
        compiler_params=_cparams("parallel"),
        name="mla_prep",
    )(x2, *head, posf, *tail)


def _rank_lt(v, k):
    n = v.shape[0]
    row = _row_iota(v.shape)
    rank = jnp.zeros(v.shape, F32)
    for b in range(n):
        vb = v[b:b + 1, :]
        rank = rank + jnp.where((vb > v) | ((vb == v) & (row > b)), 1.0, 0.0)
    return rank < k


def _top_rows(v, k):
    rowf = _row_iota(v.shape).astype(F32)
    chosen = jnp.zeros(v.shape, F32)
    for _ in range(k):
        mx = jnp.max(v, axis=0, keepdims=True)
        idx = jnp.min(jnp.where(v == mx, rowf, float(v.shape[0])), axis=0, keepdims=True)
        hit = rowf == idx
        chosen = jnp.where(hit, 1.0, chosen)
        v = jnp.where(hit, -jnp.inf, v)
    return chosen > 0.0


def _moe_route_kernel(x_ref, g_ref, sc_ref, sh_ref, wr_ref, rb_ref, h_ref, pos_ref, wt_ref, cnt_ref):
    tm = ROW_TILE
    h = _norm_mod(x_ref[...], g_ref[...], sc_ref[...], sh_ref[...])
    h_ref[...] = h.astype(BF16)
    logits = jnp.dot(h, wr_ref[...], precision=HIGHEST, preferred_element_type=F32)
    lt = logits.T[:N_EXPERTS]
    scores = jax.nn.sigmoid(lt)
    sel = scores + rb_ref[...]

    per = N_EXPERTS // N_GROUPS
    grp = sel.reshape(N_GROUPS, per, tm)
    sub = lax.broadcasted_iota(jnp.int32, grp.shape, 1)
    m1 = jnp.max(grp, axis=1, keepdims=True)
    first = jnp.min(jnp.where(grp == m1, sub, per), axis=1, keepdims=True)
    m2 = jnp.max(jnp.where(sub == first, -jnp.inf, grp), axis=1, keepdims=True)
    gscore = (m1 + m2).reshape(N_GROUPS, tm)
    gmask = _rank_lt(gscore, TOPK_GROUPS)
    emask = jnp.broadcast_to(gmask.reshape(N_GROUPS, 1, tm), grp.shape).reshape(N_EXPERTS, tm)
    chosen = _top_rows(jnp.where(emask, sel, NEG), TOP_K)

    w = jnp.where(chosen, scores, 0.0)
    wt_ref[...] = w / jnp.sum(w, axis=0, keepdims=True) * ROUTED_SCALE

    upper = jnp.where(_row_iota((tm, tm)) <= _lane_iota((tm, tm)), 1.0, 0.0).astype(BF16)
    incl = jnp.dot(jnp.where(chosen, 1.0, 0.0).astype(BF16), upper, preferred_element_type=F32)
    pos_ref[...] = jnp.where(chosen, incl - 1.0, -1.0)
    cnt_ref[0] = jnp.broadcast_to(incl[:, tm - 1:tm], (N_EXPERTS, LANE))


def _moe_route(x2, g, sc, sh, w_router_pad, router_bias_col):
    s, d = x2.shape
    tm = ROW_TILE
    vec = pl.BlockSpec((1, d), lambda i: (0, 0))
    et = pl.BlockSpec((N_EXPERTS, tm), lambda i: (0, i))
    return pl.pallas_call(
        _moe_route_kernel,
        grid=(s // tm,),
        in_specs=[pl.BlockSpec((tm, d), lambda i: (i, 0)), vec, vec, vec,
                  pl.BlockSpec((d, LANE), lambda i: (0, 0)),
                  pl.BlockSpec((N_EXPERTS, 1), lambda i: (0, 0))],
        out_specs=[pl.BlockSpec((tm, d), lambda i: (i, 0)), et, et,
                   pl.BlockSpec((1, N_EXPERTS, LANE), lambda i: (i, 0, 0))],
        out_shape=[jax.ShapeDtypeStruct((s, d), BF16), jax.ShapeDtypeStruct((N_EXPERTS, s), F32),
                   jax.ShapeDtypeStruct((N_EXPERTS, s), F32),
                   jax.ShapeDtypeStruct((s // tm, N_EXPERTS, LANE), F32)],
        compiler_params=_cparams("parallel"),
        name="moe_route",
    )(x2, g, sc, sh, w_router_pad, router_bias_col)


def _moe_kernel(cnt_ref, x_ref, h_ref, pos_ref, wt_ref, wg_ref, wu_ref, wd_ref, sg_ref, su_ref, sd_ref,
                g2_ref, o_ref, acc_ref):
    i = pl.program_id(0)
    e = pl.program_id(1)
    tm = ROW_TILE
    r = MOE_CHUNK

    @pl.when(e == 0)
    def _():
        h = h_ref[...]
        a = jnp.dot(h, sg_ref[...], preferred_element_type=F32)
        a = a * jax.nn.sigmoid(a) * jnp.dot(h, su_ref[...], preferred_element_type=F32)
        acc_ref[...] = jnp.dot(a.astype(BF16), sd_ref[...], preferred_element_type=F32)

    first = e * MOE_EXPERTS_PER_STEP
    n = cnt_ref[i * N_EXPERTS + first]
    for k in range(1, MOE_EXPERTS_PER_STEP):
        n = jnp.maximum(n, cnt_ref[i * N_EXPERTS + first + k])
    prows = [pos_ref[pl.ds(first + k, 1), :] for k in range(MOE_EXPERTS_PER_STEP)]
    wrows = [wt_ref[pl.ds(first + k, 1), :] for k in range(MOE_EXPERTS_PER_STEP)]

    def chunk(c, _):
        slot = (_row_iota((r, tm)) + c * r).astype(F32)
        hits = [prow == slot for prow in prows]
        onehot = jnp.concatenate([jnp.where(hit, 1.0, 0.0).astype(BF16) for hit in hits], axis=0)
        xg = jnp.dot(onehot, h_ref[...], preferred_element_type=F32).astype(BF16)
        def gate_up(k):
            xk = xg[k * r:(k + 1) * r]
            return (jnp.dot(xk, wg_ref[k], preferred_element_type=F32),
                    jnp.dot(xk, wu_ref[k], preferred_element_type=F32))

        ahead = [gate_up(k) for k in range(MOE_MLP_LOOKAHEAD)]
        ys = []
        for k in range(MOE_EXPERTS_PER_STEP):
            g, u = ahead.pop(0)
            if k + MOE_MLP_LOOKAHEAD < MOE_EXPERTS_PER_STEP:
                ahead.append(gate_up(k + MOE_MLP_LOOKAHEAD))
            a = g * jax.nn.sigmoid(g) * u
            y = jnp.dot(a.astype(BF16), wd_ref[k], preferred_element_type=F32)
            wr = jnp.sum(jnp.where(hits[k], wrows[k], 0.0), axis=-1, keepdims=True)
            ys.append((y * wr).astype(BF16))
        acc_ref[...] += _dot_tn(onehot, jnp.concatenate(ys, axis=0))
        return 0

    lax.fori_loop(0, (n + r - 1) // r, chunk, 0)

    @pl.when(e == N_EXPERTS // MOE_EXPERTS_PER_STEP - 1)
    def _():
        o_ref[...] = x_ref[...] + g2_ref[...] * acc_ref[...]


def _moe_experts(counts, x2, h, pos_t, w_t, wg, wu, wd, sg, su, sd, g2):
    s, d = x2.shape
    tm = ROW_TILE
    ff = wg.shape[2]
    tile = pl.BlockSpec((tm, d), lambda i, e, c: (i, 0))
    et = pl.BlockSpec((N_EXPERTS, tm), lambda i, e, c: (0, i))

    def const(a):
        return pl.BlockSpec(a.shape, lambda i, e, c: (0,) * a.ndim)

    per = MOE_EXPERTS_PER_STEP
    grid_spec = pltpu.PrefetchScalarGridSpec(
        num_scalar_prefetch=1,
        grid=(s // tm, N_EXPERTS // per),
        in_specs=[tile, tile, et, et,
                  pl.BlockSpec((per, d, ff), lambda i, e, c: (e, 0, 0)),
                  pl.BlockSpec((per, d, ff), lambda i, e, c: (e, 0, 0)),
                  pl.BlockSpec((per, ff, d), lambda i, e, c: (e, 0, 0)),
                  const(sg), const(su), const(sd), const(g2)],
        out_specs=tile,
        scratch_shapes=[pltpu.VMEM((tm, d), F32)],
    )
    return pl.pallas_call(
        _moe_kernel,
        grid_spec=grid_spec,
        out_shape=jax.ShapeDtypeStruct((s, d), F32),
        compiler_params=_cparams("parallel", "arbitrary"),
        name="moe_experts",
    )(counts, x2, h, pos_t, w_t, wg, wu, wd, sg, su, sd, g2)


def _pad_lanes(v, width=LANE, offset=0):
    out = jnp.zeros((1, width), F32)
    return out.at[0, offset:offset + v.shape[0]].set(v.astype(F32))


def _head_cols(w, n_heads, dim):
    d = w.shape[0]
    w3 = w.reshape(d, n_heads, dim)
    return jnp.pad(w3, ((0, 0), (0, 0), (0, LANE - dim))).reshape(d, n_heads * LANE)


def _hybrid_w_in(w_in):
    d = w_in.shape[0]
    nq = NSA_HEADS * HEAD_DIM
    nkv = 6 * NSA_GROUPS * HEAD_DIM
    ng = 3 * NSA_HEADS
    nf = 3 * FOX_HEADS * HEAD_DIM
    c0, c1, c2, c3 = nq, nq + nkv, nq + nkv + ng, nq + nkv + ng + nf
    gates = w_in[:, c1:c2].reshape(d, NSA_GROUPS, 3 * NSA_HPG)
    gates = jnp.pad(gates, ((0, 0), (0, 0), (0, LANE - 3 * NSA_HPG))).reshape(d, NSA_GROUPS * LANE)
    ff = jnp.pad(w_in[:, c3:], ((0, 0), (0, 2 * LANE - FOX_HEADS)))
    return jnp.concatenate([
        _head_cols(w_in[:, :c0], NSA_HEADS, HEAD_DIM),
        _head_cols(w_in[:, c0:c1], 6 * NSA_GROUPS, HEAD_DIM),
        _head_cols(w_in[:, c2:c3], 3 * FOX_HEADS, HEAD_DIM),
        gates, ff], axis=1).astype(BF16)


def _pad_head_rows(w, n_heads, dim):
    d = w.shape[1]
    w3 = w.reshape(n_heads, dim, d)
    return jnp.pad(w3, ((0, 0), (0, LANE - dim), (0, 0))).reshape(n_heads * LANE, d).astype(BF16)


def _rope_inv(dim, offset):
    inv = ROPE_THETA ** (-jnp.arange(0, dim, 2, dtype=F32) / dim)
    return _pad_lanes(jnp.concatenate([inv, inv]), offset=offset)


def _hybrid_mixer(x2, posf, mods, norm_g, w_in, fox_f_bias, nsa_q_norm, nsa_k_norm, nsa_cmp_pe, nsa_w_cmp,
                  fox_q_norm, fox_k_norm, w_out):
    sh1, sc1, g1 = mods
    (qnt, kct, vct, ks, vst, kw, vwt, gates, fqt, fk, fvt, cedge) = _hy_prep(
        x2, norm_g, sc1, sh1, _hybrid_w_in(w_in), posf, _rope_inv(HEAD_DIM, 0), _pad_lanes(nsa_q_norm),
        _pad_lanes(nsa_k_norm),
        _pad_lanes(fox_q_norm), _pad_lanes(fox_k_norm), _pad_lanes(fox_f_bias))
    kc, vc_t = _compress(kct, vct, nsa_w_cmp, nsa_cmp_pe, nsa_k_norm)
    o_a = _nsa_attention(qnt, kc, vc_t, ks, vst, kw, vwt, gates)
    slack = (2.0 * HEAD_DIM ** 0.5 * LOG2E) * jnp.max(jnp.abs(fox_q_norm)) * jnp.max(jnp.abs(fox_k_norm))
    edges = (cedge[:, 0, :FOX_HEADS].T, cedge[:, 1, :FOX_HEADS].T)
    o_b = _causal_attention(fqt, fk, fvt, Q_TILE_FOX, edges, slack.reshape(1))
    half = NSA_HEADS * HEAD_DIM
    wa = _pad_head_rows(w_out[:half], NSA_HEADS, HEAD_DIM)
    wb = _pad_head_rows(w_out[half:], FOX_HEADS, HEAD_DIM)
    return _out_proj(o_a, o_b, 0, 0, wa, wb, x2, g1)


def _mla_mixer(x2, posf, mods, norm_g, w_in, q_a_norm, kv_a_norm, w_uq, w_ukv, qn_norm, kn_norm, qr_norm,
               kr_norm, w_out):
    sh1, sc1, g1 = mods
    d = x2.shape[1]
    w_kr = jnp.zeros((d, LANE), F32).at[:, QK_NOPE:QK_NOPE + QK_ROPE].set(w_in[:, Q_LORA + KV_LORA:])
    w_in_p = jnp.concatenate([w_in[:, :Q_LORA + KV_LORA], w_kr], axis=1).astype(BF16)
    hq = QK_NOPE + QK_ROPE
    wuq = _head_cols(w_uq, MLA_HEADS, hq).astype(BF16)
    wkv3 = w_ukv.reshape(KV_LORA, MLA_HEADS, QK_NOPE + V_HEAD)
    wuk = _head_cols(wkv3[:, :, :QK_NOPE].reshape(KV_LORA, -1), MLA_HEADS, QK_NOPE).astype(BF16)
    wuv = _head_cols(wkv3[:, :, QK_NOPE:].reshape(KV_LORA, -1), MLA_HEADS, V_HEAD).astype(BF16)
    gq = _pad_lanes(jnp.concatenate([qn_norm, qr_norm]))
    qt, k, vt = _mla_prep(x2, norm_g, sc1, sh1, w_in_p, posf, _rope_inv(QK_ROPE, QK_NOPE),
                          q_a_norm.reshape(1, -1).astype(F32),
                          kv_a_norm.reshape(1, -1).astype(F32), wuq, wuk, wuv, gq, _pad_lanes(kn_norm),
                          _pad_lanes(kr_norm, offset=QK_NOPE))
    o = _causal_attention(qt, k, vt, Q_TILE_MLA)
    w_pad = _pad_head_rows(w_out, MLA_HEADS, V_HEAD)
    half = w_pad.shape[0] // 2
    return _out_proj(o, o, 0, 1, w_pad[:half], w_pad[half:], x2, g1)


def _moe_ffn(x2, mods, norm_g, w_router, router_bias, w_gate, w_up, w_down, ws_gate, ws_up, ws_down):
    sh2, sc2, g2 = mods
    w_r = jnp.pad(w_router.astype(F32), ((0, 0), (0, LANE - N_EXPERTS)))
    h, pos_t, w_t, cnt = _moe_route(x2, norm_g, sc2, sh2, w_r, router_bias.reshape(N_EXPERTS, 1).astype(F32))
    counts = cnt[:, :, 0].astype(jnp.int32).reshape(-1)
    return _moe_experts(counts, x2, h, pos_t, w_t, w_gate.astype(BF16), w_up.astype(BF16),
                        w_down.astype(BF16), ws_gate.astype(BF16), ws_up.astype(BF16), ws_down.astype(BF16), g2)


def kernel(x, c, positions, norm_attn, norm_ffn, w_ada, b_ada, hy_w_in, fox_f_bias, nsa_q_norm, nsa_k_norm, nsa_cmp_pe, nsa_w_cmp, fox_q_norm, fox_k_norm, hy_w_out, mla_w_in, mla_q_a_norm, mla_kv_a_norm, mla_w_uq, mla_w_ukv, mla_qn_norm, mla_kn_norm, mla_qr_norm, mla_kr_norm, mla_w_out, moe_w_router, moe_router_bias, moe_w_gate, moe_w_up, moe_w_down, moe_ws_gate, moe_ws_up, moe_ws_down):
    b, s, d = x.shape
    assert b == 1 and s % KV_TILE == 0 and s >= WINDOW + Q_TILE_NSA
    depth = w_ada.shape[0]
    x2 = x.reshape(s, d).astype(F32)
    posf = positions.reshape(s, 1).astype(F32)
    mod = _ada_mod(c.astype(F32), w_ada.astype(F32), b_ada.astype(F32))

    for layer in range(depth):
        m = [mod[layer, :, k * d:(k + 1) * d] for k in range(6)]
        i = layer // 2
        g_attn = norm_attn[layer].reshape(1, d).astype(F32)
        if layer % 2 == 0:
            x2 = _hybrid_mixer(x2, posf, m[0:3], g_attn, hy_w_in[i], fox_f_bias[i], nsa_q_norm[i],
                               nsa_k_norm[i], nsa_cmp_pe[i], nsa_w_cmp[i], fox_q_norm[i], fox_k_norm[i],
                               hy_w_out[i])
        else:
            x2 = _mla_mixer(x2, posf, m[0:3], g_attn, mla_w_in[i], mla_q_a_norm[i], mla_kv_a_norm[i],
                            mla_w_uq[i], mla_w_ukv[i], mla_qn_norm[i], mla_kn_norm[i], mla_qr_norm[i],
                            mla_kr_norm[i], mla_w_out[i])
        x2 = _moe_ffn(x2, m[3:6], norm_ffn[layer].reshape(1, d).astype(F32), moe_w_router[layer],
                      moe_router_bias[layer], moe_w_gate[layer], moe_w_up[layer], moe_w_down[layer],
                      moe_ws_gate[layer], moe_ws_up[layer], moe_ws_down[layer])
    return x2.reshape(b, s, d)
```
